```python
import jax, jax.numpy as jnp
from jax import lax
import numpy as np

D_MODEL = 1024
BATCH = 2
SEQ = 8192
DEPTH = 1

N_HEADS_NSA = 8
HEAD_DIM = 64
N_KV_GROUPS = 2
HPG = N_HEADS_NSA // N_KV_GROUPS
CMP_BLOCK = 32
CMP_STRIDE = 16
CMP_HIDDEN = 128
SEL_BLOCK = 64
N_SELECT = 16
WINDOW = 512
Q_BLOCK = 128
NSA_W = N_HEADS_NSA * HEAD_DIM
GMLP_WIDTH = 512
GMLP_GROUPS = 4
GMLP_CHUNK = 128
GMLP_GROUP_DIM = GMLP_WIDTH // GMLP_GROUPS
MEM_TOKENS = 256
MEM_HEADS = 4
MEM_HEAD_DIM = 128
MEM_W = MEM_HEADS * MEM_HEAD_DIM
D_FF = -(-8 * D_MODEL // (3 * 256)) * 256
KV_W = 2 * N_KV_GROUPS * HEAD_DIM
GATE_W = 3 * N_HEADS_NSA
IN_W = NSA_W + 3 * KV_W + 2 * GMLP_WIDTH + GATE_W
SPLITS = [NSA_W, NSA_W + KV_W, NSA_W + 2 * KV_W, NSA_W + 3 * KV_W, NSA_W + 3 * KV_W + 2 * GMLP_WIDTH]

EPS = 1e-6
NEG = -1e30
FORCE_SCORE = 1e6

kernel_name = "nsa_gmlp_griffin_gated_hybrid"


def rmsnorm(x, g):
    xf = x.astype(jnp.float32)
    y = xf * lax.rsqrt(jnp.mean(xf * xf, axis=-1, keepdims=True) + EPS)
    return (y * g.astype(jnp.float32)).astype(x.dtype)


def layernorm(x, g):
    xf = x.astype(jnp.float32)
    xc = xf - jnp.mean(xf, axis=-1, keepdims=True)
    y = xc * lax.rsqrt(jnp.mean(xc * xc, axis=-1, keepdims=True) + EPS)
    return (y * g.astype(jnp.float32)).astype(x.dtype)


def masked_softmax(s, mask):
    s = jnp.where(mask, s, NEG)
    m = jnp.max(s, axis=-1, keepdims=True)
    p = jnp.where(mask, jnp.exp(s - m), 0.0)
    return p / jnp.maximum(jnp.sum(p, axis=-1, keepdims=True), 1e-30)


def alibi_slopes(n):
    return jnp.asarray([2.0 ** (-8.0 * (h + 1) / n) for h in range(n)], dtype=jnp.float32)


def compress(x, pe, w1, w2):
    B, T, G, Dh = x.shape
    c = x.reshape(B, T // CMP_STRIDE, CMP_STRIDE, G, Dh)
    blk = jnp.concatenate([c[:, :-1], c[:, 1:]], axis=2) + pe[None, None, :, None, :]
    flat = blk.transpose(0, 1, 3, 2, 4).reshape(B, T // CMP_STRIDE - 1, G, CMP_BLOCK * Dh)
    return jax.nn.gelu(flat @ w1) @ w2


def nsa_attention(q, k_cmp, v_cmp, k_slc, v_slc, k_win, v_win, gates):
    B, T = q.shape[0], q.shape[1]
    G, Dh = N_KV_GROUPS, HEAD_DIM
    n_cmp = k_cmp.shape[1]
    n_slc = T // SEL_BLOCK
    n_sel = min(N_SELECT, n_slc)
    n_qb = T // Q_BLOCK
    scale = Dh ** -0.5
    slopes = alibi_slopes(N_HEADS_NSA).reshape(G, HPG)[None, :, :, None, None]
    cmp_start = jnp.arange(n_cmp) * CMP_STRIDE
    cmp_end = cmp_start + CMP_BLOCK - 1
    sel_start = jnp.arange(n_slc) * SEL_BLOCK
    overlap = ((cmp_start[:, None] < sel_start[None, :] + SEL_BLOCK)
               & (cmp_end[:, None] >= sel_start[None, :])).astype(jnp.float32)
    kb = k_slc.reshape(B, n_slc, SEL_BLOCK, G, Dh).transpose(0, 3, 1, 2, 4)
    vb = v_slc.reshape(B, n_slc, SEL_BLOCK, G, Dh).transpose(0, 3, 1, 2, 4)
    kw = jnp.pad(k_win, ((0, 0), (WINDOW, 0), (0, 0), (0, 0)))
    vw = jnp.pad(v_win, ((0, 0), (WINDOW, 0), (0, 0), (0, 0)))
    qb = q.reshape(B, n_qb, Q_BLOCK, G, HPG, Dh).transpose(1, 0, 2, 3, 4, 5)
    gb = gates.reshape(B, n_qb, Q_BLOCK, G, HPG, 3).transpose(1, 0, 2, 3, 4, 5)
    starts = jnp.arange(n_qb, dtype=jnp.int32) * Q_BLOCK
    b_idx = jnp.arange(B)[:, None, None, None]
    g_idx = jnp.arange(G)[None, :, None, None]
    blk_ids = jnp.arange(n_slc)
    key_off = jnp.arange(SEL_BLOCK)
    win_off = jnp.arange(WINDOW + Q_BLOCK)

    def one_block(args):
        qi, gi, start = args
        t = start + jnp.arange(Q_BLOCK)
        s = jnp.einsum('bqghd,bigd->bghqi', qi, k_cmp).astype(jnp.float32) * scale
        s = s - slopes * (t[:, None] - cmp_end[None, :]).astype(jnp.float32)
        p_cmp = masked_softmax(s, cmp_end[None, :] <= t[:, None])
        o_cmp = jnp.einsum('bghqi,bigd->bqghd', p_cmp.astype(v_cmp.dtype), v_cmp)
        imp = jnp.einsum('bghqi,ij->bgqj', p_cmp, overlap)
        cur = t // SEL_BLOCK
        causal_blk = blk_ids[None, :] <= cur[:, None]
        forced = ((blk_ids[None, :] == 0) | (blk_ids[None, :] == cur[:, None])
                  | (blk_ids[None, :] == cur[:, None] - 1))
        rank = jnp.where(causal_blk, jnp.where(forced, FORCE_SCORE, imp), NEG)
        top_val, top_idx = lax.top_k(rank, n_sel)
        k_sel = kb[b_idx, g_idx, top_idx].reshape(B, G, Q_BLOCK, n_sel * SEL_BLOCK, Dh)
        v_sel = vb[b_idx, g_idx, top_idx].reshape(B, G, Q_BLOCK, n_sel * SEL_BLOCK, Dh)
        pos = top_idx[..., None] * SEL_BLOCK + key_off
        ok = (pos <= t[None, None, :, None, None]) & (top_val > NEG / 2)[..., None]
        pos = pos.reshape(B, G, Q_BLOCK, n_sel * SEL_BLOCK)
        ok = ok.reshape(B, G, Q_BLOCK, n_sel * SEL_BLOCK)
        s = jnp.einsum('bqghd,bgqsd->bghqs', qi, k_sel).astype(jnp.float32) * scale
        s = s - slopes * (t[None, None, :, None] - pos)[:, :, None].astype(jnp.float32)
        p = masked_softmax(s, ok[:, :, None])
        o_slc = jnp.einsum('bghqs,bgqsd->bqghd', p.astype(v_sel.dtype), v_sel)
        k_w = lax.dynamic_slice_in_dim(kw, start, WINDOW + Q_BLOCK, axis=1)
        v_w = lax.dynamic_slice_in_dim(vw, start, WINDOW + Q_BLOCK, axis=1)
        pos_w = start - WINDOW + win_off
        dist = t[:, None] - pos_w[None, :]
        s = jnp.einsum('bqghd,bsgd->bghqs', qi, k_w).astype(jnp.float32) * scale
        s = s - slopes * dist.astype(jnp.float32)
        p = masked_softmax(s, (dist >= 0) & (dist < WINDOW) & (pos_w[None, :] >= 0))
        o_win = jnp.einsum('bghqs,bsgd->bqghd', p.astype(v_w.dtype), v_w)
        return gi[..., 0:1] * o_cmp + gi[..., 1:2] * o_slc + gi[..., 2:3] * o_win

    out = lax.map(one_block, (qb, gb, starts))
    return out.transpose(1, 0, 2, 3, 4, 5).reshape(B, T, NSA_W)


def chunked_sgu(u, v, ln_g, w_s, b_s):
    B, T, _ = v.shape
    v = layernorm(v, ln_g)
    vc = v.reshape(B, T // GMLP_CHUNK, GMLP_CHUNK, GMLP_GROUPS, GMLP_GROUP_DIM)
    w = w_s * jnp.tril(jnp.ones((GMLP_CHUNK, GMLP_CHUNK), w_s.dtype))
    s = jnp.einsum('gts,bcsgd->bctgd', w, vc) + b_s.T[None, None, :, :, None]
    return u * s.reshape(B, T, GMLP_WIDTH)


def setup_inputs(seed: int = 0) -> dict:
    key = jax.random.key(seed)
    ks = iter(jax.random.split(key, 40))
    L = DEPTH

    def w(shape, fan_in):
        return jax.random.normal(next(ks), shape, jnp.float32) * fan_in ** -0.5

    def gain(shape):
        return 1.0 + 0.02 * jax.random.normal(next(ks), shape, jnp.float32)

    def small(shape, s):
        return s * jax.random.normal(next(ks), shape, jnp.float32)

    return {
        "x": jax.random.normal(next(ks), (BATCH, SEQ, D_MODEL), jnp.float32),
        "mem": jax.random.normal(next(ks), (BATCH, MEM_TOKENS, D_MODEL), jnp.float32),
        "norm_mix": gain((L, D_MODEL)),
        "w_in": w((L, D_MODEL, IN_W), D_MODEL),
        "w_cmp_k1": w((L, CMP_BLOCK * HEAD_DIM, CMP_HIDDEN), CMP_BLOCK * HEAD_DIM),
        "w_cmp_k2": w((L, CMP_HIDDEN, HEAD_DIM), CMP_HIDDEN),
        "w_cmp_v1": w((L, CMP_BLOCK * HEAD_DIM, CMP_HIDDEN), CMP_BLOCK * HEAD_DIM),
        "w_cmp_v2": w((L, CMP_HIDDEN, HEAD_DIM), CMP_HIDDEN),
        "pe_cmp_k": small((L, CMP_BLOCK, HEAD_DIM), 0.1),
        "pe_cmp_v": small((L, CMP_BLOCK, HEAD_DIM), 0.1),
        "ln_sgu": gain((L, GMLP_WIDTH)),
        "w_spatial": w((L, GMLP_GROUPS, GMLP_CHUNK, GMLP_CHUNK), GMLP_CHUNK),
        "b_spatial": 1.0 + small((L, GMLP_GROUPS, GMLP_CHUNK), 0.1),
        "w_proj_a": w((L, NSA_W, D_MODEL), NSA_W),
        "w_proj_b": w((L, GMLP_WIDTH, D_MODEL), GMLP_WIDTH),
        "w_merge": w((L, D_MODEL, 2 * D_MODEL), D_MODEL),
        "b_merge": small((L, 2 * D_MODEL), 0.1),
        "w_out": w((L, D_MODEL, D_MODEL), D_MODEL),
        "norm_mem_q": gain((L, D_MODEL)),
        "norm_mem_kv": gain((L, D_MODEL)),
        "w_mq": w((L, D_MODEL, MEM_W), D_MODEL),
        "w_mkv": w((L, D_MODEL, 2 * MEM_W), D_MODEL),
        "w_mo": w((L, MEM_W, D_MODEL), MEM_W),
        "norm_ffn": gain((L, D_MODEL)),
        "w_gate_up": w((L, D_MODEL, 2 * D_FF), D_MODEL),
        "w_down": w((L, D_FF, D_MODEL), D_FF),
        "norm_final": gain((D_MODEL,)),
    }


def reference(x, mem, norm_mix, w_in, w_cmp_k1, w_cmp_k2, w_cmp_v1, w_cmp_v2, pe_cmp_k, pe_cmp_v,
              ln_sgu, w_spatial, b_spatial, w_proj_a, w_proj_b, w_merge, b_merge, w_out,
              norm_mem_q, norm_mem_kv, w_mq, w_mkv, w_mo, norm_ffn, w_gate_up, w_down, norm_final):
    B, T, _ = x.shape
    n_mem = mem.shape[1]
    h = x
    for l in range(DEPTH):
        xn = rmsnorm(h, norm_mix[l])
        proj = xn @ w_in[l]
        q, kvc, kvs, kvw, uv, g_raw = jnp.split(proj, SPLITS, axis=-1)
        q = q.reshape(B, T, N_HEADS_NSA, HEAD_DIM)
        kvc = kvc.reshape(B, T, 2, N_KV_GROUPS, HEAD_DIM)
        kvs = kvs.reshape(B, T, 2, N_KV_GROUPS, HEAD_DIM)
        kvw = kvw.reshape(B, T, 2, N_KV_GROUPS, HEAD_DIM)
        k_cmp = compress(kvc[:, :, 0], pe_cmp_k[l], w_cmp_k1[l], w_cmp_k2[l])
        v_cmp = compress(kvc[:, :, 1], pe_cmp_v[l], w_cmp_v1[l], w_cmp_v2[l])
        nsa_gates = jax.nn.sigmoid(g_raw.astype(jnp.float32)).reshape(B, T, N_HEADS_NSA, 3).astype(q.dtype)
        o_nsa = nsa_attention(q, k_cmp, v_cmp, kvs[:, :, 0], kvs[:, :, 1], kvw[:, :, 0], kvw[:, :, 1], nsa_gates)
        u, v = jnp.split(jax.nn.gelu(uv), 2, axis=-1)
        o_sgu = chunked_sgu(u, v, ln_sgu[l], w_spatial[l], b_spatial[l])
        mg = jax.nn.sigmoid((xn @ w_merge[l] + b_merge[l]).astype(jnp.float32)).astype(h.dtype)
        mg = mg.reshape(B, T, 2, D_MODEL)
        mixed = mg[:, :, 0] * (o_nsa @ w_proj_a[l]) + mg[:, :, 1] * (o_sgu @ w_proj_b[l])
        h = h + mixed @ w_out[l]
        hq = (rmsnorm(h, norm_mem_q[l]) @ w_mq[l]).reshape(B, T, MEM_HEADS, MEM_HEAD_DIM)
        mkv = (rmsnorm(mem, norm_mem_kv[l]) @ w_mkv[l]).reshape(B, n_mem, 2, MEM_HEADS, MEM_HEAD_DIM)
        s = jnp.einsum('bthd,bnhd->bhtn', hq, mkv[:, :, 0]).astype(jnp.float32) * MEM_HEAD_DIM ** -0.5
        p = jax.nn.softmax(s, axis=-1).astype(h.dtype)
        o = jnp.einsum('bhtn,bnhd->bthd', p, mkv[:, :, 1]).reshape(B, T, MEM_W)
        h = h + o @ w_mo[l]
        gate, up = jnp.split(rmsnorm(h, norm_ffn[l]) @ w_gate_up[l], 2, axis=-1)
        h = h + (jax.nn.silu(gate) * up) @ w_down[l]
    return rmsnorm(h, norm_final)
```

```python
import functools

import jax
import jax.numpy as jnp
from jax import lax
from jax.experimental import pallas as pl
from jax.experimental.pallas import tpu as pltpu

F32 = jnp.float32
BF16 = jnp.bfloat16

LANES = 128
D_MODEL = 1024
N_HEADS = 8
HEAD_DIM = 64
N_GROUPS = 2
HPG = N_HEADS // N_GROUPS
CMP_BLOCK = 32
CMP_STRIDE = 16
CMP_HIDDEN = 128
SEL_BLOCK = 64
N_SELECT = 16
WINDOW = 512
Q_BLOCK = 128
KEY_CHUNK = 128
N_BLK_PAD = 128
NSA_W = N_HEADS * HEAD_DIM
KV_W = 2 * N_GROUPS * HEAD_DIM
GMLP_WIDTH = 512
GMLP_GROUPS = 4
GMLP_CHUNK = 128
MEM_HEADS = 4
MEM_HEAD_DIM = 128
MEM_W = MEM_HEADS * MEM_HEAD_DIM
GATE_W = 3 * N_HEADS
FLAG_WORD_BITS = 16
EPS = 1e-6
NEG = -1e30
REMOVED = -3e38
FORCE_SCORE = 1e6
SLOPES = tuple(2.0 ** (-8.0 * (h + 1) / N_HEADS) for h in range(N_HEADS))
VMEM_LIMIT = 56 * 1024 * 1024


def _dot(a, b):
    return jnp.dot(a, b, preferred_element_type=F32)


def _dot_nt(a, b):
    return lax.dot_general(a, b, (((1,), (1,)), ((), ())), preferred_element_type=F32)


def _rms(x, g):
    return x * lax.rsqrt(jnp.mean(x * x, axis=-1, keepdims=True) + EPS) * g


def _iota(shape, dim):
    return lax.broadcasted_iota(jnp.int32, shape, dim)


def _inproj_kernel(x_ref, g_ref, lng_ref, wq_ref, wkvc_ref, wkvsw_ref, wuv_ref, wgate_ref,
                   q_ref, kvc_ref, kvsw_ref, u_ref, vn_ref, gates_ref):
    xn = _rms(x_ref[...], g_ref[...]).astype(BF16)
    q_ref[...] = (_dot(xn, wq_ref[...]) * (HEAD_DIM ** -0.5)).astype(BF16)
    kvc_ref[...] = _dot(xn, wkvc_ref[...])
    kvsw_ref[...] = _dot(xn, wkvsw_ref[...]).astype(BF16)
    uv = jax.nn.gelu(_dot(xn, wuv_ref[...]))
    u_ref[...] = uv[:, :GMLP_WIDTH]
    v = uv[:, GMLP_WIDTH:]
    vc = v - jnp.mean(v, axis=-1, keepdims=True)
    vn = vc * lax.rsqrt(jnp.mean(vc * vc, axis=-1, keepdims=True) + EPS) * lng_ref[...]
    vn_ref[...] = vn.astype(BF16)
    gates_ref[...] = jax.nn.sigmoid(_dot(xn, wgate_ref[...]))


def _inproj(x2, norm_g, ln_g, wq, wkvc, wkvsw, wuv, wgate, tm):
    n = x2.shape[0]
    row = lambda w: pl.BlockSpec((tm, w), lambda i: (i, 0))
    full = lambda a: pl.BlockSpec(a.shape, lambda i: (0,) * a.ndim)
    return pl.pallas_call(
        _inproj_kernel,
        grid=(n // tm,),
        in_specs=[row(D_MODEL), full(norm_g), full(ln_g), full(wq), full(wkvc), full(wkvsw),
                  full(wuv), full(wgate)],
        out_specs=[row(N_HEADS * LANES), row(KV_W), row(2 * KV_W), row(GMLP_WIDTH),
                   row(GMLP_WIDTH), row(LANES)],
        out_shape=[jax.ShapeDtypeStruct((n, N_HEADS * LANES), BF16),
                   jax.ShapeDtypeStruct((n, KV_W), F32),
                   jax.ShapeDtypeStruct((n, 2 * KV_W), BF16),
                   jax.ShapeDtypeStruct((n, GMLP_WIDTH), F32),
                   jax.ShapeDtypeStruct((n, GMLP_WIDTH), BF16),
                   jax.ShapeDtypeStruct((n, LANES), F32)],
        compiler_params=pltpu.CompilerParams(dimension_semantics=("arbitrary",),
                                             vmem_limit_bytes=VMEM_LIMIT),
        name="inproj",
    )(x2, norm_g, ln_g, wq, wkvc, wkvsw, wuv, wgate)


def _compress_kernel(x_ref, pe_ref, w1_ref, w2_ref, out_ref, *, nc):
    a = jnp.zeros((nc, 2 * CMP_HIDDEN), F32)
    b = jnp.zeros((nc, 2 * CMP_HIDDEN), F32)
    for t in range(CMP_STRIDE):
        xt = x_ref[0, pl.ds(t, nc, stride=CMP_STRIDE), :]
        a = a + _dot((xt + pe_ref[0, t:t + 1, :]).astype(BF16), w1_ref[0, t])
        b = b + _dot((xt + pe_ref[0, CMP_STRIDE + t:CMP_STRIDE + t + 1, :]).astype(BF16),
                     w1_ref[0, CMP_STRIDE + t])
    h = a + pltpu.roll(b, nc - 1, 0)
    out_ref[0, 0] = _dot(jax.nn.gelu(h).astype(BF16), w2_ref[0]).astype(BF16)


def _compress(kvc3, pe2, w1bd, w2bd):
    bsz, t, _ = kvc3.shape
    nc = t // CMP_STRIDE
    return pl.pallas_call(
        functools.partial(_compress_kernel, nc=nc),
        grid=(bsz, 2),
        in_specs=[pl.BlockSpec((1, t, LANES), lambda b, j: (b, 0, j)),
                  pl.BlockSpec((1, CMP_BLOCK, LANES), lambda b, j: (j, 0, 0)),
                  pl.BlockSpec((1, CMP_BLOCK, LANES, 2 * CMP_HIDDEN), lambda b, j: (j, 0, 0, 0)),
                  pl.BlockSpec((1, 2 * CMP_HIDDEN, LANES), lambda b, j: (j, 0, 0))],
        out_specs=pl.BlockSpec((1, 1, nc, LANES), lambda b, j: (b, j, 0, 0)),
        out_shape=jax.ShapeDtypeStruct((bsz, 2, nc, LANES), BF16),
        compiler_params=pltpu.CompilerParams(dimension_semantics=("arbitrary", "arbitrary"),
                                             vmem_limit_bytes=VMEM_LIMIT),
        name="compress",
    )(kvc3, pe2, w1bd, w2bd)


def _stack_group_queries(q_ref, g):
    return jnp.concatenate(
        [q_ref[0, :, LANES * (HPG * g + j):LANES * (HPG * g + j + 1)] for j in range(HPG)], axis=0)


def _place_heads(per_head):
    lane = _iota((Q_BLOCK, LANES), 1)
    tiles = []
    for m in range(N_HEADS // 2):
        g = (2 * m) // HPG
        a, b = per_head[2 * m], per_head[2 * m + 1]
        if g == 0:
            tiles.append(jnp.where(lane < HEAD_DIM, a, pltpu.roll(b, HEAD_DIM, 1)))
        else:
            tiles.append(jnp.where(lane < HEAD_DIM, pltpu.roll(a, HEAD_DIM, 1), b))
    return jnp.concatenate(tiles, axis=1)


def _dot_f32_lhs(a, b_bf16):
    a0 = a.astype(BF16)
    r1 = a - a0.astype(F32)
    a1 = r1.astype(BF16)
    a2 = (r1 - a1.astype(F32)).astype(BF16)
    return _dot(a0, b_bf16) + _dot(a1, b_bf16) + _dot(a2, b_bf16)


def _cmp_topk_kernel(q_ref, kc_ref, vc_ref, gates_ref, ocmp_ref, sel_ref, flags_ref, *, ncp):
    start = pl.program_id(1) * Q_BLOCK
    t_col = start + _iota((Q_BLOCK, 1), 0)
    cmp_end = CMP_STRIDE * _iota((1, ncp), 1) + (CMP_BLOCK - 1)
    valid = cmp_end <= t_col
    dist = (t_col - cmp_end).astype(F32)
    kc = kc_ref[0, 0]
    vc = vc_ref[0, 0]
    ci = _iota((ncp, N_BLK_PAD), 0) * CMP_STRIDE
    sj0 = _iota((ncp, N_BLK_PAD), 1) * SEL_BLOCK
    overlap = jnp.where((ci < sj0 + SEL_BLOCK) & (ci + (CMP_BLOCK - 1) >= sj0), 1.0, 0.0).astype(BF16)
    blk = _iota((Q_BLOCK, N_BLK_PAD), 1)
    blk_f = blk.astype(F32)
    cur = lax.shift_right_logical(t_col, 6)
    causal = blk <= cur
    forced = (blk == 0) | (blk == cur) | (blk == cur - 1)
    gates = gates_ref[0]
    flags_ref[...] = jnp.zeros(flags_ref.shape, jnp.int32)
    per_head = []
    for g in range(N_GROUPS):
        s = _dot_nt(_stack_group_queries(q_ref, g), kc)
        psum = jnp.zeros((Q_BLOCK, ncp), F32)
        for j in range(HPG):
            h = HPG * g + j
            sj = s[Q_BLOCK * j:Q_BLOCK * (j + 1)] - SLOPES[h] * dist
            sj = jnp.where(valid, sj, NEG)
            m = jnp.max(sj, axis=-1, keepdims=True)
            p = jnp.where(valid, jnp.exp(sj - m), 0.0)
            pn = p / jnp.maximum(jnp.sum(p, axis=-1, keepdims=True), 1e-30)
            psum = psum + pn
            per_head.append(gates[:, 3 * h:3 * h + 1] * _dot(pn.astype(BF16), vc))
        imp = _dot_f32_lhs(psum, overlap)
        rank = jnp.where(causal, jnp.where(forced, FORCE_SCORE, imp), NEG)
        sel = jnp.zeros((Q_BLOCK, N_BLK_PAD), F32)
        for _ in range(N_SELECT):
            m = jnp.max(rank, axis=-1, keepdims=True)
            idx = jnp.min(jnp.where(rank == m, blk_f, float(N_BLK_PAD)), axis=-1, keepdims=True)
            chosen = blk_f == idx
            sel = jnp.where(chosen, 1.0, sel)
            rank = jnp.where(chosen, REMOVED, rank)
        sel = jnp.where(causal, sel, 0.0)
        sel_ref[0, g] = sel.astype(BF16)
        flags_ref[0, 0, g:g + 1, :] = jnp.max(sel, axis=0, keepdims=True).astype(jnp.int32)
    ocmp_ref[0] = _place_heads(per_head)


def _cmp_topk(q3, kvcmp, gates3):
    bsz, t, _ = q3.shape
    ncp = t // CMP_STRIDE
    nqb = t // Q_BLOCK
    return pl.pallas_call(
        functools.partial(_cmp_topk_kernel, ncp=ncp),
        grid=(bsz, nqb),
        in_specs=[pl.BlockSpec((1, Q_BLOCK, N_HEADS * LANES), lambda b, i: (b, i, 0)),
                  pl.BlockSpec((1, 1, ncp, LANES), lambda b, i: (b, 0, 0, 0)),
                  pl.BlockSpec((1, 1, ncp, LANES), lambda b, i: (b, 1, 0, 0)),
                  pl.BlockSpec((1, Q_BLOCK, LANES), lambda b, i: (b, i, 0))],
        out_specs=[pl.BlockSpec((1, Q_BLOCK, NSA_W), lambda b, i: (b, i, 0)),
                   pl.BlockSpec((1, N_GROUPS, Q_BLOCK, N_BLK_PAD), lambda b, i: (b, 0, i, 0)),
                   pl.BlockSpec((1, 1, 8, N_BLK_PAD), lambda b, i: (b, i, 0, 0))],
        out_shape=[jax.ShapeDtypeStruct((bsz, t, NSA_W), F32),
                   jax.ShapeDtypeStruct((bsz, N_GROUPS, t, N_BLK_PAD), BF16),
                   jax.ShapeDtypeStruct((bsz, nqb, 8, N_BLK_PAD), jnp.int32)],
        compiler_params=pltpu.CompilerParams(dimension_semantics=("arbitrary", "arbitrary"),
                                             vmem_limit_bytes=VMEM_LIMIT),
        name="cmp_topk",
    )(q3, kvcmp, kvcmp, gates3)


def _slc_win_kernel(words_ref, q_ref, ks_ref, vs_ref, kw_ref, vw_ref, sel_ref, gates_ref,
                    ocmp_ref, out_ref, m_ref, l_ref, acc_ref, *, nqb):
    b = pl.program_id(0)
    qb = pl.program_id(1)
    t_col = qb * Q_BLOCK + _iota((Q_BLOCK, 1), 0)
    key_iota = _iota((1, KEY_CHUNK), 1)
    e_row = _iota((N_BLK_PAD, KEY_CHUNK), 0)
    e_col = lax.shift_right_logical(_iota((N_BLK_PAD, KEY_CHUNK), 1), 6)
    gates = gates_ref[0]
    words_per_row = (N_BLK_PAD // 2) // FLAG_WORD_BITS

    def init():
        m_ref[...] = jnp.full(m_ref.shape, NEG, F32)
        l_ref[...] = jnp.zeros(l_ref.shape, F32)
        acc_ref[...] = jnp.zeros(acc_ref.shape, F32)

    def attend(g, qg, kc, vc, allowed, dist):
        s = _dot_nt(qg, kc)
        for j in range(HPG):
            sj = s[Q_BLOCK * j:Q_BLOCK * (j + 1)] - SLOPES[HPG * g + j] * dist
            sj = jnp.where(allowed, sj, NEG)
            m_old = m_ref[j]
            m_new = jnp.maximum(m_old, jnp.max(sj, axis=-1, keepdims=True))
            alpha = jnp.exp(m_old - m_new)
            p = jnp.where(allowed, jnp.exp(sj - m_new), 0.0)
            l_ref[j] = alpha * l_ref[j] + jnp.sum(p, axis=-1, keepdims=True)
            acc_ref[j] = alpha * acc_ref[j] + _dot(p.astype(BF16), vc)
            m_ref[j] = m_new

    def finish():
        return [acc_ref[j] / jnp.maximum(l_ref[j], 1e-30) for j in range(HPG)]

    per_head = []
    for g in range(N_GROUPS):
        qg = _stack_group_queries(q_ref, g)
        selg = sel_ref[0, g]
        word_base = ((b * nqb + qb) * N_GROUPS + g) * words_per_row

        init()

        def slc_body(c, carry):
            word = words_ref[word_base + c // FLAG_WORD_BITS]
            bit = lax.shift_right_logical(word, c % FLAG_WORD_BITS) & 1

            @pl.when(bit != 0)
            def _():
                off = pl.multiple_of(c * KEY_CHUNK, KEY_CHUNK)
                pos = c * KEY_CHUNK + key_iota
                expand = jnp.where(e_row == 2 * c + e_col, 1.0, 0.0).astype(BF16)
                picked = _dot(selg, expand)
                allowed = (picked > 0.5) & (pos <= t_col)
                attend(g, qg, ks_ref[0, pl.ds(off, KEY_CHUNK), :], vs_ref[0, pl.ds(off, KEY_CHUNK), :],
                       allowed, (t_col - pos).astype(F32))
            return carry

        lax.fori_loop(0, qb + 1, slc_body, 0)
        o_slc = finish()

        init()
        for d in range(WINDOW // KEY_CHUNK + 1):
            c = qb - WINDOW // KEY_CHUNK + d

            @pl.when(c >= 0)
            def _():
                off = pl.multiple_of(jnp.maximum(c, 0) * KEY_CHUNK, KEY_CHUNK)
                dist_i = t_col - (c * KEY_CHUNK + key_iota)
                allowed = (dist_i >= 0) & (dist_i < WINDOW)
                attend(g, qg, kw_ref[0, pl.ds(off, KEY_CHUNK), :], vw_ref[0, pl.ds(off, KEY_CHUNK), :],
                       allowed, dist_i.astype(F32))
        o_win = finish()

        for j in range(HPG):
            h = HPG * g + j
            per_head.append(gates[:, 3 * h + 1:3 * h + 2] * o_slc[j]
                            + gates[:, 3 * h + 2:3 * h + 3] * o_win[j])
    out_ref[0] = (ocmp_ref[0] + _place_heads(per_head)).astype(BF16)


def _slc_win(words, q3, kvsw3, sel, gates3, ocmp):
    bsz, t, _ = q3.shape
    nqb = t // Q_BLOCK
    kv_spec = lambda col: pl.BlockSpec((1, t, LANES), lambda b, i, w: (b, 0, col))
    grid_spec = pltpu.PrefetchScalarGridSpec(
        num_scalar_prefetch=1,
        grid=(bsz, nqb),
        in_specs=[pl.BlockSpec((1, Q_BLOCK, N_HEADS * LANES), lambda b, i, w: (b, i, 0)),
                  kv_spec(0), kv_spec(1), kv_spec(2), kv_spec(3),
                  pl.BlockSpec((1, N_GROUPS, Q_BLOCK, N_BLK_PAD), lambda b, i, w: (b, 0, i, 0)),
                  pl.BlockSpec((1, Q_BLOCK, LANES), lambda b, i, w: (b, i, 0)),
                  pl.BlockSpec((1, Q_BLOCK, NSA_W), lambda b, i, w: (b, i, 0))],
        out_specs=pl.BlockSpec((1, Q_BLOCK, NSA_W), lambda b, i, w: (b, i, 0)),
        scratch_shapes=[pltpu.VMEM((HPG, Q_BLOCK, 1), F32),
                        pltpu.VMEM((HPG, Q_BLOCK, 1), F32),
                        pltpu.VMEM((HPG, Q_BLOCK, LANES), F32)],
    )
    return pl.pallas_call(
        functools.partial(_slc_win_kernel, nqb=nqb),
        grid_spec=grid_spec,
        out_shape=jax.ShapeDtypeStruct((bsz, t, NSA_W), BF16),
        compiler_params=pltpu.CompilerParams(dimension_semantics=("arbitrary", "arbitrary"),
                                             vmem_limit_bytes=VMEM_LIMIT),
        name="slc_win",
    )(words, q3, kvsw3, kvsw3, kvsw3, kvsw3, sel, gates3, ocmp)


def _merge_kernel(x_ref, onsa_ref, u_ref, vn_ref, g_ref, wm_ref, bm_ref, ws_ref, bs_ref,
                  wpa_ref, wpb_ref, wo_ref, h_ref, *, tm):
    x = x_ref[...]
    xn = _rms(x, g_ref[...]).astype(BF16)
    tril = _iota((GMLP_CHUNK, GMLP_CHUNK), 0) >= _iota((GMLP_CHUNK, GMLP_CHUNK), 1)
    sgu_rows = []
    for c in range(tm // GMLP_CHUNK):
        rows = slice(GMLP_CHUNK * c, GMLP_CHUNK * (c + 1))
        cols = []
        for g in range(GMLP_GROUPS):
            lanes = slice(LANES * g, LANES * (g + 1))
            w = jnp.where(tril, ws_ref[g], 0.0).astype(BF16)
            cols.append(_dot(w, vn_ref[rows, lanes]) + bs_ref[:, g:g + 1])
        sgu_rows.append(u_ref[rows, :] * jnp.concatenate(cols, axis=1))
    o_sgu = jnp.concatenate(sgu_rows, axis=0).astype(BF16)
    mg = jax.nn.sigmoid(_dot(xn, wm_ref[...]) + bm_ref[...])
    mixed = (mg[:, :D_MODEL] * _dot(onsa_ref[...], wpa_ref[...])
             + mg[:, D_MODEL:] * _dot(o_sgu, wpb_ref[...]))
    h_ref[...] = x + _dot(mixed.astype(BF16), wo_ref[...])


def _merge(x2, onsa2, u2, vn2, norm_g, wm, bm, ws, bs_t, wpa, wpb, wo, tm):
    n = x2.shape[0]
    row = lambda w: pl.BlockSpec((tm, w), lambda i: (i, 0))
    full = lambda a: pl.BlockSpec(a.shape, lambda i: (0,) * a.ndim)
    return pl.pallas_call(
        functools.partial(_merge_kernel, tm=tm),
        grid=(n // tm,),
        in_specs=[row(D_MODEL), row(NSA_W), row(GMLP_WIDTH), row(GMLP_WIDTH), full(norm_g),
                  full(wm), full(bm), full(ws), full(bs_t), full(wpa), full(wpb), full(wo)],
        out_specs=row(D_MODEL),
        out_shape=jax.ShapeDtypeStruct((n, D_MODEL), F32),
        compiler_params=pltpu.CompilerParams(dimension_semantics=("arbitrary",),
                                             vmem_limit_bytes=VMEM_LIMIT),
        name="merge",
    )(x2, onsa2, u2, vn2, norm_g, wm, bm, ws, bs_t, wpa, wpb, wo)


def _memkv_kernel(mem_ref, g_ref, w_ref, out_ref):
    out_ref[0] = _dot(_rms(mem_ref[0], g_ref[...]).astype(BF16), w_ref[...]).astype(BF16)


def _memkv(mem, norm_g, w):
    bsz, nm, _ = mem.shape
    return pl.pallas_call(
        _memkv_kernel,
        grid=(bsz,),
        in_specs=[pl.BlockSpec((1, nm, D_MODEL), lambda b: (b, 0, 0)),
                  pl.BlockSpec(norm_g.shape, lambda b: (0, 0)),
                  pl.BlockSpec(w.shape, lambda b: (0, 0))],
        out_specs=pl.BlockSpec((1, nm, 2 * MEM_W), lambda b: (b, 0, 0)),
        out_shape=jax.ShapeDtypeStruct((bsz, nm, 2 * MEM_W), BF16),
        compiler_params=pltpu.CompilerParams(dimension_semantics=("arbitrary",),
                                             vmem_limit_bytes=VMEM_LIMIT),
        name="memkv",
    )(mem, norm_g, w)


def _xattn_kernel(h_ref, g_ref, wq_ref, mkv_ref, wo_ref, out_ref):
    h = h_ref[...]
    hq = _dot(_rms(h, g_ref[...]).astype(BF16), wq_ref[...]).astype(BF16)
    heads = []
    for a in range(MEM_HEADS):
        lanes = slice(MEM_HEAD_DIM * a, MEM_HEAD_DIM * (a + 1))
        k = mkv_ref[0, :, lanes]
        v = mkv_ref[0, :, MEM_W + MEM_HEAD_DIM * a:MEM_W + MEM_HEAD_DIM * (a + 1)]
        s = _dot_nt(hq[:, lanes], k) * (MEM_HEAD_DIM ** -0.5)
        e = jnp.exp(s - jnp.max(s, axis=-1, keepdims=True))
        p = e / jnp.sum(e, axis=-1, keepdims=True)
        heads.append(_dot(p.astype(BF16), v))
    o = jnp.concatenate(heads, axis=1).astype(BF16)
    out_ref[...] = h + _dot(o, wo_ref[...])


def _xattn(h2d, norm_g, wq, mkv, wo, tm, rows_per_batch):
    n = h2d.shape[0]
    nm = mkv.shape[1]
    tiles_per_batch = rows_per_batch // tm
    full = lambda a: pl.BlockSpec(a.shape, lambda i: (0,) * a.ndim)
    return pl.pallas_call(
        _xattn_kernel,
        grid=(n // tm,),
        in_specs=[pl.BlockSpec((tm, D_MODEL), lambda i: (i, 0)), full(norm_g), full(wq),
                  pl.BlockSpec((1, nm, 2 * MEM_W), lambda i: (i // tiles_per_batch, 0, 0)),
                  full(wo)],
        out_specs=pl.BlockSpec((tm, D_MODEL), lambda i: (i, 0)),
        out_shape=jax.ShapeDtypeStruct((n, D_MODEL), F32),
        compiler_params=pltpu.CompilerParams(dimension_semantics=("arbitrary",),
                                             vmem_limit_bytes=VMEM_LIMIT),
        name="xattn",
    )(h2d, norm_g, wq, mkv, wo)


def _ffn_kernel(h_ref, g_ref, wgu_ref, wd_ref, gf_ref, out_ref, *, d_ff):
    h = h_ref[...]
    hn = _rms(h, g_ref[...]).astype(BF16)
    gate = _dot(hn, wgu_ref[:, :d_ff])
    up = _dot(hn, wgu_ref[:, d_ff:])
    act = (jax.nn.silu(gate) * up).astype(BF16)
    y = h + _dot(act, wd_ref[...])
    out_ref[...] = _rms(y, gf_ref[...])


def _ffn(h2d, norm_g, wgu, wd, norm_f, tm):
    n = h2d.shape[0]
    d_ff = wd.shape[0]
    full = lambda a: pl.BlockSpec(a.shape, lambda i: (0,) * a.ndim)
    once = lambda a: pl.BlockSpec(a.shape, lambda i: (0,) * a.ndim, pipeline_mode=pl.Buffered(1))
    return pl.pallas_call(
        functools.partial(_ffn_kernel, d_ff=d_ff),
        grid=(n // tm,),
        in_specs=[pl.BlockSpec((tm, D_MODEL), lambda i: (i, 0)), full(norm_g), once(wgu), once(wd),
                  full(norm_f)],
        out_specs=pl.BlockSpec((tm, D_MODEL), lambda i: (i, 0)),
        out_shape=jax.ShapeDtypeStruct((n, D_MODEL), F32),
        compiler_params=pltpu.CompilerParams(dimension_semantics=("arbitrary",),
                                             vmem_limit_bytes=VMEM_LIMIT),
        name="ffn",
    )(h2d, norm_g, wgu, wd, norm_f)


def _pad_q_weight(wq):
    d = wq.shape[0]
    w = wq.reshape(d, N_HEADS, HEAD_DIM)
    z = jnp.zeros_like(w)
    lo = jnp.concatenate([w, z], axis=-1)
    hi = jnp.concatenate([z, w], axis=-1)
    in_g0 = (jnp.arange(N_HEADS) < HPG)[None, :, None]
    return jnp.where(in_g0, lo, hi).reshape(d, N_HEADS * LANES)


def _block_diag2(w):
    z = jnp.zeros_like(w)
    return jnp.concatenate([jnp.concatenate([w, z], axis=-1), jnp.concatenate([z, w], axis=-1)], axis=-2)


def _flag_words(flags):
    bsz, nqb = flags.shape[:2]
    f = flags[:, :, :N_GROUPS, :].reshape(bsz, nqb, N_GROUPS, N_BLK_PAD // 2, 2).max(axis=-1)
    f = f.reshape(bsz, nqb, N_GROUPS, -1, FLAG_WORD_BITS)
    words = jnp.sum(f << jnp.arange(FLAG_WORD_BITS, dtype=jnp.int32), axis=-1)
    return words.reshape(-1).astype(jnp.int32)


def kernel(x, mem, norm_mix, w_in, w_cmp_k1, w_cmp_k2, w_cmp_v1, w_cmp_v2, pe_cmp_k, pe_cmp_v, ln_sgu, w_spatial, b_spatial, w_proj_a, w_proj_b, w_merge, b_merge, w_out, norm_mem_q, norm_mem_kv, w_mq, w_mkv, w_mo, norm_ffn, w_gate_up, w_down, norm_final):
    bsz, t, d = x.shape
    depth = norm_mix.shape[0]
    assert d == D_MODEL and t % Q_BLOCK == 0 and t // SEL_BLOCK <= N_BLK_PAD
    assert t // SEL_BLOCK >= N_SELECT
    n = bsz * t
    tm = 256
    h = x.reshape(n, d)
    c0, c1, c2, c3 = NSA_W, NSA_W + KV_W, NSA_W + 3 * KV_W, NSA_W + 3 * KV_W + 2 * GMLP_WIDTH
    for l in range(depth):
        wi = w_in[l]
        wq = _pad_q_weight(wi[:, :c0]).astype(BF16)
        wkvc = wi[:, c0:c1].astype(BF16)
        wkvsw = wi[:, c1:c2].astype(BF16)
        wuv = wi[:, c2:c3].astype(BF16)
        wgate = jnp.pad(wi[:, c3:], ((0, 0), (0, LANES - GATE_W))).astype(BF16)
        q2, kvc2, kvsw2, u2, vn2, gates2 = _inproj(
            h, norm_mix[l][None], ln_sgu[l][None], wq, wkvc, wkvsw, wuv, wgate, tm)

        pe2 = jnp.stack([pe_cmp_k[l], pe_cmp_v[l]])
        pe2 = jnp.concatenate([pe2, pe2], axis=-1)
        w1 = jnp.stack([w_cmp_k1[l], w_cmp_v1[l]]).reshape(2, CMP_BLOCK, HEAD_DIM, CMP_HIDDEN)
        w1bd = _block_diag2(w1).astype(BF16)
        w2bd = _block_diag2(jnp.stack([w_cmp_k2[l], w_cmp_v2[l]])).astype(BF16)
        kvcmp = _compress(kvc2.reshape(bsz, t, KV_W), pe2, w1bd, w2bd)

        q3 = q2.reshape(bsz, t, N_HEADS * LANES)
        gates3 = gates2.reshape(bsz, t, LANES)
        ocmp, sel, flags = _cmp_topk(q3, kvcmp, gates3)
        onsa = _slc_win(_flag_words(flags), q3, kvsw2.reshape(bsz, t, 2 * KV_W), sel, gates3, ocmp)

        h = _merge(h, onsa.reshape(n, NSA_W), u2, vn2, norm_mix[l][None], w_merge[l].astype(BF16),
                   b_merge[l][None], w_spatial[l], b_spatial[l].T, w_proj_a[l].astype(BF16),
                   w_proj_b[l].astype(BF16), w_out[l].astype(BF16), tm)

        mkv = _memkv(mem, norm_mem_kv[l][None], w_mkv[l].astype(BF16))
        h = _xattn(h, norm_mem_q[l][None], w_mq[l].astype(BF16), mkv, w_mo[l].astype(BF16), tm, t)
        assert l == depth - 1 or depth == 1
        h = _ffn(h, norm_ffn[l][None], w_gate_up[l].astype(BF16), w_down[l].astype(BF16),
                 norm_final[None], tm)
    return h.reshape(bsz, t, d)
```

```python
import functools

import jax
import jax.numpy as jnp
from jax import lax
from jax.experimental import pallas as pl
from jax.experimental.pallas import tpu as pltpu

F32 = jnp.float32
BF16 = jnp.bfloat16

LANES = 128
D_MODEL = 1024
N_HEADS = 8
HEAD_DIM = 64
N_GROUPS = 2
HPG = N_HEADS // N_GROUPS
CMP_BLOCK = 32
CMP_STRIDE = 16
CMP_HIDDEN = 128
SEL_BLOCK = 64
N_SELECT = 16
WINDOW = 512
Q_BLOCK = 128
KEY_CHUNK = 128
N_BLK_PAD = 128
NSA_W = N_HEADS * HEAD_DIM
KV_W = 2 * N_GROUPS * HEAD_DIM
GMLP_WIDTH = 512
GMLP_GROUPS = 4
GMLP_CHUNK = 128
MEM_HEADS = 4
MEM_HEAD_DIM = 128
MEM_W = MEM_HEADS * MEM_HEAD_DIM
GATE_W = 3 * N_HEADS
SLC_BATCH = 4
LIST_WORDS = (N_BLK_PAD // 2) // SLC_BATCH
VT_ROWS = 80
GATET_ROWS = 32
MASK_BIG = 1e30
EPS = 1e-6
NEG = -1e30
REMOVED = -3e38
FORCE_SCORE = 1e6
SLOPES = tuple(2.0 ** (-8.0 * (h + 1) / N_HEADS) for h in range(N_HEADS))
VMEM_LIMIT = 56 * 1024 * 1024


def _dot(a, b):
    return jnp.dot(a, b, preferred_element_type=F32)


def _dot_nt(a, b):
    return lax.dot_general(a, b, (((1,), (1,)), ((), ())), preferred_element_type=F32)


def _rms(x, g):
    return x * lax.rsqrt(jnp.mean(x * x, axis=-1, keepdims=True) + EPS) * g


def _iota(shape, dim):
    return lax.broadcasted_iota(jnp.int32, shape, dim)


def _inproj_kernel(x_ref, g_ref, lng_ref, wq_ref, wqt_ref, wkvc_ref, wk_ref, wvt_ref, wuv_ref,
                   wgate_ref, wgatet_ref,
                   q_ref, qt_ref, kvc_ref, kaug_ref, vt_ref, u_ref, vn_ref, gates_ref, gatest_ref,
                   *, tm):
    xn = _rms(x_ref[...], g_ref[...]).astype(BF16)
    q_ref[...] = (_dot(xn, wq_ref[...]) * (HEAD_DIM ** -0.5)).astype(BF16)
    qt_ref[0] = (_dot_nt(wqt_ref[...], xn) * (HEAD_DIM ** -0.5)).astype(BF16)
    kvc_ref[...] = _dot(xn, wkvc_ref[...])
    lane = _iota((tm, LANES), 1)
    key_feat = jnp.where(lane == HEAD_DIM, _iota((tm, LANES), 0) & (KEY_CHUNK - 1), 0).astype(F32)
    ones_row = jnp.where(_iota((VT_ROWS, tm), 0) == HEAD_DIM, 1.0, 0.0)
    for a in range(2 * N_GROUPS):
        kaug_ref[0, a] = (_dot(xn, wk_ref[a]) + key_feat).astype(BF16)
        vt_ref[0, a] = (_dot_nt(wvt_ref[a], xn) + ones_row).astype(BF16)
    uv = jax.nn.gelu(_dot(xn, wuv_ref[...]))
    u_ref[...] = uv[:, :GMLP_WIDTH]
    v = uv[:, GMLP_WIDTH:]
    vc = v - jnp.mean(v, axis=-1, keepdims=True)
    vn = vc * lax.rsqrt(jnp.mean(vc * vc, axis=-1, keepdims=True) + EPS) * lng_ref[...]
    vn_ref[...] = vn.astype(BF16)
    gates_ref[...] = jax.nn.sigmoid(_dot(xn, wgate_ref[...]))
    gatest_ref[0] = jax.nn.sigmoid(_dot_nt(wgatet_ref[...], xn))


def _inproj(x2, norm_g, ln_g, wq, wqt, wkvc, wk, wvt, wuv, wgate, wgatet, tm, bsz, t):
    n = x2.shape[0]
    tpb = t // tm
    row = lambda w: pl.BlockSpec((tm, w), lambda i: (i, 0))
    full = lambda a: pl.BlockSpec(a.shape, lambda i: (0,) * a.ndim)
    return pl.pallas_call(
        functools.partial(_inproj_kernel, tm=tm),
        grid=(n // tm,),
        in_specs=[row(D_MODEL), full(norm_g), full(ln_g), full(wq), full(wqt), full(wkvc), full(wk),
                  full(wvt), full(wuv), full(wgate), full(wgatet)],
        out_specs=[row(N_HEADS * LANES),
                   pl.BlockSpec((1, NSA_W, tm), lambda i: (i // tpb, 0, i % tpb)),
                   row(KV_W),
                   pl.BlockSpec((1, 2 * N_GROUPS, tm, LANES), lambda i: (i // tpb, 0, i % tpb, 0)),
                   pl.BlockSpec((1, 2 * N_GROUPS, VT_ROWS, tm), lambda i: (i // tpb, 0, 0, i % tpb)),
                   row(GMLP_WIDTH), row(GMLP_WIDTH), row(LANES),
                   pl.BlockSpec((1, GATET_ROWS, tm), lambda i: (i // tpb, 0, i % tpb))],
        out_shape=[jax.ShapeDtypeStruct((n, N_HEADS * LANES), BF16),
                   jax.ShapeDtypeStruct((bsz, NSA_W, t), BF16),
                   jax.ShapeDtypeStruct((n, KV_W), F32),
                   jax.ShapeDtypeStruct((bsz, 2 * N_GROUPS, t, LANES), BF16),
                   jax.ShapeDtypeStruct((bsz, 2 * N_GROUPS, VT_ROWS, t), BF16),
                   jax.ShapeDtypeStruct((n, GMLP_WIDTH), F32),
                   jax.ShapeDtypeStruct((n, GMLP_WIDTH), BF16),
                   jax.ShapeDtypeStruct((n, LANES), F32),
                   jax.ShapeDtypeStruct((bsz, GATET_ROWS, t), F32)],
        compiler_params=pltpu.CompilerParams(dimension_semantics=("arbitrary",),
                                             vmem_limit_bytes=VMEM_LIMIT),
        name="inproj",
    )(x2, norm_g, ln_g, wq, wqt, wkvc, wk, wvt, wuv, wgate, wgatet)


def _compress_kernel(x_ref, pe_ref, w1_ref, w2_ref, out_ref, *, nc):
    a = jnp.zeros((nc, 2 * CMP_HIDDEN), F32)
    b = jnp.zeros((nc, 2 * CMP_HIDDEN), F32)
    for t in range(CMP_STRIDE):
        xt = x_ref[0, pl.ds(t, nc, stride=CMP_STRIDE), :]
        a = a + _dot((xt + pe_ref[0, t:t + 1, :]).astype(BF16), w1_ref[0, t])
        b = b + _dot((xt + pe_ref[0, CMP_STRIDE + t:CMP_STRIDE + t + 1, :]).astype(BF16),
                     w1_ref[0, CMP_STRIDE + t])
    h = a + pltpu.roll(b, nc - 1, 0)
    out_ref[0, 0] = _dot(jax.nn.gelu(h).astype(BF16), w2_ref[0]).astype(BF16)


def _compress(kvc3, pe2, w1bd, w2bd):
    bsz, t, _ = kvc3.shape
    nc = t // CMP_STRIDE
    return pl.pallas_call(
        functools.partial(_compress_kernel, nc=nc),
        grid=(bsz, 2),
        in_specs=[pl.BlockSpec((1, t, LANES), lambda b, j: (b, 0, j)),
                  pl.BlockSpec((1, CMP_BLOCK, LANES), lambda b, j: (j, 0, 0)),
                  pl.BlockSpec((1, CMP_BLOCK, LANES, 2 * CMP_HIDDEN), lambda b, j: (j, 0, 0, 0)),
                  pl.BlockSpec((1, 2 * CMP_HIDDEN, LANES), lambda b, j: (j, 0, 0))],
        out_specs=pl.BlockSpec((1, 1, nc, LANES), lambda b, j: (b, j, 0, 0)),
        out_shape=jax.ShapeDtypeStruct((bsz, 2, nc, LANES), BF16),
        compiler_params=pltpu.CompilerParams(dimension_semantics=("arbitrary", "arbitrary"),
                                             vmem_limit_bytes=VMEM_LIMIT),
        name="compress",
    )(kvc3, pe2, w1bd, w2bd)


def _stack_group_queries(q_ref, g):
    return jnp.concatenate(
        [q_ref[0, :, LANES * (HPG * g + j):LANES * (HPG * g + j + 1)] for j in range(HPG)], axis=0)


def _place_heads(per_head):
    lane = _iota((Q_BLOCK, LANES), 1)
    tiles = []
    for m in range(N_HEADS // 2):
        g = (2 * m) // HPG
        a, b = per_head[2 * m], per_head[2 * m + 1]
        if g == 0:
            tiles.append(jnp.where(lane < HEAD_DIM, a, pltpu.roll(b, HEAD_DIM, 1)))
        else:
            tiles.append(jnp.where(lane < HEAD_DIM, pltpu.roll(a, HEAD_DIM, 1), b))
    return jnp.concatenate(tiles, axis=1)


def _dot_f32_lhs(a, b_bf16):
    a0 = a.astype(BF16)
    r1 = a - a0.astype(F32)
    a1 = r1.astype(BF16)
    a2 = (r1 - a1.astype(F32)).astype(BF16)
    return _dot(a0, b_bf16) + _dot(a1, b_bf16) + _dot(a2, b_bf16)


def _cmp_topk_kernel(q_ref, kc_ref, vc_ref, gates_ref, ocmp_ref, selt_ref, flags_ref, *, ncp):
    start = pl.program_id(1) * Q_BLOCK
    t_col = start + _iota((Q_BLOCK, 1), 0)
    cmp_end = CMP_STRIDE * _iota((1, ncp), 1) + (CMP_BLOCK - 1)
    valid = cmp_end <= t_col
    dist = (t_col - cmp_end).astype(F32)
    kc = kc_ref[0, 0]
    vc = vc_ref[0, 0]
    ci = _iota((ncp, N_BLK_PAD), 0) * CMP_STRIDE
    sj0 = _iota((ncp, N_BLK_PAD), 1) * SEL_BLOCK
    overlap = jnp.where((ci < sj0 + SEL_BLOCK) & (ci + (CMP_BLOCK - 1) >= sj0), 1.0, 0.0).astype(BF16)
    blk = _iota((Q_BLOCK, N_BLK_PAD), 1)
    blk_f = blk.astype(F32)
    cur = lax.shift_right_logical(t_col, 6)
    causal = blk <= cur
    forced = (blk == 0) | (blk == cur) | (blk == cur - 1)
    gates = gates_ref[0]
    flags_ref[...] = jnp.zeros(flags_ref.shape, jnp.int32)
    per_head = []
    for g in range(N_GROUPS):
        s = _dot_nt(_stack_group_queries(q_ref, g), kc)
        psum = jnp.zeros((Q_BLOCK, ncp), F32)
        for j in range(HPG):
            h = HPG * g + j
            sj = s[Q_BLOCK * j:Q_BLOCK * (j + 1)] - SLOPES[h] * dist
            sj = jnp.where(valid, sj, NEG)
            m = jnp.max(sj, axis=-1, keepdims=True)
            p = jnp.where(valid, jnp.exp(sj - m), 0.0)
            pn = p / jnp.maximum(jnp.sum(p, axis=-1, keepdims=True), 1e-30)
            psum = psum + pn
            per_head.append(gates[:, 3 * h:3 * h + 1] * _dot(pn.astype(BF16), vc))
        imp = _dot_f32_lhs(psum, overlap)
        rank = jnp.where(causal, jnp.where(forced, FORCE_SCORE, imp), NEG)
        sel = jnp.zeros((Q_BLOCK, N_BLK_PAD), F32)
        for _ in range(N_SELECT):
            m = jnp.max(rank, axis=-1, keepdims=True)
            idx = jnp.min(jnp.where(rank == m, blk_f, float(N_BLK_PAD)), axis=-1, keepdims=True)
            chosen = blk_f == idx
            sel = jnp.where(chosen, 1.0, sel)
            rank = jnp.where(chosen, REMOVED, rank)
        sel = jnp.where(causal, sel, 0.0)
        selt_ref[0, g] = sel.T.astype(BF16)
        flags_ref[0, 0, g:g + 1, :] = jnp.max(sel, axis=0, keepdims=True).astype(jnp.int32)
    ocmp_ref[0] = _place_heads(per_head)


def _cmp_topk(q3, kvcmp, gates3):
    bsz, t, _ = q3.shape
    ncp = t // CMP_STRIDE
    nqb = t // Q_BLOCK
    return pl.pallas_call(
        functools.partial(_cmp_topk_kernel, ncp=ncp),
        grid=(bsz, nqb),
        in_specs=[pl.BlockSpec((1, Q_BLOCK, N_HEADS * LANES), lambda b, i: (b, i, 0)),
                  pl.BlockSpec((1, 1, ncp, LANES), lambda b, i: (b, 0, 0, 0)),
                  pl.BlockSpec((1, 1, ncp, LANES), lambda b, i: (b, 1, 0, 0)),
                  pl.BlockSpec((1, Q_BLOCK, LANES), lambda b, i: (b, i, 0))],
        out_specs=[pl.BlockSpec((1, Q_BLOCK, NSA_W), lambda b, i: (b, i, 0)),
                   pl.BlockSpec((1, N_GROUPS, N_BLK_PAD, Q_BLOCK), lambda b, i: (b, 0, 0, i)),
                   pl.BlockSpec((1, 1, 8, N_BLK_PAD), lambda b, i: (b, i, 0, 0))],
        out_shape=[jax.ShapeDtypeStruct((bsz, t, NSA_W), F32),
                   jax.ShapeDtypeStruct((bsz, N_GROUPS, N_BLK_PAD, t), BF16),
                   jax.ShapeDtypeStruct((bsz, nqb, 8, N_BLK_PAD), jnp.int32)],
        compiler_params=pltpu.CompilerParams(dimension_semantics=("arbitrary", "arbitrary"),
                                             vmem_limit_bytes=VMEM_LIMIT),
        name="cmp_topk",
    )(q3, kvcmp, kvcmp, gates3)


def _slc_win_kernel(counts_ref, lists_ref, qt_ref, kaug_ref, vt_ref, oh_ref, selt_ref, gatest_ref,
                    ocmp_ref, out_ref, qaug_ref, m_ref, acc_ref, *, nqb):
    b = pl.program_id(0)
    qb = pl.program_id(1)
    ki = _iota((KEY_CHUNK, Q_BLOCK), 0)
    qi = _iota((KEY_CHUNK, Q_BLOCK), 1)
    tile_heads = lambda a: jnp.concatenate([a] * HPG, axis=1)
    diag_bias = tile_heads(jnp.where(ki <= qi, 0.0, NEG))
    band_bias = tile_heads(jnp.where(ki > qi, 0.0, NEG))
    gt = gatest_ref[0]
    feat_row = _iota((HEAD_DIM, HPG * Q_BLOCK), 0) == 0
    step_id = b * nqb + qb
    n_win = WINDOW // KEY_CHUNK + 1
    slope_rows = [jnp.concatenate([jnp.full((1, Q_BLOCK), SLOPES[HPG * g + j], F32) for j in range(HPG)],
                                  axis=1) for g in range(N_GROUPS)]

    def normalize(acc):
        return acc[0:HEAD_DIM] / jnp.maximum(acc[HEAD_DIM:HEAD_DIM + 1], 1e-30)

    def chunk_rows(c):
        return pl.ds(pl.multiple_of(c * KEY_CHUNK, KEY_CHUNK), KEY_CHUNK)

    def slc_scores(g, c):
        k = jnp.concatenate([kaug_ref[0, g, chunk_rows(c), :], oh_ref[chunk_rows(c), :]], axis=1)
        return _dot(k, qaug_ref[g])

    o_win = []
    for g in range(N_GROUPS):
        qaug_ref[g, 0:HEAD_DIM, :] = jnp.concatenate(
            [qt_ref[0, HEAD_DIM * (HPG * g + j):HEAD_DIM * (HPG * g + j + 1), :] for j in range(HPG)],
            axis=1)
        qaug_ref[g, HEAD_DIM:LANES, :] = jnp.where(feat_row, slope_rows[g], 0.0).astype(BF16)
        sel_bias = ((selt_ref[0, g].astype(F32) - 1.0) * MASK_BIG).astype(BF16)
        qaug_ref[g, LANES:2 * LANES, :] = tile_heads(sel_bias)

        a = N_GROUPS + g
        sts, shifts, cs = [], [], []
        for d in range(n_win):
            c = jnp.maximum(qb - d, 0)
            st = _dot(kaug_ref[0, a, chunk_rows(c), :], qaug_ref[g, 0:LANES, :])
            if d == 0:
                st = st + diag_bias
            elif d == n_win - 1:
                st = st + band_bias
            delta = slope_rows[g] * float(d * KEY_CHUNK)
            if d > 0:
                delta = delta + jnp.where(qb - d >= 0, 0.0, MASK_BIG)
            sts.append(st)
            shifts.append(delta)
            cs.append(c)
        m = functools.reduce(jnp.maximum, [jnp.max(st, axis=0, keepdims=True) - dl
                                           for st, dl in zip(sts, shifts)])
        acc = sum(_dot(vt_ref[0, a, :, chunk_rows(c)], jnp.exp(st - (m + dl)).astype(BF16))
                  for st, dl, c in zip(sts, shifts, cs))
        o_win.append(normalize(acc))

        st = slc_scores(g, qb) + diag_bias
        m = jnp.max(st, axis=0, keepdims=True)
        m_ref[g] = m
        acc_ref[g] = _dot(vt_ref[0, g, :, chunk_rows(qb)], jnp.exp(st - m).astype(BF16))

    def slc_body(i, carry):
        for g in range(N_GROUPS):
            word = lists_ref[(step_id * N_GROUPS + g) * LIST_WORDS + i]
            sts, shifts, cs = [], [], []
            for u in range(SLC_BATCH):
                cid = lax.shift_right_logical(word, 8 * u) & 255
                valid = cid < N_BLK_PAD // 2
                c = jnp.where(valid, cid, 0)
                sts.append(slc_scores(g, c))
                shifts.append(slope_rows[g] * ((qb - c) * KEY_CHUNK).astype(F32)
                              + jnp.where(valid, 0.0, MASK_BIG))
                cs.append(c)
            m_old = m_ref[g]
            m_new = functools.reduce(jnp.maximum, [m_old] + [jnp.max(st, axis=0, keepdims=True) - dl
                                                             for st, dl in zip(sts, shifts)])
            pv = sum(_dot(vt_ref[0, g, :, chunk_rows(c)], jnp.exp(st - (m_new + dl)).astype(BF16))
                     for st, dl, c in zip(sts, shifts, cs))
            acc_ref[g] = jnp.exp(m_old - m_new) * acc_ref[g] + pv
            m_ref[g] = m_new
        return carry

    lax.fori_loop(0, counts_ref[step_id], slc_body, 0)

    per_head = []
    for g in range(N_GROUPS):
        o_slc = normalize(acc_ref[g])
        for j in range(HPG):
            h = HPG * g + j
            lanes = slice(Q_BLOCK * j, Q_BLOCK * (j + 1))
            per_head.append(gt[3 * h + 1:3 * h + 2, :] * o_slc[:, lanes]
                            + gt[3 * h + 2:3 * h + 3, :] * o_win[g][:, lanes])
    o_t = jnp.concatenate(per_head, axis=0)
    out_ref[0] = (ocmp_ref[0] + o_t.T).astype(BF16)


def _slc_win(counts, lists, qt, kaug, vt, onehot, selt, gatest, ocmp):
    bsz, _, t = qt.shape
    nqb = t // Q_BLOCK
    grid_spec = pltpu.PrefetchScalarGridSpec(
        num_scalar_prefetch=2,
        grid=(bsz, nqb),
        in_specs=[pl.BlockSpec((1, NSA_W, Q_BLOCK), lambda b, i, *_: (b, 0, i)),
                  pl.BlockSpec((1, 2 * N_GROUPS, t, LANES), lambda b, i, *_: (b, 0, 0, 0)),
                  pl.BlockSpec((1, 2 * N_GROUPS, VT_ROWS, t), lambda b, i, *_: (b, 0, 0, 0)),
                  pl.BlockSpec((t, N_BLK_PAD), lambda b, i, *_: (0, 0)),
                  pl.BlockSpec((1, N_GROUPS, N_BLK_PAD, Q_BLOCK), lambda b, i, *_: (b, 0, 0, i)),
                  pl.BlockSpec((1, GATET_ROWS, Q_BLOCK), lambda b, i, *_: (b, 0, i)),
                  pl.BlockSpec((1, Q_BLOCK, NSA_W), lambda b, i, *_: (b, i, 0))],
        out_specs=pl.BlockSpec((1, Q_BLOCK, NSA_W), lambda b, i, *_: (b, i, 0)),
        scratch_shapes=[pltpu.VMEM((N_GROUPS, 2 * LANES, HPG * Q_BLOCK), BF16),
                        pltpu.VMEM((N_GROUPS, 1, HPG * Q_BLOCK), F32),
                        pltpu.VMEM((N_GROUPS, VT_ROWS, HPG * Q_BLOCK), F32)],
    )
    return pl.pallas_call(
        functools.partial(_slc_win_kernel, nqb=nqb),
        grid_spec=grid_spec,
        out_shape=jax.ShapeDtypeStruct((bsz, t, NSA_W), BF16),
        compiler_params=pltpu.CompilerParams(dimension_semantics=("arbitrary", "arbitrary"),
                                             vmem_limit_bytes=VMEM_LIMIT),
        name="slc_win",
    )(counts, lists, qt, kaug, vt, onehot, selt, gatest, ocmp)


def _merge_kernel(x_ref, onsa_ref, u_ref, vn_ref, g_ref, wm_ref, bm_ref, ws_ref, bs_ref,
                  wpa_ref, wpb_ref, wo_ref, h_ref, *, tm):
    x = x_ref[...]
    xn = _rms(x, g_ref[...]).astype(BF16)
    tril = _iota((GMLP_CHUNK, GMLP_CHUNK), 0) >= _iota((GMLP_CHUNK, GMLP_CHUNK), 1)
    sgu_rows = []
    for c in range(tm // GMLP_CHUNK):
        rows = slice(GMLP_CHUNK * c, GMLP_CHUNK * (c + 1))
        cols = []
        for g in range(GMLP_GROUPS):
            lanes = slice(LANES * g, LANES * (g + 1))
            w = jnp.where(tril, ws_ref[g], 0.0).astype(BF16)
            cols.append(_dot(w, vn_ref[rows, lanes]) + bs_ref[:, g:g + 1])
        sgu_rows.append(u_ref[rows, :] * jnp.concatenate(cols, axis=1))
    o_sgu = jnp.concatenate(sgu_rows, axis=0).astype(BF16)
    mg = jax.nn.sigmoid(_dot(xn, wm_ref[...]) + bm_ref[...])
    mixed = (mg[:, :D_MODEL] * _dot(onsa_ref[...], wpa_ref[...])
             + mg[:, D_MODEL:] * _dot(o_sgu, wpb_ref[...]))
    h_ref[...] = x + _dot(mixed.astype(BF16), wo_ref[...])


def _merge(x2, onsa2, u2, vn2, norm_g, wm, bm, ws, bs_t, wpa, wpb, wo, tm):
    n = x2.shape[0]
    row = lambda w: pl.BlockSpec((tm, w), lambda i: (i, 0))
    full = lambda a: pl.BlockSpec(a.shape, lambda i: (0,) * a.ndim)
    return pl.pallas_call(
        functools.partial(_merge_kernel, tm=tm),
        grid=(n // tm,),
        in_specs=[row(D_MODEL), row(NSA_W), row(GMLP_WIDTH), row(GMLP_WIDTH), full(norm_g),
                  full(wm), full(bm), full(ws), full(bs_t), full(wpa), full(wpb), full(wo)],
        out_specs=row(D_MODEL),
        out_shape=jax.ShapeDtypeStruct((n, D_MODEL), F32),
        compiler_params=pltpu.CompilerParams(dimension_semantics=("arbitrary",),
                                             vmem_limit_bytes=VMEM_LIMIT),
        name="merge",
    )(x2, onsa2, u2, vn2, norm_g, wm, bm, ws, bs_t, wpa, wpb, wo)


def _memkv_kernel(mem_ref, g_ref, w_ref, out_ref):
    out_ref[0] = _dot(_rms(mem_ref[0], g_ref[...]).astype(BF16), w_ref[...]).astype(BF16)


def _memkv(mem, norm_g, w):
    bsz, nm, _ = mem.shape
    return pl.pallas_call(
        _memkv_kernel,
        grid=(bsz,),
        in_specs=[pl.BlockSpec((1, nm, D_MODEL), lambda b: (b, 0, 0)),
                  pl.BlockSpec(norm_g.shape, lambda b: (0, 0)),
                  pl.BlockSpec(w.shape, lambda b: (0, 0))],
        out_specs=pl.BlockSpec((1, nm, 2 * MEM_W), lambda b: (b, 0, 0)),
        out_shape=jax.ShapeDtypeStruct((bsz, nm, 2 * MEM_W), BF16),
        compiler_params=pltpu.CompilerParams(dimension_semantics=("arbitrary",),
                                             vmem_limit_bytes=VMEM_LIMIT),
        name="memkv",
    )(mem, norm_g, w)


def _xattn_kernel(h_ref, g_ref, wq_ref, mkv_ref, wo_ref, out_ref):
    h = h_ref[...]
    hq = _dot(_rms(h, g_ref[...]).astype(BF16), wq_ref[...]).astype(BF16)
    heads = []
    for a in range(MEM_HEADS):
        lanes = slice(MEM_HEAD_DIM * a, MEM_HEAD_DIM * (a + 1))
        k = mkv_ref[0, :, lanes]
        v = mkv_ref[0, :, MEM_W + MEM_HEAD_DIM * a:MEM_W + MEM_HEAD_DIM * (a + 1)]
        s = _dot_nt(hq[:, lanes], k) * (MEM_HEAD_DIM ** -0.5)
        e = jnp.exp(s - jnp.max(s, axis=-1, keepdims=True))
        p = e / jnp.sum(e, axis=-1, keepdims=True)
        heads.append(_dot(p.astype(BF16), v))
    o = jnp.concatenate(heads, axis=1).astype(BF16)
    out_ref[...] = h + _dot(o, wo_ref[...])


def _xattn(h2d, norm_g, wq, mkv, wo, tm, rows_per_batch):
    n = h2d.shape[0]
    nm = mkv.shape[1]
    tiles_per_batch = rows_per_batch // tm
    full = lambda a: pl.BlockSpec(a.shape, lambda i: (0,) * a.ndim)
    return pl.pallas_call(
        _xattn_kernel,
        grid=(n // tm,),
        in_specs=[pl.BlockSpec((tm, D_MODEL), lambda i: (i, 0)), full(norm_g), full(wq),
                  pl.BlockSpec((1, nm, 2 * MEM_W), lambda i: (i // tiles_per_batch, 0, 0)),
                  full(wo)],
        out_specs=pl.BlockSpec((tm, D_MODEL), lambda i: (i, 0)),
        out_shape=jax.ShapeDtypeStruct((n, D_MODEL), F32),
        compiler_params=pltpu.CompilerParams(dimension_semantics=("arbitrary",),
                                             vmem_limit_bytes=VMEM_LIMIT),
        name="xattn",
    )(h2d, norm_g, wq, mkv, wo)


def _ffn_kernel(h_ref, g_ref, wgu_ref, wd_ref, gf_ref, out_ref, *, d_ff):
    h = h_ref[...]
    hn = _rms(h, g_ref[...]).astype(BF16)
    gate = _dot(hn, wgu_ref[:, :d_ff])
    up = _dot(hn, wgu_ref[:, d_ff:])
    act = (jax.nn.silu(gate) * up).astype(BF16)
    y = h + _dot(act, wd_ref[...])
    out_ref[...] = _rms(y, gf_ref[...])


def _ffn(h2d, norm_g, wgu, wd, norm_f, tm):
    n = h2d.shape[0]
    d_ff = wd.shape[0]
    full = lambda a: pl.BlockSpec(a.shape, lambda i: (0,) * a.ndim)
    once = lambda a: pl.BlockSpec(a.shape, lambda i: (0,) * a.ndim, pipeline_mode=pl.Buffered(1))
    return pl.pallas_call(
        functools.partial(_ffn_kernel, d_ff=d_ff),
        grid=(n // tm,),
        in_specs=[pl.BlockSpec((tm, D_MODEL), lambda i: (i, 0)), full(norm_g), once(wgu), once(wd),
                  full(norm_f)],
        out_specs=pl.BlockSpec((tm, D_MODEL), lambda i: (i, 0)),
        out_shape=jax.ShapeDtypeStruct((n, D_MODEL), F32),
        compiler_params=pltpu.CompilerParams(dimension_semantics=("arbitrary",),
                                             vmem_limit_bytes=VMEM_LIMIT),
        name="ffn",
    )(h2d, norm_g, wgu, wd, norm_f)


def _pad_q_weight(wq):
    d = wq.shape[0]
    w = wq.reshape(d, N_HEADS, HEAD_DIM)
    z = jnp.zeros_like(w)
    lo = jnp.concatenate([w, z], axis=-1)
    hi = jnp.concatenate([z, w], axis=-1)
    in_g0 = (jnp.arange(N_HEADS) < HPG)[None, :, None]
    return jnp.where(in_g0, lo, hi).reshape(d, N_HEADS * LANES)


def _block_diag2(w):
    z = jnp.zeros_like(w)
    return jnp.concatenate([jnp.concatenate([w, z], axis=-1), jnp.concatenate([z, w], axis=-1)], axis=-2)


def _chunk_lists(flags):
    bsz, nqb = flags.shape[:2]
    n_chunks = N_BLK_PAD // 2
    f = flags[:, :, :N_GROUPS, :].reshape(bsz, nqb, N_GROUPS, n_chunks, 2).max(axis=-1)
    cid = jnp.arange(n_chunks, dtype=jnp.int32)
    own = jnp.arange(nqb, dtype=jnp.int32)[None, :, None, None]
    active = (f > 0) & (cid < own)
    ids = jnp.sort(jnp.where(active, cid, 255), axis=-1)
    n_batches = (active.sum(axis=-1).max(axis=-1) + SLC_BATCH - 1) // SLC_BATCH
    packed = ids.reshape(bsz, nqb, N_GROUPS, LIST_WORDS, SLC_BATCH)
    words = functools.reduce(jnp.bitwise_or, [packed[..., u] << (8 * u) for u in range(SLC_BATCH)])
    return n_batches.reshape(-1).astype(jnp.int32), words.reshape(-1).astype(jnp.int32)


def kernel(x, mem, norm_mix, w_in, w_cmp_k1, w_cmp_k2, w_cmp_v1, w_cmp_v2, pe_cmp_k, pe_cmp_v, ln_sgu, w_spatial, b_spatial, w_proj_a, w_proj_b, w_merge, b_merge, w_out, norm_mem_q, norm_mem_kv, w_mq, w_mkv, w_mo, norm_ffn, w_gate_up, w_down, norm_final):
    bsz, t, d = x.shape
    depth = norm_mix.shape[0]
    assert d == D_MODEL and t % Q_BLOCK == 0 and t // SEL_BLOCK <= N_BLK_PAD
    assert t // SEL_BLOCK >= N_SELECT and depth == 1 and t % 256 == 0
    n = bsz * t
    tm = 256
    h = x.reshape(n, d)
    c0, c1, c2, c3 = NSA_W, NSA_W + KV_W, NSA_W + 3 * KV_W, NSA_W + 3 * KV_W + 2 * GMLP_WIDTH
    onehot = (jnp.arange(t)[:, None] // SEL_BLOCK == jnp.arange(N_BLK_PAD)[None, :]).astype(BF16)
    pad_cols = lambda w, width: jnp.pad(w, ((0, 0), (0, width - w.shape[1])))
    for l in range(depth):
        wi = w_in[l]
        wq = _pad_q_weight(wi[:, :c0]).astype(BF16)
        wqt = wi[:, :c0].T.astype(BF16)
        wkvc = wi[:, c0:c1].astype(BF16)
        wk, wvt = [], []
        for branch in range(2):
            base = c1 + KV_W * branch
            for g in range(N_GROUPS):
                wk.append(pad_cols(wi[:, base + HEAD_DIM * g:base + HEAD_DIM * (g + 1)], LANES))
                v0 = base + N_GROUPS * HEAD_DIM + HEAD_DIM * g
                wvt.append(pad_cols(wi[:, v0:v0 + HEAD_DIM], VT_ROWS).T)
        wk = jnp.stack(wk).astype(BF16)
        wvt = jnp.stack(wvt).astype(BF16)
        wuv = wi[:, c2:c3].astype(BF16)
        wgate = pad_cols(wi[:, c3:], LANES).astype(BF16)
        wgatet = pad_cols(wi[:, c3:], GATET_ROWS).T.astype(BF16)
        q2, qt, kvc2, kaug, vt, u2, vn2, gates2, gatest = _inproj(
            h, norm_mix[l][None], ln_sgu[l][None], wq, wqt, wkvc, wk, wvt, wuv, wgate, wgatet,
            tm, bsz, t)

        pe2 = jnp.stack([pe_cmp_k[l], pe_cmp_v[l]])
        pe2 = jnp.concatenate([pe2, pe2], axis=-1)
        w1 = jnp.stack([w_cmp_k1[l], w_cmp_v1[l]]).reshape(2, CMP_BLOCK, HEAD_DIM, CMP_HIDDEN)
        w1bd = _block_diag2(w1).astype(BF16)
        w2bd = _block_diag2(jnp.stack([w_cmp_k2[l], w_cmp_v2[l]])).astype(BF16)
        kvcmp = _compress(kvc2.reshape(bsz, t, KV_W), pe2, w1bd, w2bd)

        q3 = q2.reshape(bsz, t, N_HEADS * LANES)
        gates3 = gates2.reshape(bsz, t, LANES)
        ocmp, selt, flags = _cmp_topk(q3, kvcmp, gates3)
        counts, lists = _chunk_lists(flags)
        onsa = _slc_win(counts, lists, qt, kaug, vt, onehot, selt, gatest, ocmp)

        h = _merge(h, onsa.reshape(n, NSA_W), u2, vn2, norm_mix[l][None], w_merge[l].astype(BF16),
                   b_merge[l][None], w_spatial[l], b_spatial[l].T, w_proj_a[l].astype(BF16),
                   w_proj_b[l].astype(BF16), w_out[l].astype(BF16), tm)

        mkv = _memkv(mem, norm_mem_kv[l][None], w_mkv[l].astype(BF16))
        h = _xattn(h, norm_mem_q[l][None], w_mq[l].astype(BF16), mkv, w_mo[l].astype(BF16), tm, t)
        h = _ffn(h, norm_ffn[l][None], w_gate_up[l].astype(BF16), w_down[l].astype(BF16),
                 norm_final[None], tm)
    return h.reshape(bsz, t, d)
```

```python
import functools

import jax
import jax.numpy as jnp
from jax import lax
from jax.experimental import pallas as pl
from jax.experimental.pallas import tpu as pltpu

F32 = jnp.float32
BF16 = jnp.bfloat16

LANES = 128
D_MODEL = 1024
N_HEADS = 8
HEAD_DIM = 64
N_GROUPS = 2
HPG = N_HEADS // N_GROUPS
CMP_BLOCK = 32
CMP_STRIDE = 16
CMP_HIDDEN = 128
SEL_BLOCK = 64
N_SELECT = 16
WINDOW = 512
Q_BLOCK = 128
KEY_CHUNK = 128
N_BLK_PAD = 128
NSA_W = N_HEADS * HEAD_DIM
KV_W = 2 * N_GROUPS * HEAD_DIM
GMLP_WIDTH = 512
GMLP_GROUPS = 4
GMLP_CHUNK = 128
MEM_HEADS = 4
MEM_HEAD_DIM = 128
MEM_W = MEM_HEADS * MEM_HEAD_DIM
GATE_W = 3 * N_HEADS
SLC_BATCH = 4
LIST_WORDS = (N_BLK_PAD // 2) // SLC_BATCH
VT_ROWS = 80
VCT_ROWS = VT_ROWS + N_BLK_PAD
CMP_KEY_CHUNK = 128
N_FORCED = 3
GATET_ROWS = 32
MASK_BIG = 1e30
EPS = 1e-6
NEG = -1e30
REMOVED = -3e38
FORCE_SCORE = 1e6
SLOPES = tuple(2.0 ** (-8.0 * (h + 1) / N_HEADS) for h in range(N_HEADS))
VMEM_LIMIT = 56 * 1024 * 1024


def _dot(a, b):
    return jnp.dot(a, b, preferred_element_type=F32)


def _dot_nt(a, b):
    return lax.dot_general(a, b, (((1,), (1,)), ((), ())), preferred_element_type=F32)


def _rms(x, g):
    return x * lax.rsqrt(jnp.mean(x * x, axis=-1, keepdims=True) + EPS) * g


def _iota(shape, dim):
    return lax.broadcasted_iota(jnp.int32, shape, dim)


def _inproj_kernel(x_ref, g_ref, lng_ref, wqt_ref, wkvc_ref, wk_ref, wvt_ref, wuv_ref, wgatet_ref,
                   qt_ref, kvc_ref, kaug_ref, vt_ref, u_ref, vn_ref, gatest_ref, *, tm):
    xn = _rms(x_ref[...], g_ref[...]).astype(BF16)
    qt_ref[0] = (_dot_nt(wqt_ref[...], xn) * (HEAD_DIM ** -0.5)).astype(BF16)
    kvc_ref[...] = _dot(xn, wkvc_ref[...])
    lane = _iota((tm, LANES), 1)
    key_feat = jnp.where(lane == HEAD_DIM, _iota((tm, LANES), 0) & (KEY_CHUNK - 1), 0).astype(F32)
    ones_row = jnp.where(_iota((VT_ROWS, tm), 0) == HEAD_DIM, 1.0, 0.0)
    for a in range(2 * N_GROUPS):
        kaug_ref[0, a] = (_dot(xn, wk_ref[a]) + key_feat).astype(BF16)
        vt_ref[0, a] = (_dot_nt(wvt_ref[a], xn) + ones_row).astype(BF16)
    uv = jax.nn.gelu(_dot(xn, wuv_ref[...]))
    u_ref[...] = uv[:, :GMLP_WIDTH]
    v = uv[:, GMLP_WIDTH:]
    vc = v - jnp.mean(v, axis=-1, keepdims=True)
    vn = vc * lax.rsqrt(jnp.mean(vc * vc, axis=-1, keepdims=True) + EPS) * lng_ref[...]
    vn_ref[...] = vn.astype(BF16)
    gatest_ref[0] = jax.nn.sigmoid(_dot_nt(wgatet_ref[...], xn))


def _inproj(x2, norm_g, ln_g, wqt, wkvc, wk, wvt, wuv, wgatet, tm, bsz, t):
    n = x2.shape[0]
    tpb = t // tm
    row = lambda w: pl.BlockSpec((tm, w), lambda i: (i, 0))
    full = lambda a: pl.BlockSpec(a.shape, lambda i: (0,) * a.ndim)
    return pl.pallas_call(
        functools.partial(_inproj_kernel, tm=tm),
        grid=(n // tm,),
        in_specs=[row(D_MODEL), full(norm_g), full(ln_g), full(wqt), full(wkvc), full(wk),
                  full(wvt), full(wuv), full(wgatet)],
        out_specs=[pl.BlockSpec((1, NSA_W, tm), lambda i: (i // tpb, 0, i % tpb)),
                   row(KV_W),
                   pl.BlockSpec((1, 2 * N_GROUPS, tm, LANES), lambda i: (i // tpb, 0, i % tpb, 0)),
                   pl.BlockSpec((1, 2 * N_GROUPS, VT_ROWS, tm), lambda i: (i // tpb, 0, 0, i % tpb)),
                   row(GMLP_WIDTH), row(GMLP_WIDTH),
                   pl.BlockSpec((1, GATET_ROWS, tm), lambda i: (i // tpb, 0, i % tpb))],
        out_shape=[jax.ShapeDtypeStruct((bsz, NSA_W, t), BF16),
                   jax.ShapeDtypeStruct((n, KV_W), F32),
                   jax.ShapeDtypeStruct((bsz, 2 * N_GROUPS, t, LANES), BF16),
                   jax.ShapeDtypeStruct((bsz, 2 * N_GROUPS, VT_ROWS, t), BF16),
                   jax.ShapeDtypeStruct((n, GMLP_WIDTH), F32),
                   jax.ShapeDtypeStruct((n, GMLP_WIDTH), BF16),
                   jax.ShapeDtypeStruct((bsz, GATET_ROWS, t), F32)],
        compiler_params=pltpu.CompilerParams(dimension_semantics=("arbitrary",),
                                             vmem_limit_bytes=VMEM_LIMIT),
        name="inproj",
    )(x2, norm_g, ln_g, wqt, wkvc, wk, wvt, wuv, wgatet)


def _compress_kernel(xk_ref, xv_ref, pe_ref, w1_ref, w2k_ref, w2vt_ref, kc_ref, vct_ref, *, nc):
    outs = []
    for j, x_ref in enumerate((xk_ref, xv_ref)):
        a = jnp.zeros((nc, 2 * CMP_HIDDEN), F32)
        b = jnp.zeros((nc, 2 * CMP_HIDDEN), F32)
        for t in range(CMP_STRIDE):
            xt = x_ref[0, pl.ds(t, nc, stride=CMP_STRIDE), :]
            a = a + _dot((xt + pe_ref[j, t:t + 1, :]).astype(BF16), w1_ref[j, t])
            b = b + _dot((xt + pe_ref[j, CMP_STRIDE + t:CMP_STRIDE + t + 1, :]).astype(BF16),
                         w1_ref[j, CMP_STRIDE + t])
        outs.append(jax.nn.gelu(a + pltpu.roll(b, nc - 1, 0)).astype(BF16))
    k2 = _dot(outs[0], w2k_ref[...])
    v_t = _dot_nt(w2vt_ref[...], outs[1])
    lane = _iota((nc, LANES), 1)
    key_feat = jnp.where(
        lane == HEAD_DIM, CMP_STRIDE * (_iota((nc, LANES), 0) & (CMP_KEY_CHUNK - 1)), 0).astype(F32)
    ci =_iota((N_BLK_PAD, nc), 1) * CMP_STRIDE
    sj = _iota((N_BLK_PAD, nc), 0) * SEL_BLOCK
    overlap_t = jnp.where((ci < sj + SEL_BLOCK) & (ci + (CMP_BLOCK - 1) >= sj), 1.0, 0.0).astype(BF16)
    ones_rows = jnp.where(_iota((VT_ROWS - HEAD_DIM, nc), 0) == 0, 1.0, 0.0).astype(BF16)
    for g in range(N_GROUPS):
        kg = k2 if g == 0 else pltpu.roll(k2, HEAD_DIM, 1)
        kc_ref[0, g] = jnp.where(lane < HEAD_DIM, kg, key_feat).astype(BF16)
        vct_ref[0, g, 0:HEAD_DIM, :] = v_t[HEAD_DIM * g:HEAD_DIM * (g + 1), :].astype(BF16)
        vct_ref[0, g, HEAD_DIM:VT_ROWS, :] = ones_rows
        vct_ref[0, g, VT_ROWS:VCT_ROWS, :] = overlap_t


def _compress(kvc3, pe2, w1bd, w2k, w2vt):
    bsz, t, _ = kvc3.shape
    nc = t // CMP_STRIDE
    full = lambda a: pl.BlockSpec(a.shape, lambda b: (0,) * a.ndim)
    return pl.pallas_call(
        functools.partial(_compress_kernel, nc=nc),
        grid=(bsz,),
        in_specs=[pl.BlockSpec((1, t, LANES), lambda b: (b, 0, 0)),
                  pl.BlockSpec((1, t, LANES), lambda b: (b, 0, 1)),
                  full(pe2), full(w1bd), full(w2k), full(w2vt)],
        out_specs=[pl.BlockSpec((1, N_GROUPS, nc, LANES), lambda b: (b, 0, 0, 0)),
                   pl.BlockSpec((1, N_GROUPS, VCT_ROWS, nc), lambda b: (b, 0, 0, 0))],
        out_shape=[jax.ShapeDtypeStruct((bsz, N_GROUPS, nc, LANES), BF16),
                   jax.ShapeDtypeStruct((bsz, N_GROUPS, VCT_ROWS, nc), BF16)],
        compiler_params=pltpu.CompilerParams(dimension_semantics=("arbitrary",),
                                             vmem_limit_bytes=VMEM_LIMIT),
        name="compress",
    )(kvc3, kvc3, pe2, w1bd, w2k, w2vt)


def _cmp_topk_kernel(qt_ref, kc_ref, vct_ref, gatest_ref, ocmp_ref, selt_ref, flags_ref,
                     m_ref, acc_ref, *, ncp):
    qb = pl.program_id(1)
    start = qb * Q_BLOCK
    n_chunks = ncp // CMP_KEY_CHUNK
    nck = (qb * (Q_BLOCK // CMP_STRIDE) + (Q_BLOCK - CMP_BLOCK) // CMP_STRIDE) // CMP_KEY_CHUNK + 1
    chunk_tokens = CMP_KEY_CHUNK * CMP_STRIDE
    tile_heads = lambda a: jnp.concatenate([a] * HPG, axis=1)
    ki = _iota((CMP_KEY_CHUNK, Q_BLOCK), 0)
    qi = _iota((CMP_KEY_CHUNK, Q_BLOCK), 1)
    feat_row = _iota((LANES - HEAD_DIM, HPG * Q_BLOCK), 0) == 0
    blk = _iota((N_BLK_PAD, Q_BLOCK), 0)
    blk_f = blk.astype(F32)
    t_row = start + _iota((1, Q_BLOCK), 1)
    cur = lax.shift_right_logical(t_row, 6)
    causal = blk <= cur
    forced = (blk == 0) | (blk == cur) | (blk == cur - 1)
    has_key = t_row >= CMP_BLOCK - 1
    gt = gatest_ref[0]
    ones8 = jnp.ones((8, Q_BLOCK), F32)

    def chunk_rows(c):
        return pl.ds(pl.multiple_of(c * CMP_KEY_CHUNK, CMP_KEY_CHUNK), CMP_KEY_CHUNK)

    per_head = []
    for g in range(N_GROUPS):
        slope_row = jnp.concatenate(
            [jnp.full((1, Q_BLOCK), SLOPES[HPG * g + j], F32) for j in range(HPG)], axis=1)
        qa = jnp.concatenate(
            [jnp.concatenate([qt_ref[0, HEAD_DIM * (HPG * g + j):HEAD_DIM * (HPG * g + j + 1), :]
                              for j in range(HPG)], axis=1),
             jnp.where(feat_row, slope_row, 0.0).astype(BF16)], axis=0)

        def delta(c):
            return slope_row * (start - c * chunk_tokens).astype(F32)

        def weighted(sts, dls, cs, m):
            return sum(_dot(vct_ref[0, g, :, chunk_rows(c)], jnp.exp(st - (m + dl)).astype(BF16))
                       for st, dl, c in zip(sts, dls, cs))

        def col_max(sts, dls):
            return [jnp.max(st, axis=0, keepdims=True) - dl for st, dl in zip(sts, dls)]

        if n_chunks > 2:
            @pl.when(nck > 2)
            def _():
                cs = list(range(n_chunks - 2))
                sts = [_dot(kc_ref[0, g, chunk_rows(c), :], qa) for c in cs]
                dls = [delta(c) + jnp.where(c < nck - 2, 0.0, MASK_BIG) for c in cs]
                m1 = functools.reduce(jnp.maximum, col_max(sts, dls))
                m_ref[g] = m1
                acc_ref[g] = weighted(sts, dls, cs, m1)

            @pl.when(nck <= 2)
            def _():
                m_ref[g] = jnp.full(m_ref.shape[1:], NEG, F32)
                acc_ref[g] = jnp.zeros(acc_ref.shape[1:], F32)
            m_old = m_ref[g]
            acc_old = acc_ref[g]
        else:
            m_old = jnp.full((1, HPG * Q_BLOCK), NEG, F32)
            acc_old = jnp.zeros((VCT_ROWS, HPG * Q_BLOCK), F32)

        sts, dls, cs = [], [], []
        for u in (2, 1):
            c = jnp.maximum(nck - u, 0)
            key_end = CMP_STRIDE * (c * CMP_KEY_CHUNK + ki) + (CMP_BLOCK - 1) - start
            sts.append(_dot(kc_ref[0, g, chunk_rows(c), :], qa)
                       + tile_heads(jnp.where(key_end <= qi, 0.0, NEG)))
            dls.append(delta(c) + jnp.where(nck - u >= 0, 0.0, MASK_BIG))
            cs.append(c)
        m_new = functools.reduce(jnp.maximum, [m_old] + col_max(sts, dls))
        acc = jnp.exp(m_old - m_new) * acc_old + weighted(sts, dls, cs, m_new)

        inv_l = 1.0 / jnp.maximum(acc[HEAD_DIM:HEAD_DIM + 1], 1e-30)
        o = acc[0:HEAD_DIM] * inv_l
        imp_h = acc[VT_ROWS:VCT_ROWS] * inv_l
        imp = sum(imp_h[:, Q_BLOCK * j:Q_BLOCK * (j + 1)] for j in range(HPG))
        imp = jnp.where(has_key, imp, 0.0)
        for j in range(HPG):
            h = HPG * g + j
            per_head.append(jnp.where(has_key, gt[3 * h:3 * h + 1, :] * o[:, Q_BLOCK * j:Q_BLOCK * (j + 1)], 0.0))

        sel = jnp.where(forced, 1.0, 0.0)
        rank = jnp.where(causal, jnp.where(forced, REMOVED, imp), NEG)
        for _ in range(N_SELECT - N_FORCED):
            m = jnp.max(rank, axis=0, keepdims=True)
            idx = jnp.min(jnp.where(rank == m, blk_f, float(N_BLK_PAD)), axis=0, keepdims=True)
            chosen = blk_f == idx
            sel = jnp.where(chosen, 1.0, sel)
            rank = jnp.where(chosen, REMOVED, rank)
        sel = jnp.where(causal, sel, 0.0)
        selt_ref[0, g] = sel.astype(BF16)
        flags_ref[0, 0, g] = (_dot_nt(ones8, sel) > 0.5).astype(jnp.int32)
    ocmp_ref[0] = jnp.concatenate(per_head, axis=0)


def _cmp_topk(qt, kc, vct, gatest):
    bsz, _, t = qt.shape
    ncp = t // CMP_STRIDE
    nqb = t // Q_BLOCK
    return pl.pallas_call(
        functools.partial(_cmp_topk_kernel, ncp=ncp),
        grid=(bsz, nqb),
        in_specs=[pl.BlockSpec((1, NSA_W, Q_BLOCK), lambda b, i: (b, 0, i)),
                  pl.BlockSpec((1, N_GROUPS, ncp, LANES), lambda b, i: (b, 0, 0, 0)),
                  pl.BlockSpec((1, N_GROUPS, VCT_ROWS, ncp), lambda b, i: (b, 0, 0, 0)),
                  pl.BlockSpec((1, GATET_ROWS, Q_BLOCK), lambda b, i: (b, 0, i))],
        out_specs=[pl.BlockSpec((1, NSA_W, Q_BLOCK), lambda b, i: (b, 0, i)),
                   pl.BlockSpec((1, N_GROUPS, N_BLK_PAD, Q_BLOCK), lambda b, i: (b, 0, 0, i)),
                   pl.BlockSpec((1, 1, N_GROUPS, 8, N_BLK_PAD), lambda b, i: (b, i, 0, 0, 0))],
        out_shape=[jax.ShapeDtypeStruct((bsz, NSA_W, t), F32),
                   jax.ShapeDtypeStruct((bsz, N_GROUPS, N_BLK_PAD, t), BF16),
                   jax.ShapeDtypeStruct((bsz, nqb, N_GROUPS, 8, N_BLK_PAD), jnp.int32)],
        scratch_shapes=[pltpu.VMEM((N_GROUPS, 1, HPG * Q_BLOCK), F32),
                        pltpu.VMEM((N_GROUPS, VCT_ROWS, HPG * Q_BLOCK), F32)],
        compiler_params=pltpu.CompilerParams(dimension_semantics=("arbitrary", "arbitrary"),
                                             vmem_limit_bytes=VMEM_LIMIT),
        name="cmp_topk",
    )(qt, kc, vct, gatest)


def _slc_win_kernel(counts_ref, lists_ref, qt_ref, kaug_ref, vt_ref, oh_ref, selt_ref, gatest_ref,
                    ocmp_ref, out_ref, qaug_ref, m_ref, acc_ref, *, nqb):
    b = pl.program_id(0)
    qb = pl.program_id(1)
    ki = _iota((KEY_CHUNK, Q_BLOCK), 0)
    qi = _iota((KEY_CHUNK, Q_BLOCK), 1)
    tile_heads = lambda a: jnp.concatenate([a] * HPG, axis=1)
    diag_bias = tile_heads(jnp.where(ki <= qi, 0.0, NEG))
    band_bias = tile_heads(jnp.where(ki > qi, 0.0, NEG))
    gt = gatest_ref[0]
    feat_row = _iota((HEAD_DIM, HPG * Q_BLOCK), 0) == 0
    step_id = b * nqb + qb
    n_win = WINDOW // KEY_CHUNK + 1
    slope_rows = [jnp.concatenate([jnp.full((1, Q_BLOCK), SLOPES[HPG * g + j], F32) for j in range(HPG)],
                                  axis=1) for g in range(N_GROUPS)]

    def normalize(acc):
        return acc[0:HEAD_DIM] / jnp.maximum(acc[HEAD_DIM:HEAD_DIM + 1], 1e-30)

    def chunk_rows(c):
        return pl.ds(pl.multiple_of(c * KEY_CHUNK, KEY_CHUNK), KEY_CHUNK)

    def slc_scores(g, c):
        k = jnp.concatenate([kaug_ref[0, g, chunk_rows(c), :], oh_ref[chunk_rows(c), :]], axis=1)
        return _dot(k, qaug_ref[g])

    o_win = []
    for g in range(N_GROUPS):
        qaug_ref[g, 0:HEAD_DIM, :] = jnp.concatenate(
            [qt_ref[0, HEAD_DIM * (HPG * g + j):HEAD_DIM * (HPG * g + j + 1), :] for j in range(HPG)],
            axis=1)
        qaug_ref[g, HEAD_DIM:LANES, :] = jnp.where(feat_row, slope_rows[g], 0.0).astype(BF16)
        sel_bias = ((selt_ref[0, g].astype(F32) - 1.0) * MASK_BIG).astype(BF16)
        qaug_ref[g, LANES:2 * LANES, :] = tile_heads(sel_bias)

        a = N_GROUPS + g
        sts, shifts, cs = [], [], []
        for d in range(n_win):
            c = jnp.maximum(qb - d, 0)
            st = _dot(kaug_ref[0, a, chunk_rows(c), :], qaug_ref[g, 0:LANES, :])
            if d == 0:
                st = st + diag_bias
            elif d == n_win - 1:
                st = st + band_bias
            delta = slope_rows[g] * float(d * KEY_CHUNK)
            if d > 0:
                delta = delta + jnp.where(qb - d >= 0, 0.0, MASK_BIG)
            sts.append(st)
            shifts.append(delta)
            cs.append(c)
        m = functools.reduce(jnp.maximum, [jnp.max(st, axis=0, keepdims=True) - dl
                                           for st, dl in zip(sts, shifts)])
        acc = sum(_dot(vt_ref[0, a, :, chunk_rows(c)], jnp.exp(st - (m + dl)).astype(BF16))
                  for st, dl, c in zip(sts, shifts, cs))
        o_win.append(normalize(acc))

        st = slc_scores(g, qb) + diag_bias
        m = jnp.max(st, axis=0, keepdims=True)
        m_ref[g] = m
        acc_ref[g] = _dot(vt_ref[0, g, :, chunk_rows(qb)], jnp.exp(st - m).astype(BF16))

    def slc_body(i, carry):
        for g in range(N_GROUPS):
            word = lists_ref[(step_id * N_GROUPS + g) * LIST_WORDS + i]
            sts, shifts, cs = [], [], []
            for u in range(SLC_BATCH):
                cid = lax.shift_right_logical(word, 8 * u) & 255
                valid = cid < N_BLK_PAD // 2
                c = jnp.where(valid, cid, 0)
                sts.append(slc_scores(g, c))
                shifts.append(slope_rows[g] * ((qb - c) * KEY_CHUNK).astype(F32)
                              + jnp.where(valid, 0.0, MASK_BIG))
                cs.append(c)
            m_old = m_ref[g]
            m_new = functools.reduce(jnp.maximum, [m_old] + [jnp.max(st, axis=0, keepdims=True) - dl
                                                             for st, dl in zip(sts, shifts)])
            pv = sum(_dot(vt_ref[0, g, :, chunk_rows(c)], jnp.exp(st - (m_new + dl)).astype(BF16))
                     for st, dl, c in zip(sts, shifts, cs))
            acc_ref[g] = jnp.exp(m_old - m_new) * acc_ref[g] + pv
            m_ref[g] = m_new
        return carry

    lax.fori_loop(0, counts_ref[step_id], slc_body, 0)

    per_head = []
    for g in range(N_GROUPS):
        o_slc = normalize(acc_ref[g])
        for j in range(HPG):
            h = HPG * g + j
            lanes = slice(Q_BLOCK * j, Q_BLOCK * (j + 1))
            per_head.append(gt[3 * h + 1:3 * h + 2, :] * o_slc[:, lanes]
                            + gt[3 * h + 2:3 * h + 3, :] * o_win[g][:, lanes])
    o_t = jnp.concatenate(per_head, axis=0)
    out_ref[0] = (ocmp_ref[0] + o_t).T.astype(BF16)


def _slc_win(counts, lists, qt, kaug, vt, onehot, selt, gatest, ocmp):
    bsz, _, t = qt.shape
    nqb = t // Q_BLOCK
    grid_spec = pltpu.PrefetchScalarGridSpec(
        num_scalar_prefetch=2,
        grid=(bsz, nqb),
        in_specs=[pl.BlockSpec((1, NSA_W, Q_BLOCK), lambda b, i, *_: (b, 0, i)),
                  pl.BlockSpec((1, 2 * N_GROUPS, t, LANES), lambda b, i, *_: (b, 0, 0, 0)),
                  pl.BlockSpec((1, 2 * N_GROUPS, VT_ROWS, t), lambda b, i, *_: (b, 0, 0, 0)),
                  pl.BlockSpec((t, N_BLK_PAD), lambda b, i, *_: (0, 0)),
                  pl.BlockSpec((1, N_GROUPS, N_BLK_PAD, Q_BLOCK), lambda b, i, *_: (b, 0, 0, i)),
                  pl.BlockSpec((1, GATET_ROWS, Q_BLOCK), lambda b, i, *_: (b, 0, i)),
                  pl.BlockSpec((1, NSA_W, Q_BLOCK), lambda b, i, *_: (b, 0, i))],
        out_specs=pl.BlockSpec((1, Q_BLOCK, NSA_W), lambda b, i, *_: (b, i, 0)),
        scratch_shapes=[pltpu.VMEM((N_GROUPS, 2 * LANES, HPG * Q_BLOCK), BF16),
                        pltpu.VMEM((N_GROUPS, 1, HPG * Q_BLOCK), F32),
                        pltpu.VMEM((N_GROUPS, VT_ROWS, HPG * Q_BLOCK), F32)],
    )
    return pl.pallas_call(
        functools.partial(_slc_win_kernel, nqb=nqb),
        grid_spec=grid_spec,
        out_shape=jax.ShapeDtypeStruct((bsz, t, NSA_W), BF16),
        compiler_params=pltpu.CompilerParams(dimension_semantics=("arbitrary", "arbitrary"),
                                             vmem_limit_bytes=VMEM_LIMIT),
        name="slc_win",
    )(counts, lists, qt, kaug, vt, onehot, selt, gatest, ocmp)


def _merge_kernel(x_ref, onsa_ref, u_ref, vn_ref, g_ref, wm_ref, bm_ref, ws_ref, bs_ref,
                  wpa_ref, wpb_ref, wo_ref, h_ref, *, tm):
    x = x_ref[...]
    xn = _rms(x, g_ref[...]).astype(BF16)
    tril = _iota((GMLP_CHUNK, GMLP_CHUNK), 0) >= _iota((GMLP_CHUNK, GMLP_CHUNK), 1)
    sgu_rows = []
    for c in range(tm // GMLP_CHUNK):
        rows = slice(GMLP_CHUNK * c, GMLP_CHUNK * (c + 1))
        cols = []
        for g in range(GMLP_GROUPS):
            lanes = slice(LANES * g, LANES * (g + 1))
            w = jnp.where(tril, ws_ref[g], 0.0).astype(BF16)
            cols.append(_dot(w, vn_ref[rows, lanes]) + bs_ref[:, g:g + 1])
        sgu_rows.append(u_ref[rows, :] * jnp.concatenate(cols, axis=1))
    o_sgu = jnp.concatenate(sgu_rows, axis=0).astype(BF16)
    mg = jax.nn.sigmoid(_dot(xn, wm_ref[...]) + bm_ref[...])
    mixed = (mg[:, :D_MODEL] * _dot(onsa_ref[...], wpa_ref[...])
             + mg[:, D_MODEL:] * _dot(o_sgu, wpb_ref[...]))
    h_ref[...] = x + _dot(mixed.astype(BF16), wo_ref[...])


def _merge(x2, onsa2, u2, vn2, norm_g, wm, bm, ws, bs_t, wpa, wpb, wo, tm):
    n = x2.shape[0]
    row = lambda w: pl.BlockSpec((tm, w), lambda i: (i, 0))
    full = lambda a: pl.BlockSpec(a.shape, lambda i: (0,) * a.ndim)
    return pl.pallas_call(
        functools.partial(_merge_kernel, tm=tm),
        grid=(n // tm,),
        in_specs=[row(D_MODEL), row(NSA_W), row(GMLP_WIDTH), row(GMLP_WIDTH), full(norm_g),
                  full(wm), full(bm), full(ws), full(bs_t), full(wpa), full(wpb), full(wo)],
        out_specs=row(D_MODEL),
        out_shape=jax.ShapeDtypeStruct((n, D_MODEL), F32),
        compiler_params=pltpu.CompilerParams(dimension_semantics=("arbitrary",),
                                             vmem_limit_bytes=VMEM_LIMIT),
        name="merge",
    )(x2, onsa2, u2, vn2, norm_g, wm, bm, ws, bs_t, wpa, wpb, wo)


def _memkv_kernel(mem_ref, g_ref, w_ref, out_ref):
    out_ref[0] = _dot(_rms(mem_ref[0], g_ref[...]).astype(BF16), w_ref[...]).astype(BF16)


def _memkv(mem, norm_g, w):
    bsz, nm, _ = mem.shape
    return pl.pallas_call(
        _memkv_kernel,
        grid=(bsz,),
        in_specs=[pl.BlockSpec((1, nm, D_MODEL), lambda b: (b, 0, 0)),
                  pl.BlockSpec(norm_g.shape, lambda b: (0, 0)),
                  pl.BlockSpec(w.shape, lambda b: (0, 0))],
        out_specs=pl.BlockSpec((1, nm, 2 * MEM_W), lambda b: (b, 0, 0)),
        out_shape=jax.ShapeDtypeStruct((bsz, nm, 2 * MEM_W), BF16),
        compiler_params=pltpu.CompilerParams(dimension_semantics=("arbitrary",),
                                             vmem_limit_bytes=VMEM_LIMIT),
        name="memkv",
    )(mem, norm_g, w)


def _xattn_kernel(h_ref, g_ref, wq_ref, mkv_ref, wo_ref, out_ref):
    h = h_ref[...]
    hq = _dot(_rms(h, g_ref[...]).astype(BF16), wq_ref[...]).astype(BF16)
    heads = []
    for a in range(MEM_HEADS):
        lanes = slice(MEM_HEAD_DIM * a, MEM_HEAD_DIM * (a + 1))
        k = mkv_ref[0, :, lanes]
        v = mkv_ref[0, :, MEM_W + MEM_HEAD_DIM * a:MEM_W + MEM_HEAD_DIM * (a + 1)]
        s = _dot_nt(hq[:, lanes], k) * (MEM_HEAD_DIM ** -0.5)
        e = jnp.exp(s - jnp.max(s, axis=-1, keepdims=True))
        p = e / jnp.sum(e, axis=-1, keepdims=True)
        heads.append(_dot(p.astype(BF16), v))
    o = jnp.concatenate(heads, axis=1).astype(BF16)
    out_ref[...] = h + _dot(o, wo_ref[...])


def _xattn(h2d, norm_g, wq, mkv, wo, tm, rows_per_batch):
    n = h2d.shape[0]
    nm = mkv.shape[1]
    tiles_per_batch = rows_per_batch // tm
    full = lambda a: pl.BlockSpec(a.shape, lambda i: (0,) * a.ndim)
    return pl.pallas_call(
        _xattn_kernel,
        grid=(n // tm,),
        in_specs=[pl.BlockSpec((tm, D_MODEL), lambda i: (i, 0)), full(norm_g), full(wq),
                  pl.BlockSpec((1, nm, 2 * MEM_W), lambda i: (i // tiles_per_batch, 0, 0)),
                  full(wo)],
        out_specs=pl.BlockSpec((tm, D_MODEL), lambda i: (i, 0)),
        out_shape=jax.ShapeDtypeStruct((n, D_MODEL), F32),
        compiler_params=pltpu.CompilerParams(dimension_semantics=("arbitrary",),
                                             vmem_limit_bytes=VMEM_LIMIT),
        name="xattn",
    )(h2d, norm_g, wq, mkv, wo)


def _ffn_kernel(h_ref, g_ref, wgu_ref, wd_ref, gf_ref, out_ref, *, d_ff):
    h = h_ref[...]
    hn = _rms(h, g_ref[...]).astype(BF16)
    gate = _dot(hn, wgu_ref[:, :d_ff])
    up = _dot(hn, wgu_ref[:, d_ff:])
    act = (jax.nn.silu(gate) * up).astype(BF16)
    y = h + _dot(act, wd_ref[...])
    out_ref[...] = _rms(y, gf_ref[...])


def _ffn(h2d, norm_g, wgu, wd, norm_f, tm):
    n = h2d.shape[0]
    d_ff = wd.shape[0]
    full = lambda a: pl.BlockSpec(a.shape, lambda i: (0,) * a.ndim)
    once = lambda a: pl.BlockSpec(a.shape, lambda i: (0,) * a.ndim, pipeline_mode=pl.Buffered(1))
    return pl.pallas_call(
        functools.partial(_ffn_kernel, d_ff=d_ff),
        grid=(n // tm,),
        in_specs=[pl.BlockSpec((tm, D_MODEL), lambda i: (i, 0)), full(norm_g), once(wgu), once(wd),
                  full(norm_f)],
        out_specs=pl.BlockSpec((tm, D_MODEL), lambda i: (i, 0)),
        out_shape=jax.ShapeDtypeStruct((n, D_MODEL), F32),
        compiler_params=pltpu.CompilerParams(dimension_semantics=("arbitrary",),
                                             vmem_limit_bytes=VMEM_LIMIT),
        name="ffn",
    )(h2d, norm_g, wgu, wd, norm_f)


def _block_diag2(w):
    z = jnp.zeros_like(w)
    return jnp.concatenate([jnp.concatenate([w, z], axis=-1), jnp.concatenate([z, w], axis=-1)], axis=-2)


def _chunk_lists(flags):
    bsz, nqb = flags.shape[:2]
    n_chunks = N_BLK_PAD // 2
    f = flags[:, :, :, 0, :].reshape(bsz, nqb, N_GROUPS, n_chunks, 2).max(axis=-1)
    cid = jnp.arange(n_chunks, dtype=jnp.int32)
    own = jnp.arange(nqb, dtype=jnp.int32)[None, :, None, None]
    active = (f > 0) & (cid < own)
    n_active = active.sum(axis=-1)
    slot = jnp.cumsum(active, axis=-1) - 1
    hit = active[..., :, None] & (slot[..., :, None] == cid)
    ids = jnp.sum(jnp.where(hit, cid[:, None], 0), axis=-2)
    ids = jnp.where(cid < n_active[..., None], ids, 255)
    n_batches = (n_active.max(axis=-1) + SLC_BATCH - 1) // SLC_BATCH
    packed = ids.reshape(bsz, nqb, N_GROUPS, LIST_WORDS, SLC_BATCH)
    words = functools.reduce(jnp.bitwise_or, [packed[..., u] << (8 * u) for u in range(SLC_BATCH)])
    return n_batches.reshape(-1).astype(jnp.int32), words.reshape(-1).astype(jnp.int32)


def kernel(x, mem, norm_mix, w_in, w_cmp_k1, w_cmp_k2, w_cmp_v1, w_cmp_v2, pe_cmp_k, pe_cmp_v, ln_sgu, w_spatial, b_spatial, w_proj_a, w_proj_b, w_merge, b_merge, w_out, norm_mem_q, norm_mem_kv, w_mq, w_mkv, w_mo, norm_ffn, w_gate_up, w_down, norm_final):
    bsz, t, d = x.shape
    depth = norm_mix.shape[0]
    assert d == D_MODEL and t % Q_BLOCK == 0 and t // SEL_BLOCK <= N_BLK_PAD
    assert t // SEL_BLOCK >= N_SELECT and depth == 1 and t % 256 == 0
    n = bsz * t
    tm = 256
    h = x.reshape(n, d)
    c0, c1, c2, c3 = NSA_W, NSA_W + KV_W, NSA_W + 3 * KV_W, NSA_W + 3 * KV_W + 2 * GMLP_WIDTH
    onehot = (jnp.arange(t)[:, None] // SEL_BLOCK == jnp.arange(N_BLK_PAD)[None, :]).astype(BF16)
    pad_cols = lambda w, width: jnp.pad(w, ((0, 0), (0, width - w.shape[1])))
    for l in range(depth):
        wi = w_in[l]
        wqt = wi[:, :c0].T.astype(BF16)
        wkvc = wi[:, c0:c1].astype(BF16)
        wk, wvt = [], []
        for branch in range(2):
            base = c1 + KV_W * branch
            for g in range(N_GROUPS):
                wk.append(pad_cols(wi[:, base + HEAD_DIM * g:base + HEAD_DIM * (g + 1)], LANES))
                v0 = base + N_GROUPS * HEAD_DIM + HEAD_DIM * g
                wvt.append(pad_cols(wi[:, v0:v0 + HEAD_DIM], VT_ROWS).T)
        wk = jnp.stack(wk).astype(BF16)
        wvt = jnp.stack(wvt).astype(BF16)
        wuv = wi[:, c2:c3].astype(BF16)
        wgatet = pad_cols(wi[:, c3:], GATET_ROWS).T.astype(BF16)
        qt, kvc2, kaug, vt, u2, vn2, gatest = _inproj(
            h, norm_mix[l][None], ln_sgu[l][None], wqt, wkvc, wk, wvt, wuv, wgatet, tm, bsz, t)

        pe2 = jnp.stack([pe_cmp_k[l], pe_cmp_v[l]])
        pe2 = jnp.concatenate([pe2, pe2], axis=-1)
        w1 = jnp.stack([w_cmp_k1[l], w_cmp_v1[l]]).reshape(2, CMP_BLOCK, HEAD_DIM, CMP_HIDDEN)
        w1bd = _block_diag2(w1).astype(BF16)
        w2k = _block_diag2(w_cmp_k2[l]).astype(BF16)
        w2vt = _block_diag2(w_cmp_v2[l]).T.astype(BF16)
        kc, vct = _compress(kvc2.reshape(bsz, t, KV_W), pe2, w1bd, w2k, w2vt)
        ocmp, selt, flags = _cmp_topk(qt, kc, vct, gatest)
        counts, lists = _chunk_lists(flags)
        onsa = _slc_win(counts, lists, qt, kaug, vt, onehot, selt, gatest, ocmp)

        h = _merge(h, onsa.reshape(n, NSA_W), u2, vn2, norm_mix[l][None], w_merge[l].astype(BF16),
                   b_merge[l][None], w_spatial[l], b_spatial[l].T, w_proj_a[l].astype(BF16),
                   w_proj_b[l].astype(BF16), w_out[l].astype(BF16), tm)

        mkv = _memkv(mem, norm_mem_kv[l][None], w_mkv[l].astype(BF16))
        h = _xattn(h, norm_mem_q[l][None], w_mq[l].astype(BF16), mkv, w_mo[l].astype(BF16), tm, t)
        h = _ffn(h, norm_ffn[l][None], w_gate_up[l].astype(BF16), w_down[l].astype(BF16),
                 norm_final[None], tm)
    return h.reshape(bsz, t, d)
```

```python
import functools

import jax
import jax.numpy as jnp
from jax import lax
from jax.experimental import pallas as pl
from jax.experimental.pallas import tpu as pltpu

F32 = jnp.float32
BF16 = jnp.bfloat16

LANES = 128
D_MODEL = 1024
N_HEADS = 8
HEAD_DIM = 64
N_GROUPS = 2
HPG = N_HEADS // N_GROUPS
CMP_BLOCK = 32
CMP_STRIDE = 16
CMP_HIDDEN = 128
SEL_BLOCK = 64
N_SELECT = 16
WINDOW = 512
Q_BLOCK = 128
KEY_CHUNK = 128
N_BLK_PAD = 128
NSA_W = N_HEADS * HEAD_DIM
KV_W = 2 * N_GROUPS * HEAD_DIM
GMLP_WIDTH = 512
GMLP_GROUPS = 4
GMLP_CHUNK = 128
MEM_HEADS = 4
MEM_HEAD_DIM = 128
MEM_W = MEM_HEADS * MEM_HEAD_DIM
GATE_W = 3 * N_HEADS
NSA_QB = 256
SLC_BATCH = 4
LIST_WORDS = (N_BLK_PAD // 2) // SLC_BATCH
VT_ROWS = 80
VCT_ROWS = VT_ROWS + N_BLK_PAD
CMP_KEY_CHUNK = 128
N_FORCED = 3
GATET_ROWS = 32
MASK_BIG = 1e30
EPS = 1e-6
NEG = -1e30
REMOVED = -3e38
FORCE_SCORE = 1e6
SLOPES = tuple(2.0 ** (-8.0 * (h + 1) / N_HEADS) for h in range(N_HEADS))
VMEM_LIMIT = 56 * 1024 * 1024


def _dot(a, b):
    return jnp.dot(a, b, preferred_element_type=F32)


def _dot_nt(a, b):
    return lax.dot_general(a, b, (((1,), (1,)), ((), ())), preferred_element_type=F32)


def _rms(x, g):
    return x * lax.rsqrt(jnp.mean(x * x, axis=-1, keepdims=True) + EPS) * g


def _iota(shape, dim):
    return lax.broadcasted_iota(jnp.int32, shape, dim)


def _inproj_kernel(x_ref, g_ref, lng_ref, wqt_ref, wkvc_ref, wk_ref, wvt_ref, wuv_ref, wgatet_ref,
                   qt_ref, kvc_ref, kaug_ref, vt_ref, u_ref, vn_ref, gatest_ref, *, tm):
    xn = _rms(x_ref[...], g_ref[...]).astype(BF16)
    qt_ref[0] = (_dot_nt(wqt_ref[...], xn) * (HEAD_DIM ** -0.5)).astype(BF16)
    kvc_ref[...] = _dot(xn, wkvc_ref[...])
    lane = _iota((tm, LANES), 1)
    key_feat = jnp.where(lane == HEAD_DIM, _iota((tm, LANES), 0) & (KEY_CHUNK - 1), 0).astype(F32)
    ones_row = jnp.where(_iota((VT_ROWS, tm), 0) == HEAD_DIM, 1.0, 0.0)
    for a in range(2 * N_GROUPS):
        kaug_ref[0, a] = (_dot(xn, wk_ref[a]) + key_feat).astype(BF16)
        vt_ref[0, a] = (_dot_nt(wvt_ref[a], xn) + ones_row).astype(BF16)
    uv = jax.nn.gelu(_dot(xn, wuv_ref[...]))
    u_ref[...] = uv[:, :GMLP_WIDTH]
    v = uv[:, GMLP_WIDTH:]
    vc = v - jnp.mean(v, axis=-1, keepdims=True)
    vn = vc * lax.rsqrt(jnp.mean(vc * vc, axis=-1, keepdims=True) + EPS) * lng_ref[...]
    vn_ref[...] = vn.astype(BF16)
    gatest_ref[0] = jax.nn.sigmoid(_dot_nt(wgatet_ref[...], xn))


def _inproj(x2, norm_g, ln_g, wqt, wkvc, wk, wvt, wuv, wgatet, tm, bsz, t):
    n = x2.shape[0]
    tpb = t // tm
    row = lambda w: pl.BlockSpec((tm, w), lambda i: (i, 0))
    full = lambda a: pl.BlockSpec(a.shape, lambda i: (0,) * a.ndim)
    return pl.pallas_call(
        functools.partial(_inproj_kernel, tm=tm),
        grid=(n // tm,),
        in_specs=[row(D_MODEL), full(norm_g), full(ln_g), full(wqt), full(wkvc), full(wk),
                  full(wvt), full(wuv), full(wgatet)],
        out_specs=[pl.BlockSpec((1, NSA_W, tm), lambda i: (i // tpb, 0, i % tpb)),
                   row(KV_W),
                   pl.BlockSpec((1, 2 * N_GROUPS, tm, LANES), lambda i: (i // tpb, 0, i % tpb, 0)),
                   pl.BlockSpec((1, 2 * N_GROUPS, VT_ROWS, tm), lambda i: (i // tpb, 0, 0, i % tpb)),
                   row(GMLP_WIDTH), row(GMLP_WIDTH),
                   pl.BlockSpec((1, GATET_ROWS, tm), lambda i: (i // tpb, 0, i % tpb))],
        out_shape=[jax.ShapeDtypeStruct((bsz, NSA_W, t), BF16),
                   jax.ShapeDtypeStruct((n, KV_W), F32),
                   jax.ShapeDtypeStruct((bsz, 2 * N_GROUPS, t, LANES), BF16),
                   jax.ShapeDtypeStruct((bsz, 2 * N_GROUPS, VT_ROWS, t), BF16),
                   jax.ShapeDtypeStruct((n, GMLP_WIDTH), F32),
                   jax.ShapeDtypeStruct((n, GMLP_WIDTH), BF16),
                   jax.ShapeDtypeStruct((bsz, GATET_ROWS, t), F32)],
        compiler_params=pltpu.CompilerParams(dimension_semantics=("arbitrary",),
                                             vmem_limit_bytes=VMEM_LIMIT),
        name="inproj",
    )(x2, norm_g, ln_g, wqt, wkvc, wk, wvt, wuv, wgatet)


def _compress_kernel(xk_ref, xv_ref, pe_ref, w1_ref, w2k_ref, w2vt_ref, kc_ref, vct_ref, *, nc):
    outs = []
    for j, x_ref in enumerate((xk_ref, xv_ref)):
        a = jnp.zeros((nc, 2 * CMP_HIDDEN), F32)
        b = jnp.zeros((nc, 2 * CMP_HIDDEN), F32)
        for t in range(CMP_STRIDE):
            xt = x_ref[0, pl.ds(t, nc, stride=CMP_STRIDE), :]
            a = a + _dot((xt + pe_ref[j, t:t + 1, :]).astype(BF16), w1_ref[j, t])
            b = b + _dot((xt + pe_ref[j, CMP_STRIDE + t:CMP_STRIDE + t + 1, :]).astype(BF16),
                         w1_ref[j, CMP_STRIDE + t])
        outs.append(jax.nn.gelu(a + pltpu.roll(b, nc - 1, 0)).astype(BF16))
    k2 = _dot(outs[0], w2k_ref[...])
    v_t = _dot_nt(w2vt_ref[...], outs[1])
    lane = _iota((nc, LANES), 1)
    key_feat = jnp.where(
        lane == HEAD_DIM, CMP_STRIDE * (_iota((nc, LANES), 0) & (CMP_KEY_CHUNK - 1)), 0).astype(F32)
    ci =_iota((N_BLK_PAD, nc), 1) * CMP_STRIDE
    sj = _iota((N_BLK_PAD, nc), 0) * SEL_BLOCK
    overlap_t = jnp.where((ci < sj + SEL_BLOCK) & (ci + (CMP_BLOCK - 1) >= sj), 1.0, 0.0).astype(BF16)
    ones_rows = jnp.where(_iota((VT_ROWS - HEAD_DIM, nc), 0) == 0, 1.0, 0.0).astype(BF16)
    for g in range(N_GROUPS):
        kg = k2 if g == 0 else pltpu.roll(k2, HEAD_DIM, 1)
        kc_ref[0, g] = jnp.where(lane < HEAD_DIM, kg, key_feat).astype(BF16)
        vct_ref[0, g, 0:HEAD_DIM, :] = v_t[HEAD_DIM * g:HEAD_DIM * (g + 1), :].astype(BF16)
        vct_ref[0, g, HEAD_DIM:VT_ROWS, :] = ones_rows
        vct_ref[0, g, VT_ROWS:VCT_ROWS, :] = overlap_t


def _compress(kvc3, pe2, w1bd, w2k, w2vt):
    bsz, t, _ = kvc3.shape
    nc = t // CMP_STRIDE
    full = lambda a: pl.BlockSpec(a.shape, lambda b: (0,) * a.ndim)
    return pl.pallas_call(
        functools.partial(_compress_kernel, nc=nc),
        grid=(bsz,),
        in_specs=[pl.BlockSpec((1, t, LANES), lambda b: (b, 0, 0)),
                  pl.BlockSpec((1, t, LANES), lambda b: (b, 0, 1)),
                  full(pe2), full(w1bd), full(w2k), full(w2vt)],
        out_specs=[pl.BlockSpec((1, N_GROUPS, nc, LANES), lambda b: (b, 0, 0, 0)),
                   pl.BlockSpec((1, N_GROUPS, VCT_ROWS, nc), lambda b: (b, 0, 0, 0))],
        out_shape=[jax.ShapeDtypeStruct((bsz, N_GROUPS, nc, LANES), BF16),
                   jax.ShapeDtypeStruct((bsz, N_GROUPS, VCT_ROWS, nc), BF16)],
        compiler_params=pltpu.CompilerParams(dimension_semantics=("arbitrary",),
                                             vmem_limit_bytes=VMEM_LIMIT),
        name="compress",
    )(kvc3, kvc3, pe2, w1bd, w2k, w2vt)


def _cmp_topk_kernel(qt_ref, kc_ref, vct_ref, gatest_ref, ocmp_ref, selt_ref, flags_ref,
                     m_ref, acc_ref, *, ncp):
    qb = pl.program_id(1)
    start = qb * Q_BLOCK
    n_chunks = ncp // CMP_KEY_CHUNK
    nck = (qb * (Q_BLOCK // CMP_STRIDE) + (Q_BLOCK - CMP_BLOCK) // CMP_STRIDE) // CMP_KEY_CHUNK + 1
    chunk_tokens = CMP_KEY_CHUNK * CMP_STRIDE
    tile_heads = lambda a: jnp.concatenate([a] * HPG, axis=1)
    ki = _iota((CMP_KEY_CHUNK, Q_BLOCK), 0)
    qi = _iota((CMP_KEY_CHUNK, Q_BLOCK), 1)
    feat_row = _iota((LANES - HEAD_DIM, HPG * Q_BLOCK), 0) == 0
    blk = _iota((N_BLK_PAD, Q_BLOCK), 0)
    blk_f = blk.astype(F32)
    t_row = start + _iota((1, Q_BLOCK), 1)
    cur = lax.shift_right_logical(t_row, 6)
    causal = blk <= cur
    forced = (blk == 0) | (blk == cur) | (blk == cur - 1)
    has_key = t_row >= CMP_BLOCK - 1
    gt = gatest_ref[0]
    ones8 = jnp.ones((8, Q_BLOCK), F32)

    def chunk_rows(c):
        return pl.ds(pl.multiple_of(c * CMP_KEY_CHUNK, CMP_KEY_CHUNK), CMP_KEY_CHUNK)

    per_head = []
    for g in range(N_GROUPS):
        slope_row = jnp.concatenate(
            [jnp.full((1, Q_BLOCK), SLOPES[HPG * g + j], F32) for j in range(HPG)], axis=1)
        qa = jnp.concatenate(
            [jnp.concatenate([qt_ref[0, HEAD_DIM * (HPG * g + j):HEAD_DIM * (HPG * g + j + 1), :]
                              for j in range(HPG)], axis=1),
             jnp.where(feat_row, slope_row, 0.0).astype(BF16)], axis=0)

        def delta(c):
            return slope_row * (start - c * chunk_tokens).astype(F32)

        def weighted(sts, dls, cs, m):
            return sum(_dot(vct_ref[0, g, :, chunk_rows(c)], jnp.exp(st - (m + dl)).astype(BF16))
                       for st, dl, c in zip(sts, dls, cs))

        def col_max(sts, dls):
            return [jnp.max(st, axis=0, keepdims=True) - dl for st, dl in zip(sts, dls)]

        if n_chunks > 2:
            @pl.when(nck > 2)
            def _():
                cs = list(range(n_chunks - 2))
                sts = [_dot(kc_ref[0, g, chunk_rows(c), :], qa) for c in cs]
                dls = [delta(c) + jnp.where(c < nck - 2, 0.0, MASK_BIG) for c in cs]
                m1 = functools.reduce(jnp.maximum, col_max(sts, dls))
                m_ref[g] = m1
                acc_ref[g] = weighted(sts, dls, cs, m1)

            @pl.when(nck <= 2)
            def _():
                m_ref[g] = jnp.full(m_ref.shape[1:], NEG, F32)
                acc_ref[g] = jnp.zeros(acc_ref.shape[1:], F32)
            m_old = m_ref[g]
            acc_old = acc_ref[g]
        else:
            m_old = jnp.full((1, HPG * Q_BLOCK), NEG, F32)
            acc_old = jnp.zeros((VCT_ROWS, HPG * Q_BLOCK), F32)

        sts, dls, cs = [], [], []
        for u in (2, 1):
            c = jnp.maximum(nck - u, 0)
            key_end = CMP_STRIDE * (c * CMP_KEY_CHUNK + ki) + (CMP_BLOCK - 1) - start
            sts.append(_dot(kc_ref[0, g, chunk_rows(c), :], qa)
                       + tile_heads(jnp.where(key_end <= qi, 0.0, NEG)))
            dls.append(delta(c) + jnp.where(nck - u >= 0, 0.0, MASK_BIG))
            cs.append(c)
        m_new = functools.reduce(jnp.maximum, [m_old] + col_max(sts, dls))
        acc = jnp.exp(m_old - m_new) * acc_old + weighted(sts, dls, cs, m_new)

        inv_l = 1.0 / jnp.maximum(acc[HEAD_DIM:HEAD_DIM + 1], 1e-30)
        o = acc[0:HEAD_DIM] * inv_l
        imp_h = acc[VT_ROWS:VCT_ROWS] * inv_l
        imp = sum(imp_h[:, Q_BLOCK * j:Q_BLOCK * (j + 1)] for j in range(HPG))
        imp = jnp.where(has_key, imp, 0.0)
        for j in range(HPG):
            h = HPG * g + j
            per_head.append(jnp.where(has_key, gt[3 * h:3 * h + 1, :] * o[:, Q_BLOCK * j:Q_BLOCK * (j + 1)], 0.0))

        sel = jnp.where(forced, 1.0, 0.0)
        rank = jnp.where(causal, jnp.where(forced, REMOVED, imp), NEG)
        for _ in range(N_SELECT - N_FORCED):
            m = jnp.max(rank, axis=0, keepdims=True)
            idx = jnp.min(jnp.where(rank == m, blk_f, float(N_BLK_PAD)), axis=0, keepdims=True)
            chosen = blk_f == idx
            sel = jnp.where(chosen, 1.0, sel)
            rank = jnp.where(chosen, REMOVED, rank)
        sel = jnp.where(causal, sel, 0.0)
        selt_ref[0, g] = sel.astype(BF16)
        flags_ref[0, 0, g] = (_dot_nt(ones8, sel) > 0.5).astype(jnp.int32)
    ocmp_ref[0] = jnp.concatenate(per_head, axis=0)


def _cmp_topk(qt, kc, vct, gatest):
    bsz, _, t = qt.shape
    ncp = t // CMP_STRIDE
    nqb = t // Q_BLOCK
    return pl.pallas_call(
        functools.partial(_cmp_topk_kernel, ncp=ncp),
        grid=(bsz, nqb),
        in_specs=[pl.BlockSpec((1, NSA_W, Q_BLOCK), lambda b, i: (b, 0, i)),
                  pl.BlockSpec((1, N_GROUPS, ncp, LANES), lambda b, i: (b, 0, 0, 0)),
                  pl.BlockSpec((1, N_GROUPS, VCT_ROWS, ncp), lambda b, i: (b, 0, 0, 0)),
                  pl.BlockSpec((1, GATET_ROWS, Q_BLOCK), lambda b, i: (b, 0, i))],
        out_specs=[pl.BlockSpec((1, NSA_W, Q_BLOCK), lambda b, i: (b, 0, i)),
                   pl.BlockSpec((1, N_GROUPS, N_BLK_PAD, Q_BLOCK), lambda b, i: (b, 0, 0, i)),
                   pl.BlockSpec((1, 1, N_GROUPS, 8, N_BLK_PAD), lambda b, i: (b, i, 0, 0, 0))],
        out_shape=[jax.ShapeDtypeStruct((bsz, NSA_W, t), F32),
                   jax.ShapeDtypeStruct((bsz, N_GROUPS, N_BLK_PAD, t), BF16),
                   jax.ShapeDtypeStruct((bsz, nqb, N_GROUPS, 8, N_BLK_PAD), jnp.int32)],
        scratch_shapes=[pltpu.VMEM((N_GROUPS, 1, HPG * Q_BLOCK), F32),
                        pltpu.VMEM((N_GROUPS, VCT_ROWS, HPG * Q_BLOCK), F32)],
        compiler_params=pltpu.CompilerParams(dimension_semantics=("arbitrary", "arbitrary"),
                                             vmem_limit_bytes=VMEM_LIMIT),
        name="cmp_topk",
    )(qt, kc, vct, gatest)


def _slc_win_kernel(counts_ref, lists_ref, qt_ref, kaug_ref, vt_ref, oh_ref, selt_ref, gatest_ref,
                    ocmp_ref, out_ref, qaug_ref, m_ref, acc_ref, *, nqb):
    b = pl.program_id(0)
    qb = pl.program_id(1)
    step_id = b * nqb + qb
    start = qb * NSA_QB
    width = HPG * NSA_QB
    win_keys = WINDOW + NSA_QB
    win_start = jnp.maximum(start - WINDOW, 0)
    tile_heads = lambda a: jnp.concatenate([a] * HPG, axis=1)
    dist = (start - win_start) + _iota((win_keys, NSA_QB), 1) - _iota((win_keys, NSA_QB), 0)
    win_bias = tile_heads(jnp.where((dist >= 0) & (dist < WINDOW), 0.0, NEG))
    own_bias = tile_heads(jnp.where(_iota((NSA_QB, NSA_QB), 0) <= _iota((NSA_QB, NSA_QB), 1), 0.0, NEG))
    gt = gatest_ref[0]
    feat_row = _iota((HEAD_DIM, width), 0) == 0
    slope_rows = [jnp.concatenate([jnp.full((1, NSA_QB), SLOPES[HPG * g + j], F32) for j in range(HPG)],
                                  axis=1) for g in range(N_GROUPS)]

    def normalize(acc):
        return acc[0:HEAD_DIM] / jnp.maximum(acc[HEAD_DIM:HEAD_DIM + 1], 1e-30)

    def chunk_slabs(st):
        return [st[KEY_CHUNK * u:KEY_CHUNK * (u + 1)] for u in range(st.shape[0] // KEY_CHUNK)]

    def col_max(st, dls):
        return [jnp.max(s, axis=0, keepdims=True) - dl for s, dl in zip(chunk_slabs(st), dls)]

    def probs(st, dls, m):
        return jnp.concatenate([jnp.exp(s - (m + dl)).astype(BF16)
                                for s, dl in zip(chunk_slabs(st), dls)], axis=0)

    o_win = []
    for g in range(N_GROUPS):
        qaug_ref[g, 0:HEAD_DIM, :] = jnp.concatenate(
            [qt_ref[0, HEAD_DIM * (HPG * g + j):HEAD_DIM * (HPG * g + j + 1), :] for j in range(HPG)],
            axis=1)
        qaug_ref[g, HEAD_DIM:LANES, :] = jnp.where(feat_row, slope_rows[g], 0.0).astype(BF16)
        sel_bias = ((selt_ref[0, g].astype(F32) - 1.0) * MASK_BIG).astype(BF16)
        qaug_ref[g, LANES:2 * LANES, :] = tile_heads(sel_bias)

        a = N_GROUPS + g
        rows = pl.ds(pl.multiple_of(win_start, KEY_CHUNK), win_keys)
        st = _dot(kaug_ref[0, a, rows, :], qaug_ref[g, 0:LANES, :]) + win_bias
        dls = [slope_rows[g] * (start - win_start - KEY_CHUNK * u).astype(F32)
               for u in range(win_keys // KEY_CHUNK)]
        m = functools.reduce(jnp.maximum, col_max(st, dls))
        o_win.append(normalize(_dot(vt_ref[0, a, :, rows], probs(st, dls, m))))

        rows = pl.ds(pl.multiple_of(start, KEY_CHUNK), NSA_QB)
        st = _dot(jnp.concatenate([kaug_ref[0, g, rows, :], oh_ref[rows, :]], axis=1), qaug_ref[g])
        st = st + own_bias
        dls = [slope_rows[g] * float(-KEY_CHUNK * u) for u in range(NSA_QB // KEY_CHUNK)]
        m = functools.reduce(jnp.maximum, col_max(st, dls))
        m_ref[g] = m
        acc_ref[g] = _dot(vt_ref[0, g, :, rows], probs(st, dls, m))

    def slc_body(i, carry):
        for g in range(N_GROUPS):
            word = lists_ref[(step_id * N_GROUPS + g) * LIST_WORDS + i]
            ks, vs, dls = [], [], []
            for u in range(SLC_BATCH):
                cid = lax.shift_right_logical(word, 8 * u) & 255
                valid = cid < N_BLK_PAD // 2
                c = jnp.where(valid, cid, 0)
                rows = pl.ds(pl.multiple_of(c * KEY_CHUNK, KEY_CHUNK), KEY_CHUNK)
                ks.append(jnp.concatenate([kaug_ref[0, g, rows, :], oh_ref[rows, :]], axis=1))
                vs.append(vt_ref[0, g, :, rows])
                dls.append(slope_rows[g] * (start - c * KEY_CHUNK).astype(F32)
                           + jnp.where(valid, 0.0, MASK_BIG))
            st = _dot(jnp.concatenate(ks, axis=0), qaug_ref[g])
            m_old = m_ref[g]
            m_new = functools.reduce(jnp.maximum, [m_old] + col_max(st, dls))
            acc_ref[g] = (jnp.exp(m_old - m_new) * acc_ref[g]
                          + _dot(jnp.concatenate(vs, axis=1), probs(st, dls, m_new)))
            m_ref[g] = m_new
        return carry

    lax.fori_loop(0, counts_ref[step_id], slc_body, 0)

    per_head = []
    for g in range(N_GROUPS):
        o_slc = normalize(acc_ref[g])
        for j in range(HPG):
            h = HPG * g + j
            lanes = slice(NSA_QB * j, NSA_QB * (j + 1))
            per_head.append(gt[3 * h + 1:3 * h + 2, :] * o_slc[:, lanes]
                            + gt[3 * h + 2:3 * h + 3, :] * o_win[g][:, lanes])
    o_t = jnp.concatenate(per_head, axis=0)
    out_ref[0] = (ocmp_ref[0] + o_t).T.astype(BF16)


def _slc_win(counts, lists, qt, kaug, vt, onehot, selt, gatest, ocmp):
    bsz, _, t = qt.shape
    nqb = t // NSA_QB
    once = lambda shape, imap: pl.BlockSpec(shape, imap, pipeline_mode=pl.Buffered(1))
    grid_spec = pltpu.PrefetchScalarGridSpec(
        num_scalar_prefetch=2,
        grid=(bsz, nqb),
        in_specs=[pl.BlockSpec((1, NSA_W, NSA_QB), lambda b, i, *_: (b, 0, i)),
                  once((1, 2 * N_GROUPS, t, LANES), lambda b, i, *_: (b, 0, 0, 0)),
                  once((1, 2 * N_GROUPS, VT_ROWS, t), lambda b, i, *_: (b, 0, 0, 0)),
                  once((t, N_BLK_PAD), lambda b, i, *_: (0, 0)),
                  pl.BlockSpec((1, N_GROUPS, N_BLK_PAD, NSA_QB), lambda b, i, *_: (b, 0, 0, i)),
                  pl.BlockSpec((1, GATET_ROWS, NSA_QB), lambda b, i, *_: (b, 0, i)),
                  pl.BlockSpec((1, NSA_W, NSA_QB), lambda b, i, *_: (b, 0, i))],
        out_specs=pl.BlockSpec((1, NSA_QB, NSA_W), lambda b, i, *_: (b, i, 0)),
        scratch_shapes=[pltpu.VMEM((N_GROUPS, 2 * LANES, HPG * NSA_QB), BF16),
                        pltpu.VMEM((N_GROUPS, 1, HPG * NSA_QB), F32),
                        pltpu.VMEM((N_GROUPS, VT_ROWS, HPG * NSA_QB), F32)],
    )
    return pl.pallas_call(
        functools.partial(_slc_win_kernel, nqb=nqb),
        grid_spec=grid_spec,
        out_shape=jax.ShapeDtypeStruct((bsz, t, NSA_W), BF16),
        compiler_params=pltpu.CompilerParams(dimension_semantics=("arbitrary", "arbitrary"),
                                             vmem_limit_bytes=VMEM_LIMIT),
        name="slc_win",
    )(counts, lists, qt, kaug, vt, onehot, selt, gatest, ocmp)


def _merge_kernel(x_ref, onsa_ref, u_ref, vn_ref, g_ref, wm_ref, bm_ref, ws_ref, bs_ref,
                  wpa_ref, wpb_ref, wo_ref, h_ref, *, tm):
    x = x_ref[...]
    xn = _rms(x, g_ref[...]).astype(BF16)
    tril = _iota((GMLP_CHUNK, GMLP_CHUNK), 0) >= _iota((GMLP_CHUNK, GMLP_CHUNK), 1)
    sgu_rows = []
    for c in range(tm // GMLP_CHUNK):
        rows = slice(GMLP_CHUNK * c, GMLP_CHUNK * (c + 1))
        cols = []
        for g in range(GMLP_GROUPS):
            lanes = slice(LANES * g, LANES * (g + 1))
            w = jnp.where(tril, ws_ref[g], 0.0).astype(BF16)
            cols.append(_dot(w, vn_ref[rows, lanes]) + bs_ref[:, g:g + 1])
        sgu_rows.append(u_ref[rows, :] * jnp.concatenate(cols, axis=1))
    o_sgu = jnp.concatenate(sgu_rows, axis=0).astype(BF16)
    mg = jax.nn.sigmoid(_dot(xn, wm_ref[...]) + bm_ref[...])
    mixed = (mg[:, :D_MODEL] * _dot(onsa_ref[...], wpa_ref[...])
             + mg[:, D_MODEL:] * _dot(o_sgu, wpb_ref[...]))
    h_ref[...] = x + _dot(mixed.astype(BF16), wo_ref[...])


def _merge(x2, onsa2, u2, vn2, norm_g, wm, bm, ws, bs_t, wpa, wpb, wo, tm):
    n = x2.shape[0]
    row = lambda w: pl.BlockSpec((tm, w), lambda i: (i, 0))
    full = lambda a: pl.BlockSpec(a.shape, lambda i: (0,) * a.ndim)
    return pl.pallas_call(
        functools.partial(_merge_kernel, tm=tm),
        grid=(n // tm,),
        in_specs=[row(D_MODEL), row(NSA_W), row(GMLP_WIDTH), row(GMLP_WIDTH), full(norm_g),
                  full(wm), full(bm), full(ws), full(bs_t), full(wpa), full(wpb), full(wo)],
        out_specs=row(D_MODEL),
        out_shape=jax.ShapeDtypeStruct((n, D_MODEL), F32),
        compiler_params=pltpu.CompilerParams(dimension_semantics=("arbitrary",),
                                             vmem_limit_bytes=VMEM_LIMIT),
        name="merge",
    )(x2, onsa2, u2, vn2, norm_g, wm, bm, ws, bs_t, wpa, wpb, wo)


def _memkv_kernel(mem_ref, g_ref, w_ref, out_ref):
    out_ref[0] = _dot(_rms(mem_ref[0], g_ref[...]).astype(BF16), w_ref[...]).astype(BF16)


def _memkv(mem, norm_g, w):
    bsz, nm, _ = mem.shape
    return pl.pallas_call(
        _memkv_kernel,
        grid=(bsz,),
        in_specs=[pl.BlockSpec((1, nm, D_MODEL), lambda b: (b, 0, 0)),
                  pl.BlockSpec(norm_g.shape, lambda b: (0, 0)),
                  pl.BlockSpec(w.shape, lambda b: (0, 0))],
        out_specs=pl.BlockSpec((1, nm, 2 * MEM_W), lambda b: (b, 0, 0)),
        out_shape=jax.ShapeDtypeStruct((bsz, nm, 2 * MEM_W), BF16),
        compiler_params=pltpu.CompilerParams(dimension_semantics=("arbitrary",),
                                             vmem_limit_bytes=VMEM_LIMIT),
        name="memkv",
    )(mem, norm_g, w)


def _xattn_kernel(h_ref, g_ref, wq_ref, mkv_ref, wo_ref, out_ref):
    h = h_ref[...]
    hq = _dot(_rms(h, g_ref[...]).astype(BF16), wq_ref[...]).astype(BF16)
    heads = []
    for a in range(MEM_HEADS):
        lanes = slice(MEM_HEAD_DIM * a, MEM_HEAD_DIM * (a + 1))
        k = mkv_ref[0, :, lanes]
        v = mkv_ref[0, :, MEM_W + MEM_HEAD_DIM * a:MEM_W + MEM_HEAD_DIM * (a + 1)]
        s = _dot_nt(hq[:, lanes], k) * (MEM_HEAD_DIM ** -0.5)
        e = jnp.exp(s - jnp.max(s, axis=-1, keepdims=True))
        p = e / jnp.sum(e, axis=-1, keepdims=True)
        heads.append(_dot(p.astype(BF16), v))
    o = jnp.concatenate(heads, axis=1).astype(BF16)
    out_ref[...] = h + _dot(o, wo_ref[...])


def _xattn(h2d, norm_g, wq, mkv, wo, tm, rows_per_batch):
    n = h2d.shape[0]
    nm = mkv.shape[1]
    tiles_per_batch = rows_per_batch // tm
    full = lambda a: pl.BlockSpec(a.shape, lambda i: (0,) * a.ndim)
    return pl.pallas_call(
        _xattn_kernel,
        grid=(n // tm,),
        in_specs=[pl.BlockSpec((tm, D_MODEL), lambda i: (i, 0)), full(norm_g), full(wq),
                  pl.BlockSpec((1, nm, 2 * MEM_W), lambda i: (i // tiles_per_batch, 0, 0)),
                  full(wo)],
        out_specs=pl.BlockSpec((tm, D_MODEL), lambda i: (i, 0)),
        out_shape=jax.ShapeDtypeStruct((n, D_MODEL), F32),
        compiler_params=pltpu.CompilerParams(dimension_semantics=("arbitrary",),
                                             vmem_limit_bytes=VMEM_LIMIT),
        name="xattn",
    )(h2d, norm_g, wq, mkv, wo)


def _ffn_kernel(h_ref, g_ref, wgu_ref, wd_ref, gf_ref, out_ref, *, d_ff):
    h = h_ref[...]
    hn = _rms(h, g_ref[...]).astype(BF16)
    gate = _dot(hn, wgu_ref[:, :d_ff])
    up = _dot(hn, wgu_ref[:, d_ff:])
    act = (jax.nn.silu(gate) * up).astype(BF16)
    y = h + _dot(act, wd_ref[...])
    out_ref[...] = _rms(y, gf_ref[...])


def _ffn(h2d, norm_g, wgu, wd, norm_f, tm):
    n = h2d.shape[0]
    d_ff = wd.shape[0]
    full = lambda a: pl.BlockSpec(a.shape, lambda i: (0,) * a.ndim)
    once = lambda a: pl.BlockSpec(a.shape, lambda i: (0,) * a.ndim, pipeline_mode=pl.Buffered(1))
    return pl.pallas_call(
        functools.partial(_ffn_kernel, d_ff=d_ff),
        grid=(n // tm,),
        in_specs=[pl.BlockSpec((tm, D_MODEL), lambda i: (i, 0)), full(norm_g), once(wgu), once(wd),
                  full(norm_f)],
        out_specs=pl.BlockSpec((tm, D_MODEL), lambda i: (i, 0)),
        out_shape=jax.ShapeDtypeStruct((n, D_MODEL), F32),
        compiler_params=pltpu.CompilerParams(dimension_semantics=("arbitrary",),
                                             vmem_limit_bytes=VMEM_LIMIT),
        name="ffn",
    )(h2d, norm_g, wgu, wd, norm_f)


def _block_diag2(w):
    z = jnp.zeros_like(w)
    return jnp.concatenate([jnp.concatenate([w, z], axis=-1), jnp.concatenate([z, w], axis=-1)], axis=-2)


def _chunk_lists(flags):
    bsz = flags.shape[0]
    n_chunks = N_BLK_PAD // 2
    per_step = NSA_QB // Q_BLOCK
    nqb = flags.shape[1] // per_step
    f = flags[:, :, :, 0, :].reshape(bsz, nqb, per_step, N_GROUPS, n_chunks, 2).max(axis=(2, 5))
    cid = jnp.arange(n_chunks, dtype=jnp.int32)
    own = (NSA_QB // KEY_CHUNK) * jnp.arange(nqb, dtype=jnp.int32)[None, :, None, None]
    active = (f > 0) & (cid < own)
    n_active = active.sum(axis=-1)
    slot = jnp.cumsum(active, axis=-1) - 1
    hit = active[..., :, None] & (slot[..., :, None] == cid)
    ids = jnp.sum(jnp.where(hit, cid[:, None], 0), axis=-2)
    ids = jnp.where(cid < n_active[..., None], ids, 255)
    n_batches = (n_active.max(axis=-1) + SLC_BATCH - 1) // SLC_BATCH
    packed = ids.reshape(bsz, nqb, N_GROUPS, LIST_WORDS, SLC_BATCH)
    words = functools.reduce(jnp.bitwise_or, [packed[..., u] << (8 * u) for u in range(SLC_BATCH)])
    return n_batches.reshape(-1).astype(jnp.int32), words.reshape(-1).astype(jnp.int32)


def kernel(x, mem, norm_mix, w_in, w_cmp_k1, w_cmp_k2, w_cmp_v1, w_cmp_v2, pe_cmp_k, pe_cmp_v, ln_sgu, w_spatial, b_spatial, w_proj_a, w_proj_b, w_merge, b_merge, w_out, norm_mem_q, norm_mem_kv, w_mq, w_mkv, w_mo, norm_ffn, w_gate_up, w_down, norm_final):
    bsz, t, d = x.shape
    depth = norm_mix.shape[0]
    assert d == D_MODEL and t % Q_BLOCK == 0 and t // SEL_BLOCK <= N_BLK_PAD
    assert t // SEL_BLOCK >= N_SELECT and depth == 1 and t % 256 == 0
    n = bsz * t
    tm = 256
    h = x.reshape(n, d)
    c0, c1, c2, c3 = NSA_W, NSA_W + KV_W, NSA_W + 3 * KV_W, NSA_W + 3 * KV_W + 2 * GMLP_WIDTH
    onehot = (jnp.arange(t)[:, None] // SEL_BLOCK == jnp.arange(N_BLK_PAD)[None, :]).astype(BF16)
    pad_cols = lambda w, width: jnp.pad(w, ((0, 0), (0, width - w.shape[1])))
    for l in range(depth):
        wi = w_in[l]
        wqt = wi[:, :c0].T.astype(BF16)
        wkvc = wi[:, c0:c1].astype(BF16)
        wk, wvt = [], []
        for branch in range(2):
            base = c1 + KV_W * branch
            for g in range(N_GROUPS):
                wk.append(pad_cols(wi[:, base + HEAD_DIM * g:base + HEAD_DIM * (g + 1)], LANES))
                v0 = base + N_GROUPS * HEAD_DIM + HEAD_DIM * g
                wvt.append(pad_cols(wi[:, v0:v0 + HEAD_DIM], VT_ROWS).T)
        wk = jnp.stack(wk).astype(BF16)
        wvt = jnp.stack(wvt).astype(BF16)
        wuv = wi[:, c2:c3].astype(BF16)
        wgatet = pad_cols(wi[:, c3:], GATET_ROWS).T.astype(BF16)
        qt, kvc2, kaug, vt, u2, vn2, gatest = _inproj(
            h, norm_mix[l][None], ln_sgu[l][None], wqt, wkvc, wk, wvt, wuv, wgatet, tm, bsz, t)

        pe2 = jnp.stack([pe_cmp_k[l], pe_cmp_v[l]])
        pe2 = jnp.concatenate([pe2, pe2], axis=-1)
        w1 = jnp.stack([w_cmp_k1[l], w_cmp_v1[l]]).reshape(2, CMP_BLOCK, HEAD_DIM, CMP_HIDDEN)
        w1bd = _block_diag2(w1).astype(BF16)
        w2k = _block_diag2(w_cmp_k2[l]).astype(BF16)
        w2vt = _block_diag2(w_cmp_v2[l]).T.astype(BF16)
        kc, vct = _compress(kvc2.reshape(bsz, t, KV_W), pe2, w1bd, w2k, w2vt)
        ocmp, selt, flags = _cmp_topk(qt, kc, vct, gatest)
        counts, lists = _chunk_lists(flags)
        onsa = _slc_win(counts, lists, qt, kaug, vt, onehot, selt, gatest, ocmp)

        h = _merge(h, onsa.reshape(n, NSA_W), u2, vn2, norm_mix[l][None], w_merge[l].astype(BF16),
                   b_merge[l][None], w_spatial[l], b_spatial[l].T, w_proj_a[l].astype(BF16),
                   w_proj_b[l].astype(BF16), w_out[l].astype(BF16), tm)

        mkv = _memkv(mem, norm_mem_kv[l][None], w_mkv[l].astype(BF16))
        h = _xattn(h, norm_mem_q[l][None], w_mq[l].astype(BF16), mkv, w_mo[l].astype(BF16), tm, t)
        h = _ffn(h, norm_ffn[l][None], w_gate_up[l].astype(BF16), w_down[l].astype(BF16),
                 norm_final[None], tm)
    return h.reshape(bsz, t, d)
```

```python
import functools

import jax
import jax.numpy as jnp
from jax import lax
from jax.experimental import pallas as pl
from jax.experimental.pallas import tpu as pltpu

F32 = jnp.float32
BF16 = jnp.bfloat16

LANES = 128
D_MODEL = 1024
N_HEADS = 8
HEAD_DIM = 64
N_GROUPS = 2
HPG = N_HEADS // N_GROUPS
CMP_BLOCK = 32
CMP_STRIDE = 16
CMP_HIDDEN = 128
SEL_BLOCK = 64
N_SELECT = 16
WINDOW = 512
Q_BLOCK = 256
KEY_CHUNK = 128
N_BLK_PAD = 128
NSA_W = N_HEADS * HEAD_DIM
KV_W = 2 * N_GROUPS * HEAD_DIM
GMLP_WIDTH = 512
GMLP_GROUPS = 4
GMLP_CHUNK = 128
MEM_HEADS = 4
MEM_HEAD_DIM = 128
MEM_W = MEM_HEADS * MEM_HEAD_DIM
GATE_W = 3 * N_HEADS
NSA_QB = 256
SLC_BATCH = 4
LIST_WORDS = (N_BLK_PAD // 2) // SLC_BATCH
VT_ROWS = 80
VCT_ROWS = VT_ROWS + N_BLK_PAD
CMP_KEY_CHUNK = 128
N_FORCED = 3
GATET_ROWS = 32
MASK_BIG = 1e30
EPS = 1e-6
NEG = -1e30
REMOVED = -3e38
FORCE_SCORE = 1e6
SLOPES = tuple(2.0 ** (-8.0 * (h + 1) / N_HEADS) for h in range(N_HEADS))
LOG2E = 1.4426950408889634
Q_SCALE = HEAD_DIM ** -0.5 * LOG2E
VMEM_LIMIT = 56 * 1024 * 1024


def _dot(a, b):
    return jnp.dot(a, b, preferred_element_type=F32)


def _dot_nt(a, b):
    return lax.dot_general(a, b, (((1,), (1,)), ((), ())), preferred_element_type=F32)


def _rms(x, g):
    return x * lax.rsqrt(jnp.mean(x * x, axis=-1, keepdims=True) + EPS) * g


def _iota(shape, dim):
    return lax.broadcasted_iota(jnp.int32, shape, dim)


def _slope_row(g, nq):
    return jnp.concatenate(
        [jnp.full((1, nq), SLOPES[HPG * g + j] * LOG2E, F32) for j in range(HPG)], axis=1)


def _slope_feature_rows(slope_row, n_rows):
    hi = slope_row.astype(BF16).astype(F32)
    r = _iota((n_rows, slope_row.shape[1]), 0)
    return jnp.where(r == 0, hi, jnp.where(r == 1, slope_row - hi, 0.0)).astype(BF16)


def _chunk_slabs(st, chunk):
    return [st[chunk * u:chunk * (u + 1)] for u in range(st.shape[0] // chunk)]


def _col_max(st, dls, chunk):
    parts = [s.reshape(chunk // 8, 8, s.shape[1]).max(axis=0) - dl
             for s, dl in zip(_chunk_slabs(st, chunk), dls)]
    return functools.reduce(jnp.maximum, parts).max(axis=0, keepdims=True)


def _probs(st, dls, m, chunk):
    return jnp.concatenate([jnp.exp2(s - (m + dl)).astype(BF16)
                            for s, dl in zip(_chunk_slabs(st, chunk), dls)], axis=0)


def _inproj_kernel(x_ref, g_ref, lng_ref, wqt_ref, wkvc_ref, wk_ref, wvt_ref, wuv_ref, wgatet_ref,
                   qt_ref, kvc_ref, kaug_ref, vt_ref, u_ref, vn_ref, gatest_ref, *, tm):
    xn = _rms(x_ref[...], g_ref[...]).astype(BF16)
    qt_ref[0] = (_dot_nt(wqt_ref[...], xn) * Q_SCALE).astype(BF16)
    kvc_ref[...] = _dot(xn, wkvc_ref[...])
    lane = _iota((tm, LANES), 1)
    key_feat = jnp.where((lane == HEAD_DIM) | (lane == HEAD_DIM + 1),
                         _iota((tm, LANES), 0) & (KEY_CHUNK - 1), 0).astype(F32)
    ones_row = jnp.where(_iota((VT_ROWS, tm), 0) == HEAD_DIM, 1.0, 0.0)
    for a in range(2 * N_GROUPS):
        kaug_ref[0, a] = (_dot(xn, wk_ref[a]) + key_feat).astype(BF16)
        vt_ref[0, a] = (_dot_nt(wvt_ref[a], xn) + ones_row).astype(BF16)
    uv = jax.nn.gelu(_dot(xn, wuv_ref[...]))
    u_ref[...] = uv[:, :GMLP_WIDTH]
    v = uv[:, GMLP_WIDTH:]
    vc = v - jnp.mean(v, axis=-1, keepdims=True)
    vn = vc * lax.rsqrt(jnp.mean(vc * vc, axis=-1, keepdims=True) + EPS) * lng_ref[...]
    vn_ref[...] = vn.astype(BF16)
    gatest_ref[0] = jax.nn.sigmoid(_dot_nt(wgatet_ref[...], xn))


def _inproj(x2, norm_g, ln_g, wqt, wkvc, wk, wvt, wuv, wgatet, tm, bsz, t):
    n = x2.shape[0]
    tpb = t // tm
    row = lambda w: pl.BlockSpec((tm, w), lambda i: (i, 0))
    full = lambda a: pl.BlockSpec(a.shape, lambda i: (0,) * a.ndim)
    return pl.pallas_call(
        functools.partial(_inproj_kernel, tm=tm),
        grid=(n // tm,),
        in_specs=[row(D_MODEL), full(norm_g), full(ln_g), full(wqt), full(wkvc), full(wk),
                  full(wvt), full(wuv), full(wgatet)],
        out_specs=[pl.BlockSpec((1, NSA_W, tm), lambda i: (i // tpb, 0, i % tpb)),
                   row(KV_W),
                   pl.BlockSpec((1, 2 * N_GROUPS, tm, LANES), lambda i: (i // tpb, 0, i % tpb, 0)),
                   pl.BlockSpec((1, 2 * N_GROUPS, VT_ROWS, tm), lambda i: (i // tpb, 0, 0, i % tpb)),
                   row(GMLP_WIDTH), row(GMLP_WIDTH),
                   pl.BlockSpec((1, GATET_ROWS, tm), lambda i: (i // tpb, 0, i % tpb))],
        out_shape=[jax.ShapeDtypeStruct((bsz, NSA_W, t), BF16),
                   jax.ShapeDtypeStruct((n, KV_W), F32),
                   jax.ShapeDtypeStruct((bsz, 2 * N_GROUPS, t, LANES), BF16),
                   jax.ShapeDtypeStruct((bsz, 2 * N_GROUPS, VT_ROWS, t), BF16),
                   jax.ShapeDtypeStruct((n, GMLP_WIDTH), F32),
                   jax.ShapeDtypeStruct((n, GMLP_WIDTH), BF16),
                   jax.ShapeDtypeStruct((bsz, GATET_ROWS, t), F32)],
        compiler_params=pltpu.CompilerParams(dimension_semantics=("arbitrary",),
                                             vmem_limit_bytes=VMEM_LIMIT),
        name="inproj",
    )(x2, norm_g, ln_g, wqt, wkvc, wk, wvt, wuv, wgatet)


def _compress_kernel(xk_ref, xv_ref, pe_ref, w1_ref, w2k_ref, w2vt_ref, kc_ref, vct_ref, *, nc):
    outs = []
    for j, x_ref in enumerate((xk_ref, xv_ref)):
        a = jnp.zeros((nc, 2 * CMP_HIDDEN), F32)
        b = jnp.zeros((nc, 2 * CMP_HIDDEN), F32)
        for t in range(CMP_STRIDE):
            xt = x_ref[0, pl.ds(t, nc, stride=CMP_STRIDE), :]
            a = a + _dot((xt + pe_ref[j, t:t + 1, :]).astype(BF16), w1_ref[j, t])
            b = b + _dot((xt + pe_ref[j, CMP_STRIDE + t:CMP_STRIDE + t + 1, :]).astype(BF16),
                         w1_ref[j, CMP_STRIDE + t])
        outs.append(jax.nn.gelu(a + pltpu.roll(b, nc - 1, 0)).astype(BF16))
    k2 = _dot(outs[0], w2k_ref[...])
    v_t = _dot_nt(w2vt_ref[...], outs[1])
    lane = _iota((nc, LANES), 1)
    key_feat = jnp.where((lane == HEAD_DIM) | (lane == HEAD_DIM + 1),
                         CMP_STRIDE * (_iota((nc, LANES), 0) & (CMP_KEY_CHUNK - 1)), 0).astype(F32)
    ci =_iota((N_BLK_PAD, nc), 1) * CMP_STRIDE
    sj = _iota((N_BLK_PAD, nc), 0) * SEL_BLOCK
    overlap_t = jnp.where((ci < sj + SEL_BLOCK) & (ci + (CMP_BLOCK - 1) >= sj), 1.0, 0.0).astype(BF16)
    ones_rows = jnp.where(_iota((VT_ROWS - HEAD_DIM, nc), 0) == 0, 1.0, 0.0).astype(BF16)
    for g in range(N_GROUPS):
        kg = k2 if g == 0 else pltpu.roll(k2, HEAD_DIM, 1)
        kc_ref[0, g] = jnp.where(lane < HEAD_DIM, kg, key_feat).astype(BF16)
        vct_ref[0, g, 0:HEAD_DIM, :] = v_t[HEAD_DIM * g:HEAD_DIM * (g + 1), :].astype(BF16)
        vct_ref[0, g, HEAD_DIM:VT_ROWS, :] = ones_rows
        vct_ref[0, g, VT_ROWS:VCT_ROWS, :] = overlap_t


def _compress(kvc3, pe2, w1bd, w2k, w2vt):
    bsz, t, _ = kvc3.shape
    nc = t // CMP_STRIDE
    full = lambda a: pl.BlockSpec(a.shape, lambda b: (0,) * a.ndim)
    return pl.pallas_call(
        functools.partial(_compress_kernel, nc=nc),
        grid=(bsz,),
        in_specs=[pl.BlockSpec((1, t, LANES), lambda b: (b, 0, 0)),
                  pl.BlockSpec((1, t, LANES), lambda b: (b, 0, 1)),
                  full(pe2), full(w1bd), full(w2k), full(w2vt)],
        out_specs=[pl.BlockSpec((1, N_GROUPS, nc, LANES), lambda b: (b, 0, 0, 0)),
                   pl.BlockSpec((1, N_GROUPS, VCT_ROWS, nc), lambda b: (b, 0, 0, 0))],
        out_shape=[jax.ShapeDtypeStruct((bsz, N_GROUPS, nc, LANES), BF16),
                   jax.ShapeDtypeStruct((bsz, N_GROUPS, VCT_ROWS, nc), BF16)],
        compiler_params=pltpu.CompilerParams(dimension_semantics=("arbitrary",),
                                             vmem_limit_bytes=VMEM_LIMIT),
        name="compress",
    )(kvc3, kvc3, pe2, w1bd, w2k, w2vt)


def _cmp_topk_kernel(qt_ref, kc_ref, vct_ref, gatest_ref, ocmp_ref, selt_ref, flags_ref,
                     m_ref, acc_ref, *, ncp):
    qb = pl.program_id(1)
    start = qb * Q_BLOCK
    n_chunks = ncp // CMP_KEY_CHUNK
    nck = (qb * (Q_BLOCK // CMP_STRIDE) + (Q_BLOCK - CMP_BLOCK) // CMP_STRIDE) // CMP_KEY_CHUNK + 1
    chunk_tokens = CMP_KEY_CHUNK * CMP_STRIDE
    tile_heads = lambda a: jnp.concatenate([a] * HPG, axis=1)
    blk = _iota((N_BLK_PAD, Q_BLOCK), 0)
    blk_f = blk.astype(F32)
    t_row = start + _iota((1, Q_BLOCK), 1)
    cur = lax.shift_right_logical(t_row, 6)
    causal = blk <= cur
    forced = (blk == 0) | (blk == cur) | (blk == cur - 1)
    has_key = t_row >= CMP_BLOCK - 1
    gt = gatest_ref[0]
    ones8 = jnp.ones((8, Q_BLOCK), F32)
    tail_chunks = min(2, n_chunks)
    head_chunks = n_chunks - tail_chunks
    tail_c0 = jnp.maximum(nck - tail_chunks, 0)
    tail_rows = pl.ds(pl.multiple_of(tail_c0 * CMP_KEY_CHUNK, CMP_KEY_CHUNK), tail_chunks * CMP_KEY_CHUNK)
    key_end = (CMP_STRIDE * (tail_c0 * CMP_KEY_CHUNK + _iota((tail_chunks * CMP_KEY_CHUNK, Q_BLOCK), 0))
               + (CMP_BLOCK - 1) - start)
    tail_bias = tile_heads(jnp.where(key_end <= _iota((tail_chunks * CMP_KEY_CHUNK, Q_BLOCK), 1), 0.0, NEG))

    per_head = []
    for g in range(N_GROUPS):
        slope_row = _slope_row(g, Q_BLOCK)
        qa = jnp.concatenate(
            [jnp.concatenate([qt_ref[0, HEAD_DIM * (HPG * g + j):HEAD_DIM * (HPG * g + j + 1), :]
                              for j in range(HPG)], axis=1),
             _slope_feature_rows(slope_row, LANES - HEAD_DIM)], axis=0)

        def delta(c):
            return slope_row * (start - c * chunk_tokens).astype(F32)

        if head_chunks > 0:
            @pl.when(nck > tail_chunks)
            def _():
                st = _dot(kc_ref[0, g, 0:head_chunks * CMP_KEY_CHUNK, :], qa)
                dls = [delta(c) + jnp.where(c < nck - tail_chunks, 0.0, MASK_BIG)
                       for c in range(head_chunks)]
                m1 = _col_max(st, dls, CMP_KEY_CHUNK)
                m_ref[g] = m1
                acc_ref[g] = _dot(vct_ref[0, g, :, 0:head_chunks * CMP_KEY_CHUNK],
                                  _probs(st, dls, m1, CMP_KEY_CHUNK))

            @pl.when(nck <= tail_chunks)
            def _():
                m_ref[g] = jnp.full(m_ref.shape[1:], NEG, F32)
                acc_ref[g] = jnp.zeros(acc_ref.shape[1:], F32)
            m_old = m_ref[g]
            acc_old = acc_ref[g]
        else:
            m_old = jnp.full((1, HPG * Q_BLOCK), NEG, F32)
            acc_old = jnp.zeros((VCT_ROWS, HPG * Q_BLOCK), F32)

        st = _dot(kc_ref[0, g, tail_rows, :], qa) + tail_bias
        dls = [delta(tail_c0 + u) for u in range(tail_chunks)]
        m_new = jnp.maximum(m_old, _col_max(st, dls, CMP_KEY_CHUNK))
        acc = (jnp.exp2(m_old - m_new) * acc_old
               + _dot(vct_ref[0, g, :, tail_rows], _probs(st, dls, m_new, CMP_KEY_CHUNK)))

        inv_l = 1.0 / jnp.maximum(acc[HEAD_DIM:HEAD_DIM + 1], 1e-30)
        o = acc[0:HEAD_DIM] * inv_l
        imp_h = acc[VT_ROWS:VCT_ROWS] * inv_l
        imp = sum(imp_h[:, Q_BLOCK * j:Q_BLOCK * (j + 1)] for j in range(HPG))
        imp = jnp.where(has_key, imp, 0.0)
        for j in range(HPG):
            h = HPG * g + j
            per_head.append(jnp.where(has_key, gt[3 * h:3 * h + 1, :] * o[:, Q_BLOCK * j:Q_BLOCK * (j + 1)], 0.0))

        sel = jnp.where(forced, 1.0, 0.0)
        rank = jnp.where(causal, jnp.where(forced, REMOVED, imp), NEG)
        for _ in range(N_SELECT - N_FORCED):
            m = jnp.max(rank, axis=0, keepdims=True)
            idx = jnp.min(jnp.where(rank == m, blk_f, float(N_BLK_PAD)), axis=0, keepdims=True)
            chosen = blk_f == idx
            sel = jnp.where(chosen, 1.0, sel)
            rank = jnp.where(chosen, REMOVED, rank)
        sel = jnp.where(causal, sel, 0.0)
        selt_ref[0, g] = sel.astype(BF16)
        flags_ref[0, 0, g] = (_dot_nt(ones8, sel) > 0.5).astype(jnp.int32)
    ocmp_ref[0] = jnp.concatenate(per_head, axis=0)


def _cmp_topk(qt, kc, vct, gatest):
    bsz, _, t = qt.shape
    ncp = t // CMP_STRIDE
    nqb = t // Q_BLOCK
    return pl.pallas_call(
        functools.partial(_cmp_topk_kernel, ncp=ncp),
        grid=(bsz, nqb),
        in_specs=[pl.BlockSpec((1, NSA_W, Q_BLOCK), lambda b, i: (b, 0, i)),
                  pl.BlockSpec((1, N_GROUPS, ncp, LANES), lambda b, i: (b, 0, 0, 0)),
                  pl.BlockSpec((1, N_GROUPS, VCT_ROWS, ncp), lambda b, i: (b, 0, 0, 0)),
                  pl.BlockSpec((1, GATET_ROWS, Q_BLOCK), lambda b, i: (b, 0, i))],
        out_specs=[pl.BlockSpec((1, NSA_W, Q_BLOCK), lambda b, i: (b, 0, i)),
                   pl.BlockSpec((1, N_GROUPS, N_BLK_PAD, Q_BLOCK), lambda b, i: (b, 0, 0, i)),
                   pl.BlockSpec((1, 1, N_GROUPS, 8, N_BLK_PAD), lambda b, i: (b, i, 0, 0, 0))],
        out_shape=[jax.ShapeDtypeStruct((bsz, NSA_W, t), F32),
                   jax.ShapeDtypeStruct((bsz, N_GROUPS, N_BLK_PAD, t), BF16),
                   jax.ShapeDtypeStruct((bsz, nqb, N_GROUPS, 8, N_BLK_PAD), jnp.int32)],
        scratch_shapes=[pltpu.VMEM((N_GROUPS, 1, HPG * Q_BLOCK), F32),
                        pltpu.VMEM((N_GROUPS, VCT_ROWS, HPG * Q_BLOCK), F32)],
        compiler_params=pltpu.CompilerParams(dimension_semantics=("arbitrary", "arbitrary"),
                                             vmem_limit_bytes=VMEM_LIMIT),
        name="cmp_topk",
    )(qt, kc, vct, gatest)


def _slc_win_kernel(counts_ref, lists_ref, qt_ref, kaug_ref, vt_ref, oh_ref, selt_ref, gatest_ref,
                    ocmp_ref, out_ref, qaug_ref, m_ref, acc_ref, *, nqb):
    b = pl.program_id(0)
    qb = pl.program_id(1)
    step_id = b * nqb + qb
    start = qb * NSA_QB
    width = HPG * NSA_QB
    win_keys = WINDOW + NSA_QB
    win_start = jnp.maximum(start - WINDOW, 0)
    tile_heads = lambda a: jnp.concatenate([a] * HPG, axis=1)
    dist = (start - win_start) + _iota((win_keys, NSA_QB), 1) - _iota((win_keys, NSA_QB), 0)
    win_bias = tile_heads(jnp.where((dist >= 0) & (dist < WINDOW), 0.0, NEG))
    own_bias = tile_heads(jnp.where(_iota((NSA_QB, NSA_QB), 0) <= _iota((NSA_QB, NSA_QB), 1), 0.0, NEG))
    gt = gatest_ref[0]
    slope_rows = [_slope_row(g, NSA_QB) for g in range(N_GROUPS)]

    def normalize(acc):
        return acc[0:HEAD_DIM] / jnp.maximum(acc[HEAD_DIM:HEAD_DIM + 1], 1e-30)

    o_win = []
    for g in range(N_GROUPS):
        qaug_ref[g, 0:HEAD_DIM, :] = jnp.concatenate(
            [qt_ref[0, HEAD_DIM * (HPG * g + j):HEAD_DIM * (HPG * g + j + 1), :] for j in range(HPG)],
            axis=1)
        qaug_ref[g, HEAD_DIM:LANES, :] = _slope_feature_rows(slope_rows[g], LANES - HEAD_DIM)
        sel_bias = ((selt_ref[0, g].astype(F32) - 1.0) * MASK_BIG).astype(BF16)
        qaug_ref[g, LANES:2 * LANES, :] = tile_heads(sel_bias)

        a = N_GROUPS + g
        rows = pl.ds(pl.multiple_of(win_start, KEY_CHUNK), win_keys)
        st = _dot(kaug_ref[0, a, rows, :], qaug_ref[g, 0:LANES, :]) + win_bias
        dls = [slope_rows[g] * (start - win_start - KEY_CHUNK * u).astype(F32)
               for u in range(win_keys // KEY_CHUNK)]
        m = _col_max(st, dls, KEY_CHUNK)
        o_win.append(normalize(_dot(vt_ref[0, a, :, rows], _probs(st, dls, m, KEY_CHUNK))))

        rows = pl.ds(pl.multiple_of(start, KEY_CHUNK), NSA_QB)
        st = _dot(jnp.concatenate([kaug_ref[0, g, rows, :], oh_ref[rows, :]], axis=1), qaug_ref[g])
        st = st + own_bias
        dls = [slope_rows[g] * float(-KEY_CHUNK * u) for u in range(NSA_QB // KEY_CHUNK)]
        m = _col_max(st, dls, KEY_CHUNK)
        m_ref[g] = m
        acc_ref[g] = _dot(vt_ref[0, g, :, rows], _probs(st, dls, m, KEY_CHUNK))

    def slc_body(i, carry):
        for g in range(N_GROUPS):
            word = lists_ref[(step_id * N_GROUPS + g) * LIST_WORDS + i]
            ks, vs, dls = [], [], []
            for u in range(SLC_BATCH):
                cid = lax.shift_right_logical(word, 8 * u) & 255
                valid = cid < N_BLK_PAD // 2
                c = jnp.where(valid, cid, 0)
                rows = pl.ds(pl.multiple_of(c * KEY_CHUNK, KEY_CHUNK), KEY_CHUNK)
                ks.append(jnp.concatenate([kaug_ref[0, g, rows, :], oh_ref[rows, :]], axis=1))
                vs.append(vt_ref[0, g, :, rows])
                dls.append(slope_rows[g] * (start - c * KEY_CHUNK).astype(F32)
                           + jnp.where(valid, 0.0, MASK_BIG))
            st = _dot(jnp.concatenate(ks, axis=0), qaug_ref[g])
            m_old = m_ref[g]
            m_new = jnp.maximum(m_old, _col_max(st, dls, KEY_CHUNK))
            acc_ref[g] = (jnp.exp2(m_old - m_new) * acc_ref[g]
                          + _dot(jnp.concatenate(vs, axis=1), _probs(st, dls, m_new, KEY_CHUNK)))
            m_ref[g] = m_new
        return carry

    lax.fori_loop(0, counts_ref[step_id], slc_body, 0)

    per_head = []
    for g in range(N_GROUPS):
        o_slc = normalize(acc_ref[g])
        for j in range(HPG):
            h = HPG * g + j
            lanes = slice(NSA_QB * j, NSA_QB * (j + 1))
            per_head.append(gt[3 * h + 1:3 * h + 2, :] * o_slc[:, lanes]
                            + gt[3 * h + 2:3 * h + 3, :] * o_win[g][:, lanes])
    o_t = jnp.concatenate(per_head, axis=0)
    out_ref[0] = (ocmp_ref[0] + o_t).T.astype(BF16)


def _slc_win(counts, lists, qt, kaug, vt, onehot, selt, gatest, ocmp):
    bsz, _, t = qt.shape
    nqb = t // NSA_QB
    once = lambda shape, imap: pl.BlockSpec(shape, imap, pipeline_mode=pl.Buffered(1))
    grid_spec = pltpu.PrefetchScalarGridSpec(
        num_scalar_prefetch=2,
        grid=(bsz, nqb),
        in_specs=[pl.BlockSpec((1, NSA_W, NSA_QB), lambda b, i, *_: (b, 0, i)),
                  once((1, 2 * N_GROUPS, t, LANES), lambda b, i, *_: (b, 0, 0, 0)),
                  once((1, 2 * N_GROUPS, VT_ROWS, t), lambda b, i, *_: (b, 0, 0, 0)),
                  once((t, N_BLK_PAD), lambda b, i, *_: (0, 0)),
                  pl.BlockSpec((1, N_GROUPS, N_BLK_PAD, NSA_QB), lambda b, i, *_: (b, 0, 0, i)),
                  pl.BlockSpec((1, GATET_ROWS, NSA_QB), lambda b, i, *_: (b, 0, i)),
                  pl.BlockSpec((1, NSA_W, NSA_QB), lambda b, i, *_: (b, 0, i))],
        out_specs=pl.BlockSpec((1, NSA_QB, NSA_W), lambda b, i, *_: (b, i, 0)),
        scratch_shapes=[pltpu.VMEM((N_GROUPS, 2 * LANES, HPG * NSA_QB), BF16),
                        pltpu.VMEM((N_GROUPS, 1, HPG * NSA_QB), F32),
                        pltpu.VMEM((N_GROUPS, VT_ROWS, HPG * NSA_QB), F32)],
    )
    return pl.pallas_call(
        functools.partial(_slc_win_kernel, nqb=nqb),
        grid_spec=grid_spec,
        out_shape=jax.ShapeDtypeStruct((bsz, t, NSA_W), BF16),
        compiler_params=pltpu.CompilerParams(dimension_semantics=("arbitrary", "arbitrary"),
                                             vmem_limit_bytes=VMEM_LIMIT),
        name="slc_win",
    )(counts, lists, qt, kaug, vt, onehot, selt, gatest, ocmp)


def _merge_kernel(x_ref, onsa_ref, u_ref, vn_ref, g_ref, wm_ref, bm_ref, ws_ref, bs_ref,
                  wpa_ref, wpb_ref, wo_ref, h_ref, *, tm):
    x = x_ref[...]
    xn = _rms(x, g_ref[...]).astype(BF16)
    tril = _iota((GMLP_CHUNK, GMLP_CHUNK), 0) >= _iota((GMLP_CHUNK, GMLP_CHUNK), 1)
    sgu_rows = []
    for c in range(tm // GMLP_CHUNK):
        rows = slice(GMLP_CHUNK * c, GMLP_CHUNK * (c + 1))
        cols = []
        for g in range(GMLP_GROUPS):
            lanes = slice(LANES * g, LANES * (g + 1))
            w = jnp.where(tril, ws_ref[g], 0.0).astype(BF16)
            cols.append(_dot(w, vn_ref[rows, lanes]) + bs_ref[:, g:g + 1])
        sgu_rows.append(u_ref[rows, :] * jnp.concatenate(cols, axis=1))
    o_sgu = jnp.concatenate(sgu_rows, axis=0).astype(BF16)
    mg = jax.nn.sigmoid(_dot(xn, wm_ref[...]) + bm_ref[...])
    mixed = (mg[:, :D_MODEL] * _dot(onsa_ref[...], wpa_ref[...])
             + mg[:, D_MODEL:] * _dot(o_sgu, wpb_ref[...]))
    h_ref[...] = x + _dot(mixed.astype(BF16), wo_ref[...])


def _merge(x2, onsa2, u2, vn2, norm_g, wm, bm, ws, bs_t, wpa, wpb, wo, tm):
    n = x2.shape[0]
    row = lambda w: pl.BlockSpec((tm, w), lambda i: (i, 0))
    full = lambda a: pl.BlockSpec(a.shape, lambda i: (0,) * a.ndim)
    return pl.pallas_call(
        functools.partial(_merge_kernel, tm=tm),
        grid=(n // tm,),
        in_specs=[row(D_MODEL), row(NSA_W), row(GMLP_WIDTH), row(GMLP_WIDTH), full(norm_g),
                  full(wm), full(bm), full(ws), full(bs_t), full(wpa), full(wpb), full(wo)],
        out_specs=row(D_MODEL),
        out_shape=jax.ShapeDtypeStruct((n, D_MODEL), F32),
        compiler_params=pltpu.CompilerParams(dimension_semantics=("arbitrary",),
                                             vmem_limit_bytes=VMEM_LIMIT),
        name="merge",
    )(x2, onsa2, u2, vn2, norm_g, wm, bm, ws, bs_t, wpa, wpb, wo)


def _memkv_kernel(mem_ref, g_ref, w_ref, out_ref):
    out_ref[0] = _dot(_rms(mem_ref[0], g_ref[...]).astype(BF16), w_ref[...]).astype(BF16)


def _memkv(mem, norm_g, w):
    bsz, nm, _ = mem.shape
    return pl.pallas_call(
        _memkv_kernel,
        grid=(bsz,),
        in_specs=[pl.BlockSpec((1, nm, D_MODEL), lambda b: (b, 0, 0)),
                  pl.BlockSpec(norm_g.shape, lambda b: (0, 0)),
                  pl.BlockSpec(w.shape, lambda b: (0, 0))],
        out_specs=pl.BlockSpec((1, nm, 2 * MEM_W), lambda b: (b, 0, 0)),
        out_shape=jax.ShapeDtypeStruct((bsz, nm, 2 * MEM_W), BF16),
        compiler_params=pltpu.CompilerParams(dimension_semantics=("arbitrary",),
                                             vmem_limit_bytes=VMEM_LIMIT),
        name="memkv",
    )(mem, norm_g, w)


def _xattn_kernel(h_ref, g_ref, wq_ref, mkv_ref, wo_ref, out_ref):
    h = h_ref[...]
    hq = _dot(_rms(h, g_ref[...]).astype(BF16), wq_ref[...]).astype(BF16)
    heads = []
    for a in range(MEM_HEADS):
        lanes = slice(MEM_HEAD_DIM * a, MEM_HEAD_DIM * (a + 1))
        k = mkv_ref[0, :, lanes]
        v = mkv_ref[0, :, MEM_W + MEM_HEAD_DIM * a:MEM_W + MEM_HEAD_DIM * (a + 1)]
        s = _dot_nt(hq[:, lanes], k) * (MEM_HEAD_DIM ** -0.5)
        e = jnp.exp(s - jnp.max(s, axis=-1, keepdims=True))
        p = e / jnp.sum(e, axis=-1, keepdims=True)
        heads.append(_dot(p.astype(BF16), v))
    o = jnp.concatenate(heads, axis=1).astype(BF16)
    out_ref[...] = h + _dot(o, wo_ref[...])


def _xattn(h2d, norm_g, wq, mkv, wo, tm, rows_per_batch):
    n = h2d.shape[0]
    nm = mkv.shape[1]
    tiles_per_batch = rows_per_batch // tm
    full = lambda a: pl.BlockSpec(a.shape, lambda i: (0,) * a.ndim)
    return pl.pallas_call(
        _xattn_kernel,
        grid=(n // tm,),
        in_specs=[pl.BlockSpec((tm, D_MODEL), lambda i: (i, 0)), full(norm_g), full(wq),
                  pl.BlockSpec((1, nm, 2 * MEM_W), lambda i: (i // tiles_per_batch, 0, 0)),
                  full(wo)],
        out_specs=pl.BlockSpec((tm, D_MODEL), lambda i: (i, 0)),
        out_shape=jax.ShapeDtypeStruct((n, D_MODEL), F32),
        compiler_params=pltpu.CompilerParams(dimension_semantics=("arbitrary",),
                                             vmem_limit_bytes=VMEM_LIMIT),
        name="xattn",
    )(h2d, norm_g, wq, mkv, wo)


def _ffn_kernel(h_ref, g_ref, wgu_ref, wd_ref, gf_ref, out_ref, *, d_ff):
    h = h_ref[...]
    hn = _rms(h, g_ref[...]).astype(BF16)
    gate = _dot(hn, wgu_ref[:, :d_ff])
    up = _dot(hn, wgu_ref[:, d_ff:])
    act = (jax.nn.silu(gate) * up).astype(BF16)
    y = h + _dot(act, wd_ref[...])
    out_ref[...] = _rms(y, gf_ref[...])


def _ffn(h2d, norm_g, wgu, wd, norm_f, tm):
    n = h2d.shape[0]
    d_ff = wd.shape[0]
    full = lambda a: pl.BlockSpec(a.shape, lambda i: (0,) * a.ndim)
    once = lambda a: pl.BlockSpec(a.shape, lambda i: (0,) * a.ndim, pipeline_mode=pl.Buffered(1))
    return pl.pallas_call(
        functools.partial(_ffn_kernel, d_ff=d_ff),
        grid=(n // tm,),
        in_specs=[pl.BlockSpec((tm, D_MODEL), lambda i: (i, 0)), full(norm_g), once(wgu), once(wd),
                  full(norm_f)],
        out_specs=pl.BlockSpec((tm, D_MODEL), lambda i: (i, 0)),
        out_shape=jax.ShapeDtypeStruct((n, D_MODEL), F32),
        compiler_params=pltpu.CompilerParams(dimension_semantics=("arbitrary",),
                                             vmem_limit_bytes=VMEM_LIMIT),
        name="ffn",
    )(h2d, norm_g, wgu, wd, norm_f)


def _block_diag2(w):
    z = jnp.zeros_like(w)
    return jnp.concatenate([jnp.concatenate([w, z], axis=-1), jnp.concatenate([z, w], axis=-1)], axis=-2)


def _chunk_lists(flags):
    bsz = flags.shape[0]
    n_chunks = N_BLK_PAD // 2
    per_step = NSA_QB // Q_BLOCK
    nqb = flags.shape[1] // per_step
    f = flags[:, :, :, 0, :].reshape(bsz, nqb, per_step, N_GROUPS, n_chunks, 2).max(axis=(2, 5))
    cid = jnp.arange(n_chunks, dtype=jnp.int32)
    own = (NSA_QB // KEY_CHUNK) * jnp.arange(nqb, dtype=jnp.int32)[None, :, None, None]
    active = (f > 0) & (cid < own)
    n_active = active.sum(axis=-1)
    slot = jnp.cumsum(active, axis=-1) - 1
    hit = active[..., :, None] & (slot[..., :, None] == cid)
    ids = jnp.sum(jnp.where(hit, cid[:, None], 0), axis=-2)
    ids = jnp.where(cid < n_active[..., None], ids, 255)
    n_batches = (n_active.max(axis=-1) + SLC_BATCH - 1) // SLC_BATCH
    packed = ids.reshape(bsz, nqb, N_GROUPS, LIST_WORDS, SLC_BATCH)
    words = functools.reduce(jnp.bitwise_or, [packed[..., u] << (8 * u) for u in range(SLC_BATCH)])
    return n_batches.reshape(-1).astype(jnp.int32), words.reshape(-1).astype(jnp.int32)


def kernel(x, mem, norm_mix, w_in, w_cmp_k1, w_cmp_k2, w_cmp_v1, w_cmp_v2, pe_cmp_k, pe_cmp_v, ln_sgu, w_spatial, b_spatial, w_proj_a, w_proj_b, w_merge, b_merge, w_out, norm_mem_q, norm_mem_kv, w_mq, w_mkv, w_mo, norm_ffn, w_gate_up, w_down, norm_final):
    bsz, t, d = x.shape
    depth = norm_mix.shape[0]
    assert d == D_MODEL and t % Q_BLOCK == 0 and t // SEL_BLOCK <= N_BLK_PAD
    assert t // SEL_BLOCK >= N_SELECT and depth == 1 and t % 256 == 0
    n = bsz * t
    tm = 256
    h = x.reshape(n, d)
    c0, c1, c2, c3 = NSA_W, NSA_W + KV_W, NSA_W + 3 * KV_W, NSA_W + 3 * KV_W + 2 * GMLP_WIDTH
    onehot = (jnp.arange(t)[:, None] // SEL_BLOCK == jnp.arange(N_BLK_PAD)[None, :]).astype(BF16)
    pad_cols = lambda w, width: jnp.pad(w, ((0, 0), (0, width - w.shape[1])))
    for l in range(depth):
        wi = w_in[l]
        wqt = wi[:, :c0].T.astype(BF16)
        wkvc = wi[:, c0:c1].astype(BF16)
        wk, wvt = [], []
        for branch in range(2):
            base = c1 + KV_W * branch
            for g in range(N_GROUPS):
                wk.append(pad_cols(wi[:, base + HEAD_DIM * g:base + HEAD_DIM * (g + 1)], LANES))
                v0 = base + N_GROUPS * HEAD_DIM + HEAD_DIM * g
                wvt.append(pad_cols(wi[:, v0:v0 + HEAD_DIM], VT_ROWS).T)
        wk = jnp.stack(wk).astype(BF16)
        wvt = jnp.stack(wvt).astype(BF16)
        wuv = wi[:, c2:c3].astype(BF16)
        wgatet = pad_cols(wi[:, c3:], GATET_ROWS).T.astype(BF16)
        qt, kvc2, kaug, vt, u2, vn2, gatest = _inproj(
            h, norm_mix[l][None], ln_sgu[l][None], wqt, wkvc, wk, wvt, wuv, wgatet, tm, bsz, t)

        pe2 = jnp.stack([pe_cmp_k[l], pe_cmp_v[l]])
        pe2 = jnp.concatenate([pe2, pe2], axis=-1)
        w1 = jnp.stack([w_cmp_k1[l], w_cmp_v1[l]]).reshape(2, CMP_BLOCK, HEAD_DIM, CMP_HIDDEN)
        w1bd = _block_diag2(w1).astype(BF16)
        w2k = _block_diag2(w_cmp_k2[l]).astype(BF16)
        w2vt = _block_diag2(w_cmp_v2[l]).T.astype(BF16)
        kc, vct = _compress(kvc2.reshape(bsz, t, KV_W), pe2, w1bd, w2k, w2vt)
        ocmp, selt, flags = _cmp_topk(qt, kc, vct, gatest)
        counts, lists = _chunk_lists(flags)
        onsa = _slc_win(counts, lists, qt, kaug, vt, onehot, selt, gatest, ocmp)

        h = _merge(h, onsa.reshape(n, NSA_W), u2, vn2, norm_mix[l][None], w_merge[l].astype(BF16),
                   b_merge[l][None], w_spatial[l], b_spatial[l].T, w_proj_a[l].astype(BF16),
                   w_proj_b[l].astype(BF16), w_out[l].astype(BF16), tm)

        mkv = _memkv(mem, norm_mem_kv[l][None], w_mkv[l].astype(BF16))
        h = _xattn(h, norm_mem_q[l][None], w_mq[l].astype(BF16), mkv, w_mo[l].astype(BF16), tm, t)
        h = _ffn(h, norm_ffn[l][None], w_gate_up[l].astype(BF16), w_down[l].astype(BF16),
                 norm_final[None], tm)
    return h.reshape(bsz, t, d)
```

```python
import functools

import jax
import jax.numpy as jnp
from jax import lax
from jax.experimental import pallas as pl
from jax.experimental.pallas import tpu as pltpu

F32 = jnp.float32
BF16 = jnp.bfloat16

LANES = 128
D_MODEL = 1024
N_HEADS = 8
HEAD_DIM = 64
N_GROUPS = 2
HPG = N_HEADS // N_GROUPS
CMP_BLOCK = 32
CMP_STRIDE = 16
CMP_HIDDEN = 128
SEL_BLOCK = 64
N_SELECT = 16
WINDOW = 512
Q_BLOCK = 256
KEY_CHUNK = 128
N_BLK_PAD = 128
NSA_W = N_HEADS * HEAD_DIM
KV_W = 2 * N_GROUPS * HEAD_DIM
GMLP_WIDTH = 512
GMLP_GROUPS = 4
GMLP_CHUNK = 128
MEM_HEADS = 4
MEM_HEAD_DIM = 128
MEM_W = MEM_HEADS * MEM_HEAD_DIM
GATE_W = 3 * N_HEADS
NSA_QB = 256
SLC_BATCH = 4
LIST_WORDS = (N_BLK_PAD // 2) // SLC_BATCH
VT_ROWS = 80
VCT_ROWS = VT_ROWS + N_BLK_PAD
CMP_KEY_CHUNK = 128
N_FORCED = 3
GATET_ROWS = 32
MASK_BIG = 1e30
EPS = 1e-6
NEG = -1e30
REMOVED = -3e38
FORCE_SCORE = 1e6
SLOPES = tuple(2.0 ** (-8.0 * (h + 1) / N_HEADS) for h in range(N_HEADS))
LOG2E = 1.4426950408889634
Q_SCALE = HEAD_DIM ** -0.5 * LOG2E
VMEM_LIMIT = 56 * 1024 * 1024


def _dot(a, b):
    return jnp.dot(a, b, preferred_element_type=F32)


def _dot_nt(a, b):
    return lax.dot_general(a, b, (((1,), (1,)), ((), ())), preferred_element_type=F32)


def _rms(x, g):
    return x * lax.rsqrt(jnp.mean(x * x, axis=-1, keepdims=True) + EPS) * g


def _iota(shape, dim):
    return lax.broadcasted_iota(jnp.int32, shape, dim)


def _slope_row(g, nq):
    return jnp.concatenate(
        [jnp.full((1, nq), SLOPES[HPG * g + j] * LOG2E, F32) for j in range(HPG)], axis=1)


def _slope_feature_rows(slope_row, n_rows):
    hi = slope_row.astype(BF16).astype(F32)
    r = _iota((n_rows, slope_row.shape[1]), 0)
    return jnp.where(r == 0, hi, jnp.where(r == 1, slope_row - hi, 0.0)).astype(BF16)


def _chunk_slabs(st, chunk):
    return [st[chunk * u:chunk * (u + 1)] for u in range(st.shape[0] // chunk)]


def _col_max(st, dls, chunk):
    parts = [s.reshape(chunk // 8, 8, s.shape[1]).max(axis=0) - dl
             for s, dl in zip(_chunk_slabs(st, chunk), dls)]
    return functools.reduce(jnp.maximum, parts).max(axis=0, keepdims=True)


def _probs(st, dls, m, chunk):
    return jnp.concatenate([jnp.exp2(s - (m + dl)).astype(BF16)
                            for s, dl in zip(_chunk_slabs(st, chunk), dls)], axis=0)


def _inproj_kernel(x_ref, g_ref, lng_ref, wrow_ref, wnt_ref,
                   qt_ref, kvc_ref, kaug_ref, vt_ref, u_ref, vn_ref, gatest_ref, *, tm):
    xn = _rms(x_ref[...], g_ref[...]).astype(BF16)
    r = _dot(xn, wrow_ref[...])
    nt = _dot_nt(wnt_ref[...], xn)
    qt_ref[0] = (nt[0:NSA_W] * Q_SCALE).astype(BF16)
    kvc_ref[...] = r[:, 0:KV_W]
    lane = _iota((tm, LANES), 1)
    key_feat = jnp.where((lane == HEAD_DIM) | (lane == HEAD_DIM + 1),
                         _iota((tm, LANES), 0) & (KEY_CHUNK - 1), 0).astype(F32)
    ones_row = jnp.where(_iota((VT_ROWS, tm), 0) == HEAD_DIM, 1.0, 0.0)
    for a in range(2 * N_GROUPS):
        kaug_ref[0, a] = (r[:, KV_W + LANES * a:KV_W + LANES * (a + 1)] + key_feat).astype(BF16)
        vt_ref[0, a] = (nt[NSA_W + VT_ROWS * a:NSA_W + VT_ROWS * (a + 1)] + ones_row).astype(BF16)
    uv = jax.nn.gelu(r[:, KV_W + 2 * N_GROUPS * LANES:])
    u_ref[...] = uv[:, :GMLP_WIDTH]
    v = uv[:, GMLP_WIDTH:]
    vc = v - jnp.mean(v, axis=-1, keepdims=True)
    vn = vc * lax.rsqrt(jnp.mean(vc * vc, axis=-1, keepdims=True) + EPS) * lng_ref[...]
    vn_ref[...] = vn.astype(BF16)
    gatest_ref[0] = jax.nn.sigmoid(nt[NSA_W + 2 * N_GROUPS * VT_ROWS:])


def _inproj(x2, norm_g, ln_g, wrow, wnt, tm, bsz, t):
    n = x2.shape[0]
    tpb = t // tm
    row = lambda w: pl.BlockSpec((tm, w), lambda i: (i, 0))
    full = lambda a: pl.BlockSpec(a.shape, lambda i: (0,) * a.ndim)
    return pl.pallas_call(
        functools.partial(_inproj_kernel, tm=tm),
        grid=(n // tm,),
        in_specs=[row(D_MODEL), full(norm_g), full(ln_g), full(wrow), full(wnt)],
        out_specs=[pl.BlockSpec((1, NSA_W, tm), lambda i: (i // tpb, 0, i % tpb)),
                   row(KV_W),
                   pl.BlockSpec((1, 2 * N_GROUPS, tm, LANES), lambda i: (i // tpb, 0, i % tpb, 0)),
                   pl.BlockSpec((1, 2 * N_GROUPS, VT_ROWS, tm), lambda i: (i // tpb, 0, 0, i % tpb)),
                   row(GMLP_WIDTH), row(GMLP_WIDTH),
                   pl.BlockSpec((1, GATET_ROWS, tm), lambda i: (i // tpb, 0, i % tpb))],
        out_shape=[jax.ShapeDtypeStruct((bsz, NSA_W, t), BF16),
                   jax.ShapeDtypeStruct((n, KV_W), F32),
                   jax.ShapeDtypeStruct((bsz, 2 * N_GROUPS, t, LANES), BF16),
                   jax.ShapeDtypeStruct((bsz, 2 * N_GROUPS, VT_ROWS, t), BF16),
                   jax.ShapeDtypeStruct((n, GMLP_WIDTH), F32),
                   jax.ShapeDtypeStruct((n, GMLP_WIDTH), BF16),
                   jax.ShapeDtypeStruct((bsz, GATET_ROWS, t), F32)],
        compiler_params=pltpu.CompilerParams(dimension_semantics=("arbitrary",),
                                             vmem_limit_bytes=VMEM_LIMIT),
        name="inproj",
    )(x2, norm_g, ln_g, wrow, wnt)


def _compress_kernel(xk_ref, xv_ref, pe_ref, w1_ref, w2k_ref, w2vt_ref, kc_ref, vct_ref, *, nc):
    outs = []
    for j, x_ref in enumerate((xk_ref, xv_ref)):
        a = jnp.zeros((nc, 2 * CMP_HIDDEN), F32)
        b = jnp.zeros((nc, 2 * CMP_HIDDEN), F32)
        for t in range(CMP_STRIDE):
            xt = x_ref[0, pl.ds(t, nc, stride=CMP_STRIDE), :]
            a = a + _dot((xt + pe_ref[j, t:t + 1, :]).astype(BF16), w1_ref[j, t])
            b = b + _dot((xt + pe_ref[j, CMP_STRIDE + t:CMP_STRIDE + t + 1, :]).astype(BF16),
                         w1_ref[j, CMP_STRIDE + t])
        outs.append(jax.nn.gelu(a + pltpu.roll(b, nc - 1, 0)).astype(BF16))
    k2 = _dot(outs[0], w2k_ref[...])
    v_t = _dot_nt(w2vt_ref[...], outs[1])
    lane = _iota((nc, LANES), 1)
    key_feat = jnp.where((lane == HEAD_DIM) | (lane == HEAD_DIM + 1),
                         CMP_STRIDE * (_iota((nc, LANES), 0) & (CMP_KEY_CHUNK - 1)), 0).astype(F32)
    ci =_iota((N_BLK_PAD, nc), 1) * CMP_STRIDE
    sj = _iota((N_BLK_PAD, nc), 0) * SEL_BLOCK
    overlap_t = jnp.where((ci < sj + SEL_BLOCK) & (ci + (CMP_BLOCK - 1) >= sj), 1.0, 0.0).astype(BF16)
    ones_rows = jnp.where(_iota((VT_ROWS - HEAD_DIM, nc), 0) == 0, 1.0, 0.0).astype(BF16)
    for g in range(N_GROUPS):
        kg = k2 if g == 0 else pltpu.roll(k2, HEAD_DIM, 1)
        kc_ref[0, g] = jnp.where(lane < HEAD_DIM, kg, key_feat).astype(BF16)
        vct_ref[0, g, 0:HEAD_DIM, :] = v_t[HEAD_DIM * g:HEAD_DIM * (g + 1), :].astype(BF16)
        vct_ref[0, g, HEAD_DIM:VT_ROWS, :] = ones_rows
        vct_ref[0, g, VT_ROWS:VCT_ROWS, :] = overlap_t


def _compress(kvc3, pe2, w1bd, w2k, w2vt):
    bsz, t, _ = kvc3.shape
    nc = t // CMP_STRIDE
    full = lambda a: pl.BlockSpec(a.shape, lambda b: (0,) * a.ndim)
    return pl.pallas_call(
        functools.partial(_compress_kernel, nc=nc),
        grid=(bsz,),
        in_specs=[pl.BlockSpec((1, t, LANES), lambda b: (b, 0, 0)),
                  pl.BlockSpec((1, t, LANES), lambda b: (b, 0, 1)),
                  full(pe2), full(w1bd), full(w2k), full(w2vt)],
        out_specs=[pl.BlockSpec((1, N_GROUPS, nc, LANES), lambda b: (b, 0, 0, 0)),
                   pl.BlockSpec((1, N_GROUPS, VCT_ROWS, nc), lambda b: (b, 0, 0, 0))],
        out_shape=[jax.ShapeDtypeStruct((bsz, N_GROUPS, nc, LANES), BF16),
                   jax.ShapeDtypeStruct((bsz, N_GROUPS, VCT_ROWS, nc), BF16)],
        compiler_params=pltpu.CompilerParams(dimension_semantics=("arbitrary",),
                                             vmem_limit_bytes=VMEM_LIMIT),
        name="compress",
    )(kvc3, kvc3, pe2, w1bd, w2k, w2vt)


def _cmp_topk_kernel(qt_ref, kc_ref, vct_ref, gatest_ref, ocmp_ref, selt_ref, flags_ref,
                     m_ref, acc_ref, *, ncp):
    qb = pl.program_id(1)
    start = qb * Q_BLOCK
    n_chunks = ncp // CMP_KEY_CHUNK
    nck = (qb * (Q_BLOCK // CMP_STRIDE) + (Q_BLOCK - CMP_BLOCK) // CMP_STRIDE) // CMP_KEY_CHUNK + 1
    chunk_tokens = CMP_KEY_CHUNK * CMP_STRIDE
    tile_heads = lambda a: jnp.concatenate([a] * HPG, axis=1)
    blk = _iota((N_BLK_PAD, Q_BLOCK), 0)
    blk_f = blk.astype(F32)
    t_row = start + _iota((1, Q_BLOCK), 1)
    cur = lax.shift_right_logical(t_row, 6)
    causal = blk <= cur
    forced = (blk == 0) | (blk == cur) | (blk == cur - 1)
    has_key = t_row >= CMP_BLOCK - 1
    gt = gatest_ref[0]
    ones8 = jnp.ones((8, Q_BLOCK), F32)
    tail_chunks = min(2, n_chunks)
    head_chunks = n_chunks - tail_chunks
    tail_c0 = jnp.maximum(nck - tail_chunks, 0)
    tail_rows = pl.ds(pl.multiple_of(tail_c0 * CMP_KEY_CHUNK, CMP_KEY_CHUNK), tail_chunks * CMP_KEY_CHUNK)
    key_end = (CMP_STRIDE * (tail_c0 * CMP_KEY_CHUNK + _iota((tail_chunks * CMP_KEY_CHUNK, Q_BLOCK), 0))
               + (CMP_BLOCK - 1) - start)
    tail_bias = tile_heads(jnp.where(key_end <= _iota((tail_chunks * CMP_KEY_CHUNK, Q_BLOCK), 1), 0.0, NEG))

    groups = range(N_GROUPS)
    slope_rows = [_slope_row(g, Q_BLOCK) for g in groups]
    qas = [jnp.concatenate(
        [jnp.concatenate([qt_ref[0, HEAD_DIM * (HPG * g + j):HEAD_DIM * (HPG * g + j + 1), :]
                          for j in range(HPG)], axis=1),
         _slope_feature_rows(slope_rows[g], LANES - HEAD_DIM)], axis=0) for g in groups]

    def delta(g, c):
        return slope_rows[g] * (start - c * chunk_tokens).astype(F32)

    if head_chunks > 0:
        @pl.when(nck > tail_chunks)
        def _():
            sts = [_dot(kc_ref[0, g, 0:head_chunks * CMP_KEY_CHUNK, :], qas[g]) for g in groups]
            dls = [[delta(g, c) + jnp.where(c < nck - tail_chunks, 0.0, MASK_BIG)
                    for c in range(head_chunks)] for g in groups]
            ms = [_col_max(sts[g], dls[g], CMP_KEY_CHUNK) for g in groups]
            ps = [_probs(sts[g], dls[g], ms[g], CMP_KEY_CHUNK) for g in groups]
            for g in groups:
                m_ref[g] = ms[g]
                acc_ref[g] = _dot(vct_ref[0, g, :, 0:head_chunks * CMP_KEY_CHUNK], ps[g])

        @pl.when(nck <= tail_chunks)
        def _():
            m_ref[...] = jnp.full(m_ref.shape, NEG, F32)
            acc_ref[...] = jnp.zeros(acc_ref.shape, F32)
        m_old = [m_ref[g] for g in groups]
        acc_old = [acc_ref[g] for g in groups]
    else:
        m_old = [jnp.full((1, HPG * Q_BLOCK), NEG, F32) for g in groups]
        acc_old = [jnp.zeros((VCT_ROWS, HPG * Q_BLOCK), F32) for g in groups]

    sts = [_dot(kc_ref[0, g, tail_rows, :], qas[g]) + tail_bias for g in groups]
    dls = [[delta(g, tail_c0 + u) for u in range(tail_chunks)] for g in groups]
    m_new = [jnp.maximum(m_old[g], _col_max(sts[g], dls[g], CMP_KEY_CHUNK)) for g in groups]
    ps = [_probs(sts[g], dls[g], m_new[g], CMP_KEY_CHUNK) for g in groups]
    accs = [jnp.exp2(m_old[g] - m_new[g]) * acc_old[g] + _dot(vct_ref[0, g, :, tail_rows], ps[g])
            for g in groups]

    per_head, ranks = [], []
    for g in groups:
        acc = accs[g]
        inv_l = 1.0 / jnp.maximum(acc[HEAD_DIM:HEAD_DIM + 1], 1e-30)
        o = acc[0:HEAD_DIM] * inv_l
        imp_h = acc[VT_ROWS:VCT_ROWS] * inv_l
        imp = sum(imp_h[:, Q_BLOCK * j:Q_BLOCK * (j + 1)] for j in range(HPG))
        imp = jnp.where(has_key, imp, 0.0)
        for j in range(HPG):
            h = HPG * g + j
            per_head.append(jnp.where(has_key, gt[3 * h:3 * h + 1, :] * o[:, Q_BLOCK * j:Q_BLOCK * (j + 1)], 0.0))
        ranks.append(jnp.where(causal, jnp.where(forced, REMOVED, imp), NEG))
    ocmp_ref[0] = jnp.concatenate(per_head, axis=0)

    for _ in range(N_SELECT - N_FORCED):
        for g in groups:
            m = jnp.max(ranks[g], axis=0, keepdims=True)
            idx = jnp.min(jnp.where(ranks[g] == m, blk_f, float(N_BLK_PAD)), axis=0, keepdims=True)
            ranks[g] = jnp.where(blk_f == idx, REMOVED, ranks[g])
    for g in groups:
        sel = jnp.where(causal & (ranks[g] == REMOVED), 1.0, 0.0)
        selt_ref[0, g] = sel.astype(BF16)
        flags_ref[0, 0, g] = (_dot_nt(ones8, sel) > 0.5).astype(jnp.int32)


def _cmp_topk(qt, kc, vct, gatest):
    bsz, _, t = qt.shape
    ncp = t // CMP_STRIDE
    nqb = t // Q_BLOCK
    return pl.pallas_call(
        functools.partial(_cmp_topk_kernel, ncp=ncp),
        grid=(bsz, nqb),
        in_specs=[pl.BlockSpec((1, NSA_W, Q_BLOCK), lambda b, i: (b, 0, i)),
                  pl.BlockSpec((1, N_GROUPS, ncp, LANES), lambda b, i: (b, 0, 0, 0)),
                  pl.BlockSpec((1, N_GROUPS, VCT_ROWS, ncp), lambda b, i: (b, 0, 0, 0)),
                  pl.BlockSpec((1, GATET_ROWS, Q_BLOCK), lambda b, i: (b, 0, i))],
        out_specs=[pl.BlockSpec((1, NSA_W, Q_BLOCK), lambda b, i: (b, 0, i)),
                   pl.BlockSpec((1, N_GROUPS, N_BLK_PAD, Q_BLOCK), lambda b, i: (b, 0, 0, i)),
                   pl.BlockSpec((1, 1, N_GROUPS, 8, N_BLK_PAD), lambda b, i: (b, i, 0, 0, 0))],
        out_shape=[jax.ShapeDtypeStruct((bsz, NSA_W, t), F32),
                   jax.ShapeDtypeStruct((bsz, N_GROUPS, N_BLK_PAD, t), BF16),
                   jax.ShapeDtypeStruct((bsz, nqb, N_GROUPS, 8, N_BLK_PAD), jnp.int32)],
        scratch_shapes=[pltpu.VMEM((N_GROUPS, 1, HPG * Q_BLOCK), F32),
                        pltpu.VMEM((N_GROUPS, VCT_ROWS, HPG * Q_BLOCK), F32)],
        compiler_params=pltpu.CompilerParams(dimension_semantics=("arbitrary", "arbitrary"),
                                             vmem_limit_bytes=VMEM_LIMIT),
        name="cmp_topk",
    )(qt, kc, vct, gatest)


def _slc_win_kernel(counts_ref, lists_ref, qt_ref, kaug_ref, vt_ref, oh_ref, selt_ref, gatest_ref,
                    ocmp_ref, out_ref, qaug_ref, m_ref, acc_ref, *, nqb):
    b = pl.program_id(0)
    qb = pl.program_id(1)
    step_id = b * nqb + qb
    start = qb * NSA_QB
    width = HPG * NSA_QB
    win_keys = WINDOW + NSA_QB
    win_start = jnp.maximum(start - WINDOW, 0)
    tile_heads = lambda a: jnp.concatenate([a] * HPG, axis=1)
    dist = (start - win_start) + _iota((win_keys, NSA_QB), 1) - _iota((win_keys, NSA_QB), 0)
    win_bias = tile_heads(jnp.where((dist >= 0) & (dist < WINDOW), 0.0, NEG))
    own_bias = tile_heads(jnp.where(_iota((NSA_QB, NSA_QB), 0) <= _iota((NSA_QB, NSA_QB), 1), 0.0, NEG))
    gt = gatest_ref[0]
    slope_rows = [_slope_row(g, NSA_QB) for g in range(N_GROUPS)]

    def normalize(acc):
        return acc[0:HEAD_DIM] / jnp.maximum(acc[HEAD_DIM:HEAD_DIM + 1], 1e-30)

    for g in range(N_GROUPS):
        qaug_ref[g, 0:HEAD_DIM, :] = jnp.concatenate(
            [qt_ref[0, HEAD_DIM * (HPG * g + j):HEAD_DIM * (HPG * g + j + 1), :] for j in range(HPG)],
            axis=1)
        qaug_ref[g, HEAD_DIM:LANES, :] = _slope_feature_rows(slope_rows[g], LANES - HEAD_DIM)
        sel_bias = ((selt_ref[0, g].astype(F32) - 1.0) * MASK_BIG).astype(BF16)
        qaug_ref[g, LANES:2 * LANES, :] = tile_heads(sel_bias)

    win_rows = pl.ds(pl.multiple_of(win_start, KEY_CHUNK), win_keys)
    own_rows = pl.ds(pl.multiple_of(start, KEY_CHUNK), NSA_QB)
    win_st = [_dot(kaug_ref[0, N_GROUPS + g, win_rows, :], qaug_ref[g, 0:LANES, :]) + win_bias
              for g in range(N_GROUPS)]
    own_st = [_dot(jnp.concatenate([kaug_ref[0, g, own_rows, :], oh_ref[own_rows, :]], axis=1),
                   qaug_ref[g]) + own_bias for g in range(N_GROUPS)]
    win_dls = [[slope_rows[g] * (start - win_start - KEY_CHUNK * u).astype(F32)
                for u in range(win_keys // KEY_CHUNK)] for g in range(N_GROUPS)]
    own_dls = [[slope_rows[g] * float(-KEY_CHUNK * u) for u in range(NSA_QB // KEY_CHUNK)]
               for g in range(N_GROUPS)]
    win_m = [_col_max(win_st[g], win_dls[g], KEY_CHUNK) for g in range(N_GROUPS)]
    own_m = [_col_max(own_st[g], own_dls[g], KEY_CHUNK) for g in range(N_GROUPS)]
    win_p = [_probs(win_st[g], win_dls[g], win_m[g], KEY_CHUNK) for g in range(N_GROUPS)]
    own_p = [_probs(own_st[g], own_dls[g], own_m[g], KEY_CHUNK) for g in range(N_GROUPS)]
    o_win = [normalize(_dot(vt_ref[0, N_GROUPS + g, :, win_rows], win_p[g])) for g in range(N_GROUPS)]
    for g in range(N_GROUPS):
        m_ref[g] = own_m[g]
        acc_ref[g] = _dot(vt_ref[0, g, :, own_rows], own_p[g])

    def slc_body(i, carry):
        sts, vss, dlss = [], [], []
        for g in range(N_GROUPS):
            word = lists_ref[(step_id * N_GROUPS + g) * LIST_WORDS + i]
            ks, vs, dls = [], [], []
            for u in range(SLC_BATCH):
                cid = lax.shift_right_logical(word, 8 * u) & 255
                valid = cid < N_BLK_PAD // 2
                c = jnp.where(valid, cid, 0)
                rows = pl.ds(pl.multiple_of(c * KEY_CHUNK, KEY_CHUNK), KEY_CHUNK)
                ks.append(jnp.concatenate([kaug_ref[0, g, rows, :], oh_ref[rows, :]], axis=1))
                vs.append(vt_ref[0, g, :, rows])
                dls.append(slope_rows[g] * (start - c * KEY_CHUNK).astype(F32)
                           + jnp.where(valid, 0.0, MASK_BIG))
            sts.append(_dot(jnp.concatenate(ks, axis=0), qaug_ref[g]))
            vss.append(jnp.concatenate(vs, axis=1))
            dlss.append(dls)
        m_old = [m_ref[g] for g in range(N_GROUPS)]
        m_new = [jnp.maximum(m_old[g], _col_max(sts[g], dlss[g], KEY_CHUNK)) for g in range(N_GROUPS)]
        ps = [_probs(sts[g], dlss[g], m_new[g], KEY_CHUNK) for g in range(N_GROUPS)]
        for g in range(N_GROUPS):
            acc_ref[g] = jnp.exp2(m_old[g] - m_new[g]) * acc_ref[g] + _dot(vss[g], ps[g])
            m_ref[g] = m_new[g]
        return carry

    lax.fori_loop(0, counts_ref[step_id], slc_body, 0)

    per_head = []
    for g in range(N_GROUPS):
        o_slc = normalize(acc_ref[g])
        for j in range(HPG):
            h = HPG * g + j
            lanes = slice(NSA_QB * j, NSA_QB * (j + 1))
            per_head.append(gt[3 * h + 1:3 * h + 2, :] * o_slc[:, lanes]
                            + gt[3 * h + 2:3 * h + 3, :] * o_win[g][:, lanes])
    o_t = jnp.concatenate(per_head, axis=0)
    out_ref[0] = (ocmp_ref[0] + o_t).T.astype(BF16)


def _slc_win(counts, lists, qt, kaug, vt, onehot, selt, gatest, ocmp):
    bsz, _, t = qt.shape
    nqb = t // NSA_QB
    once = lambda shape, imap: pl.BlockSpec(shape, imap, pipeline_mode=pl.Buffered(1))
    grid_spec = pltpu.PrefetchScalarGridSpec(
        num_scalar_prefetch=2,
        grid=(bsz, nqb),
        in_specs=[pl.BlockSpec((1, NSA_W, NSA_QB), lambda b, i, *_: (b, 0, i)),
                  once((1, 2 * N_GROUPS, t, LANES), lambda b, i, *_: (b, 0, 0, 0)),
                  once((1, 2 * N_GROUPS, VT_ROWS, t), lambda b, i, *_: (b, 0, 0, 0)),
                  once((t, N_BLK_PAD), lambda b, i, *_: (0, 0)),
                  pl.BlockSpec((1, N_GROUPS, N_BLK_PAD, NSA_QB), lambda b, i, *_: (b, 0, 0, i)),
                  pl.BlockSpec((1, GATET_ROWS, NSA_QB), lambda b, i, *_: (b, 0, i)),
                  pl.BlockSpec((1, NSA_W, NSA_QB), lambda b, i, *_: (b, 0, i))],
        out_specs=pl.BlockSpec((1, NSA_QB, NSA_W), lambda b, i, *_: (b, i, 0)),
        scratch_shapes=[pltpu.VMEM((N_GROUPS, 2 * LANES, HPG * NSA_QB), BF16),
                        pltpu.VMEM((N_GROUPS, 1, HPG * NSA_QB), F32),
                        pltpu.VMEM((N_GROUPS, VT_ROWS, HPG * NSA_QB), F32)],
    )
    return pl.pallas_call(
        functools.partial(_slc_win_kernel, nqb=nqb),
        grid_spec=grid_spec,
        out_shape=jax.ShapeDtypeStruct((bsz, t, NSA_W), BF16),
        compiler_params=pltpu.CompilerParams(dimension_semantics=("arbitrary", "arbitrary"),
                                             vmem_limit_bytes=VMEM_LIMIT),
        name="slc_win",
    )(counts, lists, qt, kaug, vt, onehot, selt, gatest, ocmp)


def _merge_kernel(x_ref, onsa_ref, u_ref, vn_ref, g_ref, wm_ref, bm_ref, ws_ref, bs_ref,
                  wpa_ref, wpb_ref, wo_ref, h_ref, *, tm):
    x = x_ref[...]
    xn = _rms(x, g_ref[...]).astype(BF16)
    tril = _iota((GMLP_CHUNK, GMLP_CHUNK), 0) >= _iota((GMLP_CHUNK, GMLP_CHUNK), 1)
    sgu_rows = []
    for c in range(tm // GMLP_CHUNK):
        rows = slice(GMLP_CHUNK * c, GMLP_CHUNK * (c + 1))
        cols = []
        for g in range(GMLP_GROUPS):
            lanes = slice(LANES * g, LANES * (g + 1))
            w = jnp.where(tril, ws_ref[g], 0.0).astype(BF16)
            cols.append(_dot(w, vn_ref[rows, lanes]) + bs_ref[:, g:g + 1])
        sgu_rows.append(u_ref[rows, :] * jnp.concatenate(cols, axis=1))
    o_sgu = jnp.concatenate(sgu_rows, axis=0).astype(BF16)
    mg = jax.nn.sigmoid(_dot(xn, wm_ref[...]) + bm_ref[...])
    mixed = (mg[:, :D_MODEL] * _dot(onsa_ref[...], wpa_ref[...])
             + mg[:, D_MODEL:] * _dot(o_sgu, wpb_ref[...]))
    h_ref[...] = x + _dot(mixed.astype(BF16), wo_ref[...])


def _merge(x2, onsa2, u2, vn2, norm_g, wm, bm, ws, bs_t, wpa, wpb, wo, tm):
    n = x2.shape[0]
    row = lambda w: pl.BlockSpec((tm, w), lambda i: (i, 0))
    full = lambda a: pl.BlockSpec(a.shape, lambda i: (0,) * a.ndim)
    return pl.pallas_call(
        functools.partial(_merge_kernel, tm=tm),
        grid=(n // tm,),
        in_specs=[row(D_MODEL), row(NSA_W), row(GMLP_WIDTH), row(GMLP_WIDTH), full(norm_g),
                  full(wm), full(bm), full(ws), full(bs_t), full(wpa), full(wpb), full(wo)],
        out_specs=row(D_MODEL),
        out_shape=jax.ShapeDtypeStruct((n, D_MODEL), F32),
        compiler_params=pltpu.CompilerParams(dimension_semantics=("arbitrary",),
                                             vmem_limit_bytes=VMEM_LIMIT),
        name="merge",
    )(x2, onsa2, u2, vn2, norm_g, wm, bm, ws, bs_t, wpa, wpb, wo)


def _memkv_kernel(mem_ref, g_ref, w_ref, out_ref):
    out_ref[0] = _dot(_rms(mem_ref[0], g_ref[...]).astype(BF16), w_ref[...]).astype(BF16)


def _memkv(mem, norm_g, w):
    bsz, nm, _ = mem.shape
    return pl.pallas_call(
        _memkv_kernel,
        grid=(bsz,),
        in_specs=[pl.BlockSpec((1, nm, D_MODEL), lambda b: (b, 0, 0)),
                  pl.BlockSpec(norm_g.shape, lambda b: (0, 0)),
                  pl.BlockSpec(w.shape, lambda b: (0, 0))],
        out_specs=pl.BlockSpec((1, nm, 2 * MEM_W), lambda b: (b, 0, 0)),
        out_shape=jax.ShapeDtypeStruct((bsz, nm, 2 * MEM_W), BF16),
        compiler_params=pltpu.CompilerParams(dimension_semantics=("arbitrary",),
                                             vmem_limit_bytes=VMEM_LIMIT),
        name="memkv",
    )(mem, norm_g, w)


def _xattn_kernel(h_ref, g_ref, wq_ref, mkv_ref, wo_ref, out_ref):
    h = h_ref[...]
    hq = _dot(_rms(h, g_ref[...]).astype(BF16), wq_ref[...]).astype(BF16)
    heads = []
    for a in range(MEM_HEADS):
        lanes = slice(MEM_HEAD_DIM * a, MEM_HEAD_DIM * (a + 1))
        k = mkv_ref[0, :, lanes]
        v = mkv_ref[0, :, MEM_W + MEM_HEAD_DIM * a:MEM_W + MEM_HEAD_DIM * (a + 1)]
        s = _dot_nt(hq[:, lanes], k) * (MEM_HEAD_DIM ** -0.5)
        e = jnp.exp(s - jnp.max(s, axis=-1, keepdims=True))
        p = e / jnp.sum(e, axis=-1, keepdims=True)
        heads.append(_dot(p.astype(BF16), v))
    o = jnp.concatenate(heads, axis=1).astype(BF16)
    out_ref[...] = h + _dot(o, wo_ref[...])


def _xattn(h2d, norm_g, wq, mkv, wo, tm, rows_per_batch):
    n = h2d.shape[0]
    nm = mkv.shape[1]
    tiles_per_batch = rows_per_batch // tm
    full = lambda a: pl.BlockSpec(a.shape, lambda i: (0,) * a.ndim)
    return pl.pallas_call(
        _xattn_kernel,
        grid=(n // tm,),
        in_specs=[pl.BlockSpec((tm, D_MODEL), lambda i: (i, 0)), full(norm_g), full(wq),
                  pl.BlockSpec((1, nm, 2 * MEM_W), lambda i: (i // tiles_per_batch, 0, 0)),
                  full(wo)],
        out_specs=pl.BlockSpec((tm, D_MODEL), lambda i: (i, 0)),
        out_shape=jax.ShapeDtypeStruct((n, D_MODEL), F32),
        compiler_params=pltpu.CompilerParams(dimension_semantics=("arbitrary",),
                                             vmem_limit_bytes=VMEM_LIMIT),
        name="xattn",
    )(h2d, norm_g, wq, mkv, wo)


def _ffn_kernel(h_ref, g_ref, wgu_ref, wd_ref, gf_ref, out_ref, *, d_ff):
    h = h_ref[...]
    hn = _rms(h, g_ref[...]).astype(BF16)
    gate = _dot(hn, wgu_ref[:, :d_ff])
    up = _dot(hn, wgu_ref[:, d_ff:])
    act = (jax.nn.silu(gate) * up).astype(BF16)
    y = h + _dot(act, wd_ref[...])
    out_ref[...] = _rms(y, gf_ref[...])


def _ffn(h2d, norm_g, wgu, wd, norm_f, tm):
    n = h2d.shape[0]
    d_ff = wd.shape[0]
    full = lambda a: pl.BlockSpec(a.shape, lambda i: (0,) * a.ndim)
    once = lambda a: pl.BlockSpec(a.shape, lambda i: (0,) * a.ndim, pipeline_mode=pl.Buffered(1))
    return pl.pallas_call(
        functools.partial(_ffn_kernel, d_ff=d_ff),
        grid=(n // tm,),
        in_specs=[pl.BlockSpec((tm, D_MODEL), lambda i: (i, 0)), full(norm_g), once(wgu), once(wd),
                  full(norm_f)],
        out_specs=pl.BlockSpec((tm, D_MODEL), lambda i: (i, 0)),
        out_shape=jax.ShapeDtypeStruct((n, D_MODEL), F32),
        compiler_params=pltpu.CompilerParams(dimension_semantics=("arbitrary",),
                                             vmem_limit_bytes=VMEM_LIMIT),
        name="ffn",
    )(h2d, norm_g, wgu, wd, norm_f)


def _block_diag2(w):
    z = jnp.zeros_like(w)
    return jnp.concatenate([jnp.concatenate([w, z], axis=-1), jnp.concatenate([z, w], axis=-1)], axis=-2)


def _chunk_lists(flags):
    bsz = flags.shape[0]
    n_chunks = N_BLK_PAD // 2
    per_step = NSA_QB // Q_BLOCK
    nqb = flags.shape[1] // per_step
    f = flags[:, :, :, 0, :].reshape(bsz, nqb, per_step, N_GROUPS, n_chunks, 2).max(axis=(2, 5))
    cid = jnp.arange(n_chunks, dtype=jnp.int32)
    own = (NSA_QB // KEY_CHUNK) * jnp.arange(nqb, dtype=jnp.int32)[None, :, None, None]
    active = (f > 0) & (cid < own)
    n_active = active.sum(axis=-1)
    slot = jnp.cumsum(active, axis=-1) - 1
    hit = active[..., :, None] & (slot[..., :, None] == cid)
    ids = jnp.sum(jnp.where(hit, cid[:, None], 0), axis=-2)
    ids = jnp.where(cid < n_active[..., None], ids, 255)
    n_batches = (n_active.max(axis=-1) + SLC_BATCH - 1) // SLC_BATCH
    packed = ids.reshape(bsz, nqb, N_GROUPS, LIST_WORDS, SLC_BATCH)
    words = functools.reduce(jnp.bitwise_or, [packed[..., u] << (8 * u) for u in range(SLC_BATCH)])
    return n_batches.reshape(-1).astype(jnp.int32), words.reshape(-1).astype(jnp.int32)


def kernel(x, mem, norm_mix, w_in, w_cmp_k1, w_cmp_k2, w_cmp_v1, w_cmp_v2, pe_cmp_k, pe_cmp_v, ln_sgu, w_spatial, b_spatial, w_proj_a, w_proj_b, w_merge, b_merge, w_out, norm_mem_q, norm_mem_kv, w_mq, w_mkv, w_mo, norm_ffn, w_gate_up, w_down, norm_final):
    bsz, t, d = x.shape
    depth = norm_mix.shape[0]
    assert d == D_MODEL and t % Q_BLOCK == 0 and t // SEL_BLOCK <= N_BLK_PAD
    assert t // SEL_BLOCK >= N_SELECT and depth == 1 and t % 256 == 0
    n = bsz * t
    tm = 256
    h = x.reshape(n, d)
    c0, c1, c2, c3 = NSA_W, NSA_W + KV_W, NSA_W + 3 * KV_W, NSA_W + 3 * KV_W + 2 * GMLP_WIDTH
    onehot = (jnp.arange(t)[:, None] // SEL_BLOCK == jnp.arange(N_BLK_PAD)[None, :]).astype(BF16)
    pad_cols = lambda w, width: jnp.pad(w, ((0, 0), (0, width - w.shape[1])))
    for l in range(depth):
        wi = w_in[l]
        wk, wv = [], []
        for branch in range(2):
            base = c1 + KV_W * branch
            for g in range(N_GROUPS):
                wk.append(pad_cols(wi[:, base + HEAD_DIM * g:base + HEAD_DIM * (g + 1)], LANES))
                v0 = base + N_GROUPS * HEAD_DIM + HEAD_DIM * g
                wv.append(pad_cols(wi[:, v0:v0 + HEAD_DIM], VT_ROWS))
        wrow = jnp.concatenate([wi[:, c0:c1]] + wk + [wi[:, c2:c3]], axis=1).astype(BF16)
        wnt = jnp.concatenate([wi[:, :c0]] + wv + [pad_cols(wi[:, c3:], GATET_ROWS)], axis=1).T.astype(BF16)
        qt, kvc2, kaug, vt, u2, vn2, gatest = _inproj(
            h, norm_mix[l][None], ln_sgu[l][None], wrow, wnt, tm, bsz, t)

        pe2 = jnp.stack([pe_cmp_k[l], pe_cmp_v[l]])
        pe2 = jnp.concatenate([pe2, pe2], axis=-1)
        w1 = jnp.stack([w_cmp_k1[l], w_cmp_v1[l]]).reshape(2, CMP_BLOCK, HEAD_DIM, CMP_HIDDEN)
        w1bd = _block_diag2(w1).astype(BF16)
        w2k = _block_diag2(w_cmp_k2[l]).astype(BF16)
        w2vt = _block_diag2(w_cmp_v2[l]).T.astype(BF16)
        kc, vct = _compress(kvc2.reshape(bsz, t, KV_W), pe2, w1bd, w2k, w2vt)
        ocmp, selt, flags = _cmp_topk(qt, kc, vct, gatest)
        counts, lists = _chunk_lists(flags)
        onsa = _slc_win(counts, lists, qt, kaug, vt, onehot, selt, gatest, ocmp)

        h = _merge(h, onsa.reshape(n, NSA_W), u2, vn2, norm_mix[l][None], w_merge[l].astype(BF16),
                   b_merge[l][None], w_spatial[l], b_spatial[l].T, w_proj_a[l].astype(BF16),
                   w_proj_b[l].astype(BF16), w_out[l].astype(BF16), tm)

        mkv = _memkv(mem, norm_mem_kv[l][None], w_mkv[l].astype(BF16))
        h = _xattn(h, norm_mem_q[l][None], w_mq[l].astype(BF16), mkv, w_mo[l].astype(BF16), tm, t)
        h = _ffn(h, norm_ffn[l][None], w_gate_up[l].astype(BF16), w_down[l].astype(BF16),
                 norm_final[None], tm)
    return h.reshape(bsz, t, d)
```

```python
import functools

import jax
import jax.numpy as jnp
from jax import lax
from jax.experimental import pallas as pl
from jax.experimental.pallas import tpu as pltpu

F32 = jnp.float32
BF16 = jnp.bfloat16

LANES = 128
D_MODEL = 1024
N_HEADS = 8
HEAD_DIM = 64
N_GROUPS = 2
HPG = N_HEADS // N_GROUPS
CMP_BLOCK = 32
CMP_STRIDE = 16
CMP_HIDDEN = 128
SEL_BLOCK = 64
N_SELECT = 16
WINDOW = 512
Q_BLOCK = 256
KEY_CHUNK = 128
N_BLK_PAD = 128
NSA_W = N_HEADS * HEAD_DIM
KV_W = 2 * N_GROUPS * HEAD_DIM
GMLP_WIDTH = 512
GMLP_GROUPS = 4
GMLP_CHUNK = 128
MEM_HEADS = 4
MEM_HEAD_DIM = 128
MEM_W = MEM_HEADS * MEM_HEAD_DIM
GATE_W = 3 * N_HEADS
NSA_QB = 256
SLC_BATCH = 4
LIST_WORDS = (N_BLK_PAD // 2) // SLC_BATCH
VT_ROWS = 80
VCT_ROWS = VT_ROWS + N_BLK_PAD
CMP_KEY_CHUNK = 128
N_FORCED = 3
GATET_ROWS = 32
MASK_BIG = 1e30
EPS = 1e-6
NEG = -1e30
REMOVED = -3e38
FORCE_SCORE = 1e6
SLOPES = tuple(2.0 ** (-8.0 * (h + 1) / N_HEADS) for h in range(N_HEADS))
LOG2E = 1.4426950408889634
Q_SCALE = HEAD_DIM ** -0.5 * LOG2E
VMEM_LIMIT = 56 * 1024 * 1024


def _dot(a, b):
    return jnp.dot(a, b, preferred_element_type=F32)


def _dot_nt(a, b):
    return lax.dot_general(a, b, (((1,), (1,)), ((), ())), preferred_element_type=F32)


def _rms(x, g):
    return x * lax.rsqrt(jnp.mean(x * x, axis=-1, keepdims=True) + EPS) * g


def _iota(shape, dim):
    return lax.broadcasted_iota(jnp.int32, shape, dim)


def _slope_row(g, nq):
    return jnp.concatenate(
        [jnp.full((1, nq), SLOPES[HPG * g + j] * LOG2E, F32) for j in range(HPG)], axis=1)


def _slope_feature_rows(slope_row, n_rows):
    hi = slope_row.astype(BF16).astype(F32)
    r = _iota((n_rows, slope_row.shape[1]), 0)
    return jnp.where(r == 0, hi, jnp.where(r == 1, slope_row - hi, 0.0)).astype(BF16)


def _chunk_slabs(st, chunk):
    return [st[chunk * u:chunk * (u + 1)] for u in range(st.shape[0] // chunk)]


def _col_max(st, dls, chunk):
    parts = [s.reshape(chunk // 8, 8, s.shape[1]).max(axis=0) - dl
             for s, dl in zip(_chunk_slabs(st, chunk), dls)]
    return functools.reduce(jnp.maximum, parts).max(axis=0, keepdims=True)


def _probs(st, dls, m, chunk):
    return jnp.concatenate([jnp.exp2(s - (m + dl)).astype(BF16)
                            for s, dl in zip(_chunk_slabs(st, chunk), dls)], axis=0)


def _inproj_kernel(x_ref, g_ref, lng_ref, wrow_ref, wnt_ref,
                   qt_ref, kvc_ref, kaug_ref, vt_ref, u_ref, vn_ref, gatest_ref, *, tm):
    xn = _rms(x_ref[...], g_ref[...]).astype(BF16)
    r = _dot(xn, wrow_ref[...])
    nt = _dot_nt(wnt_ref[...], xn)
    qt_ref[0] = (nt[0:NSA_W] * Q_SCALE).astype(BF16)
    kvc_ref[...] = r[:, 0:KV_W]
    lane = _iota((tm, LANES), 1)
    key_feat = jnp.where((lane == HEAD_DIM) | (lane == HEAD_DIM + 1),
                         _iota((tm, LANES), 0) & (KEY_CHUNK - 1), 0).astype(F32)
    ones_row = jnp.where(_iota((VT_ROWS, tm), 0) == HEAD_DIM, 1.0, 0.0)
    for a in range(2 * N_GROUPS):
        kaug_ref[0, a] = (r[:, KV_W + LANES * a:KV_W + LANES * (a + 1)] + key_feat).astype(BF16)
        vt_ref[0, a] = (nt[NSA_W + VT_ROWS * a:NSA_W + VT_ROWS * (a + 1)] + ones_row).astype(BF16)
    uv = jax.nn.gelu(r[:, KV_W + 2 * N_GROUPS * LANES:])
    u_ref[...] = uv[:, :GMLP_WIDTH]
    v = uv[:, GMLP_WIDTH:]
    vc = v - jnp.mean(v, axis=-1, keepdims=True)
    vn = vc * lax.rsqrt(jnp.mean(vc * vc, axis=-1, keepdims=True) + EPS) * lng_ref[...]
    vn_ref[...] = vn.astype(BF16)
    gatest_ref[0] = jax.nn.sigmoid(nt[NSA_W + 2 * N_GROUPS * VT_ROWS:])


def _inproj(x2, norm_g, ln_g, wrow, wnt, tm, bsz, t):
    n = x2.shape[0]
    tpb = t // tm
    row = lambda w: pl.BlockSpec((tm, w), lambda i: (i, 0))
    full = lambda a: pl.BlockSpec(a.shape, lambda i: (0,) * a.ndim)
    return pl.pallas_call(
        functools.partial(_inproj_kernel, tm=tm),
        grid=(n // tm,),
        in_specs=[row(D_MODEL), full(norm_g), full(ln_g), full(wrow), full(wnt)],
        out_specs=[pl.BlockSpec((1, NSA_W, tm), lambda i: (i // tpb, 0, i % tpb)),
                   row(KV_W),
                   pl.BlockSpec((1, 2 * N_GROUPS, tm, LANES), lambda i: (i // tpb, 0, i % tpb, 0)),
                   pl.BlockSpec((1, 2 * N_GROUPS, VT_ROWS, tm), lambda i: (i // tpb, 0, 0, i % tpb)),
                   row(GMLP_WIDTH), row(GMLP_WIDTH),
                   pl.BlockSpec((1, GATET_ROWS, tm), lambda i: (i // tpb, 0, i % tpb))],
        out_shape=[jax.ShapeDtypeStruct((bsz, NSA_W, t), BF16),
                   jax.ShapeDtypeStruct((n, KV_W), F32),
                   jax.ShapeDtypeStruct((bsz, 2 * N_GROUPS, t, LANES), BF16),
                   jax.ShapeDtypeStruct((bsz, 2 * N_GROUPS, VT_ROWS, t), BF16),
                   jax.ShapeDtypeStruct((n, GMLP_WIDTH), F32),
                   jax.ShapeDtypeStruct((n, GMLP_WIDTH), BF16),
                   jax.ShapeDtypeStruct((bsz, GATET_ROWS, t), F32)],
        compiler_params=pltpu.CompilerParams(dimension_semantics=("arbitrary",),
                                             vmem_limit_bytes=VMEM_LIMIT),
        name="inproj",
    )(x2, norm_g, ln_g, wrow, wnt)


def _compress_kernel(xk_ref, xv_ref, pe_ref, w1_ref, w2k_ref, w2vt_ref, kc_ref, vct_ref, *, nc):
    outs = []
    for j, x_ref in enumerate((xk_ref, xv_ref)):
        a = jnp.zeros((nc, 2 * CMP_HIDDEN), F32)
        b = jnp.zeros((nc, 2 * CMP_HIDDEN), F32)
        for t in range(CMP_STRIDE):
            xt = x_ref[0, pl.ds(t, nc, stride=CMP_STRIDE), :]
            a = a + _dot((xt + pe_ref[j, t:t + 1, :]).astype(BF16), w1_ref[j, t])
            b = b + _dot((xt + pe_ref[j, CMP_STRIDE + t:CMP_STRIDE + t + 1, :]).astype(BF16),
                         w1_ref[j, CMP_STRIDE + t])
        outs.append(jax.nn.gelu(a + pltpu.roll(b, nc - 1, 0)).astype(BF16))
    k2 = _dot(outs[0], w2k_ref[...])
    v_t = _dot_nt(w2vt_ref[...], outs[1])
    lane = _iota((nc, LANES), 1)
    key_feat = jnp.where((lane == HEAD_DIM) | (lane == HEAD_DIM + 1),
                         CMP_STRIDE * (_iota((nc, LANES), 0) & (CMP_KEY_CHUNK - 1)), 0).astype(F32)
    ci =_iota((N_BLK_PAD, nc), 1) * CMP_STRIDE
    sj = _iota((N_BLK_PAD, nc), 0) * SEL_BLOCK
    overlap_t = jnp.where((ci < sj + SEL_BLOCK) & (ci + (CMP_BLOCK - 1) >= sj), 1.0, 0.0).astype(BF16)
    ones_rows = jnp.where(_iota((VT_ROWS - HEAD_DIM, nc), 0) == 0, 1.0, 0.0).astype(BF16)
    for g in range(N_GROUPS):
        kg = k2 if g == 0 else pltpu.roll(k2, HEAD_DIM, 1)
        kc_ref[0, g] = jnp.where(lane < HEAD_DIM, kg, key_feat).astype(BF16)
        vct_ref[0, g, 0:HEAD_DIM, :] = v_t[HEAD_DIM * g:HEAD_DIM * (g + 1), :].astype(BF16)
        vct_ref[0, g, HEAD_DIM:VT_ROWS, :] = ones_rows
        vct_ref[0, g, VT_ROWS:VCT_ROWS, :] = overlap_t


def _compress(kvc3, pe2, w1bd, w2k, w2vt):
    bsz, t, _ = kvc3.shape
    nc = t // CMP_STRIDE
    full = lambda a: pl.BlockSpec(a.shape, lambda b: (0,) * a.ndim)
    return pl.pallas_call(
        functools.partial(_compress_kernel, nc=nc),
        grid=(bsz,),
        in_specs=[pl.BlockSpec((1, t, LANES), lambda b: (b, 0, 0)),
                  pl.BlockSpec((1, t, LANES), lambda b: (b, 0, 1)),
                  full(pe2), full(w1bd), full(w2k), full(w2vt)],
        out_specs=[pl.BlockSpec((1, N_GROUPS, nc, LANES), lambda b: (b, 0, 0, 0)),
                   pl.BlockSpec((1, N_GROUPS, VCT_ROWS, nc), lambda b: (b, 0, 0, 0))],
        out_shape=[jax.ShapeDtypeStruct((bsz, N_GROUPS, nc, LANES), BF16),
                   jax.ShapeDtypeStruct((bsz, N_GROUPS, VCT_ROWS, nc), BF16)],
        compiler_params=pltpu.CompilerParams(dimension_semantics=("arbitrary",),
                                             vmem_limit_bytes=VMEM_LIMIT),
        name="compress",
    )(kvc3, kvc3, pe2, w1bd, w2k, w2vt)


def _cmp_topk_kernel(qt_ref, kc_ref, vct_ref, gatest_ref, ocmp_ref, selt_ref, flags_ref,
                     m_ref, acc_ref, *, ncp):
    qb = pl.program_id(1)
    start = qb * Q_BLOCK
    n_chunks = ncp // CMP_KEY_CHUNK
    nck = (qb * (Q_BLOCK // CMP_STRIDE) + (Q_BLOCK - CMP_BLOCK) // CMP_STRIDE) // CMP_KEY_CHUNK + 1
    chunk_tokens = CMP_KEY_CHUNK * CMP_STRIDE
    tile_heads = lambda a: jnp.concatenate([a] * HPG, axis=1)
    blk = _iota((N_BLK_PAD, Q_BLOCK), 0)
    blk_f = blk.astype(F32)
    t_row = start + _iota((1, Q_BLOCK), 1)
    cur = lax.shift_right_logical(t_row, 6)
    causal = blk <= cur
    forced = (blk == 0) | (blk == cur) | (blk == cur - 1)
    has_key = t_row >= CMP_BLOCK - 1
    gt = gatest_ref[0]
    ones8 = jnp.ones((8, Q_BLOCK), F32)
    tail_chunks = min(2, n_chunks)
    head_chunks = n_chunks - tail_chunks
    tail_c0 = jnp.maximum(nck - tail_chunks, 0)
    tail_rows = pl.ds(pl.multiple_of(tail_c0 * CMP_KEY_CHUNK, CMP_KEY_CHUNK), tail_chunks * CMP_KEY_CHUNK)
    key_end = (CMP_STRIDE * (tail_c0 * CMP_KEY_CHUNK + _iota((tail_chunks * CMP_KEY_CHUNK, Q_BLOCK), 0))
               + (CMP_BLOCK - 1) - start)
    tail_bias = tile_heads(jnp.where(key_end <= _iota((tail_chunks * CMP_KEY_CHUNK, Q_BLOCK), 1), 0.0, NEG))

    groups = range(N_GROUPS)
    slope_rows = [_slope_row(g, Q_BLOCK) for g in groups]
    qas = [jnp.concatenate(
        [jnp.concatenate([qt_ref[0, HEAD_DIM * (HPG * g + j):HEAD_DIM * (HPG * g + j + 1), :]
                          for j in range(HPG)], axis=1),
         _slope_feature_rows(slope_rows[g], LANES - HEAD_DIM)], axis=0) for g in groups]

    def delta(g, c):
        return slope_rows[g] * (start - c * chunk_tokens).astype(F32)

    if head_chunks > 0:
        @pl.when(nck > tail_chunks)
        def _():
            sts = [_dot(kc_ref[0, g, 0:head_chunks * CMP_KEY_CHUNK, :], qas[g]) for g in groups]
            dls = [[delta(g, c) + jnp.where(c < nck - tail_chunks, 0.0, MASK_BIG)
                    for c in range(head_chunks)] for g in groups]
            ms = [_col_max(sts[g], dls[g], CMP_KEY_CHUNK) for g in groups]
            ps = [_probs(sts[g], dls[g], ms[g], CMP_KEY_CHUNK) for g in groups]
            for g in groups:
                m_ref[g] = ms[g]
                acc_ref[g] = _dot(vct_ref[0, g, :, 0:head_chunks * CMP_KEY_CHUNK], ps[g])

        @pl.when(nck <= tail_chunks)
        def _():
            m_ref[...] = jnp.full(m_ref.shape, NEG, F32)
            acc_ref[...] = jnp.zeros(acc_ref.shape, F32)
        m_old = [m_ref[g] for g in groups]
        acc_old = [acc_ref[g] for g in groups]
    else:
        m_old = [jnp.full((1, HPG * Q_BLOCK), NEG, F32) for g in groups]
        acc_old = [jnp.zeros((VCT_ROWS, HPG * Q_BLOCK), F32) for g in groups]

    sts = [_dot(kc_ref[0, g, tail_rows, :], qas[g]) + tail_bias for g in groups]
    dls = [[delta(g, tail_c0 + u) for u in range(tail_chunks)] for g in groups]
    m_new = [jnp.maximum(m_old[g], _col_max(sts[g], dls[g], CMP_KEY_CHUNK)) for g in groups]
    ps = [_probs(sts[g], dls[g], m_new[g], CMP_KEY_CHUNK) for g in groups]
    accs = [jnp.exp2(m_old[g] - m_new[g]) * acc_old[g] + _dot(vct_ref[0, g, :, tail_rows], ps[g])
            for g in groups]

    per_head, ranks = [], []
    for g in groups:
        acc = accs[g]
        inv_l = 1.0 / jnp.maximum(acc[HEAD_DIM:HEAD_DIM + 1], 1e-30)
        o = acc[0:HEAD_DIM] * inv_l
        imp_h = acc[VT_ROWS:VCT_ROWS] * inv_l
        imp = sum(imp_h[:, Q_BLOCK * j:Q_BLOCK * (j + 1)] for j in range(HPG))
        imp = jnp.where(has_key, imp, 0.0)
        for j in range(HPG):
            h = HPG * g + j
            per_head.append(jnp.where(has_key, gt[3 * h:3 * h + 1, :] * o[:, Q_BLOCK * j:Q_BLOCK * (j + 1)], 0.0))
        ranks.append(jnp.where(causal, jnp.where(forced, REMOVED, imp), NEG))
    ocmp_ref[0] = jnp.concatenate(per_head, axis=0)

    for _ in range(N_SELECT - N_FORCED):
        for g in groups:
            m = jnp.max(ranks[g], axis=0, keepdims=True)
            idx = jnp.min(jnp.where(ranks[g] == m, blk_f, float(N_BLK_PAD)), axis=0, keepdims=True)
            ranks[g] = jnp.where(blk_f == idx, REMOVED, ranks[g])
    for g in groups:
        sel = jnp.where(causal & (ranks[g] < 2.0 * NEG), 1.0, 0.0)
        selt_ref[0, g] = sel.astype(BF16)
        flags_ref[0, 0, g] = (_dot_nt(ones8, sel) > 0.5).astype(jnp.int32)


def _cmp_topk(qt, kc, vct, gatest):
    bsz, _, t = qt.shape
    ncp = t // CMP_STRIDE
    nqb = t // Q_BLOCK
    return pl.pallas_call(
        functools.partial(_cmp_topk_kernel, ncp=ncp),
        grid=(bsz, nqb),
        in_specs=[pl.BlockSpec((1, NSA_W, Q_BLOCK), lambda b, i: (b, 0, i)),
                  pl.BlockSpec((1, N_GROUPS, ncp, LANES), lambda b, i: (b, 0, 0, 0)),
                  pl.BlockSpec((1, N_GROUPS, VCT_ROWS, ncp), lambda b, i: (b, 0, 0, 0)),
                  pl.BlockSpec((1, GATET_ROWS, Q_BLOCK), lambda b, i: (b, 0, i))],
        out_specs=[pl.BlockSpec((1, NSA_W, Q_BLOCK), lambda b, i: (b, 0, i)),
                   pl.BlockSpec((1, N_GROUPS, N_BLK_PAD, Q_BLOCK), lambda b, i: (b, 0, 0, i)),
                   pl.BlockSpec((1, 1, N_GROUPS, 8, N_BLK_PAD), lambda b, i: (b, i, 0, 0, 0))],
        out_shape=[jax.ShapeDtypeStruct((bsz, NSA_W, t), F32),
                   jax.ShapeDtypeStruct((bsz, N_GROUPS, N_BLK_PAD, t), BF16),
                   jax.ShapeDtypeStruct((bsz, nqb, N_GROUPS, 8, N_BLK_PAD), jnp.int32)],
        scratch_shapes=[pltpu.VMEM((N_GROUPS, 1, HPG * Q_BLOCK), F32),
                        pltpu.VMEM((N_GROUPS, VCT_ROWS, HPG * Q_BLOCK), F32)],
        compiler_params=pltpu.CompilerParams(dimension_semantics=("arbitrary", "arbitrary"),
                                             vmem_limit_bytes=VMEM_LIMIT),
        name="cmp_topk",
    )(qt, kc, vct, gatest)


def _slc_win_kernel(counts_ref, lists_ref, qt_ref, kaug_ref, vt_ref, oh_ref, selt_ref, gatest_ref,
                    ocmp_ref, out_ref, qaug_ref, m_ref, acc_ref, *, nqb):
    b = pl.program_id(0)
    qb = pl.program_id(1)
    step_id = b * nqb + qb
    start = qb * NSA_QB
    width = HPG * NSA_QB
    win_keys = WINDOW + NSA_QB
    win_start = jnp.maximum(start - WINDOW, 0)
    tile_heads = lambda a: jnp.concatenate([a] * HPG, axis=1)
    dist = (start - win_start) + _iota((win_keys, NSA_QB), 1) - _iota((win_keys, NSA_QB), 0)
    win_bias = tile_heads(jnp.where((dist >= 0) & (dist < WINDOW), 0.0, NEG))
    own_bias = tile_heads(jnp.where(_iota((NSA_QB, NSA_QB), 0) <= _iota((NSA_QB, NSA_QB), 1), 0.0, NEG))
    gt = gatest_ref[0]
    slope_rows = [_slope_row(g, NSA_QB) for g in range(N_GROUPS)]

    def normalize(acc):
        return acc[0:HEAD_DIM] / jnp.maximum(acc[HEAD_DIM:HEAD_DIM + 1], 1e-30)

    for g in range(N_GROUPS):
        qaug_ref[g, 0:HEAD_DIM, :] = jnp.concatenate(
            [qt_ref[0, HEAD_DIM * (HPG * g + j):HEAD_DIM * (HPG * g + j + 1), :] for j in range(HPG)],
            axis=1)
        qaug_ref[g, HEAD_DIM:LANES, :] = _slope_feature_rows(slope_rows[g], LANES - HEAD_DIM)
        sel_bias = ((selt_ref[0, g].astype(F32) - 1.0) * MASK_BIG).astype(BF16)
        qaug_ref[g, LANES:2 * LANES, :] = tile_heads(sel_bias)

    win_rows = pl.ds(pl.multiple_of(win_start, KEY_CHUNK), win_keys)
    own_rows = pl.ds(pl.multiple_of(start, KEY_CHUNK), NSA_QB)
    win_st = [_dot(kaug_ref[0, N_GROUPS + g, win_rows, :], qaug_ref[g, 0:LANES, :]) + win_bias
              for g in range(N_GROUPS)]
    own_st = [_dot(jnp.concatenate([kaug_ref[0, g, own_rows, :], oh_ref[own_rows, :]], axis=1),
                   qaug_ref[g]) + own_bias for g in range(N_GROUPS)]
    win_dls = [[slope_rows[g] * (start - win_start - KEY_CHUNK * u).astype(F32)
                for u in range(win_keys // KEY_CHUNK)] for g in range(N_GROUPS)]
    own_dls = [[slope_rows[g] * float(-KEY_CHUNK * u) for u in range(NSA_QB // KEY_CHUNK)]
               for g in range(N_GROUPS)]
    win_m = [_col_max(win_st[g], win_dls[g], KEY_CHUNK) for g in range(N_GROUPS)]
    own_m = [_col_max(own_st[g], own_dls[g], KEY_CHUNK) for g in range(N_GROUPS)]
    win_p = [_probs(win_st[g], win_dls[g], win_m[g], KEY_CHUNK) for g in range(N_GROUPS)]
    own_p = [_probs(own_st[g], own_dls[g], own_m[g], KEY_CHUNK) for g in range(N_GROUPS)]
    o_win = [normalize(_dot(vt_ref[0, N_GROUPS + g, :, win_rows], win_p[g])) for g in range(N_GROUPS)]
    for g in range(N_GROUPS):
        m_ref[g] = own_m[g]
        acc_ref[g] = _dot(vt_ref[0, g, :, own_rows], own_p[g])

    def slc_body(i, carry):
        sts, vss, dlss = [], [], []
        for g in range(N_GROUPS):
            word = lists_ref[(step_id * N_GROUPS + g) * LIST_WORDS + i]
            ks, vs, dls = [], [], []
            for u in range(SLC_BATCH):
                cid = lax.shift_right_logical(word, 8 * u) & 255
                valid = cid < N_BLK_PAD // 2
                c = jnp.where(valid, cid, 0)
                rows = pl.ds(pl.multiple_of(c * KEY_CHUNK, KEY_CHUNK), KEY_CHUNK)
                ks.append(jnp.concatenate([kaug_ref[0, g, rows, :], oh_ref[rows, :]], axis=1))
                vs.append(vt_ref[0, g, :, rows])
                dls.append(slope_rows[g] * (start - c * KEY_CHUNK).astype(F32)
                           + jnp.where(valid, 0.0, MASK_BIG))
            sts.append(_dot(jnp.concatenate(ks, axis=0), qaug_ref[g]))
            vss.append(jnp.concatenate(vs, axis=1))
            dlss.append(dls)
        m_old = [m_ref[g] for g in range(N_GROUPS)]
        m_new = [jnp.maximum(m_old[g], _col_max(sts[g], dlss[g], KEY_CHUNK)) for g in range(N_GROUPS)]
        ps = [_probs(sts[g], dlss[g], m_new[g], KEY_CHUNK) for g in range(N_GROUPS)]
        for g in range(N_GROUPS):
            acc_ref[g] = jnp.exp2(m_old[g] - m_new[g]) * acc_ref[g] + _dot(vss[g], ps[g])
            m_ref[g] = m_new[g]
        return carry

    lax.fori_loop(0, counts_ref[step_id], slc_body, 0)

    per_head = []
    for g in range(N_GROUPS):
        o_slc = normalize(acc_ref[g])
        for j in range(HPG):
            h = HPG * g + j
            lanes = slice(NSA_QB * j, NSA_QB * (j + 1))
            per_head.append(gt[3 * h + 1:3 * h + 2, :] * o_slc[:, lanes]
                            + gt[3 * h + 2:3 * h + 3, :] * o_win[g][:, lanes])
    o_t = jnp.concatenate(per_head, axis=0)
    out_ref[0] = (ocmp_ref[0] + o_t).T.astype(BF16)


def _slc_win(counts, lists, qt, kaug, vt, onehot, selt, gatest, ocmp):
    bsz, _, t = qt.shape
    nqb = t // NSA_QB
    once = lambda shape, imap: pl.BlockSpec(shape, imap, pipeline_mode=pl.Buffered(1))
    grid_spec = pltpu.PrefetchScalarGridSpec(
        num_scalar_prefetch=2,
        grid=(bsz, nqb),
        in_specs=[pl.BlockSpec((1, NSA_W, NSA_QB), lambda b, i, *_: (b, 0, i)),
                  once((1, 2 * N_GROUPS, t, LANES), lambda b, i, *_: (b, 0, 0, 0)),
                  once((1, 2 * N_GROUPS, VT_ROWS, t), lambda b, i, *_: (b, 0, 0, 0)),
                  once((t, N_BLK_PAD), lambda b, i, *_: (0, 0)),
                  pl.BlockSpec((1, N_GROUPS, N_BLK_PAD, NSA_QB), lambda b, i, *_: (b, 0, 0, i)),
                  pl.BlockSpec((1, GATET_ROWS, NSA_QB), lambda b, i, *_: (b, 0, i)),
                  pl.BlockSpec((1, NSA_W, NSA_QB), lambda b, i, *_: (b, 0, i))],
        out_specs=pl.BlockSpec((1, NSA_QB, NSA_W), lambda b, i, *_: (b, i, 0)),
        scratch_shapes=[pltpu.VMEM((N_GROUPS, 2 * LANES, HPG * NSA_QB), BF16),
                        pltpu.VMEM((N_GROUPS, 1, HPG * NSA_QB), F32),
                        pltpu.VMEM((N_GROUPS, VT_ROWS, HPG * NSA_QB), F32)],
    )
    return pl.pallas_call(
        functools.partial(_slc_win_kernel, nqb=nqb),
        grid_spec=grid_spec,
        out_shape=jax.ShapeDtypeStruct((bsz, t, NSA_W), BF16),
        compiler_params=pltpu.CompilerParams(dimension_semantics=("arbitrary", "arbitrary"),
                                             vmem_limit_bytes=VMEM_LIMIT),
        name="slc_win",
    )(counts, lists, qt, kaug, vt, onehot, selt, gatest, ocmp)


def _merge_kernel(x_ref, onsa_ref, u_ref, vn_ref, g_ref, wm_ref, bm_ref, ws_ref, bs_ref,
                  wpa_ref, wpb_ref, wo_ref, h_ref, *, tm):
    x = x_ref[...]
    xn = _rms(x, g_ref[...]).astype(BF16)
    tril = _iota((GMLP_CHUNK, GMLP_CHUNK), 0) >= _iota((GMLP_CHUNK, GMLP_CHUNK), 1)
    sgu_rows = []
    for c in range(tm // GMLP_CHUNK):
        rows = slice(GMLP_CHUNK * c, GMLP_CHUNK * (c + 1))
        cols = []
        for g in range(GMLP_GROUPS):
            lanes = slice(LANES * g, LANES * (g + 1))
            w = jnp.where(tril, ws_ref[g], 0.0).astype(BF16)
            cols.append(_dot(w, vn_ref[rows, lanes]) + bs_ref[:, g:g + 1])
        sgu_rows.append(u_ref[rows, :] * jnp.concatenate(cols, axis=1))
    o_sgu = jnp.concatenate(sgu_rows, axis=0).astype(BF16)
    mg = jax.nn.sigmoid(_dot(xn, wm_ref[...]) + bm_ref[...])
    mixed = (mg[:, :D_MODEL] * _dot(onsa_ref[...], wpa_ref[...])
             + mg[:, D_MODEL:] * _dot(o_sgu, wpb_ref[...]))
    h_ref[...] = x + _dot(mixed.astype(BF16), wo_ref[...])


def _merge(x2, onsa2, u2, vn2, norm_g, wm, bm, ws, bs_t, wpa, wpb, wo, tm):
    n = x2.shape[0]
    row = lambda w: pl.BlockSpec((tm, w), lambda i: (i, 0))
    full = lambda a: pl.BlockSpec(a.shape, lambda i: (0,) * a.ndim)
    return pl.pallas_call(
        functools.partial(_merge_kernel, tm=tm),
        grid=(n // tm,),
        in_specs=[row(D_MODEL), row(NSA_W), row(GMLP_WIDTH), row(GMLP_WIDTH), full(norm_g),
                  full(wm), full(bm), full(ws), full(bs_t), full(wpa), full(wpb), full(wo)],
        out_specs=row(D_MODEL),
        out_shape=jax.ShapeDtypeStruct((n, D_MODEL), F32),
        compiler_params=pltpu.CompilerParams(dimension_semantics=("arbitrary",),
                                             vmem_limit_bytes=VMEM_LIMIT),
        name="merge",
    )(x2, onsa2, u2, vn2, norm_g, wm, bm, ws, bs_t, wpa, wpb, wo)


def _memkv_kernel(mem_ref, g_ref, w_ref, out_ref):
    out_ref[0] = _dot(_rms(mem_ref[0], g_ref[...]).astype(BF16), w_ref[...]).astype(BF16)


def _memkv(mem, norm_g, w):
    bsz, nm, _ = mem.shape
    return pl.pallas_call(
        _memkv_kernel,
        grid=(bsz,),
        in_specs=[pl.BlockSpec((1, nm, D_MODEL), lambda b: (b, 0, 0)),
                  pl.BlockSpec(norm_g.shape, lambda b: (0, 0)),
                  pl.BlockSpec(w.shape, lambda b: (0, 0))],
        out_specs=pl.BlockSpec((1, nm, 2 * MEM_W), lambda b: (b, 0, 0)),
        out_shape=jax.ShapeDtypeStruct((bsz, nm, 2 * MEM_W), BF16),
        compiler_params=pltpu.CompilerParams(dimension_semantics=("arbitrary",),
                                             vmem_limit_bytes=VMEM_LIMIT),
        name="memkv",
    )(mem, norm_g, w)


def _xattn_kernel(h_ref, g_ref, wq_ref, mkv_ref, wo_ref, out_ref, *, tm):
    halves = [slice(0, tm // 2), slice(tm // 2, tm)]
    head_lanes = [slice(MEM_HEAD_DIM * a, MEM_HEAD_DIM * (a + 1)) for a in range(MEM_HEADS)]
    hs = [h_ref[rows, :] for rows in halves]
    hqs = [(_dot(_rms(h, g_ref[...]).astype(BF16), wq_ref[...]) * (MEM_HEAD_DIM ** -0.5 * LOG2E)).astype(BF16)
           for h in hs]
    ss = [[_dot_nt(hq[:, lanes], mkv_ref[0, :, lanes]) for lanes in head_lanes] for hq in hqs]
    es = [[jnp.exp2(s - jnp.max(s, axis=-1, keepdims=True)) for s in s_half] for s_half in ss]
    os = []
    for e_half in es:
        heads = []
        for a, e in enumerate(e_half):
            v = mkv_ref[0, :, MEM_W + MEM_HEAD_DIM * a:MEM_W + MEM_HEAD_DIM * (a + 1)]
            heads.append(_dot(e.astype(BF16), v) * (1.0 / jnp.sum(e, axis=-1, keepdims=True)))
        os.append(jnp.concatenate(heads, axis=1).astype(BF16))
    for rows, h, o in zip(halves, hs, os):
        out_ref[rows, :] = h + _dot(o, wo_ref[...])


def _xattn(h2d, norm_g, wq, mkv, wo, tm, rows_per_batch):
    n = h2d.shape[0]
    nm = mkv.shape[1]
    tiles_per_batch = rows_per_batch // tm
    full = lambda a: pl.BlockSpec(a.shape, lambda i: (0,) * a.ndim)
    return pl.pallas_call(
        functools.partial(_xattn_kernel, tm=tm),
        grid=(n // tm,),
        in_specs=[pl.BlockSpec((tm, D_MODEL), lambda i: (i, 0)), full(norm_g), full(wq),
                  pl.BlockSpec((1, nm, 2 * MEM_W), lambda i: (i // tiles_per_batch, 0, 0)),
                  full(wo)],
        out_specs=pl.BlockSpec((tm, D_MODEL), lambda i: (i, 0)),
        out_shape=jax.ShapeDtypeStruct((n, D_MODEL), F32),
        compiler_params=pltpu.CompilerParams(dimension_semantics=("arbitrary",),
                                             vmem_limit_bytes=VMEM_LIMIT),
        name="xattn",
    )(h2d, norm_g, wq, mkv, wo)


def _ffn_kernel(h_ref, g_ref, wgu_ref, wd_ref, gf_ref, out_ref, *, d_ff):
    h = h_ref[...]
    hn = _rms(h, g_ref[...]).astype(BF16)
    gate = _dot(hn, wgu_ref[:, :d_ff])
    up = _dot(hn, wgu_ref[:, d_ff:])
    act = (jax.nn.silu(gate) * up).astype(BF16)
    y = h + _dot(act, wd_ref[...])
    out_ref[...] = _rms(y, gf_ref[...])


def _ffn(h2d, norm_g, wgu, wd, norm_f, tm):
    n = h2d.shape[0]
    d_ff = wd.shape[0]
    full = lambda a: pl.BlockSpec(a.shape, lambda i: (0,) * a.ndim)
    once = lambda a: pl.BlockSpec(a.shape, lambda i: (0,) * a.ndim, pipeline_mode=pl.Buffered(1))
    return pl.pallas_call(
        functools.partial(_ffn_kernel, d_ff=d_ff),
        grid=(n // tm,),
        in_specs=[pl.BlockSpec((tm, D_MODEL), lambda i: (i, 0)), full(norm_g), once(wgu), once(wd),
                  full(norm_f)],
        out_specs=pl.BlockSpec((tm, D_MODEL), lambda i: (i, 0)),
        out_shape=jax.ShapeDtypeStruct((n, D_MODEL), F32),
        compiler_params=pltpu.CompilerParams(dimension_semantics=("arbitrary",),
                                             vmem_limit_bytes=VMEM_LIMIT),
        name="ffn",
    )(h2d, norm_g, wgu, wd, norm_f)


def _block_diag2(w):
    z = jnp.zeros_like(w)
    return jnp.concatenate([jnp.concatenate([w, z], axis=-1), jnp.concatenate([z, w], axis=-1)], axis=-2)


def _chunk_lists(flags):
    bsz = flags.shape[0]
    n_chunks = N_BLK_PAD // 2
    per_step = NSA_QB // Q_BLOCK
    nqb = flags.shape[1] // per_step
    f = flags[:, :, :, 0, :].reshape(bsz, nqb, per_step, N_GROUPS, n_chunks, 2).max(axis=(2, 5))
    cid = jnp.arange(n_chunks, dtype=jnp.int32)
    own = (NSA_QB // KEY_CHUNK) * jnp.arange(nqb, dtype=jnp.int32)[None, :, None, None]
    active = (f > 0) & (cid < own)
    n_active = active.sum(axis=-1)
    slot = jnp.cumsum(active, axis=-1) - 1
    hit = active[..., :, None] & (slot[..., :, None] == cid)
    ids = jnp.sum(jnp.where(hit, cid[:, None], 0), axis=-2)
    ids = jnp.where(cid < n_active[..., None], ids, 255)
    n_batches = (n_active.max(axis=-1) + SLC_BATCH - 1) // SLC_BATCH
    packed = ids.reshape(bsz, nqb, N_GROUPS, LIST_WORDS, SLC_BATCH)
    words = functools.reduce(jnp.bitwise_or, [packed[..., u] << (8 * u) for u in range(SLC_BATCH)])
    return n_batches.reshape(-1).astype(jnp.int32), words.reshape(-1).astype(jnp.int32)


def kernel(x, mem, norm_mix, w_in, w_cmp_k1, w_cmp_k2, w_cmp_v1, w_cmp_v2, pe_cmp_k, pe_cmp_v, ln_sgu, w_spatial, b_spatial, w_proj_a, w_proj_b, w_merge, b_merge, w_out, norm_mem_q, norm_mem_kv, w_mq, w_mkv, w_mo, norm_ffn, w_gate_up, w_down, norm_final):
    bsz, t, d = x.shape
    depth = norm_mix.shape[0]
    assert d == D_MODEL and t % Q_BLOCK == 0 and t // SEL_BLOCK <= N_BLK_PAD
    assert t // SEL_BLOCK >= N_SELECT and depth == 1 and t % 512 == 0
    n = bsz * t
    tm = 512
    h = x.reshape(n, d)
    c0, c1, c2, c3 = NSA_W, NSA_W + KV_W, NSA_W + 3 * KV_W, NSA_W + 3 * KV_W + 2 * GMLP_WIDTH
    onehot = (jnp.arange(t)[:, None] // SEL_BLOCK == jnp.arange(N_BLK_PAD)[None, :]).astype(BF16)
    pad_cols = lambda w, width: jnp.pad(w, ((0, 0), (0, width - w.shape[1])))
    for l in range(depth):
        wi = w_in[l]
        wk, wv = [], []
        for branch in range(2):
            base = c1 + KV_W * branch
            for g in range(N_GROUPS):
                wk.append(pad_cols(wi[:, base + HEAD_DIM * g:base + HEAD_DIM * (g + 1)], LANES))
                v0 = base + N_GROUPS * HEAD_DIM + HEAD_DIM * g
                wv.append(pad_cols(wi[:, v0:v0 + HEAD_DIM], VT_ROWS))
        wrow = jnp.concatenate([wi[:, c0:c1]] + wk + [wi[:, c2:c3]], axis=1).astype(BF16)
        wnt = jnp.concatenate([wi[:, :c0]] + wv + [pad_cols(wi[:, c3:], GATET_ROWS)], axis=1).T.astype(BF16)
        qt, kvc2, kaug, vt, u2, vn2, gatest = _inproj(
            h, norm_mix[l][None], ln_sgu[l][None], wrow, wnt, tm, bsz, t)

        pe2 = jnp.stack([pe_cmp_k[l], pe_cmp_v[l]])
        pe2 = jnp.concatenate([pe2, pe2], axis=-1)
        w1 = jnp.stack([w_cmp_k1[l], w_cmp_v1[l]]).reshape(2, CMP_BLOCK, HEAD_DIM, CMP_HIDDEN)
        w1bd = _block_diag2(w1).astype(BF16)
        w2k = _block_diag2(w_cmp_k2[l]).astype(BF16)
        w2vt = _block_diag2(w_cmp_v2[l]).T.astype(BF16)
        kc, vct = _compress(kvc2.reshape(bsz, t, KV_W), pe2, w1bd, w2k, w2vt)
        ocmp, selt, flags = _cmp_topk(qt, kc, vct, gatest)
        counts, lists = _chunk_lists(flags)
        onsa = _slc_win(counts, lists, qt, kaug, vt, onehot, selt, gatest, ocmp)

        h = _merge(h, onsa.reshape(n, NSA_W), u2, vn2, norm_mix[l][None], w_merge[l].astype(BF16),
                   b_merge[l][None], w_spatial[l], b_spatial[l].T, w_proj_a[l].astype(BF16),
                   w_proj_b[l].astype(BF16), w_out[l].astype(BF16), tm)

        mkv = _memkv(mem, norm_mem_kv[l][None], w_mkv[l].astype(BF16))
        h = _xattn(h, norm_mem_q[l][None], w_mq[l].astype(BF16), mkv, w_mo[l].astype(BF16), tm, t)
        h = _ffn(h, norm_ffn[l][None], w_gate_up[l].astype(BF16), w_down[l].astype(BF16),
                 norm_final[None], tm)
    return h.reshape(bsz, t, d)
```

```python
import functools

import jax
import jax.numpy as jnp
from jax import lax
from jax.experimental import pallas as pl
from jax.experimental.pallas import tpu as pltpu

F32 = jnp.float32
BF16 = jnp.bfloat16

LANES = 128
D_MODEL = 1024
N_HEADS = 8
HEAD_DIM = 64
N_GROUPS = 2
HPG = N_HEADS // N_GROUPS
CMP_BLOCK = 32
CMP_STRIDE = 16
CMP_HIDDEN = 128
SEL_BLOCK = 64
N_SELECT = 16
WINDOW = 512
Q_BLOCK = 256
KEY_CHUNK = 128
N_BLK_PAD = 128
NSA_W = N_HEADS * HEAD_DIM
KV_W = 2 * N_GROUPS * HEAD_DIM
GMLP_WIDTH = 512
GMLP_GROUPS = 4
GMLP_CHUNK = 128
MEM_HEADS = 4
MEM_HEAD_DIM = 128
MEM_W = MEM_HEADS * MEM_HEAD_DIM
GATE_W = 3 * N_HEADS
NSA_QB = 256
SLC_BATCH = 4
LIST_WORDS = (N_BLK_PAD // 2) // SLC_BATCH
VT_ROWS = 80
VCT_ROWS = VT_ROWS + N_BLK_PAD
CMP_KEY_CHUNK = 128
N_FORCED = 3
TOPK_ROW_STEP = 32
GATET_ROWS = 32
MASK_BIG = 1e30
EPS = 1e-6
NEG = -1e30
REMOVED = -3e38
FORCE_SCORE = 1e6
SLOPES = tuple(2.0 ** (-8.0 * (h + 1) / N_HEADS) for h in range(N_HEADS))
LOG2E = 1.4426950408889634
Q_SCALE = HEAD_DIM ** -0.5 * LOG2E
VMEM_LIMIT = 56 * 1024 * 1024


def _dot(a, b):
    return jnp.dot(a, b, preferred_element_type=F32)


def _dot_nt(a, b):
    return lax.dot_general(a, b, (((1,), (1,)), ((), ())), preferred_element_type=F32)


def _rms(x, g):
    return x * lax.rsqrt(jnp.mean(x * x, axis=-1, keepdims=True) + EPS) * g


def _iota(shape, dim):
    return lax.broadcasted_iota(jnp.int32, shape, dim)


def _slope_row(g, nq):
    return jnp.concatenate(
        [jnp.full((1, nq), SLOPES[HPG * g + j] * LOG2E, F32) for j in range(HPG)], axis=1)


def _slope_feature_rows(slope_row, n_rows):
    hi = slope_row.astype(BF16).astype(F32)
    r = _iota((n_rows, slope_row.shape[1]), 0)
    return jnp.where(r == 0, hi, jnp.where(r == 1, slope_row - hi, 0.0)).astype(BF16)


def _chunk_slabs(st, chunk):
    return [st[chunk * u:chunk * (u + 1)] for u in range(st.shape[0] // chunk)]


def _col_max(st, dls, chunk):
    parts = [s.reshape(chunk // 8, 8, s.shape[1]).max(axis=0) - dl
             for s, dl in zip(_chunk_slabs(st, chunk), dls)]
    return functools.reduce(jnp.maximum, parts).max(axis=0, keepdims=True)


def _probs(st, dls, m, chunk):
    return jnp.concatenate([jnp.exp2(s - (m + dl)).astype(BF16)
                            for s, dl in zip(_chunk_slabs(st, chunk), dls)], axis=0)


def _inproj_kernel(x_ref, g_ref, lng_ref, wrow_ref, wnt_ref,
                   qt_ref, kvc_ref, kaug_ref, vt_ref, u_ref, vn_ref, gatest_ref, *, tm):
    xn = _rms(x_ref[...], g_ref[...]).astype(BF16)
    r = _dot(xn, wrow_ref[...])
    nt = _dot_nt(wnt_ref[...], xn)
    qt_ref[0] = (nt[0:NSA_W] * Q_SCALE).astype(BF16)
    kvc_ref[...] = r[:, 0:KV_W]
    lane = _iota((tm, LANES), 1)
    key_feat = jnp.where((lane == HEAD_DIM) | (lane == HEAD_DIM + 1),
                         _iota((tm, LANES), 0) & (KEY_CHUNK - 1), 0).astype(F32)
    ones_row = jnp.where(_iota((VT_ROWS, tm), 0) == HEAD_DIM, 1.0, 0.0)
    for a in range(2 * N_GROUPS):
        kaug_ref[0, a] = (r[:, KV_W + LANES * a:KV_W + LANES * (a + 1)] + key_feat).astype(BF16)
        vt_ref[0, a] = (nt[NSA_W + VT_ROWS * a:NSA_W + VT_ROWS * (a + 1)] + ones_row).astype(BF16)
    uv = jax.nn.gelu(r[:, KV_W + 2 * N_GROUPS * LANES:])
    u_ref[...] = uv[:, :GMLP_WIDTH]
    v = uv[:, GMLP_WIDTH:]
    vc = v - jnp.mean(v, axis=-1, keepdims=True)
    vn = vc * lax.rsqrt(jnp.mean(vc * vc, axis=-1, keepdims=True) + EPS) * lng_ref[...]
    vn_ref[...] = vn.astype(BF16)
    gatest_ref[0] = jax.nn.sigmoid(nt[NSA_W + 2 * N_GROUPS * VT_ROWS:])


def _inproj(x2, norm_g, ln_g, wrow, wnt, tm, bsz, t):
    n = x2.shape[0]
    tpb = t // tm
    row = lambda w: pl.BlockSpec((tm, w), lambda i: (i, 0))
    full = lambda a: pl.BlockSpec(a.shape, lambda i: (0,) * a.ndim)
    return pl.pallas_call(
        functools.partial(_inproj_kernel, tm=tm),
        grid=(n // tm,),
        in_specs=[row(D_MODEL), full(norm_g), full(ln_g), full(wrow), full(wnt)],
        out_specs=[pl.BlockSpec((1, NSA_W, tm), lambda i: (i // tpb, 0, i % tpb)),
                   row(KV_W),
                   pl.BlockSpec((1, 2 * N_GROUPS, tm, LANES), lambda i: (i // tpb, 0, i % tpb, 0)),
                   pl.BlockSpec((1, 2 * N_GROUPS, VT_ROWS, tm), lambda i: (i // tpb, 0, 0, i % tpb)),
                   row(GMLP_WIDTH), row(GMLP_WIDTH),
                   pl.BlockSpec((1, GATET_ROWS, tm), lambda i: (i // tpb, 0, i % tpb))],
        out_shape=[jax.ShapeDtypeStruct((bsz, NSA_W, t), BF16),
                   jax.ShapeDtypeStruct((n, KV_W), F32),
                   jax.ShapeDtypeStruct((bsz, 2 * N_GROUPS, t, LANES), BF16),
                   jax.ShapeDtypeStruct((bsz, 2 * N_GROUPS, VT_ROWS, t), BF16),
                   jax.ShapeDtypeStruct((n, GMLP_WIDTH), F32),
                   jax.ShapeDtypeStruct((n, GMLP_WIDTH), BF16),
                   jax.ShapeDtypeStruct((bsz, GATET_ROWS, t), F32)],
        compiler_params=pltpu.CompilerParams(dimension_semantics=("arbitrary",),
                                             vmem_limit_bytes=VMEM_LIMIT),
        name="inproj",
    )(x2, norm_g, ln_g, wrow, wnt)


def _compress_kernel(xk_ref, xv_ref, pe_ref, w1_ref, w2k_ref, w2vt_ref, kc_ref, vct_ref, *, nc):
    outs = []
    for j, x_ref in enumerate((xk_ref, xv_ref)):
        a = jnp.zeros((nc, 2 * CMP_HIDDEN), F32)
        b = jnp.zeros((nc, 2 * CMP_HIDDEN), F32)
        for t in range(CMP_STRIDE):
            xt = x_ref[0, pl.ds(t, nc, stride=CMP_STRIDE), :]
            a = a + _dot((xt + pe_ref[j, t:t + 1, :]).astype(BF16), w1_ref[j, t])
            b = b + _dot((xt + pe_ref[j, CMP_STRIDE + t:CMP_STRIDE + t + 1, :]).astype(BF16),
                         w1_ref[j, CMP_STRIDE + t])
        outs.append(jax.nn.gelu(a + pltpu.roll(b, nc - 1, 0)).astype(BF16))
    k2 = _dot(outs[0], w2k_ref[...])
    v_t = _dot_nt(w2vt_ref[...], outs[1])
    lane = _iota((nc, LANES), 1)
    key_feat = jnp.where((lane == HEAD_DIM) | (lane == HEAD_DIM + 1),
                         CMP_STRIDE * (_iota((nc, LANES), 0) & (CMP_KEY_CHUNK - 1)), 0).astype(F32)
    ci =_iota((N_BLK_PAD, nc), 1) * CMP_STRIDE
    sj = _iota((N_BLK_PAD, nc), 0) * SEL_BLOCK
    overlap_t = jnp.where((ci < sj + SEL_BLOCK) & (ci + (CMP_BLOCK - 1) >= sj), 1.0, 0.0).astype(BF16)
    ones_rows = jnp.where(_iota((VT_ROWS - HEAD_DIM, nc), 0) == 0, 1.0, 0.0).astype(BF16)
    for g in range(N_GROUPS):
        kg = k2 if g == 0 else pltpu.roll(k2, HEAD_DIM, 1)
        kc_ref[0, g] = jnp.where(lane < HEAD_DIM, kg, key_feat).astype(BF16)
        vct_ref[0, g, 0:HEAD_DIM, :] = v_t[HEAD_DIM * g:HEAD_DIM * (g + 1), :].astype(BF16)
        vct_ref[0, g, HEAD_DIM:VT_ROWS, :] = ones_rows
        vct_ref[0, g, VT_ROWS:VCT_ROWS, :] = overlap_t


def _compress(kvc3, pe2, w1bd, w2k, w2vt):
    bsz, t, _ = kvc3.shape
    nc = t // CMP_STRIDE
    full = lambda a: pl.BlockSpec(a.shape, lambda b: (0,) * a.ndim)
    return pl.pallas_call(
        functools.partial(_compress_kernel, nc=nc),
        grid=(bsz,),
        in_specs=[pl.BlockSpec((1, t, LANES), lambda b: (b, 0, 0)),
                  pl.BlockSpec((1, t, LANES), lambda b: (b, 0, 1)),
                  full(pe2), full(w1bd), full(w2k), full(w2vt)],
        out_specs=[pl.BlockSpec((1, N_GROUPS, nc, LANES), lambda b: (b, 0, 0, 0)),
                   pl.BlockSpec((1, N_GROUPS, VCT_ROWS, nc), lambda b: (b, 0, 0, 0))],
        out_shape=[jax.ShapeDtypeStruct((bsz, N_GROUPS, nc, LANES), BF16),
                   jax.ShapeDtypeStruct((bsz, N_GROUPS, VCT_ROWS, nc), BF16)],
        compiler_params=pltpu.CompilerParams(dimension_semantics=("arbitrary",),
                                             vmem_limit_bytes=VMEM_LIMIT),
        name="compress",
    )(kvc3, kvc3, pe2, w1bd, w2k, w2vt)


def _cmp_topk_kernel(qt_ref, kc_ref, vct_ref, gatest_ref, ocmp_ref, selt_ref, flags_ref,
                     m_ref, acc_ref, *, ncp):
    qb = pl.program_id(1)
    start = qb * Q_BLOCK
    n_chunks = ncp // CMP_KEY_CHUNK
    nck = (qb * (Q_BLOCK // CMP_STRIDE) + (Q_BLOCK - CMP_BLOCK) // CMP_STRIDE) // CMP_KEY_CHUNK + 1
    chunk_tokens = CMP_KEY_CHUNK * CMP_STRIDE
    tile_heads = lambda a: jnp.concatenate([a] * HPG, axis=1)
    blk = _iota((N_BLK_PAD, Q_BLOCK), 0)
    blk_f = blk.astype(F32)
    t_row = start + _iota((1, Q_BLOCK), 1)
    cur = lax.shift_right_logical(t_row, 6)
    causal = blk <= cur
    forced = (blk == 0) | (blk == cur) | (blk == cur - 1)
    has_key = t_row >= CMP_BLOCK - 1
    gt = gatest_ref[0]
    ones8 = jnp.ones((8, Q_BLOCK), F32)
    tail_chunks = min(2, n_chunks)
    head_chunks = n_chunks - tail_chunks
    tail_c0 = jnp.maximum(nck - tail_chunks, 0)
    tail_rows = pl.ds(pl.multiple_of(tail_c0 * CMP_KEY_CHUNK, CMP_KEY_CHUNK), tail_chunks * CMP_KEY_CHUNK)
    key_end = (CMP_STRIDE * (tail_c0 * CMP_KEY_CHUNK + _iota((tail_chunks * CMP_KEY_CHUNK, Q_BLOCK), 0))
               + (CMP_BLOCK - 1) - start)
    tail_bias = tile_heads(jnp.where(key_end <= _iota((tail_chunks * CMP_KEY_CHUNK, Q_BLOCK), 1), 0.0, NEG))

    groups = range(N_GROUPS)
    slope_rows = [_slope_row(g, Q_BLOCK) for g in groups]
    qas = [jnp.concatenate(
        [jnp.concatenate([qt_ref[0, HEAD_DIM * (HPG * g + j):HEAD_DIM * (HPG * g + j + 1), :]
                          for j in range(HPG)], axis=1),
         _slope_feature_rows(slope_rows[g], LANES - HEAD_DIM)], axis=0) for g in groups]

    def delta(g, c):
        return slope_rows[g] * (start - c * chunk_tokens).astype(F32)

    if head_chunks > 0:
        @pl.when(nck > tail_chunks)
        def _():
            sts = [_dot(kc_ref[0, g, 0:head_chunks * CMP_KEY_CHUNK, :], qas[g]) for g in groups]
            dls = [[delta(g, c) + jnp.where(c < nck - tail_chunks, 0.0, MASK_BIG)
                    for c in range(head_chunks)] for g in groups]
            ms = [_col_max(sts[g], dls[g], CMP_KEY_CHUNK) for g in groups]
            ps = [_probs(sts[g], dls[g], ms[g], CMP_KEY_CHUNK) for g in groups]
            for g in groups:
                m_ref[g] = ms[g]
                acc_ref[g] = _dot(vct_ref[0, g, :, 0:head_chunks * CMP_KEY_CHUNK], ps[g])

        @pl.when(nck <= tail_chunks)
        def _():
            m_ref[...] = jnp.full(m_ref.shape, NEG, F32)
            acc_ref[...] = jnp.zeros(acc_ref.shape, F32)
        m_old = [m_ref[g] for g in groups]
        acc_old = [acc_ref[g] for g in groups]
    else:
        m_old = [jnp.full((1, HPG * Q_BLOCK), NEG, F32) for g in groups]
        acc_old = [jnp.zeros((VCT_ROWS, HPG * Q_BLOCK), F32) for g in groups]

    sts = [_dot(kc_ref[0, g, tail_rows, :], qas[g]) + tail_bias for g in groups]
    dls = [[delta(g, tail_c0 + u) for u in range(tail_chunks)] for g in groups]
    m_new = [jnp.maximum(m_old[g], _col_max(sts[g], dls[g], CMP_KEY_CHUNK)) for g in groups]
    ps = [_probs(sts[g], dls[g], m_new[g], CMP_KEY_CHUNK) for g in groups]
    accs = [jnp.exp2(m_old[g] - m_new[g]) * acc_old[g] + _dot(vct_ref[0, g, :, tail_rows], ps[g])
            for g in groups]

    per_head, ranks = [], []
    for g in groups:
        acc = accs[g]
        inv_l = 1.0 / jnp.maximum(acc[HEAD_DIM:HEAD_DIM + 1], 1e-30)
        o = acc[0:HEAD_DIM] * inv_l
        imp_h = acc[VT_ROWS:VCT_ROWS] * inv_l
        imp = sum(imp_h[:, Q_BLOCK * j:Q_BLOCK * (j + 1)] for j in range(HPG))
        imp = jnp.where(has_key, imp, 0.0)
        for j in range(HPG):
            h = HPG * g + j
            per_head.append(jnp.where(has_key, gt[3 * h:3 * h + 1, :] * o[:, Q_BLOCK * j:Q_BLOCK * (j + 1)], 0.0))
        ranks.append(jnp.where(causal, jnp.where(forced, REMOVED, imp), NEG))
    ocmp_ref[0] = jnp.concatenate(per_head, axis=0)

    causal_rows = (start + Q_BLOCK) // SEL_BLOCK
    for n_rows in range(TOPK_ROW_STEP, N_BLK_PAD + 1, TOPK_ROW_STEP):
        @pl.when((causal_rows > n_rows - TOPK_ROW_STEP) & (causal_rows <= n_rows))
        def _():
            rk = [ranks[g][0:n_rows] for g in groups]
            blk_n = _iota((n_rows, Q_BLOCK), 0)
            bf = blk_n.astype(F32)
            for _ in range(N_SELECT - N_FORCED):
                for g in groups:
                    m = jnp.max(rk[g], axis=0, keepdims=True)
                    idx = jnp.min(jnp.where(rk[g] == m, bf, float(N_BLK_PAD)), axis=0, keepdims=True)
                    rk[g] = jnp.where(bf == idx, REMOVED, rk[g])
            for g in groups:
                sel = jnp.where((blk_n <= cur) & (rk[g] < 2.0 * NEG), 1.0, 0.0)
                selt_ref[0, g, 0:n_rows, :] = sel.astype(BF16)
                if n_rows < N_BLK_PAD:
                    selt_ref[0, g, n_rows:N_BLK_PAD, :] = jnp.zeros((N_BLK_PAD - n_rows, Q_BLOCK), BF16)
    for g in groups:
        flags_ref[0, 0, g] = (_dot_nt(ones8, selt_ref[0, g].astype(F32)) > 0.5).astype(jnp.int32)


def _cmp_topk(qt, kc, vct, gatest):
    bsz, _, t = qt.shape
    ncp = t // CMP_STRIDE
    nqb = t // Q_BLOCK
    return pl.pallas_call(
        functools.partial(_cmp_topk_kernel, ncp=ncp),
        grid=(bsz, nqb),
        in_specs=[pl.BlockSpec((1, NSA_W, Q_BLOCK), lambda b, i: (b, 0, i)),
                  pl.BlockSpec((1, N_GROUPS, ncp, LANES), lambda b, i: (b, 0, 0, 0)),
                  pl.BlockSpec((1, N_GROUPS, VCT_ROWS, ncp), lambda b, i: (b, 0, 0, 0)),
                  pl.BlockSpec((1, GATET_ROWS, Q_BLOCK), lambda b, i: (b, 0, i))],
        out_specs=[pl.BlockSpec((1, NSA_W, Q_BLOCK), lambda b, i: (b, 0, i)),
                   pl.BlockSpec((1, N_GROUPS, N_BLK_PAD, Q_BLOCK), lambda b, i: (b, 0, 0, i)),
                   pl.BlockSpec((1, 1, N_GROUPS, 8, N_BLK_PAD), lambda b, i: (b, i, 0, 0, 0))],
        out_shape=[jax.ShapeDtypeStruct((bsz, NSA_W, t), F32),
                   jax.ShapeDtypeStruct((bsz, N_GROUPS, N_BLK_PAD, t), BF16),
                   jax.ShapeDtypeStruct((bsz, nqb, N_GROUPS, 8, N_BLK_PAD), jnp.int32)],
        scratch_shapes=[pltpu.VMEM((N_GROUPS, 1, HPG * Q_BLOCK), F32),
                        pltpu.VMEM((N_GROUPS, VCT_ROWS, HPG * Q_BLOCK), F32)],
        compiler_params=pltpu.CompilerParams(dimension_semantics=("arbitrary", "arbitrary"),
                                             vmem_limit_bytes=VMEM_LIMIT),
        name="cmp_topk",
    )(qt, kc, vct, gatest)


def _slc_win_kernel(counts_ref, lists_ref, qt_ref, kaug_ref, vt_ref, oh_ref, selt_ref, gatest_ref,
                    ocmp_ref, out_ref, qaug_ref, m_ref, acc_ref, *, nqb):
    b = pl.program_id(0)
    qb = pl.program_id(1)
    step_id = b * nqb + qb
    start = qb * NSA_QB
    width = HPG * NSA_QB
    win_keys = WINDOW + NSA_QB
    win_start = jnp.maximum(start - WINDOW, 0)
    tile_heads = lambda a: jnp.concatenate([a] * HPG, axis=1)
    dist = (start - win_start) + _iota((win_keys, NSA_QB), 1) - _iota((win_keys, NSA_QB), 0)
    win_bias = tile_heads(jnp.where((dist >= 0) & (dist < WINDOW), 0.0, NEG))
    own_bias = tile_heads(jnp.where(_iota((NSA_QB, NSA_QB), 0) <= _iota((NSA_QB, NSA_QB), 1), 0.0, NEG))
    gt = gatest_ref[0]
    slope_rows = [_slope_row(g, NSA_QB) for g in range(N_GROUPS)]

    def normalize(acc):
        return acc[0:HEAD_DIM] / jnp.maximum(acc[HEAD_DIM:HEAD_DIM + 1], 1e-30)

    for g in range(N_GROUPS):
        qaug_ref[g, 0:HEAD_DIM, :] = jnp.concatenate(
            [qt_ref[0, HEAD_DIM * (HPG * g + j):HEAD_DIM * (HPG * g + j + 1), :] for j in range(HPG)],
            axis=1)
        qaug_ref[g, HEAD_DIM:LANES, :] = _slope_feature_rows(slope_rows[g], LANES - HEAD_DIM)
        sel_bias = ((selt_ref[0, g].astype(F32) - 1.0) * MASK_BIG).astype(BF16)
        qaug_ref[g, LANES:2 * LANES, :] = tile_heads(sel_bias)

    win_rows = pl.ds(pl.multiple_of(win_start, KEY_CHUNK), win_keys)
    own_rows = pl.ds(pl.multiple_of(start, KEY_CHUNK), NSA_QB)
    win_st = [_dot(kaug_ref[0, N_GROUPS + g, win_rows, :], qaug_ref[g, 0:LANES, :]) + win_bias
              for g in range(N_GROUPS)]
    own_st = [_dot(jnp.concatenate([kaug_ref[0, g, own_rows, :], oh_ref[own_rows, :]], axis=1),
                   qaug_ref[g]) + own_bias for g in range(N_GROUPS)]
    win_dls = [[slope_rows[g] * (start - win_start - KEY_CHUNK * u).astype(F32)
                for u in range(win_keys // KEY_CHUNK)] for g in range(N_GROUPS)]
    own_dls = [[slope_rows[g] * float(-KEY_CHUNK * u) for u in range(NSA_QB // KEY_CHUNK)]
               for g in range(N_GROUPS)]
    win_m = [_col_max(win_st[g], win_dls[g], KEY_CHUNK) for g in range(N_GROUPS)]
    own_m = [_col_max(own_st[g], own_dls[g], KEY_CHUNK) for g in range(N_GROUPS)]
    win_p = [_probs(win_st[g], win_dls[g], win_m[g], KEY_CHUNK) for g in range(N_GROUPS)]
    own_p = [_probs(own_st[g], own_dls[g], own_m[g], KEY_CHUNK) for g in range(N_GROUPS)]
    o_win = [normalize(_dot(vt_ref[0, N_GROUPS + g, :, win_rows], win_p[g])) for g in range(N_GROUPS)]
    for g in range(N_GROUPS):
        m_ref[g] = own_m[g]
        acc_ref[g] = _dot(vt_ref[0, g, :, own_rows], own_p[g])

    def slc_body(i, carry):
        sts, vss, dlss = [], [], []
        for g in range(N_GROUPS):
            word = lists_ref[(step_id * N_GROUPS + g) * LIST_WORDS + i]
            ks, vs, dls = [], [], []
            for u in range(SLC_BATCH):
                cid = lax.shift_right_logical(word, 8 * u) & 255
                valid = cid < N_BLK_PAD // 2
                c = jnp.where(valid, cid, 0)
                rows = pl.ds(pl.multiple_of(c * KEY_CHUNK, KEY_CHUNK), KEY_CHUNK)
                ks.append(jnp.concatenate([kaug_ref[0, g, rows, :], oh_ref[rows, :]], axis=1))
                vs.append(vt_ref[0, g, :, rows])
                dls.append(slope_rows[g] * (start - c * KEY_CHUNK).astype(F32)
                           + jnp.where(valid, 0.0, MASK_BIG))
            sts.append(_dot(jnp.concatenate(ks, axis=0), qaug_ref[g]))
            vss.append(jnp.concatenate(vs, axis=1))
            dlss.append(dls)
        m_old = [m_ref[g] for g in range(N_GROUPS)]
        m_new = [jnp.maximum(m_old[g], _col_max(sts[g], dlss[g], KEY_CHUNK)) for g in range(N_GROUPS)]
        ps = [_probs(sts[g], dlss[g], m_new[g], KEY_CHUNK) for g in range(N_GROUPS)]
        for g in range(N_GROUPS):
            acc_ref[g] = jnp.exp2(m_old[g] - m_new[g]) * acc_ref[g] + _dot(vss[g], ps[g])
            m_ref[g] = m_new[g]
        return carry

    lax.fori_loop(0, counts_ref[step_id], slc_body, 0)

    per_head = []
    for g in range(N_GROUPS):
        o_slc = normalize(acc_ref[g])
        for j in range(HPG):
            h = HPG * g + j
            lanes = slice(NSA_QB * j, NSA_QB * (j + 1))
            per_head.append(gt[3 * h + 1:3 * h + 2, :] * o_slc[:, lanes]
                            + gt[3 * h + 2:3 * h + 3, :] * o_win[g][:, lanes])
    o_t = jnp.concatenate(per_head, axis=0)
    out_ref[0] = (ocmp_ref[0] + o_t).T.astype(BF16)


def _slc_win(counts, lists, qt, kaug, vt, onehot, selt, gatest, ocmp):
    bsz, _, t = qt.shape
    nqb = t // NSA_QB
    once = lambda shape, imap: pl.BlockSpec(shape, imap, pipeline_mode=pl.Buffered(1))
    grid_spec = pltpu.PrefetchScalarGridSpec(
        num_scalar_prefetch=2,
        grid=(bsz, nqb),
        in_specs=[pl.BlockSpec((1, NSA_W, NSA_QB), lambda b, i, *_: (b, 0, i)),
                  once((1, 2 * N_GROUPS, t, LANES), lambda b, i, *_: (b, 0, 0, 0)),
                  once((1, 2 * N_GROUPS, VT_ROWS, t), lambda b, i, *_: (b, 0, 0, 0)),
                  once((t, N_BLK_PAD), lambda b, i, *_: (0, 0)),
                  pl.BlockSpec((1, N_GROUPS, N_BLK_PAD, NSA_QB), lambda b, i, *_: (b, 0, 0, i)),
                  pl.BlockSpec((1, GATET_ROWS, NSA_QB), lambda b, i, *_: (b, 0, i)),
                  pl.BlockSpec((1, NSA_W, NSA_QB), lambda b, i, *_: (b, 0, i))],
        out_specs=pl.BlockSpec((1, NSA_QB, NSA_W), lambda b, i, *_: (b, i, 0)),
        scratch_shapes=[pltpu.VMEM((N_GROUPS, 2 * LANES, HPG * NSA_QB), BF16),
                        pltpu.VMEM((N_GROUPS, 1, HPG * NSA_QB), F32),
                        pltpu.VMEM((N_GROUPS, VT_ROWS, HPG * NSA_QB), F32)],
    )
    return pl.pallas_call(
        functools.partial(_slc_win_kernel, nqb=nqb),
        grid_spec=grid_spec,
        out_shape=jax.ShapeDtypeStruct((bsz, t, NSA_W), BF16),
        compiler_params=pltpu.CompilerParams(dimension_semantics=("arbitrary", "arbitrary"),
                                             vmem_limit_bytes=VMEM_LIMIT),
        name="slc_win",
    )(counts, lists, qt, kaug, vt, onehot, selt, gatest, ocmp)


def _merge_kernel(x_ref, onsa_ref, u_ref, vn_ref, g_ref, wm_ref, bm_ref, ws_ref, bs_ref,
                  wpa_ref, wpb_ref, wo_ref, h_ref, *, tm):
    x = x_ref[...]
    xn = _rms(x, g_ref[...]).astype(BF16)
    tril = _iota((GMLP_CHUNK, GMLP_CHUNK), 0) >= _iota((GMLP_CHUNK, GMLP_CHUNK), 1)
    sgu_rows = []
    for c in range(tm // GMLP_CHUNK):
        rows = slice(GMLP_CHUNK * c, GMLP_CHUNK * (c + 1))
        cols = []
        for g in range(GMLP_GROUPS):
            lanes = slice(LANES * g, LANES * (g + 1))
            w = jnp.where(tril, ws_ref[g], 0.0).astype(BF16)
            cols.append(_dot(w, vn_ref[rows, lanes]) + bs_ref[:, g:g + 1])
        sgu_rows.append(u_ref[rows, :] * jnp.concatenate(cols, axis=1))
    o_sgu = jnp.concatenate(sgu_rows, axis=0).astype(BF16)
    mg = jax.nn.sigmoid(_dot(xn, wm_ref[...]) + bm_ref[...])
    mixed = (mg[:, :D_MODEL] * _dot(onsa_ref[...], wpa_ref[...])
             + mg[:, D_MODEL:] * _dot(o_sgu, wpb_ref[...]))
    h_ref[...] = x + _dot(mixed.astype(BF16), wo_ref[...])


def _merge(x2, onsa2, u2, vn2, norm_g, wm, bm, ws, bs_t, wpa, wpb, wo, tm):
    n = x2.shape[0]
    row = lambda w: pl.BlockSpec((tm, w), lambda i: (i, 0))
    full = lambda a: pl.BlockSpec(a.shape, lambda i: (0,) * a.ndim)
    return pl.pallas_call(
        functools.partial(_merge_kernel, tm=tm),
        grid=(n // tm,),
        in_specs=[row(D_MODEL), row(NSA_W), row(GMLP_WIDTH), row(GMLP_WIDTH), full(norm_g),
                  full(wm), full(bm), full(ws), full(bs_t), full(wpa), full(wpb), full(wo)],
        out_specs=row(D_MODEL),
        out_shape=jax.ShapeDtypeStruct((n, D_MODEL), F32),
        compiler_params=pltpu.CompilerParams(dimension_semantics=("arbitrary",),
                                             vmem_limit_bytes=VMEM_LIMIT),
        name="merge",
    )(x2, onsa2, u2, vn2, norm_g, wm, bm, ws, bs_t, wpa, wpb, wo)


def _memkv_kernel(mem_ref, g_ref, w_ref, out_ref):
    out_ref[0] = _dot(_rms(mem_ref[0], g_ref[...]).astype(BF16), w_ref[...]).astype(BF16)


def _memkv(mem, norm_g, w):
    bsz, nm, _ = mem.shape
    return pl.pallas_call(
        _memkv_kernel,
        grid=(bsz,),
        in_specs=[pl.BlockSpec((1, nm, D_MODEL), lambda b: (b, 0, 0)),
                  pl.BlockSpec(norm_g.shape, lambda b: (0, 0)),
                  pl.BlockSpec(w.shape, lambda b: (0, 0))],
        out_specs=pl.BlockSpec((1, nm, 2 * MEM_W), lambda b: (b, 0, 0)),
        out_shape=jax.ShapeDtypeStruct((bsz, nm, 2 * MEM_W), BF16),
        compiler_params=pltpu.CompilerParams(dimension_semantics=("arbitrary",),
                                             vmem_limit_bytes=VMEM_LIMIT),
        name="memkv",
    )(mem, norm_g, w)


def _xattn_kernel(h_ref, g_ref, wq_ref, mkv_ref, wo_ref, out_ref, *, tm):
    halves = [slice(0, tm // 2), slice(tm // 2, tm)]
    head_lanes = [slice(MEM_HEAD_DIM * a, MEM_HEAD_DIM * (a + 1)) for a in range(MEM_HEADS)]
    hs = [h_ref[rows, :] for rows in halves]
    hqs = [(_dot(_rms(h, g_ref[...]).astype(BF16), wq_ref[...]) * (MEM_HEAD_DIM ** -0.5 * LOG2E)).astype(BF16)
           for h in hs]
    ss = [[_dot_nt(hq[:, lanes], mkv_ref[0, :, lanes]) for lanes in head_lanes] for hq in hqs]
    es = [[jnp.exp2(s - jnp.max(s, axis=-1, keepdims=True)) for s in s_half] for s_half in ss]
    os = []
    for e_half in es:
        heads = []
        for a, e in enumerate(e_half):
            v = mkv_ref[0, :, MEM_W + MEM_HEAD_DIM * a:MEM_W + MEM_HEAD_DIM * (a + 1)]
            heads.append(_dot(e.astype(BF16), v) * (1.0 / jnp.sum(e, axis=-1, keepdims=True)))
        os.append(jnp.concatenate(heads, axis=1).astype(BF16))
    for rows, h, o in zip(halves, hs, os):
        out_ref[rows, :] = h + _dot(o, wo_ref[...])


def _xattn(h2d, norm_g, wq, mkv, wo, tm, rows_per_batch):
    n = h2d.shape[0]
    nm = mkv.shape[1]
    tiles_per_batch = rows_per_batch // tm
    full = lambda a: pl.BlockSpec(a.shape, lambda i: (0,) * a.ndim)
    return pl.pallas_call(
        functools.partial(_xattn_kernel, tm=tm),
        grid=(n // tm,),
        in_specs=[pl.BlockSpec((tm, D_MODEL), lambda i: (i, 0)), full(norm_g), full(wq),
                  pl.BlockSpec((1, nm, 2 * MEM_W), lambda i: (i // tiles_per_batch, 0, 0)),
                  full(wo)],
        out_specs=pl.BlockSpec((tm, D_MODEL), lambda i: (i, 0)),
        out_shape=jax.ShapeDtypeStruct((n, D_MODEL), F32),
        compiler_params=pltpu.CompilerParams(dimension_semantics=("arbitrary",),
                                             vmem_limit_bytes=VMEM_LIMIT),
        name="xattn",
    )(h2d, norm_g, wq, mkv, wo)


def _ffn_kernel(h_ref, g_ref, wgu_ref, wd_ref, gf_ref, out_ref, *, d_ff):
    h = h_ref[...]
    hn = _rms(h, g_ref[...]).astype(BF16)
    gate = _dot(hn, wgu_ref[:, :d_ff])
    up = _dot(hn, wgu_ref[:, d_ff:])
    act = (jax.nn.silu(gate) * up).astype(BF16)
    y = h + _dot(act, wd_ref[...])
    out_ref[...] = _rms(y, gf_ref[...])


def _ffn(h2d, norm_g, wgu, wd, norm_f, tm):
    n = h2d.shape[0]
    d_ff = wd.shape[0]
    full = lambda a: pl.BlockSpec(a.shape, lambda i: (0,) * a.ndim)
    once = lambda a: pl.BlockSpec(a.shape, lambda i: (0,) * a.ndim, pipeline_mode=pl.Buffered(1))
    return pl.pallas_call(
        functools.partial(_ffn_kernel, d_ff=d_ff),
        grid=(n // tm,),
        in_specs=[pl.BlockSpec((tm, D_MODEL), lambda i: (i, 0)), full(norm_g), once(wgu), once(wd),
                  full(norm_f)],
        out_specs=pl.BlockSpec((tm, D_MODEL), lambda i: (i, 0)),
        out_shape=jax.ShapeDtypeStruct((n, D_MODEL), F32),
        compiler_params=pltpu.CompilerParams(dimension_semantics=("arbitrary",),
                                             vmem_limit_bytes=VMEM_LIMIT),
        name="ffn",
    )(h2d, norm_g, wgu, wd, norm_f)


def _block_diag2(w):
    z = jnp.zeros_like(w)
    return jnp.concatenate([jnp.concatenate([w, z], axis=-1), jnp.concatenate([z, w], axis=-1)], axis=-2)


def _chunk_lists(flags):
    bsz = flags.shape[0]
    n_chunks = N_BLK_PAD // 2
    per_step = NSA_QB // Q_BLOCK
    nqb = flags.shape[1] // per_step
    f = flags[:, :, :, 0, :].reshape(bsz, nqb, per_step, N_GROUPS, n_chunks, 2).max(axis=(2, 5))
    cid = jnp.arange(n_chunks, dtype=jnp.int32)
    own = (NSA_QB // KEY_CHUNK) * jnp.arange(nqb, dtype=jnp.int32)[None, :, None, None]
    active = (f > 0) & (cid < own)
    n_active = active.sum(axis=-1)
    slot = jnp.cumsum(active, axis=-1) - 1
    hit = active[..., :, None] & (slot[..., :, None] == cid)
    ids = jnp.sum(jnp.where(hit, cid[:, None], 0), axis=-2)
    ids = jnp.where(cid < n_active[..., None], ids, 255)
    n_batches = (n_active.max(axis=-1) + SLC_BATCH - 1) // SLC_BATCH
    packed = ids.reshape(bsz, nqb, N_GROUPS, LIST_WORDS, SLC_BATCH)
    words = functools.reduce(jnp.bitwise_or, [packed[..., u] << (8 * u) for u in range(SLC_BATCH)])
    return n_batches.reshape(-1).astype(jnp.int32), words.reshape(-1).astype(jnp.int32)


def kernel(x, mem, norm_mix, w_in, w_cmp_k1, w_cmp_k2, w_cmp_v1, w_cmp_v2, pe_cmp_k, pe_cmp_v, ln_sgu, w_spatial, b_spatial, w_proj_a, w_proj_b, w_merge, b_merge, w_out, norm_mem_q, norm_mem_kv, w_mq, w_mkv, w_mo, norm_ffn, w_gate_up, w_down, norm_final):
    bsz, t, d = x.shape
    depth = norm_mix.shape[0]
    assert d == D_MODEL and t % Q_BLOCK == 0 and t // SEL_BLOCK <= N_BLK_PAD
    assert t // SEL_BLOCK >= N_SELECT and depth == 1 and t % 512 == 0
    n = bsz * t
    tm = 512
    h = x.reshape(n, d)
    c0, c1, c2, c3 = NSA_W, NSA_W + KV_W, NSA_W + 3 * KV_W, NSA_W + 3 * KV_W + 2 * GMLP_WIDTH
    onehot = (jnp.arange(t)[:, None] // SEL_BLOCK == jnp.arange(N_BLK_PAD)[None, :]).astype(BF16)
    pad_cols = lambda w, width: jnp.pad(w, ((0, 0), (0, width - w.shape[1])))
    for l in range(depth):
        wi = w_in[l]
        wk, wv = [], []
        for branch in range(2):
            base = c1 + KV_W * branch
            for g in range(N_GROUPS):
                wk.append(pad_cols(wi[:, base + HEAD_DIM * g:base + HEAD_DIM * (g + 1)], LANES))
                v0 = base + N_GROUPS * HEAD_DIM + HEAD_DIM * g
                wv.append(pad_cols(wi[:, v0:v0 + HEAD_DIM], VT_ROWS))
        wrow = jnp.concatenate([wi[:, c0:c1]] + wk + [wi[:, c2:c3]], axis=1).astype(BF16)
        wnt = jnp.concatenate([wi[:, :c0]] + wv + [pad_cols(wi[:, c3:], GATET_ROWS)], axis=1).T.astype(BF16)
        qt, kvc2, kaug, vt, u2, vn2, gatest = _inproj(
            h, norm_mix[l][None], ln_sgu[l][None], wrow, wnt, tm, bsz, t)

        pe2 = jnp.stack([pe_cmp_k[l], pe_cmp_v[l]])
        pe2 = jnp.concatenate([pe2, pe2], axis=-1)
        w1 = jnp.stack([w_cmp_k1[l], w_cmp_v1[l]]).reshape(2, CMP_BLOCK, HEAD_DIM, CMP_HIDDEN)
        w1bd = _block_diag2(w1).astype(BF16)
        w2k = _block_diag2(w_cmp_k2[l]).astype(BF16)
        w2vt = _block_diag2(w_cmp_v2[l]).T.astype(BF16)
        kc, vct = _compress(kvc2.reshape(bsz, t, KV_W), pe2, w1bd, w2k, w2vt)
        ocmp, selt, flags = _cmp_topk(qt, kc, vct, gatest)
        counts, lists = _chunk_lists(flags)
        onsa = _slc_win(counts, lists, qt, kaug, vt, onehot, selt, gatest, ocmp)

        h = _merge(h, onsa.reshape(n, NSA_W), u2, vn2, norm_mix[l][None], w_merge[l].astype(BF16),
                   b_merge[l][None], w_spatial[l], b_spatial[l].T, w_proj_a[l].astype(BF16),
                   w_proj_b[l].astype(BF16), w_out[l].astype(BF16), tm)

        mkv = _memkv(mem, norm_mem_kv[l][None], w_mkv[l].astype(BF16))
        h = _xattn(h, norm_mem_q[l][None], w_mq[l].astype(BF16), mkv, w_mo[l].astype(BF16), tm, t)
        h = _ffn(h, norm_ffn[l][None], w_gate_up[l].astype(BF16), w_down[l].astype(BF16),
                 norm_final[None], tm)
    return h.reshape(bsz, t, d)
```

```python
import functools

import jax
import jax.numpy as jnp
from jax import lax
from jax.experimental import pallas as pl
from jax.experimental.pallas import tpu as pltpu

F32 = jnp.float32
BF16 = jnp.bfloat16

LANES = 128
D_MODEL = 1024
N_HEADS = 8
HEAD_DIM = 64
N_GROUPS = 2
HPG = N_HEADS // N_GROUPS
CMP_BLOCK = 32
CMP_STRIDE = 16
CMP_HIDDEN = 128
SEL_BLOCK = 64
N_SELECT = 16
WINDOW = 512
Q_BLOCK = 256
KEY_CHUNK = 128
N_BLK_PAD = 128
NSA_W = N_HEADS * HEAD_DIM
KV_W = 2 * N_GROUPS * HEAD_DIM
GMLP_WIDTH = 512
GMLP_GROUPS = 4
GMLP_CHUNK = 128
MEM_HEADS = 4
MEM_HEAD_DIM = 128
MEM_W = MEM_HEADS * MEM_HEAD_DIM
GATE_W = 3 * N_HEADS
NSA_QB = 256
SLC_BATCH = 5
N_KEY_CHUNKS = N_BLK_PAD // 2
CHUNK_ID_BITS = 6
VOID_CHUNK = N_KEY_CHUNKS - 1
LIST_WORDS = -(-N_KEY_CHUNKS // SLC_BATCH)
VT_ROWS = 80
VCT_ROWS = VT_ROWS + N_BLK_PAD
CMP_KEY_CHUNK = 128
N_FORCED = 3
TOPK_ROW_STEP = 32
GATET_ROWS = 32
MASK_BIG = 1e30
EPS = 1e-6
NEG = -1e30
REMOVED = -3e38
FORCE_SCORE = 1e6
SLOPES = tuple(2.0 ** (-8.0 * (h + 1) / N_HEADS) for h in range(N_HEADS))
LOG2E = 1.4426950408889634
Q_SCALE = HEAD_DIM ** -0.5 * LOG2E
VMEM_LIMIT = 56 * 1024 * 1024


def _dot(a, b):
    return jnp.dot(a, b, preferred_element_type=F32)


def _dot_nt(a, b):
    return lax.dot_general(a, b, (((1,), (1,)), ((), ())), preferred_element_type=F32)


def _rms(x, g):
    return x * lax.rsqrt(jnp.mean(x * x, axis=-1, keepdims=True) + EPS) * g


def _iota(shape, dim):
    return lax.broadcasted_iota(jnp.int32, shape, dim)


def _slope_row(g, nq):
    return jnp.concatenate(
        [jnp.full((1, nq), SLOPES[HPG * g + j] * LOG2E, F32) for j in range(HPG)], axis=1)


def _slope_feature_rows(slope_row, n_rows):
    hi = slope_row.astype(BF16).astype(F32)
    r = _iota((n_rows, slope_row.shape[1]), 0)
    return jnp.where(r == 0, hi, jnp.where(r == 1, slope_row - hi, 0.0)).astype(BF16)


def _chunk_slabs(st, chunk):
    return [st[chunk * u:chunk * (u + 1)] for u in range(st.shape[0] // chunk)]


def _col_max(st, dls, chunk):
    parts = [s.reshape(chunk // 8, 8, s.shape[1]).max(axis=0) - dl
             for s, dl in zip(_chunk_slabs(st, chunk), dls)]
    return functools.reduce(jnp.maximum, parts).max(axis=0, keepdims=True)


def _probs(st, dls, m, chunk):
    return jnp.concatenate([jnp.exp2(s - (m + dl)).astype(BF16)
                            for s, dl in zip(_chunk_slabs(st, chunk), dls)], axis=0)


def _inproj_kernel(x_ref, g_ref, lng_ref, wrow_ref, wnt_ref,
                   qt_ref, kvc_ref, kaug_ref, vt_ref, u_ref, vn_ref, gatest_ref, *, tm):
    xn = _rms(x_ref[...], g_ref[...]).astype(BF16)
    r = _dot(xn, wrow_ref[...])
    nt = _dot_nt(wnt_ref[...], xn)
    qt_ref[0] = (nt[0:NSA_W] * Q_SCALE).astype(BF16)
    kvc_ref[...] = r[:, 0:KV_W]
    lane = _iota((tm, LANES), 1)
    key_feat = jnp.where((lane == HEAD_DIM) | (lane == HEAD_DIM + 1),
                         _iota((tm, LANES), 0) & (KEY_CHUNK - 1), 0).astype(F32)
    ones_row = jnp.where(_iota((VT_ROWS, tm), 0) == HEAD_DIM, 1.0, 0.0)
    for a in range(2 * N_GROUPS):
        kaug_ref[0, a] = (r[:, KV_W + LANES * a:KV_W + LANES * (a + 1)] + key_feat).astype(BF16)
        vt_ref[0, a] = (nt[NSA_W + VT_ROWS * a:NSA_W + VT_ROWS * (a + 1)] + ones_row).astype(BF16)
    uv = jax.nn.gelu(r[:, KV_W + 2 * N_GROUPS * LANES:])
    u_ref[...] = uv[:, :GMLP_WIDTH]
    v = uv[:, GMLP_WIDTH:]
    vc = v - jnp.mean(v, axis=-1, keepdims=True)
    vn = vc * lax.rsqrt(jnp.mean(vc * vc, axis=-1, keepdims=True) + EPS) * lng_ref[...]
    vn_ref[...] = vn.astype(BF16)
    gatest_ref[0] = jax.nn.sigmoid(nt[NSA_W + 2 * N_GROUPS * VT_ROWS:])


def _inproj(x2, norm_g, ln_g, wrow, wnt, tm, bsz, t):
    n = x2.shape[0]
    tpb = t // tm
    row = lambda w: pl.BlockSpec((tm, w), lambda i: (i, 0))
    full = lambda a: pl.BlockSpec(a.shape, lambda i: (0,) * a.ndim)
    return pl.pallas_call(
        functools.partial(_inproj_kernel, tm=tm),
        grid=(n // tm,),
        in_specs=[row(D_MODEL), full(norm_g), full(ln_g), full(wrow), full(wnt)],
        out_specs=[pl.BlockSpec((1, NSA_W, tm), lambda i: (i // tpb, 0, i % tpb)),
                   row(KV_W),
                   pl.BlockSpec((1, 2 * N_GROUPS, tm, LANES), lambda i: (i // tpb, 0, i % tpb, 0)),
                   pl.BlockSpec((1, 2 * N_GROUPS, VT_ROWS, tm), lambda i: (i // tpb, 0, 0, i % tpb)),
                   row(GMLP_WIDTH), row(GMLP_WIDTH),
                   pl.BlockSpec((1, GATET_ROWS, tm), lambda i: (i // tpb, 0, i % tpb))],
        out_shape=[jax.ShapeDtypeStruct((bsz, NSA_W, t), BF16),
                   jax.ShapeDtypeStruct((n, KV_W), F32),
                   jax.ShapeDtypeStruct((bsz, 2 * N_GROUPS, t, LANES), BF16),
                   jax.ShapeDtypeStruct((bsz, 2 * N_GROUPS, VT_ROWS, t), BF16),
                   jax.ShapeDtypeStruct((n, GMLP_WIDTH), F32),
                   jax.ShapeDtypeStruct((n, GMLP_WIDTH), BF16),
                   jax.ShapeDtypeStruct((bsz, GATET_ROWS, t), F32)],
        compiler_params=pltpu.CompilerParams(dimension_semantics=("arbitrary",),
                                             vmem_limit_bytes=VMEM_LIMIT),
        name="inproj",
    )(x2, norm_g, ln_g, wrow, wnt)


def _compress_kernel(xk_ref, xv_ref, pe_ref, w1_ref, w2k_ref, w2vt_ref, kc_ref, vct_ref, *, nc):
    outs = []
    for j, x_ref in enumerate((xk_ref, xv_ref)):
        a = jnp.zeros((nc, 2 * CMP_HIDDEN), F32)
        b = jnp.zeros((nc, 2 * CMP_HIDDEN), F32)
        for t in range(CMP_STRIDE):
            xt = x_ref[0, pl.ds(t, nc, stride=CMP_STRIDE), :]
            a = a + _dot((xt + pe_ref[j, t:t + 1, :]).astype(BF16), w1_ref[j, t])
            b = b + _dot((xt + pe_ref[j, CMP_STRIDE + t:CMP_STRIDE + t + 1, :]).astype(BF16),
                         w1_ref[j, CMP_STRIDE + t])
        outs.append(jax.nn.gelu(a + pltpu.roll(b, nc - 1, 0)).astype(BF16))
    k2 = _dot(outs[0], w2k_ref[...])
    v_t = _dot_nt(w2vt_ref[...], outs[1])
    lane = _iota((nc, LANES), 1)
    key_feat = jnp.where((lane == HEAD_DIM) | (lane == HEAD_DIM + 1),
                         CMP_STRIDE * (_iota((nc, LANES), 0) & (CMP_KEY_CHUNK - 1)), 0).astype(F32)
    ci =_iota((N_BLK_PAD, nc), 1) * CMP_STRIDE
    sj = _iota((N_BLK_PAD, nc), 0) * SEL_BLOCK
    overlap_t = jnp.where((ci < sj + SEL_BLOCK) & (ci + (CMP_BLOCK - 1) >= sj), 1.0, 0.0).astype(BF16)
    ones_rows = jnp.where(_iota((VT_ROWS - HEAD_DIM, nc), 0) == 0, 1.0, 0.0).astype(BF16)
    for g in range(N_GROUPS):
        kg = k2 if g == 0 else pltpu.roll(k2, HEAD_DIM, 1)
        kc_ref[0, g] = jnp.where(lane < HEAD_DIM, kg, key_feat).astype(BF16)
        vct_ref[0, g, 0:HEAD_DIM, :] = v_t[HEAD_DIM * g:HEAD_DIM * (g + 1), :].astype(BF16)
        vct_ref[0, g, HEAD_DIM:VT_ROWS, :] = ones_rows
        vct_ref[0, g, VT_ROWS:VCT_ROWS, :] = overlap_t


def _compress(kvc3, pe2, w1bd, w2k, w2vt):
    bsz, t, _ = kvc3.shape
    nc = t // CMP_STRIDE
    full = lambda a: pl.BlockSpec(a.shape, lambda b: (0,) * a.ndim)
    return pl.pallas_call(
        functools.partial(_compress_kernel, nc=nc),
        grid=(bsz,),
        in_specs=[pl.BlockSpec((1, t, LANES), lambda b: (b, 0, 0)),
                  pl.BlockSpec((1, t, LANES), lambda b: (b, 0, 1)),
                  full(pe2), full(w1bd), full(w2k), full(w2vt)],
        out_specs=[pl.BlockSpec((1, N_GROUPS, nc, LANES), lambda b: (b, 0, 0, 0)),
                   pl.BlockSpec((1, N_GROUPS, VCT_ROWS, nc), lambda b: (b, 0, 0, 0))],
        out_shape=[jax.ShapeDtypeStruct((bsz, N_GROUPS, nc, LANES), BF16),
                   jax.ShapeDtypeStruct((bsz, N_GROUPS, VCT_ROWS, nc), BF16)],
        compiler_params=pltpu.CompilerParams(dimension_semantics=("arbitrary",),
                                             vmem_limit_bytes=VMEM_LIMIT),
        name="compress",
    )(kvc3, kvc3, pe2, w1bd, w2k, w2vt)


def _cmp_topk_kernel(qt_ref, kc_ref, vct_ref, gatest_ref, ocmp_ref, selt_ref, flags_ref,
                     m_ref, acc_ref, *, ncp):
    qb = pl.program_id(1)
    start = qb * Q_BLOCK
    n_chunks = ncp // CMP_KEY_CHUNK
    nck = (qb * (Q_BLOCK // CMP_STRIDE) + (Q_BLOCK - CMP_BLOCK) // CMP_STRIDE) // CMP_KEY_CHUNK + 1
    chunk_tokens = CMP_KEY_CHUNK * CMP_STRIDE
    tile_heads = lambda a: jnp.concatenate([a] * HPG, axis=1)
    blk = _iota((N_BLK_PAD, Q_BLOCK), 0)
    blk_f = blk.astype(F32)
    t_row = start + _iota((1, Q_BLOCK), 1)
    cur = lax.shift_right_logical(t_row, 6)
    causal = blk <= cur
    forced = (blk == 0) | (blk == cur) | (blk == cur - 1)
    has_key = t_row >= CMP_BLOCK - 1
    gt = gatest_ref[0]
    ones8 = jnp.ones((8, Q_BLOCK), F32)
    tail_chunks = min(2, n_chunks)
    head_chunks = n_chunks - tail_chunks
    tail_c0 = jnp.maximum(nck - tail_chunks, 0)
    tail_rows = pl.ds(pl.multiple_of(tail_c0 * CMP_KEY_CHUNK, CMP_KEY_CHUNK), tail_chunks * CMP_KEY_CHUNK)
    key_end = (CMP_STRIDE * (tail_c0 * CMP_KEY_CHUNK + _iota((tail_chunks * CMP_KEY_CHUNK, Q_BLOCK), 0))
               + (CMP_BLOCK - 1) - start)
    tail_bias = tile_heads(jnp.where(key_end <= _iota((tail_chunks * CMP_KEY_CHUNK, Q_BLOCK), 1), 0.0, NEG))

    groups = range(N_GROUPS)
    slope_rows = [_slope_row(g, Q_BLOCK) for g in groups]
    qas = [jnp.concatenate(
        [jnp.concatenate([qt_ref[0, HEAD_DIM * (HPG * g + j):HEAD_DIM * (HPG * g + j + 1), :]
                          for j in range(HPG)], axis=1),
         _slope_feature_rows(slope_rows[g], LANES - HEAD_DIM)], axis=0) for g in groups]

    def delta(g, c):
        return slope_rows[g] * (start - c * chunk_tokens).astype(F32)

    if head_chunks > 0:
        @pl.when(nck > tail_chunks)
        def _():
            sts = [_dot(kc_ref[0, g, 0:head_chunks * CMP_KEY_CHUNK, :], qas[g]) for g in groups]
            dls = [[delta(g, c) + jnp.where(c < nck - tail_chunks, 0.0, MASK_BIG)
                    for c in range(head_chunks)] for g in groups]
            ms = [_col_max(sts[g], dls[g], CMP_KEY_CHUNK) for g in groups]
            ps = [_probs(sts[g], dls[g], ms[g], CMP_KEY_CHUNK) for g in groups]
            for g in groups:
                m_ref[g] = ms[g]
                acc_ref[g] = _dot(vct_ref[0, g, :, 0:head_chunks * CMP_KEY_CHUNK], ps[g])

        @pl.when(nck <= tail_chunks)
        def _():
            m_ref[...] = jnp.full(m_ref.shape, NEG, F32)
            acc_ref[...] = jnp.zeros(acc_ref.shape, F32)
        m_old = [m_ref[g] for g in groups]
        acc_old = [acc_ref[g] for g in groups]
    else:
        m_old = [jnp.full((1, HPG * Q_BLOCK), NEG, F32) for g in groups]
        acc_old = [jnp.zeros((VCT_ROWS, HPG * Q_BLOCK), F32) for g in groups]

    sts = [_dot(kc_ref[0, g, tail_rows, :], qas[g]) + tail_bias for g in groups]
    dls = [[delta(g, tail_c0 + u) for u in range(tail_chunks)] for g in groups]
    m_new = [jnp.maximum(m_old[g], _col_max(sts[g], dls[g], CMP_KEY_CHUNK)) for g in groups]
    ps = [_probs(sts[g], dls[g], m_new[g], CMP_KEY_CHUNK) for g in groups]
    accs = [jnp.exp2(m_old[g] - m_new[g]) * acc_old[g] + _dot(vct_ref[0, g, :, tail_rows], ps[g])
            for g in groups]

    per_head, ranks = [], []
    for g in groups:
        acc = accs[g]
        inv_l = 1.0 / jnp.maximum(acc[HEAD_DIM:HEAD_DIM + 1], 1e-30)
        o = acc[0:HEAD_DIM] * inv_l
        imp_h = acc[VT_ROWS:VCT_ROWS] * inv_l
        imp = sum(imp_h[:, Q_BLOCK * j:Q_BLOCK * (j + 1)] for j in range(HPG))
        imp = jnp.where(has_key, imp, 0.0)
        for j in range(HPG):
            h = HPG * g + j
            per_head.append(jnp.where(has_key, gt[3 * h:3 * h + 1, :] * o[:, Q_BLOCK * j:Q_BLOCK * (j + 1)], 0.0))
        ranks.append(jnp.where(causal, jnp.where(forced, REMOVED, imp), NEG))
    ocmp_ref[0] = jnp.concatenate(per_head, axis=0)

    causal_rows = (start + Q_BLOCK) // SEL_BLOCK
    for n_rows in range(TOPK_ROW_STEP, N_BLK_PAD + 1, TOPK_ROW_STEP):
        @pl.when((causal_rows > n_rows - TOPK_ROW_STEP) & (causal_rows <= n_rows))
        def _():
            rk = [ranks[g][0:n_rows] for g in groups]
            blk_n = _iota((n_rows, Q_BLOCK), 0)
            bf = blk_n.astype(F32)
            for _ in range(N_SELECT - N_FORCED):
                for g in groups:
                    m = jnp.max(rk[g], axis=0, keepdims=True)
                    idx = jnp.min(jnp.where(rk[g] == m, bf, float(N_BLK_PAD)), axis=0, keepdims=True)
                    rk[g] = jnp.where(bf == idx, REMOVED, rk[g])
            for g in groups:
                sel = jnp.where((blk_n <= cur) & (rk[g] < 2.0 * NEG), 1.0, 0.0)
                selt_ref[0, g, 0:n_rows, :] = sel.astype(BF16)
                if n_rows < N_BLK_PAD:
                    selt_ref[0, g, n_rows:N_BLK_PAD, :] = jnp.zeros((N_BLK_PAD - n_rows, Q_BLOCK), BF16)
    for g in groups:
        flags_ref[0, 0, g] = (_dot_nt(ones8, selt_ref[0, g].astype(F32)) > 0.5).astype(jnp.int32)


def _cmp_topk(qt, kc, vct, gatest):
    bsz, _, t = qt.shape
    ncp = t // CMP_STRIDE
    nqb = t // Q_BLOCK
    return pl.pallas_call(
        functools.partial(_cmp_topk_kernel, ncp=ncp),
        grid=(bsz, nqb),
        in_specs=[pl.BlockSpec((1, NSA_W, Q_BLOCK), lambda b, i: (b, 0, i)),
                  pl.BlockSpec((1, N_GROUPS, ncp, LANES), lambda b, i: (b, 0, 0, 0)),
                  pl.BlockSpec((1, N_GROUPS, VCT_ROWS, ncp), lambda b, i: (b, 0, 0, 0)),
                  pl.BlockSpec((1, GATET_ROWS, Q_BLOCK), lambda b, i: (b, 0, i))],
        out_specs=[pl.BlockSpec((1, NSA_W, Q_BLOCK), lambda b, i: (b, 0, i)),
                   pl.BlockSpec((1, N_GROUPS, N_BLK_PAD, Q_BLOCK), lambda b, i: (b, 0, 0, i)),
                   pl.BlockSpec((1, 1, N_GROUPS, 8, N_BLK_PAD), lambda b, i: (b, i, 0, 0, 0))],
        out_shape=[jax.ShapeDtypeStruct((bsz, NSA_W, t), F32),
                   jax.ShapeDtypeStruct((bsz, N_GROUPS, N_BLK_PAD, t), BF16),
                   jax.ShapeDtypeStruct((bsz, nqb, N_GROUPS, 8, N_BLK_PAD), jnp.int32)],
        scratch_shapes=[pltpu.VMEM((N_GROUPS, 1, HPG * Q_BLOCK), F32),
                        pltpu.VMEM((N_GROUPS, VCT_ROWS, HPG * Q_BLOCK), F32)],
        compiler_params=pltpu.CompilerParams(dimension_semantics=("arbitrary", "arbitrary"),
                                             vmem_limit_bytes=VMEM_LIMIT),
        name="cmp_topk",
    )(qt, kc, vct, gatest)


def _slc_win_kernel(counts_ref, lists_ref, qt_ref, kaug_ref, vt_ref, oh_ref, selt_ref, gatest_ref,
                    ocmp_ref, out_ref, qaug_ref, m_ref, acc_ref, *, nqb):
    b = pl.program_id(0)
    qb = pl.program_id(1)
    step_id = b * nqb + qb
    start = qb * NSA_QB
    width = HPG * NSA_QB
    win_keys = WINDOW + NSA_QB
    win_start = jnp.maximum(start - WINDOW, 0)
    tile_heads = lambda a: jnp.concatenate([a] * HPG, axis=1)
    dist = (start - win_start) + _iota((win_keys, NSA_QB), 1) - _iota((win_keys, NSA_QB), 0)
    win_bias = tile_heads(jnp.where((dist >= 0) & (dist < WINDOW), 0.0, NEG))
    own_bias = tile_heads(jnp.where(_iota((NSA_QB, NSA_QB), 0) <= _iota((NSA_QB, NSA_QB), 1), 0.0, NEG))
    gt = gatest_ref[0]
    slope_rows = [_slope_row(g, NSA_QB) for g in range(N_GROUPS)]

    def normalize(acc):
        return acc[0:HEAD_DIM] / jnp.maximum(acc[HEAD_DIM:HEAD_DIM + 1], 1e-30)

    for g in range(N_GROUPS):
        qaug_ref[g, 0:HEAD_DIM, :] = jnp.concatenate(
            [qt_ref[0, HEAD_DIM * (HPG * g + j):HEAD_DIM * (HPG * g + j + 1), :] for j in range(HPG)],
            axis=1)
        qaug_ref[g, HEAD_DIM:LANES, :] = _slope_feature_rows(slope_rows[g], LANES - HEAD_DIM)
        sel_bias = ((selt_ref[0, g].astype(F32) - 1.0) * MASK_BIG).astype(BF16)
        qaug_ref[g, LANES:2 * LANES, :] = tile_heads(sel_bias)

    win_rows = pl.ds(pl.multiple_of(win_start, KEY_CHUNK), win_keys)
    own_rows = pl.ds(pl.multiple_of(start, KEY_CHUNK), NSA_QB)
    win_st = [_dot(kaug_ref[0, N_GROUPS + g, win_rows, :], qaug_ref[g, 0:LANES, :]) + win_bias
              for g in range(N_GROUPS)]
    own_st = [_dot(jnp.concatenate([kaug_ref[0, g, own_rows, :], oh_ref[own_rows, :]], axis=1),
                   qaug_ref[g]) + own_bias for g in range(N_GROUPS)]
    win_dls = [[slope_rows[g] * (start - win_start - KEY_CHUNK * u).astype(F32)
                for u in range(win_keys // KEY_CHUNK)] for g in range(N_GROUPS)]
    own_dls = [[slope_rows[g] * float(-KEY_CHUNK * u) for u in range(NSA_QB // KEY_CHUNK)]
               for g in range(N_GROUPS)]
    win_m = [_col_max(win_st[g], win_dls[g], KEY_CHUNK) for g in range(N_GROUPS)]
    own_m = [_col_max(own_st[g], own_dls[g], KEY_CHUNK) for g in range(N_GROUPS)]
    win_p = [_probs(win_st[g], win_dls[g], win_m[g], KEY_CHUNK) for g in range(N_GROUPS)]
    own_p = [_probs(own_st[g], own_dls[g], own_m[g], KEY_CHUNK) for g in range(N_GROUPS)]
    o_win = [normalize(_dot(vt_ref[0, N_GROUPS + g, :, win_rows], win_p[g])) for g in range(N_GROUPS)]
    for g in range(N_GROUPS):
        m_ref[g] = own_m[g]
        acc_ref[g] = _dot(vt_ref[0, g, :, own_rows], own_p[g])

    def slc_body(i, carry):
        sts, vss, dlss = [], [], []
        for g in range(N_GROUPS):
            word = lists_ref[(step_id * N_GROUPS + g) * LIST_WORDS + i]
            ks, vs, dls = [], [], []
            for u in range(SLC_BATCH):
                cid = lax.shift_right_logical(word, CHUNK_ID_BITS * u) & VOID_CHUNK
                valid = cid < VOID_CHUNK
                c = jnp.where(valid, cid, 0)
                rows = pl.ds(pl.multiple_of(c * KEY_CHUNK, KEY_CHUNK), KEY_CHUNK)
                ks.append(jnp.concatenate([kaug_ref[0, g, rows, :], oh_ref[rows, :]], axis=1))
                vs.append(vt_ref[0, g, :, rows])
                dls.append(slope_rows[g] * (start - c * KEY_CHUNK).astype(F32)
                           + jnp.where(valid, 0.0, MASK_BIG))
            sts.append(_dot(jnp.concatenate(ks, axis=0), qaug_ref[g]))
            vss.append(jnp.concatenate(vs, axis=1))
            dlss.append(dls)
        m_old = [m_ref[g] for g in range(N_GROUPS)]
        m_new = [jnp.maximum(m_old[g], _col_max(sts[g], dlss[g], KEY_CHUNK)) for g in range(N_GROUPS)]
        ps = [_probs(sts[g], dlss[g], m_new[g], KEY_CHUNK) for g in range(N_GROUPS)]
        for g in range(N_GROUPS):
            acc_ref[g] = jnp.exp2(m_old[g] - m_new[g]) * acc_ref[g] + _dot(vss[g], ps[g])
            m_ref[g] = m_new[g]
        return carry

    lax.fori_loop(0, counts_ref[step_id], slc_body, 0)

    per_head = []
    for g in range(N_GROUPS):
        o_slc = normalize(acc_ref[g])
        for j in range(HPG):
            h = HPG * g + j
            lanes = slice(NSA_QB * j, NSA_QB * (j + 1))
            per_head.append(gt[3 * h + 1:3 * h + 2, :] * o_slc[:, lanes]
                            + gt[3 * h + 2:3 * h + 3, :] * o_win[g][:, lanes])
    o_t = jnp.concatenate(per_head, axis=0)
    out_ref[0] = (ocmp_ref[0] + o_t).T.astype(BF16)


def _slc_win(counts, lists, qt, kaug, vt, onehot, selt, gatest, ocmp):
    bsz, _, t = qt.shape
    nqb = t // NSA_QB
    once = lambda shape, imap: pl.BlockSpec(shape, imap, pipeline_mode=pl.Buffered(1))
    grid_spec = pltpu.PrefetchScalarGridSpec(
        num_scalar_prefetch=2,
        grid=(bsz, nqb),
        in_specs=[pl.BlockSpec((1, NSA_W, NSA_QB), lambda b, i, *_: (b, 0, i)),
                  once((1, 2 * N_GROUPS, t, LANES), lambda b, i, *_: (b, 0, 0, 0)),
                  once((1, 2 * N_GROUPS, VT_ROWS, t), lambda b, i, *_: (b, 0, 0, 0)),
                  once((t, N_BLK_PAD), lambda b, i, *_: (0, 0)),
                  pl.BlockSpec((1, N_GROUPS, N_BLK_PAD, NSA_QB), lambda b, i, *_: (b, 0, 0, i)),
                  pl.BlockSpec((1, GATET_ROWS, NSA_QB), lambda b, i, *_: (b, 0, i)),
                  pl.BlockSpec((1, NSA_W, NSA_QB), lambda b, i, *_: (b, 0, i))],
        out_specs=pl.BlockSpec((1, NSA_QB, NSA_W), lambda b, i, *_: (b, i, 0)),
        scratch_shapes=[pltpu.VMEM((N_GROUPS, 2 * LANES, HPG * NSA_QB), BF16),
                        pltpu.VMEM((N_GROUPS, 1, HPG * NSA_QB), F32),
                        pltpu.VMEM((N_GROUPS, VT_ROWS, HPG * NSA_QB), F32)],
    )
    return pl.pallas_call(
        functools.partial(_slc_win_kernel, nqb=nqb),
        grid_spec=grid_spec,
        out_shape=jax.ShapeDtypeStruct((bsz, t, NSA_W), BF16),
        compiler_params=pltpu.CompilerParams(dimension_semantics=("arbitrary", "arbitrary"),
                                             vmem_limit_bytes=VMEM_LIMIT),
        name="slc_win",
    )(counts, lists, qt, kaug, vt, onehot, selt, gatest, ocmp)


def _merge_kernel(x_ref, onsa_ref, u_ref, vn_ref, g_ref, wm_ref, bm_ref, ws_ref, bs_ref,
                  wpa_ref, wpb_ref, wo_ref, h_ref, *, tm):
    x = x_ref[...]
    xn = _rms(x, g_ref[...]).astype(BF16)
    tril = _iota((GMLP_CHUNK, GMLP_CHUNK), 0) >= _iota((GMLP_CHUNK, GMLP_CHUNK), 1)
    sgu_rows = []
    for c in range(tm // GMLP_CHUNK):
        rows = slice(GMLP_CHUNK * c, GMLP_CHUNK * (c + 1))
        cols = []
        for g in range(GMLP_GROUPS):
            lanes = slice(LANES * g, LANES * (g + 1))
            w = jnp.where(tril, ws_ref[g], 0.0).astype(BF16)
            cols.append(_dot(w, vn_ref[rows, lanes]) + bs_ref[:, g:g + 1])
        sgu_rows.append(u_ref[rows, :] * jnp.concatenate(cols, axis=1))
    o_sgu = jnp.concatenate(sgu_rows, axis=0).astype(BF16)
    mg = jax.nn.sigmoid(_dot(xn, wm_ref[...]) + bm_ref[...])
    mixed = (mg[:, :D_MODEL] * _dot(onsa_ref[...], wpa_ref[...])
             + mg[:, D_MODEL:] * _dot(o_sgu, wpb_ref[...]))
    h_ref[...] = x + _dot(mixed.astype(BF16), wo_ref[...])


def _merge(x2, onsa2, u2, vn2, norm_g, wm, bm, ws, bs_t, wpa, wpb, wo, tm):
    n = x2.shape[0]
    row = lambda w: pl.BlockSpec((tm, w), lambda i: (i, 0))
    full = lambda a: pl.BlockSpec(a.shape, lambda i: (0,) * a.ndim)
    return pl.pallas_call(
        functools.partial(_merge_kernel, tm=tm),
        grid=(n // tm,),
        in_specs=[row(D_MODEL), row(NSA_W), row(GMLP_WIDTH), row(GMLP_WIDTH), full(norm_g),
                  full(wm), full(bm), full(ws), full(bs_t), full(wpa), full(wpb), full(wo)],
        out_specs=row(D_MODEL),
        out_shape=jax.ShapeDtypeStruct((n, D_MODEL), F32),
        compiler_params=pltpu.CompilerParams(dimension_semantics=("arbitrary",),
                                             vmem_limit_bytes=VMEM_LIMIT),
        name="merge",
    )(x2, onsa2, u2, vn2, norm_g, wm, bm, ws, bs_t, wpa, wpb, wo)


def _memkv_kernel(mem_ref, g_ref, w_ref, out_ref):
    out_ref[0] = _dot(_rms(mem_ref[0], g_ref[...]).astype(BF16), w_ref[...]).astype(BF16)


def _memkv(mem, norm_g, w):
    bsz, nm, _ = mem.shape
    return pl.pallas_call(
        _memkv_kernel,
        grid=(bsz,),
        in_specs=[pl.BlockSpec((1, nm, D_MODEL), lambda b: (b, 0, 0)),
                  pl.BlockSpec(norm_g.shape, lambda b: (0, 0)),
                  pl.BlockSpec(w.shape, lambda b: (0, 0))],
        out_specs=pl.BlockSpec((1, nm, 2 * MEM_W), lambda b: (b, 0, 0)),
        out_shape=jax.ShapeDtypeStruct((bsz, nm, 2 * MEM_W), BF16),
        compiler_params=pltpu.CompilerParams(dimension_semantics=("arbitrary",),
                                             vmem_limit_bytes=VMEM_LIMIT),
        name="memkv",
    )(mem, norm_g, w)


def _xattn_kernel(h_ref, g_ref, wq_ref, mkv_ref, wo_ref, out_ref, *, tm):
    halves = [slice(0, tm // 2), slice(tm // 2, tm)]
    head_lanes = [slice(MEM_HEAD_DIM * a, MEM_HEAD_DIM * (a + 1)) for a in range(MEM_HEADS)]
    hs = [h_ref[rows, :] for rows in halves]
    hqs = [(_dot(_rms(h, g_ref[...]).astype(BF16), wq_ref[...]) * (MEM_HEAD_DIM ** -0.5 * LOG2E)).astype(BF16)
           for h in hs]
    ss = [[_dot_nt(hq[:, lanes], mkv_ref[0, :, lanes]) for lanes in head_lanes] for hq in hqs]
    es = [[jnp.exp2(s - jnp.max(s, axis=-1, keepdims=True)) for s in s_half] for s_half in ss]
    os = []
    for e_half in es:
        heads = []
        for a, e in enumerate(e_half):
            v = mkv_ref[0, :, MEM_W + MEM_HEAD_DIM * a:MEM_W + MEM_HEAD_DIM * (a + 1)]
            heads.append(_dot(e.astype(BF16), v) * (1.0 / jnp.sum(e, axis=-1, keepdims=True)))
        os.append(jnp.concatenate(heads, axis=1).astype(BF16))
    for rows, h, o in zip(halves, hs, os):
        out_ref[rows, :] = h + _dot(o, wo_ref[...])


def _xattn(h2d, norm_g, wq, mkv, wo, tm, rows_per_batch):
    n = h2d.shape[0]
    nm = mkv.shape[1]
    tiles_per_batch = rows_per_batch // tm
    full = lambda a: pl.BlockSpec(a.shape, lambda i: (0,) * a.ndim)
    return pl.pallas_call(
        functools.partial(_xattn_kernel, tm=tm),
        grid=(n // tm,),
        in_specs=[pl.BlockSpec((tm, D_MODEL), lambda i: (i, 0)), full(norm_g), full(wq),
                  pl.BlockSpec((1, nm, 2 * MEM_W), lambda i: (i // tiles_per_batch, 0, 0)),
                  full(wo)],
        out_specs=pl.BlockSpec((tm, D_MODEL), lambda i: (i, 0)),
        out_shape=jax.ShapeDtypeStruct((n, D_MODEL), F32),
        compiler_params=pltpu.CompilerParams(dimension_semantics=("arbitrary",),
                                             vmem_limit_bytes=VMEM_LIMIT),
        name="xattn",
    )(h2d, norm_g, wq, mkv, wo)


def _ffn_kernel(h_ref, g_ref, wgu_ref, wd_ref, gf_ref, out_ref, *, d_ff):
    h = h_ref[...]
    hn = _rms(h, g_ref[...]).astype(BF16)
    gate = _dot(hn, wgu_ref[:, :d_ff])
    up = _dot(hn, wgu_ref[:, d_ff:])
    act = (jax.nn.silu(gate) * up).astype(BF16)
    y = h + _dot(act, wd_ref[...])
    out_ref[...] = _rms(y, gf_ref[...])


def _ffn(h2d, norm_g, wgu, wd, norm_f, tm):
    n = h2d.shape[0]
    d_ff = wd.shape[0]
    full = lambda a: pl.BlockSpec(a.shape, lambda i: (0,) * a.ndim)
    once = lambda a: pl.BlockSpec(a.shape, lambda i: (0,) * a.ndim, pipeline_mode=pl.Buffered(1))
    return pl.pallas_call(
        functools.partial(_ffn_kernel, d_ff=d_ff),
        grid=(n // tm,),
        in_specs=[pl.BlockSpec((tm, D_MODEL), lambda i: (i, 0)), full(norm_g), once(wgu), once(wd),
                  full(norm_f)],
        out_specs=pl.BlockSpec((tm, D_MODEL), lambda i: (i, 0)),
        out_shape=jax.ShapeDtypeStruct((n, D_MODEL), F32),
        compiler_params=pltpu.CompilerParams(dimension_semantics=("arbitrary",),
                                             vmem_limit_bytes=VMEM_LIMIT),
        name="ffn",
    )(h2d, norm_g, wgu, wd, norm_f)


def _block_diag2(w):
    z = jnp.zeros_like(w)
    return jnp.concatenate([jnp.concatenate([w, z], axis=-1), jnp.concatenate([z, w], axis=-1)], axis=-2)


def _chunk_lists(flags):
    bsz = flags.shape[0]
    n_chunks = N_KEY_CHUNKS
    per_step = NSA_QB // Q_BLOCK
    nqb = flags.shape[1] // per_step
    f = flags[:, :, :, 0, :].reshape(bsz, nqb, per_step, N_GROUPS, n_chunks, 2).max(axis=(2, 5))
    cid = jnp.arange(n_chunks, dtype=jnp.int32)
    own = (NSA_QB // KEY_CHUNK) * jnp.arange(nqb, dtype=jnp.int32)[None, :, None, None]
    active = (f > 0) & (cid < own)
    n_active = active.sum(axis=-1)
    slot = jnp.cumsum(active, axis=-1) - 1
    hit = active[..., :, None] & (slot[..., :, None] == cid)
    ids = jnp.sum(jnp.where(hit, cid[:, None], 0), axis=-2)
    ids = jnp.where(cid < n_active[..., None], ids, VOID_CHUNK)
    ids = jnp.pad(ids, ((0, 0), (0, 0), (0, 0), (0, LIST_WORDS * SLC_BATCH - n_chunks)),
                  constant_values=VOID_CHUNK)
    n_batches = (n_active.max(axis=-1) + SLC_BATCH - 1) // SLC_BATCH
    packed = ids.reshape(bsz, nqb, N_GROUPS, LIST_WORDS, SLC_BATCH)
    words = functools.reduce(jnp.bitwise_or,
                             [packed[..., u] << (CHUNK_ID_BITS * u) for u in range(SLC_BATCH)])
    return n_batches.reshape(-1).astype(jnp.int32), words.reshape(-1).astype(jnp.int32)


def kernel(x, mem, norm_mix, w_in, w_cmp_k1, w_cmp_k2, w_cmp_v1, w_cmp_v2, pe_cmp_k, pe_cmp_v, ln_sgu, w_spatial, b_spatial, w_proj_a, w_proj_b, w_merge, b_merge, w_out, norm_mem_q, norm_mem_kv, w_mq, w_mkv, w_mo, norm_ffn, w_gate_up, w_down, norm_final):
    bsz, t, d = x.shape
    depth = norm_mix.shape[0]
    assert d == D_MODEL and t % Q_BLOCK == 0 and t // SEL_BLOCK <= N_BLK_PAD
    assert t // SEL_BLOCK >= N_SELECT and depth == 1 and t % 512 == 0
    n = bsz * t
    tm = 512
    h = x.reshape(n, d)
    c0, c1, c2, c3 = NSA_W, NSA_W + KV_W, NSA_W + 3 * KV_W, NSA_W + 3 * KV_W + 2 * GMLP_WIDTH
    onehot = (jnp.arange(t)[:, None] // SEL_BLOCK == jnp.arange(N_BLK_PAD)[None, :]).astype(BF16)
    pad_cols = lambda w, width: jnp.pad(w, ((0, 0), (0, width - w.shape[1])))
    for l in range(depth):
        wi = w_in[l]
        wk, wv = [], []
        for branch in range(2):
            base = c1 + KV_W * branch
            for g in range(N_GROUPS):
                wk.append(pad_cols(wi[:, base + HEAD_DIM * g:base + HEAD_DIM * (g + 1)], LANES))
                v0 = base + N_GROUPS * HEAD_DIM + HEAD_DIM * g
                wv.append(pad_cols(wi[:, v0:v0 + HEAD_DIM], VT_ROWS))
        wrow = jnp.concatenate([wi[:, c0:c1]] + wk + [wi[:, c2:c3]], axis=1).astype(BF16)
        wnt = jnp.concatenate([wi[:, :c0]] + wv + [pad_cols(wi[:, c3:], GATET_ROWS)], axis=1).T.astype(BF16)
        qt, kvc2, kaug, vt, u2, vn2, gatest = _inproj(
            h, norm_mix[l][None], ln_sgu[l][None], wrow, wnt, tm, bsz, t)

        pe2 = jnp.stack([pe_cmp_k[l], pe_cmp_v[l]])
        pe2 = jnp.concatenate([pe2, pe2], axis=-1)
        w1 = jnp.stack([w_cmp_k1[l], w_cmp_v1[l]]).reshape(2, CMP_BLOCK, HEAD_DIM, CMP_HIDDEN)
        w1bd = _block_diag2(w1).astype(BF16)
        w2k = _block_diag2(w_cmp_k2[l]).astype(BF16)
        w2vt = _block_diag2(w_cmp_v2[l]).T.astype(BF16)
        kc, vct = _compress(kvc2.reshape(bsz, t, KV_W), pe2, w1bd, w2k, w2vt)
        ocmp, selt, flags = _cmp_topk(qt, kc, vct, gatest)
        counts, lists = _chunk_lists(flags)
        onsa = _slc_win(counts, lists, qt, kaug, vt, onehot, selt, gatest, ocmp)

        h = _merge(h, onsa.reshape(n, NSA_W), u2, vn2, norm_mix[l][None], w_merge[l].astype(BF16),
                   b_merge[l][None], w_spatial[l], b_spatial[l].T, w_proj_a[l].astype(BF16),
                   w_proj_b[l].astype(BF16), w_out[l].astype(BF16), tm)

        mkv = _memkv(mem, norm_mem_kv[l][None], w_mkv[l].astype(BF16))
        h = _xattn(h, norm_mem_q[l][None], w_mq[l].astype(BF16), mkv, w_mo[l].astype(BF16), tm, t)
        h = _ffn(h, norm_ffn[l][None], w_gate_up[l].astype(BF16), w_down[l].astype(BF16),
                 norm_final[None], tm)
    return h.reshape(bsz, t, d)
```

```python
import functools

import jax
import jax.numpy as jnp
from jax import lax
from jax.experimental import pallas as pl
from jax.experimental.pallas import tpu as pltpu

F32 = jnp.float32
BF16 = jnp.bfloat16

LANES = 128
D_MODEL = 1024
N_HEADS = 8
HEAD_DIM = 64
N_GROUPS = 2
HPG = N_HEADS // N_GROUPS
CMP_BLOCK = 32
CMP_STRIDE = 16
CMP_HIDDEN = 128
SEL_BLOCK = 64
N_SELECT = 16
WINDOW = 512
Q_BLOCK = 256
KEY_CHUNK = 128
N_BLK_PAD = 128
NSA_W = N_HEADS * HEAD_DIM
KV_W = 2 * N_GROUPS * HEAD_DIM
GMLP_WIDTH = 512
GMLP_GROUPS = 4
GMLP_CHUNK = 128
MEM_HEADS = 4
MEM_HEAD_DIM = 128
MEM_W = MEM_HEADS * MEM_HEAD_DIM
GATE_W = 3 * N_HEADS
NSA_QB = 256
SLC_BATCH = 5
N_KEY_CHUNKS = N_BLK_PAD // 2
CHUNK_ID_BITS = 6
VOID_CHUNK = N_KEY_CHUNKS - 1
LIST_WORDS = -(-N_KEY_CHUNKS // SLC_BATCH)
VT_ROWS = 80
VCT_ROWS = VT_ROWS + N_BLK_PAD
CMP_KEY_CHUNK = 128
N_FORCED = 3
TOPK_ROW_STEP = 64
CMP_BLOCKS = 2
GATET_ROWS = 32
MASK_BIG = 1e30
EPS = 1e-6
NEG = -1e30
REMOVED = -3e38
FORCE_SCORE = 1e6
SLOPES = tuple(2.0 ** (-8.0 * (h + 1) / N_HEADS) for h in range(N_HEADS))
LOG2E = 1.4426950408889634
Q_SCALE = HEAD_DIM ** -0.5 * LOG2E
VMEM_LIMIT = 56 * 1024 * 1024


def _dot(a, b):
    return jnp.dot(a, b, preferred_element_type=F32)


def _dot_nt(a, b):
    return lax.dot_general(a, b, (((1,), (1,)), ((), ())), preferred_element_type=F32)


def _rms(x, g):
    return x * lax.rsqrt(jnp.mean(x * x, axis=-1, keepdims=True) + EPS) * g


def _iota(shape, dim):
    return lax.broadcasted_iota(jnp.int32, shape, dim)


def _slope_row(g, nq):
    return jnp.concatenate(
        [jnp.full((1, nq), SLOPES[HPG * g + j] * LOG2E, F32) for j in range(HPG)], axis=1)


def _slope_feature_rows(slope_row, n_rows):
    hi = slope_row.astype(BF16).astype(F32)
    r = _iota((n_rows, slope_row.shape[1]), 0)
    return jnp.where(r == 0, hi, jnp.where(r == 1, slope_row - hi, 0.0)).astype(BF16)


def _chunk_slabs(st, chunk):
    return [st[chunk * u:chunk * (u + 1)] for u in range(st.shape[0] // chunk)]


def _col_max(st, dls, chunk):
    parts = [s.reshape(chunk // 8, 8, s.shape[1]).max(axis=0) - dl
             for s, dl in zip(_chunk_slabs(st, chunk), dls)]
    return functools.reduce(jnp.maximum, parts).max(axis=0, keepdims=True)


def _probs(st, dls, m, chunk):
    return jnp.concatenate([jnp.exp2(s - (m + dl)).astype(BF16)
                            for s, dl in zip(_chunk_slabs(st, chunk), dls)], axis=0)


def _inproj_kernel(x_ref, g_ref, lng_ref, wrow_ref, wnt_ref,
                   qt_ref, kvc_ref, kaug_ref, vt_ref, u_ref, vn_ref, gatest_ref, *, tm):
    xn = _rms(x_ref[...], g_ref[...]).astype(BF16)
    r = _dot(xn, wrow_ref[...])
    nt = _dot_nt(wnt_ref[...], xn)
    qt_ref[0] = (nt[0:NSA_W] * Q_SCALE).astype(BF16)
    kvc_ref[...] = r[:, 0:KV_W]
    lane = _iota((tm, LANES), 1)
    key_feat = jnp.where((lane == HEAD_DIM) | (lane == HEAD_DIM + 1),
                         _iota((tm, LANES), 0) & (KEY_CHUNK - 1), 0).astype(F32)
    ones_row = jnp.where(_iota((VT_ROWS, tm), 0) == HEAD_DIM, 1.0, 0.0)
    for a in range(2 * N_GROUPS):
        kaug_ref[0, a] = (r[:, KV_W + LANES * a:KV_W + LANES * (a + 1)] + key_feat).astype(BF16)
        vt_ref[0, a] = (nt[NSA_W + VT_ROWS * a:NSA_W + VT_ROWS * (a + 1)] + ones_row).astype(BF16)
    uv = jax.nn.gelu(r[:, KV_W + 2 * N_GROUPS * LANES:])
    u_ref[...] = uv[:, :GMLP_WIDTH]
    v = uv[:, GMLP_WIDTH:]
    vc = v - jnp.mean(v, axis=-1, keepdims=True)
    vn = vc * lax.rsqrt(jnp.mean(vc * vc, axis=-1, keepdims=True) + EPS) * lng_ref[...]
    vn_ref[...] = vn.astype(BF16)
    gatest_ref[0] = jax.nn.sigmoid(nt[NSA_W + 2 * N_GROUPS * VT_ROWS:])


def _inproj(x2, norm_g, ln_g, wrow, wnt, tm, bsz, t):
    n = x2.shape[0]
    tpb = t // tm
    row = lambda w: pl.BlockSpec((tm, w), lambda i: (i, 0))
    full = lambda a: pl.BlockSpec(a.shape, lambda i: (0,) * a.ndim)
    return pl.pallas_call(
        functools.partial(_inproj_kernel, tm=tm),
        grid=(n // tm,),
        in_specs=[row(D_MODEL), full(norm_g), full(ln_g), full(wrow), full(wnt)],
        out_specs=[pl.BlockSpec((1, NSA_W, tm), lambda i: (i // tpb, 0, i % tpb)),
                   row(KV_W),
                   pl.BlockSpec((1, 2 * N_GROUPS, tm, LANES), lambda i: (i // tpb, 0, i % tpb, 0)),
                   pl.BlockSpec((1, 2 * N_GROUPS, VT_ROWS, tm), lambda i: (i // tpb, 0, 0, i % tpb)),
                   row(GMLP_WIDTH), row(GMLP_WIDTH),
                   pl.BlockSpec((1, GATET_ROWS, tm), lambda i: (i // tpb, 0, i % tpb))],
        out_shape=[jax.ShapeDtypeStruct((bsz, NSA_W, t), BF16),
                   jax.ShapeDtypeStruct((n, KV_W), F32),
                   jax.ShapeDtypeStruct((bsz, 2 * N_GROUPS, t, LANES), BF16),
                   jax.ShapeDtypeStruct((bsz, 2 * N_GROUPS, VT_ROWS, t), BF16),
                   jax.ShapeDtypeStruct((n, GMLP_WIDTH), F32),
                   jax.ShapeDtypeStruct((n, GMLP_WIDTH), BF16),
                   jax.ShapeDtypeStruct((bsz, GATET_ROWS, t), F32)],
        compiler_params=pltpu.CompilerParams(dimension_semantics=("arbitrary",),
                                             vmem_limit_bytes=VMEM_LIMIT),
        name="inproj",
    )(x2, norm_g, ln_g, wrow, wnt)


def _compress_kernel(xk_ref, xv_ref, pe_ref, w1_ref, w2k_ref, w2vt_ref, kc_ref, vct_ref, *, nc):
    outs = []
    for j, x_ref in enumerate((xk_ref, xv_ref)):
        a = jnp.zeros((nc, 2 * CMP_HIDDEN), F32)
        b = jnp.zeros((nc, 2 * CMP_HIDDEN), F32)
        for t in range(CMP_STRIDE):
            xt = x_ref[0, pl.ds(t, nc, stride=CMP_STRIDE), :]
            a = a + _dot((xt + pe_ref[j, t:t + 1, :]).astype(BF16), w1_ref[j, t])
            b = b + _dot((xt + pe_ref[j, CMP_STRIDE + t:CMP_STRIDE + t + 1, :]).astype(BF16),
                         w1_ref[j, CMP_STRIDE + t])
        outs.append(jax.nn.gelu(a + pltpu.roll(b, nc - 1, 0)).astype(BF16))
    k2 = _dot(outs[0], w2k_ref[...])
    v_t = _dot_nt(w2vt_ref[...], outs[1])
    lane = _iota((nc, LANES), 1)
    key_feat = jnp.where((lane == HEAD_DIM) | (lane == HEAD_DIM + 1),
                         CMP_STRIDE * (_iota((nc, LANES), 0) & (CMP_KEY_CHUNK - 1)), 0).astype(F32)
    ci =_iota((N_BLK_PAD, nc), 1) * CMP_STRIDE
    sj = _iota((N_BLK_PAD, nc), 0) * SEL_BLOCK
    overlap_t = jnp.where((ci < sj + SEL_BLOCK) & (ci + (CMP_BLOCK - 1) >= sj), 1.0, 0.0).astype(BF16)
    ones_rows = jnp.where(_iota((VT_ROWS - HEAD_DIM, nc), 0) == 0, 1.0, 0.0).astype(BF16)
    for g in range(N_GROUPS):
        kg = k2 if g == 0 else pltpu.roll(k2, HEAD_DIM, 1)
        kc_ref[0, g] = jnp.where(lane < HEAD_DIM, kg, key_feat).astype(BF16)
        vct_ref[0, g, 0:HEAD_DIM, :] = v_t[HEAD_DIM * g:HEAD_DIM * (g + 1), :].astype(BF16)
        vct_ref[0, g, HEAD_DIM:VT_ROWS, :] = ones_rows
        vct_ref[0, g, VT_ROWS:VCT_ROWS, :] = overlap_t


def _compress(kvc3, pe2, w1bd, w2k, w2vt):
    bsz, t, _ = kvc3.shape
    nc = t // CMP_STRIDE
    full = lambda a: pl.BlockSpec(a.shape, lambda b: (0,) * a.ndim)
    return pl.pallas_call(
        functools.partial(_compress_kernel, nc=nc),
        grid=(bsz,),
        in_specs=[pl.BlockSpec((1, t, LANES), lambda b: (b, 0, 0)),
                  pl.BlockSpec((1, t, LANES), lambda b: (b, 0, 1)),
                  full(pe2), full(w1bd), full(w2k), full(w2vt)],
        out_specs=[pl.BlockSpec((1, N_GROUPS, nc, LANES), lambda b: (b, 0, 0, 0)),
                   pl.BlockSpec((1, N_GROUPS, VCT_ROWS, nc), lambda b: (b, 0, 0, 0))],
        out_shape=[jax.ShapeDtypeStruct((bsz, N_GROUPS, nc, LANES), BF16),
                   jax.ShapeDtypeStruct((bsz, N_GROUPS, VCT_ROWS, nc), BF16)],
        compiler_params=pltpu.CompilerParams(dimension_semantics=("arbitrary",),
                                             vmem_limit_bytes=VMEM_LIMIT),
        name="compress",
    )(kvc3, kvc3, pe2, w1bd, w2k, w2vt)


def _cmp_topk_kernel(qt_ref, kc_ref, vct_ref, gatest_ref, ocmp_ref, selt_ref, flags_ref,
                     m_ref, acc_ref, *, ncp):
    step = pl.program_id(1)
    n_chunks = ncp // CMP_KEY_CHUNK
    chunk_tokens = CMP_KEY_CHUNK * CMP_STRIDE
    tile_heads = lambda a: jnp.concatenate([a] * HPG, axis=1)
    tail_chunks = min(2, n_chunks)
    head_chunks = n_chunks - tail_chunks
    tail_keys = tail_chunks * CMP_KEY_CHUNK
    blocks = range(CMP_BLOCKS)
    groups = range(N_GROUPS)
    items = [(bi, g) for bi in blocks for g in groups]
    start = [(step * CMP_BLOCKS + bi) * Q_BLOCK for bi in blocks]
    qlanes = [slice(Q_BLOCK * bi, Q_BLOCK * (bi + 1)) for bi in blocks]
    nck = [(start[bi] + Q_BLOCK - CMP_BLOCK) // CMP_STRIDE // CMP_KEY_CHUNK + 1 for bi in blocks]
    tail_c0 = [jnp.maximum(nck[bi] - tail_chunks, 0) for bi in blocks]
    tail_rows = [pl.ds(pl.multiple_of(tail_c0[bi] * CMP_KEY_CHUNK, CMP_KEY_CHUNK), tail_keys)
                 for bi in blocks]
    gt = gatest_ref[0]
    slope_rows = [_slope_row(g, Q_BLOCK) for g in groups]
    qas = {(bi, g): jnp.concatenate(
        [jnp.concatenate([qt_ref[0, HEAD_DIM * (HPG * g + j):HEAD_DIM * (HPG * g + j + 1), qlanes[bi]]
                          for j in range(HPG)], axis=1),
         _slope_feature_rows(slope_rows[g], LANES - HEAD_DIM)], axis=0) for bi, g in items}

    def delta(bi, g, c):
        return slope_rows[g] * (start[bi] - c * chunk_tokens).astype(F32)

    for bi in blocks:
        if head_chunks > 0:
            @pl.when(nck[bi] > tail_chunks)
            def _():
                sts = [_dot(kc_ref[0, g, 0:head_chunks * CMP_KEY_CHUNK, :], qas[bi, g]) for g in groups]
                dls = [[delta(bi, g, c) + jnp.where(c < nck[bi] - tail_chunks, 0.0, MASK_BIG)
                        for c in range(head_chunks)] for g in groups]
                ms = [_col_max(sts[g], dls[g], CMP_KEY_CHUNK) for g in groups]
                ps = [_probs(sts[g], dls[g], ms[g], CMP_KEY_CHUNK) for g in groups]
                for g in groups:
                    m_ref[bi, g] = ms[g]
                    acc_ref[bi, g] = _dot(vct_ref[0, g, :, 0:head_chunks * CMP_KEY_CHUNK], ps[g])

            @pl.when(nck[bi] <= tail_chunks)
            def _():
                m_ref[bi] = jnp.full(m_ref.shape[1:], NEG, F32)
                acc_ref[bi] = jnp.zeros(acc_ref.shape[1:], F32)

    def tail_and_select(n_rows):
        if head_chunks > 0:
            m_old = {it: m_ref[it[0], it[1]] for it in items}
            acc_old = {it: acc_ref[it[0], it[1]] for it in items}
        else:
            m_old = {it: jnp.full((1, HPG * Q_BLOCK), NEG, F32) for it in items}
            acc_old = {it: jnp.zeros((VCT_ROWS, HPG * Q_BLOCK), F32) for it in items}
        key_row = _iota((tail_keys, Q_BLOCK), 0)
        q_lane = _iota((tail_keys, Q_BLOCK), 1)
        tail_bias = []
        for bi in blocks:
            key_end = CMP_STRIDE * (tail_c0[bi] * CMP_KEY_CHUNK + key_row) + (CMP_BLOCK - 1) - start[bi]
            tail_bias.append(tile_heads(jnp.where(key_end <= q_lane, 0.0, NEG)))
        sts = {(bi, g): _dot(kc_ref[0, g, tail_rows[bi], :], qas[bi, g]) + tail_bias[bi] for bi, g in items}
        dls = {(bi, g): [delta(bi, g, tail_c0[bi] + u) for u in range(tail_chunks)] for bi, g in items}
        m_new = {it: jnp.maximum(m_old[it], _col_max(sts[it], dls[it], CMP_KEY_CHUNK)) for it in items}
        ps = {it: _probs(sts[it], dls[it], m_new[it], CMP_KEY_CHUNK) for it in items}
        accs = {(bi, g): jnp.exp2(m_old[bi, g] - m_new[bi, g]) * acc_old[bi, g]
                + _dot(vct_ref[0, g, :, tail_rows[bi]], ps[bi, g]) for bi, g in items}

        blk_n = _iota((n_rows, Q_BLOCK), 0)
        bf = blk_n.astype(F32)
        rk, cur = {}, []
        for bi in blocks:
            t_row = start[bi] + _iota((1, Q_BLOCK), 1)
            cur.append(lax.shift_right_logical(t_row, 6))
            forced = (blk_n == 0) | (blk_n == cur[bi]) | (blk_n == cur[bi] - 1)
            has_key = t_row >= CMP_BLOCK - 1
            per_head = []
            for g in groups:
                acc = accs[bi, g]
                inv_l = 1.0 / jnp.maximum(acc[HEAD_DIM:HEAD_DIM + 1], 1e-30)
                o = acc[0:HEAD_DIM] * inv_l
                imp_h = acc[VT_ROWS:VT_ROWS + n_rows] * inv_l
                imp = sum(imp_h[:, Q_BLOCK * j:Q_BLOCK * (j + 1)] for j in range(HPG))
                imp = jnp.where(has_key, imp, 0.0)
                for j in range(HPG):
                    h = HPG * g + j
                    per_head.append(jnp.where(
                        has_key, gt[3 * h:3 * h + 1, qlanes[bi]] * o[:, Q_BLOCK * j:Q_BLOCK * (j + 1)], 0.0))
                rk[bi, g] = jnp.where(blk_n <= cur[bi], jnp.where(forced, REMOVED, imp), NEG)
            ocmp_ref[0, :, qlanes[bi]] = jnp.concatenate(per_head, axis=0)

        for _ in range(N_SELECT - N_FORCED):
            for it in items:
                m = jnp.max(rk[it], axis=0, keepdims=True)
                idx = jnp.min(jnp.where(rk[it] == m, bf, float(N_BLK_PAD)), axis=0, keepdims=True)
                rk[it] = jnp.where(bf == idx, REMOVED, rk[it])
        ones8 = jnp.ones((8, Q_BLOCK), F32)
        for bi, g in items:
            sel = jnp.where((blk_n <= cur[bi]) & (rk[bi, g] < 2.0 * NEG), 1.0, 0.0)
            selt_ref[0, g, 0:n_rows, qlanes[bi]] = sel.astype(BF16)
            flag = (_dot_nt(ones8, sel) > 0.5).astype(jnp.int32)
            if n_rows < N_BLK_PAD:
                selt_ref[0, g, n_rows:N_BLK_PAD, qlanes[bi]] = jnp.zeros((N_BLK_PAD - n_rows, Q_BLOCK), BF16)
                flag = jnp.concatenate([flag, jnp.zeros((8, N_BLK_PAD - n_rows), jnp.int32)], axis=1)
            flags_ref[0, bi, g] = flag

    causal_rows = (start[-1] + Q_BLOCK) // SEL_BLOCK
    for n_rows in range(TOPK_ROW_STEP, N_BLK_PAD + 1, TOPK_ROW_STEP):
        @pl.when((causal_rows > n_rows - TOPK_ROW_STEP) & (causal_rows <= n_rows))
        def _():
            tail_and_select(n_rows)


def _cmp_topk(qt, kc, vct, gatest):
    bsz, _, t = qt.shape
    ncp = t // CMP_STRIDE
    nqb = t // Q_BLOCK
    step_q = CMP_BLOCKS * Q_BLOCK
    return pl.pallas_call(
        functools.partial(_cmp_topk_kernel, ncp=ncp),
        grid=(bsz, nqb // CMP_BLOCKS),
        in_specs=[pl.BlockSpec((1, NSA_W, step_q), lambda b, i: (b, 0, i)),
                  pl.BlockSpec((1, N_GROUPS, ncp, LANES), lambda b, i: (b, 0, 0, 0)),
                  pl.BlockSpec((1, N_GROUPS, VCT_ROWS, ncp), lambda b, i: (b, 0, 0, 0)),
                  pl.BlockSpec((1, GATET_ROWS, step_q), lambda b, i: (b, 0, i))],
        out_specs=[pl.BlockSpec((1, NSA_W, step_q), lambda b, i: (b, 0, i)),
                   pl.BlockSpec((1, N_GROUPS, N_BLK_PAD, step_q), lambda b, i: (b, 0, 0, i)),
                   pl.BlockSpec((1, CMP_BLOCKS, N_GROUPS, 8, N_BLK_PAD), lambda b, i: (b, i, 0, 0, 0))],
        out_shape=[jax.ShapeDtypeStruct((bsz, NSA_W, t), F32),
                   jax.ShapeDtypeStruct((bsz, N_GROUPS, N_BLK_PAD, t), BF16),
                   jax.ShapeDtypeStruct((bsz, nqb, N_GROUPS, 8, N_BLK_PAD), jnp.int32)],
        scratch_shapes=[pltpu.VMEM((CMP_BLOCKS, N_GROUPS, 1, HPG * Q_BLOCK), F32),
                        pltpu.VMEM((CMP_BLOCKS, N_GROUPS, VCT_ROWS, HPG * Q_BLOCK), F32)],
        compiler_params=pltpu.CompilerParams(dimension_semantics=("arbitrary", "arbitrary"),
                                             vmem_limit_bytes=VMEM_LIMIT),
        name="cmp_topk",
    )(qt, kc, vct, gatest)


def _slc_win_kernel(counts_ref, lists_ref, qt_ref, kaug_ref, vt_ref, oh_ref, selt_ref, gatest_ref,
                    ocmp_ref, out_ref, qaug_ref, m_ref, acc_ref, *, nqb):
    b = pl.program_id(0)
    qb = pl.program_id(1)
    step_id = b * nqb + qb
    start = qb * NSA_QB
    width = HPG * NSA_QB
    win_keys = WINDOW + NSA_QB
    win_start = jnp.maximum(start - WINDOW, 0)
    tile_heads = lambda a: jnp.concatenate([a] * HPG, axis=1)
    dist = (start - win_start) + _iota((win_keys, NSA_QB), 1) - _iota((win_keys, NSA_QB), 0)
    win_bias = tile_heads(jnp.where((dist >= 0) & (dist < WINDOW), 0.0, NEG))
    own_bias = tile_heads(jnp.where(_iota((NSA_QB, NSA_QB), 0) <= _iota((NSA_QB, NSA_QB), 1), 0.0, NEG))
    gt = gatest_ref[0]
    slope_rows = [_slope_row(g, NSA_QB) for g in range(N_GROUPS)]

    def normalize(acc):
        return acc[0:HEAD_DIM] / jnp.maximum(acc[HEAD_DIM:HEAD_DIM + 1], 1e-30)

    for g in range(N_GROUPS):
        qaug_ref[g, 0:HEAD_DIM, :] = jnp.concatenate(
            [qt_ref[0, HEAD_DIM * (HPG * g + j):HEAD_DIM * (HPG * g + j + 1), :] for j in range(HPG)],
            axis=1)
        qaug_ref[g, HEAD_DIM:LANES, :] = _slope_feature_rows(slope_rows[g], LANES - HEAD_DIM)
        sel_bias = ((selt_ref[0, g].astype(F32) - 1.0) * MASK_BIG).astype(BF16)
        qaug_ref[g, LANES:2 * LANES, :] = tile_heads(sel_bias)

    win_rows = pl.ds(pl.multiple_of(win_start, KEY_CHUNK), win_keys)
    own_rows = pl.ds(pl.multiple_of(start, KEY_CHUNK), NSA_QB)
    win_st = [_dot(kaug_ref[0, N_GROUPS + g, win_rows, :], qaug_ref[g, 0:LANES, :]) + win_bias
              for g in range(N_GROUPS)]
    own_st = [_dot(jnp.concatenate([kaug_ref[0, g, own_rows, :], oh_ref[own_rows, :]], axis=1),
                   qaug_ref[g]) + own_bias for g in range(N_GROUPS)]
    win_dls = [[slope_rows[g] * (start - win_start - KEY_CHUNK * u).astype(F32)
                for u in range(win_keys // KEY_CHUNK)] for g in range(N_GROUPS)]
    own_dls = [[slope_rows[g] * float(-KEY_CHUNK * u) for u in range(NSA_QB // KEY_CHUNK)]
               for g in range(N_GROUPS)]
    win_m = [_col_max(win_st[g], win_dls[g], KEY_CHUNK) for g in range(N_GROUPS)]
    own_m = [_col_max(own_st[g], own_dls[g], KEY_CHUNK) for g in range(N_GROUPS)]
    win_p = [_probs(win_st[g], win_dls[g], win_m[g], KEY_CHUNK) for g in range(N_GROUPS)]
    own_p = [_probs(own_st[g], own_dls[g], own_m[g], KEY_CHUNK) for g in range(N_GROUPS)]
    o_win = [normalize(_dot(vt_ref[0, N_GROUPS + g, :, win_rows], win_p[g])) for g in range(N_GROUPS)]
    for g in range(N_GROUPS):
        m_ref[g] = own_m[g]
        acc_ref[g] = _dot(vt_ref[0, g, :, own_rows], own_p[g])

    def slc_body(i, carry):
        sts, vss, dlss = [], [], []
        for g in range(N_GROUPS):
            word = lists_ref[(step_id * N_GROUPS + g) * LIST_WORDS + i]
            ks, vs, dls = [], [], []
            for u in range(SLC_BATCH):
                cid = lax.shift_right_logical(word, CHUNK_ID_BITS * u) & VOID_CHUNK
                valid = cid < VOID_CHUNK
                c = jnp.where(valid, cid, 0)
                rows = pl.ds(pl.multiple_of(c * KEY_CHUNK, KEY_CHUNK), KEY_CHUNK)
                ks.append(jnp.concatenate([kaug_ref[0, g, rows, :], oh_ref[rows, :]], axis=1))
                vs.append(vt_ref[0, g, :, rows])
                dls.append(slope_rows[g] * (start - c * KEY_CHUNK).astype(F32)
                           + jnp.where(valid, 0.0, MASK_BIG))
            sts.append(_dot(jnp.concatenate(ks, axis=0), qaug_ref[g]))
            vss.append(jnp.concatenate(vs, axis=1))
            dlss.append(dls)
        m_old = [m_ref[g] for g in range(N_GROUPS)]
        m_new = [jnp.maximum(m_old[g], _col_max(sts[g], dlss[g], KEY_CHUNK)) for g in range(N_GROUPS)]
        ps = [_probs(sts[g], dlss[g], m_new[g], KEY_CHUNK) for g in range(N_GROUPS)]
        for g in range(N_GROUPS):
            acc_ref[g] = jnp.exp2(m_old[g] - m_new[g]) * acc_ref[g] + _dot(vss[g], ps[g])
            m_ref[g] = m_new[g]
        return carry

    lax.fori_loop(0, counts_ref[step_id], slc_body, 0)

    per_head = []
    for g in range(N_GROUPS):
        o_slc = normalize(acc_ref[g])
        for j in range(HPG):
            h = HPG * g + j
            lanes = slice(NSA_QB * j, NSA_QB * (j + 1))
            per_head.append(gt[3 * h + 1:3 * h + 2, :] * o_slc[:, lanes]
                            + gt[3 * h + 2:3 * h + 3, :] * o_win[g][:, lanes])
    o_t = jnp.concatenate(per_head, axis=0)
    out_ref[0] = (ocmp_ref[0] + o_t).T.astype(BF16)


def _slc_win(counts, lists, qt, kaug, vt, onehot, selt, gatest, ocmp):
    bsz, _, t = qt.shape
    nqb = t // NSA_QB
    once = lambda shape, imap: pl.BlockSpec(shape, imap, pipeline_mode=pl.Buffered(1))
    grid_spec = pltpu.PrefetchScalarGridSpec(
        num_scalar_prefetch=2,
        grid=(bsz, nqb),
        in_specs=[pl.BlockSpec((1, NSA_W, NSA_QB), lambda b, i, *_: (b, 0, i)),
                  once((1, 2 * N_GROUPS, t, LANES), lambda b, i, *_: (b, 0, 0, 0)),
                  once((1, 2 * N_GROUPS, VT_ROWS, t), lambda b, i, *_: (b, 0, 0, 0)),
                  once((t, N_BLK_PAD), lambda b, i, *_: (0, 0)),
                  pl.BlockSpec((1, N_GROUPS, N_BLK_PAD, NSA_QB), lambda b, i, *_: (b, 0, 0, i)),
                  pl.BlockSpec((1, GATET_ROWS, NSA_QB), lambda b, i, *_: (b, 0, i)),
                  pl.BlockSpec((1, NSA_W, NSA_QB), lambda b, i, *_: (b, 0, i))],
        out_specs=pl.BlockSpec((1, NSA_QB, NSA_W), lambda b, i, *_: (b, i, 0)),
        scratch_shapes=[pltpu.VMEM((N_GROUPS, 2 * LANES, HPG * NSA_QB), BF16),
                        pltpu.VMEM((N_GROUPS, 1, HPG * NSA_QB), F32),
                        pltpu.VMEM((N_GROUPS, VT_ROWS, HPG * NSA_QB), F32)],
    )
    return pl.pallas_call(
        functools.partial(_slc_win_kernel, nqb=nqb),
        grid_spec=grid_spec,
        out_shape=jax.ShapeDtypeStruct((bsz, t, NSA_W), BF16),
        compiler_params=pltpu.CompilerParams(dimension_semantics=("arbitrary", "arbitrary"),
                                             vmem_limit_bytes=VMEM_LIMIT),
        name="slc_win",
    )(counts, lists, qt, kaug, vt, onehot, selt, gatest, ocmp)


def _merge_kernel(x_ref, onsa_ref, u_ref, vn_ref, g_ref, wm_ref, bm_ref, ws_ref, bs_ref,
                  wpa_ref, wpb_ref, wo_ref, h_ref, *, tm):
    x = x_ref[...]
    xn = _rms(x, g_ref[...]).astype(BF16)
    tril = _iota((GMLP_CHUNK, GMLP_CHUNK), 0) >= _iota((GMLP_CHUNK, GMLP_CHUNK), 1)
    sgu_rows = []
    for c in range(tm // GMLP_CHUNK):
        rows = slice(GMLP_CHUNK * c, GMLP_CHUNK * (c + 1))
        cols = []
        for g in range(GMLP_GROUPS):
            lanes = slice(LANES * g, LANES * (g + 1))
            w = jnp.where(tril, ws_ref[g], 0.0).astype(BF16)
            cols.append(_dot(w, vn_ref[rows, lanes]) + bs_ref[:, g:g + 1])
        sgu_rows.append(u_ref[rows, :] * jnp.concatenate(cols, axis=1))
    o_sgu = jnp.concatenate(sgu_rows, axis=0).astype(BF16)
    mg = jax.nn.sigmoid(_dot(xn, wm_ref[...]) + bm_ref[...])
    mixed = (mg[:, :D_MODEL] * _dot(onsa_ref[...], wpa_ref[...])
             + mg[:, D_MODEL:] * _dot(o_sgu, wpb_ref[...]))
    h_ref[...] = x + _dot(mixed.astype(BF16), wo_ref[...])


def _merge(x2, onsa2, u2, vn2, norm_g, wm, bm, ws, bs_t, wpa, wpb, wo, tm):
    n = x2.shape[0]
    row = lambda w: pl.BlockSpec((tm, w), lambda i: (i, 0))
    full = lambda a: pl.BlockSpec(a.shape, lambda i: (0,) * a.ndim)
    return pl.pallas_call(
        functools.partial(_merge_kernel, tm=tm),
        grid=(n // tm,),
        in_specs=[row(D_MODEL), row(NSA_W), row(GMLP_WIDTH), row(GMLP_WIDTH), full(norm_g),
                  full(wm), full(bm), full(ws), full(bs_t), full(wpa), full(wpb), full(wo)],
        out_specs=row(D_MODEL),
        out_shape=jax.ShapeDtypeStruct((n, D_MODEL), F32),
        compiler_params=pltpu.CompilerParams(dimension_semantics=("arbitrary",),
                                             vmem_limit_bytes=VMEM_LIMIT),
        name="merge",
    )(x2, onsa2, u2, vn2, norm_g, wm, bm, ws, bs_t, wpa, wpb, wo)


def _memkv_kernel(mem_ref, g_ref, w_ref, out_ref):
    out_ref[0] = _dot(_rms(mem_ref[0], g_ref[...]).astype(BF16), w_ref[...]).astype(BF16)


def _memkv(mem, norm_g, w):
    bsz, nm, _ = mem.shape
    return pl.pallas_call(
        _memkv_kernel,
        grid=(bsz,),
        in_specs=[pl.BlockSpec((1, nm, D_MODEL), lambda b: (b, 0, 0)),
                  pl.BlockSpec(norm_g.shape, lambda b: (0, 0)),
                  pl.BlockSpec(w.shape, lambda b: (0, 0))],
        out_specs=pl.BlockSpec((1, nm, 2 * MEM_W), lambda b: (b, 0, 0)),
        out_shape=jax.ShapeDtypeStruct((bsz, nm, 2 * MEM_W), BF16),
        compiler_params=pltpu.CompilerParams(dimension_semantics=("arbitrary",),
                                             vmem_limit_bytes=VMEM_LIMIT),
        name="memkv",
    )(mem, norm_g, w)


def _xattn_kernel(h_ref, g_ref, wq_ref, mkv_ref, wo_ref, out_ref, *, tm):
    halves = [slice(0, tm // 2), slice(tm // 2, tm)]
    head_lanes = [slice(MEM_HEAD_DIM * a, MEM_HEAD_DIM * (a + 1)) for a in range(MEM_HEADS)]
    hs = [h_ref[rows, :] for rows in halves]
    hqs = [(_dot(_rms(h, g_ref[...]).astype(BF16), wq_ref[...]) * (MEM_HEAD_DIM ** -0.5 * LOG2E)).astype(BF16)
           for h in hs]
    ss = [[_dot_nt(hq[:, lanes], mkv_ref[0, :, lanes]) for lanes in head_lanes] for hq in hqs]
    es = [[jnp.exp2(s - jnp.max(s, axis=-1, keepdims=True)) for s in s_half] for s_half in ss]
    os = []
    for e_half in es:
        heads = []
        for a, e in enumerate(e_half):
            v = mkv_ref[0, :, MEM_W + MEM_HEAD_DIM * a:MEM_W + MEM_HEAD_DIM * (a + 1)]
            heads.append(_dot(e.astype(BF16), v) * (1.0 / jnp.sum(e, axis=-1, keepdims=True)))
        os.append(jnp.concatenate(heads, axis=1).astype(BF16))
    for rows, h, o in zip(halves, hs, os):
        out_ref[rows, :] = h + _dot(o, wo_ref[...])


def _xattn(h2d, norm_g, wq, mkv, wo, tm, rows_per_batch):
    n = h2d.shape[0]
    nm = mkv.shape[1]
    tiles_per_batch = rows_per_batch // tm
    full = lambda a: pl.BlockSpec(a.shape, lambda i: (0,) * a.ndim)
    return pl.pallas_call(
        functools.partial(_xattn_kernel, tm=tm),
        grid=(n // tm,),
        in_specs=[pl.BlockSpec((tm, D_MODEL), lambda i: (i, 0)), full(norm_g), full(wq),
                  pl.BlockSpec((1, nm, 2 * MEM_W), lambda i: (i // tiles_per_batch, 0, 0)),
                  full(wo)],
        out_specs=pl.BlockSpec((tm, D_MODEL), lambda i: (i, 0)),
        out_shape=jax.ShapeDtypeStruct((n, D_MODEL), F32),
        compiler_params=pltpu.CompilerParams(dimension_semantics=("arbitrary",),
                                             vmem_limit_bytes=VMEM_LIMIT),
        name="xattn",
    )(h2d, norm_g, wq, mkv, wo)


def _ffn_kernel(h_ref, g_ref, wgu_ref, wd_ref, gf_ref, out_ref, *, d_ff):
    h = h_ref[...]
    hn = _rms(h, g_ref[...]).astype(BF16)
    gate = _dot(hn, wgu_ref[:, :d_ff])
    up = _dot(hn, wgu_ref[:, d_ff:])
    act = (jax.nn.silu(gate) * up).astype(BF16)
    y = h + _dot(act, wd_ref[...])
    out_ref[...] = _rms(y, gf_ref[...])


def _ffn(h2d, norm_g, wgu, wd, norm_f, tm):
    n = h2d.shape[0]
    d_ff = wd.shape[0]
    full = lambda a: pl.BlockSpec(a.shape, lambda i: (0,) * a.ndim)
    once = lambda a: pl.BlockSpec(a.shape, lambda i: (0,) * a.ndim, pipeline_mode=pl.Buffered(1))
    return pl.pallas_call(
        functools.partial(_ffn_kernel, d_ff=d_ff),
        grid=(n // tm,),
        in_specs=[pl.BlockSpec((tm, D_MODEL), lambda i: (i, 0)), full(norm_g), once(wgu), once(wd),
                  full(norm_f)],
        out_specs=pl.BlockSpec((tm, D_MODEL), lambda i: (i, 0)),
        out_shape=jax.ShapeDtypeStruct((n, D_MODEL), F32),
        compiler_params=pltpu.CompilerParams(dimension_semantics=("arbitrary",),
                                             vmem_limit_bytes=VMEM_LIMIT),
        name="ffn",
    )(h2d, norm_g, wgu, wd, norm_f)


def _block_diag2(w):
    z = jnp.zeros_like(w)
    return jnp.concatenate([jnp.concatenate([w, z], axis=-1), jnp.concatenate([z, w], axis=-1)], axis=-2)


def _chunk_lists(flags):
    bsz = flags.shape[0]
    n_chunks = N_KEY_CHUNKS
    per_step = NSA_QB // Q_BLOCK
    nqb = flags.shape[1] // per_step
    f = flags[:, :, :, 0, :].reshape(bsz, nqb, per_step, N_GROUPS, n_chunks, 2).max(axis=(2, 5))
    cid = jnp.arange(n_chunks, dtype=jnp.int32)
    own = (NSA_QB // KEY_CHUNK) * jnp.arange(nqb, dtype=jnp.int32)[None, :, None, None]
    active = (f > 0) & (cid < own)
    n_active = active.sum(axis=-1)
    slot = jnp.cumsum(active, axis=-1) - 1
    hit = active[..., :, None] & (slot[..., :, None] == cid)
    ids = jnp.sum(jnp.where(hit, cid[:, None], 0), axis=-2)
    ids = jnp.where(cid < n_active[..., None], ids, VOID_CHUNK)
    ids = jnp.pad(ids, ((0, 0), (0, 0), (0, 0), (0, LIST_WORDS * SLC_BATCH - n_chunks)),
                  constant_values=VOID_CHUNK)
    n_batches = (n_active.max(axis=-1) + SLC_BATCH - 1) // SLC_BATCH
    packed = ids.reshape(bsz, nqb, N_GROUPS, LIST_WORDS, SLC_BATCH)
    words = functools.reduce(jnp.bitwise_or,
                             [packed[..., u] << (CHUNK_ID_BITS * u) for u in range(SLC_BATCH)])
    return n_batches.reshape(-1).astype(jnp.int32), words.reshape(-1).astype(jnp.int32)


def kernel(x, mem, norm_mix, w_in, w_cmp_k1, w_cmp_k2, w_cmp_v1, w_cmp_v2, pe_cmp_k, pe_cmp_v, ln_sgu, w_spatial, b_spatial, w_proj_a, w_proj_b, w_merge, b_merge, w_out, norm_mem_q, norm_mem_kv, w_mq, w_mkv, w_mo, norm_ffn, w_gate_up, w_down, norm_final):
    bsz, t, d = x.shape
    depth = norm_mix.shape[0]
    assert d == D_MODEL and t % Q_BLOCK == 0 and t // SEL_BLOCK <= N_BLK_PAD
    assert t // SEL_BLOCK >= N_SELECT and depth == 1 and t % 512 == 0
    n = bsz * t
    tm = 512
    h = x.reshape(n, d)
    c0, c1, c2, c3 = NSA_W, NSA_W + KV_W, NSA_W + 3 * KV_W, NSA_W + 3 * KV_W + 2 * GMLP_WIDTH
    onehot = (jnp.arange(t)[:, None] // SEL_BLOCK == jnp.arange(N_BLK_PAD)[None, :]).astype(BF16)
    pad_cols = lambda w, width: jnp.pad(w, ((0, 0), (0, width - w.shape[1])))
    for l in range(depth):
        wi = w_in[l]
        wk, wv = [], []
        for branch in range(2):
            base = c1 + KV_W * branch
            for g in range(N_GROUPS):
                wk.append(pad_cols(wi[:, base + HEAD_DIM * g:base + HEAD_DIM * (g + 1)], LANES))
                v0 = base + N_GROUPS * HEAD_DIM + HEAD_DIM * g
                wv.append(pad_cols(wi[:, v0:v0 + HEAD_DIM], VT_ROWS))
        wrow = jnp.concatenate([wi[:, c0:c1]] + wk + [wi[:, c2:c3]], axis=1).astype(BF16)
        wnt = jnp.concatenate([wi[:, :c0]] + wv + [pad_cols(wi[:, c3:], GATET_ROWS)], axis=1).T.astype(BF16)
        qt, kvc2, kaug, vt, u2, vn2, gatest = _inproj(
            h, norm_mix[l][None], ln_sgu[l][None], wrow, wnt, tm, bsz, t)

        pe2 = jnp.stack([pe_cmp_k[l], pe_cmp_v[l]])
        pe2 = jnp.concatenate([pe2, pe2], axis=-1)
        w1 = jnp.stack([w_cmp_k1[l], w_cmp_v1[l]]).reshape(2, CMP_BLOCK, HEAD_DIM, CMP_HIDDEN)
        w1bd = _block_diag2(w1).astype(BF16)
        w2k = _block_diag2(w_cmp_k2[l]).astype(BF16)
        w2vt = _block_diag2(w_cmp_v2[l]).T.astype(BF16)
        kc, vct = _compress(kvc2.reshape(bsz, t, KV_W), pe2, w1bd, w2k, w2vt)
        ocmp, selt, flags = _cmp_topk(qt, kc, vct, gatest)
        counts, lists = _chunk_lists(flags)
        onsa = _slc_win(counts, lists, qt, kaug, vt, onehot, selt, gatest, ocmp)

        h = _merge(h, onsa.reshape(n, NSA_W), u2, vn2, norm_mix[l][None], w_merge[l].astype(BF16),
                   b_merge[l][None], w_spatial[l], b_spatial[l].T, w_proj_a[l].astype(BF16),
                   w_proj_b[l].astype(BF16), w_out[l].astype(BF16), tm)

        mkv = _memkv(mem, norm_mem_kv[l][None], w_mkv[l].astype(BF16))
        h = _xattn(h, norm_mem_q[l][None], w_mq[l].astype(BF16), mkv, w_mo[l].astype(BF16), tm, t)
        h = _ffn(h, norm_ffn[l][None], w_gate_up[l].astype(BF16), w_down[l].astype(BF16),
                 norm_final[None], tm)
    return h.reshape(bsz, t, d)
```

```python
import functools

import jax
import jax.numpy as jnp
from jax import lax
from jax.experimental import pallas as pl
from jax.experimental.pallas import tpu as pltpu

F32 = jnp.float32
BF16 = jnp.bfloat16

LANES = 128
SUBLANES = 8
D_MODEL = 1024
N_HEADS = 8
HEAD_DIM = 64
N_GROUPS = 2
HPG = N_HEADS // N_GROUPS
CMP_BLOCK = 32
CMP_STRIDE = 16
CMP_HIDDEN = 128
SEL_BLOCK = 64
N_SELECT = 16
WINDOW = 512
Q_BLOCK = 256
KEY_CHUNK = 128
N_BLK_PAD = 128
NSA_W = N_HEADS * HEAD_DIM
KV_W = 2 * N_GROUPS * HEAD_DIM
GMLP_WIDTH = 512
GMLP_GROUPS = 4
GMLP_CHUNK = 128
MEM_HEADS = 4
MEM_HEAD_DIM = 128
MEM_W = MEM_HEADS * MEM_HEAD_DIM
GATE_W = 3 * N_HEADS
NSA_QB = 256
SLC_BATCH = 5
N_KEY_CHUNKS = N_BLK_PAD // 2
CHUNK_ID_BITS = 6
VOID_CHUNK = N_KEY_CHUNKS - 1
LIST_WORDS = -(-N_KEY_CHUNKS // SLC_BATCH)
VT_ROWS = 80
VCT_ROWS = VT_ROWS + N_BLK_PAD
CMP_KEY_CHUNK = 128
N_FORCED = 3
TOPK_ROW_STEP = 64
CMP_BLOCKS = 2
GATET_ROWS = 32
MASK_BIG = 1e30
EPS = 1e-6
NEG = -1e30
REMOVED = -3e38
FORCE_SCORE = 1e6
SLOPES = tuple(2.0 ** (-8.0 * (h + 1) / N_HEADS) for h in range(N_HEADS))
LOG2E = 1.4426950408889634
Q_SCALE = HEAD_DIM ** -0.5 * LOG2E
VMEM_LIMIT = 56 * 1024 * 1024


def _dot(a, b):
    return jnp.dot(a, b, preferred_element_type=F32)


def _dot_nt(a, b):
    return lax.dot_general(a, b, (((1,), (1,)), ((), ())), preferred_element_type=F32)


def _rms(x, g):
    return x * lax.rsqrt(jnp.mean(x * x, axis=-1, keepdims=True) + EPS) * g


def _iota(shape, dim):
    return lax.broadcasted_iota(jnp.int32, shape, dim)


def _slope_row(g, nq):
    return jnp.concatenate(
        [jnp.full((1, nq), SLOPES[HPG * g + j] * LOG2E, F32) for j in range(HPG)], axis=1)


def _slope_feature_rows(slope_row, n_rows):
    hi = slope_row.astype(BF16).astype(F32)
    r = _iota((n_rows, slope_row.shape[1]), 0)
    return jnp.where(r == 0, hi, jnp.where(r == 1, slope_row - hi, 0.0)).astype(BF16)


def _chunk_slabs(st, chunk):
    return [st[chunk * u:chunk * (u + 1)] for u in range(st.shape[0] // chunk)]


def _col_max(st, dls, chunk):
    parts = [s.reshape(chunk // SUBLANES, SUBLANES, s.shape[1]).max(axis=0) - dl
             for s, dl in zip(_chunk_slabs(st, chunk), dls)]
    return functools.reduce(jnp.maximum, parts).max(axis=0, keepdims=True)


def _probs(st, dls, m, chunk):
    return jnp.concatenate([jnp.exp2(s - (m + dl)).astype(BF16)
                            for s, dl in zip(_chunk_slabs(st, chunk), dls)], axis=0)


def _inproj_kernel(x_ref, g_ref, lng_ref, wrow_ref, wnt_ref,
                   qt_ref, kvc_ref, kaug_ref, vt_ref, u_ref, vn_ref, gatest_ref, *, tm):
    xn = _rms(x_ref[...], g_ref[...]).astype(BF16)
    r = _dot(xn, wrow_ref[...])
    nt = _dot_nt(wnt_ref[...], xn)
    qt_ref[0] = (nt[0:NSA_W] * Q_SCALE).astype(BF16)
    kvc_ref[...] = r[:, 0:KV_W]
    lane = _iota((tm, LANES), 1)
    key_feat = jnp.where((lane == HEAD_DIM) | (lane == HEAD_DIM + 1),
                         _iota((tm, LANES), 0) & (KEY_CHUNK - 1), 0).astype(F32)
    ones_row = jnp.where(_iota((VT_ROWS, tm), 0) == HEAD_DIM, 1.0, 0.0)
    for a in range(2 * N_GROUPS):
        kaug_ref[0, a] = (r[:, KV_W + LANES * a:KV_W + LANES * (a + 1)] + key_feat).astype(BF16)
        vt_ref[0, a] = (nt[NSA_W + VT_ROWS * a:NSA_W + VT_ROWS * (a + 1)] + ones_row).astype(BF16)
    uv = jax.nn.gelu(r[:, KV_W + 2 * N_GROUPS * LANES:])
    u_ref[...] = uv[:, :GMLP_WIDTH]
    v = uv[:, GMLP_WIDTH:]
    vc = v - jnp.mean(v, axis=-1, keepdims=True)
    vn = vc * lax.rsqrt(jnp.mean(vc * vc, axis=-1, keepdims=True) + EPS) * lng_ref[...]
    vn_ref[...] = vn.astype(BF16)
    gatest_ref[0] = jax.nn.sigmoid(nt[NSA_W + 2 * N_GROUPS * VT_ROWS:])


def _inproj(x2, norm_g, ln_g, wrow, wnt, tm, bsz, t):
    n = x2.shape[0]
    tpb = t // tm
    row = lambda w: pl.BlockSpec((tm, w), lambda i: (i, 0))
    full = lambda a: pl.BlockSpec(a.shape, lambda i: (0,) * a.ndim)
    return pl.pallas_call(
        functools.partial(_inproj_kernel, tm=tm),
        grid=(n // tm,),
        in_specs=[row(D_MODEL), full(norm_g), full(ln_g), full(wrow), full(wnt)],
        out_specs=[pl.BlockSpec((1, NSA_W, tm), lambda i: (i // tpb, 0, i % tpb)),
                   row(KV_W),
                   pl.BlockSpec((1, 2 * N_GROUPS, tm, LANES), lambda i: (i // tpb, 0, i % tpb, 0)),
                   pl.BlockSpec((1, 2 * N_GROUPS, VT_ROWS, tm), lambda i: (i // tpb, 0, 0, i % tpb)),
                   row(GMLP_WIDTH), row(GMLP_WIDTH),
                   pl.BlockSpec((1, GATET_ROWS, tm), lambda i: (i // tpb, 0, i % tpb))],
        out_shape=[jax.ShapeDtypeStruct((bsz, NSA_W, t), BF16),
                   jax.ShapeDtypeStruct((n, KV_W), F32),
                   jax.ShapeDtypeStruct((bsz, 2 * N_GROUPS, t, LANES), BF16),
                   jax.ShapeDtypeStruct((bsz, 2 * N_GROUPS, VT_ROWS, t), BF16),
                   jax.ShapeDtypeStruct((n, GMLP_WIDTH), F32),
                   jax.ShapeDtypeStruct((n, GMLP_WIDTH), BF16),
                   jax.ShapeDtypeStruct((bsz, GATET_ROWS, t), F32)],
        compiler_params=pltpu.CompilerParams(dimension_semantics=("arbitrary",),
                                             vmem_limit_bytes=VMEM_LIMIT),
        name="inproj",
    )(x2, norm_g, ln_g, wrow, wnt)


def _compress_kernel(xk_ref, xv_ref, pe_ref, w1_ref, w2k_ref, w2vt_ref, kc_ref, vct_ref, *, nc):
    outs = []
    for j, x_ref in enumerate((xk_ref, xv_ref)):
        a = jnp.zeros((nc, 2 * CMP_HIDDEN), F32)
        b = jnp.zeros((nc, 2 * CMP_HIDDEN), F32)
        for t in range(CMP_STRIDE):
            xt = x_ref[0, pl.ds(t, nc, stride=CMP_STRIDE), :]
            a = a + _dot((xt + pe_ref[j, t:t + 1, :]).astype(BF16), w1_ref[j, t])
            b = b + _dot((xt + pe_ref[j, CMP_STRIDE + t:CMP_STRIDE + t + 1, :]).astype(BF16),
                         w1_ref[j, CMP_STRIDE + t])
        outs.append(jax.nn.gelu(a + pltpu.roll(b, nc - 1, 0)).astype(BF16))
    k2 = _dot(outs[0], w2k_ref[...])
    v_t = _dot_nt(w2vt_ref[...], outs[1])
    lane = _iota((nc, LANES), 1)
    key_feat = jnp.where((lane == HEAD_DIM) | (lane == HEAD_DIM + 1),
                         CMP_STRIDE * (_iota((nc, LANES), 0) & (CMP_KEY_CHUNK - 1)), 0).astype(F32)
    ci =_iota((N_BLK_PAD, nc), 1) * CMP_STRIDE
    sj = _iota((N_BLK_PAD, nc), 0) * SEL_BLOCK
    overlap_t = jnp.where((ci < sj + SEL_BLOCK) & (ci + (CMP_BLOCK - 1) >= sj), 1.0, 0.0).astype(BF16)
    ones_rows = jnp.where(_iota((VT_ROWS - HEAD_DIM, nc), 0) == 0, 1.0, 0.0).astype(BF16)
    for g in range(N_GROUPS):
        kg = k2 if g == 0 else pltpu.roll(k2, HEAD_DIM, 1)
        kc_ref[0, g] = jnp.where(lane < HEAD_DIM, kg, key_feat).astype(BF16)
        vct_ref[0, g, 0:HEAD_DIM, :] = v_t[HEAD_DIM * g:HEAD_DIM * (g + 1), :].astype(BF16)
        vct_ref[0, g, HEAD_DIM:VT_ROWS, :] = ones_rows
        vct_ref[0, g, VT_ROWS:VCT_ROWS, :] = overlap_t


def _compress(kvc3, pe2, w1bd, w2k, w2vt):
    bsz, t, _ = kvc3.shape
    nc = t // CMP_STRIDE
    full = lambda a: pl.BlockSpec(a.shape, lambda b: (0,) * a.ndim)
    return pl.pallas_call(
        functools.partial(_compress_kernel, nc=nc),
        grid=(bsz,),
        in_specs=[pl.BlockSpec((1, t, LANES), lambda b: (b, 0, 0)),
                  pl.BlockSpec((1, t, LANES), lambda b: (b, 0, 1)),
                  full(pe2), full(w1bd), full(w2k), full(w2vt)],
        out_specs=[pl.BlockSpec((1, N_GROUPS, nc, LANES), lambda b: (b, 0, 0, 0)),
                   pl.BlockSpec((1, N_GROUPS, VCT_ROWS, nc), lambda b: (b, 0, 0, 0))],
        out_shape=[jax.ShapeDtypeStruct((bsz, N_GROUPS, nc, LANES), BF16),
                   jax.ShapeDtypeStruct((bsz, N_GROUPS, VCT_ROWS, nc), BF16)],
        compiler_params=pltpu.CompilerParams(dimension_semantics=("arbitrary",),
                                             vmem_limit_bytes=VMEM_LIMIT),
        name="compress",
    )(kvc3, kvc3, pe2, w1bd, w2k, w2vt)


def _cmp_topk_kernel(qt_ref, kc_ref, vct_ref, gatest_ref, ocmp_ref, selt_ref, flags_ref,
                     m_ref, acc_ref, *, ncp):
    step = pl.program_id(1)
    n_chunks = ncp // CMP_KEY_CHUNK
    chunk_tokens = CMP_KEY_CHUNK * CMP_STRIDE
    tile_heads = lambda a: jnp.concatenate([a] * HPG, axis=1)
    tail_chunks = min(2, n_chunks)
    head_chunks = n_chunks - tail_chunks
    tail_keys = tail_chunks * CMP_KEY_CHUNK
    blocks = range(CMP_BLOCKS)
    groups = range(N_GROUPS)
    items = [(bi, g) for bi in blocks for g in groups]
    start = [(step * CMP_BLOCKS + bi) * Q_BLOCK for bi in blocks]
    qlanes = [slice(Q_BLOCK * bi, Q_BLOCK * (bi + 1)) for bi in blocks]
    nck = [(start[bi] + Q_BLOCK - CMP_BLOCK) // CMP_STRIDE // CMP_KEY_CHUNK + 1 for bi in blocks]
    tail_c0 = [jnp.maximum(nck[bi] - tail_chunks, 0) for bi in blocks]
    tail_rows = [pl.ds(pl.multiple_of(tail_c0[bi] * CMP_KEY_CHUNK, CMP_KEY_CHUNK), tail_keys)
                 for bi in blocks]
    gt = gatest_ref[0]
    slope_rows = [_slope_row(g, Q_BLOCK) for g in groups]
    qas = {(bi, g): jnp.concatenate(
        [jnp.concatenate([qt_ref[0, HEAD_DIM * (HPG * g + j):HEAD_DIM * (HPG * g + j + 1), qlanes[bi]]
                          for j in range(HPG)], axis=1),
         _slope_feature_rows(slope_rows[g], LANES - HEAD_DIM)], axis=0) for bi, g in items}

    def delta(bi, g, c):
        return slope_rows[g] * (start[bi] - c * chunk_tokens).astype(F32)

    for bi in blocks:
        if head_chunks > 0:
            @pl.when(nck[bi] > tail_chunks)
            def _():
                sts = [_dot(kc_ref[0, g, 0:head_chunks * CMP_KEY_CHUNK, :], qas[bi, g]) for g in groups]
                dls = [[delta(bi, g, c) + jnp.where(c < nck[bi] - tail_chunks, 0.0, MASK_BIG)
                        for c in range(head_chunks)] for g in groups]
                ms = [_col_max(sts[g], dls[g], CMP_KEY_CHUNK) for g in groups]
                ps = [_probs(sts[g], dls[g], ms[g], CMP_KEY_CHUNK) for g in groups]
                for g in groups:
                    m_ref[bi, g] = ms[g]
                    acc_ref[bi, g] = _dot(vct_ref[0, g, :, 0:head_chunks * CMP_KEY_CHUNK], ps[g])

            @pl.when(nck[bi] <= tail_chunks)
            def _():
                m_ref[bi] = jnp.full(m_ref.shape[1:], NEG, F32)
                acc_ref[bi] = jnp.zeros(acc_ref.shape[1:], F32)

    def tail_and_select(n_rows):
        if head_chunks > 0:
            m_old = {it: m_ref[it[0], it[1]] for it in items}
            acc_old = {it: acc_ref[it[0], it[1]] for it in items}
        else:
            m_old = {it: jnp.full((1, HPG * Q_BLOCK), NEG, F32) for it in items}
            acc_old = {it: jnp.zeros((VCT_ROWS, HPG * Q_BLOCK), F32) for it in items}
        key_row = _iota((tail_keys, Q_BLOCK), 0)
        q_lane = _iota((tail_keys, Q_BLOCK), 1)
        tail_bias = []
        for bi in blocks:
            key_end = CMP_STRIDE * (tail_c0[bi] * CMP_KEY_CHUNK + key_row) + (CMP_BLOCK - 1) - start[bi]
            tail_bias.append(tile_heads(jnp.where(key_end <= q_lane, 0.0, NEG)))
        sts = {(bi, g): _dot(kc_ref[0, g, tail_rows[bi], :], qas[bi, g]) + tail_bias[bi] for bi, g in items}
        dls = {(bi, g): [delta(bi, g, tail_c0[bi] + u) for u in range(tail_chunks)] for bi, g in items}
        m_new = {it: jnp.maximum(m_old[it], _col_max(sts[it], dls[it], CMP_KEY_CHUNK)) for it in items}
        ps = {it: _probs(sts[it], dls[it], m_new[it], CMP_KEY_CHUNK) for it in items}
        accs = {(bi, g): jnp.exp2(m_old[bi, g] - m_new[bi, g]) * acc_old[bi, g]
                + _dot(vct_ref[0, g, :, tail_rows[bi]], ps[bi, g]) for bi, g in items}

        blk_n = _iota((n_rows, Q_BLOCK), 0)
        bf = blk_n.astype(F32)
        rk, cur = {}, []
        for bi in blocks:
            t_row = start[bi] + _iota((1, Q_BLOCK), 1)
            cur.append(lax.shift_right_logical(t_row, SEL_BLOCK.bit_length() - 1))
            forced = (blk_n == 0) | (blk_n == cur[bi]) | (blk_n == cur[bi] - 1)
            has_key = t_row >= CMP_BLOCK - 1
            per_head = []
            for g in groups:
                acc = accs[bi, g]
                inv_l = 1.0 / jnp.maximum(acc[HEAD_DIM:HEAD_DIM + 1], 1e-30)
                o = acc[0:HEAD_DIM] * inv_l
                imp_h = acc[VT_ROWS:VT_ROWS + n_rows] * inv_l
                imp = sum(imp_h[:, Q_BLOCK * j:Q_BLOCK * (j + 1)] for j in range(HPG))
                imp = jnp.where(has_key, imp, 0.0)
                for j in range(HPG):
                    h = HPG * g + j
                    per_head.append(jnp.where(
                        has_key, gt[3 * h:3 * h + 1, qlanes[bi]] * o[:, Q_BLOCK * j:Q_BLOCK * (j + 1)], 0.0))
                rk[bi, g] = jnp.where(blk_n <= cur[bi], jnp.where(forced, REMOVED, imp), NEG)
            ocmp_ref[0, :, qlanes[bi]] = jnp.concatenate(per_head, axis=0)

        for _ in range(N_SELECT - N_FORCED):
            for it in items:
                m = jnp.max(rk[it], axis=0, keepdims=True)
                idx = jnp.min(jnp.where(rk[it] == m, bf, float(N_BLK_PAD)), axis=0, keepdims=True)
                rk[it] = jnp.where(bf == idx, REMOVED, rk[it])
        ones8 = jnp.ones((SUBLANES, Q_BLOCK), F32)
        for bi, g in items:
            sel = jnp.where((blk_n <= cur[bi]) & (rk[bi, g] < 2.0 * NEG), 1.0, 0.0)
            selt_ref[0, g, 0:n_rows, qlanes[bi]] = sel.astype(BF16)
            flag = (_dot_nt(ones8, sel) > 0.5).astype(jnp.int32)
            if n_rows < N_BLK_PAD:
                selt_ref[0, g, n_rows:N_BLK_PAD, qlanes[bi]] = jnp.zeros((N_BLK_PAD - n_rows, Q_BLOCK), BF16)
                flag = jnp.concatenate([flag, jnp.zeros((SUBLANES, N_BLK_PAD - n_rows), jnp.int32)], axis=1)
            flags_ref[0, bi, g] = flag

    causal_rows = (start[-1] + Q_BLOCK) // SEL_BLOCK
    for n_rows in range(TOPK_ROW_STEP, N_BLK_PAD + 1, TOPK_ROW_STEP):
        @pl.when((causal_rows > n_rows - TOPK_ROW_STEP) & (causal_rows <= n_rows))
        def _():
            tail_and_select(n_rows)


def _cmp_topk(qt, kc, vct, gatest):
    bsz, _, t = qt.shape
    ncp = t // CMP_STRIDE
    nqb = t // Q_BLOCK
    step_q = CMP_BLOCKS * Q_BLOCK
    return pl.pallas_call(
        functools.partial(_cmp_topk_kernel, ncp=ncp),
        grid=(bsz, nqb // CMP_BLOCKS),
        in_specs=[pl.BlockSpec((1, NSA_W, step_q), lambda b, i: (b, 0, i)),
                  pl.BlockSpec((1, N_GROUPS, ncp, LANES), lambda b, i: (b, 0, 0, 0)),
                  pl.BlockSpec((1, N_GROUPS, VCT_ROWS, ncp), lambda b, i: (b, 0, 0, 0)),
                  pl.BlockSpec((1, GATET_ROWS, step_q), lambda b, i: (b, 0, i))],
        out_specs=[pl.BlockSpec((1, NSA_W, step_q), lambda b, i: (b, 0, i)),
                   pl.BlockSpec((1, N_GROUPS, N_BLK_PAD, step_q), lambda b, i: (b, 0, 0, i)),
                   pl.BlockSpec((1, CMP_BLOCKS, N_GROUPS, SUBLANES, N_BLK_PAD), lambda b, i: (b, i, 0, 0, 0))],
        out_shape=[jax.ShapeDtypeStruct((bsz, NSA_W, t), F32),
                   jax.ShapeDtypeStruct((bsz, N_GROUPS, N_BLK_PAD, t), BF16),
                   jax.ShapeDtypeStruct((bsz, nqb, N_GROUPS, SUBLANES, N_BLK_PAD), jnp.int32)],
        scratch_shapes=[pltpu.VMEM((CMP_BLOCKS, N_GROUPS, 1, HPG * Q_BLOCK), F32),
                        pltpu.VMEM((CMP_BLOCKS, N_GROUPS, VCT_ROWS, HPG * Q_BLOCK), F32)],
        compiler_params=pltpu.CompilerParams(dimension_semantics=("arbitrary", "arbitrary"),
                                             vmem_limit_bytes=VMEM_LIMIT),
        name="cmp_topk",
    )(qt, kc, vct, gatest)


def _slc_win_kernel(counts_ref, lists_ref, qt_ref, kaug_ref, vt_ref, oh_ref, selt_ref, gatest_ref,
                    ocmp_ref, out_ref, qaug_ref, m_ref, acc_ref, *, nqb):
    b = pl.program_id(0)
    qb = pl.program_id(1)
    step_id = b * nqb + qb
    start = qb * NSA_QB
    tile_heads = lambda a: jnp.concatenate([a] * HPG, axis=1)
    groups = range(N_GROUPS)
    gt = gatest_ref[0]
    slope_rows = [_slope_row(g, NSA_QB) for g in groups]

    def normalize(acc):
        return acc[0:HEAD_DIM] / jnp.maximum(acc[HEAD_DIM:HEAD_DIM + 1], 1e-30)

    def half_lanes(w):
        return [slice(NSA_QB * h + KEY_CHUNK * w, NSA_QB * h + KEY_CHUNK * (w + 1)) for h in range(HPG)]

    def half(a, w):
        return jnp.concatenate([a[:, s] for s in half_lanes(w)], axis=1)

    def unhalf(lo, hi):
        return jnp.concatenate([x[:, KEY_CHUNK * h:KEY_CHUNK * (h + 1)]
                                for h in range(HPG) for x in (lo, hi)], axis=1)

    def chunk_at(rel):
        pos = start + rel * KEY_CHUNK
        rows = pl.ds(pl.multiple_of(jnp.maximum(pos, 0), KEY_CHUNK), KEY_CHUNK)
        return rows, (jnp.where(pos >= 0, 0.0, MASK_BIG) if rel < 0 else 0.0)

    ki = _iota((KEY_CHUNK, KEY_CHUNK), 0)
    qi = _iota((KEY_CHUNK, KEY_CHUNK), 1)
    upper_bias = tile_heads(jnp.where(ki > qi, 0.0, NEG))
    lower_bias = tile_heads(jnp.where(ki <= qi, 0.0, NEG))
    n_mid = WINDOW // KEY_CHUNK
    mid_dist = (_iota((n_mid * KEY_CHUNK, NSA_QB), 1) + (n_mid - 1) * KEY_CHUNK
                - _iota((n_mid * KEY_CHUNK, NSA_QB), 0))
    mid_bias = tile_heads(jnp.where((mid_dist >= 0) & (mid_dist < WINDOW), 0.0, NEG))
    own_bias = tile_heads(jnp.where(_iota((KEY_CHUNK, NSA_QB), 0) <= _iota((KEY_CHUNK, NSA_QB), 1), 0.0, NEG))

    for g in range(N_GROUPS):
        qaug_ref[g, 0:HEAD_DIM, :] = jnp.concatenate(
            [qt_ref[0, HEAD_DIM * (HPG * g + j):HEAD_DIM * (HPG * g + j + 1), :] for j in range(HPG)],
            axis=1)
        qaug_ref[g, HEAD_DIM:LANES, :] = _slope_feature_rows(slope_rows[g], LANES - HEAD_DIM)
        sel_bias = ((selt_ref[0, g].astype(F32) - 1.0) * MASK_BIG).astype(BF16)
        qaug_ref[g, LANES:2 * LANES, :] = tile_heads(sel_bias)

    mid = [chunk_at(r) for r in range(1 - n_mid, 1)]
    lo_rows, lo_kill = chunk_at(-n_mid)
    hi_rows, _ = chunk_at(1)
    own_rows, _ = chunk_at(0)
    kw = lambda g: kaug_ref.at[0, N_GROUPS + g]
    vw = lambda g: vt_ref.at[0, N_GROUPS + g]
    slc_keys = lambda g, rows: jnp.concatenate([kaug_ref[0, g, rows, :], oh_ref[rows, :]], axis=1)
    q_half = lambda g, w, nrow: jnp.concatenate([qaug_ref[g, 0:nrow, s] for s in half_lanes(w)], axis=1)

    wmid_st = [_dot(jnp.concatenate([kw(g)[rows, :] for rows, _ in mid], axis=0), qaug_ref[g, 0:LANES, :])
               + mid_bias for g in groups]
    wlo_st = [_dot(kw(g)[lo_rows, :], q_half(g, 0, LANES)) + upper_bias for g in groups]
    whi_st = [_dot(kw(g)[hi_rows, :], q_half(g, 1, LANES)) + lower_bias for g in groups]
    own_st = [_dot(slc_keys(g, own_rows), qaug_ref[g]) + own_bias for g in groups]
    ohi_st = [_dot(slc_keys(g, hi_rows), q_half(g, 1, 2 * LANES)) + lower_bias for g in groups]

    wmid_dls = [[slope_rows[g] * float(-KEY_CHUNK * r) + kill
                 for r, (_, kill) in zip(range(1 - n_mid, 1), mid)] for g in groups]
    wlo_dl = [half(slope_rows[g], 0) * float(KEY_CHUNK * n_mid) + lo_kill for g in groups]
    hi_dl = [half(slope_rows[g], 1) * float(-KEY_CHUNK) for g in groups]
    neg_half = jnp.full((1, HPG * KEY_CHUNK), NEG, F32)
    win_m = [jnp.maximum(_col_max(wmid_st[g], wmid_dls[g], KEY_CHUNK),
                         unhalf(_col_max(wlo_st[g], [wlo_dl[g]], KEY_CHUNK),
                                _col_max(whi_st[g], [hi_dl[g]], KEY_CHUNK))) for g in groups]
    own_m = [jnp.maximum(_col_max(own_st[g], [0.0], KEY_CHUNK),
                         unhalf(neg_half, _col_max(ohi_st[g], [hi_dl[g]], KEY_CHUNK))) for g in groups]
    wmid_p = [_probs(wmid_st[g], wmid_dls[g], win_m[g], KEY_CHUNK) for g in groups]
    wlo_p = [_probs(wlo_st[g], [wlo_dl[g]], half(win_m[g], 0), KEY_CHUNK) for g in groups]
    whi_p = [_probs(whi_st[g], [hi_dl[g]], half(win_m[g], 1), KEY_CHUNK) for g in groups]
    own_p = [_probs(own_st[g], [0.0], own_m[g], KEY_CHUNK) for g in groups]
    ohi_p = [_probs(ohi_st[g], [hi_dl[g]], half(own_m[g], 1), KEY_CHUNK) for g in groups]
    o_win = []
    for g in groups:
        acc = _dot(jnp.concatenate([vw(g)[:, rows] for rows, _ in mid], axis=1), wmid_p[g])
        acc = acc + unhalf(_dot(vw(g)[:, lo_rows], wlo_p[g]), _dot(vw(g)[:, hi_rows], whi_p[g]))
        o_win.append(normalize(acc))
    zero_half = jnp.zeros((VT_ROWS, HPG * KEY_CHUNK), F32)
    for g in groups:
        m_ref[g] = own_m[g]
        acc_ref[g] = (_dot(vt_ref[0, g, :, own_rows], own_p[g])
                      + unhalf(zero_half, _dot(vt_ref[0, g, :, hi_rows], ohi_p[g])))

    def slc_body(i, carry):
        sts, vss, dlss = [], [], []
        for g in range(N_GROUPS):
            word = lists_ref[(step_id * N_GROUPS + g) * LIST_WORDS + i]
            ks, vs, dls = [], [], []
            for u in range(SLC_BATCH):
                cid = lax.shift_right_logical(word, CHUNK_ID_BITS * u) & VOID_CHUNK
                valid = cid < VOID_CHUNK
                c = jnp.where(valid, cid, 0)
                rows = pl.ds(pl.multiple_of(c * KEY_CHUNK, KEY_CHUNK), KEY_CHUNK)
                ks.append(jnp.concatenate([kaug_ref[0, g, rows, :], oh_ref[rows, :]], axis=1))
                vs.append(vt_ref[0, g, :, rows])
                dls.append(slope_rows[g] * (start - c * KEY_CHUNK).astype(F32)
                           + jnp.where(valid, 0.0, MASK_BIG))
            sts.append(_dot(jnp.concatenate(ks, axis=0), qaug_ref[g]))
            vss.append(jnp.concatenate(vs, axis=1))
            dlss.append(dls)
        m_old = [m_ref[g] for g in range(N_GROUPS)]
        m_new = [jnp.maximum(m_old[g], _col_max(sts[g], dlss[g], KEY_CHUNK)) for g in range(N_GROUPS)]
        ps = [_probs(sts[g], dlss[g], m_new[g], KEY_CHUNK) for g in range(N_GROUPS)]
        for g in range(N_GROUPS):
            acc_ref[g] = jnp.exp2(m_old[g] - m_new[g]) * acc_ref[g] + _dot(vss[g], ps[g])
            m_ref[g] = m_new[g]
        return carry

    lax.fori_loop(0, counts_ref[step_id], slc_body, 0)

    per_head = []
    for g in range(N_GROUPS):
        o_slc = normalize(acc_ref[g])
        for j in range(HPG):
            h = HPG * g + j
            lanes = slice(NSA_QB * j, NSA_QB * (j + 1))
            per_head.append(gt[3 * h + 1:3 * h + 2, :] * o_slc[:, lanes]
                            + gt[3 * h + 2:3 * h + 3, :] * o_win[g][:, lanes])
    o_t = jnp.concatenate(per_head, axis=0)
    out_ref[0] = (ocmp_ref[0] + o_t).T.astype(BF16)


def _slc_win(counts, lists, qt, kaug, vt, onehot, selt, gatest, ocmp):
    bsz, _, t = qt.shape
    nqb = t // NSA_QB
    once = lambda shape, imap: pl.BlockSpec(shape, imap, pipeline_mode=pl.Buffered(1))
    grid_spec = pltpu.PrefetchScalarGridSpec(
        num_scalar_prefetch=2,
        grid=(bsz, nqb),
        in_specs=[pl.BlockSpec((1, NSA_W, NSA_QB), lambda b, i, *_: (b, 0, i)),
                  once((1, 2 * N_GROUPS, t, LANES), lambda b, i, *_: (b, 0, 0, 0)),
                  once((1, 2 * N_GROUPS, VT_ROWS, t), lambda b, i, *_: (b, 0, 0, 0)),
                  once((t, N_BLK_PAD), lambda b, i, *_: (0, 0)),
                  pl.BlockSpec((1, N_GROUPS, N_BLK_PAD, NSA_QB), lambda b, i, *_: (b, 0, 0, i)),
                  pl.BlockSpec((1, GATET_ROWS, NSA_QB), lambda b, i, *_: (b, 0, i)),
                  pl.BlockSpec((1, NSA_W, NSA_QB), lambda b, i, *_: (b, 0, i))],
        out_specs=pl.BlockSpec((1, NSA_QB, NSA_W), lambda b, i, *_: (b, i, 0)),
        scratch_shapes=[pltpu.VMEM((N_GROUPS, 2 * LANES, HPG * NSA_QB), BF16),
                        pltpu.VMEM((N_GROUPS, 1, HPG * NSA_QB), F32),
                        pltpu.VMEM((N_GROUPS, VT_ROWS, HPG * NSA_QB), F32)],
    )
    return pl.pallas_call(
        functools.partial(_slc_win_kernel, nqb=nqb),
        grid_spec=grid_spec,
        out_shape=jax.ShapeDtypeStruct((bsz, t, NSA_W), BF16),
        compiler_params=pltpu.CompilerParams(dimension_semantics=("arbitrary", "arbitrary"),
                                             vmem_limit_bytes=VMEM_LIMIT),
        name="slc_win",
    )(counts, lists, qt, kaug, vt, onehot, selt, gatest, ocmp)


def _merge_kernel(x_ref, onsa_ref, u_ref, vn_ref, g_ref, wm_ref, bm_ref, ws_ref, bs_ref,
                  wpa_ref, wpb_ref, wo_ref, h_ref, *, tm):
    x = x_ref[...]
    xn = _rms(x, g_ref[...]).astype(BF16)
    tril = _iota((GMLP_CHUNK, GMLP_CHUNK), 0) >= _iota((GMLP_CHUNK, GMLP_CHUNK), 1)
    sgu_rows = []
    for c in range(tm // GMLP_CHUNK):
        rows = slice(GMLP_CHUNK * c, GMLP_CHUNK * (c + 1))
        cols = []
        for g in range(GMLP_GROUPS):
            lanes = slice(LANES * g, LANES * (g + 1))
            w = jnp.where(tril, ws_ref[g], 0.0).astype(BF16)
            cols.append(_dot(w, vn_ref[rows, lanes]) + bs_ref[:, g:g + 1])
        sgu_rows.append(u_ref[rows, :] * jnp.concatenate(cols, axis=1))
    o_sgu = jnp.concatenate(sgu_rows, axis=0).astype(BF16)
    mg = jax.nn.sigmoid(_dot(xn, wm_ref[...]) + bm_ref[...])
    mixed = (mg[:, :D_MODEL] * _dot(onsa_ref[...], wpa_ref[...])
             + mg[:, D_MODEL:] * _dot(o_sgu, wpb_ref[...]))
    h_ref[...] = x + _dot(mixed.astype(BF16), wo_ref[...])


def _merge(x2, onsa2, u2, vn2, norm_g, wm, bm, ws, bs_t, wpa, wpb, wo, tm):
    n = x2.shape[0]
    row = lambda w: pl.BlockSpec((tm, w), lambda i: (i, 0))
    full = lambda a: pl.BlockSpec(a.shape, lambda i: (0,) * a.ndim)
    return pl.pallas_call(
        functools.partial(_merge_kernel, tm=tm),
        grid=(n // tm,),
        in_specs=[row(D_MODEL), row(NSA_W), row(GMLP_WIDTH), row(GMLP_WIDTH), full(norm_g),
                  full(wm), full(bm), full(ws), full(bs_t), full(wpa), full(wpb), full(wo)],
        out_specs=row(D_MODEL),
        out_shape=jax.ShapeDtypeStruct((n, D_MODEL), F32),
        compiler_params=pltpu.CompilerParams(dimension_semantics=("arbitrary",),
                                             vmem_limit_bytes=VMEM_LIMIT),
        name="merge",
    )(x2, onsa2, u2, vn2, norm_g, wm, bm, ws, bs_t, wpa, wpb, wo)


def _memkv_kernel(mem_ref, g_ref, w_ref, out_ref):
    out_ref[0] = _dot(_rms(mem_ref[0], g_ref[...]).astype(BF16), w_ref[...]).astype(BF16)


def _memkv(mem, norm_g, w):
    bsz, nm, _ = mem.shape
    return pl.pallas_call(
        _memkv_kernel,
        grid=(bsz,),
        in_specs=[pl.BlockSpec((1, nm, D_MODEL), lambda b: (b, 0, 0)),
                  pl.BlockSpec(norm_g.shape, lambda b: (0, 0)),
                  pl.BlockSpec(w.shape, lambda b: (0, 0))],
        out_specs=pl.BlockSpec((1, nm, 2 * MEM_W), lambda b: (b, 0, 0)),
        out_shape=jax.ShapeDtypeStruct((bsz, nm, 2 * MEM_W), BF16),
        compiler_params=pltpu.CompilerParams(dimension_semantics=("arbitrary",),
                                             vmem_limit_bytes=VMEM_LIMIT),
        name="memkv",
    )(mem, norm_g, w)


def _xattn_kernel(h_ref, g_ref, wq_ref, mkv_ref, wo_ref, out_ref, *, tm):
    halves = [slice(0, tm // 2), slice(tm // 2, tm)]
    head_lanes = [slice(MEM_HEAD_DIM * a, MEM_HEAD_DIM * (a + 1)) for a in range(MEM_HEADS)]
    hs = [h_ref[rows, :] for rows in halves]
    hqs = [(_dot(_rms(h, g_ref[...]).astype(BF16), wq_ref[...]) * (MEM_HEAD_DIM ** -0.5 * LOG2E)).astype(BF16)
           for h in hs]
    ss = [[_dot_nt(hq[:, lanes], mkv_ref[0, :, lanes]) for lanes in head_lanes] for hq in hqs]
    es = [[jnp.exp2(s - jnp.max(s, axis=-1, keepdims=True)) for s in s_half] for s_half in ss]
    os = []
    for e_half in es:
        heads = []
        for a, e in enumerate(e_half):
            v = mkv_ref[0, :, MEM_W + MEM_HEAD_DIM * a:MEM_W + MEM_HEAD_DIM * (a + 1)]
            heads.append(_dot(e.astype(BF16), v) * (1.0 / jnp.sum(e, axis=-1, keepdims=True)))
        os.append(jnp.concatenate(heads, axis=1).astype(BF16))
    for rows, h, o in zip(halves, hs, os):
        out_ref[rows, :] = h + _dot(o, wo_ref[...])


def _xattn(h2d, norm_g, wq, mkv, wo, tm, rows_per_batch):
    n = h2d.shape[0]
    nm = mkv.shape[1]
    tiles_per_batch = rows_per_batch // tm
    full = lambda a: pl.BlockSpec(a.shape, lambda i: (0,) * a.ndim)
    return pl.pallas_call(
        functools.partial(_xattn_kernel, tm=tm),
        grid=(n // tm,),
        in_specs=[pl.BlockSpec((tm, D_MODEL), lambda i: (i, 0)), full(norm_g), full(wq),
                  pl.BlockSpec((1, nm, 2 * MEM_W), lambda i: (i // tiles_per_batch, 0, 0)),
                  full(wo)],
        out_specs=pl.BlockSpec((tm, D_MODEL), lambda i: (i, 0)),
        out_shape=jax.ShapeDtypeStruct((n, D_MODEL), F32),
        compiler_params=pltpu.CompilerParams(dimension_semantics=("arbitrary",),
                                             vmem_limit_bytes=VMEM_LIMIT),
        name="xattn",
    )(h2d, norm_g, wq, mkv, wo)


def _ffn_kernel(h_ref, g_ref, wgu_ref, wd_ref, gf_ref, out_ref, *, d_ff):
    h = h_ref[...]
    hn = _rms(h, g_ref[...]).astype(BF16)
    gate = _dot(hn, wgu_ref[:, :d_ff])
    up = _dot(hn, wgu_ref[:, d_ff:])
    act = (jax.nn.silu(gate) * up).astype(BF16)
    y = h + _dot(act, wd_ref[...])
    out_ref[...] = _rms(y, gf_ref[...])


def _ffn(h2d, norm_g, wgu, wd, norm_f, tm):
    n = h2d.shape[0]
    d_ff = wd.shape[0]
    full = lambda a: pl.BlockSpec(a.shape, lambda i: (0,) * a.ndim)
    once = lambda a: pl.BlockSpec(a.shape, lambda i: (0,) * a.ndim, pipeline_mode=pl.Buffered(1))
    return pl.pallas_call(
        functools.partial(_ffn_kernel, d_ff=d_ff),
        grid=(n // tm,),
        in_specs=[pl.BlockSpec((tm, D_MODEL), lambda i: (i, 0)), full(norm_g), once(wgu), once(wd),
                  full(norm_f)],
        out_specs=pl.BlockSpec((tm, D_MODEL), lambda i: (i, 0)),
        out_shape=jax.ShapeDtypeStruct((n, D_MODEL), F32),
        compiler_params=pltpu.CompilerParams(dimension_semantics=("arbitrary",),
                                             vmem_limit_bytes=VMEM_LIMIT),
        name="ffn",
    )(h2d, norm_g, wgu, wd, norm_f)


def _block_diag2(w):
    z = jnp.zeros_like(w)
    return jnp.concatenate([jnp.concatenate([w, z], axis=-1), jnp.concatenate([z, w], axis=-1)], axis=-2)


def _chunk_lists(flags):
    bsz = flags.shape[0]
    n_chunks = N_KEY_CHUNKS
    per_step = NSA_QB // Q_BLOCK
    nqb = flags.shape[1] // per_step
    f = flags[:, :, :, 0, :].reshape(bsz, nqb, per_step, N_GROUPS, n_chunks, 2).max(axis=(2, 5))
    cid = jnp.arange(n_chunks, dtype=jnp.int32)
    own = (NSA_QB // KEY_CHUNK) * jnp.arange(nqb, dtype=jnp.int32)[None, :, None, None]
    active = (f > 0) & (cid < own)
    n_active = active.sum(axis=-1)
    slot = jnp.cumsum(active, axis=-1) - 1
    hit = active[..., :, None] & (slot[..., :, None] == cid)
    ids = jnp.sum(jnp.where(hit, cid[:, None], 0), axis=-2)
    ids = jnp.where(cid < n_active[..., None], ids, VOID_CHUNK)
    ids = jnp.pad(ids, ((0, 0), (0, 0), (0, 0), (0, LIST_WORDS * SLC_BATCH - n_chunks)),
                  constant_values=VOID_CHUNK)
    n_batches = (n_active.max(axis=-1) + SLC_BATCH - 1) // SLC_BATCH
    packed = ids.reshape(bsz, nqb, N_GROUPS, LIST_WORDS, SLC_BATCH)
    words = functools.reduce(jnp.bitwise_or,
                             [packed[..., u] << (CHUNK_ID_BITS * u) for u in range(SLC_BATCH)])
    return n_batches.reshape(-1).astype(jnp.int32), words.reshape(-1).astype(jnp.int32)


def kernel(x, mem, norm_mix, w_in, w_cmp_k1, w_cmp_k2, w_cmp_v1, w_cmp_v2, pe_cmp_k, pe_cmp_v, ln_sgu, w_spatial, b_spatial, w_proj_a, w_proj_b, w_merge, b_merge, w_out, norm_mem_q, norm_mem_kv, w_mq, w_mkv, w_mo, norm_ffn, w_gate_up, w_down, norm_final):
    bsz, t, d = x.shape
    depth = norm_mix.shape[0]
    assert d == D_MODEL and t % Q_BLOCK == 0 and t // SEL_BLOCK <= N_BLK_PAD
    assert t // SEL_BLOCK >= N_SELECT and depth == 1 and t % 512 == 0
    n = bsz * t
    tm = 512
    h = x.reshape(n, d)
    c0, c1, c2, c3 = NSA_W, NSA_W + KV_W, NSA_W + 3 * KV_W, NSA_W + 3 * KV_W + 2 * GMLP_WIDTH
    onehot = (jnp.arange(t)[:, None] // SEL_BLOCK == jnp.arange(N_BLK_PAD)[None, :]).astype(BF16)
    pad_cols = lambda w, width: jnp.pad(w, ((0, 0), (0, width - w.shape[1])))
    for l in range(depth):
        wi = w_in[l]
        wk, wv = [], []
        for branch in range(2):
            base = c1 + KV_W * branch
            for g in range(N_GROUPS):
                wk.append(pad_cols(wi[:, base + HEAD_DIM * g:base + HEAD_DIM * (g + 1)], LANES))
                v0 = base + N_GROUPS * HEAD_DIM + HEAD_DIM * g
                wv.append(pad_cols(wi[:, v0:v0 + HEAD_DIM], VT_ROWS))
        wrow = jnp.concatenate([wi[:, c0:c1]] + wk + [wi[:, c2:c3]], axis=1).astype(BF16)
        wnt = jnp.concatenate([wi[:, :c0]] + wv + [pad_cols(wi[:, c3:], GATET_ROWS)], axis=1).T.astype(BF16)
        qt, kvc2, kaug, vt, u2, vn2, gatest = _inproj(
            h, norm_mix[l][None], ln_sgu[l][None], wrow, wnt, tm, bsz, t)

        pe2 = jnp.stack([pe_cmp_k[l], pe_cmp_v[l]])
        pe2 = jnp.concatenate([pe2, pe2], axis=-1)
        w1 = jnp.stack([w_cmp_k1[l], w_cmp_v1[l]]).reshape(2, CMP_BLOCK, HEAD_DIM, CMP_HIDDEN)
        w1bd = _block_diag2(w1).astype(BF16)
        w2k = _block_diag2(w_cmp_k2[l]).astype(BF16)
        w2vt = _block_diag2(w_cmp_v2[l]).T.astype(BF16)
        kc, vct = _compress(kvc2.reshape(bsz, t, KV_W), pe2, w1bd, w2k, w2vt)
        ocmp, selt, flags = _cmp_topk(qt, kc, vct, gatest)
        counts, lists = _chunk_lists(flags)
        onsa = _slc_win(counts, lists, qt, kaug, vt, onehot, selt, gatest, ocmp)

        h = _merge(h, onsa.reshape(n, NSA_W), u2, vn2, norm_mix[l][None], w_merge[l].astype(BF16),
                   b_merge[l][None], w_spatial[l], b_spatial[l].T, w_proj_a[l].astype(BF16),
                   w_proj_b[l].astype(BF16), w_out[l].astype(BF16), tm)

        mkv = _memkv(mem, norm_mem_kv[l][None], w_mkv[l].astype(BF16))
        h = _xattn(h, norm_mem_q[l][None], w_mq[l].astype(BF16), mkv, w_mo[l].astype(BF16), tm, t)
        h = _ffn(h, norm_ffn[l][None], w_gate_up[l].astype(BF16), w_down[l].astype(BF16),
                 norm_final[None], tm)
    return h.reshape(bsz, t, d)
```

```python
import functools

import jax
import jax.numpy as jnp
from jax import lax
from jax.experimental import pallas as pl
from jax.experimental.pallas import tpu as pltpu

F32 = jnp.float32
BF16 = jnp.bfloat16

LANES = 128
SUBLANES = 8
D_MODEL = 1024
N_HEADS = 8
HEAD_DIM = 64
N_GROUPS = 2
HPG = N_HEADS // N_GROUPS
CMP_BLOCK = 32
CMP_STRIDE = 16
CMP_HIDDEN = 128
SEL_BLOCK = 64
N_SELECT = 16
WINDOW = 512
Q_BLOCK = 256
KEY_CHUNK = 128
N_BLK_PAD = 128
NSA_W = N_HEADS * HEAD_DIM
KV_W = 2 * N_GROUPS * HEAD_DIM
GMLP_WIDTH = 512
GMLP_GROUPS = 4
GMLP_CHUNK = 128
MEM_HEADS = 4
MEM_HEAD_DIM = 128
MEM_W = MEM_HEADS * MEM_HEAD_DIM
GATE_W = 3 * N_HEADS
NSA_QB = 256
SLC_BATCH = (4, 5)
N_KEY_CHUNKS = N_BLK_PAD // 2
CHUNK_ID_BITS = 6
VOID_CHUNK = N_KEY_CHUNKS - 1
LIST_WORDS = -(-N_KEY_CHUNKS // min(SLC_BATCH))
VT_ROWS = 80
VCT_ROWS = VT_ROWS + N_BLK_PAD
CMP_KEY_CHUNK = 128
N_FORCED = 3
TOPK_ROW_STEP = 64
CMP_BLOCKS = 2
GATET_ROWS = 32
MASK_BIG = 1e30
EPS = 1e-6
NEG = -1e30
REMOVED = -3e38
FORCE_SCORE = 1e6
SLOPES = tuple(2.0 ** (-8.0 * (h + 1) / N_HEADS) for h in range(N_HEADS))
LOG2E = 1.4426950408889634
Q_SCALE = HEAD_DIM ** -0.5 * LOG2E
VMEM_LIMIT = 56 * 1024 * 1024


def _dot(a, b):
    return jnp.dot(a, b, preferred_element_type=F32)


def _dot_nt(a, b):
    return lax.dot_general(a, b, (((1,), (1,)), ((), ())), preferred_element_type=F32)


def _rms(x, g):
    return x * lax.rsqrt(jnp.mean(x * x, axis=-1, keepdims=True) + EPS) * g


def _iota(shape, dim):
    return lax.broadcasted_iota(jnp.int32, shape, dim)


def _slope_row(g, nq):
    return jnp.concatenate(
        [jnp.full((1, nq), SLOPES[HPG * g + j] * LOG2E, F32) for j in range(HPG)], axis=1)


def _slope_feature_rows(slope_row, n_rows):
    hi = slope_row.astype(BF16).astype(F32)
    r = _iota((n_rows, slope_row.shape[1]), 0)
    return jnp.where(r == 0, hi, jnp.where(r == 1, slope_row - hi, 0.0)).astype(BF16)


def _chunk_slabs(st, chunk):
    return [st[chunk * u:chunk * (u + 1)] for u in range(st.shape[0] // chunk)]


def _col_max(st, dls, chunk):
    parts = [s.reshape(chunk // SUBLANES, SUBLANES, s.shape[1]).max(axis=0) - dl
             for s, dl in zip(_chunk_slabs(st, chunk), dls)]
    return functools.reduce(jnp.maximum, parts).max(axis=0, keepdims=True)


def _probs(st, dls, m, chunk):
    return jnp.concatenate([jnp.exp2(s - (m + dl)).astype(BF16)
                            for s, dl in zip(_chunk_slabs(st, chunk), dls)], axis=0)


def _inproj_kernel(x_ref, g_ref, lng_ref, wrow_ref, wnt_ref,
                   qt_ref, kvc_ref, kaug_ref, vt_ref, u_ref, vn_ref, gatest_ref, *, tm):
    xn = _rms(x_ref[...], g_ref[...]).astype(BF16)
    r = _dot(xn, wrow_ref[...])
    nt = _dot_nt(wnt_ref[...], xn)
    qt_ref[0] = (nt[0:NSA_W] * Q_SCALE).astype(BF16)
    kvc_ref[...] = r[:, 0:KV_W]
    lane = _iota((tm, LANES), 1)
    key_feat = jnp.where((lane == HEAD_DIM) | (lane == HEAD_DIM + 1),
                         _iota((tm, LANES), 0) & (KEY_CHUNK - 1), 0).astype(F32)
    ones_row = jnp.where(_iota((VT_ROWS, tm), 0) == HEAD_DIM, 1.0, 0.0)
    for a in range(2 * N_GROUPS):
        kaug_ref[0, a] = (r[:, KV_W + LANES * a:KV_W + LANES * (a + 1)] + key_feat).astype(BF16)
        vt_ref[0, a] = (nt[NSA_W + VT_ROWS * a:NSA_W + VT_ROWS * (a + 1)] + ones_row).astype(BF16)
    uv = jax.nn.gelu(r[:, KV_W + 2 * N_GROUPS * LANES:])
    u_ref[...] = uv[:, :GMLP_WIDTH]
    v = uv[:, GMLP_WIDTH:]
    vc = v - jnp.mean(v, axis=-1, keepdims=True)
    vn = vc * lax.rsqrt(jnp.mean(vc * vc, axis=-1, keepdims=True) + EPS) * lng_ref[...]
    vn_ref[...] = vn.astype(BF16)
    gatest_ref[0] = jax.nn.sigmoid(nt[NSA_W + 2 * N_GROUPS * VT_ROWS:])


def _inproj(x2, norm_g, ln_g, wrow, wnt, tm, bsz, t):
    n = x2.shape[0]
    tpb = t // tm
    row = lambda w: pl.BlockSpec((tm, w), lambda i: (i, 0))
    full = lambda a: pl.BlockSpec(a.shape, lambda i: (0,) * a.ndim)
    return pl.pallas_call(
        functools.partial(_inproj_kernel, tm=tm),
        grid=(n // tm,),
        in_specs=[row(D_MODEL), full(norm_g), full(ln_g), full(wrow), full(wnt)],
        out_specs=[pl.BlockSpec((1, NSA_W, tm), lambda i: (i // tpb, 0, i % tpb)),
                   row(KV_W),
                   pl.BlockSpec((1, 2 * N_GROUPS, tm, LANES), lambda i: (i // tpb, 0, i % tpb, 0)),
                   pl.BlockSpec((1, 2 * N_GROUPS, VT_ROWS, tm), lambda i: (i // tpb, 0, 0, i % tpb)),
                   row(GMLP_WIDTH), row(GMLP_WIDTH),
                   pl.BlockSpec((1, GATET_ROWS, tm), lambda i: (i // tpb, 0, i % tpb))],
        out_shape=[jax.ShapeDtypeStruct((bsz, NSA_W, t), BF16),
                   jax.ShapeDtypeStruct((n, KV_W), F32),
                   jax.ShapeDtypeStruct((bsz, 2 * N_GROUPS, t, LANES), BF16),
                   jax.ShapeDtypeStruct((bsz, 2 * N_GROUPS, VT_ROWS, t), BF16),
                   jax.ShapeDtypeStruct((n, GMLP_WIDTH), F32),
                   jax.ShapeDtypeStruct((n, GMLP_WIDTH), BF16),
                   jax.ShapeDtypeStruct((bsz, GATET_ROWS, t), F32)],
        compiler_params=pltpu.CompilerParams(dimension_semantics=("arbitrary",),
                                             vmem_limit_bytes=VMEM_LIMIT),
        name="inproj",
    )(x2, norm_g, ln_g, wrow, wnt)


def _compress_kernel(xk_ref, xv_ref, pe_ref, w1_ref, w2k_ref, w2vt_ref, kc_ref, vct_ref, *, nc):
    outs = []
    for j, x_ref in enumerate((xk_ref, xv_ref)):
        a = jnp.zeros((nc, 2 * CMP_HIDDEN), F32)
        b = jnp.zeros((nc, 2 * CMP_HIDDEN), F32)
        for t in range(CMP_STRIDE):
            xt = x_ref[0, pl.ds(t, nc, stride=CMP_STRIDE), :]
            a = a + _dot((xt + pe_ref[j, t:t + 1, :]).astype(BF16), w1_ref[j, t])
            b = b + _dot((xt + pe_ref[j, CMP_STRIDE + t:CMP_STRIDE + t + 1, :]).astype(BF16),
                         w1_ref[j, CMP_STRIDE + t])
        outs.append(jax.nn.gelu(a + pltpu.roll(b, nc - 1, 0)).astype(BF16))
    k2 = _dot(outs[0], w2k_ref[...])
    v_t = _dot_nt(w2vt_ref[...], outs[1])
    lane = _iota((nc, LANES), 1)
    key_feat = jnp.where((lane == HEAD_DIM) | (lane == HEAD_DIM + 1),
                         CMP_STRIDE * (_iota((nc, LANES), 0) & (CMP_KEY_CHUNK - 1)), 0).astype(F32)
    ci =_iota((N_BLK_PAD, nc), 1) * CMP_STRIDE
    sj = _iota((N_BLK_PAD, nc), 0) * SEL_BLOCK
    overlap_t = jnp.where((ci < sj + SEL_BLOCK) & (ci + (CMP_BLOCK - 1) >= sj), 1.0, 0.0).astype(BF16)
    ones_rows = jnp.where(_iota((VT_ROWS - HEAD_DIM, nc), 0) == 0, 1.0, 0.0).astype(BF16)
    for g in range(N_GROUPS):
        kg = k2 if g == 0 else pltpu.roll(k2, HEAD_DIM, 1)
        kc_ref[0, g] = jnp.where(lane < HEAD_DIM, kg, key_feat).astype(BF16)
        vct_ref[0, g, 0:HEAD_DIM, :] = v_t[HEAD_DIM * g:HEAD_DIM * (g + 1), :].astype(BF16)
        vct_ref[0, g, HEAD_DIM:VT_ROWS, :] = ones_rows
        vct_ref[0, g, VT_ROWS:VCT_ROWS, :] = overlap_t


def _compress(kvc3, pe2, w1bd, w2k, w2vt):
    bsz, t, _ = kvc3.shape
    nc = t // CMP_STRIDE
    full = lambda a: pl.BlockSpec(a.shape, lambda b: (0,) * a.ndim)
    return pl.pallas_call(
        functools.partial(_compress_kernel, nc=nc),
        grid=(bsz,),
        in_specs=[pl.BlockSpec((1, t, LANES), lambda b: (b, 0, 0)),
                  pl.BlockSpec((1, t, LANES), lambda b: (b, 0, 1)),
                  full(pe2), full(w1bd), full(w2k), full(w2vt)],
        out_specs=[pl.BlockSpec((1, N_GROUPS, nc, LANES), lambda b: (b, 0, 0, 0)),
                   pl.BlockSpec((1, N_GROUPS, VCT_ROWS, nc), lambda b: (b, 0, 0, 0))],
        out_shape=[jax.ShapeDtypeStruct((bsz, N_GROUPS, nc, LANES), BF16),
                   jax.ShapeDtypeStruct((bsz, N_GROUPS, VCT_ROWS, nc), BF16)],
        compiler_params=pltpu.CompilerParams(dimension_semantics=("arbitrary",),
                                             vmem_limit_bytes=VMEM_LIMIT),
        name="compress",
    )(kvc3, kvc3, pe2, w1bd, w2k, w2vt)


def _cmp_topk_kernel(qt_ref, kc_ref, vct_ref, gatest_ref, ocmp_ref, selt_ref, flags_ref,
                     m_ref, acc_ref, *, ncp):
    step = pl.program_id(1)
    n_chunks = ncp // CMP_KEY_CHUNK
    chunk_tokens = CMP_KEY_CHUNK * CMP_STRIDE
    tile_heads = lambda a: jnp.concatenate([a] * HPG, axis=1)
    tail_chunks = min(2, n_chunks)
    head_chunks = n_chunks - tail_chunks
    tail_keys = tail_chunks * CMP_KEY_CHUNK
    blocks = range(CMP_BLOCKS)
    groups = range(N_GROUPS)
    items = [(bi, g) for bi in blocks for g in groups]
    start = [(step * CMP_BLOCKS + bi) * Q_BLOCK for bi in blocks]
    qlanes = [slice(Q_BLOCK * bi, Q_BLOCK * (bi + 1)) for bi in blocks]
    nck = [(start[bi] + Q_BLOCK - CMP_BLOCK) // CMP_STRIDE // CMP_KEY_CHUNK + 1 for bi in blocks]
    tail_c0 = [jnp.maximum(nck[bi] - tail_chunks, 0) for bi in blocks]
    tail_rows = [pl.ds(pl.multiple_of(tail_c0[bi] * CMP_KEY_CHUNK, CMP_KEY_CHUNK), tail_keys)
                 for bi in blocks]
    gt = gatest_ref[0]
    slope_rows = [_slope_row(g, Q_BLOCK) for g in groups]
    qas = {(bi, g): jnp.concatenate(
        [jnp.concatenate([qt_ref[0, HEAD_DIM * (HPG * g + j):HEAD_DIM * (HPG * g + j + 1), qlanes[bi]]
                          for j in range(HPG)], axis=1),
         _slope_feature_rows(slope_rows[g], LANES - HEAD_DIM)], axis=0) for bi, g in items}

    def delta(bi, g, c):
        return slope_rows[g] * (start[bi] - c * chunk_tokens).astype(F32)

    for bi in blocks:
        if head_chunks > 0:
            @pl.when(nck[bi] > tail_chunks)
            def _():
                sts = [_dot(kc_ref[0, g, 0:head_chunks * CMP_KEY_CHUNK, :], qas[bi, g]) for g in groups]
                dls = [[delta(bi, g, c) + jnp.where(c < nck[bi] - tail_chunks, 0.0, MASK_BIG)
                        for c in range(head_chunks)] for g in groups]
                ms = [_col_max(sts[g], dls[g], CMP_KEY_CHUNK) for g in groups]
                ps = [_probs(sts[g], dls[g], ms[g], CMP_KEY_CHUNK) for g in groups]
                for g in groups:
                    m_ref[bi, g] = ms[g]
                    acc_ref[bi, g] = _dot(vct_ref[0, g, :, 0:head_chunks * CMP_KEY_CHUNK], ps[g])

            @pl.when(nck[bi] <= tail_chunks)
            def _():
                m_ref[bi] = jnp.full(m_ref.shape[1:], NEG, F32)
                acc_ref[bi] = jnp.zeros(acc_ref.shape[1:], F32)

    def tail_and_select(n_rows):
        if head_chunks > 0:
            m_old = {it: m_ref[it[0], it[1]] for it in items}
            acc_old = {it: acc_ref[it[0], it[1]] for it in items}
        else:
            m_old = {it: jnp.full((1, HPG * Q_BLOCK), NEG, F32) for it in items}
            acc_old = {it: jnp.zeros((VCT_ROWS, HPG * Q_BLOCK), F32) for it in items}
        key_row = _iota((tail_keys, Q_BLOCK), 0)
        q_lane = _iota((tail_keys, Q_BLOCK), 1)
        tail_bias = []
        for bi in blocks:
            key_end = CMP_STRIDE * (tail_c0[bi] * CMP_KEY_CHUNK + key_row) + (CMP_BLOCK - 1) - start[bi]
            tail_bias.append(tile_heads(jnp.where(key_end <= q_lane, 0.0, NEG)))
        sts = {(bi, g): _dot(kc_ref[0, g, tail_rows[bi], :], qas[bi, g]) + tail_bias[bi] for bi, g in items}
        dls = {(bi, g): [delta(bi, g, tail_c0[bi] + u) for u in range(tail_chunks)] for bi, g in items}
        m_new = {it: jnp.maximum(m_old[it], _col_max(sts[it], dls[it], CMP_KEY_CHUNK)) for it in items}
        ps = {it: _probs(sts[it], dls[it], m_new[it], CMP_KEY_CHUNK) for it in items}
        accs = {(bi, g): jnp.exp2(m_old[bi, g] - m_new[bi, g]) * acc_old[bi, g]
                + _dot(vct_ref[0, g, :, tail_rows[bi]], ps[bi, g]) for bi, g in items}

        blk_n = _iota((n_rows, Q_BLOCK), 0)
        bf = blk_n.astype(F32)
        rk, cur = {}, []
        for bi in blocks:
            t_row = start[bi] + _iota((1, Q_BLOCK), 1)
            cur.append(lax.shift_right_logical(t_row, SEL_BLOCK.bit_length() - 1))
            forced = (blk_n == 0) | (blk_n == cur[bi]) | (blk_n == cur[bi] - 1)
            has_key = t_row >= CMP_BLOCK - 1
            per_head = []
            for g in groups:
                acc = accs[bi, g]
                inv_l = 1.0 / jnp.maximum(acc[HEAD_DIM:HEAD_DIM + 1], 1e-30)
                o = acc[0:HEAD_DIM] * inv_l
                imp_h = acc[VT_ROWS:VT_ROWS + n_rows] * inv_l
                imp = sum(imp_h[:, Q_BLOCK * j:Q_BLOCK * (j + 1)] for j in range(HPG))
                imp = jnp.where(has_key, imp, 0.0)
                for j in range(HPG):
                    h = HPG * g + j
                    per_head.append(jnp.where(
                        has_key, gt[3 * h:3 * h + 1, qlanes[bi]] * o[:, Q_BLOCK * j:Q_BLOCK * (j + 1)], 0.0))
                rk[bi, g] = jnp.where(blk_n <= cur[bi], jnp.where(forced, REMOVED, imp), NEG)
            ocmp_ref[0, :, qlanes[bi]] = jnp.concatenate(per_head, axis=0)

        for _ in range(N_SELECT - N_FORCED):
            for it in items:
                m = jnp.max(rk[it], axis=0, keepdims=True)
                idx = jnp.min(jnp.where(rk[it] == m, bf, float(N_BLK_PAD)), axis=0, keepdims=True)
                rk[it] = jnp.where(bf == idx, REMOVED, rk[it])
        ones8 = jnp.ones((SUBLANES, Q_BLOCK), F32)
        for bi, g in items:
            sel = jnp.where((blk_n <= cur[bi]) & (rk[bi, g] < 2.0 * NEG), 1.0, 0.0)
            selt_ref[0, g, 0:n_rows, qlanes[bi]] = sel.astype(BF16)
            flag = (_dot_nt(ones8, sel) > 0.5).astype(jnp.int32)
            if n_rows < N_BLK_PAD:
                selt_ref[0, g, n_rows:N_BLK_PAD, qlanes[bi]] = jnp.zeros((N_BLK_PAD - n_rows, Q_BLOCK), BF16)
                flag = jnp.concatenate([flag, jnp.zeros((SUBLANES, N_BLK_PAD - n_rows), jnp.int32)], axis=1)
            flags_ref[0, bi, g] = flag

    causal_rows = (start[-1] + Q_BLOCK) // SEL_BLOCK
    for n_rows in range(TOPK_ROW_STEP, N_BLK_PAD + 1, TOPK_ROW_STEP):
        @pl.when((causal_rows > n_rows - TOPK_ROW_STEP) & (causal_rows <= n_rows))
        def _():
            tail_and_select(n_rows)


def _cmp_topk(qt, kc, vct, gatest):
    bsz, _, t = qt.shape
    ncp = t // CMP_STRIDE
    nqb = t // Q_BLOCK
    step_q = CMP_BLOCKS * Q_BLOCK
    return pl.pallas_call(
        functools.partial(_cmp_topk_kernel, ncp=ncp),
        grid=(bsz, nqb // CMP_BLOCKS),
        in_specs=[pl.BlockSpec((1, NSA_W, step_q), lambda b, i: (b, 0, i)),
                  pl.BlockSpec((1, N_GROUPS, ncp, LANES), lambda b, i: (b, 0, 0, 0)),
                  pl.BlockSpec((1, N_GROUPS, VCT_ROWS, ncp), lambda b, i: (b, 0, 0, 0)),
                  pl.BlockSpec((1, GATET_ROWS, step_q), lambda b, i: (b, 0, i))],
        out_specs=[pl.BlockSpec((1, NSA_W, step_q), lambda b, i: (b, 0, i)),
                   pl.BlockSpec((1, N_GROUPS, N_BLK_PAD, step_q), lambda b, i: (b, 0, 0, i)),
                   pl.BlockSpec((1, CMP_BLOCKS, N_GROUPS, SUBLANES, N_BLK_PAD), lambda b, i: (b, i, 0, 0, 0))],
        out_shape=[jax.ShapeDtypeStruct((bsz, NSA_W, t), F32),
                   jax.ShapeDtypeStruct((bsz, N_GROUPS, N_BLK_PAD, t), BF16),
                   jax.ShapeDtypeStruct((bsz, nqb, N_GROUPS, SUBLANES, N_BLK_PAD), jnp.int32)],
        scratch_shapes=[pltpu.VMEM((CMP_BLOCKS, N_GROUPS, 1, HPG * Q_BLOCK), F32),
                        pltpu.VMEM((CMP_BLOCKS, N_GROUPS, VCT_ROWS, HPG * Q_BLOCK), F32)],
        compiler_params=pltpu.CompilerParams(dimension_semantics=("arbitrary", "arbitrary"),
                                             vmem_limit_bytes=VMEM_LIMIT),
        name="cmp_topk",
    )(qt, kc, vct, gatest)


def _slc_win_kernel(counts_ref, lists_ref, qt_ref, kaug_ref, vt_ref, oh_ref, selt_ref, gatest_ref,
                    ocmp_ref, out_ref, qaug_ref, m_ref, acc_ref, *, nqb):
    b = pl.program_id(0)
    qb = pl.program_id(1)
    step_id = b * nqb + qb
    start = qb * NSA_QB
    tile_heads = lambda a: jnp.concatenate([a] * HPG, axis=1)
    groups = range(N_GROUPS)
    gt = gatest_ref[0]
    slope_rows = [_slope_row(g, NSA_QB) for g in groups]

    def normalize(acc):
        return acc[0:HEAD_DIM] / jnp.maximum(acc[HEAD_DIM:HEAD_DIM + 1], 1e-30)

    def half_lanes(w):
        return [slice(NSA_QB * h + KEY_CHUNK * w, NSA_QB * h + KEY_CHUNK * (w + 1)) for h in range(HPG)]

    def half(a, w):
        return jnp.concatenate([a[:, s] for s in half_lanes(w)], axis=1)

    def unhalf(lo, hi):
        return jnp.concatenate([x[:, KEY_CHUNK * h:KEY_CHUNK * (h + 1)]
                                for h in range(HPG) for x in (lo, hi)], axis=1)

    def chunk_at(rel):
        pos = start + rel * KEY_CHUNK
        rows = pl.ds(pl.multiple_of(jnp.maximum(pos, 0), KEY_CHUNK), KEY_CHUNK)
        return rows, (jnp.where(pos >= 0, 0.0, MASK_BIG) if rel < 0 else 0.0)

    ki = _iota((KEY_CHUNK, KEY_CHUNK), 0)
    qi = _iota((KEY_CHUNK, KEY_CHUNK), 1)
    upper_bias = tile_heads(jnp.where(ki > qi, 0.0, NEG))
    lower_bias = tile_heads(jnp.where(ki <= qi, 0.0, NEG))
    n_mid = WINDOW // KEY_CHUNK
    mid_dist = (_iota((n_mid * KEY_CHUNK, NSA_QB), 1) + (n_mid - 1) * KEY_CHUNK
                - _iota((n_mid * KEY_CHUNK, NSA_QB), 0))
    mid_bias = tile_heads(jnp.where((mid_dist >= 0) & (mid_dist < WINDOW), 0.0, NEG))
    own_bias = tile_heads(jnp.where(_iota((KEY_CHUNK, NSA_QB), 0) <= _iota((KEY_CHUNK, NSA_QB), 1), 0.0, NEG))

    for g in range(N_GROUPS):
        qaug_ref[g, 0:HEAD_DIM, :] = jnp.concatenate(
            [qt_ref[0, HEAD_DIM * (HPG * g + j):HEAD_DIM * (HPG * g + j + 1), :] for j in range(HPG)],
            axis=1)
        qaug_ref[g, HEAD_DIM:LANES, :] = _slope_feature_rows(slope_rows[g], LANES - HEAD_DIM)
        sel_bias = ((selt_ref[0, g].astype(F32) - 1.0) * MASK_BIG).astype(BF16)
        qaug_ref[g, LANES:2 * LANES, :] = tile_heads(sel_bias)

    mid = [chunk_at(r) for r in range(1 - n_mid, 1)]
    lo_rows, lo_kill = chunk_at(-n_mid)
    hi_rows, _ = chunk_at(1)
    own_rows, _ = chunk_at(0)
    kw = lambda g: kaug_ref.at[0, N_GROUPS + g]
    vw = lambda g: vt_ref.at[0, N_GROUPS + g]
    slc_keys = lambda g, rows: jnp.concatenate([kaug_ref[0, g, rows, :], oh_ref[rows, :]], axis=1)
    q_half = lambda g, w, nrow: jnp.concatenate([qaug_ref[g, 0:nrow, s] for s in half_lanes(w)], axis=1)

    wmid_st = [_dot(jnp.concatenate([kw(g)[rows, :] for rows, _ in mid], axis=0), qaug_ref[g, 0:LANES, :])
               + mid_bias for g in groups]
    wlo_st = [_dot(kw(g)[lo_rows, :], q_half(g, 0, LANES)) + upper_bias for g in groups]
    whi_st = [_dot(kw(g)[hi_rows, :], q_half(g, 1, LANES)) + lower_bias for g in groups]
    own_st = [_dot(slc_keys(g, own_rows), qaug_ref[g]) + own_bias for g in groups]
    ohi_st = [_dot(slc_keys(g, hi_rows), q_half(g, 1, 2 * LANES)) + lower_bias for g in groups]

    wmid_dls = [[slope_rows[g] * float(-KEY_CHUNK * r) + kill
                 for r, (_, kill) in zip(range(1 - n_mid, 1), mid)] for g in groups]
    wlo_dl = [half(slope_rows[g], 0) * float(KEY_CHUNK * n_mid) + lo_kill for g in groups]
    hi_dl = [half(slope_rows[g], 1) * float(-KEY_CHUNK) for g in groups]
    neg_half = jnp.full((1, HPG * KEY_CHUNK), NEG, F32)
    win_m = [jnp.maximum(_col_max(wmid_st[g], wmid_dls[g], KEY_CHUNK),
                         unhalf(_col_max(wlo_st[g], [wlo_dl[g]], KEY_CHUNK),
                                _col_max(whi_st[g], [hi_dl[g]], KEY_CHUNK))) for g in groups]
    own_m = [jnp.maximum(_col_max(own_st[g], [0.0], KEY_CHUNK),
                         unhalf(neg_half, _col_max(ohi_st[g], [hi_dl[g]], KEY_CHUNK))) for g in groups]
    wmid_p = [_probs(wmid_st[g], wmid_dls[g], win_m[g], KEY_CHUNK) for g in groups]
    wlo_p = [_probs(wlo_st[g], [wlo_dl[g]], half(win_m[g], 0), KEY_CHUNK) for g in groups]
    whi_p = [_probs(whi_st[g], [hi_dl[g]], half(win_m[g], 1), KEY_CHUNK) for g in groups]
    own_p = [_probs(own_st[g], [0.0], own_m[g], KEY_CHUNK) for g in groups]
    ohi_p = [_probs(ohi_st[g], [hi_dl[g]], half(own_m[g], 1), KEY_CHUNK) for g in groups]
    o_win = []
    for g in groups:
        acc = _dot(jnp.concatenate([vw(g)[:, rows] for rows, _ in mid], axis=1), wmid_p[g])
        acc = acc + unhalf(_dot(vw(g)[:, lo_rows], wlo_p[g]), _dot(vw(g)[:, hi_rows], whi_p[g]))
        o_win.append(normalize(acc))
    zero_half = jnp.zeros((VT_ROWS, HPG * KEY_CHUNK), F32)
    for g in groups:
        m_ref[g] = own_m[g]
        acc_ref[g] = (_dot(vt_ref[0, g, :, own_rows], own_p[g])
                      + unhalf(zero_half, _dot(vt_ref[0, g, :, hi_rows], ohi_p[g])))

    def slc_body(i, carry):
        sts, vss, dlss = [], [], []
        for g in range(N_GROUPS):
            word = lists_ref[(step_id * N_GROUPS + g) * LIST_WORDS + i]
            ks, vs, dls = [], [], []
            for u in range(SLC_BATCH[g]):
                cid = lax.shift_right_logical(word, CHUNK_ID_BITS * u) & VOID_CHUNK
                valid = cid < VOID_CHUNK
                c = jnp.where(valid, cid, 0)
                rows = pl.ds(pl.multiple_of(c * KEY_CHUNK, KEY_CHUNK), KEY_CHUNK)
                ks.append(jnp.concatenate([kaug_ref[0, g, rows, :], oh_ref[rows, :]], axis=1))
                vs.append(vt_ref[0, g, :, rows])
                dls.append(slope_rows[g] * (start - c * KEY_CHUNK).astype(F32)
                           + jnp.where(valid, 0.0, MASK_BIG))
            sts.append(_dot(jnp.concatenate(ks, axis=0), qaug_ref[g]))
            vss.append(jnp.concatenate(vs, axis=1))
            dlss.append(dls)
        m_old = [m_ref[g] for g in range(N_GROUPS)]
        m_new = [jnp.maximum(m_old[g], _col_max(sts[g], dlss[g], KEY_CHUNK)) for g in range(N_GROUPS)]
        ps = [_probs(sts[g], dlss[g], m_new[g], KEY_CHUNK) for g in range(N_GROUPS)]
        for g in range(N_GROUPS):
            acc_ref[g] = jnp.exp2(m_old[g] - m_new[g]) * acc_ref[g] + _dot(vss[g], ps[g])
            m_ref[g] = m_new[g]
        return carry

    lax.fori_loop(0, counts_ref[step_id], slc_body, 0)

    per_head = []
    for g in range(N_GROUPS):
        o_slc = normalize(acc_ref[g])
        for j in range(HPG):
            h = HPG * g + j
            lanes = slice(NSA_QB * j, NSA_QB * (j + 1))
            per_head.append(gt[3 * h + 1:3 * h + 2, :] * o_slc[:, lanes]
                            + gt[3 * h + 2:3 * h + 3, :] * o_win[g][:, lanes])
    o_t = jnp.concatenate(per_head, axis=0)
    out_ref[0] = (ocmp_ref[0] + o_t).T.astype(BF16)


def _slc_win(counts, lists, qt, kaug, vt, onehot, selt, gatest, ocmp):
    bsz, _, t = qt.shape
    nqb = t // NSA_QB
    once = lambda shape, imap: pl.BlockSpec(shape, imap, pipeline_mode=pl.Buffered(1))
    grid_spec = pltpu.PrefetchScalarGridSpec(
        num_scalar_prefetch=2,
        grid=(bsz, nqb),
        in_specs=[pl.BlockSpec((1, NSA_W, NSA_QB), lambda b, i, *_: (b, 0, i)),
                  once((1, 2 * N_GROUPS, t, LANES), lambda b, i, *_: (b, 0, 0, 0)),
                  once((1, 2 * N_GROUPS, VT_ROWS, t), lambda b, i, *_: (b, 0, 0, 0)),
                  once((t, N_BLK_PAD), lambda b, i, *_: (0, 0)),
                  pl.BlockSpec((1, N_GROUPS, N_BLK_PAD, NSA_QB), lambda b, i, *_: (b, 0, 0, i)),
                  pl.BlockSpec((1, GATET_ROWS, NSA_QB), lambda b, i, *_: (b, 0, i)),
                  pl.BlockSpec((1, NSA_W, NSA_QB), lambda b, i, *_: (b, 0, i))],
        out_specs=pl.BlockSpec((1, NSA_QB, NSA_W), lambda b, i, *_: (b, i, 0)),
        scratch_shapes=[pltpu.VMEM((N_GROUPS, 2 * LANES, HPG * NSA_QB), BF16),
                        pltpu.VMEM((N_GROUPS, 1, HPG * NSA_QB), F32),
                        pltpu.VMEM((N_GROUPS, VT_ROWS, HPG * NSA_QB), F32)],
    )
    return pl.pallas_call(
        functools.partial(_slc_win_kernel, nqb=nqb),
        grid_spec=grid_spec,
        out_shape=jax.ShapeDtypeStruct((bsz, t, NSA_W), BF16),
        compiler_params=pltpu.CompilerParams(dimension_semantics=("arbitrary", "arbitrary"),
                                             vmem_limit_bytes=VMEM_LIMIT),
        name="slc_win",
    )(counts, lists, qt, kaug, vt, onehot, selt, gatest, ocmp)


def _merge_kernel(x_ref, onsa_ref, u_ref, vn_ref, g_ref, wm_ref, bm_ref, ws_ref, bs_ref,
                  wpa_ref, wpb_ref, wo_ref, h_ref, *, tm):
    x = x_ref[...]
    xn = _rms(x, g_ref[...]).astype(BF16)
    tril = _iota((GMLP_CHUNK, GMLP_CHUNK), 0) >= _iota((GMLP_CHUNK, GMLP_CHUNK), 1)
    sgu_rows = []
    for c in range(tm // GMLP_CHUNK):
        rows = slice(GMLP_CHUNK * c, GMLP_CHUNK * (c + 1))
        cols = []
        for g in range(GMLP_GROUPS):
            lanes = slice(LANES * g, LANES * (g + 1))
            w = jnp.where(tril, ws_ref[g], 0.0).astype(BF16)
            cols.append(_dot(w, vn_ref[rows, lanes]) + bs_ref[:, g:g + 1])
        sgu_rows.append(u_ref[rows, :] * jnp.concatenate(cols, axis=1))
    o_sgu = jnp.concatenate(sgu_rows, axis=0).astype(BF16)
    mg = jax.nn.sigmoid(_dot(xn, wm_ref[...]) + bm_ref[...])
    mixed = (mg[:, :D_MODEL] * _dot(onsa_ref[...], wpa_ref[...])
             + mg[:, D_MODEL:] * _dot(o_sgu, wpb_ref[...]))
    h_ref[...] = x + _dot(mixed.astype(BF16), wo_ref[...])


def _merge(x2, onsa2, u2, vn2, norm_g, wm, bm, ws, bs_t, wpa, wpb, wo, tm):
    n = x2.shape[0]
    row = lambda w: pl.BlockSpec((tm, w), lambda i: (i, 0))
    full = lambda a: pl.BlockSpec(a.shape, lambda i: (0,) * a.ndim)
    return pl.pallas_call(
        functools.partial(_merge_kernel, tm=tm),
        grid=(n // tm,),
        in_specs=[row(D_MODEL), row(NSA_W), row(GMLP_WIDTH), row(GMLP_WIDTH), full(norm_g),
                  full(wm), full(bm), full(ws), full(bs_t), full(wpa), full(wpb), full(wo)],
        out_specs=row(D_MODEL),
        out_shape=jax.ShapeDtypeStruct((n, D_MODEL), F32),
        compiler_params=pltpu.CompilerParams(dimension_semantics=("arbitrary",),
                                             vmem_limit_bytes=VMEM_LIMIT),
        name="merge",
    )(x2, onsa2, u2, vn2, norm_g, wm, bm, ws, bs_t, wpa, wpb, wo)


def _memkv_kernel(mem_ref, g_ref, w_ref, out_ref):
    out_ref[0] = _dot(_rms(mem_ref[0], g_ref[...]).astype(BF16), w_ref[...]).astype(BF16)


def _memkv(mem, norm_g, w):
    bsz, nm, _ = mem.shape
    return pl.pallas_call(
        _memkv_kernel,
        grid=(bsz,),
        in_specs=[pl.BlockSpec((1, nm, D_MODEL), lambda b: (b, 0, 0)),
                  pl.BlockSpec(norm_g.shape, lambda b: (0, 0)),
                  pl.BlockSpec(w.shape, lambda b: (0, 0))],
        out_specs=pl.BlockSpec((1, nm, 2 * MEM_W), lambda b: (b, 0, 0)),
        out_shape=jax.ShapeDtypeStruct((bsz, nm, 2 * MEM_W), BF16),
        compiler_params=pltpu.CompilerParams(dimension_semantics=("arbitrary",),
                                             vmem_limit_bytes=VMEM_LIMIT),
        name="memkv",
    )(mem, norm_g, w)


def _xattn_kernel(h_ref, g_ref, wq_ref, mkv_ref, wo_ref, out_ref, *, tm):
    halves = [slice(0, tm // 2), slice(tm // 2, tm)]
    head_lanes = [slice(MEM_HEAD_DIM * a, MEM_HEAD_DIM * (a + 1)) for a in range(MEM_HEADS)]
    hs = [h_ref[rows, :] for rows in halves]
    hqs = [(_dot(_rms(h, g_ref[...]).astype(BF16), wq_ref[...]) * (MEM_HEAD_DIM ** -0.5 * LOG2E)).astype(BF16)
           for h in hs]
    ss = [[_dot_nt(hq[:, lanes], mkv_ref[0, :, lanes]) for lanes in head_lanes] for hq in hqs]
    es = [[jnp.exp2(s - jnp.max(s, axis=-1, keepdims=True)) for s in s_half] for s_half in ss]
    os = []
    for e_half in es:
        heads = []
        for a, e in enumerate(e_half):
            v = mkv_ref[0, :, MEM_W + MEM_HEAD_DIM * a:MEM_W + MEM_HEAD_DIM * (a + 1)]
            heads.append(_dot(e.astype(BF16), v) * (1.0 / jnp.sum(e, axis=-1, keepdims=True)))
        os.append(jnp.concatenate(heads, axis=1).astype(BF16))
    for rows, h, o in zip(halves, hs, os):
        out_ref[rows, :] = h + _dot(o, wo_ref[...])


def _xattn(h2d, norm_g, wq, mkv, wo, tm, rows_per_batch):
    n = h2d.shape[0]
    nm = mkv.shape[1]
    tiles_per_batch = rows_per_batch // tm
    full = lambda a: pl.BlockSpec(a.shape, lambda i: (0,) * a.ndim)
    return pl.pallas_call(
        functools.partial(_xattn_kernel, tm=tm),
        grid=(n // tm,),
        in_specs=[pl.BlockSpec((tm, D_MODEL), lambda i: (i, 0)), full(norm_g), full(wq),
                  pl.BlockSpec((1, nm, 2 * MEM_W), lambda i: (i // tiles_per_batch, 0, 0)),
                  full(wo)],
        out_specs=pl.BlockSpec((tm, D_MODEL), lambda i: (i, 0)),
        out_shape=jax.ShapeDtypeStruct((n, D_MODEL), F32),
        compiler_params=pltpu.CompilerParams(dimension_semantics=("arbitrary",),
                                             vmem_limit_bytes=VMEM_LIMIT),
        name="xattn",
    )(h2d, norm_g, wq, mkv, wo)


def _ffn_kernel(h_ref, g_ref, wgu_ref, wd_ref, gf_ref, out_ref, *, d_ff):
    h = h_ref[...]
    hn = _rms(h, g_ref[...]).astype(BF16)
    gate = _dot(hn, wgu_ref[:, :d_ff])
    up = _dot(hn, wgu_ref[:, d_ff:])
    act = (jax.nn.silu(gate) * up).astype(BF16)
    y = h + _dot(act, wd_ref[...])
    out_ref[...] = _rms(y, gf_ref[...])


def _ffn(h2d, norm_g, wgu, wd, norm_f, tm):
    n = h2d.shape[0]
    d_ff = wd.shape[0]
    full = lambda a: pl.BlockSpec(a.shape, lambda i: (0,) * a.ndim)
    once = lambda a: pl.BlockSpec(a.shape, lambda i: (0,) * a.ndim, pipeline_mode=pl.Buffered(1))
    return pl.pallas_call(
        functools.partial(_ffn_kernel, d_ff=d_ff),
        grid=(n // tm,),
        in_specs=[pl.BlockSpec((tm, D_MODEL), lambda i: (i, 0)), full(norm_g), once(wgu), once(wd),
                  full(norm_f)],
        out_specs=pl.BlockSpec((tm, D_MODEL), lambda i: (i, 0)),
        out_shape=jax.ShapeDtypeStruct((n, D_MODEL), F32),
        compiler_params=pltpu.CompilerParams(dimension_semantics=("arbitrary",),
                                             vmem_limit_bytes=VMEM_LIMIT),
        name="ffn",
    )(h2d, norm_g, wgu, wd, norm_f)


def _block_diag2(w):
    z = jnp.zeros_like(w)
    return jnp.concatenate([jnp.concatenate([w, z], axis=-1), jnp.concatenate([z, w], axis=-1)], axis=-2)


def _chunk_lists(flags):
    bsz = flags.shape[0]
    n_chunks = N_KEY_CHUNKS
    per_step = NSA_QB // Q_BLOCK
    nqb = flags.shape[1] // per_step
    f = flags[:, :, :, 0, :].reshape(bsz, nqb, per_step, N_GROUPS, n_chunks, 2).max(axis=(2, 5))
    cid = jnp.arange(n_chunks, dtype=jnp.int32)
    own = (NSA_QB // KEY_CHUNK) * jnp.arange(nqb, dtype=jnp.int32)[None, :, None, None]
    active = (f > 0) & (cid < own)
    n_active = active.sum(axis=-1)
    slot = jnp.cumsum(active, axis=-1) - 1
    hit = active[..., :, None] & (slot[..., :, None] == cid)
    ids = jnp.sum(jnp.where(hit, cid[:, None], 0), axis=-2)
    ids = jnp.where(cid < n_active[..., None], ids, VOID_CHUNK)
    words, n_batches = [], []
    for g, batch in enumerate(SLC_BATCH):
        n_batches.append((n_active[:, :, g] + batch - 1) // batch)
        padded = jnp.pad(ids[:, :, g], ((0, 0), (0, 0), (0, LIST_WORDS * batch - n_chunks)),
                         constant_values=VOID_CHUNK).reshape(bsz, nqb, LIST_WORDS, batch)
        words.append(functools.reduce(jnp.bitwise_or,
                                      [padded[..., u] << (CHUNK_ID_BITS * u) for u in range(batch)]))
    n_batches = functools.reduce(jnp.maximum, n_batches)
    words = jnp.stack(words, axis=2)
    return n_batches.reshape(-1).astype(jnp.int32), words.reshape(-1).astype(jnp.int32)


def kernel(x, mem, norm_mix, w_in, w_cmp_k1, w_cmp_k2, w_cmp_v1, w_cmp_v2, pe_cmp_k, pe_cmp_v, ln_sgu, w_spatial, b_spatial, w_proj_a, w_proj_b, w_merge, b_merge, w_out, norm_mem_q, norm_mem_kv, w_mq, w_mkv, w_mo, norm_ffn, w_gate_up, w_down, norm_final):
    bsz, t, d = x.shape
    depth = norm_mix.shape[0]
    assert d == D_MODEL and t % Q_BLOCK == 0 and t // SEL_BLOCK <= N_BLK_PAD
    assert t // SEL_BLOCK >= N_SELECT and depth == 1 and t % 512 == 0
    n = bsz * t
    tm = 512
    h = x.reshape(n, d)
    c0, c1, c2, c3 = NSA_W, NSA_W + KV_W, NSA_W + 3 * KV_W, NSA_W + 3 * KV_W + 2 * GMLP_WIDTH
    onehot = (jnp.arange(t)[:, None] // SEL_BLOCK == jnp.arange(N_BLK_PAD)[None, :]).astype(BF16)
    pad_cols = lambda w, width: jnp.pad(w, ((0, 0), (0, width - w.shape[1])))
    for l in range(depth):
        wi = w_in[l]
        wk, wv = [], []
        for branch in range(2):
            base = c1 + KV_W * branch
            for g in range(N_GROUPS):
                wk.append(pad_cols(wi[:, base + HEAD_DIM * g:base + HEAD_DIM * (g + 1)], LANES))
                v0 = base + N_GROUPS * HEAD_DIM + HEAD_DIM * g
                wv.append(pad_cols(wi[:, v0:v0 + HEAD_DIM], VT_ROWS))
        wrow = jnp.concatenate([wi[:, c0:c1]] + wk + [wi[:, c2:c3]], axis=1).astype(BF16)
        wnt = jnp.concatenate([wi[:, :c0]] + wv + [pad_cols(wi[:, c3:], GATET_ROWS)], axis=1).T.astype(BF16)
        qt, kvc2, kaug, vt, u2, vn2, gatest = _inproj(
            h, norm_mix[l][None], ln_sgu[l][None], wrow, wnt, tm, bsz, t)

        pe2 = jnp.stack([pe_cmp_k[l], pe_cmp_v[l]])
        pe2 = jnp.concatenate([pe2, pe2], axis=-1)
        w1 = jnp.stack([w_cmp_k1[l], w_cmp_v1[l]]).reshape(2, CMP_BLOCK, HEAD_DIM, CMP_HIDDEN)
        w1bd = _block_diag2(w1).astype(BF16)
        w2k = _block_diag2(w_cmp_k2[l]).astype(BF16)
        w2vt = _block_diag2(w_cmp_v2[l]).T.astype(BF16)
        kc, vct = _compress(kvc2.reshape(bsz, t, KV_W), pe2, w1bd, w2k, w2vt)
        ocmp, selt, flags = _cmp_topk(qt, kc, vct, gatest)
        counts, lists = _chunk_lists(flags)
        onsa = _slc_win(counts, lists, qt, kaug, vt, onehot, selt, gatest, ocmp)

        h = _merge(h, onsa.reshape(n, NSA_W), u2, vn2, norm_mix[l][None], w_merge[l].astype(BF16),
                   b_merge[l][None], w_spatial[l], b_spatial[l].T, w_proj_a[l].astype(BF16),
                   w_proj_b[l].astype(BF16), w_out[l].astype(BF16), tm)

        mkv = _memkv(mem, norm_mem_kv[l][None], w_mkv[l].astype(BF16))
        h = _xattn(h, norm_mem_q[l][None], w_mq[l].astype(BF16), mkv, w_mo[l].astype(BF16), tm, t)
        h = _ffn(h, norm_ffn[l][None], w_gate_up[l].astype(BF16), w_down[l].astype(BF16),
                 norm_final[None], tm)
    return h.reshape(bsz, t, d)
```

```python
import functools

import jax
import jax.numpy as jnp
from jax import lax
from jax.experimental import pallas as pl
from jax.experimental.pallas import tpu as pltpu

F32 = jnp.float32
BF16 = jnp.bfloat16

LANES = 128
SUBLANES = 8
D_MODEL = 1024
N_HEADS = 8
HEAD_DIM = 64
N_GROUPS = 2
HPG = N_HEADS // N_GROUPS
CMP_BLOCK = 32
CMP_STRIDE = 16
CMP_HIDDEN = 128
SEL_BLOCK = 64
N_SELECT = 16
WINDOW = 512
Q_BLOCK = 256
KEY_CHUNK = 128
N_BLK_PAD = 128
NSA_W = N_HEADS * HEAD_DIM
KV_W = 2 * N_GROUPS * HEAD_DIM
GMLP_WIDTH = 512
GMLP_GROUPS = 4
GMLP_CHUNK = 128
MEM_HEADS = 4
MEM_HEAD_DIM = 128
MEM_W = MEM_HEADS * MEM_HEAD_DIM
GATE_W = 3 * N_HEADS
NSA_QB = 256
SLC_BATCH = (4, 5)
N_KEY_CHUNKS = N_BLK_PAD // 2
CHUNK_ID_BITS = 6
VOID_CHUNK = N_KEY_CHUNKS - 1
LIST_WORDS = -(-N_KEY_CHUNKS // min(SLC_BATCH))
VT_ROWS = 80
VCT_ROWS = VT_ROWS + N_BLK_PAD
CMP_KEY_CHUNK = 128
N_FORCED = 3
TOPK_ROW_STEP = 64
CMP_BLOCKS = 2
GATET_ROWS = 32
MASK_BIG = 1e30
EPS = 1e-6
NEG = -1e30
REMOVED = -3e38
FORCE_SCORE = 1e6
SLOPES = tuple(2.0 ** (-8.0 * (h + 1) / N_HEADS) for h in range(N_HEADS))
LOG2E = 1.4426950408889634
Q_SCALE = HEAD_DIM ** -0.5 * LOG2E
VMEM_LIMIT = 56 * 1024 * 1024


def _dot(a, b):
    return jnp.dot(a, b, preferred_element_type=F32)


def _dot_nt(a, b):
    return lax.dot_general(a, b, (((1,), (1,)), ((), ())), preferred_element_type=F32)


def _rms(x, g):
    return x * lax.rsqrt(jnp.mean(x * x, axis=-1, keepdims=True) + EPS) * g


def _iota(shape, dim):
    return lax.broadcasted_iota(jnp.int32, shape, dim)


def _slope_row(g, nq):
    return jnp.concatenate(
        [jnp.full((1, nq), SLOPES[HPG * g + j] * LOG2E, F32) for j in range(HPG)], axis=1)


def _slope_feature_rows(slope_row, n_rows):
    hi = slope_row.astype(BF16).astype(F32)
    r = _iota((n_rows, slope_row.shape[1]), 0)
    return jnp.where(r == 0, hi, jnp.where(r == 1, slope_row - hi, 0.0)).astype(BF16)


def _chunk_slabs(st, chunk):
    return [st[chunk * u:chunk * (u + 1)] for u in range(st.shape[0] // chunk)]


def _col_max(st, dls, chunk):
    parts = [s.reshape(chunk // SUBLANES, SUBLANES, s.shape[1]).max(axis=0) - dl
             for s, dl in zip(_chunk_slabs(st, chunk), dls)]
    return functools.reduce(jnp.maximum, parts).max(axis=0, keepdims=True)


def _probs(st, dls, m, chunk):
    return jnp.concatenate([jnp.exp2(s - (m + dl)).astype(BF16)
                            for s, dl in zip(_chunk_slabs(st, chunk), dls)], axis=0)


def _inproj_kernel(x_ref, g_ref, lng_ref, wrow_ref, wnt_ref,
                   qt_ref, kvc_ref, kaug_ref, vt_ref, u_ref, vn_ref, gatest_ref, *, tm):
    xn = _rms(x_ref[...], g_ref[...]).astype(BF16)
    r = _dot(xn, wrow_ref[...])
    nt = _dot_nt(wnt_ref[...], xn)
    qt_ref[0] = (nt[0:NSA_W] * Q_SCALE).astype(BF16)
    kvc_ref[...] = r[:, 0:KV_W]
    lane = _iota((tm, LANES), 1)
    key_feat = jnp.where((lane == HEAD_DIM) | (lane == HEAD_DIM + 1),
                         _iota((tm, LANES), 0) & (KEY_CHUNK - 1), 0).astype(F32)
    ones_row = jnp.where(_iota((VT_ROWS, tm), 0) == HEAD_DIM, 1.0, 0.0)
    for a in range(2 * N_GROUPS):
        kaug_ref[0, a] = (r[:, KV_W + LANES * a:KV_W + LANES * (a + 1)] + key_feat).astype(BF16)
        vt_ref[0, a] = (nt[NSA_W + VT_ROWS * a:NSA_W + VT_ROWS * (a + 1)] + ones_row).astype(BF16)
    uv = jax.nn.gelu(r[:, KV_W + 2 * N_GROUPS * LANES:])
    u_ref[...] = uv[:, :GMLP_WIDTH]
    v = uv[:, GMLP_WIDTH:]
    vc = v - jnp.mean(v, axis=-1, keepdims=True)
    vn = vc * lax.rsqrt(jnp.mean(vc * vc, axis=-1, keepdims=True) + EPS) * lng_ref[...]
    vn_ref[...] = vn.astype(BF16)
    gatest_ref[0] = jax.nn.sigmoid(nt[NSA_W + 2 * N_GROUPS * VT_ROWS:])


def _inproj(x2, norm_g, ln_g, wrow, wnt, tm, bsz, t):
    n = x2.shape[0]
    tpb = t // tm
    row = lambda w: pl.BlockSpec((tm, w), lambda i: (i, 0))
    full = lambda a: pl.BlockSpec(a.shape, lambda i: (0,) * a.ndim)
    return pl.pallas_call(
        functools.partial(_inproj_kernel, tm=tm),
        grid=(n // tm,),
        in_specs=[row(D_MODEL), full(norm_g), full(ln_g), full(wrow), full(wnt)],
        out_specs=[pl.BlockSpec((1, NSA_W, tm), lambda i: (i // tpb, 0, i % tpb)),
                   row(KV_W),
                   pl.BlockSpec((1, 2 * N_GROUPS, tm, LANES), lambda i: (i // tpb, 0, i % tpb, 0)),
                   pl.BlockSpec((1, 2 * N_GROUPS, VT_ROWS, tm), lambda i: (i // tpb, 0, 0, i % tpb)),
                   row(GMLP_WIDTH), row(GMLP_WIDTH),
                   pl.BlockSpec((1, GATET_ROWS, tm), lambda i: (i // tpb, 0, i % tpb))],
        out_shape=[jax.ShapeDtypeStruct((bsz, NSA_W, t), BF16),
                   jax.ShapeDtypeStruct((n, KV_W), F32),
                   jax.ShapeDtypeStruct((bsz, 2 * N_GROUPS, t, LANES), BF16),
                   jax.ShapeDtypeStruct((bsz, 2 * N_GROUPS, VT_ROWS, t), BF16),
                   jax.ShapeDtypeStruct((n, GMLP_WIDTH), F32),
                   jax.ShapeDtypeStruct((n, GMLP_WIDTH), BF16),
                   jax.ShapeDtypeStruct((bsz, GATET_ROWS, t), F32)],
        compiler_params=pltpu.CompilerParams(dimension_semantics=("arbitrary",),
                                             vmem_limit_bytes=VMEM_LIMIT),
        name="inproj",
    )(x2, norm_g, ln_g, wrow, wnt)


def _compress_kernel(xk_ref, xv_ref, pe_ref, w1_ref, w2k_ref, w2vt_ref, kc_ref, vct_ref, *, nc):
    outs = []
    for j, x_ref in enumerate((xk_ref, xv_ref)):
        a = jnp.zeros((nc, 2 * CMP_HIDDEN), F32)
        b = jnp.zeros((nc, 2 * CMP_HIDDEN), F32)
        for t in range(CMP_STRIDE):
            xt = x_ref[0, pl.ds(t, nc, stride=CMP_STRIDE), :]
            a = a + _dot((xt + pe_ref[j, t:t + 1, :]).astype(BF16), w1_ref[j, t])
            b = b + _dot((xt + pe_ref[j, CMP_STRIDE + t:CMP_STRIDE + t + 1, :]).astype(BF16),
                         w1_ref[j, CMP_STRIDE + t])
        outs.append(jax.nn.gelu(a + pltpu.roll(b, nc - 1, 0)).astype(BF16))
    k2 = _dot(outs[0], w2k_ref[...])
    v_t = _dot_nt(w2vt_ref[...], outs[1])
    lane = _iota((nc, LANES), 1)
    key_feat = jnp.where((lane == HEAD_DIM) | (lane == HEAD_DIM + 1),
                         CMP_STRIDE * (_iota((nc, LANES), 0) & (CMP_KEY_CHUNK - 1)), 0).astype(F32)
    ci =_iota((N_BLK_PAD, nc), 1) * CMP_STRIDE
    sj = _iota((N_BLK_PAD, nc), 0) * SEL_BLOCK
    overlap_t = jnp.where((ci < sj + SEL_BLOCK) & (ci + (CMP_BLOCK - 1) >= sj), 1.0, 0.0).astype(BF16)
    ones_rows = jnp.where(_iota((VT_ROWS - HEAD_DIM, nc), 0) == 0, 1.0, 0.0).astype(BF16)
    for g in range(N_GROUPS):
        kg = k2 if g == 0 else pltpu.roll(k2, HEAD_DIM, 1)
        kc_ref[0, g] = jnp.where(lane < HEAD_DIM, kg, key_feat).astype(BF16)
        vct_ref[0, g, 0:HEAD_DIM, :] = v_t[HEAD_DIM * g:HEAD_DIM * (g + 1), :].astype(BF16)
        vct_ref[0, g, HEAD_DIM:VT_ROWS, :] = ones_rows
        vct_ref[0, g, VT_ROWS:VCT_ROWS, :] = overlap_t


def _compress(kvc3, pe2, w1bd, w2k, w2vt):
    bsz, t, _ = kvc3.shape
    nc = t // CMP_STRIDE
    full = lambda a: pl.BlockSpec(a.shape, lambda b: (0,) * a.ndim)
    return pl.pallas_call(
        functools.partial(_compress_kernel, nc=nc),
        grid=(bsz,),
        in_specs=[pl.BlockSpec((1, t, LANES), lambda b: (b, 0, 0)),
                  pl.BlockSpec((1, t, LANES), lambda b: (b, 0, 1)),
                  full(pe2), full(w1bd), full(w2k), full(w2vt)],
        out_specs=[pl.BlockSpec((1, N_GROUPS, nc, LANES), lambda b: (b, 0, 0, 0)),
                   pl.BlockSpec((1, N_GROUPS, VCT_ROWS, nc), lambda b: (b, 0, 0, 0))],
        out_shape=[jax.ShapeDtypeStruct((bsz, N_GROUPS, nc, LANES), BF16),
                   jax.ShapeDtypeStruct((bsz, N_GROUPS, VCT_ROWS, nc), BF16)],
        compiler_params=pltpu.CompilerParams(dimension_semantics=("arbitrary",),
                                             vmem_limit_bytes=VMEM_LIMIT),
        name="compress",
    )(kvc3, kvc3, pe2, w1bd, w2k, w2vt)


def _cmp_topk_kernel(qt_ref, kc_ref, vct_ref, gatest_ref, ocmp_ref, selt_ref, flags_ref,
                     m_ref, acc_ref, *, ncp):
    step = pl.program_id(1)
    n_chunks = ncp // CMP_KEY_CHUNK
    chunk_tokens = CMP_KEY_CHUNK * CMP_STRIDE
    tile_heads = lambda a: jnp.concatenate([a] * HPG, axis=1)
    tail_chunks = min(2, n_chunks)
    head_chunks = n_chunks - tail_chunks
    tail_keys = tail_chunks * CMP_KEY_CHUNK
    blocks = range(CMP_BLOCKS)
    groups = range(N_GROUPS)
    items = [(bi, g) for bi in blocks for g in groups]
    start = [(step * CMP_BLOCKS + bi) * Q_BLOCK for bi in blocks]
    qlanes = [slice(Q_BLOCK * bi, Q_BLOCK * (bi + 1)) for bi in blocks]
    nck = [(start[bi] + Q_BLOCK - CMP_BLOCK) // CMP_STRIDE // CMP_KEY_CHUNK + 1 for bi in blocks]
    tail_c0 = [jnp.maximum(nck[bi] - tail_chunks, 0) for bi in blocks]
    tail_rows = [pl.ds(pl.multiple_of(tail_c0[bi] * CMP_KEY_CHUNK, CMP_KEY_CHUNK), tail_keys)
                 for bi in blocks]
    gt = gatest_ref[0]
    slope_rows = [_slope_row(g, Q_BLOCK) for g in groups]
    qas = {(bi, g): jnp.concatenate(
        [jnp.concatenate([qt_ref[0, HEAD_DIM * (HPG * g + j):HEAD_DIM * (HPG * g + j + 1), qlanes[bi]]
                          for j in range(HPG)], axis=1),
         _slope_feature_rows(slope_rows[g], LANES - HEAD_DIM)], axis=0) for bi, g in items}

    def delta(bi, g, c):
        return slope_rows[g] * (start[bi] - c * chunk_tokens).astype(F32)

    for bi in blocks:
        if head_chunks > 0:
            @pl.when(nck[bi] > tail_chunks)
            def _():
                sts = [_dot(kc_ref[0, g, 0:head_chunks * CMP_KEY_CHUNK, :], qas[bi, g]) for g in groups]
                dls = [[delta(bi, g, c) + jnp.where(c < nck[bi] - tail_chunks, 0.0, MASK_BIG)
                        for c in range(head_chunks)] for g in groups]
                ms = [_col_max(sts[g], dls[g], CMP_KEY_CHUNK) for g in groups]
                ps = [_probs(sts[g], dls[g], ms[g], CMP_KEY_CHUNK) for g in groups]
                for g in groups:
                    m_ref[bi, g] = ms[g]
                    acc_ref[bi, g] = _dot(vct_ref[0, g, :, 0:head_chunks * CMP_KEY_CHUNK], ps[g])

            @pl.when(nck[bi] <= tail_chunks)
            def _():
                m_ref[bi] = jnp.full(m_ref.shape[1:], NEG, F32)
                acc_ref[bi] = jnp.zeros(acc_ref.shape[1:], F32)

    def tail_and_select(n_rows):
        if head_chunks > 0:
            m_old = {it: m_ref[it[0], it[1]] for it in items}
            acc_old = {it: acc_ref[it[0], it[1]] for it in items}
        else:
            m_old = {it: jnp.full((1, HPG * Q_BLOCK), NEG, F32) for it in items}
            acc_old = {it: jnp.zeros((VCT_ROWS, HPG * Q_BLOCK), F32) for it in items}
        key_row = _iota((tail_keys, Q_BLOCK), 0)
        q_lane = _iota((tail_keys, Q_BLOCK), 1)
        tail_bias = []
        for bi in blocks:
            key_end = CMP_STRIDE * (tail_c0[bi] * CMP_KEY_CHUNK + key_row) + (CMP_BLOCK - 1) - start[bi]
            tail_bias.append(tile_heads(jnp.where(key_end <= q_lane, 0.0, NEG)))
        sts = {(bi, g): _dot(kc_ref[0, g, tail_rows[bi], :], qas[bi, g]) + tail_bias[bi] for bi, g in items}
        dls = {(bi, g): [delta(bi, g, tail_c0[bi] + u) for u in range(tail_chunks)] for bi, g in items}
        m_new = {it: jnp.maximum(m_old[it], _col_max(sts[it], dls[it], CMP_KEY_CHUNK)) for it in items}
        ps = {it: _probs(sts[it], dls[it], m_new[it], CMP_KEY_CHUNK) for it in items}
        accs = {(bi, g): jnp.exp2(m_old[bi, g] - m_new[bi, g]) * acc_old[bi, g]
                + _dot(vct_ref[0, g, :, tail_rows[bi]], ps[bi, g]) for bi, g in items}

        blk_n = _iota((n_rows, Q_BLOCK), 0)
        bf = blk_n.astype(F32)
        rk, cur = {}, []
        for bi in blocks:
            t_row = start[bi] + _iota((1, Q_BLOCK), 1)
            cur.append(lax.shift_right_logical(t_row, SEL_BLOCK.bit_length() - 1))
            forced = (blk_n == 0) | (blk_n == cur[bi]) | (blk_n == cur[bi] - 1)
            has_key = t_row >= CMP_BLOCK - 1
            per_head = []
            for g in groups:
                acc = accs[bi, g]
                inv_l = 1.0 / jnp.maximum(acc[HEAD_DIM:HEAD_DIM + 1], 1e-30)
                o = acc[0:HEAD_DIM] * inv_l
                imp_h = acc[VT_ROWS:VT_ROWS + n_rows] * inv_l
                imp = sum(imp_h[:, Q_BLOCK * j:Q_BLOCK * (j + 1)] for j in range(HPG))
                imp = jnp.where(has_key, imp, 0.0)
                for j in range(HPG):
                    h = HPG * g + j
                    per_head.append(jnp.where(
                        has_key, gt[3 * h:3 * h + 1, qlanes[bi]] * o[:, Q_BLOCK * j:Q_BLOCK * (j + 1)], 0.0))
                rk[bi, g] = jnp.where(blk_n <= cur[bi], jnp.where(forced, REMOVED, imp), NEG)
            ocmp_ref[0, :, qlanes[bi]] = jnp.concatenate(per_head, axis=0)

        for _ in range(N_SELECT - N_FORCED):
            for it in items:
                m = jnp.max(rk[it], axis=0, keepdims=True)
                idx = jnp.min(jnp.where(rk[it] == m, bf, float(N_BLK_PAD)), axis=0, keepdims=True)
                rk[it] = jnp.where(bf == idx, REMOVED, rk[it])
        ones8 = jnp.ones((SUBLANES, Q_BLOCK), F32)
        for bi, g in items:
            sel = jnp.where((blk_n <= cur[bi]) & (rk[bi, g] < 2.0 * NEG), 1.0, 0.0)
            selt_ref[0, g, 0:n_rows, qlanes[bi]] = sel.astype(BF16)
            flag = (_dot_nt(ones8, sel) > 0.5).astype(jnp.int32)
            if n_rows < N_BLK_PAD:
                selt_ref[0, g, n_rows:N_BLK_PAD, qlanes[bi]] = jnp.zeros((N_BLK_PAD - n_rows, Q_BLOCK), BF16)
                flag = jnp.concatenate([flag, jnp.zeros((SUBLANES, N_BLK_PAD - n_rows), jnp.int32)], axis=1)
            flags_ref[0, bi, g] = flag

    causal_rows = (start[-1] + Q_BLOCK) // SEL_BLOCK
    for n_rows in range(TOPK_ROW_STEP, N_BLK_PAD + 1, TOPK_ROW_STEP):
        @pl.when((causal_rows > n_rows - TOPK_ROW_STEP) & (causal_rows <= n_rows))
        def _():
            tail_and_select(n_rows)


def _cmp_topk(qt, kc, vct, gatest):
    bsz, _, t = qt.shape
    ncp = t // CMP_STRIDE
    nqb = t // Q_BLOCK
    step_q = CMP_BLOCKS * Q_BLOCK
    return pl.pallas_call(
        functools.partial(_cmp_topk_kernel, ncp=ncp),
        grid=(bsz, nqb // CMP_BLOCKS),
        in_specs=[pl.BlockSpec((1, NSA_W, step_q), lambda b, i: (b, 0, i)),
                  pl.BlockSpec((1, N_GROUPS, ncp, LANES), lambda b, i: (b, 0, 0, 0)),
                  pl.BlockSpec((1, N_GROUPS, VCT_ROWS, ncp), lambda b, i: (b, 0, 0, 0)),
                  pl.BlockSpec((1, GATET_ROWS, step_q), lambda b, i: (b, 0, i))],
        out_specs=[pl.BlockSpec((1, NSA_W, step_q), lambda b, i: (b, 0, i)),
                   pl.BlockSpec((1, N_GROUPS, N_BLK_PAD, step_q), lambda b, i: (b, 0, 0, i)),
                   pl.BlockSpec((1, CMP_BLOCKS, N_GROUPS, SUBLANES, N_BLK_PAD), lambda b, i: (b, i, 0, 0, 0))],
        out_shape=[jax.ShapeDtypeStruct((bsz, NSA_W, t), F32),
                   jax.ShapeDtypeStruct((bsz, N_GROUPS, N_BLK_PAD, t), BF16),
                   jax.ShapeDtypeStruct((bsz, nqb, N_GROUPS, SUBLANES, N_BLK_PAD), jnp.int32)],
        scratch_shapes=[pltpu.VMEM((CMP_BLOCKS, N_GROUPS, 1, HPG * Q_BLOCK), F32),
                        pltpu.VMEM((CMP_BLOCKS, N_GROUPS, VCT_ROWS, HPG * Q_BLOCK), F32)],
        compiler_params=pltpu.CompilerParams(dimension_semantics=("arbitrary", "arbitrary"),
                                             vmem_limit_bytes=VMEM_LIMIT),
        name="cmp_topk",
    )(qt, kc, vct, gatest)


def _slc_win_kernel(counts_ref, lists_ref, qt_ref, kaug_ref, vt_ref, oh_ref, selt_ref, gatest_ref,
                    ocmp_ref, out_ref, qaug_ref, m_ref, acc_ref, *, nqb):
    b = pl.program_id(0)
    qb = pl.program_id(1)
    step_id = b * nqb + qb
    start = qb * NSA_QB
    tile_heads = lambda a: jnp.concatenate([a] * HPG, axis=1)
    groups = range(N_GROUPS)
    gt = gatest_ref[0]
    slope_rows = [_slope_row(g, NSA_QB) for g in groups]

    def normalize(acc):
        return acc[0:HEAD_DIM] / jnp.maximum(acc[HEAD_DIM:HEAD_DIM + 1], 1e-30)

    def half_lanes(w):
        return [slice(NSA_QB * h + KEY_CHUNK * w, NSA_QB * h + KEY_CHUNK * (w + 1)) for h in range(HPG)]

    def half(a, w):
        return jnp.concatenate([a[:, s] for s in half_lanes(w)], axis=1)

    def unhalf(lo, hi):
        return jnp.concatenate([x[:, KEY_CHUNK * h:KEY_CHUNK * (h + 1)]
                                for h in range(HPG) for x in (lo, hi)], axis=1)

    def chunk_at(rel):
        pos = start + rel * KEY_CHUNK
        rows = pl.ds(pl.multiple_of(jnp.maximum(pos, 0), KEY_CHUNK), KEY_CHUNK)
        return rows, (jnp.where(pos >= 0, 0.0, MASK_BIG) if rel < 0 else 0.0)

    ki = _iota((KEY_CHUNK, KEY_CHUNK), 0)
    qi = _iota((KEY_CHUNK, KEY_CHUNK), 1)
    upper_bias = tile_heads(jnp.where(ki > qi, 0.0, NEG))
    lower_bias = tile_heads(jnp.where(ki <= qi, 0.0, NEG))
    n_mid = WINDOW // KEY_CHUNK
    mid_dist = (_iota((n_mid * KEY_CHUNK, NSA_QB), 1) + (n_mid - 1) * KEY_CHUNK
                - _iota((n_mid * KEY_CHUNK, NSA_QB), 0))
    mid_bias = tile_heads(jnp.where((mid_dist >= 0) & (mid_dist < WINDOW), 0.0, NEG))
    own_bias = tile_heads(jnp.where(_iota((KEY_CHUNK, NSA_QB), 0) <= _iota((KEY_CHUNK, NSA_QB), 1), 0.0, NEG))

    for g in range(N_GROUPS):
        qaug_ref[g, 0:HEAD_DIM, :] = jnp.concatenate(
            [qt_ref[0, HEAD_DIM * (HPG * g + j):HEAD_DIM * (HPG * g + j + 1), :] for j in range(HPG)],
            axis=1)
        qaug_ref[g, HEAD_DIM:LANES, :] = _slope_feature_rows(slope_rows[g], LANES - HEAD_DIM)
        sel_bias = ((selt_ref[0, g].astype(F32) - 1.0) * MASK_BIG).astype(BF16)
        qaug_ref[g, LANES:2 * LANES, :] = tile_heads(sel_bias)

    mid = [chunk_at(r) for r in range(1 - n_mid, 1)]
    lo_rows, lo_kill = chunk_at(-n_mid)
    hi_rows, _ = chunk_at(1)
    own_rows, _ = chunk_at(0)
    kw = lambda g: kaug_ref.at[0, N_GROUPS + g]
    vw = lambda g: vt_ref.at[0, N_GROUPS + g]
    slc_keys = lambda g, rows: jnp.concatenate([kaug_ref[0, g, rows, :], oh_ref[rows, :]], axis=1)
    q_half = lambda g, w, nrow: jnp.concatenate([qaug_ref[g, 0:nrow, s] for s in half_lanes(w)], axis=1)

    wmid_st = [_dot(jnp.concatenate([kw(g)[rows, :] for rows, _ in mid], axis=0), qaug_ref[g, 0:LANES, :])
               + mid_bias for g in groups]
    wlo_st = [_dot(kw(g)[lo_rows, :], q_half(g, 0, LANES)) + upper_bias for g in groups]
    whi_st = [_dot(kw(g)[hi_rows, :], q_half(g, 1, LANES)) + lower_bias for g in groups]
    own_st = [_dot(slc_keys(g, own_rows), qaug_ref[g]) + own_bias for g in groups]
    ohi_st = [_dot(slc_keys(g, hi_rows), q_half(g, 1, 2 * LANES)) + lower_bias for g in groups]

    wmid_dls = [[slope_rows[g] * float(-KEY_CHUNK * r) + kill
                 for r, (_, kill) in zip(range(1 - n_mid, 1), mid)] for g in groups]
    wlo_dl = [half(slope_rows[g], 0) * float(KEY_CHUNK * n_mid) + lo_kill for g in groups]
    hi_dl = [half(slope_rows[g], 1) * float(-KEY_CHUNK) for g in groups]
    neg_half = jnp.full((1, HPG * KEY_CHUNK), NEG, F32)
    win_m = [jnp.maximum(_col_max(wmid_st[g], wmid_dls[g], KEY_CHUNK),
                         unhalf(_col_max(wlo_st[g], [wlo_dl[g]], KEY_CHUNK),
                                _col_max(whi_st[g], [hi_dl[g]], KEY_CHUNK))) for g in groups]
    own_m = [jnp.maximum(_col_max(own_st[g], [0.0], KEY_CHUNK),
                         unhalf(neg_half, _col_max(ohi_st[g], [hi_dl[g]], KEY_CHUNK))) for g in groups]
    wmid_p = [_probs(wmid_st[g], wmid_dls[g], win_m[g], KEY_CHUNK) for g in groups]
    wlo_p = [_probs(wlo_st[g], [wlo_dl[g]], half(win_m[g], 0), KEY_CHUNK) for g in groups]
    whi_p = [_probs(whi_st[g], [hi_dl[g]], half(win_m[g], 1), KEY_CHUNK) for g in groups]
    own_p = [_probs(own_st[g], [0.0], own_m[g], KEY_CHUNK) for g in groups]
    ohi_p = [_probs(ohi_st[g], [hi_dl[g]], half(own_m[g], 1), KEY_CHUNK) for g in groups]
    o_win = []
    for g in groups:
        acc = _dot(jnp.concatenate([vw(g)[:, rows] for rows, _ in mid], axis=1), wmid_p[g])
        acc = acc + unhalf(_dot(vw(g)[:, lo_rows], wlo_p[g]), _dot(vw(g)[:, hi_rows], whi_p[g]))
        o_win.append(normalize(acc))
    zero_half = jnp.zeros((VT_ROWS, HPG * KEY_CHUNK), F32)
    for g in groups:
        m_ref[g] = own_m[g]
        acc_ref[g] = (_dot(vt_ref[0, g, :, own_rows], own_p[g])
                      + unhalf(zero_half, _dot(vt_ref[0, g, :, hi_rows], ohi_p[g])))

    def slc_body(i, carry):
        sts, vss, dlss = [], [], []
        for g in range(N_GROUPS):
            word = lists_ref[(step_id * N_GROUPS + g) * LIST_WORDS + i]
            ks, vs, dls = [], [], []
            for u in range(SLC_BATCH[g]):
                cid = lax.shift_right_logical(word, CHUNK_ID_BITS * u) & VOID_CHUNK
                valid = cid < VOID_CHUNK
                c = jnp.where(valid, cid, 0)
                rows = pl.ds(pl.multiple_of(c * KEY_CHUNK, KEY_CHUNK), KEY_CHUNK)
                ks.append(jnp.concatenate([kaug_ref[0, g, rows, :], oh_ref[rows, :]], axis=1))
                vs.append(vt_ref[0, g, :, rows])
                dls.append(slope_rows[g] * (start - c * KEY_CHUNK).astype(F32)
                           + jnp.where(valid, 0.0, MASK_BIG))
            sts.append(_dot(jnp.concatenate(ks, axis=0), qaug_ref[g]))
            vss.append(jnp.concatenate(vs, axis=1))
            dlss.append(dls)
        m_old = [m_ref[g] for g in range(N_GROUPS)]
        m_new = [jnp.maximum(m_old[g], _col_max(sts[g], dlss[g], KEY_CHUNK)) for g in range(N_GROUPS)]
        ps = [_probs(sts[g], dlss[g], m_new[g], KEY_CHUNK) for g in range(N_GROUPS)]
        for g in range(N_GROUPS):
            acc_ref[g] = jnp.exp2(m_old[g] - m_new[g]) * acc_ref[g] + _dot(vss[g], ps[g])
            m_ref[g] = m_new[g]
        return carry

    lax.fori_loop(0, counts_ref[step_id], slc_body, 0)

    per_head = []
    for g in range(N_GROUPS):
        o_slc = normalize(acc_ref[g])
        for j in range(HPG):
            h = HPG * g + j
            lanes = slice(NSA_QB * j, NSA_QB * (j + 1))
            per_head.append(gt[3 * h + 1:3 * h + 2, :] * o_slc[:, lanes]
                            + gt[3 * h + 2:3 * h + 3, :] * o_win[g][:, lanes])
    o_t = jnp.concatenate(per_head, axis=0)
    out_ref[0] = (ocmp_ref[0] + o_t).T.astype(BF16)


def _slc_win(counts, lists, qt, kaug, vt, onehot, selt, gatest, ocmp):
    bsz, _, t = qt.shape
    nqb = t // NSA_QB
    once = lambda shape, imap: pl.BlockSpec(shape, imap, pipeline_mode=pl.Buffered(1))
    grid_spec = pltpu.PrefetchScalarGridSpec(
        num_scalar_prefetch=2,
        grid=(bsz, nqb),
        in_specs=[pl.BlockSpec((1, NSA_W, NSA_QB), lambda b, i, *_: (b, 0, i)),
                  once((1, 2 * N_GROUPS, t, LANES), lambda b, i, *_: (b, 0, 0, 0)),
                  once((1, 2 * N_GROUPS, VT_ROWS, t), lambda b, i, *_: (b, 0, 0, 0)),
                  once((t, N_BLK_PAD), lambda b, i, *_: (0, 0)),
                  pl.BlockSpec((1, N_GROUPS, N_BLK_PAD, NSA_QB), lambda b, i, *_: (b, 0, 0, i)),
                  pl.BlockSpec((1, GATET_ROWS, NSA_QB), lambda b, i, *_: (b, 0, i)),
                  pl.BlockSpec((1, NSA_W, NSA_QB), lambda b, i, *_: (b, 0, i))],
        out_specs=pl.BlockSpec((1, NSA_QB, NSA_W), lambda b, i, *_: (b, i, 0)),
        scratch_shapes=[pltpu.VMEM((N_GROUPS, 2 * LANES, HPG * NSA_QB), BF16),
                        pltpu.VMEM((N_GROUPS, 1, HPG * NSA_QB), F32),
                        pltpu.VMEM((N_GROUPS, VT_ROWS, HPG * NSA_QB), F32)],
    )
    return pl.pallas_call(
        functools.partial(_slc_win_kernel, nqb=nqb),
        grid_spec=grid_spec,
        out_shape=jax.ShapeDtypeStruct((bsz, t, NSA_W), BF16),
        compiler_params=pltpu.CompilerParams(dimension_semantics=("arbitrary", "arbitrary"),
                                             vmem_limit_bytes=VMEM_LIMIT),
        name="slc_win",
    )(counts, lists, qt, kaug, vt, onehot, selt, gatest, ocmp)


def _merge_kernel(x_ref, onsa_ref, u_ref, vn_ref, g_ref, wm_ref, bm_ref, ws_ref, bs_ref,
                  wpa_ref, wpb_ref, wo_ref, h_ref, *, tm):
    x = x_ref[...]
    xn = _rms(x, g_ref[...]).astype(BF16)
    tril = _iota((GMLP_CHUNK, GMLP_CHUNK), 0) >= _iota((GMLP_CHUNK, GMLP_CHUNK), 1)
    sgu_rows = []
    for c in range(tm // GMLP_CHUNK):
        rows = slice(GMLP_CHUNK * c, GMLP_CHUNK * (c + 1))
        cols = []
        for g in range(GMLP_GROUPS):
            lanes = slice(LANES * g, LANES * (g + 1))
            w = jnp.where(tril, ws_ref[g], 0.0).astype(BF16)
            cols.append(_dot(w, vn_ref[rows, lanes]) + bs_ref[:, g:g + 1])
        sgu_rows.append(u_ref[rows, :] * jnp.concatenate(cols, axis=1))
    o_sgu = jnp.concatenate(sgu_rows, axis=0).astype(BF16)
    mg = jax.nn.sigmoid(_dot(xn, wm_ref[...]) + bm_ref[...])
    mixed = (mg[:, :D_MODEL] * _dot(onsa_ref[...], wpa_ref[...])
             + mg[:, D_MODEL:] * _dot(o_sgu, wpb_ref[...]))
    h_ref[...] = x + _dot(mixed.astype(BF16), wo_ref[...])


def _merge(x2, onsa2, u2, vn2, norm_g, wm, bm, ws, bs_t, wpa, wpb, wo, tm):
    n = x2.shape[0]
    row = lambda w: pl.BlockSpec((tm, w), lambda i: (i, 0))
    full = lambda a: pl.BlockSpec(a.shape, lambda i: (0,) * a.ndim)
    return pl.pallas_call(
        functools.partial(_merge_kernel, tm=tm),
        grid=(n // tm,),
        in_specs=[row(D_MODEL), row(NSA_W), row(GMLP_WIDTH), row(GMLP_WIDTH), full(norm_g),
                  full(wm), full(bm), full(ws), full(bs_t), full(wpa), full(wpb), full(wo)],
        out_specs=row(D_MODEL),
        out_shape=jax.ShapeDtypeStruct((n, D_MODEL), F32),
        compiler_params=pltpu.CompilerParams(dimension_semantics=("arbitrary",),
                                             vmem_limit_bytes=VMEM_LIMIT),
        name="merge",
    )(x2, onsa2, u2, vn2, norm_g, wm, bm, ws, bs_t, wpa, wpb, wo)


def _memkv_kernel(mem_ref, g_ref, w_ref, out_ref):
    out_ref[0] = _dot(_rms(mem_ref[0], g_ref[...]).astype(BF16), w_ref[...]).astype(BF16)


def _memkv(mem, norm_g, w):
    bsz, nm, _ = mem.shape
    return pl.pallas_call(
        _memkv_kernel,
        grid=(bsz,),
        in_specs=[pl.BlockSpec((1, nm, D_MODEL), lambda b: (b, 0, 0)),
                  pl.BlockSpec(norm_g.shape, lambda b: (0, 0)),
                  pl.BlockSpec(w.shape, lambda b: (0, 0))],
        out_specs=pl.BlockSpec((1, nm, 2 * MEM_W), lambda b: (b, 0, 0)),
        out_shape=jax.ShapeDtypeStruct((bsz, nm, 2 * MEM_W), BF16),
        compiler_params=pltpu.CompilerParams(dimension_semantics=("arbitrary",),
                                             vmem_limit_bytes=VMEM_LIMIT),
        name="memkv",
    )(mem, norm_g, w)


def _xattn_kernel(h_ref, g_ref, wq_ref, mkv_ref, wo_ref, out_ref, *, tm):
    halves = [slice(0, tm // 2), slice(tm // 2, tm)]
    head_lanes = [slice(MEM_HEAD_DIM * a, MEM_HEAD_DIM * (a + 1)) for a in range(MEM_HEADS)]
    hs = [h_ref[rows, :] for rows in halves]
    hqs = [(_dot(_rms(h, g_ref[...]).astype(BF16), wq_ref[...]) * (MEM_HEAD_DIM ** -0.5 * LOG2E)).astype(BF16)
           for h in hs]
    ss = [[_dot_nt(hq[:, lanes], mkv_ref[0, :, lanes]) for lanes in head_lanes] for hq in hqs]
    es = [[jnp.exp2(s - jnp.max(s, axis=-1, keepdims=True)) for s in s_half] for s_half in ss]
    os = []
    for e_half in es:
        heads = []
        for a, e in enumerate(e_half):
            v = mkv_ref[0, :, MEM_W + MEM_HEAD_DIM * a:MEM_W + MEM_HEAD_DIM * (a + 1)]
            heads.append(_dot(e.astype(BF16), v) * (1.0 / jnp.sum(e, axis=-1, keepdims=True)))
        os.append(jnp.concatenate(heads, axis=1).astype(BF16))
    for rows, h, o in zip(halves, hs, os):
        out_ref[rows, :] = h + _dot(o, wo_ref[...])


def _xattn(h2d, norm_g, wq, mkv, wo, tm, rows_per_batch):
    n = h2d.shape[0]
    nm = mkv.shape[1]
    tiles_per_batch = rows_per_batch // tm
    full = lambda a: pl.BlockSpec(a.shape, lambda i: (0,) * a.ndim)
    return pl.pallas_call(
        functools.partial(_xattn_kernel, tm=tm),
        grid=(n // tm,),
        in_specs=[pl.BlockSpec((tm, D_MODEL), lambda i: (i, 0)), full(norm_g), full(wq),
                  pl.BlockSpec((1, nm, 2 * MEM_W), lambda i: (i // tiles_per_batch, 0, 0)),
                  full(wo)],
        out_specs=pl.BlockSpec((tm, D_MODEL), lambda i: (i, 0)),
        out_shape=jax.ShapeDtypeStruct((n, D_MODEL), F32),
        compiler_params=pltpu.CompilerParams(dimension_semantics=("arbitrary",),
                                             vmem_limit_bytes=VMEM_LIMIT),
        name="xattn",
    )(h2d, norm_g, wq, mkv, wo)


def _ffn_kernel(h_ref, g_ref, wgu_ref, wd_ref, gf_ref, out_ref, *, d_ff):
    h = h_ref[...]
    hn = _rms(h, g_ref[...]).astype(BF16)
    gate = _dot(hn, wgu_ref[:, :d_ff])
    up = _dot(hn, wgu_ref[:, d_ff:])
    act = (jax.nn.silu(gate) * up).astype(BF16)
    y = h + _dot(act, wd_ref[...])
    out_ref[...] = _rms(y, gf_ref[...])


def _ffn(h2d, norm_g, wgu, wd, norm_f, tm):
    n = h2d.shape[0]
    d_ff = wd.shape[0]
    full = lambda a: pl.BlockSpec(a.shape, lambda i: (0,) * a.ndim)
    once = lambda a: pl.BlockSpec(a.shape, lambda i: (0,) * a.ndim, pipeline_mode=pl.Buffered(1))
    return pl.pallas_call(
        functools.partial(_ffn_kernel, d_ff=d_ff),
        grid=(n // tm,),
        in_specs=[pl.BlockSpec((tm, D_MODEL), lambda i: (i, 0)), full(norm_g), once(wgu), once(wd),
                  full(norm_f)],
        out_specs=pl.BlockSpec((tm, D_MODEL), lambda i: (i, 0)),
        out_shape=jax.ShapeDtypeStruct((n, D_MODEL), F32),
        compiler_params=pltpu.CompilerParams(dimension_semantics=("arbitrary",),
                                             vmem_limit_bytes=VMEM_LIMIT),
        name="ffn",
    )(h2d, norm_g, wgu, wd, norm_f)


def _block_diag2(w):
    z = jnp.zeros_like(w)
    return jnp.concatenate([jnp.concatenate([w, z], axis=-1), jnp.concatenate([z, w], axis=-1)], axis=-2)


def _chunk_lists(flags):
    bsz = flags.shape[0]
    n_chunks = N_KEY_CHUNKS
    per_step = NSA_QB // Q_BLOCK
    nqb = flags.shape[1] // per_step
    f = flags[:, :, :, 0, :].reshape(bsz, nqb, per_step, N_GROUPS, n_chunks, 2).max(axis=(2, 5))
    cid = jnp.arange(n_chunks, dtype=jnp.int32)
    own = (NSA_QB // KEY_CHUNK) * jnp.arange(nqb, dtype=jnp.int32)[None, :, None, None]
    active = (f > 0) & (cid < own)
    n_active = active.sum(axis=-1)
    slot = jnp.cumsum(active, axis=-1) - 1
    hit = active[..., :, None] & (slot[..., :, None] == cid)
    ids = jnp.sum(jnp.where(hit, cid[:, None], 0), axis=-2)
    ids = jnp.where(cid < n_active[..., None], ids, VOID_CHUNK)
    words, n_batches = [], []
    for g, batch in enumerate(SLC_BATCH):
        n_batches.append((n_active[:, :, g] + batch - 1) // batch)
        padded = jnp.pad(ids[:, :, g], ((0, 0), (0, 0), (0, LIST_WORDS * batch - n_chunks)),
                         constant_values=VOID_CHUNK).reshape(bsz, nqb, LIST_WORDS, batch)
        words.append(functools.reduce(jnp.bitwise_or,
                                      [padded[..., u] << (CHUNK_ID_BITS * u) for u in range(batch)]))
    n_batches = functools.reduce(jnp.maximum, n_batches)
    words = jnp.stack(words, axis=2)
    return n_batches.reshape(-1).astype(jnp.int32), words.reshape(-1).astype(jnp.int32)


def kernel(x, mem, norm_mix, w_in, w_cmp_k1, w_cmp_k2, w_cmp_v1, w_cmp_v2, pe_cmp_k, pe_cmp_v, ln_sgu, w_spatial, b_spatial, w_proj_a, w_proj_b, w_merge, b_merge, w_out, norm_mem_q, norm_mem_kv, w_mq, w_mkv, w_mo, norm_ffn, w_gate_up, w_down, norm_final):
    bsz, t, d = x.shape
    depth = norm_mix.shape[0]
    assert d == D_MODEL and t % Q_BLOCK == 0 and t // SEL_BLOCK <= N_BLK_PAD
    assert t // SEL_BLOCK >= N_SELECT and depth == 1 and t % 1024 == 0
    n = bsz * t
    tm = 512
    h = x.reshape(n, d)
    c0, c1, c2, c3 = NSA_W, NSA_W + KV_W, NSA_W + 3 * KV_W, NSA_W + 3 * KV_W + 2 * GMLP_WIDTH
    onehot = (jnp.arange(t)[:, None] // SEL_BLOCK == jnp.arange(N_BLK_PAD)[None, :]).astype(BF16)
    pad_cols = lambda w, width: jnp.pad(w, ((0, 0), (0, width - w.shape[1])))
    for l in range(depth):
        wi = w_in[l]
        wk, wv = [], []
        for branch in range(2):
            base = c1 + KV_W * branch
            for g in range(N_GROUPS):
                wk.append(pad_cols(wi[:, base + HEAD_DIM * g:base + HEAD_DIM * (g + 1)], LANES))
                v0 = base + N_GROUPS * HEAD_DIM + HEAD_DIM * g
                wv.append(pad_cols(wi[:, v0:v0 + HEAD_DIM], VT_ROWS))
        wrow = jnp.concatenate([wi[:, c0:c1]] + wk + [wi[:, c2:c3]], axis=1).astype(BF16)
        wnt = jnp.concatenate([wi[:, :c0]] + wv + [pad_cols(wi[:, c3:], GATET_ROWS)], axis=1).T.astype(BF16)
        qt, kvc2, kaug, vt, u2, vn2, gatest = _inproj(
            h, norm_mix[l][None], ln_sgu[l][None], wrow, wnt, 2 * tm, bsz, t)

        pe2 = jnp.stack([pe_cmp_k[l], pe_cmp_v[l]])
        pe2 = jnp.concatenate([pe2, pe2], axis=-1)
        w1 = jnp.stack([w_cmp_k1[l], w_cmp_v1[l]]).reshape(2, CMP_BLOCK, HEAD_DIM, CMP_HIDDEN)
        w1bd = _block_diag2(w1).astype(BF16)
        w2k = _block_diag2(w_cmp_k2[l]).astype(BF16)
        w2vt = _block_diag2(w_cmp_v2[l]).T.astype(BF16)
        kc, vct = _compress(kvc2.reshape(bsz, t, KV_W), pe2, w1bd, w2k, w2vt)
        ocmp, selt, flags = _cmp_topk(qt, kc, vct, gatest)
        counts, lists = _chunk_lists(flags)
        onsa = _slc_win(counts, lists, qt, kaug, vt, onehot, selt, gatest, ocmp)

        h = _merge(h, onsa.reshape(n, NSA_W), u2, vn2, norm_mix[l][None], w_merge[l].astype(BF16),
                   b_merge[l][None], w_spatial[l], b_spatial[l].T, w_proj_a[l].astype(BF16),
                   w_proj_b[l].astype(BF16), w_out[l].astype(BF16), 2 * tm)

        mkv = _memkv(mem, norm_mem_kv[l][None], w_mkv[l].astype(BF16))
        h = _xattn(h, norm_mem_q[l][None], w_mq[l].astype(BF16), mkv, w_mo[l].astype(BF16), 2 * tm, t)
        h = _ffn(h, norm_ffn[l][None], w_gate_up[l].astype(BF16), w_down[l].astype(BF16),
                 norm_final[None], tm)
    return h.reshape(bsz, t, d)
```

```python
import functools

import jax
import jax.numpy as jnp
from jax import lax
from jax.experimental import pallas as pl
from jax.experimental.pallas import tpu as pltpu

F32 = jnp.float32
BF16 = jnp.bfloat16

LANES = 128
SUBLANES = 8
D_MODEL = 1024
N_HEADS = 8
HEAD_DIM = 64
N_GROUPS = 2
HPG = N_HEADS // N_GROUPS
CMP_BLOCK = 32
CMP_STRIDE = 16
CMP_HIDDEN = 128
SEL_BLOCK = 64
N_SELECT = 16
WINDOW = 512
Q_BLOCK = 256
KEY_CHUNK = 128
N_BLK_PAD = 128
NSA_W = N_HEADS * HEAD_DIM
KV_W = 2 * N_GROUPS * HEAD_DIM
GMLP_WIDTH = 512
GMLP_GROUPS = 4
GMLP_CHUNK = 128
MEM_HEADS = 4
MEM_HEAD_DIM = 128
MEM_W = MEM_HEADS * MEM_HEAD_DIM
NSA_QB = 256
SLC_BATCH = (4, 5)
N_KEY_CHUNKS = N_BLK_PAD // 2
CHUNK_ID_BITS = 6
VOID_CHUNK = N_KEY_CHUNKS - 1
LIST_WORDS = -(-N_KEY_CHUNKS // min(SLC_BATCH))
VT_ROWS = 80
VCT_ROWS = VT_ROWS + N_BLK_PAD
CMP_KEY_CHUNK = 128
N_FORCED = 3
TOPK_ROW_STEP = 32
CMP_BLOCKS = 2
GATET_ROWS = 32
MASK_BIG = 1e30
EPS = 1e-6
NEG = -1e30
REMOVED = -3e38
SLOPES = tuple(2.0 ** (-8.0 * (h + 1) / N_HEADS) for h in range(N_HEADS))
LOG2E = 1.4426950408889634
Q_SCALE = HEAD_DIM ** -0.5 * LOG2E
VMEM_LIMIT = 56 * 1024 * 1024


def _dot(a, b):
    return jnp.dot(a, b, preferred_element_type=F32)


def _dot_nt(a, b):
    return lax.dot_general(a, b, (((1,), (1,)), ((), ())), preferred_element_type=F32)


def _rms(x, g):
    return x * lax.rsqrt(jnp.mean(x * x, axis=-1, keepdims=True) + EPS) * g


def _iota(shape, dim):
    return lax.broadcasted_iota(jnp.int32, shape, dim)


def _slope_row(g, nq):
    return jnp.concatenate(
        [jnp.full((1, nq), SLOPES[HPG * g + j] * LOG2E, F32) for j in range(HPG)], axis=1)


def _slope_feature_rows(slope_row, n_rows):
    hi = slope_row.astype(BF16).astype(F32)
    r = _iota((n_rows, slope_row.shape[1]), 0)
    return jnp.where(r == 0, hi, jnp.where(r == 1, slope_row - hi, 0.0)).astype(BF16)


def _chunk_slabs(st, chunk):
    return [st[chunk * u:chunk * (u + 1)] for u in range(st.shape[0] // chunk)]


def _col_max(st, dls, chunk):
    parts = [s.reshape(chunk // SUBLANES, SUBLANES, s.shape[1]).max(axis=0) - dl
             for s, dl in zip(_chunk_slabs(st, chunk), dls)]
    return functools.reduce(jnp.maximum, parts).max(axis=0, keepdims=True)


def _probs(st, dls, m, chunk):
    return jnp.concatenate([jnp.exp2(s - (m + dl)).astype(BF16)
                            for s, dl in zip(_chunk_slabs(st, chunk), dls)], axis=0)


def _cast_plan(weights, n_steps):
    for w in weights:
        assert w.shape[0] % (16 * n_steps) == 0, (w.shape, n_steps)
    specs = [pl.BlockSpec((w.shape[0] // n_steps, w.shape[1]), lambda i: (i, 0)) for w in weights]
    shapes = [jax.ShapeDtypeStruct(w.shape, BF16) for w in weights]
    return specs, shapes


def _cast_blocks(src_refs, dst_refs):
    for src, dst in zip(src_refs, dst_refs):
        dst[...] = src[...].astype(BF16)


def _inproj_kernel(x_ref, g_ref, lng_ref, wrow_ref, wnt_ref, *refs, tm, n_cast):
    cast_in, cast_out = refs[:n_cast], refs[len(refs) - n_cast:]
    qt_ref, kvc_ref, kaug_ref, vt_ref, u_ref, vn_ref, gatest_ref = refs[n_cast:len(refs) - n_cast]
    _cast_blocks(cast_in, cast_out)
    xn = _rms(x_ref[...], g_ref[...]).astype(BF16)
    r = _dot(xn, wrow_ref[...])
    nt = _dot_nt(wnt_ref[...], xn)
    qt_ref[0] = (nt[0:NSA_W] * Q_SCALE).astype(BF16)
    kvc_ref[...] = r[:, 0:KV_W]
    lane = _iota((tm, LANES), 1)
    key_feat = jnp.where((lane == HEAD_DIM) | (lane == HEAD_DIM + 1),
                         _iota((tm, LANES), 0) & (KEY_CHUNK - 1), 0).astype(F32)
    ones_row = jnp.where(_iota((VT_ROWS, tm), 0) == HEAD_DIM, 1.0, 0.0)
    for a in range(2 * N_GROUPS):
        kaug_ref[0, a] = (r[:, KV_W + LANES * a:KV_W + LANES * (a + 1)] + key_feat).astype(BF16)
        vt_ref[0, a] = (nt[NSA_W + VT_ROWS * a:NSA_W + VT_ROWS * (a + 1)] + ones_row).astype(BF16)
    uv = jax.nn.gelu(r[:, KV_W + 2 * N_GROUPS * LANES:])
    u_ref[...] = uv[:, :GMLP_WIDTH]
    v = uv[:, GMLP_WIDTH:]
    vc = v - jnp.mean(v, axis=-1, keepdims=True)
    vn = vc * lax.rsqrt(jnp.mean(vc * vc, axis=-1, keepdims=True) + EPS) * lng_ref[...]
    vn_ref[...] = vn.astype(BF16)
    gatest_ref[0] = jax.nn.sigmoid(nt[NSA_W + 2 * N_GROUPS * VT_ROWS:])


def _inproj(x2, norm_g, ln_g, wrow, wnt, to_cast, tm, bsz, t):
    n = x2.shape[0]
    tpb = t // tm
    row = lambda w: pl.BlockSpec((tm, w), lambda i: (i, 0))
    full = lambda a: pl.BlockSpec(a.shape, lambda i: (0,) * a.ndim)
    cast_specs, cast_shapes = _cast_plan(to_cast, n // tm)
    outs = pl.pallas_call(
        functools.partial(_inproj_kernel, tm=tm, n_cast=len(to_cast)),
        grid=(n // tm,),
        in_specs=[row(D_MODEL), full(norm_g), full(ln_g), full(wrow), full(wnt)] + cast_specs,
        out_specs=[pl.BlockSpec((1, NSA_W, tm), lambda i: (i // tpb, 0, i % tpb)),
                   row(KV_W),
                   pl.BlockSpec((1, 2 * N_GROUPS, tm, LANES), lambda i: (i // tpb, 0, i % tpb, 0)),
                   pl.BlockSpec((1, 2 * N_GROUPS, VT_ROWS, tm), lambda i: (i // tpb, 0, 0, i % tpb)),
                   row(GMLP_WIDTH), row(GMLP_WIDTH),
                   pl.BlockSpec((1, GATET_ROWS, tm), lambda i: (i // tpb, 0, i % tpb))] + cast_specs,
        out_shape=[jax.ShapeDtypeStruct((bsz, NSA_W, t), BF16),
                   jax.ShapeDtypeStruct((n, KV_W), F32),
                   jax.ShapeDtypeStruct((bsz, 2 * N_GROUPS, t, LANES), BF16),
                   jax.ShapeDtypeStruct((bsz, 2 * N_GROUPS, VT_ROWS, t), BF16),
                   jax.ShapeDtypeStruct((n, GMLP_WIDTH), F32),
                   jax.ShapeDtypeStruct((n, GMLP_WIDTH), BF16),
                   jax.ShapeDtypeStruct((bsz, GATET_ROWS, t), F32)] + cast_shapes,
        compiler_params=pltpu.CompilerParams(dimension_semantics=("arbitrary",),
                                             vmem_limit_bytes=VMEM_LIMIT),
        name="inproj",
    )(x2, norm_g, ln_g, wrow, wnt, *to_cast)
    return outs[:7], outs[7:]


def _compress_kernel(xk_ref, xv_ref, pe_ref, w1_ref, w2k_ref, w2vt_ref, kc_ref, vct_ref, *, nc):
    outs = []
    for j, x_ref in enumerate((xk_ref, xv_ref)):
        a = jnp.zeros((nc, 2 * CMP_HIDDEN), F32)
        b = jnp.zeros((nc, 2 * CMP_HIDDEN), F32)
        for t in range(0, CMP_STRIDE, 2):
            xs = [x_ref[0, pl.ds(t + e, nc, stride=CMP_STRIDE), :] for e in range(2)]

            def half_block(first):
                lhs = jnp.concatenate([(xs[e] + pe_ref[j, first + t + e:first + t + e + 1, :]).astype(BF16)
                                       for e in range(2)], axis=1)
                rhs = jnp.concatenate([w1_ref[j, first + t], w1_ref[j, first + t + 1]], axis=0)
                return _dot(lhs, rhs)

            a = a + half_block(0)
            b = b + half_block(CMP_STRIDE)
        outs.append(jax.nn.gelu(a + pltpu.roll(b, nc - 1, 0)).astype(BF16))
    k2 = _dot(outs[0], w2k_ref[...])
    v_t = _dot_nt(w2vt_ref[...], outs[1])
    lane = _iota((nc, LANES), 1)
    key_feat = jnp.where((lane == HEAD_DIM) | (lane == HEAD_DIM + 1),
                         CMP_STRIDE * (_iota((nc, LANES), 0) & (CMP_KEY_CHUNK - 1)), 0).astype(F32)
    ci =_iota((N_BLK_PAD, nc), 1) * CMP_STRIDE
    sj = _iota((N_BLK_PAD, nc), 0) * SEL_BLOCK
    overlap_t = jnp.where((ci < sj + SEL_BLOCK) & (ci + (CMP_BLOCK - 1) >= sj), 1.0, 0.0).astype(BF16)
    ones_rows = jnp.where(_iota((VT_ROWS - HEAD_DIM, nc), 0) == 0, 1.0, 0.0).astype(BF16)
    for g in range(N_GROUPS):
        kg = k2 if g == 0 else pltpu.roll(k2, HEAD_DIM, 1)
        kc_ref[0, g] = jnp.where(lane < HEAD_DIM, kg, key_feat).astype(BF16)
        vct_ref[0, g, 0:HEAD_DIM, :] = v_t[HEAD_DIM * g:HEAD_DIM * (g + 1), :].astype(BF16)
        vct_ref[0, g, HEAD_DIM:VT_ROWS, :] = ones_rows
        vct_ref[0, g, VT_ROWS:VCT_ROWS, :] = overlap_t


def _compress(kvc3, pe2, w1bd, w2k, w2vt):
    bsz, t, _ = kvc3.shape
    nc = t // CMP_STRIDE
    full = lambda a: pl.BlockSpec(a.shape, lambda b: (0,) * a.ndim)
    return pl.pallas_call(
        functools.partial(_compress_kernel, nc=nc),
        grid=(bsz,),
        in_specs=[pl.BlockSpec((1, t, LANES), lambda b: (b, 0, 0)),
                  pl.BlockSpec((1, t, LANES), lambda b: (b, 0, 1)),
                  full(pe2), full(w1bd), full(w2k), full(w2vt)],
        out_specs=[pl.BlockSpec((1, N_GROUPS, nc, LANES), lambda b: (b, 0, 0, 0)),
                   pl.BlockSpec((1, N_GROUPS, VCT_ROWS, nc), lambda b: (b, 0, 0, 0))],
        out_shape=[jax.ShapeDtypeStruct((bsz, N_GROUPS, nc, LANES), BF16),
                   jax.ShapeDtypeStruct((bsz, N_GROUPS, VCT_ROWS, nc), BF16)],
        compiler_params=pltpu.CompilerParams(dimension_semantics=("arbitrary",),
                                             vmem_limit_bytes=VMEM_LIMIT),
        name="compress",
    )(kvc3, kvc3, pe2, w1bd, w2k, w2vt)


def _cmp_topk_kernel(qt_ref, kc_ref, vct_ref, gatest_ref, ocmp_ref, selt_ref, flags_ref,
                     m_ref, acc_ref, *, ncp):
    step = pl.program_id(1)
    n_chunks = ncp // CMP_KEY_CHUNK
    chunk_tokens = CMP_KEY_CHUNK * CMP_STRIDE
    tile_heads = lambda a: jnp.concatenate([a] * HPG, axis=1)
    tail_chunks = min(2, n_chunks)
    head_chunks = n_chunks - tail_chunks
    tail_keys = tail_chunks * CMP_KEY_CHUNK
    blocks = range(CMP_BLOCKS)
    groups = range(N_GROUPS)
    items = [(bi, g) for bi in blocks for g in groups]
    start = [(step * CMP_BLOCKS + bi) * Q_BLOCK for bi in blocks]
    qlanes = [slice(Q_BLOCK * bi, Q_BLOCK * (bi + 1)) for bi in blocks]
    nck = [(start[bi] + Q_BLOCK - CMP_BLOCK) // CMP_STRIDE // CMP_KEY_CHUNK + 1 for bi in blocks]
    tail_c0 = [jnp.maximum(nck[bi] - tail_chunks, 0) for bi in blocks]
    tail_rows = [pl.ds(pl.multiple_of(tail_c0[bi] * CMP_KEY_CHUNK, CMP_KEY_CHUNK), tail_keys)
                 for bi in blocks]
    gt = gatest_ref[0]
    slope_rows = [_slope_row(g, Q_BLOCK) for g in groups]
    qas = {(bi, g): jnp.concatenate(
        [jnp.concatenate([qt_ref[0, HEAD_DIM * (HPG * g + j):HEAD_DIM * (HPG * g + j + 1), qlanes[bi]]
                          for j in range(HPG)], axis=1),
         _slope_feature_rows(slope_rows[g], LANES - HEAD_DIM)], axis=0) for bi, g in items}

    def delta(bi, g, c):
        return slope_rows[g] * (start[bi] - c * chunk_tokens).astype(F32)

    for bi in blocks:
        if head_chunks > 0:
            @pl.when(nck[bi] > tail_chunks)
            def _():
                sts = [_dot(kc_ref[0, g, 0:head_chunks * CMP_KEY_CHUNK, :], qas[bi, g]) for g in groups]
                dls = [[delta(bi, g, c) + jnp.where(c < nck[bi] - tail_chunks, 0.0, MASK_BIG)
                        for c in range(head_chunks)] for g in groups]
                ms = [_col_max(sts[g], dls[g], CMP_KEY_CHUNK) for g in groups]
                ps = [_probs(sts[g], dls[g], ms[g], CMP_KEY_CHUNK) for g in groups]
                for g in groups:
                    m_ref[bi, g] = ms[g]
                    acc_ref[bi, g] = _dot(vct_ref[0, g, :, 0:head_chunks * CMP_KEY_CHUNK], ps[g])

            @pl.when(nck[bi] <= tail_chunks)
            def _():
                m_ref[bi] = jnp.full(m_ref.shape[1:], NEG, F32)
                acc_ref[bi] = jnp.zeros(acc_ref.shape[1:], F32)

    def tail_and_select(n_rows):
        if head_chunks > 0:
            m_old = {it: m_ref[it[0], it[1]] for it in items}
            acc_old = {it: acc_ref[it[0], it[1]] for it in items}
        else:
            m_old = {it: jnp.full((1, HPG * Q_BLOCK), NEG, F32) for it in items}
            acc_old = {it: jnp.zeros((VCT_ROWS, HPG * Q_BLOCK), F32) for it in items}
        key_row = _iota((tail_keys, Q_BLOCK), 0)
        q_lane = _iota((tail_keys, Q_BLOCK), 1)
        tail_bias = []
        for bi in blocks:
            key_end = CMP_STRIDE * (tail_c0[bi] * CMP_KEY_CHUNK + key_row) + (CMP_BLOCK - 1) - start[bi]
            tail_bias.append(tile_heads(jnp.where(key_end <= q_lane, 0.0, NEG)))
        sts = {(bi, g): _dot(kc_ref[0, g, tail_rows[bi], :], qas[bi, g]) + tail_bias[bi] for bi, g in items}
        dls = {(bi, g): [delta(bi, g, tail_c0[bi] + u) for u in range(tail_chunks)] for bi, g in items}
        m_new = {it: jnp.maximum(m_old[it], _col_max(sts[it], dls[it], CMP_KEY_CHUNK)) for it in items}
        ps = {it: _probs(sts[it], dls[it], m_new[it], CMP_KEY_CHUNK) for it in items}
        accs = {(bi, g): jnp.exp2(m_old[bi, g] - m_new[bi, g]) * acc_old[bi, g]
                + _dot(vct_ref[0, g, :, tail_rows[bi]], ps[bi, g]) for bi, g in items}

        blk_n = _iota((n_rows, Q_BLOCK), 0)
        bf = blk_n.astype(F32)
        rk, cur = {}, []
        for bi in blocks:
            t_row = start[bi] + _iota((1, Q_BLOCK), 1)
            cur.append(lax.shift_right_logical(t_row, SEL_BLOCK.bit_length() - 1))
            forced = (blk_n == 0) | (blk_n == cur[bi]) | (blk_n == cur[bi] - 1)
            has_key = t_row >= CMP_BLOCK - 1
            per_head = []
            for g in groups:
                acc = accs[bi, g]
                inv_l = 1.0 / jnp.maximum(acc[HEAD_DIM:HEAD_DIM + 1], 1e-30)
                o = acc[0:HEAD_DIM] * inv_l
                imp_h = acc[VT_ROWS:VT_ROWS + n_rows] * inv_l
                imp = sum(imp_h[:, Q_BLOCK * j:Q_BLOCK * (j + 1)] for j in range(HPG))
                imp = jnp.where(has_key, imp, 0.0)
                for j in range(HPG):
                    h = HPG * g + j
                    per_head.append(jnp.where(
                        has_key, gt[3 * h:3 * h + 1, qlanes[bi]] * o[:, Q_BLOCK * j:Q_BLOCK * (j + 1)], 0.0))
                rk[bi, g] = jnp.where(blk_n <= cur[bi], jnp.where(forced, REMOVED, imp), NEG)
            ocmp_ref[0, :, qlanes[bi]] = jnp.concatenate(per_head, axis=0)

        for _ in range(N_SELECT - N_FORCED):
            for it in items:
                m = jnp.max(rk[it], axis=0, keepdims=True)
                idx = jnp.min(jnp.where(rk[it] == m, bf, float(N_BLK_PAD)), axis=0, keepdims=True)
                rk[it] = jnp.where(bf == idx, REMOVED, rk[it])
        ones8 = jnp.ones((SUBLANES, Q_BLOCK), F32)
        for bi, g in items:
            sel = jnp.where((blk_n <= cur[bi]) & (rk[bi, g] < 2.0 * NEG), 1.0, 0.0)
            selt_ref[0, g, 0:n_rows, qlanes[bi]] = sel.astype(BF16)
            flag = (_dot_nt(ones8, sel) > 0.5).astype(jnp.int32)
            if n_rows < N_BLK_PAD:
                selt_ref[0, g, n_rows:N_BLK_PAD, qlanes[bi]] = jnp.zeros((N_BLK_PAD - n_rows, Q_BLOCK), BF16)
                flag = jnp.concatenate([flag, jnp.zeros((SUBLANES, N_BLK_PAD - n_rows), jnp.int32)], axis=1)
            flags_ref[0, bi, g] = flag

    causal_rows = (start[-1] + Q_BLOCK) // SEL_BLOCK
    for n_rows in range(TOPK_ROW_STEP, N_BLK_PAD + 1, TOPK_ROW_STEP):
        @pl.when((causal_rows > n_rows - TOPK_ROW_STEP) & (causal_rows <= n_rows))
        def _():
            tail_and_select(n_rows)


def _cmp_topk(qt, kc, vct, gatest):
    bsz, _, t = qt.shape
    ncp = t // CMP_STRIDE
    nqb = t // Q_BLOCK
    step_q = CMP_BLOCKS * Q_BLOCK
    return pl.pallas_call(
        functools.partial(_cmp_topk_kernel, ncp=ncp),
        grid=(bsz, nqb // CMP_BLOCKS),
        in_specs=[pl.BlockSpec((1, NSA_W, step_q), lambda b, i: (b, 0, i)),
                  pl.BlockSpec((1, N_GROUPS, ncp, LANES), lambda b, i: (b, 0, 0, 0)),
                  pl.BlockSpec((1, N_GROUPS, VCT_ROWS, ncp), lambda b, i: (b, 0, 0, 0)),
                  pl.BlockSpec((1, GATET_ROWS, step_q), lambda b, i: (b, 0, i))],
        out_specs=[pl.BlockSpec((1, NSA_W, step_q), lambda b, i: (b, 0, i)),
                   pl.BlockSpec((1, N_GROUPS, N_BLK_PAD, step_q), lambda b, i: (b, 0, 0, i)),
                   pl.BlockSpec((1, CMP_BLOCKS, N_GROUPS, SUBLANES, N_BLK_PAD), lambda b, i: (b, i, 0, 0, 0))],
        out_shape=[jax.ShapeDtypeStruct((bsz, NSA_W, t), F32),
                   jax.ShapeDtypeStruct((bsz, N_GROUPS, N_BLK_PAD, t), BF16),
                   jax.ShapeDtypeStruct((bsz, nqb, N_GROUPS, SUBLANES, N_BLK_PAD), jnp.int32)],
        scratch_shapes=[pltpu.VMEM((CMP_BLOCKS, N_GROUPS, 1, HPG * Q_BLOCK), F32),
                        pltpu.VMEM((CMP_BLOCKS, N_GROUPS, VCT_ROWS, HPG * Q_BLOCK), F32)],
        compiler_params=pltpu.CompilerParams(dimension_semantics=("arbitrary", "arbitrary"),
                                             vmem_limit_bytes=VMEM_LIMIT),
        name="cmp_topk",
    )(qt, kc, vct, gatest)


def _slc_win_kernel(counts_ref, lists_ref, qt_ref, kaug_ref, vt_ref, oh_ref, selt_ref, gatest_ref,
                    ocmp_ref, out_ref, qaug_ref, m_ref, acc_ref, *, nqb):
    b = pl.program_id(0)
    qb = pl.program_id(1)
    step_id = b * nqb + qb
    start = qb * NSA_QB
    tile_heads = lambda a: jnp.concatenate([a] * HPG, axis=1)
    groups = range(N_GROUPS)
    gt = gatest_ref[0]
    slope_rows = [_slope_row(g, NSA_QB) for g in groups]

    def normalize(acc):
        return acc[0:HEAD_DIM] / jnp.maximum(acc[HEAD_DIM:HEAD_DIM + 1], 1e-30)

    def half_lanes(w):
        return [slice(NSA_QB * h + KEY_CHUNK * w, NSA_QB * h + KEY_CHUNK * (w + 1)) for h in range(HPG)]

    def half(a, w):
        return jnp.concatenate([a[:, s] for s in half_lanes(w)], axis=1)

    def unhalf(lo, hi):
        return jnp.concatenate([x[:, KEY_CHUNK * h:KEY_CHUNK * (h + 1)]
                                for h in range(HPG) for x in (lo, hi)], axis=1)

    def chunk_at(rel):
        pos = start + rel * KEY_CHUNK
        rows = pl.ds(pl.multiple_of(jnp.maximum(pos, 0), KEY_CHUNK), KEY_CHUNK)
        return rows, (jnp.where(pos >= 0, 0.0, MASK_BIG) if rel < 0 else 0.0)

    ki = _iota((KEY_CHUNK, KEY_CHUNK), 0)
    qi = _iota((KEY_CHUNK, KEY_CHUNK), 1)
    upper_bias = tile_heads(jnp.where(ki > qi, 0.0, NEG))
    lower_bias = tile_heads(jnp.where(ki <= qi, 0.0, NEG))
    n_mid = WINDOW // KEY_CHUNK
    mid_dist = (_iota((n_mid * KEY_CHUNK, NSA_QB), 1) + (n_mid - 1) * KEY_CHUNK
                - _iota((n_mid * KEY_CHUNK, NSA_QB), 0))
    mid_bias = tile_heads(jnp.where((mid_dist >= 0) & (mid_dist < WINDOW), 0.0, NEG))
    own_bias = tile_heads(jnp.where(_iota((KEY_CHUNK, NSA_QB), 0) <= _iota((KEY_CHUNK, NSA_QB), 1), 0.0, NEG))

    for g in range(N_GROUPS):
        qaug_ref[g, 0:HEAD_DIM, :] = jnp.concatenate(
            [qt_ref[0, HEAD_DIM * (HPG * g + j):HEAD_DIM * (HPG * g + j + 1), :] for j in range(HPG)],
            axis=1)
        qaug_ref[g, HEAD_DIM:LANES, :] = _slope_feature_rows(slope_rows[g], LANES - HEAD_DIM)
        sel_bias = ((selt_ref[0, g].astype(F32) - 1.0) * MASK_BIG).astype(BF16)
        qaug_ref[g, LANES:2 * LANES, :] = tile_heads(sel_bias)

    mid = [chunk_at(r) for r in range(1 - n_mid, 1)]
    lo_rows, lo_kill = chunk_at(-n_mid)
    hi_rows, _ = chunk_at(1)
    own_rows, _ = chunk_at(0)
    kw = lambda g: kaug_ref.at[0, N_GROUPS + g]
    vw = lambda g: vt_ref.at[0, N_GROUPS + g]
    slc_keys = lambda g, rows: jnp.concatenate([kaug_ref[0, g, rows, :], oh_ref[rows, :]], axis=1)
    q_half = lambda g, w, nrow: jnp.concatenate([qaug_ref[g, 0:nrow, s] for s in half_lanes(w)], axis=1)

    wmid_st = [_dot(jnp.concatenate([kw(g)[rows, :] for rows, _ in mid], axis=0), qaug_ref[g, 0:LANES, :])
               + mid_bias for g in groups]
    wlo_st = [_dot(kw(g)[lo_rows, :], q_half(g, 0, LANES)) + upper_bias for g in groups]
    whi_st = [_dot(kw(g)[hi_rows, :], q_half(g, 1, LANES)) + lower_bias for g in groups]
    own_st = [_dot(slc_keys(g, own_rows), qaug_ref[g]) + own_bias for g in groups]
    ohi_st = [_dot(slc_keys(g, hi_rows), q_half(g, 1, 2 * LANES)) + lower_bias for g in groups]

    wmid_dls = [[slope_rows[g] * float(-KEY_CHUNK * r) + kill
                 for r, (_, kill) in zip(range(1 - n_mid, 1), mid)] for g in groups]
    wlo_dl = [half(slope_rows[g], 0) * float(KEY_CHUNK * n_mid) + lo_kill for g in groups]
    hi_dl = [half(slope_rows[g], 1) * float(-KEY_CHUNK) for g in groups]
    neg_half = jnp.full((1, HPG * KEY_CHUNK), NEG, F32)
    win_m = [jnp.maximum(_col_max(wmid_st[g], wmid_dls[g], KEY_CHUNK),
                         unhalf(_col_max(wlo_st[g], [wlo_dl[g]], KEY_CHUNK),
                                _col_max(whi_st[g], [hi_dl[g]], KEY_CHUNK))) for g in groups]
    own_m = [jnp.maximum(_col_max(own_st[g], [0.0], KEY_CHUNK),
                         unhalf(neg_half, _col_max(ohi_st[g], [hi_dl[g]], KEY_CHUNK))) for g in groups]
    wmid_p = [_probs(wmid_st[g], wmid_dls[g], win_m[g], KEY_CHUNK) for g in groups]
    wlo_p = [_probs(wlo_st[g], [wlo_dl[g]], half(win_m[g], 0), KEY_CHUNK) for g in groups]
    whi_p = [_probs(whi_st[g], [hi_dl[g]], half(win_m[g], 1), KEY_CHUNK) for g in groups]
    own_p = [_probs(own_st[g], [0.0], own_m[g], KEY_CHUNK) for g in groups]
    ohi_p = [_probs(ohi_st[g], [hi_dl[g]], half(own_m[g], 1), KEY_CHUNK) for g in groups]
    o_win = []
    for g in groups:
        acc = _dot(jnp.concatenate([vw(g)[:, rows] for rows, _ in mid], axis=1), wmid_p[g])
        acc = acc + unhalf(_dot(vw(g)[:, lo_rows], wlo_p[g]), _dot(vw(g)[:, hi_rows], whi_p[g]))
        o_win.append(normalize(acc))
    zero_half = jnp.zeros((VT_ROWS, HPG * KEY_CHUNK), F32)
    for g in groups:
        m_ref[g] = own_m[g]
        acc_ref[g] = (_dot(vt_ref[0, g, :, own_rows], own_p[g])
                      + unhalf(zero_half, _dot(vt_ref[0, g, :, hi_rows], ohi_p[g])))

    def slc_body(i, carry):
        sts, vss, dlss = [], [], []
        for g in range(N_GROUPS):
            word = lists_ref[(step_id * N_GROUPS + g) * LIST_WORDS + i]
            ks, vs, dls = [], [], []
            for u in range(SLC_BATCH[g]):
                cid = lax.shift_right_logical(word, CHUNK_ID_BITS * u) & VOID_CHUNK
                valid = cid < VOID_CHUNK
                c = jnp.where(valid, cid, 0)
                rows = pl.ds(pl.multiple_of(c * KEY_CHUNK, KEY_CHUNK), KEY_CHUNK)
                ks.append(jnp.concatenate([kaug_ref[0, g, rows, :], oh_ref[rows, :]], axis=1))
                vs.append(vt_ref[0, g, :, rows])
                dls.append(slope_rows[g] * (start - c * KEY_CHUNK).astype(F32)
                           + jnp.where(valid, 0.0, MASK_BIG))
            sts.append(_dot(jnp.concatenate(ks, axis=0), qaug_ref[g]))
            vss.append(jnp.concatenate(vs, axis=1))
            dlss.append(dls)
        m_old = [m_ref[g] for g in range(N_GROUPS)]
        m_new = [jnp.maximum(m_old[g], _col_max(sts[g], dlss[g], KEY_CHUNK)) for g in range(N_GROUPS)]
        ps = [_probs(sts[g], dlss[g], m_new[g], KEY_CHUNK) for g in range(N_GROUPS)]
        for g in range(N_GROUPS):
            acc_ref[g] = jnp.exp2(m_old[g] - m_new[g]) * acc_ref[g] + _dot(vss[g], ps[g])
            m_ref[g] = m_new[g]
        return carry

    lax.fori_loop(0, counts_ref[step_id], slc_body, 0)

    per_head = []
    for g in range(N_GROUPS):
        o_slc = normalize(acc_ref[g])
        for j in range(HPG):
            h = HPG * g + j
            lanes = slice(NSA_QB * j, NSA_QB * (j + 1))
            per_head.append(gt[3 * h + 1:3 * h + 2, :] * o_slc[:, lanes]
                            + gt[3 * h + 2:3 * h + 3, :] * o_win[g][:, lanes])
    o_t = jnp.concatenate(per_head, axis=0)
    out_ref[0] = (ocmp_ref[0] + o_t).T.astype(BF16)


def _slc_win(counts, lists, qt, kaug, vt, onehot, selt, gatest, ocmp):
    bsz, _, t = qt.shape
    nqb = t // NSA_QB
    once = lambda shape, imap: pl.BlockSpec(shape, imap, pipeline_mode=pl.Buffered(1))
    grid_spec = pltpu.PrefetchScalarGridSpec(
        num_scalar_prefetch=2,
        grid=(bsz, nqb),
        in_specs=[pl.BlockSpec((1, NSA_W, NSA_QB), lambda b, i, *_: (b, 0, i)),
                  pl.BlockSpec((1, 2 * N_GROUPS, t, LANES), lambda b, i, *_: (b, 0, 0, 0)),
                  pl.BlockSpec((1, 2 * N_GROUPS, VT_ROWS, t), lambda b, i, *_: (b, 0, 0, 0)),
                  once((t, N_BLK_PAD), lambda b, i, *_: (0, 0)),
                  pl.BlockSpec((1, N_GROUPS, N_BLK_PAD, NSA_QB), lambda b, i, *_: (b, 0, 0, i)),
                  pl.BlockSpec((1, GATET_ROWS, NSA_QB), lambda b, i, *_: (b, 0, i)),
                  pl.BlockSpec((1, NSA_W, NSA_QB), lambda b, i, *_: (b, 0, i))],
        out_specs=pl.BlockSpec((1, NSA_QB, NSA_W), lambda b, i, *_: (b, i, 0)),
        scratch_shapes=[pltpu.VMEM((N_GROUPS, 2 * LANES, HPG * NSA_QB), BF16),
                        pltpu.VMEM((N_GROUPS, 1, HPG * NSA_QB), F32),
                        pltpu.VMEM((N_GROUPS, VT_ROWS, HPG * NSA_QB), F32)],
    )
    return pl.pallas_call(
        functools.partial(_slc_win_kernel, nqb=nqb),
        grid_spec=grid_spec,
        out_shape=jax.ShapeDtypeStruct((bsz, t, NSA_W), BF16),
        compiler_params=pltpu.CompilerParams(dimension_semantics=("arbitrary", "arbitrary"),
                                             vmem_limit_bytes=VMEM_LIMIT),
        name="slc_win",
    )(counts, lists, qt, kaug, vt, onehot, selt, gatest, ocmp)


def _merge_kernel(x_ref, onsa_ref, u_ref, vn_ref, g_ref, wm_ref, bm_ref, ws_ref, bs_ref,
                  wpa_ref, wpb_ref, wo_ref, *refs, tm, n_cast):
    cast_in, h_ref, cast_out = refs[:n_cast], refs[n_cast], refs[n_cast + 1:]
    _cast_blocks(cast_in, cast_out)
    x = x_ref[...]
    xn = _rms(x, g_ref[...]).astype(BF16)
    tril = _iota((GMLP_CHUNK, GMLP_CHUNK), 0) >= _iota((GMLP_CHUNK, GMLP_CHUNK), 1)
    sgu_rows = []
    for c in range(tm // GMLP_CHUNK):
        rows = slice(GMLP_CHUNK * c, GMLP_CHUNK * (c + 1))
        cols = []
        for g in range(GMLP_GROUPS):
            lanes = slice(LANES * g, LANES * (g + 1))
            w = jnp.where(tril, ws_ref[g], 0.0).astype(BF16)
            cols.append(_dot(w, vn_ref[rows, lanes]) + bs_ref[:, g:g + 1])
        sgu_rows.append(u_ref[rows, :] * jnp.concatenate(cols, axis=1))
    o_sgu = jnp.concatenate(sgu_rows, axis=0).astype(BF16)
    mg = jax.nn.sigmoid(_dot(xn, wm_ref[...]) + bm_ref[...])
    mixed = (mg[:, :D_MODEL] * _dot(onsa_ref[...], wpa_ref[...])
             + mg[:, D_MODEL:] * _dot(o_sgu, wpb_ref[...]))
    h_ref[...] = x + _dot(mixed.astype(BF16), wo_ref[...])


def _merge(x2, onsa2, u2, vn2, norm_g, wm, bm, ws, bs_t, wpa, wpb, wo, to_cast, tm):
    n = x2.shape[0]
    row = lambda w: pl.BlockSpec((tm, w), lambda i: (i, 0))
    full = lambda a: pl.BlockSpec(a.shape, lambda i: (0,) * a.ndim)
    cast_specs, cast_shapes = _cast_plan(to_cast, n // tm)
    outs = pl.pallas_call(
        functools.partial(_merge_kernel, tm=tm, n_cast=len(to_cast)),
        grid=(n // tm,),
        in_specs=[row(D_MODEL), row(NSA_W), row(GMLP_WIDTH), row(GMLP_WIDTH), full(norm_g),
                  full(wm), full(bm), full(ws), full(bs_t), full(wpa), full(wpb), full(wo)] + cast_specs,
        out_specs=[row(D_MODEL)] + cast_specs,
        out_shape=[jax.ShapeDtypeStruct((n, D_MODEL), F32)] + cast_shapes,
        compiler_params=pltpu.CompilerParams(dimension_semantics=("arbitrary",),
                                             vmem_limit_bytes=VMEM_LIMIT),
        name="merge",
    )(x2, onsa2, u2, vn2, norm_g, wm, bm, ws, bs_t, wpa, wpb, wo, *to_cast)
    return outs[0], outs[1:]


def _memkv_kernel(mem_ref, g_ref, w_ref, out_ref):
    out_ref[0] = _dot(_rms(mem_ref[0], g_ref[...]).astype(BF16), w_ref[...]).astype(BF16)


def _memkv(mem, norm_g, w):
    bsz, nm, _ = mem.shape
    return pl.pallas_call(
        _memkv_kernel,
        grid=(bsz,),
        in_specs=[pl.BlockSpec((1, nm, D_MODEL), lambda b: (b, 0, 0)),
                  pl.BlockSpec(norm_g.shape, lambda b: (0, 0)),
                  pl.BlockSpec(w.shape, lambda b: (0, 0))],
        out_specs=pl.BlockSpec((1, nm, 2 * MEM_W), lambda b: (b, 0, 0)),
        out_shape=jax.ShapeDtypeStruct((bsz, nm, 2 * MEM_W), BF16),
        compiler_params=pltpu.CompilerParams(dimension_semantics=("arbitrary",),
                                             vmem_limit_bytes=VMEM_LIMIT),
        name="memkv",
    )(mem, norm_g, w)


def _xattn_kernel(h_ref, g_ref, wq_ref, mkv_ref, wo_ref, out_ref, *, tm):
    halves = [slice(0, tm // 2), slice(tm // 2, tm)]
    head_lanes = [slice(MEM_HEAD_DIM * a, MEM_HEAD_DIM * (a + 1)) for a in range(MEM_HEADS)]
    hs = [h_ref[rows, :] for rows in halves]
    hqs = [(_dot(_rms(h, g_ref[...]).astype(BF16), wq_ref[...]) * (MEM_HEAD_DIM ** -0.5 * LOG2E)).astype(BF16)
           for h in hs]
    ss = [[_dot_nt(hq[:, lanes], mkv_ref[0, :, lanes]) for lanes in head_lanes] for hq in hqs]
    es = [[jnp.exp2(s - jnp.max(s, axis=-1, keepdims=True)) for s in s_half] for s_half in ss]
    os = []
    for e_half in es:
        heads = []
        for a, e in enumerate(e_half):
            v = mkv_ref[0, :, MEM_W + MEM_HEAD_DIM * a:MEM_W + MEM_HEAD_DIM * (a + 1)]
            heads.append(_dot(e.astype(BF16), v) * (1.0 / jnp.sum(e, axis=-1, keepdims=True)))
        os.append(jnp.concatenate(heads, axis=1).astype(BF16))
    for rows, h, o in zip(halves, hs, os):
        out_ref[rows, :] = h + _dot(o, wo_ref[...])


def _xattn(h2d, norm_g, wq, mkv, wo, tm, rows_per_batch):
    n = h2d.shape[0]
    nm = mkv.shape[1]
    tiles_per_batch = rows_per_batch // tm
    full = lambda a: pl.BlockSpec(a.shape, lambda i: (0,) * a.ndim)
    return pl.pallas_call(
        functools.partial(_xattn_kernel, tm=tm),
        grid=(n // tm,),
        in_specs=[pl.BlockSpec((tm, D_MODEL), lambda i: (i, 0)), full(norm_g), full(wq),
                  pl.BlockSpec((1, nm, 2 * MEM_W), lambda i: (i // tiles_per_batch, 0, 0)),
                  full(wo)],
        out_specs=pl.BlockSpec((tm, D_MODEL), lambda i: (i, 0)),
        out_shape=jax.ShapeDtypeStruct((n, D_MODEL), F32),
        compiler_params=pltpu.CompilerParams(dimension_semantics=("arbitrary",),
                                             vmem_limit_bytes=VMEM_LIMIT),
        name="xattn",
    )(h2d, norm_g, wq, mkv, wo)


def _ffn_kernel(h_ref, g_ref, wgu_ref, wd_ref, gf_ref, out_ref, *, d_ff):
    tm = h_ref.shape[0]
    halves = [slice(0, tm // 2), slice(tm // 2, tm)]
    hs = [h_ref[rows, :] for rows in halves]
    hns = [_rms(h, g_ref[...]).astype(BF16) for h in hs]
    gates = [_dot(hn, wgu_ref[:, :d_ff]) for hn in hns]
    ups = [_dot(hn, wgu_ref[:, d_ff:]) for hn in hns]
    acts = [(jax.nn.silu(gate) * up).astype(BF16) for gate, up in zip(gates, ups)]
    for rows, h, act in zip(halves, hs, acts):
        out_ref[rows, :] = _rms(h + _dot(act, wd_ref[...]), gf_ref[...])


def _ffn(h2d, norm_g, wgu, wd, norm_f, tm):
    n = h2d.shape[0]
    d_ff = wd.shape[0]
    full = lambda a: pl.BlockSpec(a.shape, lambda i: (0,) * a.ndim)
    once = lambda a: pl.BlockSpec(a.shape, lambda i: (0,) * a.ndim, pipeline_mode=pl.Buffered(1))
    return pl.pallas_call(
        functools.partial(_ffn_kernel, d_ff=d_ff),
        grid=(n // tm,),
        in_specs=[pl.BlockSpec((tm, D_MODEL), lambda i: (i, 0)), full(norm_g), once(wgu), once(wd),
                  full(norm_f)],
        out_specs=pl.BlockSpec((tm, D_MODEL), lambda i: (i, 0)),
        out_shape=jax.ShapeDtypeStruct((n, D_MODEL), F32),
        compiler_params=pltpu.CompilerParams(dimension_semantics=("arbitrary",),
                                             vmem_limit_bytes=VMEM_LIMIT),
        name="ffn",
    )(h2d, norm_g, wgu, wd, norm_f)


def _block_diag2(w):
    z = jnp.zeros_like(w)
    return jnp.concatenate([jnp.concatenate([w, z], axis=-1), jnp.concatenate([z, w], axis=-1)], axis=-2)


def _chunk_lists(flags):
    bsz = flags.shape[0]
    n_chunks = N_KEY_CHUNKS
    per_step = NSA_QB // Q_BLOCK
    nqb = flags.shape[1] // per_step
    f = flags[:, :, :, 0, :].reshape(bsz, nqb, per_step, N_GROUPS, n_chunks, 2).max(axis=(2, 5))
    cid = jnp.arange(n_chunks, dtype=jnp.int32)
    own = (NSA_QB // KEY_CHUNK) * jnp.arange(nqb, dtype=jnp.int32)[None, :, None, None]
    active = (f > 0) & (cid < own)
    n_active = active.sum(axis=-1)
    slot = jnp.cumsum(active, axis=-1) - 1
    hit = active[..., :, None] & (slot[..., :, None] == cid)
    ids = jnp.sum(jnp.where(hit, cid[:, None], 0), axis=-2)
    ids = jnp.where(cid < n_active[..., None], ids, VOID_CHUNK)
    words, n_batches = [], []
    for g, batch in enumerate(SLC_BATCH):
        n_batches.append((n_active[:, :, g] + batch - 1) // batch)
        padded = jnp.pad(ids[:, :, g], ((0, 0), (0, 0), (0, LIST_WORDS * batch - n_chunks)),
                         constant_values=VOID_CHUNK).reshape(bsz, nqb, LIST_WORDS, batch)
        words.append(functools.reduce(jnp.bitwise_or,
                                      [padded[..., u] << (CHUNK_ID_BITS * u) for u in range(batch)]))
    n_batches = functools.reduce(jnp.maximum, n_batches)
    words = jnp.stack(words, axis=2)
    return n_batches.reshape(-1).astype(jnp.int32), words.reshape(-1).astype(jnp.int32)


def kernel(x, mem, norm_mix, w_in, w_cmp_k1, w_cmp_k2, w_cmp_v1, w_cmp_v2, pe_cmp_k, pe_cmp_v, ln_sgu, w_spatial, b_spatial, w_proj_a, w_proj_b, w_merge, b_merge, w_out, norm_mem_q, norm_mem_kv, w_mq, w_mkv, w_mo, norm_ffn, w_gate_up, w_down, norm_final):
    bsz, t, d = x.shape
    depth = norm_mix.shape[0]
    assert d == D_MODEL and t % Q_BLOCK == 0 and t // SEL_BLOCK <= N_BLK_PAD
    assert t // SEL_BLOCK >= N_SELECT and depth == 1 and t % 1024 == 0
    n = bsz * t
    tm = 512
    h = x.reshape(n, d)
    c0, c1, c2, c3 = NSA_W, NSA_W + KV_W, NSA_W + 3 * KV_W, NSA_W + 3 * KV_W + 2 * GMLP_WIDTH
    onehot = (jnp.arange(t)[:, None] // SEL_BLOCK == jnp.arange(N_BLK_PAD)[None, :]).astype(BF16)
    pad_cols = lambda w, width: jnp.pad(w, ((0, 0), (0, width - w.shape[1])))
    for l in range(depth):
        wi = w_in[l]
        wk, wv = [], []
        for branch in range(2):
            base = c1 + KV_W * branch
            for g in range(N_GROUPS):
                wk.append(pad_cols(wi[:, base + HEAD_DIM * g:base + HEAD_DIM * (g + 1)], LANES))
                v0 = base + N_GROUPS * HEAD_DIM + HEAD_DIM * g
                wv.append(pad_cols(wi[:, v0:v0 + HEAD_DIM], VT_ROWS))
        wrow = jnp.concatenate([wi[:, c0:c1]] + wk + [wi[:, c2:c3]], axis=1).astype(BF16)
        wnt = jnp.concatenate([wi[:, :c0]] + wv + [pad_cols(wi[:, c3:], GATET_ROWS)], axis=1).T.astype(BF16)
        (qt, kvc2, kaug, vt, u2, vn2, gatest), (wm_b, wpa_b, wpb_b, wo_b, wmq_b, wmkv_b, wmo_b) = _inproj(
            h, norm_mix[l][None], ln_sgu[l][None], wrow, wnt,
            [w_merge[l], w_proj_a[l], w_proj_b[l], w_out[l], w_mq[l], w_mkv[l], w_mo[l]], 2 * tm, bsz, t)

        pe2 = jnp.stack([pe_cmp_k[l], pe_cmp_v[l]])
        pe2 = jnp.concatenate([pe2, pe2], axis=-1)
        w1 = jnp.stack([w_cmp_k1[l], w_cmp_v1[l]]).reshape(2, CMP_BLOCK, HEAD_DIM, CMP_HIDDEN)
        w1bd = _block_diag2(w1).astype(BF16)
        w2k = _block_diag2(w_cmp_k2[l]).astype(BF16)
        w2vt = _block_diag2(w_cmp_v2[l]).T.astype(BF16)
        kc, vct = _compress(kvc2.reshape(bsz, t, KV_W), pe2, w1bd, w2k, w2vt)
        ocmp, selt, flags = _cmp_topk(qt, kc, vct, gatest)
        counts, lists = _chunk_lists(flags)
        onsa = _slc_win(counts, lists, qt, kaug, vt, onehot, selt, gatest, ocmp)

        h, (wgu_b, wd_b) = _merge(h, onsa.reshape(n, NSA_W), u2, vn2, norm_mix[l][None], wm_b,
                                  b_merge[l][None], w_spatial[l], b_spatial[l].T, wpa_b, wpb_b, wo_b,
                                  [w_gate_up[l], w_down[l]], 2 * tm)

        mkv = _memkv(mem, norm_mem_kv[l][None], wmkv_b)
        h = _xattn(h, norm_mem_q[l][None], wmq_b, mkv, wmo_b, 2 * tm, t)
        h = _ffn(h, norm_ffn[l][None], wgu_b, wd_b, norm_final[None], tm)
    return h.reshape(bsz, t, d)
```

```python
import functools

import jax
import jax.numpy as jnp
from jax import lax
from jax.experimental import pallas as pl
from jax.experimental.pallas import tpu as pltpu

F32 = jnp.float32
BF16 = jnp.bfloat16

LANES = 128
SUBLANES = 8
D_MODEL = 1024
N_HEADS = 8
HEAD_DIM = 64
N_GROUPS = 2
HPG = N_HEADS // N_GROUPS
CMP_BLOCK = 32
CMP_STRIDE = 16
CMP_HIDDEN = 128
SEL_BLOCK = 64
N_SELECT = 16
WINDOW = 512
Q_BLOCK = 256
KEY_CHUNK = 128
N_BLK_PAD = 128
NSA_W = N_HEADS * HEAD_DIM
KV_W = 2 * N_GROUPS * HEAD_DIM
GMLP_WIDTH = 512
GMLP_GROUPS = 4
GMLP_CHUNK = 128
MEM_HEADS = 4
MEM_HEAD_DIM = 128
MEM_W = MEM_HEADS * MEM_HEAD_DIM
NSA_QB = 256
SLC_BATCH = (4, 5)
N_KEY_CHUNKS = N_BLK_PAD // 2
CHUNK_ID_BITS = 6
VOID_CHUNK = N_KEY_CHUNKS - 1
LIST_WORDS = -(-N_KEY_CHUNKS // min(SLC_BATCH))
VT_ROWS = 80
VCT_ROWS = VT_ROWS + N_BLK_PAD
CMP_KEY_CHUNK = 128
N_FORCED = 3
TOPK_ROW_STEP = 64
CMP_BLOCKS = 2
GATET_ROWS = 32
MASK_BIG = 1e30
EPS = 1e-6
NEG = -1e30
REMOVED = -3e38
SLOPES = tuple(2.0 ** (-8.0 * (h + 1) / N_HEADS) for h in range(N_HEADS))
LOG2E = 1.4426950408889634
Q_SCALE = HEAD_DIM ** -0.5 * LOG2E
VMEM_LIMIT = 56 * 1024 * 1024


def _dot(a, b):
    return jnp.dot(a, b, preferred_element_type=F32)


def _dot_nt(a, b):
    return lax.dot_general(a, b, (((1,), (1,)), ((), ())), preferred_element_type=F32)


def _rms(x, g):
    return x * lax.rsqrt(jnp.mean(x * x, axis=-1, keepdims=True) + EPS) * g


def _iota(shape, dim):
    return lax.broadcasted_iota(jnp.int32, shape, dim)


def _slope_row(g, nq):
    return jnp.concatenate(
        [jnp.full((1, nq), SLOPES[HPG * g + j] * LOG2E, F32) for j in range(HPG)], axis=1)


def _slope_feature_rows(slope_row, n_rows):
    hi = slope_row.astype(BF16).astype(F32)
    r = _iota((n_rows, slope_row.shape[1]), 0)
    return jnp.where(r == 0, hi, jnp.where(r == 1, slope_row - hi, 0.0)).astype(BF16)


def _chunk_slabs(st, chunk):
    return [st[chunk * u:chunk * (u + 1)] for u in range(st.shape[0] // chunk)]


def _col_max(st, dls, chunk):
    parts = [s.reshape(chunk // SUBLANES, SUBLANES, s.shape[1]).max(axis=0) - dl
             for s, dl in zip(_chunk_slabs(st, chunk), dls)]
    return functools.reduce(jnp.maximum, parts).max(axis=0, keepdims=True)


def _probs(st, dls, m, chunk):
    return jnp.concatenate([jnp.exp2(s - (m + dl)).astype(BF16)
                            for s, dl in zip(_chunk_slabs(st, chunk), dls)], axis=0)


def _cast_plan(weights, n_steps):
    for w in weights:
        assert w.shape[0] % (16 * n_steps) == 0, (w.shape, n_steps)
    specs = [pl.BlockSpec((w.shape[0] // n_steps, w.shape[1]), lambda i: (i, 0)) for w in weights]
    shapes = [jax.ShapeDtypeStruct(w.shape, BF16) for w in weights]
    return specs, shapes


def _cast_blocks(src_refs, dst_refs):
    for src, dst in zip(src_refs, dst_refs):
        dst[...] = src[...].astype(BF16)


def _inproj_kernel(x_ref, g_ref, lng_ref, wrow_ref, wnt_ref, *refs, tm, n_cast):
    cast_in, cast_out = refs[:n_cast], refs[len(refs) - n_cast:]
    qt_ref, kvc_ref, kaug_ref, vt_ref, u_ref, vn_ref, gatest_ref = refs[n_cast:len(refs) - n_cast]
    _cast_blocks(cast_in, cast_out)
    xn = _rms(x_ref[...], g_ref[...]).astype(BF16)
    r = _dot(xn, wrow_ref[...])
    nt = _dot_nt(wnt_ref[...], xn)
    qt_ref[0] = (nt[0:NSA_W] * Q_SCALE).astype(BF16)
    kvc_ref[...] = r[:, 0:KV_W]
    lane = _iota((tm, LANES), 1)
    key_feat = jnp.where((lane == HEAD_DIM) | (lane == HEAD_DIM + 1),
                         _iota((tm, LANES), 0) & (KEY_CHUNK - 1), 0).astype(F32)
    ones_row = jnp.where(_iota((VT_ROWS, tm), 0) == HEAD_DIM, 1.0, 0.0)
    for a in range(2 * N_GROUPS):
        kaug_ref[0, a] = (r[:, KV_W + LANES * a:KV_W + LANES * (a + 1)] + key_feat).astype(BF16)
        vt_ref[0, a] = (nt[NSA_W + VT_ROWS * a:NSA_W + VT_ROWS * (a + 1)] + ones_row).astype(BF16)
    uv = jax.nn.gelu(r[:, KV_W + 2 * N_GROUPS * LANES:])
    u_ref[...] = uv[:, :GMLP_WIDTH]
    v = uv[:, GMLP_WIDTH:]
    vc = v - jnp.mean(v, axis=-1, keepdims=True)
    vn = vc * lax.rsqrt(jnp.mean(vc * vc, axis=-1, keepdims=True) + EPS) * lng_ref[...]
    vn_ref[...] = vn.astype(BF16)
    gatest_ref[0] = jax.nn.sigmoid(nt[NSA_W + 2 * N_GROUPS * VT_ROWS:])


def _inproj(x2, norm_g, ln_g, wrow, wnt, to_cast, tm, bsz, t):
    n = x2.shape[0]
    tpb = t // tm
    row = lambda w: pl.BlockSpec((tm, w), lambda i: (i, 0))
    full = lambda a: pl.BlockSpec(a.shape, lambda i: (0,) * a.ndim)
    cast_specs, cast_shapes = _cast_plan(to_cast, n // tm)
    outs = pl.pallas_call(
        functools.partial(_inproj_kernel, tm=tm, n_cast=len(to_cast)),
        grid=(n // tm,),
        in_specs=[row(D_MODEL), full(norm_g), full(ln_g), full(wrow), full(wnt)] + cast_specs,
        out_specs=[pl.BlockSpec((1, NSA_W, tm), lambda i: (i // tpb, 0, i % tpb)),
                   row(KV_W),
                   pl.BlockSpec((1, 2 * N_GROUPS, tm, LANES), lambda i: (i // tpb, 0, i % tpb, 0)),
                   pl.BlockSpec((1, 2 * N_GROUPS, VT_ROWS, tm), lambda i: (i // tpb, 0, 0, i % tpb)),
                   row(GMLP_WIDTH), row(GMLP_WIDTH),
                   pl.BlockSpec((1, GATET_ROWS, tm), lambda i: (i // tpb, 0, i % tpb))] + cast_specs,
        out_shape=[jax.ShapeDtypeStruct((bsz, NSA_W, t), BF16),
                   jax.ShapeDtypeStruct((n, KV_W), F32),
                   jax.ShapeDtypeStruct((bsz, 2 * N_GROUPS, t, LANES), BF16),
                   jax.ShapeDtypeStruct((bsz, 2 * N_GROUPS, VT_ROWS, t), BF16),
                   jax.ShapeDtypeStruct((n, GMLP_WIDTH), F32),
                   jax.ShapeDtypeStruct((n, GMLP_WIDTH), BF16),
                   jax.ShapeDtypeStruct((bsz, GATET_ROWS, t), F32)] + cast_shapes,
        compiler_params=pltpu.CompilerParams(dimension_semantics=("arbitrary",),
                                             vmem_limit_bytes=VMEM_LIMIT),
        name="inproj",
    )(x2, norm_g, ln_g, wrow, wnt, *to_cast)
    return outs[:7], outs[7:]


def _compress_kernel(xk_ref, xv_ref, pe_ref, w1_ref, w2k_ref, w2vt_ref, kc_ref, vct_ref, *, nc):
    outs = []
    for j, x_ref in enumerate((xk_ref, xv_ref)):
        a = jnp.zeros((nc, 2 * CMP_HIDDEN), F32)
        b = jnp.zeros((nc, 2 * CMP_HIDDEN), F32)
        for t in range(0, CMP_STRIDE, 2):
            xs = [x_ref[0, pl.ds(t + e, nc, stride=CMP_STRIDE), :] for e in range(2)]

            def half_block(first):
                lhs = jnp.concatenate([(xs[e] + pe_ref[j, first + t + e:first + t + e + 1, :]).astype(BF16)
                                       for e in range(2)], axis=1)
                rhs = jnp.concatenate([w1_ref[j, first + t], w1_ref[j, first + t + 1]], axis=0)
                return _dot(lhs, rhs)

            a = a + half_block(0)
            b = b + half_block(CMP_STRIDE)
        outs.append(jax.nn.gelu(a + pltpu.roll(b, nc - 1, 0)).astype(BF16))
    k2 = _dot(outs[0], w2k_ref[...])
    v_t = _dot_nt(w2vt_ref[...], outs[1])
    lane = _iota((nc, LANES), 1)
    key_feat = jnp.where((lane == HEAD_DIM) | (lane == HEAD_DIM + 1),
                         CMP_STRIDE * (_iota((nc, LANES), 0) & (CMP_KEY_CHUNK - 1)), 0).astype(F32)
    ci =_iota((N_BLK_PAD, nc), 1) * CMP_STRIDE
    sj = _iota((N_BLK_PAD, nc), 0) * SEL_BLOCK
    overlap_t = jnp.where((ci < sj + SEL_BLOCK) & (ci + (CMP_BLOCK - 1) >= sj), 1.0, 0.0).astype(BF16)
    ones_rows = jnp.where(_iota((VT_ROWS - HEAD_DIM, nc), 0) == 0, 1.0, 0.0).astype(BF16)
    for g in range(N_GROUPS):
        kg = k2 if g == 0 else pltpu.roll(k2, HEAD_DIM, 1)
        kc_ref[0, g] = jnp.where(lane < HEAD_DIM, kg, key_feat).astype(BF16)
        vct_ref[0, g, 0:HEAD_DIM, :] = v_t[HEAD_DIM * g:HEAD_DIM * (g + 1), :].astype(BF16)
        vct_ref[0, g, HEAD_DIM:VT_ROWS, :] = ones_rows
        vct_ref[0, g, VT_ROWS:VCT_ROWS, :] = overlap_t


def _compress(kvc3, pe2, w1bd, w2k, w2vt):
    bsz, t, _ = kvc3.shape
    nc = t // CMP_STRIDE
    full = lambda a: pl.BlockSpec(a.shape, lambda b: (0,) * a.ndim)
    return pl.pallas_call(
        functools.partial(_compress_kernel, nc=nc),
        grid=(bsz,),
        in_specs=[pl.BlockSpec((1, t, LANES), lambda b: (b, 0, 0)),
                  pl.BlockSpec((1, t, LANES), lambda b: (b, 0, 1)),
                  full(pe2), full(w1bd), full(w2k), full(w2vt)],
        out_specs=[pl.BlockSpec((1, N_GROUPS, nc, LANES), lambda b: (b, 0, 0, 0)),
                   pl.BlockSpec((1, N_GROUPS, VCT_ROWS, nc), lambda b: (b, 0, 0, 0))],
        out_shape=[jax.ShapeDtypeStruct((bsz, N_GROUPS, nc, LANES), BF16),
                   jax.ShapeDtypeStruct((bsz, N_GROUPS, VCT_ROWS, nc), BF16)],
        compiler_params=pltpu.CompilerParams(dimension_semantics=("arbitrary",),
                                             vmem_limit_bytes=VMEM_LIMIT),
        name="compress",
    )(kvc3, kvc3, pe2, w1bd, w2k, w2vt)


def _cmp_topk_kernel(qt_ref, kc_ref, vct_ref, gatest_ref, ocmp_ref, selt_ref, flags_ref,
                     m_ref, acc_ref, *, ncp):
    step = pl.program_id(1)
    n_chunks = ncp // CMP_KEY_CHUNK
    chunk_tokens = CMP_KEY_CHUNK * CMP_STRIDE
    tile_heads = lambda a: jnp.concatenate([a] * HPG, axis=1)
    tail_chunks = min(2, n_chunks)
    head_chunks = n_chunks - tail_chunks
    tail_keys = tail_chunks * CMP_KEY_CHUNK
    blocks = range(CMP_BLOCKS)
    groups = range(N_GROUPS)
    items = [(bi, g) for bi in blocks for g in groups]
    start = [(step * CMP_BLOCKS + bi) * Q_BLOCK for bi in blocks]
    qlanes = [slice(Q_BLOCK * bi, Q_BLOCK * (bi + 1)) for bi in blocks]
    nck = [(start[bi] + Q_BLOCK - CMP_BLOCK) // CMP_STRIDE // CMP_KEY_CHUNK + 1 for bi in blocks]
    tail_c0 = [jnp.maximum(nck[bi] - tail_chunks, 0) for bi in blocks]
    tail_rows = [pl.ds(pl.multiple_of(tail_c0[bi] * CMP_KEY_CHUNK, CMP_KEY_CHUNK), tail_keys)
                 for bi in blocks]
    gt = gatest_ref[0]
    slope_rows = [_slope_row(g, Q_BLOCK) for g in groups]
    qas = {(bi, g): jnp.concatenate(
        [jnp.concatenate([qt_ref[0, HEAD_DIM * (HPG * g + j):HEAD_DIM * (HPG * g + j + 1), qlanes[bi]]
                          for j in range(HPG)], axis=1),
         _slope_feature_rows(slope_rows[g], LANES - HEAD_DIM)], axis=0) for bi, g in items}

    def delta(bi, g, c):
        return slope_rows[g] * (start[bi] - c * chunk_tokens).astype(F32)

    for bi in blocks:
        if head_chunks > 0:
            @pl.when(nck[bi] > tail_chunks)
            def _():
                sts = [_dot(kc_ref[0, g, 0:head_chunks * CMP_KEY_CHUNK, :], qas[bi, g]) for g in groups]
                dls = [[delta(bi, g, c) + jnp.where(c < nck[bi] - tail_chunks, 0.0, MASK_BIG)
                        for c in range(head_chunks)] for g in groups]
                ms = [_col_max(sts[g], dls[g], CMP_KEY_CHUNK) for g in groups]
                ps = [_probs(sts[g], dls[g], ms[g], CMP_KEY_CHUNK) for g in groups]
                for g in groups:
                    m_ref[bi, g] = ms[g]
                    acc_ref[bi, g] = _dot(vct_ref[0, g, :, 0:head_chunks * CMP_KEY_CHUNK], ps[g])

            @pl.when(nck[bi] <= tail_chunks)
            def _():
                m_ref[bi] = jnp.full(m_ref.shape[1:], NEG, F32)
                acc_ref[bi] = jnp.zeros(acc_ref.shape[1:], F32)

    def tail_and_select(n_rows):
        if head_chunks > 0:
            m_old = {it: m_ref[it[0], it[1]] for it in items}
            acc_old = {it: acc_ref[it[0], it[1]] for it in items}
        else:
            m_old = {it: jnp.full((1, HPG * Q_BLOCK), NEG, F32) for it in items}
            acc_old = {it: jnp.zeros((VCT_ROWS, HPG * Q_BLOCK), F32) for it in items}
        key_row = _iota((tail_keys, Q_BLOCK), 0)
        q_lane = _iota((tail_keys, Q_BLOCK), 1)
        tail_bias = []
        for bi in blocks:
            key_end = CMP_STRIDE * (tail_c0[bi] * CMP_KEY_CHUNK + key_row) + (CMP_BLOCK - 1) - start[bi]
            tail_bias.append(tile_heads(jnp.where(key_end <= q_lane, 0.0, NEG)))
        sts = {(bi, g): _dot(kc_ref[0, g, tail_rows[bi], :], qas[bi, g]) + tail_bias[bi] for bi, g in items}
        dls = {(bi, g): [delta(bi, g, tail_c0[bi] + u) for u in range(tail_chunks)] for bi, g in items}
        m_new = {it: jnp.maximum(m_old[it], _col_max(sts[it], dls[it], CMP_KEY_CHUNK)) for it in items}
        ps = {it: _probs(sts[it], dls[it], m_new[it], CMP_KEY_CHUNK) for it in items}
        accs = {(bi, g): jnp.exp2(m_old[bi, g] - m_new[bi, g]) * acc_old[bi, g]
                + _dot(vct_ref[0, g, :, tail_rows[bi]], ps[bi, g]) for bi, g in items}

        blk_n = _iota((n_rows, Q_BLOCK), 0)
        bf = blk_n.astype(F32)
        rk, cur = {}, []
        for bi in blocks:
            t_row = start[bi] + _iota((1, Q_BLOCK), 1)
            cur.append(lax.shift_right_logical(t_row, SEL_BLOCK.bit_length() - 1))
            forced = (blk_n == 0) | (blk_n == cur[bi]) | (blk_n == cur[bi] - 1)
            has_key = t_row >= CMP_BLOCK - 1
            per_head = []
            for g in groups:
                acc = accs[bi, g]
                inv_l = 1.0 / jnp.maximum(acc[HEAD_DIM:HEAD_DIM + 1], 1e-30)
                o = acc[0:HEAD_DIM] * inv_l
                imp_h = acc[VT_ROWS:VT_ROWS + n_rows] * inv_l
                imp = sum(imp_h[:, Q_BLOCK * j:Q_BLOCK * (j + 1)] for j in range(HPG))
                imp = jnp.where(has_key, imp, 0.0)
                for j in range(HPG):
                    h = HPG * g + j
                    per_head.append(jnp.where(
                        has_key, gt[3 * h:3 * h + 1, qlanes[bi]] * o[:, Q_BLOCK * j:Q_BLOCK * (j + 1)], 0.0))
                rk[bi, g] = jnp.where(blk_n <= cur[bi], jnp.where(forced, REMOVED, imp), NEG)
            ocmp_ref[0, :, qlanes[bi]] = jnp.concatenate(per_head, axis=0)

        for _ in range(N_SELECT - N_FORCED):
            for it in items:
                m = jnp.max(rk[it], axis=0, keepdims=True)
                idx = jnp.min(jnp.where(rk[it] == m, bf, float(N_BLK_PAD)), axis=0, keepdims=True)
                rk[it] = jnp.where(bf == idx, REMOVED, rk[it])
        ones8 = jnp.ones((SUBLANES, Q_BLOCK), F32)
        for bi, g in items:
            sel = jnp.where((blk_n <= cur[bi]) & (rk[bi, g] < 2.0 * NEG), 1.0, 0.0)
            selt_ref[0, g, 0:n_rows, qlanes[bi]] = sel.astype(BF16)
            flag = (_dot_nt(ones8, sel) > 0.5).astype(jnp.int32)
            if n_rows < N_BLK_PAD:
                selt_ref[0, g, n_rows:N_BLK_PAD, qlanes[bi]] = jnp.zeros((N_BLK_PAD - n_rows, Q_BLOCK), BF16)
                flag = jnp.concatenate([flag, jnp.zeros((SUBLANES, N_BLK_PAD - n_rows), jnp.int32)], axis=1)
            flags_ref[0, bi, g] = flag

    causal_rows = (start[-1] + Q_BLOCK) // SEL_BLOCK
    for n_rows in range(TOPK_ROW_STEP, N_BLK_PAD + 1, TOPK_ROW_STEP):
        @pl.when((causal_rows > n_rows - TOPK_ROW_STEP) & (causal_rows <= n_rows))
        def _():
            tail_and_select(n_rows)


def _cmp_topk(qt, kc, vct, gatest):
    bsz, _, t = qt.shape
    ncp = t // CMP_STRIDE
    nqb = t // Q_BLOCK
    step_q = CMP_BLOCKS * Q_BLOCK
    return pl.pallas_call(
        functools.partial(_cmp_topk_kernel, ncp=ncp),
        grid=(bsz, nqb // CMP_BLOCKS),
        in_specs=[pl.BlockSpec((1, NSA_W, step_q), lambda b, i: (b, 0, i)),
                  pl.BlockSpec((1, N_GROUPS, ncp, LANES), lambda b, i: (b, 0, 0, 0)),
                  pl.BlockSpec((1, N_GROUPS, VCT_ROWS, ncp), lambda b, i: (b, 0, 0, 0)),
                  pl.BlockSpec((1, GATET_ROWS, step_q), lambda b, i: (b, 0, i))],
        out_specs=[pl.BlockSpec((1, NSA_W, step_q), lambda b, i: (b, 0, i)),
                   pl.BlockSpec((1, N_GROUPS, N_BLK_PAD, step_q), lambda b, i: (b, 0, 0, i)),
                   pl.BlockSpec((1, CMP_BLOCKS, N_GROUPS, SUBLANES, N_BLK_PAD), lambda b, i: (b, i, 0, 0, 0))],
        out_shape=[jax.ShapeDtypeStruct((bsz, NSA_W, t), F32),
                   jax.ShapeDtypeStruct((bsz, N_GROUPS, N_BLK_PAD, t), BF16),
                   jax.ShapeDtypeStruct((bsz, nqb, N_GROUPS, SUBLANES, N_BLK_PAD), jnp.int32)],
        scratch_shapes=[pltpu.VMEM((CMP_BLOCKS, N_GROUPS, 1, HPG * Q_BLOCK), F32),
                        pltpu.VMEM((CMP_BLOCKS, N_GROUPS, VCT_ROWS, HPG * Q_BLOCK), F32)],
        compiler_params=pltpu.CompilerParams(dimension_semantics=("arbitrary", "arbitrary"),
                                             vmem_limit_bytes=VMEM_LIMIT),
        name="cmp_topk",
    )(qt, kc, vct, gatest)


def _slc_win_kernel(counts_ref, lists_ref, qt_ref, kaug_ref, vt_ref, oh_ref, selt_ref, gatest_ref,
                    ocmp_ref, out_ref, qaug_ref, m_ref, acc_ref, *, nqb):
    b = pl.program_id(0)
    qb = pl.program_id(1)
    step_id = b * nqb + qb
    start = qb * NSA_QB
    tile_heads = lambda a: jnp.concatenate([a] * HPG, axis=1)
    groups = range(N_GROUPS)
    gt = gatest_ref[0]
    slope_rows = [_slope_row(g, NSA_QB) for g in groups]

    def normalize(acc):
        return acc[0:HEAD_DIM] / jnp.maximum(acc[HEAD_DIM:HEAD_DIM + 1], 1e-30)

    def half_lanes(w):
        return [slice(NSA_QB * h + KEY_CHUNK * w, NSA_QB * h + KEY_CHUNK * (w + 1)) for h in range(HPG)]

    def half(a, w):
        return jnp.concatenate([a[:, s] for s in half_lanes(w)], axis=1)

    def unhalf(lo, hi):
        return jnp.concatenate([x[:, KEY_CHUNK * h:KEY_CHUNK * (h + 1)]
                                for h in range(HPG) for x in (lo, hi)], axis=1)

    def chunk_at(rel):
        pos = start + rel * KEY_CHUNK
        rows = pl.ds(pl.multiple_of(jnp.maximum(pos, 0), KEY_CHUNK), KEY_CHUNK)
        return rows, (jnp.where(pos >= 0, 0.0, MASK_BIG) if rel < 0 else 0.0)

    ki = _iota((KEY_CHUNK, KEY_CHUNK), 0)
    qi = _iota((KEY_CHUNK, KEY_CHUNK), 1)
    upper_bias = tile_heads(jnp.where(ki > qi, 0.0, NEG))
    lower_bias = tile_heads(jnp.where(ki <= qi, 0.0, NEG))
    n_mid = WINDOW // KEY_CHUNK
    mid_dist = (_iota((n_mid * KEY_CHUNK, NSA_QB), 1) + (n_mid - 1) * KEY_CHUNK
                - _iota((n_mid * KEY_CHUNK, NSA_QB), 0))
    mid_bias = tile_heads(jnp.where((mid_dist >= 0) & (mid_dist < WINDOW), 0.0, NEG))
    own_bias = tile_heads(jnp.where(_iota((KEY_CHUNK, NSA_QB), 0) <= _iota((KEY_CHUNK, NSA_QB), 1), 0.0, NEG))

    for g in range(N_GROUPS):
        qaug_ref[g, 0:HEAD_DIM, :] = jnp.concatenate(
            [qt_ref[0, HEAD_DIM * (HPG * g + j):HEAD_DIM * (HPG * g + j + 1), :] for j in range(HPG)],
            axis=1)
        qaug_ref[g, HEAD_DIM:LANES, :] = _slope_feature_rows(slope_rows[g], LANES - HEAD_DIM)
        sel_bias = ((selt_ref[0, g].astype(F32) - 1.0) * MASK_BIG).astype(BF16)
        qaug_ref[g, LANES:2 * LANES, :] = tile_heads(sel_bias)

    mid = [chunk_at(r) for r in range(1 - n_mid, 1)]
    lo_rows, lo_kill = chunk_at(-n_mid)
    hi_rows, _ = chunk_at(1)
    own_rows, _ = chunk_at(0)
    kw = lambda g: kaug_ref.at[0, N_GROUPS + g]
    vw = lambda g: vt_ref.at[0, N_GROUPS + g]
    slc_keys = lambda g, rows: jnp.concatenate([kaug_ref[0, g, rows, :], oh_ref[rows, :]], axis=1)
    q_half = lambda g, w, nrow: jnp.concatenate([qaug_ref[g, 0:nrow, s] for s in half_lanes(w)], axis=1)

    wmid_st = [_dot(jnp.concatenate([kw(g)[rows, :] for rows, _ in mid], axis=0), qaug_ref[g, 0:LANES, :])
               + mid_bias for g in groups]
    wlo_st = [_dot(kw(g)[lo_rows, :], q_half(g, 0, LANES)) + upper_bias for g in groups]
    whi_st = [_dot(kw(g)[hi_rows, :], q_half(g, 1, LANES)) + lower_bias for g in groups]
    own_st = [_dot(slc_keys(g, own_rows), qaug_ref[g]) + own_bias for g in groups]
    ohi_st = [_dot(slc_keys(g, hi_rows), q_half(g, 1, 2 * LANES)) + lower_bias for g in groups]

    wmid_dls = [[slope_rows[g] * float(-KEY_CHUNK * r) + kill
                 for r, (_, kill) in zip(range(1 - n_mid, 1), mid)] for g in groups]
    wlo_dl = [half(slope_rows[g], 0) * float(KEY_CHUNK * n_mid) + lo_kill for g in groups]
    hi_dl = [half(slope_rows[g], 1) * float(-KEY_CHUNK) for g in groups]
    neg_half = jnp.full((1, HPG * KEY_CHUNK), NEG, F32)
    win_m = [jnp.maximum(_col_max(wmid_st[g], wmid_dls[g], KEY_CHUNK),
                         unhalf(_col_max(wlo_st[g], [wlo_dl[g]], KEY_CHUNK),
                                _col_max(whi_st[g], [hi_dl[g]], KEY_CHUNK))) for g in groups]
    own_m = [jnp.maximum(_col_max(own_st[g], [0.0], KEY_CHUNK),
                         unhalf(neg_half, _col_max(ohi_st[g], [hi_dl[g]], KEY_CHUNK))) for g in groups]
    wmid_p = [_probs(wmid_st[g], wmid_dls[g], win_m[g], KEY_CHUNK) for g in groups]
    wlo_p = [_probs(wlo_st[g], [wlo_dl[g]], half(win_m[g], 0), KEY_CHUNK) for g in groups]
    whi_p = [_probs(whi_st[g], [hi_dl[g]], half(win_m[g], 1), KEY_CHUNK) for g in groups]
    own_p = [_probs(own_st[g], [0.0], own_m[g], KEY_CHUNK) for g in groups]
    ohi_p = [_probs(ohi_st[g], [hi_dl[g]], half(own_m[g], 1), KEY_CHUNK) for g in groups]
    o_win = []
    for g in groups:
        acc = _dot(jnp.concatenate([vw(g)[:, rows] for rows, _ in mid], axis=1), wmid_p[g])
        acc = acc + unhalf(_dot(vw(g)[:, lo_rows], wlo_p[g]), _dot(vw(g)[:, hi_rows], whi_p[g]))
        o_win.append(normalize(acc))
    zero_half = jnp.zeros((VT_ROWS, HPG * KEY_CHUNK), F32)
    for g in groups:
        m_ref[g] = own_m[g]
        acc_ref[g] = (_dot(vt_ref[0, g, :, own_rows], own_p[g])
                      + unhalf(zero_half, _dot(vt_ref[0, g, :, hi_rows], ohi_p[g])))

    def slc_body(i, carry):
        sts, vss, dlss = [], [], []
        for g in range(N_GROUPS):
            word = lists_ref[(step_id * N_GROUPS + g) * LIST_WORDS + i]
            ks, vs, dls = [], [], []
            for u in range(SLC_BATCH[g]):
                cid = lax.shift_right_logical(word, CHUNK_ID_BITS * u) & VOID_CHUNK
                valid = cid < VOID_CHUNK
                c = jnp.where(valid, cid, 0)
                rows = pl.ds(pl.multiple_of(c * KEY_CHUNK, KEY_CHUNK), KEY_CHUNK)
                ks.append(jnp.concatenate([kaug_ref[0, g, rows, :], oh_ref[rows, :]], axis=1))
                vs.append(vt_ref[0, g, :, rows])
                dls.append(slope_rows[g] * (start - c * KEY_CHUNK).astype(F32)
                           + jnp.where(valid, 0.0, MASK_BIG))
            sts.append(_dot(jnp.concatenate(ks, axis=0), qaug_ref[g]))
            vss.append(jnp.concatenate(vs, axis=1))
            dlss.append(dls)
        m_old = [m_ref[g] for g in range(N_GROUPS)]
        m_new = [jnp.maximum(m_old[g], _col_max(sts[g], dlss[g], KEY_CHUNK)) for g in range(N_GROUPS)]
        ps = [_probs(sts[g], dlss[g], m_new[g], KEY_CHUNK) for g in range(N_GROUPS)]
        for g in range(N_GROUPS):
            acc_ref[g] = jnp.exp2(m_old[g] - m_new[g]) * acc_ref[g] + _dot(vss[g], ps[g])
            m_ref[g] = m_new[g]
        return carry

    lax.fori_loop(0, counts_ref[step_id], slc_body, 0)

    per_head = []
    for g in range(N_GROUPS):
        o_slc = normalize(acc_ref[g])
        for j in range(HPG):
            h = HPG * g + j
            lanes = slice(NSA_QB * j, NSA_QB * (j + 1))
            per_head.append(gt[3 * h + 1:3 * h + 2, :] * o_slc[:, lanes]
                            + gt[3 * h + 2:3 * h + 3, :] * o_win[g][:, lanes])
    o_t = jnp.concatenate(per_head, axis=0)
    out_ref[0] = (ocmp_ref[0] + o_t).T.astype(BF16)


def _slc_win(counts, lists, qt, kaug, vt, onehot, selt, gatest, ocmp):
    bsz, _, t = qt.shape
    nqb = t // NSA_QB
    once = lambda shape, imap: pl.BlockSpec(shape, imap, pipeline_mode=pl.Buffered(1))
    grid_spec = pltpu.PrefetchScalarGridSpec(
        num_scalar_prefetch=2,
        grid=(bsz, nqb),
        in_specs=[pl.BlockSpec((1, NSA_W, NSA_QB), lambda b, i, *_: (b, 0, i)),
                  pl.BlockSpec((1, 2 * N_GROUPS, t, LANES), lambda b, i, *_: (b, 0, 0, 0)),
                  pl.BlockSpec((1, 2 * N_GROUPS, VT_ROWS, t), lambda b, i, *_: (b, 0, 0, 0)),
                  once((t, N_BLK_PAD), lambda b, i, *_: (0, 0)),
                  pl.BlockSpec((1, N_GROUPS, N_BLK_PAD, NSA_QB), lambda b, i, *_: (b, 0, 0, i)),
                  pl.BlockSpec((1, GATET_ROWS, NSA_QB), lambda b, i, *_: (b, 0, i)),
                  pl.BlockSpec((1, NSA_W, NSA_QB), lambda b, i, *_: (b, 0, i))],
        out_specs=pl.BlockSpec((1, NSA_QB, NSA_W), lambda b, i, *_: (b, i, 0)),
        scratch_shapes=[pltpu.VMEM((N_GROUPS, 2 * LANES, HPG * NSA_QB), BF16),
                        pltpu.VMEM((N_GROUPS, 1, HPG * NSA_QB), F32),
                        pltpu.VMEM((N_GROUPS, VT_ROWS, HPG * NSA_QB), F32)],
    )
    return pl.pallas_call(
        functools.partial(_slc_win_kernel, nqb=nqb),
        grid_spec=grid_spec,
        out_shape=jax.ShapeDtypeStruct((bsz, t, NSA_W), BF16),
        compiler_params=pltpu.CompilerParams(dimension_semantics=("arbitrary", "arbitrary"),
                                             vmem_limit_bytes=VMEM_LIMIT),
        name="slc_win",
    )(counts, lists, qt, kaug, vt, onehot, selt, gatest, ocmp)


def _merge_kernel(x_ref, onsa_ref, u_ref, vn_ref, g_ref, wm_ref, bm_ref, ws_ref, bs_ref,
                  wpa_ref, wpb_ref, wo_ref, *refs, tm, n_cast):
    cast_in, h_ref, cast_out = refs[:n_cast], refs[n_cast], refs[n_cast + 1:]
    _cast_blocks(cast_in, cast_out)
    x = x_ref[...]
    xn = _rms(x, g_ref[...]).astype(BF16)
    tril = _iota((GMLP_CHUNK, GMLP_CHUNK), 0) >= _iota((GMLP_CHUNK, GMLP_CHUNK), 1)
    sgu_rows = []
    for c in range(tm // GMLP_CHUNK):
        rows = slice(GMLP_CHUNK * c, GMLP_CHUNK * (c + 1))
        cols = []
        for g in range(GMLP_GROUPS):
            lanes = slice(LANES * g, LANES * (g + 1))
            w = jnp.where(tril, ws_ref[g], 0.0).astype(BF16)
            cols.append(_dot(w, vn_ref[rows, lanes]) + bs_ref[:, g:g + 1])
        sgu_rows.append(u_ref[rows, :] * jnp.concatenate(cols, axis=1))
    o_sgu = jnp.concatenate(sgu_rows, axis=0).astype(BF16)
    mg = jax.nn.sigmoid(_dot(xn, wm_ref[...]) + bm_ref[...])
    mixed = (mg[:, :D_MODEL] * _dot(onsa_ref[...], wpa_ref[...])
             + mg[:, D_MODEL:] * _dot(o_sgu, wpb_ref[...]))
    h_ref[...] = x + _dot(mixed.astype(BF16), wo_ref[...])


def _merge(x2, onsa2, u2, vn2, norm_g, wm, bm, ws, bs_t, wpa, wpb, wo, to_cast, tm):
    n = x2.shape[0]
    row = lambda w: pl.BlockSpec((tm, w), lambda i: (i, 0))
    full = lambda a: pl.BlockSpec(a.shape, lambda i: (0,) * a.ndim)
    cast_specs, cast_shapes = _cast_plan(to_cast, n // tm)
    outs = pl.pallas_call(
        functools.partial(_merge_kernel, tm=tm, n_cast=len(to_cast)),
        grid=(n // tm,),
        in_specs=[row(D_MODEL), row(NSA_W), row(GMLP_WIDTH), row(GMLP_WIDTH), full(norm_g),
                  full(wm), full(bm), full(ws), full(bs_t), full(wpa), full(wpb), full(wo)] + cast_specs,
        out_specs=[row(D_MODEL)] + cast_specs,
        out_shape=[jax.ShapeDtypeStruct((n, D_MODEL), F32)] + cast_shapes,
        compiler_params=pltpu.CompilerParams(dimension_semantics=("arbitrary",),
                                             vmem_limit_bytes=VMEM_LIMIT),
        name="merge",
    )(x2, onsa2, u2, vn2, norm_g, wm, bm, ws, bs_t, wpa, wpb, wo, *to_cast)
    return outs[0], outs[1:]


def _memkv_kernel(mem_ref, g_ref, w_ref, out_ref):
    out_ref[0] = _dot(_rms(mem_ref[0], g_ref[...]).astype(BF16), w_ref[...]).astype(BF16)


def _memkv(mem, norm_g, w):
    bsz, nm, _ = mem.shape
    return pl.pallas_call(
        _memkv_kernel,
        grid=(bsz,),
        in_specs=[pl.BlockSpec((1, nm, D_MODEL), lambda b: (b, 0, 0)),
                  pl.BlockSpec(norm_g.shape, lambda b: (0, 0)),
                  pl.BlockSpec(w.shape, lambda b: (0, 0))],
        out_specs=pl.BlockSpec((1, nm, 2 * MEM_W), lambda b: (b, 0, 0)),
        out_shape=jax.ShapeDtypeStruct((bsz, nm, 2 * MEM_W), BF16),
        compiler_params=pltpu.CompilerParams(dimension_semantics=("arbitrary",),
                                             vmem_limit_bytes=VMEM_LIMIT),
        name="memkv",
    )(mem, norm_g, w)


def _xattn_kernel(h_ref, g_ref, wq_ref, mkv_ref, wo_ref, out_ref, *, tm):
    halves = [slice(0, tm // 2), slice(tm // 2, tm)]
    head_lanes = [slice(MEM_HEAD_DIM * a, MEM_HEAD_DIM * (a + 1)) for a in range(MEM_HEADS)]
    hs = [h_ref[rows, :] for rows in halves]
    hqs = [(_dot(_rms(h, g_ref[...]).astype(BF16), wq_ref[...]) * (MEM_HEAD_DIM ** -0.5 * LOG2E)).astype(BF16)
           for h in hs]
    ss = [[_dot_nt(hq[:, lanes], mkv_ref[0, :, lanes]) for lanes in head_lanes] for hq in hqs]
    es = [[jnp.exp2(s - jnp.max(s, axis=-1, keepdims=True)) for s in s_half] for s_half in ss]
    os = []
    for e_half in es:
        heads = []
        for a, e in enumerate(e_half):
            v = mkv_ref[0, :, MEM_W + MEM_HEAD_DIM * a:MEM_W + MEM_HEAD_DIM * (a + 1)]
            heads.append(_dot(e.astype(BF16), v) * (1.0 / jnp.sum(e, axis=-1, keepdims=True)))
        os.append(jnp.concatenate(heads, axis=1).astype(BF16))
    for rows, h, o in zip(halves, hs, os):
        out_ref[rows, :] = h + _dot(o, wo_ref[...])


def _xattn(h2d, norm_g, wq, mkv, wo, tm, rows_per_batch):
    n = h2d.shape[0]
    nm = mkv.shape[1]
    tiles_per_batch = rows_per_batch // tm
    full = lambda a: pl.BlockSpec(a.shape, lambda i: (0,) * a.ndim)
    return pl.pallas_call(
        functools.partial(_xattn_kernel, tm=tm),
        grid=(n // tm,),
        in_specs=[pl.BlockSpec((tm, D_MODEL), lambda i: (i, 0)), full(norm_g), full(wq),
                  pl.BlockSpec((1, nm, 2 * MEM_W), lambda i: (i // tiles_per_batch, 0, 0)),
                  full(wo)],
        out_specs=pl.BlockSpec((tm, D_MODEL), lambda i: (i, 0)),
        out_shape=jax.ShapeDtypeStruct((n, D_MODEL), F32),
        compiler_params=pltpu.CompilerParams(dimension_semantics=("arbitrary",),
                                             vmem_limit_bytes=VMEM_LIMIT),
        name="xattn",
    )(h2d, norm_g, wq, mkv, wo)


def _ffn_kernel(h_ref, g_ref, wgu_ref, wd_ref, gf_ref, out_ref, *, d_ff):
    tm = h_ref.shape[0]
    halves = [slice(0, tm // 2), slice(tm // 2, tm)]
    hs = [h_ref[rows, :] for rows in halves]
    hns = [_rms(h, g_ref[...]).astype(BF16) for h in hs]
    gates = [_dot(hn, wgu_ref[:, :d_ff]) for hn in hns]
    ups = [_dot(hn, wgu_ref[:, d_ff:]) for hn in hns]
    acts = [(jax.nn.silu(gate) * up).astype(BF16) for gate, up in zip(gates, ups)]
    for rows, h, act in zip(halves, hs, acts):
        out_ref[rows, :] = _rms(h + _dot(act, wd_ref[...]), gf_ref[...])


def _ffn(h2d, norm_g, wgu, wd, norm_f, tm):
    n = h2d.shape[0]
    d_ff = wd.shape[0]
    full = lambda a: pl.BlockSpec(a.shape, lambda i: (0,) * a.ndim)
    once = lambda a: pl.BlockSpec(a.shape, lambda i: (0,) * a.ndim, pipeline_mode=pl.Buffered(1))
    return pl.pallas_call(
        functools.partial(_ffn_kernel, d_ff=d_ff),
        grid=(n // tm,),
        in_specs=[pl.BlockSpec((tm, D_MODEL), lambda i: (i, 0)), full(norm_g), once(wgu), once(wd),
                  full(norm_f)],
        out_specs=pl.BlockSpec((tm, D_MODEL), lambda i: (i, 0)),
        out_shape=jax.ShapeDtypeStruct((n, D_MODEL), F32),
        compiler_params=pltpu.CompilerParams(dimension_semantics=("arbitrary",),
                                             vmem_limit_bytes=VMEM_LIMIT),
        name="ffn",
    )(h2d, norm_g, wgu, wd, norm_f)


def _block_diag2(w):
    z = jnp.zeros_like(w)
    return jnp.concatenate([jnp.concatenate([w, z], axis=-1), jnp.concatenate([z, w], axis=-1)], axis=-2)


def _chunk_lists(flags):
    bsz = flags.shape[0]
    n_chunks = N_KEY_CHUNKS
    per_step = NSA_QB // Q_BLOCK
    nqb = flags.shape[1] // per_step
    f = flags[:, :, :, 0, :].reshape(bsz, nqb, per_step, N_GROUPS, n_chunks, 2).max(axis=(2, 5))
    cid = jnp.arange(n_chunks, dtype=jnp.int32)
    own = (NSA_QB // KEY_CHUNK) * jnp.arange(nqb, dtype=jnp.int32)[None, :, None, None]
    active = (f > 0) & (cid < own)
    n_active = active.sum(axis=-1)
    slot = jnp.cumsum(active, axis=-1) - 1
    hit = active[..., :, None] & (slot[..., :, None] == cid)
    ids = jnp.sum(jnp.where(hit, cid[:, None], 0), axis=-2)
    ids = jnp.where(cid < n_active[..., None], ids, VOID_CHUNK)
    words, n_batches = [], []
    for g, batch in enumerate(SLC_BATCH):
        n_batches.append((n_active[:, :, g] + batch - 1) // batch)
        padded = jnp.pad(ids[:, :, g], ((0, 0), (0, 0), (0, LIST_WORDS * batch - n_chunks)),
                         constant_values=VOID_CHUNK).reshape(bsz, nqb, LIST_WORDS, batch)
        words.append(functools.reduce(jnp.bitwise_or,
                                      [padded[..., u] << (CHUNK_ID_BITS * u) for u in range(batch)]))
    n_batches = functools.reduce(jnp.maximum, n_batches)
    words = jnp.stack(words, axis=2)
    return n_batches.reshape(-1).astype(jnp.int32), words.reshape(-1).astype(jnp.int32)


def kernel(x, mem, norm_mix, w_in, w_cmp_k1, w_cmp_k2, w_cmp_v1, w_cmp_v2, pe_cmp_k, pe_cmp_v, ln_sgu, w_spatial, b_spatial, w_proj_a, w_proj_b, w_merge, b_merge, w_out, norm_mem_q, norm_mem_kv, w_mq, w_mkv, w_mo, norm_ffn, w_gate_up, w_down, norm_final):
    bsz, t, d = x.shape
    depth = norm_mix.shape[0]
    assert d == D_MODEL and t % Q_BLOCK == 0 and t // SEL_BLOCK <= N_BLK_PAD
    assert t // SEL_BLOCK >= N_SELECT and depth == 1 and t % 1024 == 0
    n = bsz * t
    tm = 512
    h = x.reshape(n, d)
    c0, c1, c2, c3 = NSA_W, NSA_W + KV_W, NSA_W + 3 * KV_W, NSA_W + 3 * KV_W + 2 * GMLP_WIDTH
    onehot = (jnp.arange(t)[:, None] // SEL_BLOCK == jnp.arange(N_BLK_PAD)[None, :]).astype(BF16)
    pad_cols = lambda w, width: jnp.pad(w, ((0, 0), (0, width - w.shape[1])))
    for l in range(depth):
        wi = w_in[l]
        wk, wv = [], []
        for branch in range(2):
            base = c1 + KV_W * branch
            for g in range(N_GROUPS):
                wk.append(pad_cols(wi[:, base + HEAD_DIM * g:base + HEAD_DIM * (g + 1)], LANES))
                v0 = base + N_GROUPS * HEAD_DIM + HEAD_DIM * g
                wv.append(pad_cols(wi[:, v0:v0 + HEAD_DIM], VT_ROWS))
        wrow = jnp.concatenate([wi[:, c0:c1]] + wk + [wi[:, c2:c3]], axis=1).astype(BF16)
        wnt = jnp.concatenate([wi[:, :c0]] + wv + [pad_cols(wi[:, c3:], GATET_ROWS)], axis=1).T.astype(BF16)
        (qt, kvc2, kaug, vt, u2, vn2, gatest), (wm_b, wpa_b, wpb_b, wo_b, wmq_b, wmkv_b, wmo_b) = _inproj(
            h, norm_mix[l][None], ln_sgu[l][None], wrow, wnt,
            [w_merge[l], w_proj_a[l], w_proj_b[l], w_out[l], w_mq[l], w_mkv[l], w_mo[l]], 2 * tm, bsz, t)

        pe2 = jnp.stack([pe_cmp_k[l], pe_cmp_v[l]])
        pe2 = jnp.concatenate([pe2, pe2], axis=-1)
        w1 = jnp.stack([w_cmp_k1[l], w_cmp_v1[l]]).reshape(2, CMP_BLOCK, HEAD_DIM, CMP_HIDDEN)
        w1bd = _block_diag2(w1).astype(BF16)
        w2k = _block_diag2(w_cmp_k2[l]).astype(BF16)
        w2vt = _block_diag2(w_cmp_v2[l]).T.astype(BF16)
        kc, vct = _compress(kvc2.reshape(bsz, t, KV_W), pe2, w1bd, w2k, w2vt)
        ocmp, selt, flags = _cmp_topk(qt, kc, vct, gatest)
        counts, lists = _chunk_lists(flags)
        onsa = _slc_win(counts, lists, qt, kaug, vt, onehot, selt, gatest, ocmp)

        h, (wgu_b, wd_b) = _merge(h, onsa.reshape(n, NSA_W), u2, vn2, norm_mix[l][None], wm_b,
                                  b_merge[l][None], w_spatial[l], b_spatial[l].T, wpa_b, wpb_b, wo_b,
                                  [w_gate_up[l], w_down[l]], 2 * tm)

        mkv = _memkv(mem, norm_mem_kv[l][None], wmkv_b)
        h = _xattn(h, norm_mem_q[l][None], wmq_b, mkv, wmo_b, 2 * tm, t)
        h = _ffn(h, norm_ffn[l][None], wgu_b, wd_b, norm_final[None], tm)
    return h.reshape(bsz, t, d)
```

```python
import functools

import jax
import jax.numpy as jnp
from jax import lax
from jax.experimental import pallas as pl
from jax.experimental.pallas import tpu as pltpu

F32 = jnp.float32
BF16 = jnp.bfloat16

LANES = 128
SUBLANES = 8
D_MODEL = 1024
N_HEADS = 8
HEAD_DIM = 64
N_GROUPS = 2
HPG = N_HEADS // N_GROUPS
CMP_BLOCK = 32
CMP_STRIDE = 16
CMP_HIDDEN = 128
SEL_BLOCK = 64
N_SELECT = 16
WINDOW = 512
Q_BLOCK = 256
KEY_CHUNK = 128
N_BLK_PAD = 128
NSA_W = N_HEADS * HEAD_DIM
KV_W = 2 * N_GROUPS * HEAD_DIM
GMLP_WIDTH = 512
GMLP_GROUPS = 4
GMLP_CHUNK = 128
MEM_HEADS = 4
MEM_HEAD_DIM = 128
MEM_W = MEM_HEADS * MEM_HEAD_DIM
NSA_QB = 256
SLC_BATCH = (4, 5)
N_KEY_CHUNKS = N_BLK_PAD // 2
CHUNK_ID_BITS = 6
VOID_CHUNK = N_KEY_CHUNKS - 1
LIST_WORDS = -(-N_KEY_CHUNKS // min(SLC_BATCH))
VT_ROWS = 80
VCT_ROWS = VT_ROWS + N_BLK_PAD
CMP_KEY_CHUNK = 128
N_FORCED = 3
TOPK_ROW_STEP = 64
CMP_BLOCKS = 2
GATET_ROWS = 32
MASK_BIG = 1e30
EPS = 1e-6
NEG = -1e30
REMOVED = -3e38
SLOPES = tuple(2.0 ** (-8.0 * (h + 1) / N_HEADS) for h in range(N_HEADS))
LOG2E = 1.4426950408889634
Q_SCALE = HEAD_DIM ** -0.5 * LOG2E
VMEM_LIMIT = 56 * 1024 * 1024


def _dot(a, b):
    return jnp.dot(a, b, preferred_element_type=F32)


def _dot_nt(a, b):
    return lax.dot_general(a, b, (((1,), (1,)), ((), ())), preferred_element_type=F32)


def _rms(x, g):
    return x * lax.rsqrt(jnp.mean(x * x, axis=-1, keepdims=True) + EPS) * g


def _iota(shape, dim):
    return lax.broadcasted_iota(jnp.int32, shape, dim)


def _slope_row(g, nq):
    return jnp.concatenate(
        [jnp.full((1, nq), SLOPES[HPG * g + j] * LOG2E, F32) for j in range(HPG)], axis=1)


def _slope_feature_rows(slope_row, n_rows):
    hi = slope_row.astype(BF16).astype(F32)
    r = _iota((n_rows, slope_row.shape[1]), 0)
    return jnp.where(r == 0, hi, jnp.where(r == 1, slope_row - hi, 0.0)).astype(BF16)


def _chunk_slabs(st, chunk):
    return [st[chunk * u:chunk * (u + 1)] for u in range(st.shape[0] // chunk)]


def _col_max(st, dls, chunk):
    parts = [s.reshape(chunk // SUBLANES, SUBLANES, s.shape[1]).max(axis=0) - dl
             for s, dl in zip(_chunk_slabs(st, chunk), dls)]
    return functools.reduce(jnp.maximum, parts).max(axis=0, keepdims=True)


def _probs(st, dls, m, chunk, exp_dtype=F32):
    return jnp.concatenate([jnp.exp2((s - (m + dl)).astype(exp_dtype)).astype(BF16)
                            for s, dl in zip(_chunk_slabs(st, chunk), dls)], axis=0)


def _cast_plan(weights, n_steps):
    for w in weights:
        assert w.shape[0] % (16 * n_steps) == 0, (w.shape, n_steps)
    specs = [pl.BlockSpec((w.shape[0] // n_steps, w.shape[1]), lambda i: (i, 0)) for w in weights]
    shapes = [jax.ShapeDtypeStruct(w.shape, BF16) for w in weights]
    return specs, shapes


def _cast_blocks(src_refs, dst_refs):
    for src, dst in zip(src_refs, dst_refs):
        dst[...] = src[...].astype(BF16)


def _inproj_kernel(x_ref, g_ref, lng_ref, wrow_ref, wnt_ref, *refs, tm, n_cast):
    cast_in, cast_out = refs[:n_cast], refs[len(refs) - n_cast:]
    qt_ref, kvc_ref, kaug_ref, vt_ref, u_ref, vn_ref, gatest_ref = refs[n_cast:len(refs) - n_cast]
    _cast_blocks(cast_in, cast_out)
    xn = _rms(x_ref[...], g_ref[...]).astype(BF16)
    r = _dot(xn, wrow_ref[...])
    nt = _dot_nt(wnt_ref[...], xn)
    qt_ref[0] = (nt[0:NSA_W] * Q_SCALE).astype(BF16)
    kvc_ref[...] = r[:, 0:KV_W]
    lane = _iota((tm, LANES), 1)
    key_feat = jnp.where((lane == HEAD_DIM) | (lane == HEAD_DIM + 1),
                         _iota((tm, LANES), 0) & (KEY_CHUNK - 1), 0).astype(F32)
    ones_row = jnp.where(_iota((VT_ROWS, tm), 0) == HEAD_DIM, 1.0, 0.0)
    for a in range(2 * N_GROUPS):
        kaug_ref[0, a] = (r[:, KV_W + LANES * a:KV_W + LANES * (a + 1)] + key_feat).astype(BF16)
        vt_ref[0, a] = (nt[NSA_W + VT_ROWS * a:NSA_W + VT_ROWS * (a + 1)] + ones_row).astype(BF16)
    uv = jax.nn.gelu(r[:, KV_W + 2 * N_GROUPS * LANES:])
    u_ref[...] = uv[:, :GMLP_WIDTH]
    v = uv[:, GMLP_WIDTH:]
    vc = v - jnp.mean(v, axis=-1, keepdims=True)
    vn = vc * lax.rsqrt(jnp.mean(vc * vc, axis=-1, keepdims=True) + EPS) * lng_ref[...]
    vn_ref[...] = vn.astype(BF16)
    gatest_ref[0] = jax.nn.sigmoid(nt[NSA_W + 2 * N_GROUPS * VT_ROWS:])


def _inproj(x2, norm_g, ln_g, wrow, wnt, to_cast, tm, bsz, t):
    n = x2.shape[0]
    tpb = t // tm
    row = lambda w: pl.BlockSpec((tm, w), lambda i: (i, 0))
    full = lambda a: pl.BlockSpec(a.shape, lambda i: (0,) * a.ndim)
    cast_specs, cast_shapes = _cast_plan(to_cast, n // tm)
    outs = pl.pallas_call(
        functools.partial(_inproj_kernel, tm=tm, n_cast=len(to_cast)),
        grid=(n // tm,),
        in_specs=[row(D_MODEL), full(norm_g), full(ln_g), full(wrow), full(wnt)] + cast_specs,
        out_specs=[pl.BlockSpec((1, NSA_W, tm), lambda i: (i // tpb, 0, i % tpb)),
                   row(KV_W),
                   pl.BlockSpec((1, 2 * N_GROUPS, tm, LANES), lambda i: (i // tpb, 0, i % tpb, 0)),
                   pl.BlockSpec((1, 2 * N_GROUPS, VT_ROWS, tm), lambda i: (i // tpb, 0, 0, i % tpb)),
                   row(GMLP_WIDTH), row(GMLP_WIDTH),
                   pl.BlockSpec((1, GATET_ROWS, tm), lambda i: (i // tpb, 0, i % tpb))] + cast_specs,
        out_shape=[jax.ShapeDtypeStruct((bsz, NSA_W, t), BF16),
                   jax.ShapeDtypeStruct((n, KV_W), F32),
                   jax.ShapeDtypeStruct((bsz, 2 * N_GROUPS, t, LANES), BF16),
                   jax.ShapeDtypeStruct((bsz, 2 * N_GROUPS, VT_ROWS, t), BF16),
                   jax.ShapeDtypeStruct((n, GMLP_WIDTH), F32),
                   jax.ShapeDtypeStruct((n, GMLP_WIDTH), BF16),
                   jax.ShapeDtypeStruct((bsz, GATET_ROWS, t), F32)] + cast_shapes,
        compiler_params=pltpu.CompilerParams(dimension_semantics=("arbitrary",),
                                             vmem_limit_bytes=VMEM_LIMIT),
        name="inproj",
    )(x2, norm_g, ln_g, wrow, wnt, *to_cast)
    return outs[:7], outs[7:]


def _compress_kernel(xk_ref, xv_ref, pe_ref, w1_ref, w2k_ref, w2vt_ref, kc_ref, vct_ref, *, nc):
    outs = []
    for j, x_ref in enumerate((xk_ref, xv_ref)):
        a = jnp.zeros((nc, 2 * CMP_HIDDEN), F32)
        b = jnp.zeros((nc, 2 * CMP_HIDDEN), F32)
        for t in range(0, CMP_STRIDE, 2):
            xs = [x_ref[0, pl.ds(t + e, nc, stride=CMP_STRIDE), :] for e in range(2)]

            def half_block(first):
                lhs = jnp.concatenate([(xs[e] + pe_ref[j, first + t + e:first + t + e + 1, :]).astype(BF16)
                                       for e in range(2)], axis=1)
                rhs = jnp.concatenate([w1_ref[j, first + t], w1_ref[j, first + t + 1]], axis=0)
                return _dot(lhs, rhs)

            a = a + half_block(0)
            b = b + half_block(CMP_STRIDE)
        outs.append(jax.nn.gelu(a + pltpu.roll(b, nc - 1, 0)).astype(BF16))
    k2 = _dot(outs[0], w2k_ref[...])
    v_t = _dot_nt(w2vt_ref[...], outs[1])
    lane = _iota((nc, LANES), 1)
    key_feat = jnp.where((lane == HEAD_DIM) | (lane == HEAD_DIM + 1),
                         CMP_STRIDE * (_iota((nc, LANES), 0) & (CMP_KEY_CHUNK - 1)), 0).astype(F32)
    ci =_iota((N_BLK_PAD, nc), 1) * CMP_STRIDE
    sj = _iota((N_BLK_PAD, nc), 0) * SEL_BLOCK
    overlap_t = jnp.where((ci < sj + SEL_BLOCK) & (ci + (CMP_BLOCK - 1) >= sj), 1.0, 0.0).astype(BF16)
    ones_rows = jnp.where(_iota((VT_ROWS - HEAD_DIM, nc), 0) == 0, 1.0, 0.0).astype(BF16)
    for g in range(N_GROUPS):
        kg = k2 if g == 0 else pltpu.roll(k2, HEAD_DIM, 1)
        kc_ref[0, g] = jnp.where(lane < HEAD_DIM, kg, key_feat).astype(BF16)
        vct_ref[0, g, 0:HEAD_DIM, :] = v_t[HEAD_DIM * g:HEAD_DIM * (g + 1), :].astype(BF16)
        vct_ref[0, g, HEAD_DIM:VT_ROWS, :] = ones_rows
        vct_ref[0, g, VT_ROWS:VCT_ROWS, :] = overlap_t


def _compress(kvc3, pe2, w1bd, w2k, w2vt):
    bsz, t, _ = kvc3.shape
    nc = t // CMP_STRIDE
    full = lambda a: pl.BlockSpec(a.shape, lambda b: (0,) * a.ndim)
    return pl.pallas_call(
        functools.partial(_compress_kernel, nc=nc),
        grid=(bsz,),
        in_specs=[pl.BlockSpec((1, t, LANES), lambda b: (b, 0, 0)),
                  pl.BlockSpec((1, t, LANES), lambda b: (b, 0, 1)),
                  full(pe2), full(w1bd), full(w2k), full(w2vt)],
        out_specs=[pl.BlockSpec((1, N_GROUPS, nc, LANES), lambda b: (b, 0, 0, 0)),
                   pl.BlockSpec((1, N_GROUPS, VCT_ROWS, nc), lambda b: (b, 0, 0, 0))],
        out_shape=[jax.ShapeDtypeStruct((bsz, N_GROUPS, nc, LANES), BF16),
                   jax.ShapeDtypeStruct((bsz, N_GROUPS, VCT_ROWS, nc), BF16)],
        compiler_params=pltpu.CompilerParams(dimension_semantics=("arbitrary",),
                                             vmem_limit_bytes=VMEM_LIMIT),
        name="compress",
    )(kvc3, kvc3, pe2, w1bd, w2k, w2vt)


def _cmp_topk_kernel(qt_ref, kc_ref, vct_ref, gatest_ref, ocmp_ref, selt_ref, flags_ref,
                     m_ref, acc_ref, *, ncp):
    step = pl.program_id(1)
    n_chunks = ncp // CMP_KEY_CHUNK
    chunk_tokens = CMP_KEY_CHUNK * CMP_STRIDE
    tile_heads = lambda a: jnp.concatenate([a] * HPG, axis=1)
    tail_chunks = min(2, n_chunks)
    head_chunks = n_chunks - tail_chunks
    tail_keys = tail_chunks * CMP_KEY_CHUNK
    blocks = range(CMP_BLOCKS)
    groups = range(N_GROUPS)
    items = [(bi, g) for bi in blocks for g in groups]
    start = [(step * CMP_BLOCKS + bi) * Q_BLOCK for bi in blocks]
    qlanes = [slice(Q_BLOCK * bi, Q_BLOCK * (bi + 1)) for bi in blocks]
    nck = [(start[bi] + Q_BLOCK - CMP_BLOCK) // CMP_STRIDE // CMP_KEY_CHUNK + 1 for bi in blocks]
    tail_c0 = [jnp.maximum(nck[bi] - tail_chunks, 0) for bi in blocks]
    tail_rows = [pl.ds(pl.multiple_of(tail_c0[bi] * CMP_KEY_CHUNK, CMP_KEY_CHUNK), tail_keys)
                 for bi in blocks]
    gt = gatest_ref[0]
    slope_rows = [_slope_row(g, Q_BLOCK) for g in groups]
    qas = {(bi, g): jnp.concatenate(
        [jnp.concatenate([qt_ref[0, HEAD_DIM * (HPG * g + j):HEAD_DIM * (HPG * g + j + 1), qlanes[bi]]
                          for j in range(HPG)], axis=1),
         _slope_feature_rows(slope_rows[g], LANES - HEAD_DIM)], axis=0) for bi, g in items}

    def delta(bi, g, c):
        return slope_rows[g] * (start[bi] - c * chunk_tokens).astype(F32)

    for bi in blocks:
        if head_chunks > 0:
            @pl.when(nck[bi] > tail_chunks)
            def _():
                sts = [_dot(kc_ref[0, g, 0:head_chunks * CMP_KEY_CHUNK, :], qas[bi, g]) for g in groups]
                dls = [[delta(bi, g, c) + jnp.where(c < nck[bi] - tail_chunks, 0.0, MASK_BIG)
                        for c in range(head_chunks)] for g in groups]
                ms = [_col_max(sts[g], dls[g], CMP_KEY_CHUNK) for g in groups]
                ps = [_probs(sts[g], dls[g], ms[g], CMP_KEY_CHUNK) for g in groups]
                for g in groups:
                    m_ref[bi, g] = ms[g]
                    acc_ref[bi, g] = _dot(vct_ref[0, g, :, 0:head_chunks * CMP_KEY_CHUNK], ps[g])

            @pl.when(nck[bi] <= tail_chunks)
            def _():
                m_ref[bi] = jnp.full(m_ref.shape[1:], NEG, F32)
                acc_ref[bi] = jnp.zeros(acc_ref.shape[1:], F32)

    def tail_and_select(n_rows):
        if head_chunks > 0:
            m_old = {it: m_ref[it[0], it[1]] for it in items}
            acc_old = {it: acc_ref[it[0], it[1]] for it in items}
        else:
            m_old = {it: jnp.full((1, HPG * Q_BLOCK), NEG, F32) for it in items}
            acc_old = {it: jnp.zeros((VCT_ROWS, HPG * Q_BLOCK), F32) for it in items}
        key_row = _iota((tail_keys, Q_BLOCK), 0)
        q_lane = _iota((tail_keys, Q_BLOCK), 1)
        tail_bias = []
        for bi in blocks:
            key_end = CMP_STRIDE * (tail_c0[bi] * CMP_KEY_CHUNK + key_row) + (CMP_BLOCK - 1) - start[bi]
            tail_bias.append(tile_heads(jnp.where(key_end <= q_lane, 0.0, NEG)))
        sts = {(bi, g): _dot(kc_ref[0, g, tail_rows[bi], :], qas[bi, g]) + tail_bias[bi] for bi, g in items}
        dls = {(bi, g): [delta(bi, g, tail_c0[bi] + u) for u in range(tail_chunks)] for bi, g in items}
        m_new = {it: jnp.maximum(m_old[it], _col_max(sts[it], dls[it], CMP_KEY_CHUNK)) for it in items}
        ps = {it: _probs(sts[it], dls[it], m_new[it], CMP_KEY_CHUNK) for it in items}
        accs = {(bi, g): jnp.exp2(m_old[bi, g] - m_new[bi, g]) * acc_old[bi, g]
                + _dot(vct_ref[0, g, :, tail_rows[bi]], ps[bi, g]) for bi, g in items}

        blk_n = _iota((n_rows, Q_BLOCK), 0)
        bf = blk_n.astype(F32)
        rk, cur = {}, []
        for bi in blocks:
            t_row = start[bi] + _iota((1, Q_BLOCK), 1)
            cur.append(lax.shift_right_logical(t_row, SEL_BLOCK.bit_length() - 1))
            forced = (blk_n == 0) | (blk_n == cur[bi]) | (blk_n == cur[bi] - 1)
            has_key = t_row >= CMP_BLOCK - 1
            per_head = []
            for g in groups:
                acc = accs[bi, g]
                inv_l = 1.0 / jnp.maximum(acc[HEAD_DIM:HEAD_DIM + 1], 1e-30)
                o = acc[0:HEAD_DIM] * inv_l
                imp_h = acc[VT_ROWS:VT_ROWS + n_rows] * inv_l
                imp = sum(imp_h[:, Q_BLOCK * j:Q_BLOCK * (j + 1)] for j in range(HPG))
                imp = jnp.where(has_key, imp, 0.0)
                for j in range(HPG):
                    h = HPG * g + j
                    per_head.append(jnp.where(
                        has_key, gt[3 * h:3 * h + 1, qlanes[bi]] * o[:, Q_BLOCK * j:Q_BLOCK * (j + 1)], 0.0))
                rk[bi, g] = jnp.where(blk_n <= cur[bi], jnp.where(forced, REMOVED, imp), NEG)
            ocmp_ref[0, :, qlanes[bi]] = jnp.concatenate(per_head, axis=0)

        for _ in range(N_SELECT - N_FORCED):
            for it in items:
                m = jnp.max(rk[it], axis=0, keepdims=True)
                idx = jnp.min(jnp.where(rk[it] == m, bf, float(N_BLK_PAD)), axis=0, keepdims=True)
                rk[it] = jnp.where(bf == idx, REMOVED, rk[it])
        ones8 = jnp.ones((SUBLANES, Q_BLOCK), F32)
        for bi, g in items:
            sel = jnp.where((blk_n <= cur[bi]) & (rk[bi, g] < 2.0 * NEG), 1.0, 0.0)
            selt_ref[0, g, 0:n_rows, qlanes[bi]] = sel.astype(BF16)
            flag = (_dot_nt(ones8, sel) > 0.5).astype(jnp.int32)
            if n_rows < N_BLK_PAD:
                selt_ref[0, g, n_rows:N_BLK_PAD, qlanes[bi]] = jnp.zeros((N_BLK_PAD - n_rows, Q_BLOCK), BF16)
                flag = jnp.concatenate([flag, jnp.zeros((SUBLANES, N_BLK_PAD - n_rows), jnp.int32)], axis=1)
            flags_ref[0, bi, g] = flag

    causal_rows = (start[-1] + Q_BLOCK) // SEL_BLOCK
    for n_rows in range(TOPK_ROW_STEP, N_BLK_PAD + 1, TOPK_ROW_STEP):
        @pl.when((causal_rows > n_rows - TOPK_ROW_STEP) & (causal_rows <= n_rows))
        def _():
            tail_and_select(n_rows)


def _cmp_topk(qt, kc, vct, gatest):
    bsz, _, t = qt.shape
    ncp = t // CMP_STRIDE
    nqb = t // Q_BLOCK
    step_q = CMP_BLOCKS * Q_BLOCK
    return pl.pallas_call(
        functools.partial(_cmp_topk_kernel, ncp=ncp),
        grid=(bsz, nqb // CMP_BLOCKS),
        in_specs=[pl.BlockSpec((1, NSA_W, step_q), lambda b, i: (b, 0, i)),
                  pl.BlockSpec((1, N_GROUPS, ncp, LANES), lambda b, i: (b, 0, 0, 0)),
                  pl.BlockSpec((1, N_GROUPS, VCT_ROWS, ncp), lambda b, i: (b, 0, 0, 0)),
                  pl.BlockSpec((1, GATET_ROWS, step_q), lambda b, i: (b, 0, i))],
        out_specs=[pl.BlockSpec((1, NSA_W, step_q), lambda b, i: (b, 0, i)),
                   pl.BlockSpec((1, N_GROUPS, N_BLK_PAD, step_q), lambda b, i: (b, 0, 0, i)),
                   pl.BlockSpec((1, CMP_BLOCKS, N_GROUPS, SUBLANES, N_BLK_PAD), lambda b, i: (b, i, 0, 0, 0))],
        out_shape=[jax.ShapeDtypeStruct((bsz, NSA_W, t), F32),
                   jax.ShapeDtypeStruct((bsz, N_GROUPS, N_BLK_PAD, t), BF16),
                   jax.ShapeDtypeStruct((bsz, nqb, N_GROUPS, SUBLANES, N_BLK_PAD), jnp.int32)],
        scratch_shapes=[pltpu.VMEM((CMP_BLOCKS, N_GROUPS, 1, HPG * Q_BLOCK), F32),
                        pltpu.VMEM((CMP_BLOCKS, N_GROUPS, VCT_ROWS, HPG * Q_BLOCK), F32)],
        compiler_params=pltpu.CompilerParams(dimension_semantics=("arbitrary", "arbitrary"),
                                             vmem_limit_bytes=VMEM_LIMIT),
        name="cmp_topk",
    )(qt, kc, vct, gatest)


def _slc_win_kernel(counts_ref, lists_ref, qt_ref, kaug_ref, vt_ref, oh_ref, selt_ref, gatest_ref,
                    ocmp_ref, out_ref, qaug_ref, m_ref, acc_ref, *, nqb):
    b = pl.program_id(0)
    qb = pl.program_id(1)
    step_id = b * nqb + qb
    start = qb * NSA_QB
    tile_heads = lambda a: jnp.concatenate([a] * HPG, axis=1)
    groups = range(N_GROUPS)
    gt = gatest_ref[0]
    slope_rows = [_slope_row(g, NSA_QB) for g in groups]

    def normalize(acc):
        return acc[0:HEAD_DIM] / jnp.maximum(acc[HEAD_DIM:HEAD_DIM + 1], 1e-30)

    def half_lanes(w):
        return [slice(NSA_QB * h + KEY_CHUNK * w, NSA_QB * h + KEY_CHUNK * (w + 1)) for h in range(HPG)]

    def half(a, w):
        return jnp.concatenate([a[:, s] for s in half_lanes(w)], axis=1)

    def unhalf(lo, hi):
        return jnp.concatenate([x[:, KEY_CHUNK * h:KEY_CHUNK * (h + 1)]
                                for h in range(HPG) for x in (lo, hi)], axis=1)

    def chunk_at(rel):
        pos = start + rel * KEY_CHUNK
        rows = pl.ds(pl.multiple_of(jnp.maximum(pos, 0), KEY_CHUNK), KEY_CHUNK)
        return rows, (jnp.where(pos >= 0, 0.0, MASK_BIG) if rel < 0 else 0.0)

    ki = _iota((KEY_CHUNK, KEY_CHUNK), 0)
    qi = _iota((KEY_CHUNK, KEY_CHUNK), 1)
    upper_bias = tile_heads(jnp.where(ki > qi, 0.0, NEG))
    lower_bias = tile_heads(jnp.where(ki <= qi, 0.0, NEG))
    n_mid = WINDOW // KEY_CHUNK
    mid_dist = (_iota((n_mid * KEY_CHUNK, NSA_QB), 1) + (n_mid - 1) * KEY_CHUNK
                - _iota((n_mid * KEY_CHUNK, NSA_QB), 0))
    mid_bias = tile_heads(jnp.where((mid_dist >= 0) & (mid_dist < WINDOW), 0.0, NEG))
    own_bias = tile_heads(jnp.where(_iota((KEY_CHUNK, NSA_QB), 0) <= _iota((KEY_CHUNK, NSA_QB), 1), 0.0, NEG))

    for g in range(N_GROUPS):
        qaug_ref[g, 0:HEAD_DIM, :] = jnp.concatenate(
            [qt_ref[0, HEAD_DIM * (HPG * g + j):HEAD_DIM * (HPG * g + j + 1), :] for j in range(HPG)],
            axis=1)
        qaug_ref[g, HEAD_DIM:LANES, :] = _slope_feature_rows(slope_rows[g], LANES - HEAD_DIM)
        sel_bias = ((selt_ref[0, g].astype(F32) - 1.0) * MASK_BIG).astype(BF16)
        qaug_ref[g, LANES:2 * LANES, :] = tile_heads(sel_bias)

    mid = [chunk_at(r) for r in range(1 - n_mid, 1)]
    lo_rows, lo_kill = chunk_at(-n_mid)
    hi_rows, _ = chunk_at(1)
    own_rows, _ = chunk_at(0)
    kw = lambda g: kaug_ref.at[0, N_GROUPS + g]
    vw = lambda g: vt_ref.at[0, N_GROUPS + g]
    slc_keys = lambda g, rows: jnp.concatenate([kaug_ref[0, g, rows, :], oh_ref[rows, :]], axis=1)
    q_half = lambda g, w, nrow: jnp.concatenate([qaug_ref[g, 0:nrow, s] for s in half_lanes(w)], axis=1)

    wmid_st = [_dot(jnp.concatenate([kw(g)[rows, :] for rows, _ in mid], axis=0), qaug_ref[g, 0:LANES, :])
               + mid_bias for g in groups]
    wlo_st = [_dot(kw(g)[lo_rows, :], q_half(g, 0, LANES)) + upper_bias for g in groups]
    whi_st = [_dot(kw(g)[hi_rows, :], q_half(g, 1, LANES)) + lower_bias for g in groups]
    own_st = [_dot(slc_keys(g, own_rows), qaug_ref[g]) + own_bias for g in groups]
    ohi_st = [_dot(slc_keys(g, hi_rows), q_half(g, 1, 2 * LANES)) + lower_bias for g in groups]

    wmid_dls = [[slope_rows[g] * float(-KEY_CHUNK * r) + kill
                 for r, (_, kill) in zip(range(1 - n_mid, 1), mid)] for g in groups]
    wlo_dl = [half(slope_rows[g], 0) * float(KEY_CHUNK * n_mid) + lo_kill for g in groups]
    hi_dl = [half(slope_rows[g], 1) * float(-KEY_CHUNK) for g in groups]
    neg_half = jnp.full((1, HPG * KEY_CHUNK), NEG, F32)
    win_m = [jnp.maximum(_col_max(wmid_st[g], wmid_dls[g], KEY_CHUNK),
                         unhalf(_col_max(wlo_st[g], [wlo_dl[g]], KEY_CHUNK),
                                _col_max(whi_st[g], [hi_dl[g]], KEY_CHUNK))) for g in groups]
    own_m = [jnp.maximum(_col_max(own_st[g], [0.0], KEY_CHUNK),
                         unhalf(neg_half, _col_max(ohi_st[g], [hi_dl[g]], KEY_CHUNK))) for g in groups]
    probs = functools.partial(_probs, chunk=KEY_CHUNK, exp_dtype=BF16)
    wmid_p = [probs(wmid_st[g], wmid_dls[g], win_m[g]) for g in groups]
    wlo_p = [probs(wlo_st[g], [wlo_dl[g]], half(win_m[g], 0)) for g in groups]
    whi_p = [probs(whi_st[g], [hi_dl[g]], half(win_m[g], 1)) for g in groups]
    own_p = [probs(own_st[g], [0.0], own_m[g]) for g in groups]
    ohi_p = [probs(ohi_st[g], [hi_dl[g]], half(own_m[g], 1)) for g in groups]
    o_win = []
    for g in groups:
        acc = _dot(jnp.concatenate([vw(g)[:, rows] for rows, _ in mid], axis=1), wmid_p[g])
        acc = acc + unhalf(_dot(vw(g)[:, lo_rows], wlo_p[g]), _dot(vw(g)[:, hi_rows], whi_p[g]))
        o_win.append(normalize(acc))
    zero_half = jnp.zeros((VT_ROWS, HPG * KEY_CHUNK), F32)
    for g in groups:
        m_ref[g] = own_m[g]
        acc_ref[g] = (_dot(vt_ref[0, g, :, own_rows], own_p[g])
                      + unhalf(zero_half, _dot(vt_ref[0, g, :, hi_rows], ohi_p[g])))

    def slc_body(i, carry):
        sts, vss, dlss = [], [], []
        for g in range(N_GROUPS):
            word = lists_ref[(step_id * N_GROUPS + g) * LIST_WORDS + i]
            ks, vs, dls = [], [], []
            for u in range(SLC_BATCH[g]):
                cid = lax.shift_right_logical(word, CHUNK_ID_BITS * u) & VOID_CHUNK
                valid = cid < VOID_CHUNK
                c = jnp.where(valid, cid, 0)
                rows = pl.ds(pl.multiple_of(c * KEY_CHUNK, KEY_CHUNK), KEY_CHUNK)
                ks.append(jnp.concatenate([kaug_ref[0, g, rows, :], oh_ref[rows, :]], axis=1))
                vs.append(vt_ref[0, g, :, rows])
                dls.append(slope_rows[g] * (start - c * KEY_CHUNK).astype(F32)
                           + jnp.where(valid, 0.0, MASK_BIG))
            sts.append(_dot(jnp.concatenate(ks, axis=0), qaug_ref[g]))
            vss.append(jnp.concatenate(vs, axis=1))
            dlss.append(dls)
        m_old = [m_ref[g] for g in range(N_GROUPS)]
        m_new = [jnp.maximum(m_old[g], _col_max(sts[g], dlss[g], KEY_CHUNK)) for g in range(N_GROUPS)]
        ps = [probs(sts[g], dlss[g], m_new[g]) for g in range(N_GROUPS)]
        for g in range(N_GROUPS):
            acc_ref[g] = jnp.exp2(m_old[g] - m_new[g]) * acc_ref[g] + _dot(vss[g], ps[g])
            m_ref[g] = m_new[g]
        return carry

    lax.fori_loop(0, counts_ref[step_id], slc_body, 0)

    per_head = []
    for g in range(N_GROUPS):
        o_slc = normalize(acc_ref[g])
        for j in range(HPG):
            h = HPG * g + j
            lanes = slice(NSA_QB * j, NSA_QB * (j + 1))
            per_head.append(gt[3 * h + 1:3 * h + 2, :] * o_slc[:, lanes]
                            + gt[3 * h + 2:3 * h + 3, :] * o_win[g][:, lanes])
    o_t = jnp.concatenate(per_head, axis=0)
    out_ref[0] = (ocmp_ref[0] + o_t).T.astype(BF16)


def _slc_win(counts, lists, qt, kaug, vt, onehot, selt, gatest, ocmp):
    bsz, _, t = qt.shape
    nqb = t // NSA_QB
    once = lambda shape, imap: pl.BlockSpec(shape, imap, pipeline_mode=pl.Buffered(1))
    grid_spec = pltpu.PrefetchScalarGridSpec(
        num_scalar_prefetch=2,
        grid=(bsz, nqb),
        in_specs=[pl.BlockSpec((1, NSA_W, NSA_QB), lambda b, i, *_: (b, 0, i)),
                  pl.BlockSpec((1, 2 * N_GROUPS, t, LANES), lambda b, i, *_: (b, 0, 0, 0)),
                  pl.BlockSpec((1, 2 * N_GROUPS, VT_ROWS, t), lambda b, i, *_: (b, 0, 0, 0)),
                  once((t, N_BLK_PAD), lambda b, i, *_: (0, 0)),
                  pl.BlockSpec((1, N_GROUPS, N_BLK_PAD, NSA_QB), lambda b, i, *_: (b, 0, 0, i)),
                  pl.BlockSpec((1, GATET_ROWS, NSA_QB), lambda b, i, *_: (b, 0, i)),
                  pl.BlockSpec((1, NSA_W, NSA_QB), lambda b, i, *_: (b, 0, i))],
        out_specs=pl.BlockSpec((1, NSA_QB, NSA_W), lambda b, i, *_: (b, i, 0)),
        scratch_shapes=[pltpu.VMEM((N_GROUPS, 2 * LANES, HPG * NSA_QB), BF16),
                        pltpu.VMEM((N_GROUPS, 1, HPG * NSA_QB), F32),
                        pltpu.VMEM((N_GROUPS, VT_ROWS, HPG * NSA_QB), F32)],
    )
    return pl.pallas_call(
        functools.partial(_slc_win_kernel, nqb=nqb),
        grid_spec=grid_spec,
        out_shape=jax.ShapeDtypeStruct((bsz, t, NSA_W), BF16),
        compiler_params=pltpu.CompilerParams(dimension_semantics=("arbitrary", "arbitrary"),
                                             vmem_limit_bytes=VMEM_LIMIT),
        name="slc_win",
    )(counts, lists, qt, kaug, vt, onehot, selt, gatest, ocmp)


def _merge_kernel(x_ref, onsa_ref, u_ref, vn_ref, g_ref, wm_ref, bm_ref, ws_ref, bs_ref,
                  wpa_ref, wpb_ref, wo_ref, *refs, tm, n_cast):
    cast_in, h_ref, cast_out = refs[:n_cast], refs[n_cast], refs[n_cast + 1:]
    _cast_blocks(cast_in, cast_out)
    x = x_ref[...]
    xn = _rms(x, g_ref[...]).astype(BF16)
    tril = _iota((GMLP_CHUNK, GMLP_CHUNK), 0) >= _iota((GMLP_CHUNK, GMLP_CHUNK), 1)
    sgu_rows = []
    for c in range(tm // GMLP_CHUNK):
        rows = slice(GMLP_CHUNK * c, GMLP_CHUNK * (c + 1))
        cols = []
        for g in range(GMLP_GROUPS):
            lanes = slice(LANES * g, LANES * (g + 1))
            w = jnp.where(tril, ws_ref[g], 0.0).astype(BF16)
            cols.append(_dot(w, vn_ref[rows, lanes]) + bs_ref[:, g:g + 1])
        sgu_rows.append(u_ref[rows, :] * jnp.concatenate(cols, axis=1))
    o_sgu = jnp.concatenate(sgu_rows, axis=0).astype(BF16)
    mg = jax.nn.sigmoid(_dot(xn, wm_ref[...]) + bm_ref[...])
    mixed = (mg[:, :D_MODEL] * _dot(onsa_ref[...], wpa_ref[...])
             + mg[:, D_MODEL:] * _dot(o_sgu, wpb_ref[...]))
    h_ref[...] = x + _dot(mixed.astype(BF16), wo_ref[...])


def _merge(x2, onsa2, u2, vn2, norm_g, wm, bm, ws, bs_t, wpa, wpb, wo, to_cast, tm):
    n = x2.shape[0]
    row = lambda w: pl.BlockSpec((tm, w), lambda i: (i, 0))
    full = lambda a: pl.BlockSpec(a.shape, lambda i: (0,) * a.ndim)
    cast_specs, cast_shapes = _cast_plan(to_cast, n // tm)
    outs = pl.pallas_call(
        functools.partial(_merge_kernel, tm=tm, n_cast=len(to_cast)),
        grid=(n // tm,),
        in_specs=[row(D_MODEL), row(NSA_W), row(GMLP_WIDTH), row(GMLP_WIDTH), full(norm_g),
                  full(wm), full(bm), full(ws), full(bs_t), full(wpa), full(wpb), full(wo)] + cast_specs,
        out_specs=[row(D_MODEL)] + cast_specs,
        out_shape=[jax.ShapeDtypeStruct((n, D_MODEL), F32)] + cast_shapes,
        compiler_params=pltpu.CompilerParams(dimension_semantics=("arbitrary",),
                                             vmem_limit_bytes=VMEM_LIMIT),
        name="merge",
    )(x2, onsa2, u2, vn2, norm_g, wm, bm, ws, bs_t, wpa, wpb, wo, *to_cast)
    return outs[0], outs[1:]


def _memkv_kernel(mem_ref, g_ref, w_ref, out_ref):
    out_ref[0] = _dot(_rms(mem_ref[0], g_ref[...]).astype(BF16), w_ref[...]).astype(BF16)


def _memkv(mem, norm_g, w):
    bsz, nm, _ = mem.shape
    return pl.pallas_call(
        _memkv_kernel,
        grid=(bsz,),
        in_specs=[pl.BlockSpec((1, nm, D_MODEL), lambda b: (b, 0, 0)),
                  pl.BlockSpec(norm_g.shape, lambda b: (0, 0)),
                  pl.BlockSpec(w.shape, lambda b: (0, 0))],
        out_specs=pl.BlockSpec((1, nm, 2 * MEM_W), lambda b: (b, 0, 0)),
        out_shape=jax.ShapeDtypeStruct((bsz, nm, 2 * MEM_W), BF16),
        compiler_params=pltpu.CompilerParams(dimension_semantics=("arbitrary",),
                                             vmem_limit_bytes=VMEM_LIMIT),
        name="memkv",
    )(mem, norm_g, w)


def _xattn_kernel(h_ref, g_ref, wq_ref, mkv_ref, wo_ref, out_ref, *, tm):
    halves = [slice(0, tm // 2), slice(tm // 2, tm)]
    head_lanes = [slice(MEM_HEAD_DIM * a, MEM_HEAD_DIM * (a + 1)) for a in range(MEM_HEADS)]
    hs = [h_ref[rows, :] for rows in halves]
    hqs = [(_dot(_rms(h, g_ref[...]).astype(BF16), wq_ref[...]) * (MEM_HEAD_DIM ** -0.5 * LOG2E)).astype(BF16)
           for h in hs]
    ss = [[_dot_nt(hq[:, lanes], mkv_ref[0, :, lanes]) for lanes in head_lanes] for hq in hqs]
    es = [[jnp.exp2(s - jnp.max(s, axis=-1, keepdims=True)) for s in s_half] for s_half in ss]
    os = []
    for e_half in es:
        heads = []
        for a, e in enumerate(e_half):
            v = mkv_ref[0, :, MEM_W + MEM_HEAD_DIM * a:MEM_W + MEM_HEAD_DIM * (a + 1)]
            heads.append(_dot(e.astype(BF16), v) * (1.0 / jnp.sum(e, axis=-1, keepdims=True)))
        os.append(jnp.concatenate(heads, axis=1).astype(BF16))
    for rows, h, o in zip(halves, hs, os):
        out_ref[rows, :] = h + _dot(o, wo_ref[...])


def _xattn(h2d, norm_g, wq, mkv, wo, tm, rows_per_batch):
    n = h2d.shape[0]
    nm = mkv.shape[1]
    tiles_per_batch = rows_per_batch // tm
    full = lambda a: pl.BlockSpec(a.shape, lambda i: (0,) * a.ndim)
    return pl.pallas_call(
        functools.partial(_xattn_kernel, tm=tm),
        grid=(n // tm,),
        in_specs=[pl.BlockSpec((tm, D_MODEL), lambda i: (i, 0)), full(norm_g), full(wq),
                  pl.BlockSpec((1, nm, 2 * MEM_W), lambda i: (i // tiles_per_batch, 0, 0)),
                  full(wo)],
        out_specs=pl.BlockSpec((tm, D_MODEL), lambda i: (i, 0)),
        out_shape=jax.ShapeDtypeStruct((n, D_MODEL), F32),
        compiler_params=pltpu.CompilerParams(dimension_semantics=("arbitrary",),
                                             vmem_limit_bytes=VMEM_LIMIT),
        name="xattn",
    )(h2d, norm_g, wq, mkv, wo)


def _ffn_kernel(h_ref, g_ref, wgu_ref, wd_ref, gf_ref, out_ref, *, d_ff):
    tm = h_ref.shape[0]
    halves = [slice(0, tm // 2), slice(tm // 2, tm)]
    hs = [h_ref[rows, :] for rows in halves]
    hns = [_rms(h, g_ref[...]).astype(BF16) for h in hs]
    gates = [_dot(hn, wgu_ref[:, :d_ff]) for hn in hns]
    ups = [_dot(hn, wgu_ref[:, d_ff:]) for hn in hns]
    acts = [(jax.nn.silu(gate) * up).astype(BF16) for gate, up in zip(gates, ups)]
    for rows, h, act in zip(halves, hs, acts):
        out_ref[rows, :] = _rms(h + _dot(act, wd_ref[...]), gf_ref[...])


def _ffn(h2d, norm_g, wgu, wd, norm_f, tm):
    n = h2d.shape[0]
    d_ff = wd.shape[0]
    full = lambda a: pl.BlockSpec(a.shape, lambda i: (0,) * a.ndim)
    once = lambda a: pl.BlockSpec(a.shape, lambda i: (0,) * a.ndim, pipeline_mode=pl.Buffered(1))
    return pl.pallas_call(
        functools.partial(_ffn_kernel, d_ff=d_ff),
        grid=(n // tm,),
        in_specs=[pl.BlockSpec((tm, D_MODEL), lambda i: (i, 0)), full(norm_g), once(wgu), once(wd),
                  full(norm_f)],
        out_specs=pl.BlockSpec((tm, D_MODEL), lambda i: (i, 0)),
        out_shape=jax.ShapeDtypeStruct((n, D_MODEL), F32),
        compiler_params=pltpu.CompilerParams(dimension_semantics=("arbitrary",),
                                             vmem_limit_bytes=VMEM_LIMIT),
        name="ffn",
    )(h2d, norm_g, wgu, wd, norm_f)


def _block_diag2(w):
    z = jnp.zeros_like(w)
    return jnp.concatenate([jnp.concatenate([w, z], axis=-1), jnp.concatenate([z, w], axis=-1)], axis=-2)


def _chunk_lists(flags):
    bsz = flags.shape[0]
    n_chunks = N_KEY_CHUNKS
    per_step = NSA_QB // Q_BLOCK
    nqb = flags.shape[1] // per_step
    f = flags[:, :, :, 0, :].reshape(bsz, nqb, per_step, N_GROUPS, n_chunks, 2).max(axis=(2, 5))
    cid = jnp.arange(n_chunks, dtype=jnp.int32)
    own = (NSA_QB // KEY_CHUNK) * jnp.arange(nqb, dtype=jnp.int32)[None, :, None, None]
    active = (f > 0) & (cid < own)
    n_active = active.sum(axis=-1)
    slot = jnp.cumsum(active, axis=-1) - 1
    hit = active[..., :, None] & (slot[..., :, None] == cid)
    ids = jnp.sum(jnp.where(hit, cid[:, None], 0), axis=-2)
    ids = jnp.where(cid < n_active[..., None], ids, VOID_CHUNK)
    words, n_batches = [], []
    for g, batch in enumerate(SLC_BATCH):
        n_batches.append((n_active[:, :, g] + batch - 1) // batch)
        padded = jnp.pad(ids[:, :, g], ((0, 0), (0, 0), (0, LIST_WORDS * batch - n_chunks)),
                         constant_values=VOID_CHUNK).reshape(bsz, nqb, LIST_WORDS, batch)
        words.append(functools.reduce(jnp.bitwise_or,
                                      [padded[..., u] << (CHUNK_ID_BITS * u) for u in range(batch)]))
    n_batches = functools.reduce(jnp.maximum, n_batches)
    words = jnp.stack(words, axis=2)
    return n_batches.reshape(-1).astype(jnp.int32), words.reshape(-1).astype(jnp.int32)


def kernel(x, mem, norm_mix, w_in, w_cmp_k1, w_cmp_k2, w_cmp_v1, w_cmp_v2, pe_cmp_k, pe_cmp_v, ln_sgu, w_spatial, b_spatial, w_proj_a, w_proj_b, w_merge, b_merge, w_out, norm_mem_q, norm_mem_kv, w_mq, w_mkv, w_mo, norm_ffn, w_gate_up, w_down, norm_final):
    bsz, t, d = x.shape
    depth = norm_mix.shape[0]
    assert d == D_MODEL and t % Q_BLOCK == 0 and t // SEL_BLOCK <= N_BLK_PAD
    assert t // SEL_BLOCK >= N_SELECT and depth == 1 and t % 1024 == 0
    n = bsz * t
    tm = 512
    h = x.reshape(n, d)
    c0, c1, c2, c3 = NSA_W, NSA_W + KV_W, NSA_W + 3 * KV_W, NSA_W + 3 * KV_W + 2 * GMLP_WIDTH
    onehot = (jnp.arange(t)[:, None] // SEL_BLOCK == jnp.arange(N_BLK_PAD)[None, :]).astype(BF16)
    pad_cols = lambda w, width: jnp.pad(w, ((0, 0), (0, width - w.shape[1])))
    for l in range(depth):
        wi = w_in[l]
        wk, wv = [], []
        for branch in range(2):
            base = c1 + KV_W * branch
            for g in range(N_GROUPS):
                wk.append(pad_cols(wi[:, base + HEAD_DIM * g:base + HEAD_DIM * (g + 1)], LANES))
                v0 = base + N_GROUPS * HEAD_DIM + HEAD_DIM * g
                wv.append(pad_cols(wi[:, v0:v0 + HEAD_DIM], VT_ROWS))
        wrow = jnp.concatenate([wi[:, c0:c1]] + wk + [wi[:, c2:c3]], axis=1).astype(BF16)
        wnt = jnp.concatenate([wi[:, :c0]] + wv + [pad_cols(wi[:, c3:], GATET_ROWS)], axis=1).T.astype(BF16)
        (qt, kvc2, kaug, vt, u2, vn2, gatest), (wm_b, wpa_b, wpb_b, wo_b, wmq_b, wmkv_b, wmo_b) = _inproj(
            h, norm_mix[l][None], ln_sgu[l][None], wrow, wnt,
            [w_merge[l], w_proj_a[l], w_proj_b[l], w_out[l], w_mq[l], w_mkv[l], w_mo[l]], 2 * tm, bsz, t)

        pe2 = jnp.stack([pe_cmp_k[l], pe_cmp_v[l]])
        pe2 = jnp.concatenate([pe2, pe2], axis=-1)
        w1 = jnp.stack([w_cmp_k1[l], w_cmp_v1[l]]).reshape(2, CMP_BLOCK, HEAD_DIM, CMP_HIDDEN)
        w1bd = _block_diag2(w1).astype(BF16)
        w2k = _block_diag2(w_cmp_k2[l]).astype(BF16)
        w2vt = _block_diag2(w_cmp_v2[l]).T.astype(BF16)
        kc, vct = _compress(kvc2.reshape(bsz, t, KV_W), pe2, w1bd, w2k, w2vt)
        ocmp, selt, flags = _cmp_topk(qt, kc, vct, gatest)
        counts, lists = _chunk_lists(flags)
        onsa = _slc_win(counts, lists, qt, kaug, vt, onehot, selt, gatest, ocmp)

        h, (wgu_b, wd_b) = _merge(h, onsa.reshape(n, NSA_W), u2, vn2, norm_mix[l][None], wm_b,
                                  b_merge[l][None], w_spatial[l], b_spatial[l].T, wpa_b, wpb_b, wo_b,
                                  [w_gate_up[l], w_down[l]], 2 * tm)

        mkv = _memkv(mem, norm_mem_kv[l][None], wmkv_b)
        h = _xattn(h, norm_mem_q[l][None], wmq_b, mkv, wmo_b, 2 * tm, t)
        h = _ffn(h, norm_ffn[l][None], wgu_b, wd_b, norm_final[None], tm)
    return h.reshape(bsz, t, d)
```

```python
import functools

import jax
import jax.numpy as jnp
from jax import lax
from jax.experimental import pallas as pl
from jax.experimental.pallas import tpu as pltpu

F32 = jnp.float32
BF16 = jnp.bfloat16

LANES = 128
SUBLANES = 8
D_MODEL = 1024
N_HEADS = 8
HEAD_DIM = 64
N_GROUPS = 2
HPG = N_HEADS // N_GROUPS
CMP_BLOCK = 32
CMP_STRIDE = 16
CMP_HIDDEN = 128
SEL_BLOCK = 64
N_SELECT = 16
WINDOW = 512
Q_BLOCK = 256
KEY_CHUNK = 128
N_BLK_PAD = 128
NSA_W = N_HEADS * HEAD_DIM
KV_W = 2 * N_GROUPS * HEAD_DIM
GMLP_WIDTH = 512
GMLP_GROUPS = 4
GMLP_CHUNK = 128
MEM_HEADS = 4
MEM_HEAD_DIM = 128
MEM_W = MEM_HEADS * MEM_HEAD_DIM
NSA_QB = 256
SLC_BATCH = (4, 5)
N_KEY_CHUNKS = N_BLK_PAD // 2
CHUNK_ID_BITS = 6
VOID_CHUNK = N_KEY_CHUNKS - 1
LIST_WORDS = -(-N_KEY_CHUNKS // min(SLC_BATCH))
VT_ROWS = 80
VCT_ROWS = VT_ROWS + N_BLK_PAD
CMP_KEY_CHUNK = 128
N_FORCED = 3
TOPK_ROW_STEP = 64
CMP_BLOCKS = 2
GATET_ROWS = 32
MASK_BIG = 1e30
EPS = 1e-6
NEG = -1e30
REMOVED = -3e38
SLOPES = tuple(2.0 ** (-8.0 * (h + 1) / N_HEADS) for h in range(N_HEADS))
LOG2E = 1.4426950408889634
Q_SCALE = HEAD_DIM ** -0.5 * LOG2E
VMEM_LIMIT = 56 * 1024 * 1024


def _dot(a, b):
    return jnp.dot(a, b, preferred_element_type=F32)


def _dot_nt(a, b):
    return lax.dot_general(a, b, (((1,), (1,)), ((), ())), preferred_element_type=F32)


def _rms(x, g):
    return x * lax.rsqrt(jnp.mean(x * x, axis=-1, keepdims=True) + EPS) * g


def _iota(shape, dim):
    return lax.broadcasted_iota(jnp.int32, shape, dim)


def _slope_row(g, nq):
    return jnp.concatenate(
        [jnp.full((1, nq), SLOPES[HPG * g + j] * LOG2E, F32) for j in range(HPG)], axis=1)


def _slope_feature_rows(slope_row, n_rows):
    hi = slope_row.astype(BF16).astype(F32)
    r = _iota((n_rows, slope_row.shape[1]), 0)
    return jnp.where(r == 0, hi, jnp.where(r == 1, slope_row - hi, 0.0)).astype(BF16)


def _chunk_slabs(st, chunk):
    return [st[chunk * u:chunk * (u + 1)] for u in range(st.shape[0] // chunk)]


def _col_max(st, dls, chunk):
    parts = [s.reshape(chunk // SUBLANES, SUBLANES, s.shape[1]).max(axis=0) - dl
             for s, dl in zip(_chunk_slabs(st, chunk), dls)]
    return functools.reduce(jnp.maximum, parts).max(axis=0, keepdims=True)


def _probs(st, dls, m, chunk, exp_dtype=F32):
    return jnp.concatenate([jnp.exp2((s - (m + dl)).astype(exp_dtype)).astype(BF16)
                            for s, dl in zip(_chunk_slabs(st, chunk), dls)], axis=0)


def _cast_plan(weights, n_steps):
    for w in weights:
        assert w.shape[0] % (16 * n_steps) == 0, (w.shape, n_steps)
    specs = [pl.BlockSpec((w.shape[0] // n_steps, w.shape[1]), lambda i: (i, 0)) for w in weights]
    shapes = [jax.ShapeDtypeStruct(w.shape, BF16) for w in weights]
    return specs, shapes


def _cast_blocks(src_refs, dst_refs):
    for src, dst in zip(src_refs, dst_refs):
        dst[...] = src[...].astype(BF16)


def _inproj_kernel(x_ref, g_ref, lng_ref, wrow_ref, wnt_ref, *refs, tm, n_cast):
    cast_in, cast_out = refs[:n_cast], refs[len(refs) - n_cast:]
    qt_ref, kvc_ref, kaug_ref, vt_ref, u_ref, vn_ref, gatest_ref = refs[n_cast:len(refs) - n_cast]
    _cast_blocks(cast_in, cast_out)
    xn = _rms(x_ref[...], g_ref[...]).astype(BF16)
    r = _dot(xn, wrow_ref[...])
    nt = _dot_nt(wnt_ref[...], xn)
    qt_ref[0] = (nt[0:NSA_W] * Q_SCALE).astype(BF16)
    kvc_ref[...] = r[:, 0:KV_W]
    lane = _iota((tm, LANES), 1)
    key_feat = jnp.where((lane == HEAD_DIM) | (lane == HEAD_DIM + 1),
                         _iota((tm, LANES), 0) & (KEY_CHUNK - 1), 0).astype(F32)
    ones_row = jnp.where(_iota((VT_ROWS, tm), 0) == HEAD_DIM, 1.0, 0.0)
    for a in range(2 * N_GROUPS):
        kaug_ref[0, a] = (r[:, KV_W + LANES * a:KV_W + LANES * (a + 1)] + key_feat).astype(BF16)
        vt_ref[0, a] = (nt[NSA_W + VT_ROWS * a:NSA_W + VT_ROWS * (a + 1)] + ones_row).astype(BF16)
    uv = jax.nn.gelu(r[:, KV_W + 2 * N_GROUPS * LANES:])
    u_ref[...] = uv[:, :GMLP_WIDTH]
    v = uv[:, GMLP_WIDTH:]
    vc = v - jnp.mean(v, axis=-1, keepdims=True)
    vn = vc * lax.rsqrt(jnp.mean(vc * vc, axis=-1, keepdims=True) + EPS) * lng_ref[...]
    vn_ref[...] = vn.astype(BF16)
    gatest_ref[0] = jax.nn.sigmoid(nt[NSA_W + 2 * N_GROUPS * VT_ROWS:])


def _inproj(x2, norm_g, ln_g, wrow, wnt, to_cast, tm, bsz, t):
    n = x2.shape[0]
    tpb = t // tm
    row = lambda w: pl.BlockSpec((tm, w), lambda i: (i, 0))
    full = lambda a: pl.BlockSpec(a.shape, lambda i: (0,) * a.ndim)
    cast_specs, cast_shapes = _cast_plan(to_cast, n // tm)
    outs = pl.pallas_call(
        functools.partial(_inproj_kernel, tm=tm, n_cast=len(to_cast)),
        grid=(n // tm,),
        in_specs=[row(D_MODEL), full(norm_g), full(ln_g), full(wrow), full(wnt)] + cast_specs,
        out_specs=[pl.BlockSpec((1, NSA_W, tm), lambda i: (i // tpb, 0, i % tpb)),
                   row(KV_W),
                   pl.BlockSpec((1, 2 * N_GROUPS, tm, LANES), lambda i: (i // tpb, 0, i % tpb, 0)),
                   pl.BlockSpec((1, 2 * N_GROUPS, VT_ROWS, tm), lambda i: (i // tpb, 0, 0, i % tpb)),
                   row(GMLP_WIDTH), row(GMLP_WIDTH),
                   pl.BlockSpec((1, GATET_ROWS, tm), lambda i: (i // tpb, 0, i % tpb))] + cast_specs,
        out_shape=[jax.ShapeDtypeStruct((bsz, NSA_W, t), BF16),
                   jax.ShapeDtypeStruct((n, KV_W), F32),
                   jax.ShapeDtypeStruct((bsz, 2 * N_GROUPS, t, LANES), BF16),
                   jax.ShapeDtypeStruct((bsz, 2 * N_GROUPS, VT_ROWS, t), BF16),
                   jax.ShapeDtypeStruct((n, GMLP_WIDTH), F32),
                   jax.ShapeDtypeStruct((n, GMLP_WIDTH), BF16),
                   jax.ShapeDtypeStruct((bsz, GATET_ROWS, t), F32)] + cast_shapes,
        compiler_params=pltpu.CompilerParams(dimension_semantics=("arbitrary",),
                                             vmem_limit_bytes=VMEM_LIMIT),
        name="inproj",
    )(x2, norm_g, ln_g, wrow, wnt, *to_cast)
    return outs[:7], outs[7:]


def _compress_kernel(xk_ref, xv_ref, pe_ref, w1_ref, w2k_ref, w2vt_ref, kc_ref, vct_ref, *, nc):
    outs = []
    for j, x_ref in enumerate((xk_ref, xv_ref)):
        a = jnp.zeros((nc, 2 * CMP_HIDDEN), F32)
        b = jnp.zeros((nc, 2 * CMP_HIDDEN), F32)
        for t in range(0, CMP_STRIDE, 2):
            xs = [x_ref[0, pl.ds(t + e, nc, stride=CMP_STRIDE), :] for e in range(2)]

            def half_block(first):
                lhs = jnp.concatenate([(xs[e] + pe_ref[j, first + t + e:first + t + e + 1, :]).astype(BF16)
                                       for e in range(2)], axis=1)
                rhs = jnp.concatenate([w1_ref[j, first + t], w1_ref[j, first + t + 1]], axis=0)
                return _dot(lhs, rhs)

            a = a + half_block(0)
            b = b + half_block(CMP_STRIDE)
        outs.append(jax.nn.gelu(a + pltpu.roll(b, nc - 1, 0)).astype(BF16))
    k2 = _dot(outs[0], w2k_ref[...])
    v_t = _dot_nt(w2vt_ref[...], outs[1])
    lane = _iota((nc, LANES), 1)
    key_feat = jnp.where((lane == HEAD_DIM) | (lane == HEAD_DIM + 1),
                         CMP_STRIDE * (_iota((nc, LANES), 0) & (CMP_KEY_CHUNK - 1)), 0).astype(F32)
    ci =_iota((N_BLK_PAD, nc), 1) * CMP_STRIDE
    sj = _iota((N_BLK_PAD, nc), 0) * SEL_BLOCK
    overlap_t = jnp.where((ci < sj + SEL_BLOCK) & (ci + (CMP_BLOCK - 1) >= sj), 1.0, 0.0).astype(BF16)
    ones_rows = jnp.where(_iota((VT_ROWS - HEAD_DIM, nc), 0) == 0, 1.0, 0.0).astype(BF16)
    for g in range(N_GROUPS):
        kg = k2 if g == 0 else pltpu.roll(k2, HEAD_DIM, 1)
        kc_ref[0, g] = jnp.where(lane < HEAD_DIM, kg, key_feat).astype(BF16)
        vct_ref[0, g, 0:HEAD_DIM, :] = v_t[HEAD_DIM * g:HEAD_DIM * (g + 1), :].astype(BF16)
        vct_ref[0, g, HEAD_DIM:VT_ROWS, :] = ones_rows
        vct_ref[0, g, VT_ROWS:VCT_ROWS, :] = overlap_t


def _compress(kvc3, pe2, w1bd, w2k, w2vt):
    bsz, t, _ = kvc3.shape
    nc = t // CMP_STRIDE
    full = lambda a: pl.BlockSpec(a.shape, lambda b: (0,) * a.ndim)
    return pl.pallas_call(
        functools.partial(_compress_kernel, nc=nc),
        grid=(bsz,),
        in_specs=[pl.BlockSpec((1, t, LANES), lambda b: (b, 0, 0)),
                  pl.BlockSpec((1, t, LANES), lambda b: (b, 0, 1)),
                  full(pe2), full(w1bd), full(w2k), full(w2vt)],
        out_specs=[pl.BlockSpec((1, N_GROUPS, nc, LANES), lambda b: (b, 0, 0, 0)),
                   pl.BlockSpec((1, N_GROUPS, VCT_ROWS, nc), lambda b: (b, 0, 0, 0))],
        out_shape=[jax.ShapeDtypeStruct((bsz, N_GROUPS, nc, LANES), BF16),
                   jax.ShapeDtypeStruct((bsz, N_GROUPS, VCT_ROWS, nc), BF16)],
        compiler_params=pltpu.CompilerParams(dimension_semantics=("arbitrary",),
                                             vmem_limit_bytes=VMEM_LIMIT),
        name="compress",
    )(kvc3, kvc3, pe2, w1bd, w2k, w2vt)


def _cmp_topk_kernel(qt_ref, kc_ref, vct_ref, gatest_ref, ocmp_ref, selt_ref, flags_ref,
                     m_ref, acc_ref, *, ncp):
    step = pl.program_id(1)
    n_chunks = ncp // CMP_KEY_CHUNK
    chunk_tokens = CMP_KEY_CHUNK * CMP_STRIDE
    tile_heads = lambda a: jnp.concatenate([a] * HPG, axis=1)
    tail_chunks = min(2, n_chunks)
    head_chunks = n_chunks - tail_chunks
    tail_keys = tail_chunks * CMP_KEY_CHUNK
    blocks = range(CMP_BLOCKS)
    groups = range(N_GROUPS)
    items = [(bi, g) for bi in blocks for g in groups]
    start = [(step * CMP_BLOCKS + bi) * Q_BLOCK for bi in blocks]
    qlanes = [slice(Q_BLOCK * bi, Q_BLOCK * (bi + 1)) for bi in blocks]
    nck = [(start[bi] + Q_BLOCK - CMP_BLOCK) // CMP_STRIDE // CMP_KEY_CHUNK + 1 for bi in blocks]
    tail_c0 = [jnp.maximum(nck[bi] - tail_chunks, 0) for bi in blocks]
    tail_rows = [pl.ds(pl.multiple_of(tail_c0[bi] * CMP_KEY_CHUNK, CMP_KEY_CHUNK), tail_keys)
                 for bi in blocks]
    gt = gatest_ref[0]
    slope_rows = [_slope_row(g, Q_BLOCK) for g in groups]
    qas = {(bi, g): jnp.concatenate(
        [jnp.concatenate([qt_ref[0, HEAD_DIM * (HPG * g + j):HEAD_DIM * (HPG * g + j + 1), qlanes[bi]]
                          for j in range(HPG)], axis=1),
         _slope_feature_rows(slope_rows[g], LANES - HEAD_DIM)], axis=0) for bi, g in items}

    def delta(bi, g, c):
        return slope_rows[g] * (start[bi] - c * chunk_tokens).astype(F32)

    for bi in blocks:
        if head_chunks > 0:
            @pl.when(nck[bi] > tail_chunks)
            def _():
                sts = [_dot(kc_ref[0, g, 0:head_chunks * CMP_KEY_CHUNK, :], qas[bi, g]) for g in groups]
                dls = [[delta(bi, g, c) + jnp.where(c < nck[bi] - tail_chunks, 0.0, MASK_BIG)
                        for c in range(head_chunks)] for g in groups]
                ms = [_col_max(sts[g], dls[g], CMP_KEY_CHUNK) for g in groups]
                ps = [_probs(sts[g], dls[g], ms[g], CMP_KEY_CHUNK, BF16) for g in groups]
                for g in groups:
                    m_ref[bi, g] = ms[g]
                    acc_ref[bi, g] = _dot(vct_ref[0, g, :, 0:head_chunks * CMP_KEY_CHUNK], ps[g])

            @pl.when(nck[bi] <= tail_chunks)
            def _():
                m_ref[bi] = jnp.full(m_ref.shape[1:], NEG, F32)
                acc_ref[bi] = jnp.zeros(acc_ref.shape[1:], F32)

    def tail_and_select(n_rows):
        if head_chunks > 0:
            m_old = {it: m_ref[it[0], it[1]] for it in items}
            acc_old = {it: acc_ref[it[0], it[1]] for it in items}
        else:
            m_old = {it: jnp.full((1, HPG * Q_BLOCK), NEG, F32) for it in items}
            acc_old = {it: jnp.zeros((VCT_ROWS, HPG * Q_BLOCK), F32) for it in items}
        key_row = _iota((tail_keys, Q_BLOCK), 0)
        q_lane = _iota((tail_keys, Q_BLOCK), 1)
        tail_bias = []
        for bi in blocks:
            key_end = CMP_STRIDE * (tail_c0[bi] * CMP_KEY_CHUNK + key_row) + (CMP_BLOCK - 1) - start[bi]
            tail_bias.append(tile_heads(jnp.where(key_end <= q_lane, 0.0, NEG)))
        sts = {(bi, g): _dot(kc_ref[0, g, tail_rows[bi], :], qas[bi, g]) + tail_bias[bi] for bi, g in items}
        dls = {(bi, g): [delta(bi, g, tail_c0[bi] + u) for u in range(tail_chunks)] for bi, g in items}
        m_new = {it: jnp.maximum(m_old[it], _col_max(sts[it], dls[it], CMP_KEY_CHUNK)) for it in items}
        ps = {it: _probs(sts[it], dls[it], m_new[it], CMP_KEY_CHUNK, BF16) for it in items}
        accs = {(bi, g): jnp.exp2(m_old[bi, g] - m_new[bi, g]) * acc_old[bi, g]
                + _dot(vct_ref[0, g, :, tail_rows[bi]], ps[bi, g]) for bi, g in items}

        blk_n = _iota((n_rows, Q_BLOCK), 0)
        bf = blk_n.astype(F32)
        rk, cur = {}, []
        for bi in blocks:
            t_row = start[bi] + _iota((1, Q_BLOCK), 1)
            cur.append(lax.shift_right_logical(t_row, SEL_BLOCK.bit_length() - 1))
            forced = (blk_n == 0) | (blk_n == cur[bi]) | (blk_n == cur[bi] - 1)
            has_key = t_row >= CMP_BLOCK - 1
            per_head = []
            for g in groups:
                acc = accs[bi, g]
                inv_l = 1.0 / jnp.maximum(acc[HEAD_DIM:HEAD_DIM + 1], 1e-30)
                o = acc[0:HEAD_DIM] * inv_l
                imp_h = acc[VT_ROWS:VT_ROWS + n_rows] * inv_l
                imp = sum(imp_h[:, Q_BLOCK * j:Q_BLOCK * (j + 1)] for j in range(HPG))
                imp = jnp.where(has_key, imp, 0.0)
                for j in range(HPG):
                    h = HPG * g + j
                    per_head.append(jnp.where(
                        has_key, gt[3 * h:3 * h + 1, qlanes[bi]] * o[:, Q_BLOCK * j:Q_BLOCK * (j + 1)], 0.0))
                rk[bi, g] = jnp.where(blk_n <= cur[bi], jnp.where(forced, REMOVED, imp), NEG)
            ocmp_ref[0, :, qlanes[bi]] = jnp.concatenate(per_head, axis=0)

        for _ in range(N_SELECT - N_FORCED):
            for it in items:
                m = jnp.max(rk[it], axis=0, keepdims=True)
                idx = jnp.min(jnp.where(rk[it] == m, bf, float(N_BLK_PAD)), axis=0, keepdims=True)
                rk[it] = jnp.where(bf == idx, REMOVED, rk[it])
        ones8 = jnp.ones((SUBLANES, Q_BLOCK), F32)
        for bi, g in items:
            sel = jnp.where((blk_n <= cur[bi]) & (rk[bi, g] < 2.0 * NEG), 1.0, 0.0)
            selt_ref[0, g, 0:n_rows, qlanes[bi]] = sel.astype(BF16)
            flag = (_dot_nt(ones8, sel) > 0.5).astype(jnp.int32)
            if n_rows < N_BLK_PAD:
                selt_ref[0, g, n_rows:N_BLK_PAD, qlanes[bi]] = jnp.zeros((N_BLK_PAD - n_rows, Q_BLOCK), BF16)
                flag = jnp.concatenate([flag, jnp.zeros((SUBLANES, N_BLK_PAD - n_rows), jnp.int32)], axis=1)
            flags_ref[0, bi, g] = flag

    causal_rows = (start[-1] + Q_BLOCK) // SEL_BLOCK
    for n_rows in range(TOPK_ROW_STEP, N_BLK_PAD + 1, TOPK_ROW_STEP):
        @pl.when((causal_rows > n_rows - TOPK_ROW_STEP) & (causal_rows <= n_rows))
        def _():
            tail_and_select(n_rows)


def _cmp_topk(qt, kc, vct, gatest):
    bsz, _, t = qt.shape
    ncp = t // CMP_STRIDE
    nqb = t // Q_BLOCK
    step_q = CMP_BLOCKS * Q_BLOCK
    return pl.pallas_call(
        functools.partial(_cmp_topk_kernel, ncp=ncp),
        grid=(bsz, nqb // CMP_BLOCKS),
        in_specs=[pl.BlockSpec((1, NSA_W, step_q), lambda b, i: (b, 0, i)),
                  pl.BlockSpec((1, N_GROUPS, ncp, LANES), lambda b, i: (b, 0, 0, 0)),
                  pl.BlockSpec((1, N_GROUPS, VCT_ROWS, ncp), lambda b, i: (b, 0, 0, 0)),
                  pl.BlockSpec((1, GATET_ROWS, step_q), lambda b, i: (b, 0, i))],
        out_specs=[pl.BlockSpec((1, NSA_W, step_q), lambda b, i: (b, 0, i)),
                   pl.BlockSpec((1, N_GROUPS, N_BLK_PAD, step_q), lambda b, i: (b, 0, 0, i)),
                   pl.BlockSpec((1, CMP_BLOCKS, N_GROUPS, SUBLANES, N_BLK_PAD), lambda b, i: (b, i, 0, 0, 0))],
        out_shape=[jax.ShapeDtypeStruct((bsz, NSA_W, t), F32),
                   jax.ShapeDtypeStruct((bsz, N_GROUPS, N_BLK_PAD, t), BF16),
                   jax.ShapeDtypeStruct((bsz, nqb, N_GROUPS, SUBLANES, N_BLK_PAD), jnp.int32)],
        scratch_shapes=[pltpu.VMEM((CMP_BLOCKS, N_GROUPS, 1, HPG * Q_BLOCK), F32),
                        pltpu.VMEM((CMP_BLOCKS, N_GROUPS, VCT_ROWS, HPG * Q_BLOCK), F32)],
        compiler_params=pltpu.CompilerParams(dimension_semantics=("arbitrary", "arbitrary"),
                                             vmem_limit_bytes=VMEM_LIMIT),
        name="cmp_topk",
    )(qt, kc, vct, gatest)


def _slc_win_kernel(counts_ref, lists_ref, qt_ref, kaug_ref, vt_ref, oh_ref, selt_ref, gatest_ref,
                    ocmp_ref, out_ref, qaug_ref, m_ref, acc_ref, *, nqb):
    b = pl.program_id(0)
    qb = pl.program_id(1)
    step_id = b * nqb + qb
    start = qb * NSA_QB
    tile_heads = lambda a: jnp.concatenate([a] * HPG, axis=1)
    groups = range(N_GROUPS)
    gt = gatest_ref[0]
    slope_rows = [_slope_row(g, NSA_QB) for g in groups]

    def normalize(acc):
        return acc[0:HEAD_DIM] / jnp.maximum(acc[HEAD_DIM:HEAD_DIM + 1], 1e-30)

    def half_lanes(w):
        return [slice(NSA_QB * h + KEY_CHUNK * w, NSA_QB * h + KEY_CHUNK * (w + 1)) for h in range(HPG)]

    def half(a, w):
        return jnp.concatenate([a[:, s] for s in half_lanes(w)], axis=1)

    def unhalf(lo, hi):
        return jnp.concatenate([x[:, KEY_CHUNK * h:KEY_CHUNK * (h + 1)]
                                for h in range(HPG) for x in (lo, hi)], axis=1)

    def chunk_at(rel):
        pos = start + rel * KEY_CHUNK
        rows = pl.ds(pl.multiple_of(jnp.maximum(pos, 0), KEY_CHUNK), KEY_CHUNK)
        return rows, (jnp.where(pos >= 0, 0.0, MASK_BIG) if rel < 0 else 0.0)

    ki = _iota((KEY_CHUNK, KEY_CHUNK), 0)
    qi = _iota((KEY_CHUNK, KEY_CHUNK), 1)
    upper_bias = tile_heads(jnp.where(ki > qi, 0.0, NEG))
    lower_bias = tile_heads(jnp.where(ki <= qi, 0.0, NEG))
    n_mid = WINDOW // KEY_CHUNK
    mid_dist = (_iota((n_mid * KEY_CHUNK, NSA_QB), 1) + (n_mid - 1) * KEY_CHUNK
                - _iota((n_mid * KEY_CHUNK, NSA_QB), 0))
    mid_bias = tile_heads(jnp.where((mid_dist >= 0) & (mid_dist < WINDOW), 0.0, NEG))
    own_bias = tile_heads(jnp.where(_iota((KEY_CHUNK, NSA_QB), 0) <= _iota((KEY_CHUNK, NSA_QB), 1), 0.0, NEG))

    for g in range(N_GROUPS):
        qaug_ref[g, 0:HEAD_DIM, :] = jnp.concatenate(
            [qt_ref[0, HEAD_DIM * (HPG * g + j):HEAD_DIM * (HPG * g + j + 1), :] for j in range(HPG)],
            axis=1)
        qaug_ref[g, HEAD_DIM:LANES, :] = _slope_feature_rows(slope_rows[g], LANES - HEAD_DIM)
        sel_bias = ((selt_ref[0, g].astype(F32) - 1.0) * MASK_BIG).astype(BF16)
        qaug_ref[g, LANES:2 * LANES, :] = tile_heads(sel_bias)

    mid = [chunk_at(r) for r in range(1 - n_mid, 1)]
    lo_rows, lo_kill = chunk_at(-n_mid)
    hi_rows, _ = chunk_at(1)
    own_rows, _ = chunk_at(0)
    kw = lambda g: kaug_ref.at[0, N_GROUPS + g]
    vw = lambda g: vt_ref.at[0, N_GROUPS + g]
    slc_keys = lambda g, rows: jnp.concatenate([kaug_ref[0, g, rows, :], oh_ref[rows, :]], axis=1)
    q_half = lambda g, w, nrow: jnp.concatenate([qaug_ref[g, 0:nrow, s] for s in half_lanes(w)], axis=1)

    wmid_st = [_dot(jnp.concatenate([kw(g)[rows, :] for rows, _ in mid], axis=0), qaug_ref[g, 0:LANES, :])
               + mid_bias for g in groups]
    wlo_st = [_dot(kw(g)[lo_rows, :], q_half(g, 0, LANES)) + upper_bias for g in groups]
    whi_st = [_dot(kw(g)[hi_rows, :], q_half(g, 1, LANES)) + lower_bias for g in groups]
    own_st = [_dot(slc_keys(g, own_rows), qaug_ref[g]) + own_bias for g in groups]
    ohi_st = [_dot(slc_keys(g, hi_rows), q_half(g, 1, 2 * LANES)) + lower_bias for g in groups]

    wmid_dls = [[slope_rows[g] * float(-KEY_CHUNK * r) + kill
                 for r, (_, kill) in zip(range(1 - n_mid, 1), mid)] for g in groups]
    wlo_dl = [half(slope_rows[g], 0) * float(KEY_CHUNK * n_mid) + lo_kill for g in groups]
    hi_dl = [half(slope_rows[g], 1) * float(-KEY_CHUNK) for g in groups]
    neg_half = jnp.full((1, HPG * KEY_CHUNK), NEG, F32)
    win_m = [jnp.maximum(_col_max(wmid_st[g], wmid_dls[g], KEY_CHUNK),
                         unhalf(_col_max(wlo_st[g], [wlo_dl[g]], KEY_CHUNK),
                                _col_max(whi_st[g], [hi_dl[g]], KEY_CHUNK))) for g in groups]
    own_m = [jnp.maximum(_col_max(own_st[g], [0.0], KEY_CHUNK),
                         unhalf(neg_half, _col_max(ohi_st[g], [hi_dl[g]], KEY_CHUNK))) for g in groups]
    probs = functools.partial(_probs, chunk=KEY_CHUNK, exp_dtype=BF16)
    wmid_p = [probs(wmid_st[g], wmid_dls[g], win_m[g]) for g in groups]
    wlo_p = [probs(wlo_st[g], [wlo_dl[g]], half(win_m[g], 0)) for g in groups]
    whi_p = [probs(whi_st[g], [hi_dl[g]], half(win_m[g], 1)) for g in groups]
    own_p = [probs(own_st[g], [0.0], own_m[g]) for g in groups]
    ohi_p = [probs(ohi_st[g], [hi_dl[g]], half(own_m[g], 1)) for g in groups]
    o_win = []
    for g in groups:
        acc = _dot(jnp.concatenate([vw(g)[:, rows] for rows, _ in mid], axis=1), wmid_p[g])
        acc = acc + unhalf(_dot(vw(g)[:, lo_rows], wlo_p[g]), _dot(vw(g)[:, hi_rows], whi_p[g]))
        o_win.append(normalize(acc))
    zero_half = jnp.zeros((VT_ROWS, HPG * KEY_CHUNK), F32)
    for g in groups:
        m_ref[g] = own_m[g]
        acc_ref[g] = (_dot(vt_ref[0, g, :, own_rows], own_p[g])
                      + unhalf(zero_half, _dot(vt_ref[0, g, :, hi_rows], ohi_p[g])))

    def slc_body(i, carry):
        sts, vss, dlss = [], [], []
        for g in range(N_GROUPS):
            word = lists_ref[(step_id * N_GROUPS + g) * LIST_WORDS + i]
            ks, vs, dls = [], [], []
            for u in range(SLC_BATCH[g]):
                cid = lax.shift_right_logical(word, CHUNK_ID_BITS * u) & VOID_CHUNK
                valid = cid < VOID_CHUNK
                c = jnp.where(valid, cid, 0)
                rows = pl.ds(pl.multiple_of(c * KEY_CHUNK, KEY_CHUNK), KEY_CHUNK)
                ks.append(jnp.concatenate([kaug_ref[0, g, rows, :], oh_ref[rows, :]], axis=1))
                vs.append(vt_ref[0, g, :, rows])
                dls.append(slope_rows[g] * (start - c * KEY_CHUNK).astype(F32)
                           + jnp.where(valid, 0.0, MASK_BIG))
            sts.append(_dot(jnp.concatenate(ks, axis=0), qaug_ref[g]))
            vss.append(jnp.concatenate(vs, axis=1))
            dlss.append(dls)
        m_old = [m_ref[g] for g in range(N_GROUPS)]
        m_new = [jnp.maximum(m_old[g], _col_max(sts[g], dlss[g], KEY_CHUNK)) for g in range(N_GROUPS)]
        ps = [probs(sts[g], dlss[g], m_new[g]) for g in range(N_GROUPS)]
        for g in range(N_GROUPS):
            acc_ref[g] = jnp.exp2(m_old[g] - m_new[g]) * acc_ref[g] + _dot(vss[g], ps[g])
            m_ref[g] = m_new[g]
        return carry

    lax.fori_loop(0, counts_ref[step_id], slc_body, 0)

    per_head = []
    for g in range(N_GROUPS):
        o_slc = normalize(acc_ref[g])
        for j in range(HPG):
            h = HPG * g + j
            lanes = slice(NSA_QB * j, NSA_QB * (j + 1))
            per_head.append(gt[3 * h + 1:3 * h + 2, :] * o_slc[:, lanes]
                            + gt[3 * h + 2:3 * h + 3, :] * o_win[g][:, lanes])
    o_t = jnp.concatenate(per_head, axis=0)
    out_ref[0] = (ocmp_ref[0] + o_t).T.astype(BF16)


def _slc_win(counts, lists, qt, kaug, vt, onehot, selt, gatest, ocmp):
    bsz, _, t = qt.shape
    nqb = t // NSA_QB
    once = lambda shape, imap: pl.BlockSpec(shape, imap, pipeline_mode=pl.Buffered(1))
    grid_spec = pltpu.PrefetchScalarGridSpec(
        num_scalar_prefetch=2,
        grid=(bsz, nqb),
        in_specs=[pl.BlockSpec((1, NSA_W, NSA_QB), lambda b, i, *_: (b, 0, i)),
                  pl.BlockSpec((1, 2 * N_GROUPS, t, LANES), lambda b, i, *_: (b, 0, 0, 0)),
                  pl.BlockSpec((1, 2 * N_GROUPS, VT_ROWS, t), lambda b, i, *_: (b, 0, 0, 0)),
                  once((t, N_BLK_PAD), lambda b, i, *_: (0, 0)),
                  pl.BlockSpec((1, N_GROUPS, N_BLK_PAD, NSA_QB), lambda b, i, *_: (b, 0, 0, i)),
                  pl.BlockSpec((1, GATET_ROWS, NSA_QB), lambda b, i, *_: (b, 0, i)),
                  pl.BlockSpec((1, NSA_W, NSA_QB), lambda b, i, *_: (b, 0, i))],
        out_specs=pl.BlockSpec((1, NSA_QB, NSA_W), lambda b, i, *_: (b, i, 0)),
        scratch_shapes=[pltpu.VMEM((N_GROUPS, 2 * LANES, HPG * NSA_QB), BF16),
                        pltpu.VMEM((N_GROUPS, 1, HPG * NSA_QB), F32),
                        pltpu.VMEM((N_GROUPS, VT_ROWS, HPG * NSA_QB), F32)],
    )
    return pl.pallas_call(
        functools.partial(_slc_win_kernel, nqb=nqb),
        grid_spec=grid_spec,
        out_shape=jax.ShapeDtypeStruct((bsz, t, NSA_W), BF16),
        compiler_params=pltpu.CompilerParams(dimension_semantics=("arbitrary", "arbitrary"),
                                             vmem_limit_bytes=VMEM_LIMIT),
        name="slc_win",
    )(counts, lists, qt, kaug, vt, onehot, selt, gatest, ocmp)


def _merge_kernel(x_ref, onsa_ref, u_ref, vn_ref, g_ref, wm_ref, bm_ref, ws_ref, bs_ref,
                  wpa_ref, wpb_ref, wo_ref, *refs, tm, n_cast):
    cast_in, h_ref, cast_out = refs[:n_cast], refs[n_cast], refs[n_cast + 1:]
    _cast_blocks(cast_in, cast_out)
    x = x_ref[...]
    xn = _rms(x, g_ref[...]).astype(BF16)
    tril = _iota((GMLP_CHUNK, GMLP_CHUNK), 0) >= _iota((GMLP_CHUNK, GMLP_CHUNK), 1)
    sgu_rows = []
    for c in range(tm // GMLP_CHUNK):
        rows = slice(GMLP_CHUNK * c, GMLP_CHUNK * (c + 1))
        cols = []
        for g in range(GMLP_GROUPS):
            lanes = slice(LANES * g, LANES * (g + 1))
            w = jnp.where(tril, ws_ref[g], 0.0).astype(BF16)
            cols.append(_dot(w, vn_ref[rows, lanes]) + bs_ref[:, g:g + 1])
        sgu_rows.append(u_ref[rows, :] * jnp.concatenate(cols, axis=1))
    o_sgu = jnp.concatenate(sgu_rows, axis=0).astype(BF16)
    mg = jax.nn.sigmoid(_dot(xn, wm_ref[...]) + bm_ref[...])
    mixed = (mg[:, :D_MODEL] * _dot(onsa_ref[...], wpa_ref[...])
             + mg[:, D_MODEL:] * _dot(o_sgu, wpb_ref[...]))
    h_ref[...] = x + _dot(mixed.astype(BF16), wo_ref[...])


def _merge(x2, onsa2, u2, vn2, norm_g, wm, bm, ws, bs_t, wpa, wpb, wo, to_cast, tm):
    n = x2.shape[0]
    row = lambda w: pl.BlockSpec((tm, w), lambda i: (i, 0))
    full = lambda a: pl.BlockSpec(a.shape, lambda i: (0,) * a.ndim)
    cast_specs, cast_shapes = _cast_plan(to_cast, n // tm)
    outs = pl.pallas_call(
        functools.partial(_merge_kernel, tm=tm, n_cast=len(to_cast)),
        grid=(n // tm,),
        in_specs=[row(D_MODEL), row(NSA_W), row(GMLP_WIDTH), row(GMLP_WIDTH), full(norm_g),
                  full(wm), full(bm), full(ws), full(bs_t), full(wpa), full(wpb), full(wo)] + cast_specs,
        out_specs=[row(D_MODEL)] + cast_specs,
        out_shape=[jax.ShapeDtypeStruct((n, D_MODEL), F32)] + cast_shapes,
        compiler_params=pltpu.CompilerParams(dimension_semantics=("arbitrary",),
                                             vmem_limit_bytes=VMEM_LIMIT),
        name="merge",
    )(x2, onsa2, u2, vn2, norm_g, wm, bm, ws, bs_t, wpa, wpb, wo, *to_cast)
    return outs[0], outs[1:]


def _memkv_kernel(mem_ref, g_ref, w_ref, out_ref):
    out_ref[0] = _dot(_rms(mem_ref[0], g_ref[...]).astype(BF16), w_ref[...]).astype(BF16)


def _memkv(mem, norm_g, w):
    bsz, nm, _ = mem.shape
    return pl.pallas_call(
        _memkv_kernel,
        grid=(bsz,),
        in_specs=[pl.BlockSpec((1, nm, D_MODEL), lambda b: (b, 0, 0)),
                  pl.BlockSpec(norm_g.shape, lambda b: (0, 0)),
                  pl.BlockSpec(w.shape, lambda b: (0, 0))],
        out_specs=pl.BlockSpec((1, nm, 2 * MEM_W), lambda b: (b, 0, 0)),
        out_shape=jax.ShapeDtypeStruct((bsz, nm, 2 * MEM_W), BF16),
        compiler_params=pltpu.CompilerParams(dimension_semantics=("arbitrary",),
                                             vmem_limit_bytes=VMEM_LIMIT),
        name="memkv",
    )(mem, norm_g, w)


def _xattn_kernel(h_ref, g_ref, wq_ref, mkv_ref, wo_ref, out_ref, *, tm):
    halves = [slice(0, tm // 2), slice(tm // 2, tm)]
    head_lanes = [slice(MEM_HEAD_DIM * a, MEM_HEAD_DIM * (a + 1)) for a in range(MEM_HEADS)]
    hs = [h_ref[rows, :] for rows in halves]
    hqs = [(_dot(_rms(h, g_ref[...]).astype(BF16), wq_ref[...]) * (MEM_HEAD_DIM ** -0.5 * LOG2E)).astype(BF16)
           for h in hs]
    ss = [[_dot_nt(hq[:, lanes], mkv_ref[0, :, lanes]) for lanes in head_lanes] for hq in hqs]
    es = [[jnp.exp2(s - jnp.max(s, axis=-1, keepdims=True)) for s in s_half] for s_half in ss]
    os = []
    for e_half in es:
        heads = []
        for a, e in enumerate(e_half):
            v = mkv_ref[0, :, MEM_W + MEM_HEAD_DIM * a:MEM_W + MEM_HEAD_DIM * (a + 1)]
            heads.append(_dot(e.astype(BF16), v) * (1.0 / jnp.sum(e, axis=-1, keepdims=True)))
        os.append(jnp.concatenate(heads, axis=1).astype(BF16))
    for rows, h, o in zip(halves, hs, os):
        out_ref[rows, :] = h + _dot(o, wo_ref[...])


def _xattn(h2d, norm_g, wq, mkv, wo, tm, rows_per_batch):
    n = h2d.shape[0]
    nm = mkv.shape[1]
    tiles_per_batch = rows_per_batch // tm
    full = lambda a: pl.BlockSpec(a.shape, lambda i: (0,) * a.ndim)
    return pl.pallas_call(
        functools.partial(_xattn_kernel, tm=tm),
        grid=(n // tm,),
        in_specs=[pl.BlockSpec((tm, D_MODEL), lambda i: (i, 0)), full(norm_g), full(wq),
                  pl.BlockSpec((1, nm, 2 * MEM_W), lambda i: (i // tiles_per_batch, 0, 0)),
                  full(wo)],
        out_specs=pl.BlockSpec((tm, D_MODEL), lambda i: (i, 0)),
        out_shape=jax.ShapeDtypeStruct((n, D_MODEL), F32),
        compiler_params=pltpu.CompilerParams(dimension_semantics=("arbitrary",),
                                             vmem_limit_bytes=VMEM_LIMIT),
        name="xattn",
    )(h2d, norm_g, wq, mkv, wo)


def _ffn_kernel(h_ref, g_ref, wgu_ref, wd_ref, gf_ref, out_ref, *, d_ff):
    tm = h_ref.shape[0]
    halves = [slice(0, tm // 2), slice(tm // 2, tm)]
    hs = [h_ref[rows, :] for rows in halves]
    hns = [_rms(h, g_ref[...]).astype(BF16) for h in hs]
    gates = [_dot(hn, wgu_ref[:, :d_ff]) for hn in hns]
    ups = [_dot(hn, wgu_ref[:, d_ff:]) for hn in hns]
    acts = [(jax.nn.silu(gate) * up).astype(BF16) for gate, up in zip(gates, ups)]
    for rows, h, act in zip(halves, hs, acts):
        out_ref[rows, :] = _rms(h + _dot(act, wd_ref[...]), gf_ref[...])


def _ffn(h2d, norm_g, wgu, wd, norm_f, tm):
    n = h2d.shape[0]
    d_ff = wd.shape[0]
    full = lambda a: pl.BlockSpec(a.shape, lambda i: (0,) * a.ndim)
    once = lambda a: pl.BlockSpec(a.shape, lambda i: (0,) * a.ndim, pipeline_mode=pl.Buffered(1))
    return pl.pallas_call(
        functools.partial(_ffn_kernel, d_ff=d_ff),
        grid=(n // tm,),
        in_specs=[pl.BlockSpec((tm, D_MODEL), lambda i: (i, 0)), full(norm_g), once(wgu), once(wd),
                  full(norm_f)],
        out_specs=pl.BlockSpec((tm, D_MODEL), lambda i: (i, 0)),
        out_shape=jax.ShapeDtypeStruct((n, D_MODEL), F32),
        compiler_params=pltpu.CompilerParams(dimension_semantics=("arbitrary",),
                                             vmem_limit_bytes=VMEM_LIMIT),
        name="ffn",
    )(h2d, norm_g, wgu, wd, norm_f)


def _block_diag2(w):
    z = jnp.zeros_like(w)
    return jnp.concatenate([jnp.concatenate([w, z], axis=-1), jnp.concatenate([z, w], axis=-1)], axis=-2)


def _chunk_lists(flags):
    bsz = flags.shape[0]
    n_chunks = N_KEY_CHUNKS
    per_step = NSA_QB // Q_BLOCK
    nqb = flags.shape[1] // per_step
    f = flags[:, :, :, 0, :].reshape(bsz, nqb, per_step, N_GROUPS, n_chunks, 2).max(axis=(2, 5))
    cid = jnp.arange(n_chunks, dtype=jnp.int32)
    own = (NSA_QB // KEY_CHUNK) * jnp.arange(nqb, dtype=jnp.int32)[None, :, None, None]
    active = (f > 0) & (cid < own)
    n_active = active.sum(axis=-1)
    slot = jnp.cumsum(active, axis=-1) - 1
    hit = active[..., :, None] & (slot[..., :, None] == cid)
    ids = jnp.sum(jnp.where(hit, cid[:, None], 0), axis=-2)
    ids = jnp.where(cid < n_active[..., None], ids, VOID_CHUNK)
    words, n_batches = [], []
    for g, batch in enumerate(SLC_BATCH):
        n_batches.append((n_active[:, :, g] + batch - 1) // batch)
        padded = jnp.pad(ids[:, :, g], ((0, 0), (0, 0), (0, LIST_WORDS * batch - n_chunks)),
                         constant_values=VOID_CHUNK).reshape(bsz, nqb, LIST_WORDS, batch)
        words.append(functools.reduce(jnp.bitwise_or,
                                      [padded[..., u] << (CHUNK_ID_BITS * u) for u in range(batch)]))
    n_batches = functools.reduce(jnp.maximum, n_batches)
    words = jnp.stack(words, axis=2)
    return n_batches.reshape(-1).astype(jnp.int32), words.reshape(-1).astype(jnp.int32)


def kernel(x, mem, norm_mix, w_in, w_cmp_k1, w_cmp_k2, w_cmp_v1, w_cmp_v2, pe_cmp_k, pe_cmp_v, ln_sgu, w_spatial, b_spatial, w_proj_a, w_proj_b, w_merge, b_merge, w_out, norm_mem_q, norm_mem_kv, w_mq, w_mkv, w_mo, norm_ffn, w_gate_up, w_down, norm_final):
    bsz, t, d = x.shape
    depth = norm_mix.shape[0]
    assert d == D_MODEL and t % Q_BLOCK == 0 and t // SEL_BLOCK <= N_BLK_PAD
    assert t // SEL_BLOCK >= N_SELECT and depth == 1 and t % 1024 == 0
    n = bsz * t
    tm = 512
    h = x.reshape(n, d)
    c0, c1, c2, c3 = NSA_W, NSA_W + KV_W, NSA_W + 3 * KV_W, NSA_W + 3 * KV_W + 2 * GMLP_WIDTH
    onehot = (jnp.arange(t)[:, None] // SEL_BLOCK == jnp.arange(N_BLK_PAD)[None, :]).astype(BF16)
    pad_cols = lambda w, width: jnp.pad(w, ((0, 0), (0, width - w.shape[1])))
    for l in range(depth):
        wi = w_in[l]
        wk, wv = [], []
        for branch in range(2):
            base = c1 + KV_W * branch
            for g in range(N_GROUPS):
                wk.append(pad_cols(wi[:, base + HEAD_DIM * g:base + HEAD_DIM * (g + 1)], LANES))
                v0 = base + N_GROUPS * HEAD_DIM + HEAD_DIM * g
                wv.append(pad_cols(wi[:, v0:v0 + HEAD_DIM], VT_ROWS))
        wrow = jnp.concatenate([wi[:, c0:c1]] + wk + [wi[:, c2:c3]], axis=1).astype(BF16)
        wnt = jnp.concatenate([wi[:, :c0]] + wv + [pad_cols(wi[:, c3:], GATET_ROWS)], axis=1).T.astype(BF16)
        (qt, kvc2, kaug, vt, u2, vn2, gatest), (wm_b, wpa_b, wpb_b, wo_b, wmq_b, wmkv_b, wmo_b) = _inproj(
            h, norm_mix[l][None], ln_sgu[l][None], wrow, wnt,
            [w_merge[l], w_proj_a[l], w_proj_b[l], w_out[l], w_mq[l], w_mkv[l], w_mo[l]], 2 * tm, bsz, t)

        pe2 = jnp.stack([pe_cmp_k[l], pe_cmp_v[l]])
        pe2 = jnp.concatenate([pe2, pe2], axis=-1)
        w1 = jnp.stack([w_cmp_k1[l], w_cmp_v1[l]]).reshape(2, CMP_BLOCK, HEAD_DIM, CMP_HIDDEN)
        w1bd = _block_diag2(w1).astype(BF16)
        w2k = _block_diag2(w_cmp_k2[l]).astype(BF16)
        w2vt = _block_diag2(w_cmp_v2[l]).T.astype(BF16)
        kc, vct = _compress(kvc2.reshape(bsz, t, KV_W), pe2, w1bd, w2k, w2vt)
        ocmp, selt, flags = _cmp_topk(qt, kc, vct, gatest)
        counts, lists = _chunk_lists(flags)
        onsa = _slc_win(counts, lists, qt, kaug, vt, onehot, selt, gatest, ocmp)

        h, (wgu_b, wd_b) = _merge(h, onsa.reshape(n, NSA_W), u2, vn2, norm_mix[l][None], wm_b,
                                  b_merge[l][None], w_spatial[l], b_spatial[l].T, wpa_b, wpb_b, wo_b,
                                  [w_gate_up[l], w_down[l]], 2 * tm)

        mkv = _memkv(mem, norm_mem_kv[l][None], wmkv_b)
        h = _xattn(h, norm_mem_q[l][None], wmq_b, mkv, wmo_b, 2 * tm, t)
        h = _ffn(h, norm_ffn[l][None], wgu_b, wd_b, norm_final[None], tm)
    return h.reshape(bsz, t, d)
```

```python
import functools

import jax
import jax.numpy as jnp
from jax import lax
from jax.experimental import pallas as pl
from jax.experimental.pallas import tpu as pltpu

F32 = jnp.float32
BF16 = jnp.bfloat16

LANES = 128
SUBLANES = 8
D_MODEL = 1024
N_HEADS = 8
HEAD_DIM = 64
N_GROUPS = 2
HPG = N_HEADS // N_GROUPS
CMP_BLOCK = 32
CMP_STRIDE = 16
CMP_HIDDEN = 128
SEL_BLOCK = 64
N_SELECT = 16
WINDOW = 512
Q_BLOCK = 256
KEY_CHUNK = 128
N_BLK_PAD = 128
NSA_W = N_HEADS * HEAD_DIM
KV_W = 2 * N_GROUPS * HEAD_DIM
GMLP_WIDTH = 512
GMLP_GROUPS = 4
GMLP_CHUNK = 128
MEM_HEADS = 4
MEM_HEAD_DIM = 128
MEM_W = MEM_HEADS * MEM_HEAD_DIM
NSA_QB = 256
SLC_BATCH = (4, 5)
N_KEY_CHUNKS = N_BLK_PAD // 2
CHUNK_ID_BITS = 6
VOID_CHUNK = N_KEY_CHUNKS - 1
LIST_WORDS = -(-N_KEY_CHUNKS // min(SLC_BATCH))
VT_ROWS = 80
VCT_ROWS = VT_ROWS + N_BLK_PAD
CMP_KEY_CHUNK = 128
N_FORCED = 3
TOPK_ROW_STEP = 64
CMP_BLOCKS = 2
GATET_ROWS = 32
MASK_BIG = 1e30
EPS = 1e-6
NEG = -1e30
REMOVED = -3e38
SLOPES = tuple(2.0 ** (-8.0 * (h + 1) / N_HEADS) for h in range(N_HEADS))
LOG2E = 1.4426950408889634
Q_SCALE = HEAD_DIM ** -0.5 * LOG2E
VMEM_LIMIT = 56 * 1024 * 1024


def _dot(a, b):
    return jnp.dot(a, b, preferred_element_type=F32)


def _dot_nt(a, b):
    return lax.dot_general(a, b, (((1,), (1,)), ((), ())), preferred_element_type=F32)


def _rms(x, g):
    return x * lax.rsqrt(jnp.mean(x * x, axis=-1, keepdims=True) + EPS) * g


def _iota(shape, dim):
    return lax.broadcasted_iota(jnp.int32, shape, dim)


def _slope_row(g, nq):
    return jnp.concatenate(
        [jnp.full((1, nq), SLOPES[HPG * g + j] * LOG2E, F32) for j in range(HPG)], axis=1)


def _slope_feature_rows(slope_row, n_rows):
    hi = slope_row.astype(BF16).astype(F32)
    r = _iota((n_rows, slope_row.shape[1]), 0)
    return jnp.where(r == 0, hi, jnp.where(r == 1, slope_row - hi, 0.0)).astype(BF16)


def _chunk_slabs(st, chunk):
    return [st[chunk * u:chunk * (u + 1)] for u in range(st.shape[0] // chunk)]


def _col_max(st, dls, chunk):
    parts = [s.reshape(chunk // SUBLANES, SUBLANES, s.shape[1]).max(axis=0) - dl
             for s, dl in zip(_chunk_slabs(st, chunk), dls)]
    return functools.reduce(jnp.maximum, parts).max(axis=0, keepdims=True)


def _probs(st, dls, m, chunk, exp_dtype=F32):
    return jnp.concatenate([jnp.exp2((s - (m + dl)).astype(exp_dtype)).astype(BF16)
                            for s, dl in zip(_chunk_slabs(st, chunk), dls)], axis=0)


def _cast_plan(weights, n_steps):
    for w in weights:
        assert w.shape[0] % (16 * n_steps) == 0, (w.shape, n_steps)
    specs = [pl.BlockSpec((w.shape[0] // n_steps, w.shape[1]), lambda i: (i, 0)) for w in weights]
    shapes = [jax.ShapeDtypeStruct(w.shape, BF16) for w in weights]
    return specs, shapes


def _cast_blocks(src_refs, dst_refs):
    for src, dst in zip(src_refs, dst_refs):
        dst[...] = src[...].astype(BF16)


def _inproj_kernel(x_ref, g_ref, lng_ref, wrow_ref, wnt_ref, *refs, tm, tpb, n_cast):
    cast_in, cast_out = refs[:n_cast], refs[len(refs) - n_cast:]
    qt_ref, kvc_ref, kslc_ref, kwin_ref, vt_ref, u_ref, vn_ref, gatest_ref = refs[n_cast:len(refs) - n_cast]
    _cast_blocks(cast_in, cast_out)
    xn = _rms(x_ref[...], g_ref[...]).astype(BF16)
    r = _dot(xn, wrow_ref[...])
    nt = _dot_nt(wnt_ref[...], xn)
    qt_ref[0] = (nt[0:NSA_W] * Q_SCALE).astype(BF16)
    kvc_ref[...] = r[:, 0:KV_W]
    lane = _iota((tm, LANES), 1)
    key_feat = jnp.where((lane == HEAD_DIM) | (lane == HEAD_DIM + 1),
                         _iota((tm, LANES), 0) & (KEY_CHUNK - 1), 0).astype(F32)
    ones_row = jnp.where(_iota((VT_ROWS, tm), 0) == HEAD_DIM, 1.0, 0.0)
    token = (pl.program_id(0) % tpb) * tm + _iota((tm, N_BLK_PAD), 0)
    block_onehot = jnp.where(_iota((tm, N_BLK_PAD), 1) == lax.shift_right_logical(token, SEL_BLOCK.bit_length() - 1),
                             1.0, 0.0).astype(BF16)
    for a in range(2 * N_GROUPS):
        keys = (r[:, KV_W + LANES * a:KV_W + LANES * (a + 1)] + key_feat).astype(BF16)
        if a < N_GROUPS:
            kslc_ref[0, a] = jnp.concatenate([keys, block_onehot], axis=1)
        else:
            kwin_ref[0, a - N_GROUPS] = keys
        vt_ref[0, a] = (nt[NSA_W + VT_ROWS * a:NSA_W + VT_ROWS * (a + 1)] + ones_row).astype(BF16)
    uv = jax.nn.gelu(r[:, KV_W + 2 * N_GROUPS * LANES:])
    u_ref[...] = uv[:, :GMLP_WIDTH]
    v = uv[:, GMLP_WIDTH:]
    vc = v - jnp.mean(v, axis=-1, keepdims=True)
    vn = vc * lax.rsqrt(jnp.mean(vc * vc, axis=-1, keepdims=True) + EPS) * lng_ref[...]
    vn_ref[...] = vn.astype(BF16)
    gatest_ref[0] = jax.nn.sigmoid(nt[NSA_W + 2 * N_GROUPS * VT_ROWS:])


def _inproj(x2, norm_g, ln_g, wrow, wnt, to_cast, tm, bsz, t):
    n = x2.shape[0]
    tpb = t // tm
    row = lambda w: pl.BlockSpec((tm, w), lambda i: (i, 0))
    full = lambda a: pl.BlockSpec(a.shape, lambda i: (0,) * a.ndim)
    cast_specs, cast_shapes = _cast_plan(to_cast, n // tm)
    outs = pl.pallas_call(
        functools.partial(_inproj_kernel, tm=tm, tpb=tpb, n_cast=len(to_cast)),
        grid=(n // tm,),
        in_specs=[row(D_MODEL), full(norm_g), full(ln_g), full(wrow), full(wnt)] + cast_specs,
        out_specs=[pl.BlockSpec((1, NSA_W, tm), lambda i: (i // tpb, 0, i % tpb)),
                   row(KV_W),
                   pl.BlockSpec((1, N_GROUPS, tm, 2 * LANES), lambda i: (i // tpb, 0, i % tpb, 0)),
                   pl.BlockSpec((1, N_GROUPS, tm, LANES), lambda i: (i // tpb, 0, i % tpb, 0)),
                   pl.BlockSpec((1, 2 * N_GROUPS, VT_ROWS, tm), lambda i: (i // tpb, 0, 0, i % tpb)),
                   row(GMLP_WIDTH), row(GMLP_WIDTH),
                   pl.BlockSpec((1, GATET_ROWS, tm), lambda i: (i // tpb, 0, i % tpb))] + cast_specs,
        out_shape=[jax.ShapeDtypeStruct((bsz, NSA_W, t), BF16),
                   jax.ShapeDtypeStruct((n, KV_W), F32),
                   jax.ShapeDtypeStruct((bsz, N_GROUPS, t, 2 * LANES), BF16),
                   jax.ShapeDtypeStruct((bsz, N_GROUPS, t, LANES), BF16),
                   jax.ShapeDtypeStruct((bsz, 2 * N_GROUPS, VT_ROWS, t), BF16),
                   jax.ShapeDtypeStruct((n, GMLP_WIDTH), F32),
                   jax.ShapeDtypeStruct((n, GMLP_WIDTH), BF16),
                   jax.ShapeDtypeStruct((bsz, GATET_ROWS, t), F32)] + cast_shapes,
        compiler_params=pltpu.CompilerParams(dimension_semantics=("arbitrary",),
                                             vmem_limit_bytes=VMEM_LIMIT),
        name="inproj",
    )(x2, norm_g, ln_g, wrow, wnt, *to_cast)
    return outs[:8], outs[8:]


def _compress_kernel(xk_ref, xv_ref, pe_ref, w1_ref, w2k_ref, w2vt_ref, kc_ref, vct_ref, *, nc):
    outs = []
    for j, x_ref in enumerate((xk_ref, xv_ref)):
        a = jnp.zeros((nc, 2 * CMP_HIDDEN), F32)
        b = jnp.zeros((nc, 2 * CMP_HIDDEN), F32)
        for t in range(0, CMP_STRIDE, 2):
            xs = [x_ref[0, pl.ds(t + e, nc, stride=CMP_STRIDE), :] for e in range(2)]

            def half_block(first):
                lhs = jnp.concatenate([(xs[e] + pe_ref[j, first + t + e:first + t + e + 1, :]).astype(BF16)
                                       for e in range(2)], axis=1)
                rhs = jnp.concatenate([w1_ref[j, first + t], w1_ref[j, first + t + 1]], axis=0)
                return _dot(lhs, rhs)

            a = a + half_block(0)
            b = b + half_block(CMP_STRIDE)
        outs.append(jax.nn.gelu(a + pltpu.roll(b, nc - 1, 0)).astype(BF16))
    k2 = _dot(outs[0], w2k_ref[...])
    v_t = _dot_nt(w2vt_ref[...], outs[1])
    lane = _iota((nc, LANES), 1)
    key_feat = jnp.where((lane == HEAD_DIM) | (lane == HEAD_DIM + 1),
                         CMP_STRIDE * (_iota((nc, LANES), 0) & (CMP_KEY_CHUNK - 1)), 0).astype(F32)
    ci =_iota((N_BLK_PAD, nc), 1) * CMP_STRIDE
    sj = _iota((N_BLK_PAD, nc), 0) * SEL_BLOCK
    overlap_t = jnp.where((ci < sj + SEL_BLOCK) & (ci + (CMP_BLOCK - 1) >= sj), 1.0, 0.0).astype(BF16)
    ones_rows = jnp.where(_iota((VT_ROWS - HEAD_DIM, nc), 0) == 0, 1.0, 0.0).astype(BF16)
    for g in range(N_GROUPS):
        kg = k2 if g == 0 else pltpu.roll(k2, HEAD_DIM, 1)
        kc_ref[0, g] = jnp.where(lane < HEAD_DIM, kg, key_feat).astype(BF16)
        vct_ref[0, g, 0:HEAD_DIM, :] = v_t[HEAD_DIM * g:HEAD_DIM * (g + 1), :].astype(BF16)
        vct_ref[0, g, HEAD_DIM:VT_ROWS, :] = ones_rows
        vct_ref[0, g, VT_ROWS:VCT_ROWS, :] = overlap_t


def _compress(kvc3, pe2, w1bd, w2k, w2vt):
    bsz, t, _ = kvc3.shape
    nc = t // CMP_STRIDE
    full = lambda a: pl.BlockSpec(a.shape, lambda b: (0,) * a.ndim)
    return pl.pallas_call(
        functools.partial(_compress_kernel, nc=nc),
        grid=(bsz,),
        in_specs=[pl.BlockSpec((1, t, LANES), lambda b: (b, 0, 0)),
                  pl.BlockSpec((1, t, LANES), lambda b: (b, 0, 1)),
                  full(pe2), full(w1bd), full(w2k), full(w2vt)],
        out_specs=[pl.BlockSpec((1, N_GROUPS, nc, LANES), lambda b: (b, 0, 0, 0)),
                   pl.BlockSpec((1, N_GROUPS, VCT_ROWS, nc), lambda b: (b, 0, 0, 0))],
        out_shape=[jax.ShapeDtypeStruct((bsz, N_GROUPS, nc, LANES), BF16),
                   jax.ShapeDtypeStruct((bsz, N_GROUPS, VCT_ROWS, nc), BF16)],
        compiler_params=pltpu.CompilerParams(dimension_semantics=("arbitrary",),
                                             vmem_limit_bytes=VMEM_LIMIT),
        name="compress",
    )(kvc3, kvc3, pe2, w1bd, w2k, w2vt)


def _cmp_topk_kernel(qt_ref, kc_ref, vct_ref, gatest_ref, ocmp_ref, selt_ref, flags_ref,
                     m_ref, acc_ref, *, ncp):
    step = pl.program_id(1)
    n_chunks = ncp // CMP_KEY_CHUNK
    chunk_tokens = CMP_KEY_CHUNK * CMP_STRIDE
    tile_heads = lambda a: jnp.concatenate([a] * HPG, axis=1)
    tail_chunks = min(2, n_chunks)
    head_chunks = n_chunks - tail_chunks
    tail_keys = tail_chunks * CMP_KEY_CHUNK
    blocks = range(CMP_BLOCKS)
    groups = range(N_GROUPS)
    items = [(bi, g) for bi in blocks for g in groups]
    start = [(step * CMP_BLOCKS + bi) * Q_BLOCK for bi in blocks]
    qlanes = [slice(Q_BLOCK * bi, Q_BLOCK * (bi + 1)) for bi in blocks]
    nck = [(start[bi] + Q_BLOCK - CMP_BLOCK) // CMP_STRIDE // CMP_KEY_CHUNK + 1 for bi in blocks]
    tail_c0 = [jnp.maximum(nck[bi] - tail_chunks, 0) for bi in blocks]
    tail_rows = [pl.ds(pl.multiple_of(tail_c0[bi] * CMP_KEY_CHUNK, CMP_KEY_CHUNK), tail_keys)
                 for bi in blocks]
    gt = gatest_ref[0]
    slope_rows = [_slope_row(g, Q_BLOCK) for g in groups]
    qas = {(bi, g): jnp.concatenate(
        [jnp.concatenate([qt_ref[0, HEAD_DIM * (HPG * g + j):HEAD_DIM * (HPG * g + j + 1), qlanes[bi]]
                          for j in range(HPG)], axis=1),
         _slope_feature_rows(slope_rows[g], LANES - HEAD_DIM)], axis=0) for bi, g in items}

    def delta(bi, g, c):
        return slope_rows[g] * (start[bi] - c * chunk_tokens).astype(F32)

    for bi in blocks:
        if head_chunks > 0:
            @pl.when(nck[bi] > tail_chunks)
            def _():
                sts = [_dot(kc_ref[0, g, 0:head_chunks * CMP_KEY_CHUNK, :], qas[bi, g]) for g in groups]
                dls = [[delta(bi, g, c) + jnp.where(c < nck[bi] - tail_chunks, 0.0, MASK_BIG)
                        for c in range(head_chunks)] for g in groups]
                ms = [_col_max(sts[g], dls[g], CMP_KEY_CHUNK) for g in groups]
                ps = [_probs(sts[g], dls[g], ms[g], CMP_KEY_CHUNK) for g in groups]
                for g in groups:
                    m_ref[bi, g] = ms[g]
                    acc_ref[bi, g] = _dot(vct_ref[0, g, :, 0:head_chunks * CMP_KEY_CHUNK], ps[g])

            @pl.when(nck[bi] <= tail_chunks)
            def _():
                m_ref[bi] = jnp.full(m_ref.shape[1:], NEG, F32)
                acc_ref[bi] = jnp.zeros(acc_ref.shape[1:], F32)

    def tail_and_select(n_rows):
        if head_chunks > 0:
            m_old = {it: m_ref[it[0], it[1]] for it in items}
            acc_old = {it: acc_ref[it[0], it[1]] for it in items}
        else:
            m_old = {it: jnp.full((1, HPG * Q_BLOCK), NEG, F32) for it in items}
            acc_old = {it: jnp.zeros((VCT_ROWS, HPG * Q_BLOCK), F32) for it in items}
        key_row = _iota((tail_keys, Q_BLOCK), 0)
        q_lane = _iota((tail_keys, Q_BLOCK), 1)
        tail_bias = []
        for bi in blocks:
            key_end = CMP_STRIDE * (tail_c0[bi] * CMP_KEY_CHUNK + key_row) + (CMP_BLOCK - 1) - start[bi]
            tail_bias.append(tile_heads(jnp.where(key_end <= q_lane, 0.0, NEG)))
        sts = {(bi, g): _dot(kc_ref[0, g, tail_rows[bi], :], qas[bi, g]) + tail_bias[bi] for bi, g in items}
        dls = {(bi, g): [delta(bi, g, tail_c0[bi] + u) for u in range(tail_chunks)] for bi, g in items}
        m_new = {it: jnp.maximum(m_old[it], _col_max(sts[it], dls[it], CMP_KEY_CHUNK)) for it in items}
        ps = {it: _probs(sts[it], dls[it], m_new[it], CMP_KEY_CHUNK) for it in items}
        accs = {(bi, g): jnp.exp2(m_old[bi, g] - m_new[bi, g]) * acc_old[bi, g]
                + _dot(vct_ref[0, g, :, tail_rows[bi]], ps[bi, g]) for bi, g in items}

        blk_n = _iota((n_rows, Q_BLOCK), 0)
        bf = blk_n.astype(F32)
        rk, cur = {}, []
        for bi in blocks:
            t_row = start[bi] + _iota((1, Q_BLOCK), 1)
            cur.append(lax.shift_right_logical(t_row, SEL_BLOCK.bit_length() - 1))
            forced = (blk_n == 0) | (blk_n == cur[bi]) | (blk_n == cur[bi] - 1)
            has_key = t_row >= CMP_BLOCK - 1
            per_head = []
            for g in groups:
                acc = accs[bi, g]
                inv_l = 1.0 / jnp.maximum(acc[HEAD_DIM:HEAD_DIM + 1], 1e-30)
                o = acc[0:HEAD_DIM] * inv_l
                imp_h = acc[VT_ROWS:VT_ROWS + n_rows] * inv_l
                imp = sum(imp_h[:, Q_BLOCK * j:Q_BLOCK * (j + 1)] for j in range(HPG))
                imp = jnp.where(has_key, imp, 0.0)
                for j in range(HPG):
                    h = HPG * g + j
                    per_head.append(jnp.where(
                        has_key, gt[3 * h:3 * h + 1, qlanes[bi]] * o[:, Q_BLOCK * j:Q_BLOCK * (j + 1)], 0.0))
                rk[bi, g] = jnp.where(blk_n <= cur[bi], jnp.where(forced, REMOVED, imp), NEG)
            ocmp_ref[0, :, qlanes[bi]] = jnp.concatenate(per_head, axis=0)

        for _ in range(N_SELECT - N_FORCED):
            for it in items:
                m = jnp.max(rk[it], axis=0, keepdims=True)
                idx = jnp.min(jnp.where(rk[it] == m, bf, float(N_BLK_PAD)), axis=0, keepdims=True)
                rk[it] = jnp.where(bf == idx, REMOVED, rk[it])
        ones8 = jnp.ones((SUBLANES, Q_BLOCK), F32)
        for bi, g in items:
            sel = jnp.where((blk_n <= cur[bi]) & (rk[bi, g] < 2.0 * NEG), 1.0, 0.0)
            selt_ref[0, g, 0:n_rows, qlanes[bi]] = sel.astype(BF16)
            flag = (_dot_nt(ones8, sel) > 0.5).astype(jnp.int32)
            if n_rows < N_BLK_PAD:
                selt_ref[0, g, n_rows:N_BLK_PAD, qlanes[bi]] = jnp.zeros((N_BLK_PAD - n_rows, Q_BLOCK), BF16)
                flag = jnp.concatenate([flag, jnp.zeros((SUBLANES, N_BLK_PAD - n_rows), jnp.int32)], axis=1)
            flags_ref[0, bi, g] = flag

    causal_rows = (start[-1] + Q_BLOCK) // SEL_BLOCK
    for n_rows in range(TOPK_ROW_STEP, N_BLK_PAD + 1, TOPK_ROW_STEP):
        @pl.when((causal_rows > n_rows - TOPK_ROW_STEP) & (causal_rows <= n_rows))
        def _():
            tail_and_select(n_rows)


def _cmp_topk(qt, kc, vct, gatest):
    bsz, _, t = qt.shape
    ncp = t // CMP_STRIDE
    nqb = t // Q_BLOCK
    step_q = CMP_BLOCKS * Q_BLOCK
    return pl.pallas_call(
        functools.partial(_cmp_topk_kernel, ncp=ncp),
        grid=(bsz, nqb // CMP_BLOCKS),
        in_specs=[pl.BlockSpec((1, NSA_W, step_q), lambda b, i: (b, 0, i)),
                  pl.BlockSpec((1, N_GROUPS, ncp, LANES), lambda b, i: (b, 0, 0, 0)),
                  pl.BlockSpec((1, N_GROUPS, VCT_ROWS, ncp), lambda b, i: (b, 0, 0, 0)),
                  pl.BlockSpec((1, GATET_ROWS, step_q), lambda b, i: (b, 0, i))],
        out_specs=[pl.BlockSpec((1, NSA_W, step_q), lambda b, i: (b, 0, i)),
                   pl.BlockSpec((1, N_GROUPS, N_BLK_PAD, step_q), lambda b, i: (b, 0, 0, i)),
                   pl.BlockSpec((1, CMP_BLOCKS, N_GROUPS, SUBLANES, N_BLK_PAD), lambda b, i: (b, i, 0, 0, 0))],
        out_shape=[jax.ShapeDtypeStruct((bsz, NSA_W, t), F32),
                   jax.ShapeDtypeStruct((bsz, N_GROUPS, N_BLK_PAD, t), BF16),
                   jax.ShapeDtypeStruct((bsz, nqb, N_GROUPS, SUBLANES, N_BLK_PAD), jnp.int32)],
        scratch_shapes=[pltpu.VMEM((CMP_BLOCKS, N_GROUPS, 1, HPG * Q_BLOCK), F32),
                        pltpu.VMEM((CMP_BLOCKS, N_GROUPS, VCT_ROWS, HPG * Q_BLOCK), F32)],
        compiler_params=pltpu.CompilerParams(dimension_semantics=("arbitrary", "arbitrary"),
                                             vmem_limit_bytes=VMEM_LIMIT),
        name="cmp_topk",
    )(qt, kc, vct, gatest)


def _slc_win_kernel(counts_ref, lists_ref, qt_ref, kslc_ref, kwin_ref, vt_ref, selt_ref, gatest_ref,
                    ocmp_ref, out_ref, qaug_ref, m_ref, acc_ref, *, nqb):
    b = pl.program_id(0)
    qb = pl.program_id(1)
    step_id = b * nqb + qb
    start = qb * NSA_QB
    tile_heads = lambda a: jnp.concatenate([a] * HPG, axis=1)
    groups = range(N_GROUPS)
    gt = gatest_ref[0]
    slope_rows = [_slope_row(g, NSA_QB) for g in groups]

    def normalize(acc):
        return acc[0:HEAD_DIM] / jnp.maximum(acc[HEAD_DIM:HEAD_DIM + 1], 1e-30)

    def half_lanes(w):
        return [slice(NSA_QB * h + KEY_CHUNK * w, NSA_QB * h + KEY_CHUNK * (w + 1)) for h in range(HPG)]

    def half(a, w):
        return jnp.concatenate([a[:, s] for s in half_lanes(w)], axis=1)

    def unhalf(lo, hi):
        return jnp.concatenate([x[:, KEY_CHUNK * h:KEY_CHUNK * (h + 1)]
                                for h in range(HPG) for x in (lo, hi)], axis=1)

    def chunk_at(rel):
        pos = start + rel * KEY_CHUNK
        rows = pl.ds(pl.multiple_of(jnp.maximum(pos, 0), KEY_CHUNK), KEY_CHUNK)
        return rows, (jnp.where(pos >= 0, 0.0, MASK_BIG) if rel < 0 else 0.0)

    ki = _iota((KEY_CHUNK, KEY_CHUNK), 0)
    qi = _iota((KEY_CHUNK, KEY_CHUNK), 1)
    upper_bias = tile_heads(jnp.where(ki > qi, 0.0, NEG))
    lower_bias = tile_heads(jnp.where(ki <= qi, 0.0, NEG))
    n_mid = WINDOW // KEY_CHUNK
    mid_dist = (_iota((n_mid * KEY_CHUNK, NSA_QB), 1) + (n_mid - 1) * KEY_CHUNK
                - _iota((n_mid * KEY_CHUNK, NSA_QB), 0))
    mid_bias = tile_heads(jnp.where((mid_dist >= 0) & (mid_dist < WINDOW), 0.0, NEG))
    own_bias = tile_heads(jnp.where(_iota((KEY_CHUNK, NSA_QB), 0) <= _iota((KEY_CHUNK, NSA_QB), 1), 0.0, NEG))

    for g in range(N_GROUPS):
        qaug_ref[g, 0:HEAD_DIM, :] = jnp.concatenate(
            [qt_ref[0, HEAD_DIM * (HPG * g + j):HEAD_DIM * (HPG * g + j + 1), :] for j in range(HPG)],
            axis=1)
        qaug_ref[g, HEAD_DIM:LANES, :] = _slope_feature_rows(slope_rows[g], LANES - HEAD_DIM)
        sel_bias = ((selt_ref[0, g].astype(F32) - 1.0) * MASK_BIG).astype(BF16)
        qaug_ref[g, LANES:2 * LANES, :] = tile_heads(sel_bias)

    mid = [chunk_at(r) for r in range(1 - n_mid, 1)]
    lo_rows, lo_kill = chunk_at(-n_mid)
    hi_rows, _ = chunk_at(1)
    own_rows, _ = chunk_at(0)
    kw = lambda g: kwin_ref.at[0, g]
    vw = lambda g: vt_ref.at[0, N_GROUPS + g]
    slc_keys = lambda g, rows: kslc_ref[0, g, rows, :]
    q_half = lambda g, w, nrow: jnp.concatenate([qaug_ref[g, 0:nrow, s] for s in half_lanes(w)], axis=1)

    wmid_st = [_dot(jnp.concatenate([kw(g)[rows, :] for rows, _ in mid], axis=0), qaug_ref[g, 0:LANES, :])
               + mid_bias for g in groups]
    wlo_st = [_dot(kw(g)[lo_rows, :], q_half(g, 0, LANES)) + upper_bias for g in groups]
    whi_st = [_dot(kw(g)[hi_rows, :], q_half(g, 1, LANES)) + lower_bias for g in groups]
    own_st = [_dot(slc_keys(g, own_rows), qaug_ref[g]) + own_bias for g in groups]
    ohi_st = [_dot(slc_keys(g, hi_rows), q_half(g, 1, 2 * LANES)) + lower_bias for g in groups]

    wmid_dls = [[slope_rows[g] * float(-KEY_CHUNK * r) + kill
                 for r, (_, kill) in zip(range(1 - n_mid, 1), mid)] for g in groups]
    wlo_dl = [half(slope_rows[g], 0) * float(KEY_CHUNK * n_mid) + lo_kill for g in groups]
    hi_dl = [half(slope_rows[g], 1) * float(-KEY_CHUNK) for g in groups]
    neg_half = jnp.full((1, HPG * KEY_CHUNK), NEG, F32)
    win_m = [jnp.maximum(_col_max(wmid_st[g], wmid_dls[g], KEY_CHUNK),
                         unhalf(_col_max(wlo_st[g], [wlo_dl[g]], KEY_CHUNK),
                                _col_max(whi_st[g], [hi_dl[g]], KEY_CHUNK))) for g in groups]
    own_m = [jnp.maximum(_col_max(own_st[g], [0.0], KEY_CHUNK),
                         unhalf(neg_half, _col_max(ohi_st[g], [hi_dl[g]], KEY_CHUNK))) for g in groups]
    probs = functools.partial(_probs, chunk=KEY_CHUNK, exp_dtype=BF16)
    wmid_p = [probs(wmid_st[g], wmid_dls[g], win_m[g]) for g in groups]
    wlo_p = [probs(wlo_st[g], [wlo_dl[g]], half(win_m[g], 0)) for g in groups]
    whi_p = [probs(whi_st[g], [hi_dl[g]], half(win_m[g], 1)) for g in groups]
    own_p = [probs(own_st[g], [0.0], own_m[g]) for g in groups]
    ohi_p = [probs(ohi_st[g], [hi_dl[g]], half(own_m[g], 1)) for g in groups]
    o_win = []
    for g in groups:
        acc = _dot(jnp.concatenate([vw(g)[:, rows] for rows, _ in mid], axis=1), wmid_p[g])
        acc = acc + unhalf(_dot(vw(g)[:, lo_rows], wlo_p[g]), _dot(vw(g)[:, hi_rows], whi_p[g]))
        o_win.append(normalize(acc))
    zero_half = jnp.zeros((VT_ROWS, HPG * KEY_CHUNK), F32)
    for g in groups:
        m_ref[g] = own_m[g]
        acc_ref[g] = (_dot(vt_ref[0, g, :, own_rows], own_p[g])
                      + unhalf(zero_half, _dot(vt_ref[0, g, :, hi_rows], ohi_p[g])))

    def slc_body(i, carry):
        sts, vss, dlss = [], [], []
        for g in range(N_GROUPS):
            word = lists_ref[(step_id * N_GROUPS + g) * LIST_WORDS + i]
            ks, vs, dls = [], [], []
            for u in range(SLC_BATCH[g]):
                cid = lax.shift_right_logical(word, CHUNK_ID_BITS * u) & VOID_CHUNK
                valid = cid < VOID_CHUNK
                c = jnp.where(valid, cid, 0)
                rows = pl.ds(pl.multiple_of(c * KEY_CHUNK, KEY_CHUNK), KEY_CHUNK)
                ks.append(kslc_ref[0, g, rows, :])
                vs.append(vt_ref[0, g, :, rows])
                dls.append(slope_rows[g] * (start - c * KEY_CHUNK).astype(F32)
                           + jnp.where(valid, 0.0, MASK_BIG))
            sts.append(_dot(jnp.concatenate(ks, axis=0), qaug_ref[g]))
            vss.append(jnp.concatenate(vs, axis=1))
            dlss.append(dls)
        m_old = [m_ref[g] for g in range(N_GROUPS)]
        m_new = [jnp.maximum(m_old[g], _col_max(sts[g], dlss[g], KEY_CHUNK)) for g in range(N_GROUPS)]
        ps = [probs(sts[g], dlss[g], m_new[g]) for g in range(N_GROUPS)]
        for g in range(N_GROUPS):
            acc_ref[g] = jnp.exp2(m_old[g] - m_new[g]) * acc_ref[g] + _dot(vss[g], ps[g])
            m_ref[g] = m_new[g]
        return carry

    lax.fori_loop(0, counts_ref[step_id], slc_body, 0)

    per_head = []
    for g in range(N_GROUPS):
        o_slc = normalize(acc_ref[g])
        for j in range(HPG):
            h = HPG * g + j
            lanes = slice(NSA_QB * j, NSA_QB * (j + 1))
            per_head.append(gt[3 * h + 1:3 * h + 2, :] * o_slc[:, lanes]
                            + gt[3 * h + 2:3 * h + 3, :] * o_win[g][:, lanes])
    o_t = jnp.concatenate(per_head, axis=0)
    out_ref[0] = (ocmp_ref[0] + o_t).T.astype(BF16)


def _slc_win(counts, lists, qt, kslc, kwin, vt, selt, gatest, ocmp):
    bsz, _, t = qt.shape
    nqb = t // NSA_QB
    grid_spec = pltpu.PrefetchScalarGridSpec(
        num_scalar_prefetch=2,
        grid=(bsz, nqb),
        in_specs=[pl.BlockSpec((1, NSA_W, NSA_QB), lambda b, i, *_: (b, 0, i)),
                  pl.BlockSpec((1, N_GROUPS, t, 2 * LANES), lambda b, i, *_: (b, 0, 0, 0)),
                  pl.BlockSpec((1, N_GROUPS, t, LANES), lambda b, i, *_: (b, 0, 0, 0)),
                  pl.BlockSpec((1, 2 * N_GROUPS, VT_ROWS, t), lambda b, i, *_: (b, 0, 0, 0)),
                  pl.BlockSpec((1, N_GROUPS, N_BLK_PAD, NSA_QB), lambda b, i, *_: (b, 0, 0, i)),
                  pl.BlockSpec((1, GATET_ROWS, NSA_QB), lambda b, i, *_: (b, 0, i)),
                  pl.BlockSpec((1, NSA_W, NSA_QB), lambda b, i, *_: (b, 0, i))],
        out_specs=pl.BlockSpec((1, NSA_QB, NSA_W), lambda b, i, *_: (b, i, 0)),
        scratch_shapes=[pltpu.VMEM((N_GROUPS, 2 * LANES, HPG * NSA_QB), BF16),
                        pltpu.VMEM((N_GROUPS, 1, HPG * NSA_QB), F32),
                        pltpu.VMEM((N_GROUPS, VT_ROWS, HPG * NSA_QB), F32)],
    )
    return pl.pallas_call(
        functools.partial(_slc_win_kernel, nqb=nqb),
        grid_spec=grid_spec,
        out_shape=jax.ShapeDtypeStruct((bsz, t, NSA_W), BF16),
        compiler_params=pltpu.CompilerParams(dimension_semantics=("arbitrary", "arbitrary"),
                                             vmem_limit_bytes=VMEM_LIMIT),
        name="slc_win",
    )(counts, lists, qt, kslc, kwin, vt, selt, gatest, ocmp)


def _merge_kernel(x_ref, onsa_ref, u_ref, vn_ref, g_ref, wm_ref, bm_ref, ws_ref, bs_ref,
                  wpa_ref, wpb_ref, wo_ref, *refs, tm, n_cast):
    cast_in, h_ref, cast_out = refs[:n_cast], refs[n_cast], refs[n_cast + 1:]
    _cast_blocks(cast_in, cast_out)
    x = x_ref[...]
    xn = _rms(x, g_ref[...]).astype(BF16)
    tril = _iota((GMLP_CHUNK, GMLP_CHUNK), 0) >= _iota((GMLP_CHUNK, GMLP_CHUNK), 1)
    sgu_rows = []
    for c in range(tm // GMLP_CHUNK):
        rows = slice(GMLP_CHUNK * c, GMLP_CHUNK * (c + 1))
        cols = []
        for g in range(GMLP_GROUPS):
            lanes = slice(LANES * g, LANES * (g + 1))
            w = jnp.where(tril, ws_ref[g], 0.0).astype(BF16)
            cols.append(_dot(w, vn_ref[rows, lanes]) + bs_ref[:, g:g + 1])
        sgu_rows.append(u_ref[rows, :] * jnp.concatenate(cols, axis=1))
    o_sgu = jnp.concatenate(sgu_rows, axis=0).astype(BF16)
    mg = jax.nn.sigmoid(_dot(xn, wm_ref[...]) + bm_ref[...])
    mixed = (mg[:, :D_MODEL] * _dot(onsa_ref[...], wpa_ref[...])
             + mg[:, D_MODEL:] * _dot(o_sgu, wpb_ref[...]))
    h_ref[...] = x + _dot(mixed.astype(BF16), wo_ref[...])


def _merge(x2, onsa2, u2, vn2, norm_g, wm, bm, ws, bs_t, wpa, wpb, wo, to_cast, tm):
    n = x2.shape[0]
    row = lambda w: pl.BlockSpec((tm, w), lambda i: (i, 0))
    full = lambda a: pl.BlockSpec(a.shape, lambda i: (0,) * a.ndim)
    cast_specs, cast_shapes = _cast_plan(to_cast, n // tm)
    outs = pl.pallas_call(
        functools.partial(_merge_kernel, tm=tm, n_cast=len(to_cast)),
        grid=(n // tm,),
        in_specs=[row(D_MODEL), row(NSA_W), row(GMLP_WIDTH), row(GMLP_WIDTH), full(norm_g),
                  full(wm), full(bm), full(ws), full(bs_t), full(wpa), full(wpb), full(wo)] + cast_specs,
        out_specs=[row(D_MODEL)] + cast_specs,
        out_shape=[jax.ShapeDtypeStruct((n, D_MODEL), F32)] + cast_shapes,
        compiler_params=pltpu.CompilerParams(dimension_semantics=("arbitrary",),
                                             vmem_limit_bytes=VMEM_LIMIT),
        name="merge",
    )(x2, onsa2, u2, vn2, norm_g, wm, bm, ws, bs_t, wpa, wpb, wo, *to_cast)
    return outs[0], outs[1:]


def _memkv_kernel(mem_ref, g_ref, w_ref, out_ref):
    out_ref[0] = _dot(_rms(mem_ref[0], g_ref[...]).astype(BF16), w_ref[...]).astype(BF16)


def _memkv(mem, norm_g, w):
    bsz, nm, _ = mem.shape
    return pl.pallas_call(
        _memkv_kernel,
        grid=(bsz,),
        in_specs=[pl.BlockSpec((1, nm, D_MODEL), lambda b: (b, 0, 0)),
                  pl.BlockSpec(norm_g.shape, lambda b: (0, 0)),
                  pl.BlockSpec(w.shape, lambda b: (0, 0))],
        out_specs=pl.BlockSpec((1, nm, 2 * MEM_W), lambda b: (b, 0, 0)),
        out_shape=jax.ShapeDtypeStruct((bsz, nm, 2 * MEM_W), BF16),
        compiler_params=pltpu.CompilerParams(dimension_semantics=("arbitrary",),
                                             vmem_limit_bytes=VMEM_LIMIT),
        name="memkv",
    )(mem, norm_g, w)


def _xattn_kernel(h_ref, g_ref, wq_ref, mkv_ref, wo_ref, out_ref, *, tm):
    halves = [slice(0, tm // 2), slice(tm // 2, tm)]
    head_lanes = [slice(MEM_HEAD_DIM * a, MEM_HEAD_DIM * (a + 1)) for a in range(MEM_HEADS)]
    hs = [h_ref[rows, :] for rows in halves]
    hqs = [(_dot(_rms(h, g_ref[...]).astype(BF16), wq_ref[...]) * (MEM_HEAD_DIM ** -0.5 * LOG2E)).astype(BF16)
           for h in hs]
    ss = [[_dot_nt(hq[:, lanes], mkv_ref[0, :, lanes]) for lanes in head_lanes] for hq in hqs]
    es = [[jnp.exp2(s - jnp.max(s, axis=-1, keepdims=True)) for s in s_half] for s_half in ss]
    os = []
    for e_half in es:
        heads = []
        for a, e in enumerate(e_half):
            v = mkv_ref[0, :, MEM_W + MEM_HEAD_DIM * a:MEM_W + MEM_HEAD_DIM * (a + 1)]
            heads.append(_dot(e.astype(BF16), v) * (1.0 / jnp.sum(e, axis=-1, keepdims=True)))
        os.append(jnp.concatenate(heads, axis=1).astype(BF16))
    for rows, h, o in zip(halves, hs, os):
        out_ref[rows, :] = h + _dot(o, wo_ref[...])


def _xattn(h2d, norm_g, wq, mkv, wo, tm, rows_per_batch):
    n = h2d.shape[0]
    nm = mkv.shape[1]
    tiles_per_batch = rows_per_batch // tm
    full = lambda a: pl.BlockSpec(a.shape, lambda i: (0,) * a.ndim)
    return pl.pallas_call(
        functools.partial(_xattn_kernel, tm=tm),
        grid=(n // tm,),
        in_specs=[pl.BlockSpec((tm, D_MODEL), lambda i: (i, 0)), full(norm_g), full(wq),
                  pl.BlockSpec((1, nm, 2 * MEM_W), lambda i: (i // tiles_per_batch, 0, 0)),
                  full(wo)],
        out_specs=pl.BlockSpec((tm, D_MODEL), lambda i: (i, 0)),
        out_shape=jax.ShapeDtypeStruct((n, D_MODEL), F32),
        compiler_params=pltpu.CompilerParams(dimension_semantics=("arbitrary",),
                                             vmem_limit_bytes=VMEM_LIMIT),
        name="xattn",
    )(h2d, norm_g, wq, mkv, wo)


def _ffn_kernel(h_ref, g_ref, wgu_ref, wd_ref, gf_ref, out_ref, *, d_ff):
    tm = h_ref.shape[0]
    halves = [slice(0, tm // 2), slice(tm // 2, tm)]
    hs = [h_ref[rows, :] for rows in halves]
    hns = [_rms(h, g_ref[...]).astype(BF16) for h in hs]
    gates = [_dot(hn, wgu_ref[:, :d_ff]) for hn in hns]
    ups = [_dot(hn, wgu_ref[:, d_ff:]) for hn in hns]
    acts = [(jax.nn.silu(gate) * up).astype(BF16) for gate, up in zip(gates, ups)]
    for rows, h, act in zip(halves, hs, acts):
        out_ref[rows, :] = _rms(h + _dot(act, wd_ref[...]), gf_ref[...])


def _ffn(h2d, norm_g, wgu, wd, norm_f, tm):
    n = h2d.shape[0]
    d_ff = wd.shape[0]
    full = lambda a: pl.BlockSpec(a.shape, lambda i: (0,) * a.ndim)
    once = lambda a: pl.BlockSpec(a.shape, lambda i: (0,) * a.ndim, pipeline_mode=pl.Buffered(1))
    return pl.pallas_call(
        functools.partial(_ffn_kernel, d_ff=d_ff),
        grid=(n // tm,),
        in_specs=[pl.BlockSpec((tm, D_MODEL), lambda i: (i, 0)), full(norm_g), once(wgu), once(wd),
                  full(norm_f)],
        out_specs=pl.BlockSpec((tm, D_MODEL), lambda i: (i, 0)),
        out_shape=jax.ShapeDtypeStruct((n, D_MODEL), F32),
        compiler_params=pltpu.CompilerParams(dimension_semantics=("arbitrary",),
                                             vmem_limit_bytes=VMEM_LIMIT),
        name="ffn",
    )(h2d, norm_g, wgu, wd, norm_f)


def _block_diag2(w):
    z = jnp.zeros_like(w)
    return jnp.concatenate([jnp.concatenate([w, z], axis=-1), jnp.concatenate([z, w], axis=-1)], axis=-2)


def _chunk_lists(flags):
    bsz = flags.shape[0]
    n_chunks = N_KEY_CHUNKS
    per_step = NSA_QB // Q_BLOCK
    nqb = flags.shape[1] // per_step
    f = flags[:, :, :, 0, :].reshape(bsz, nqb, per_step, N_GROUPS, n_chunks, 2).max(axis=(2, 5))
    cid = jnp.arange(n_chunks, dtype=jnp.int32)
    own = (NSA_QB // KEY_CHUNK) * jnp.arange(nqb, dtype=jnp.int32)[None, :, None, None]
    active = (f > 0) & (cid < own)
    n_active = active.sum(axis=-1)
    slot = jnp.cumsum(active, axis=-1) - 1
    hit = active[..., :, None] & (slot[..., :, None] == cid)
    ids = jnp.sum(jnp.where(hit, cid[:, None], 0), axis=-2)
    ids = jnp.where(cid < n_active[..., None], ids, VOID_CHUNK)
    words, n_batches = [], []
    for g, batch in enumerate(SLC_BATCH):
        n_batches.append((n_active[:, :, g] + batch - 1) // batch)
        padded = jnp.pad(ids[:, :, g], ((0, 0), (0, 0), (0, LIST_WORDS * batch - n_chunks)),
                         constant_values=VOID_CHUNK).reshape(bsz, nqb, LIST_WORDS, batch)
        words.append(functools.reduce(jnp.bitwise_or,
                                      [padded[..., u] << (CHUNK_ID_BITS * u) for u in range(batch)]))
    n_batches = functools.reduce(jnp.maximum, n_batches)
    words = jnp.stack(words, axis=2)
    return n_batches.reshape(-1).astype(jnp.int32), words.reshape(-1).astype(jnp.int32)


def kernel(x, mem, norm_mix, w_in, w_cmp_k1, w_cmp_k2, w_cmp_v1, w_cmp_v2, pe_cmp_k, pe_cmp_v, ln_sgu, w_spatial, b_spatial, w_proj_a, w_proj_b, w_merge, b_merge, w_out, norm_mem_q, norm_mem_kv, w_mq, w_mkv, w_mo, norm_ffn, w_gate_up, w_down, norm_final):
    bsz, t, d = x.shape
    depth = norm_mix.shape[0]
    assert d == D_MODEL and t % Q_BLOCK == 0 and t // SEL_BLOCK <= N_BLK_PAD
    assert t // SEL_BLOCK >= N_SELECT and depth == 1 and t % 1024 == 0
    n = bsz * t
    tm = 512
    h = x.reshape(n, d)
    c0, c1, c2, c3 = NSA_W, NSA_W + KV_W, NSA_W + 3 * KV_W, NSA_W + 3 * KV_W + 2 * GMLP_WIDTH
    pad_cols = lambda w, width: jnp.pad(w, ((0, 0), (0, width - w.shape[1])))
    for l in range(depth):
        wi = w_in[l]
        wk, wv = [], []
        for branch in range(2):
            base = c1 + KV_W * branch
            for g in range(N_GROUPS):
                wk.append(pad_cols(wi[:, base + HEAD_DIM * g:base + HEAD_DIM * (g + 1)], LANES))
                v0 = base + N_GROUPS * HEAD_DIM + HEAD_DIM * g
                wv.append(pad_cols(wi[:, v0:v0 + HEAD_DIM], VT_ROWS))
        wrow = jnp.concatenate([wi[:, c0:c1]] + wk + [wi[:, c2:c3]], axis=1).astype(BF16)
        wnt = jnp.concatenate([wi[:, :c0]] + wv + [pad_cols(wi[:, c3:], GATET_ROWS)], axis=1).T.astype(BF16)
        (qt, kvc2, kslc, kwin, vt, u2, vn2, gatest), (wm_b, wpa_b, wpb_b, wo_b, wmq_b, wmkv_b, wmo_b) = _inproj(
            h, norm_mix[l][None], ln_sgu[l][None], wrow, wnt,
            [w_merge[l], w_proj_a[l], w_proj_b[l], w_out[l], w_mq[l], w_mkv[l], w_mo[l]], 2 * tm, bsz, t)

        pe2 = jnp.stack([pe_cmp_k[l], pe_cmp_v[l]])
        pe2 = jnp.concatenate([pe2, pe2], axis=-1)
        w1 = jnp.stack([w_cmp_k1[l], w_cmp_v1[l]]).reshape(2, CMP_BLOCK, HEAD_DIM, CMP_HIDDEN)
        w1bd = _block_diag2(w1).astype(BF16)
        w2k = _block_diag2(w_cmp_k2[l]).astype(BF16)
        w2vt = _block_diag2(w_cmp_v2[l]).T.astype(BF16)
        kc, vct = _compress(kvc2.reshape(bsz, t, KV_W), pe2, w1bd, w2k, w2vt)
        ocmp, selt, flags = _cmp_topk(qt, kc, vct, gatest)
        counts, lists = _chunk_lists(flags)
        onsa = _slc_win(counts, lists, qt, kslc, kwin, vt, selt, gatest, ocmp)

        h, (wgu_b, wd_b) = _merge(h, onsa.reshape(n, NSA_W), u2, vn2, norm_mix[l][None], wm_b,
                                  b_merge[l][None], w_spatial[l], b_spatial[l].T, wpa_b, wpb_b, wo_b,
                                  [w_gate_up[l], w_down[l]], 2 * tm)

        mkv = _memkv(mem, norm_mem_kv[l][None], wmkv_b)
        h = _xattn(h, norm_mem_q[l][None], wmq_b, mkv, wmo_b, 2 * tm, t)
        h = _ffn(h, norm_ffn[l][None], wgu_b, wd_b, norm_final[None], tm)
    return h.reshape(bsz, t, d)
```

```python
import functools

import jax
import jax.numpy as jnp
from jax import lax
from jax.experimental import pallas as pl
from jax.experimental.pallas import tpu as pltpu

F32 = jnp.float32
BF16 = jnp.bfloat16

LANES = 128
SUBLANES = 8
D_MODEL = 1024
N_HEADS = 8
HEAD_DIM = 64
N_GROUPS = 2
HPG = N_HEADS // N_GROUPS
CMP_BLOCK = 32
CMP_STRIDE = 16
CMP_HIDDEN = 128
SEL_BLOCK = 64
N_SELECT = 16
WINDOW = 512
Q_BLOCK = 256
KEY_CHUNK = 128
N_BLK_PAD = 128
NSA_W = N_HEADS * HEAD_DIM
KV_W = 2 * N_GROUPS * HEAD_DIM
GMLP_WIDTH = 512
GMLP_GROUPS = 4
GMLP_CHUNK = 128
MEM_HEADS = 4
MEM_HEAD_DIM = 128
MEM_W = MEM_HEADS * MEM_HEAD_DIM
NSA_QB = 256
SLC_BATCH = (4, 5)
N_KEY_CHUNKS = N_BLK_PAD // 2
CHUNK_ID_BITS = 6
VOID_CHUNK = N_KEY_CHUNKS - 1
LIST_WORDS = -(-N_KEY_CHUNKS // min(SLC_BATCH))
VT_ROWS = 80
VCT_ROWS = VT_ROWS + N_BLK_PAD
CMP_KEY_CHUNK = 128
N_FORCED = 3
TOPK_ROW_STEP = 64
CMP_BLOCKS = 2
GATET_ROWS = 32
MASK_BIG = 1e30
EPS = 1e-6
NEG = -1e30
REMOVED = -3e38
SLOPES = tuple(2.0 ** (-8.0 * (h + 1) / N_HEADS) for h in range(N_HEADS))
LOG2E = 1.4426950408889634
Q_SCALE = HEAD_DIM ** -0.5 * LOG2E
VMEM_LIMIT = 56 * 1024 * 1024


def _dot(a, b):
    return jnp.dot(a, b, preferred_element_type=F32)


def _dot_nt(a, b):
    return lax.dot_general(a, b, (((1,), (1,)), ((), ())), preferred_element_type=F32)


def _rms(x, g):
    return x * lax.rsqrt(jnp.mean(x * x, axis=-1, keepdims=True) + EPS) * g


def _iota(shape, dim):
    return lax.broadcasted_iota(jnp.int32, shape, dim)


def _slope_row(g, nq):
    return jnp.concatenate(
        [jnp.full((1, nq), SLOPES[HPG * g + j] * LOG2E, F32) for j in range(HPG)], axis=1)


def _slope_feature_rows(slope_row, n_rows):
    hi = slope_row.astype(BF16).astype(F32)
    r = _iota((n_rows, slope_row.shape[1]), 0)
    return jnp.where(r == 0, hi, jnp.where(r == 1, slope_row - hi, 0.0)).astype(BF16)


def _chunk_slabs(st, chunk):
    return [st[chunk * u:chunk * (u + 1)] for u in range(st.shape[0] // chunk)]


def _col_max(st, dls, chunk):
    parts = [s.reshape(chunk // SUBLANES, SUBLANES, s.shape[1]).max(axis=0) - dl
             for s, dl in zip(_chunk_slabs(st, chunk), dls)]
    return functools.reduce(jnp.maximum, parts).max(axis=0, keepdims=True)


def _probs(st, dls, m, chunk, exp_dtype=F32):
    return jnp.concatenate([jnp.exp2((s - (m + dl)).astype(exp_dtype)).astype(BF16)
                            for s, dl in zip(_chunk_slabs(st, chunk), dls)], axis=0)


def _cast_plan(weights, n_steps):
    for w in weights:
        assert w.shape[0] % (16 * n_steps) == 0, (w.shape, n_steps)
    specs = [pl.BlockSpec((w.shape[0] // n_steps, w.shape[1]), lambda i: (i, 0)) for w in weights]
    shapes = [jax.ShapeDtypeStruct(w.shape, BF16) for w in weights]
    return specs, shapes


def _cast_blocks(src_refs, dst_refs):
    for src, dst in zip(src_refs, dst_refs):
        dst[...] = src[...].astype(BF16)


def _inproj_kernel(x_ref, g_ref, lng_ref, wrow_ref, wnt_ref, *refs, tm, n_cast):
    cast_in, cast_out = refs[:n_cast], refs[len(refs) - n_cast:]
    qt_ref, kvc_ref, kaug_ref, vt_ref, u_ref, vn_ref, gatest_ref = refs[n_cast:len(refs) - n_cast]
    _cast_blocks(cast_in, cast_out)
    xn = _rms(x_ref[...], g_ref[...]).astype(BF16)
    r = _dot(xn, wrow_ref[...])
    nt = _dot_nt(wnt_ref[...], xn)
    qt_ref[0] = (nt[0:NSA_W] * Q_SCALE).astype(BF16)
    kvc_ref[...] = r[:, 0:KV_W]
    lane = _iota((tm, LANES), 1)
    key_feat = jnp.where((lane == HEAD_DIM) | (lane == HEAD_DIM + 1),
                         _iota((tm, LANES), 0) & (KEY_CHUNK - 1), 0).astype(F32)
    ones_row = jnp.where(_iota((VT_ROWS, tm), 0) == HEAD_DIM, 1.0, 0.0)
    for a in range(2 * N_GROUPS):
        kaug_ref[0, a] = (r[:, KV_W + LANES * a:KV_W + LANES * (a + 1)] + key_feat).astype(BF16)
        vt_ref[0, a] = (nt[NSA_W + VT_ROWS * a:NSA_W + VT_ROWS * (a + 1)] + ones_row).astype(BF16)
    uv = jax.nn.gelu(r[:, KV_W + 2 * N_GROUPS * LANES:])
    u_ref[...] = uv[:, :GMLP_WIDTH]
    v = uv[:, GMLP_WIDTH:]
    vc = v - jnp.mean(v, axis=-1, keepdims=True)
    vn = vc * lax.rsqrt(jnp.mean(vc * vc, axis=-1, keepdims=True) + EPS) * lng_ref[...]
    vn_ref[...] = vn.astype(BF16)
    gatest_ref[0] = jax.nn.sigmoid(nt[NSA_W + 2 * N_GROUPS * VT_ROWS:])


def _inproj(x2, norm_g, ln_g, wrow, wnt, to_cast, tm, bsz, t):
    n = x2.shape[0]
    tpb = t // tm
    row = lambda w: pl.BlockSpec((tm, w), lambda i: (i, 0))
    full = lambda a: pl.BlockSpec(a.shape, lambda i: (0,) * a.ndim)
    cast_specs, cast_shapes = _cast_plan(to_cast, n // tm)
    outs = pl.pallas_call(
        functools.partial(_inproj_kernel, tm=tm, n_cast=len(to_cast)),
        grid=(n // tm,),
        in_specs=[row(D_MODEL), full(norm_g), full(ln_g), full(wrow), full(wnt)] + cast_specs,
        out_specs=[pl.BlockSpec((1, NSA_W, tm), lambda i: (i // tpb, 0, i % tpb)),
                   row(KV_W),
                   pl.BlockSpec((1, 2 * N_GROUPS, tm, LANES), lambda i: (i // tpb, 0, i % tpb, 0)),
                   pl.BlockSpec((1, 2 * N_GROUPS, VT_ROWS, tm), lambda i: (i // tpb, 0, 0, i % tpb)),
                   row(GMLP_WIDTH), row(GMLP_WIDTH),
                   pl.BlockSpec((1, GATET_ROWS, tm), lambda i: (i // tpb, 0, i % tpb))] + cast_specs,
        out_shape=[jax.ShapeDtypeStruct((bsz, NSA_W, t), BF16),
                   jax.ShapeDtypeStruct((n, KV_W), F32),
                   jax.ShapeDtypeStruct((bsz, 2 * N_GROUPS, t, LANES), BF16),
                   jax.ShapeDtypeStruct((bsz, 2 * N_GROUPS, VT_ROWS, t), BF16),
                   jax.ShapeDtypeStruct((n, GMLP_WIDTH), F32),
                   jax.ShapeDtypeStruct((n, GMLP_WIDTH), BF16),
                   jax.ShapeDtypeStruct((bsz, GATET_ROWS, t), F32)] + cast_shapes,
        compiler_params=pltpu.CompilerParams(dimension_semantics=("arbitrary",),
                                             vmem_limit_bytes=VMEM_LIMIT),
        name="inproj",
    )(x2, norm_g, ln_g, wrow, wnt, *to_cast)
    return outs[:7], outs[7:]


def _compress_kernel(xk_ref, xv_ref, pe_ref, w1_ref, w2k_ref, w2vt_ref, kc_ref, vct_ref, *, nc):
    outs = []
    for j, x_ref in enumerate((xk_ref, xv_ref)):
        a = jnp.zeros((nc, 2 * CMP_HIDDEN), F32)
        b = jnp.zeros((nc, 2 * CMP_HIDDEN), F32)
        for t in range(0, CMP_STRIDE, 2):
            xs = [x_ref[0, pl.ds(t + e, nc, stride=CMP_STRIDE), :] for e in range(2)]

            def half_block(first):
                lhs = jnp.concatenate([(xs[e] + pe_ref[j, first + t + e:first + t + e + 1, :]).astype(BF16)
                                       for e in range(2)], axis=1)
                rhs = jnp.concatenate([w1_ref[j, first + t], w1_ref[j, first + t + 1]], axis=0)
                return _dot(lhs, rhs)

            a = a + half_block(0)
            b = b + half_block(CMP_STRIDE)
        outs.append(jax.nn.gelu(a + pltpu.roll(b, nc - 1, 0)).astype(BF16))
    k2 = _dot(outs[0], w2k_ref[...])
    v_t = _dot_nt(w2vt_ref[...], outs[1])
    lane = _iota((nc, LANES), 1)
    key_feat = jnp.where((lane == HEAD_DIM) | (lane == HEAD_DIM + 1),
                         CMP_STRIDE * (_iota((nc, LANES), 0) & (CMP_KEY_CHUNK - 1)), 0).astype(F32)
    ci =_iota((N_BLK_PAD, nc), 1) * CMP_STRIDE
    sj = _iota((N_BLK_PAD, nc), 0) * SEL_BLOCK
    overlap_t = jnp.where((ci < sj + SEL_BLOCK) & (ci + (CMP_BLOCK - 1) >= sj), 1.0, 0.0).astype(BF16)
    ones_rows = jnp.where(_iota((VT_ROWS - HEAD_DIM, nc), 0) == 0, 1.0, 0.0).astype(BF16)
    for g in range(N_GROUPS):
        kg = k2 if g == 0 else pltpu.roll(k2, HEAD_DIM, 1)
        kc_ref[0, g] = jnp.where(lane < HEAD_DIM, kg, key_feat).astype(BF16)
        vct_ref[0, g, 0:HEAD_DIM, :] = v_t[HEAD_DIM * g:HEAD_DIM * (g + 1), :].astype(BF16)
        vct_ref[0, g, HEAD_DIM:VT_ROWS, :] = ones_rows
        vct_ref[0, g, VT_ROWS:VCT_ROWS, :] = overlap_t


def _compress(kvc3, pe2, w1bd, w2k, w2vt):
    bsz, t, _ = kvc3.shape
    nc = t // CMP_STRIDE
    full = lambda a: pl.BlockSpec(a.shape, lambda b: (0,) * a.ndim)
    return pl.pallas_call(
        functools.partial(_compress_kernel, nc=nc),
        grid=(bsz,),
        in_specs=[pl.BlockSpec((1, t, LANES), lambda b: (b, 0, 0)),
                  pl.BlockSpec((1, t, LANES), lambda b: (b, 0, 1)),
                  full(pe2), full(w1bd), full(w2k), full(w2vt)],
        out_specs=[pl.BlockSpec((1, N_GROUPS, nc, LANES), lambda b: (b, 0, 0, 0)),
                   pl.BlockSpec((1, N_GROUPS, VCT_ROWS, nc), lambda b: (b, 0, 0, 0))],
        out_shape=[jax.ShapeDtypeStruct((bsz, N_GROUPS, nc, LANES), BF16),
                   jax.ShapeDtypeStruct((bsz, N_GROUPS, VCT_ROWS, nc), BF16)],
        compiler_params=pltpu.CompilerParams(dimension_semantics=("arbitrary",),
                                             vmem_limit_bytes=VMEM_LIMIT),
        name="compress",
    )(kvc3, kvc3, pe2, w1bd, w2k, w2vt)


def _cmp_topk_kernel(qt_ref, kc_ref, vct_ref, gatest_ref, ocmp_ref, selt_ref, flags_ref,
                     m_ref, acc_ref, *, ncp):
    step = pl.program_id(1)
    n_chunks = ncp // CMP_KEY_CHUNK
    chunk_tokens = CMP_KEY_CHUNK * CMP_STRIDE
    tile_heads = lambda a: jnp.concatenate([a] * HPG, axis=1)
    tail_chunks = min(2, n_chunks)
    head_chunks = n_chunks - tail_chunks
    tail_keys = tail_chunks * CMP_KEY_CHUNK
    blocks = range(CMP_BLOCKS)
    groups = range(N_GROUPS)
    items = [(bi, g) for bi in blocks for g in groups]
    start = [(step * CMP_BLOCKS + bi) * Q_BLOCK for bi in blocks]
    qlanes = [slice(Q_BLOCK * bi, Q_BLOCK * (bi + 1)) for bi in blocks]
    nck = [(start[bi] + Q_BLOCK - CMP_BLOCK) // CMP_STRIDE // CMP_KEY_CHUNK + 1 for bi in blocks]
    tail_c0 = [jnp.maximum(nck[bi] - tail_chunks, 0) for bi in blocks]
    tail_rows = [pl.ds(pl.multiple_of(tail_c0[bi] * CMP_KEY_CHUNK, CMP_KEY_CHUNK), tail_keys)
                 for bi in blocks]
    gt = gatest_ref[0]
    slope_rows = [_slope_row(g, Q_BLOCK) for g in groups]
    qas = {(bi, g): jnp.concatenate(
        [jnp.concatenate([qt_ref[0, HEAD_DIM * (HPG * g + j):HEAD_DIM * (HPG * g + j + 1), qlanes[bi]]
                          for j in range(HPG)], axis=1),
         _slope_feature_rows(slope_rows[g], LANES - HEAD_DIM)], axis=0) for bi, g in items}

    def delta(bi, g, c):
        return slope_rows[g] * (start[bi] - c * chunk_tokens).astype(F32)

    for bi in blocks:
        if head_chunks > 0:
            @pl.when(nck[bi] > tail_chunks)
            def _():
                sts = [_dot(kc_ref[0, g, 0:head_chunks * CMP_KEY_CHUNK, :], qas[bi, g]) for g in groups]
                dls = [[delta(bi, g, c) + jnp.where(c < nck[bi] - tail_chunks, 0.0, MASK_BIG)
                        for c in range(head_chunks)] for g in groups]
                ms = [_col_max(sts[g], dls[g], CMP_KEY_CHUNK) for g in groups]
                ps = [_probs(sts[g], dls[g], ms[g], CMP_KEY_CHUNK) for g in groups]
                for g in groups:
                    m_ref[bi, g] = ms[g]
                    acc_ref[bi, g] = _dot(vct_ref[0, g, :, 0:head_chunks * CMP_KEY_CHUNK], ps[g])

            @pl.when(nck[bi] <= tail_chunks)
            def _():
                m_ref[bi] = jnp.full(m_ref.shape[1:], NEG, F32)
                acc_ref[bi] = jnp.zeros(acc_ref.shape[1:], F32)

    def tail_and_select(n_rows):
        if head_chunks > 0:
            m_old = {it: m_ref[it[0], it[1]] for it in items}
            acc_old = {it: acc_ref[it[0], it[1]] for it in items}
        else:
            m_old = {it: jnp.full((1, HPG * Q_BLOCK), NEG, F32) for it in items}
            acc_old = {it: jnp.zeros((VCT_ROWS, HPG * Q_BLOCK), F32) for it in items}
        key_row = _iota((tail_keys, Q_BLOCK), 0)
        q_lane = _iota((tail_keys, Q_BLOCK), 1)
        tail_bias = []
        for bi in blocks:
            key_end = CMP_STRIDE * (tail_c0[bi] * CMP_KEY_CHUNK + key_row) + (CMP_BLOCK - 1) - start[bi]
            tail_bias.append(tile_heads(jnp.where(key_end <= q_lane, 0.0, NEG)))
        sts = {(bi, g): _dot(kc_ref[0, g, tail_rows[bi], :], qas[bi, g]) + tail_bias[bi] for bi, g in items}
        dls = {(bi, g): [delta(bi, g, tail_c0[bi] + u) for u in range(tail_chunks)] for bi, g in items}
        m_new = {it: jnp.maximum(m_old[it], _col_max(sts[it], dls[it], CMP_KEY_CHUNK)) for it in items}
        ps = {it: _probs(sts[it], dls[it], m_new[it], CMP_KEY_CHUNK) for it in items}
        accs = {(bi, g): jnp.exp2(m_old[bi, g] - m_new[bi, g]) * acc_old[bi, g]
                + _dot(vct_ref[0, g, :, tail_rows[bi]], ps[bi, g]) for bi, g in items}

        blk_n = _iota((n_rows, Q_BLOCK), 0)
        bf = blk_n.astype(F32)
        rk, cur = {}, []
        for bi in blocks:
            t_row = start[bi] + _iota((1, Q_BLOCK), 1)
            cur.append(lax.shift_right_logical(t_row, SEL_BLOCK.bit_length() - 1))
            forced = (blk_n == 0) | (blk_n == cur[bi]) | (blk_n == cur[bi] - 1)
            has_key = t_row >= CMP_BLOCK - 1
            per_head = []
            for g in groups:
                acc = accs[bi, g]
                inv_l = 1.0 / jnp.maximum(acc[HEAD_DIM:HEAD_DIM + 1], 1e-30)
                o = acc[0:HEAD_DIM] * inv_l
                imp_h = acc[VT_ROWS:VT_ROWS + n_rows] * inv_l
                imp = sum(imp_h[:, Q_BLOCK * j:Q_BLOCK * (j + 1)] for j in range(HPG))
                imp = jnp.where(has_key, imp, 0.0)
                for j in range(HPG):
                    h = HPG * g + j
                    per_head.append(jnp.where(
                        has_key, gt[3 * h:3 * h + 1, qlanes[bi]] * o[:, Q_BLOCK * j:Q_BLOCK * (j + 1)], 0.0))
                rk[bi, g] = jnp.where(blk_n <= cur[bi], jnp.where(forced, REMOVED, imp), NEG)
            ocmp_ref[0, :, qlanes[bi]] = jnp.concatenate(per_head, axis=0)

        for _ in range(N_SELECT - N_FORCED):
            for it in items:
                m = jnp.max(rk[it], axis=0, keepdims=True)
                idx = jnp.min(jnp.where(rk[it] == m, bf, float(N_BLK_PAD)), axis=0, keepdims=True)
                rk[it] = jnp.where(bf == idx, REMOVED, rk[it])
        ones8 = jnp.ones((SUBLANES, Q_BLOCK), F32)
        for bi, g in items:
            sel = jnp.where((blk_n <= cur[bi]) & (rk[bi, g] < 2.0 * NEG), 1.0, 0.0)
            selt_ref[0, g, 0:n_rows, qlanes[bi]] = sel.astype(BF16)
            flag = (_dot_nt(ones8, sel) > 0.5).astype(jnp.int32)
            if n_rows < N_BLK_PAD:
                selt_ref[0, g, n_rows:N_BLK_PAD, qlanes[bi]] = jnp.zeros((N_BLK_PAD - n_rows, Q_BLOCK), BF16)
                flag = jnp.concatenate([flag, jnp.zeros((SUBLANES, N_BLK_PAD - n_rows), jnp.int32)], axis=1)
            flags_ref[0, bi, g] = flag

    causal_rows = (start[-1] + Q_BLOCK) // SEL_BLOCK
    for n_rows in range(TOPK_ROW_STEP, N_BLK_PAD + 1, TOPK_ROW_STEP):
        @pl.when((causal_rows > n_rows - TOPK_ROW_STEP) & (causal_rows <= n_rows))
        def _():
            tail_and_select(n_rows)


def _cmp_topk(qt, kc, vct, gatest):
    bsz, _, t = qt.shape
    ncp = t // CMP_STRIDE
    nqb = t // Q_BLOCK
    step_q = CMP_BLOCKS * Q_BLOCK
    return pl.pallas_call(
        functools.partial(_cmp_topk_kernel, ncp=ncp),
        grid=(bsz, nqb // CMP_BLOCKS),
        in_specs=[pl.BlockSpec((1, NSA_W, step_q), lambda b, i: (b, 0, i)),
                  pl.BlockSpec((1, N_GROUPS, ncp, LANES), lambda b, i: (b, 0, 0, 0)),
                  pl.BlockSpec((1, N_GROUPS, VCT_ROWS, ncp), lambda b, i: (b, 0, 0, 0)),
                  pl.BlockSpec((1, GATET_ROWS, step_q), lambda b, i: (b, 0, i))],
        out_specs=[pl.BlockSpec((1, NSA_W, step_q), lambda b, i: (b, 0, i)),
                   pl.BlockSpec((1, N_GROUPS, N_BLK_PAD, step_q), lambda b, i: (b, 0, 0, i)),
                   pl.BlockSpec((1, CMP_BLOCKS, N_GROUPS, SUBLANES, N_BLK_PAD), lambda b, i: (b, i, 0, 0, 0))],
        out_shape=[jax.ShapeDtypeStruct((bsz, NSA_W, t), F32),
                   jax.ShapeDtypeStruct((bsz, N_GROUPS, N_BLK_PAD, t), BF16),
                   jax.ShapeDtypeStruct((bsz, nqb, N_GROUPS, SUBLANES, N_BLK_PAD), jnp.int32)],
        scratch_shapes=[pltpu.VMEM((CMP_BLOCKS, N_GROUPS, 1, HPG * Q_BLOCK), F32),
                        pltpu.VMEM((CMP_BLOCKS, N_GROUPS, VCT_ROWS, HPG * Q_BLOCK), F32)],
        compiler_params=pltpu.CompilerParams(dimension_semantics=("arbitrary", "arbitrary"),
                                             vmem_limit_bytes=VMEM_LIMIT),
        name="cmp_topk",
    )(qt, kc, vct, gatest)


def _slc_win_kernel(counts_ref, lists_ref, qt_ref, kaug_ref, vt_ref, oh_ref, selt_ref, gatest_ref,
                    ocmp_ref, out_ref, qaug_ref, m_ref, acc_ref, *, nqb):
    b = pl.program_id(0)
    qb = pl.program_id(1)
    step_id = b * nqb + qb
    start = qb * NSA_QB
    tile_heads = lambda a: jnp.concatenate([a] * HPG, axis=1)
    groups = range(N_GROUPS)
    gt = gatest_ref[0]
    slope_rows = [_slope_row(g, NSA_QB) for g in groups]

    def normalize(acc):
        return acc[0:HEAD_DIM] / jnp.maximum(acc[HEAD_DIM:HEAD_DIM + 1], 1e-30)

    def half_lanes(w):
        return [slice(NSA_QB * h + KEY_CHUNK * w, NSA_QB * h + KEY_CHUNK * (w + 1)) for h in range(HPG)]

    def half(a, w):
        return jnp.concatenate([a[:, s] for s in half_lanes(w)], axis=1)

    def unhalf(lo, hi):
        return jnp.concatenate([x[:, KEY_CHUNK * h:KEY_CHUNK * (h + 1)]
                                for h in range(HPG) for x in (lo, hi)], axis=1)

    def chunk_at(rel):
        pos = start + rel * KEY_CHUNK
        rows = pl.ds(pl.multiple_of(jnp.maximum(pos, 0), KEY_CHUNK), KEY_CHUNK)
        return rows, (jnp.where(pos >= 0, 0.0, MASK_BIG) if rel < 0 else 0.0)

    ki = _iota((KEY_CHUNK, KEY_CHUNK), 0)
    qi = _iota((KEY_CHUNK, KEY_CHUNK), 1)
    upper_bias = tile_heads(jnp.where(ki > qi, 0.0, NEG))
    lower_bias = tile_heads(jnp.where(ki <= qi, 0.0, NEG))
    n_mid = WINDOW // KEY_CHUNK
    mid_dist = (_iota((n_mid * KEY_CHUNK, NSA_QB), 1) + (n_mid - 1) * KEY_CHUNK
                - _iota((n_mid * KEY_CHUNK, NSA_QB), 0))
    mid_bias = tile_heads(jnp.where((mid_dist >= 0) & (mid_dist < WINDOW), 0.0, NEG))
    own_bias = tile_heads(jnp.where(_iota((KEY_CHUNK, NSA_QB), 0) <= _iota((KEY_CHUNK, NSA_QB), 1), 0.0, NEG))

    for g in range(N_GROUPS):
        qaug_ref[g, 0:HEAD_DIM, :] = jnp.concatenate(
            [qt_ref[0, HEAD_DIM * (HPG * g + j):HEAD_DIM * (HPG * g + j + 1), :] for j in range(HPG)],
            axis=1)
        qaug_ref[g, HEAD_DIM:LANES, :] = _slope_feature_rows(slope_rows[g], LANES - HEAD_DIM)
        sel_bias = ((selt_ref[0, g].astype(F32) - 1.0) * MASK_BIG).astype(BF16)
        qaug_ref[g, LANES:2 * LANES, :] = tile_heads(sel_bias)

    mid = [chunk_at(r) for r in range(1 - n_mid, 1)]
    lo_rows, lo_kill = chunk_at(-n_mid)
    hi_rows, _ = chunk_at(1)
    own_rows, _ = chunk_at(0)
    kw = lambda g: kaug_ref.at[0, N_GROUPS + g]
    vw = lambda g: vt_ref.at[0, N_GROUPS + g]
    slc_keys = lambda g, rows: jnp.concatenate([kaug_ref[0, g, rows, :], oh_ref[rows, :]], axis=1)
    q_half = lambda g, w, nrow: jnp.concatenate([qaug_ref[g, 0:nrow, s] for s in half_lanes(w)], axis=1)

    def slc_scores(i):
        sts, vss, dlss = [], [], []
        for g in range(N_GROUPS):
            word = lists_ref[(step_id * N_GROUPS + g) * LIST_WORDS + i]
            ks, vs, dls = [], [], []
            for u in range(SLC_BATCH[g]):
                cid = lax.shift_right_logical(word, CHUNK_ID_BITS * u) & VOID_CHUNK
                valid = cid < VOID_CHUNK
                c = jnp.where(valid, cid, 0)
                rows = pl.ds(pl.multiple_of(c * KEY_CHUNK, KEY_CHUNK), KEY_CHUNK)
                ks.append(jnp.concatenate([kaug_ref[0, g, rows, :], oh_ref[rows, :]], axis=1))
                vs.append(vt_ref[0, g, :, rows])
                dls.append(slope_rows[g] * (start - c * KEY_CHUNK).astype(F32)
                           + jnp.where(valid, 0.0, MASK_BIG))
            sts.append(_dot(jnp.concatenate(ks, axis=0), qaug_ref[g]))
            vss.append(jnp.concatenate(vs, axis=1))
            dlss.append(dls)
        return sts, vss, dlss

    wmid_st = [_dot(jnp.concatenate([kw(g)[rows, :] for rows, _ in mid], axis=0), qaug_ref[g, 0:LANES, :])
               + mid_bias for g in groups]
    wlo_st = [_dot(kw(g)[lo_rows, :], q_half(g, 0, LANES)) + upper_bias for g in groups]
    whi_st = [_dot(kw(g)[hi_rows, :], q_half(g, 1, LANES)) + lower_bias for g in groups]
    own_st = [_dot(slc_keys(g, own_rows), qaug_ref[g]) + own_bias for g in groups]
    ohi_st = [_dot(slc_keys(g, hi_rows), q_half(g, 1, 2 * LANES)) + lower_bias for g in groups]
    st0, vs0, dls0 = slc_scores(0)

    wmid_dls = [[slope_rows[g] * float(-KEY_CHUNK * r) + kill
                 for r, (_, kill) in zip(range(1 - n_mid, 1), mid)] for g in groups]
    wlo_dl = [half(slope_rows[g], 0) * float(KEY_CHUNK * n_mid) + lo_kill for g in groups]
    hi_dl = [half(slope_rows[g], 1) * float(-KEY_CHUNK) for g in groups]
    neg_half = jnp.full((1, HPG * KEY_CHUNK), NEG, F32)
    win_m = [jnp.maximum(_col_max(wmid_st[g], wmid_dls[g], KEY_CHUNK),
                         unhalf(_col_max(wlo_st[g], [wlo_dl[g]], KEY_CHUNK),
                                _col_max(whi_st[g], [hi_dl[g]], KEY_CHUNK))) for g in groups]
    own_m = [jnp.maximum(jnp.maximum(_col_max(own_st[g], [0.0], KEY_CHUNK),
                                     _col_max(st0[g], dls0[g], KEY_CHUNK)),
                         unhalf(neg_half, _col_max(ohi_st[g], [hi_dl[g]], KEY_CHUNK))) for g in groups]
    probs = functools.partial(_probs, chunk=KEY_CHUNK, exp_dtype=BF16)
    wmid_p = [probs(wmid_st[g], wmid_dls[g], win_m[g]) for g in groups]
    wlo_p = [probs(wlo_st[g], [wlo_dl[g]], half(win_m[g], 0)) for g in groups]
    whi_p = [probs(whi_st[g], [hi_dl[g]], half(win_m[g], 1)) for g in groups]
    own_p = [probs(own_st[g], [0.0], own_m[g]) for g in groups]
    ohi_p = [probs(ohi_st[g], [hi_dl[g]], half(own_m[g], 1)) for g in groups]
    p0 = [probs(st0[g], dls0[g], own_m[g]) for g in groups]
    o_win = []
    for g in groups:
        acc = _dot(jnp.concatenate([vw(g)[:, rows] for rows, _ in mid], axis=1), wmid_p[g])
        acc = acc + unhalf(_dot(vw(g)[:, lo_rows], wlo_p[g]), _dot(vw(g)[:, hi_rows], whi_p[g]))
        o_win.append(normalize(acc))
    zero_half = jnp.zeros((VT_ROWS, HPG * KEY_CHUNK), F32)
    for g in groups:
        m_ref[g] = own_m[g]
        acc_ref[g] = (_dot(vt_ref[0, g, :, own_rows], own_p[g])
                      + unhalf(zero_half, _dot(vt_ref[0, g, :, hi_rows], ohi_p[g]))
                      + _dot(vs0[g], p0[g]))

    def slc_body(i, carry):
        sts, vss, dlss = slc_scores(i)
        m_old = [m_ref[g] for g in range(N_GROUPS)]
        m_new = [jnp.maximum(m_old[g], _col_max(sts[g], dlss[g], KEY_CHUNK)) for g in range(N_GROUPS)]
        ps = [probs(sts[g], dlss[g], m_new[g]) for g in range(N_GROUPS)]
        for g in range(N_GROUPS):
            acc_ref[g] = jnp.exp2(m_old[g] - m_new[g]) * acc_ref[g] + _dot(vss[g], ps[g])
            m_ref[g] = m_new[g]
        return carry

    lax.fori_loop(1, counts_ref[step_id], slc_body, 0)

    per_head = []
    for g in range(N_GROUPS):
        o_slc = normalize(acc_ref[g])
        for j in range(HPG):
            h = HPG * g + j
            lanes = slice(NSA_QB * j, NSA_QB * (j + 1))
            per_head.append(gt[3 * h + 1:3 * h + 2, :] * o_slc[:, lanes]
                            + gt[3 * h + 2:3 * h + 3, :] * o_win[g][:, lanes])
    o_t = jnp.concatenate(per_head, axis=0)
    out_ref[0] = (ocmp_ref[0] + o_t).T.astype(BF16)


def _slc_win(counts, lists, qt, kaug, vt, onehot, selt, gatest, ocmp):
    bsz, _, t = qt.shape
    nqb = t // NSA_QB
    once = lambda shape, imap: pl.BlockSpec(shape, imap, pipeline_mode=pl.Buffered(1))
    grid_spec = pltpu.PrefetchScalarGridSpec(
        num_scalar_prefetch=2,
        grid=(bsz, nqb),
        in_specs=[pl.BlockSpec((1, NSA_W, NSA_QB), lambda b, i, *_: (b, 0, i)),
                  pl.BlockSpec((1, 2 * N_GROUPS, t, LANES), lambda b, i, *_: (b, 0, 0, 0)),
                  pl.BlockSpec((1, 2 * N_GROUPS, VT_ROWS, t), lambda b, i, *_: (b, 0, 0, 0)),
                  once((t, N_BLK_PAD), lambda b, i, *_: (0, 0)),
                  pl.BlockSpec((1, N_GROUPS, N_BLK_PAD, NSA_QB), lambda b, i, *_: (b, 0, 0, i)),
                  pl.BlockSpec((1, GATET_ROWS, NSA_QB), lambda b, i, *_: (b, 0, i)),
                  pl.BlockSpec((1, NSA_W, NSA_QB), lambda b, i, *_: (b, 0, i))],
        out_specs=pl.BlockSpec((1, NSA_QB, NSA_W), lambda b, i, *_: (b, i, 0)),
        scratch_shapes=[pltpu.VMEM((N_GROUPS, 2 * LANES, HPG * NSA_QB), BF16),
                        pltpu.VMEM((N_GROUPS, 1, HPG * NSA_QB), F32),
                        pltpu.VMEM((N_GROUPS, VT_ROWS, HPG * NSA_QB), F32)],
    )
    return pl.pallas_call(
        functools.partial(_slc_win_kernel, nqb=nqb),
        grid_spec=grid_spec,
        out_shape=jax.ShapeDtypeStruct((bsz, t, NSA_W), BF16),
        compiler_params=pltpu.CompilerParams(dimension_semantics=("arbitrary", "arbitrary"),
                                             vmem_limit_bytes=VMEM_LIMIT),
        name="slc_win",
    )(counts, lists, qt, kaug, vt, onehot, selt, gatest, ocmp)


def _merge_kernel(x_ref, onsa_ref, u_ref, vn_ref, g_ref, wm_ref, bm_ref, ws_ref, bs_ref,
                  wpa_ref, wpb_ref, wo_ref, *refs, tm, n_cast):
    cast_in, h_ref, cast_out = refs[:n_cast], refs[n_cast], refs[n_cast + 1:]
    _cast_blocks(cast_in, cast_out)
    x = x_ref[...]
    xn = _rms(x, g_ref[...]).astype(BF16)
    tril = _iota((GMLP_CHUNK, GMLP_CHUNK), 0) >= _iota((GMLP_CHUNK, GMLP_CHUNK), 1)
    sgu_rows = []
    for c in range(tm // GMLP_CHUNK):
        rows = slice(GMLP_CHUNK * c, GMLP_CHUNK * (c + 1))
        cols = []
        for g in range(GMLP_GROUPS):
            lanes = slice(LANES * g, LANES * (g + 1))
            w = jnp.where(tril, ws_ref[g], 0.0).astype(BF16)
            cols.append(_dot(w, vn_ref[rows, lanes]) + bs_ref[:, g:g + 1])
        sgu_rows.append(u_ref[rows, :] * jnp.concatenate(cols, axis=1))
    o_sgu = jnp.concatenate(sgu_rows, axis=0).astype(BF16)
    mg = jax.nn.sigmoid(_dot(xn, wm_ref[...]) + bm_ref[...])
    mixed = (mg[:, :D_MODEL] * _dot(onsa_ref[...], wpa_ref[...])
             + mg[:, D_MODEL:] * _dot(o_sgu, wpb_ref[...]))
    h_ref[...] = x + _dot(mixed.astype(BF16), wo_ref[...])


def _merge(x2, onsa2, u2, vn2, norm_g, wm, bm, ws, bs_t, wpa, wpb, wo, to_cast, tm):
    n = x2.shape[0]
    row = lambda w: pl.BlockSpec((tm, w), lambda i: (i, 0))
    full = lambda a: pl.BlockSpec(a.shape, lambda i: (0,) * a.ndim)
    cast_specs, cast_shapes = _cast_plan(to_cast, n // tm)
    outs = pl.pallas_call(
        functools.partial(_merge_kernel, tm=tm, n_cast=len(to_cast)),
        grid=(n // tm,),
        in_specs=[row(D_MODEL), row(NSA_W), row(GMLP_WIDTH), row(GMLP_WIDTH), full(norm_g),
                  full(wm), full(bm), full(ws), full(bs_t), full(wpa), full(wpb), full(wo)] + cast_specs,
        out_specs=[row(D_MODEL)] + cast_specs,
        out_shape=[jax.ShapeDtypeStruct((n, D_MODEL), F32)] + cast_shapes,
        compiler_params=pltpu.CompilerParams(dimension_semantics=("arbitrary",),
                                             vmem_limit_bytes=VMEM_LIMIT),
        name="merge",
    )(x2, onsa2, u2, vn2, norm_g, wm, bm, ws, bs_t, wpa, wpb, wo, *to_cast)
    return outs[0], outs[1:]


def _memkv_kernel(mem_ref, g_ref, w_ref, out_ref):
    out_ref[0] = _dot(_rms(mem_ref[0], g_ref[...]).astype(BF16), w_ref[...]).astype(BF16)


def _memkv(mem, norm_g, w):
    bsz, nm, _ = mem.shape
    return pl.pallas_call(
        _memkv_kernel,
        grid=(bsz,),
        in_specs=[pl.BlockSpec((1, nm, D_MODEL), lambda b: (b, 0, 0)),
                  pl.BlockSpec(norm_g.shape, lambda b: (0, 0)),
                  pl.BlockSpec(w.shape, lambda b: (0, 0))],
        out_specs=pl.BlockSpec((1, nm, 2 * MEM_W), lambda b: (b, 0, 0)),
        out_shape=jax.ShapeDtypeStruct((bsz, nm, 2 * MEM_W), BF16),
        compiler_params=pltpu.CompilerParams(dimension_semantics=("arbitrary",),
                                             vmem_limit_bytes=VMEM_LIMIT),
        name="memkv",
    )(mem, norm_g, w)


def _xattn_kernel(h_ref, g_ref, wq_ref, mkv_ref, wo_ref, out_ref, *, tm):
    halves = [slice(0, tm // 2), slice(tm // 2, tm)]
    head_lanes = [slice(MEM_HEAD_DIM * a, MEM_HEAD_DIM * (a + 1)) for a in range(MEM_HEADS)]
    hs = [h_ref[rows, :] for rows in halves]
    hqs = [(_dot(_rms(h, g_ref[...]).astype(BF16), wq_ref[...]) * (MEM_HEAD_DIM ** -0.5 * LOG2E)).astype(BF16)
           for h in hs]
    ss = [[_dot_nt(hq[:, lanes], mkv_ref[0, :, lanes]) for lanes in head_lanes] for hq in hqs]
    es = [[jnp.exp2(s - jnp.max(s, axis=-1, keepdims=True)) for s in s_half] for s_half in ss]
    os = []
    for e_half in es:
        heads = []
        for a, e in enumerate(e_half):
            v = mkv_ref[0, :, MEM_W + MEM_HEAD_DIM * a:MEM_W + MEM_HEAD_DIM * (a + 1)]
            heads.append(_dot(e.astype(BF16), v) * (1.0 / jnp.sum(e, axis=-1, keepdims=True)))
        os.append(jnp.concatenate(heads, axis=1).astype(BF16))
    for rows, h, o in zip(halves, hs, os):
        out_ref[rows, :] = h + _dot(o, wo_ref[...])


def _xattn(h2d, norm_g, wq, mkv, wo, tm, rows_per_batch):
    n = h2d.shape[0]
    nm = mkv.shape[1]
    tiles_per_batch = rows_per_batch // tm
    full = lambda a: pl.BlockSpec(a.shape, lambda i: (0,) * a.ndim)
    return pl.pallas_call(
        functools.partial(_xattn_kernel, tm=tm),
        grid=(n // tm,),
        in_specs=[pl.BlockSpec((tm, D_MODEL), lambda i: (i, 0)), full(norm_g), full(wq),
                  pl.BlockSpec((1, nm, 2 * MEM_W), lambda i: (i // tiles_per_batch, 0, 0)),
                  full(wo)],
        out_specs=pl.BlockSpec((tm, D_MODEL), lambda i: (i, 0)),
        out_shape=jax.ShapeDtypeStruct((n, D_MODEL), F32),
        compiler_params=pltpu.CompilerParams(dimension_semantics=("arbitrary",),
                                             vmem_limit_bytes=VMEM_LIMIT),
        name="xattn",
    )(h2d, norm_g, wq, mkv, wo)


def _ffn_kernel(h_ref, g_ref, wgu_ref, wd_ref, gf_ref, out_ref, *, d_ff):
    tm = h_ref.shape[0]
    halves = [slice(0, tm // 2), slice(tm // 2, tm)]
    hs = [h_ref[rows, :] for rows in halves]
    hns = [_rms(h, g_ref[...]).astype(BF16) for h in hs]
    gates = [_dot(hn, wgu_ref[:, :d_ff]) for hn in hns]
    ups = [_dot(hn, wgu_ref[:, d_ff:]) for hn in hns]
    acts = [(jax.nn.silu(gate) * up).astype(BF16) for gate, up in zip(gates, ups)]
    for rows, h, act in zip(halves, hs, acts):
        out_ref[rows, :] = _rms(h + _dot(act, wd_ref[...]), gf_ref[...])


def _ffn(h2d, norm_g, wgu, wd, norm_f, tm):
    n = h2d.shape[0]
    d_ff = wd.shape[0]
    full = lambda a: pl.BlockSpec(a.shape, lambda i: (0,) * a.ndim)
    once = lambda a: pl.BlockSpec(a.shape, lambda i: (0,) * a.ndim, pipeline_mode=pl.Buffered(1))
    return pl.pallas_call(
        functools.partial(_ffn_kernel, d_ff=d_ff),
        grid=(n // tm,),
        in_specs=[pl.BlockSpec((tm, D_MODEL), lambda i: (i, 0)), full(norm_g), once(wgu), once(wd),
                  full(norm_f)],
        out_specs=pl.BlockSpec((tm, D_MODEL), lambda i: (i, 0)),
        out_shape=jax.ShapeDtypeStruct((n, D_MODEL), F32),
        compiler_params=pltpu.CompilerParams(dimension_semantics=("arbitrary",),
                                             vmem_limit_bytes=VMEM_LIMIT),
        name="ffn",
    )(h2d, norm_g, wgu, wd, norm_f)


def _block_diag2(w):
    z = jnp.zeros_like(w)
    return jnp.concatenate([jnp.concatenate([w, z], axis=-1), jnp.concatenate([z, w], axis=-1)], axis=-2)


def _chunk_lists(flags):
    bsz = flags.shape[0]
    n_chunks = N_KEY_CHUNKS
    per_step = NSA_QB // Q_BLOCK
    nqb = flags.shape[1] // per_step
    f = flags[:, :, :, 0, :].reshape(bsz, nqb, per_step, N_GROUPS, n_chunks, 2).max(axis=(2, 5))
    cid = jnp.arange(n_chunks, dtype=jnp.int32)
    own = (NSA_QB // KEY_CHUNK) * jnp.arange(nqb, dtype=jnp.int32)[None, :, None, None]
    active = (f > 0) & (cid < own)
    n_active = active.sum(axis=-1)
    slot = jnp.cumsum(active, axis=-1) - 1
    hit = active[..., :, None] & (slot[..., :, None] == cid)
    ids = jnp.sum(jnp.where(hit, cid[:, None], 0), axis=-2)
    ids = jnp.where(cid < n_active[..., None], ids, VOID_CHUNK)
    words, n_batches = [], []
    for g, batch in enumerate(SLC_BATCH):
        n_batches.append((n_active[:, :, g] + batch - 1) // batch)
        padded = jnp.pad(ids[:, :, g], ((0, 0), (0, 0), (0, LIST_WORDS * batch - n_chunks)),
                         constant_values=VOID_CHUNK).reshape(bsz, nqb, LIST_WORDS, batch)
        words.append(functools.reduce(jnp.bitwise_or,
                                      [padded[..., u] << (CHUNK_ID_BITS * u) for u in range(batch)]))
    n_batches = functools.reduce(jnp.maximum, n_batches)
    words = jnp.stack(words, axis=2)
    return n_batches.reshape(-1).astype(jnp.int32), words.reshape(-1).astype(jnp.int32)


def kernel(x, mem, norm_mix, w_in, w_cmp_k1, w_cmp_k2, w_cmp_v1, w_cmp_v2, pe_cmp_k, pe_cmp_v, ln_sgu, w_spatial, b_spatial, w_proj_a, w_proj_b, w_merge, b_merge, w_out, norm_mem_q, norm_mem_kv, w_mq, w_mkv, w_mo, norm_ffn, w_gate_up, w_down, norm_final):
    bsz, t, d = x.shape
    depth = norm_mix.shape[0]
    assert d == D_MODEL and t % Q_BLOCK == 0 and t // SEL_BLOCK <= N_BLK_PAD
    assert t // SEL_BLOCK >= N_SELECT and depth == 1 and t % 1024 == 0
    n = bsz * t
    tm = 512
    h = x.reshape(n, d)
    c0, c1, c2, c3 = NSA_W, NSA_W + KV_W, NSA_W + 3 * KV_W, NSA_W + 3 * KV_W + 2 * GMLP_WIDTH
    onehot = (jnp.arange(t)[:, None] // SEL_BLOCK == jnp.arange(N_BLK_PAD)[None, :]).astype(BF16)
    pad_cols = lambda w, width: jnp.pad(w, ((0, 0), (0, width - w.shape[1])))
    for l in range(depth):
        wi = w_in[l]
        wk, wv = [], []
        for branch in range(2):
            base = c1 + KV_W * branch
            for g in range(N_GROUPS):
                wk.append(pad_cols(wi[:, base + HEAD_DIM * g:base + HEAD_DIM * (g + 1)], LANES))
                v0 = base + N_GROUPS * HEAD_DIM + HEAD_DIM * g
                wv.append(pad_cols(wi[:, v0:v0 + HEAD_DIM], VT_ROWS))
        wrow = jnp.concatenate([wi[:, c0:c1]] + wk + [wi[:, c2:c3]], axis=1).astype(BF16)
        wnt = jnp.concatenate([wi[:, :c0]] + wv + [pad_cols(wi[:, c3:], GATET_ROWS)], axis=1).T.astype(BF16)
        (qt, kvc2, kaug, vt, u2, vn2, gatest), (wm_b, wpa_b, wpb_b, wo_b, wmq_b, wmkv_b, wmo_b) = _inproj(
            h, norm_mix[l][None], ln_sgu[l][None], wrow, wnt,
            [w_merge[l], w_proj_a[l], w_proj_b[l], w_out[l], w_mq[l], w_mkv[l], w_mo[l]], 2 * tm, bsz, t)

        pe2 = jnp.stack([pe_cmp_k[l], pe_cmp_v[l]])
        pe2 = jnp.concatenate([pe2, pe2], axis=-1)
        w1 = jnp.stack([w_cmp_k1[l], w_cmp_v1[l]]).reshape(2, CMP_BLOCK, HEAD_DIM, CMP_HIDDEN)
        w1bd = _block_diag2(w1).astype(BF16)
        w2k = _block_diag2(w_cmp_k2[l]).astype(BF16)
        w2vt = _block_diag2(w_cmp_v2[l]).T.astype(BF16)
        kc, vct = _compress(kvc2.reshape(bsz, t, KV_W), pe2, w1bd, w2k, w2vt)
        ocmp, selt, flags = _cmp_topk(qt, kc, vct, gatest)
        counts, lists = _chunk_lists(flags)
        onsa = _slc_win(counts, lists, qt, kaug, vt, onehot, selt, gatest, ocmp)

        h, (wgu_b, wd_b) = _merge(h, onsa.reshape(n, NSA_W), u2, vn2, norm_mix[l][None], wm_b,
                                  b_merge[l][None], w_spatial[l], b_spatial[l].T, wpa_b, wpb_b, wo_b,
                                  [w_gate_up[l], w_down[l]], 2 * tm)

        mkv = _memkv(mem, norm_mem_kv[l][None], wmkv_b)
        h = _xattn(h, norm_mem_q[l][None], wmq_b, mkv, wmo_b, 2 * tm, t)
        h = _ffn(h, norm_ffn[l][None], wgu_b, wd_b, norm_final[None], tm)
    return h.reshape(bsz, t, d)
```

```python
import functools

import jax
import jax.numpy as jnp
from jax import lax
from jax.experimental import pallas as pl
from jax.experimental.pallas import tpu as pltpu

F32 = jnp.float32
BF16 = jnp.bfloat16

LANES = 128
SUBLANES = 8
D_MODEL = 1024
N_HEADS = 8
HEAD_DIM = 64
N_GROUPS = 2
HPG = N_HEADS // N_GROUPS
CMP_BLOCK = 32
CMP_STRIDE = 16
CMP_HIDDEN = 128
SEL_BLOCK = 64
N_SELECT = 16
WINDOW = 512
Q_BLOCK = 256
KEY_CHUNK = 128
N_BLK_PAD = 128
NSA_W = N_HEADS * HEAD_DIM
KV_W = 2 * N_GROUPS * HEAD_DIM
GMLP_WIDTH = 512
GMLP_GROUPS = 4
GMLP_CHUNK = 128
MEM_HEADS = 4
MEM_HEAD_DIM = 128
MEM_W = MEM_HEADS * MEM_HEAD_DIM
NSA_QB = 256
SLC_BATCH = (4, 5)
PEELED_WORDS = 2
N_KEY_CHUNKS = N_BLK_PAD // 2
CHUNK_ID_BITS = 6
VOID_CHUNK = N_KEY_CHUNKS - 1
LIST_WORDS = -(-N_KEY_CHUNKS // min(SLC_BATCH))
VT_ROWS = 80
VCT_ROWS = VT_ROWS + N_BLK_PAD
CMP_KEY_CHUNK = 128
N_FORCED = 3
TOPK_ROW_STEP = 64
CMP_BLOCKS = 2
GATET_ROWS = 32
MASK_BIG = 1e30
EPS = 1e-6
NEG = -1e30
REMOVED = -3e38
SLOPES = tuple(2.0 ** (-8.0 * (h + 1) / N_HEADS) for h in range(N_HEADS))
LOG2E = 1.4426950408889634
Q_SCALE = HEAD_DIM ** -0.5 * LOG2E
VMEM_LIMIT = 56 * 1024 * 1024


def _dot(a, b):
    return jnp.dot(a, b, preferred_element_type=F32)


def _dot_nt(a, b):
    return lax.dot_general(a, b, (((1,), (1,)), ((), ())), preferred_element_type=F32)


def _rms(x, g):
    return x * lax.rsqrt(jnp.mean(x * x, axis=-1, keepdims=True) + EPS) * g


def _iota(shape, dim):
    return lax.broadcasted_iota(jnp.int32, shape, dim)


def _slope_row(g, nq):
    return jnp.concatenate(
        [jnp.full((1, nq), SLOPES[HPG * g + j] * LOG2E, F32) for j in range(HPG)], axis=1)


def _slope_feature_rows(slope_row, n_rows):
    hi = slope_row.astype(BF16).astype(F32)
    r = _iota((n_rows, slope_row.shape[1]), 0)
    return jnp.where(r == 0, hi, jnp.where(r == 1, slope_row - hi, 0.0)).astype(BF16)


def _chunk_slabs(st, chunk):
    return [st[chunk * u:chunk * (u + 1)] for u in range(st.shape[0] // chunk)]


def _col_max(st, dls, chunk):
    parts = [s.reshape(chunk // SUBLANES, SUBLANES, s.shape[1]).max(axis=0) - dl
             for s, dl in zip(_chunk_slabs(st, chunk), dls)]
    return functools.reduce(jnp.maximum, parts).max(axis=0, keepdims=True)


def _probs(st, dls, m, chunk, exp_dtype=F32):
    return jnp.concatenate([jnp.exp2((s - (m + dl)).astype(exp_dtype)).astype(BF16)
                            for s, dl in zip(_chunk_slabs(st, chunk), dls)], axis=0)


def _cast_plan(weights, n_steps):
    for w in weights:
        assert w.shape[0] % (16 * n_steps) == 0, (w.shape, n_steps)
    specs = [pl.BlockSpec((w.shape[0] // n_steps, w.shape[1]), lambda i: (i, 0)) for w in weights]
    shapes = [jax.ShapeDtypeStruct(w.shape, BF16) for w in weights]
    return specs, shapes


def _cast_blocks(src_refs, dst_refs):
    for src, dst in zip(src_refs, dst_refs):
        dst[...] = src[...].astype(BF16)


def _inproj_kernel(x_ref, g_ref, lng_ref, wrow_ref, wnt_ref, *refs, tm, n_cast):
    cast_in, cast_out = refs[:n_cast], refs[len(refs) - n_cast:]
    qt_ref, kvc_ref, kaug_ref, vt_ref, u_ref, vn_ref, gatest_ref = refs[n_cast:len(refs) - n_cast]
    _cast_blocks(cast_in, cast_out)
    xn = _rms(x_ref[...], g_ref[...]).astype(BF16)
    r = _dot(xn, wrow_ref[...])
    nt = _dot_nt(wnt_ref[...], xn)
    qt_ref[0] = (nt[0:NSA_W] * Q_SCALE).astype(BF16)
    kvc_ref[...] = r[:, 0:KV_W]
    lane = _iota((tm, LANES), 1)
    key_feat = jnp.where((lane == HEAD_DIM) | (lane == HEAD_DIM + 1),
                         _iota((tm, LANES), 0) & (KEY_CHUNK - 1), 0).astype(F32)
    ones_row = jnp.where(_iota((VT_ROWS, tm), 0) == HEAD_DIM, 1.0, 0.0)
    for a in range(2 * N_GROUPS):
        kaug_ref[0, a] = (r[:, KV_W + LANES * a:KV_W + LANES * (a + 1)] + key_feat).astype(BF16)
        vt_ref[0, a] = (nt[NSA_W + VT_ROWS * a:NSA_W + VT_ROWS * (a + 1)] + ones_row).astype(BF16)
    uv = jax.nn.gelu(r[:, KV_W + 2 * N_GROUPS * LANES:])
    u_ref[...] = uv[:, :GMLP_WIDTH]
    v = uv[:, GMLP_WIDTH:]
    vc = v - jnp.mean(v, axis=-1, keepdims=True)
    vn = vc * lax.rsqrt(jnp.mean(vc * vc, axis=-1, keepdims=True) + EPS) * lng_ref[...]
    vn_ref[...] = vn.astype(BF16)
    gatest_ref[0] = jax.nn.sigmoid(nt[NSA_W + 2 * N_GROUPS * VT_ROWS:])


def _inproj(x2, norm_g, ln_g, wrow, wnt, to_cast, tm, bsz, t):
    n = x2.shape[0]
    tpb = t // tm
    row = lambda w: pl.BlockSpec((tm, w), lambda i: (i, 0))
    full = lambda a: pl.BlockSpec(a.shape, lambda i: (0,) * a.ndim)
    cast_specs, cast_shapes = _cast_plan(to_cast, n // tm)
    outs = pl.pallas_call(
        functools.partial(_inproj_kernel, tm=tm, n_cast=len(to_cast)),
        grid=(n // tm,),
        in_specs=[row(D_MODEL), full(norm_g), full(ln_g), full(wrow), full(wnt)] + cast_specs,
        out_specs=[pl.BlockSpec((1, NSA_W, tm), lambda i: (i // tpb, 0, i % tpb)),
                   row(KV_W),
                   pl.BlockSpec((1, 2 * N_GROUPS, tm, LANES), lambda i: (i // tpb, 0, i % tpb, 0)),
                   pl.BlockSpec((1, 2 * N_GROUPS, VT_ROWS, tm), lambda i: (i // tpb, 0, 0, i % tpb)),
                   row(GMLP_WIDTH), row(GMLP_WIDTH),
                   pl.BlockSpec((1, GATET_ROWS, tm), lambda i: (i // tpb, 0, i % tpb))] + cast_specs,
        out_shape=[jax.ShapeDtypeStruct((bsz, NSA_W, t), BF16),
                   jax.ShapeDtypeStruct((n, KV_W), F32),
                   jax.ShapeDtypeStruct((bsz, 2 * N_GROUPS, t, LANES), BF16),
                   jax.ShapeDtypeStruct((bsz, 2 * N_GROUPS, VT_ROWS, t), BF16),
                   jax.ShapeDtypeStruct((n, GMLP_WIDTH), F32),
                   jax.ShapeDtypeStruct((n, GMLP_WIDTH), BF16),
                   jax.ShapeDtypeStruct((bsz, GATET_ROWS, t), F32)] + cast_shapes,
        compiler_params=pltpu.CompilerParams(dimension_semantics=("arbitrary",),
                                             vmem_limit_bytes=VMEM_LIMIT),
        name="inproj",
    )(x2, norm_g, ln_g, wrow, wnt, *to_cast)
    return outs[:7], outs[7:]


def _compress_kernel(xk_ref, xv_ref, pe_ref, w1_ref, w2k_ref, w2vt_ref, kc_ref, vct_ref, *, nc):
    outs = []
    for j, x_ref in enumerate((xk_ref, xv_ref)):
        a = jnp.zeros((nc, 2 * CMP_HIDDEN), F32)
        b = jnp.zeros((nc, 2 * CMP_HIDDEN), F32)
        for t in range(0, CMP_STRIDE, 2):
            xs = [x_ref[0, pl.ds(t + e, nc, stride=CMP_STRIDE), :] for e in range(2)]

            def half_block(first):
                lhs = jnp.concatenate([(xs[e] + pe_ref[j, first + t + e:first + t + e + 1, :]).astype(BF16)
                                       for e in range(2)], axis=1)
                rhs = jnp.concatenate([w1_ref[j, first + t], w1_ref[j, first + t + 1]], axis=0)
                return _dot(lhs, rhs)

            a = a + half_block(0)
            b = b + half_block(CMP_STRIDE)
        outs.append(jax.nn.gelu(a + pltpu.roll(b, nc - 1, 0)).astype(BF16))
    k2 = _dot(outs[0], w2k_ref[...])
    v_t = _dot_nt(w2vt_ref[...], outs[1])
    lane = _iota((nc, LANES), 1)
    key_feat = jnp.where((lane == HEAD_DIM) | (lane == HEAD_DIM + 1),
                         CMP_STRIDE * (_iota((nc, LANES), 0) & (CMP_KEY_CHUNK - 1)), 0).astype(F32)
    ci =_iota((N_BLK_PAD, nc), 1) * CMP_STRIDE
    sj = _iota((N_BLK_PAD, nc), 0) * SEL_BLOCK
    overlap_t = jnp.where((ci < sj + SEL_BLOCK) & (ci + (CMP_BLOCK - 1) >= sj), 1.0, 0.0).astype(BF16)
    ones_rows = jnp.where(_iota((VT_ROWS - HEAD_DIM, nc), 0) == 0, 1.0, 0.0).astype(BF16)
    for g in range(N_GROUPS):
        kg = k2 if g == 0 else pltpu.roll(k2, HEAD_DIM, 1)
        kc_ref[0, g] = jnp.where(lane < HEAD_DIM, kg, key_feat).astype(BF16)
        vct_ref[0, g, 0:HEAD_DIM, :] = v_t[HEAD_DIM * g:HEAD_DIM * (g + 1), :].astype(BF16)
        vct_ref[0, g, HEAD_DIM:VT_ROWS, :] = ones_rows
        vct_ref[0, g, VT_ROWS:VCT_ROWS, :] = overlap_t


def _compress(kvc3, pe2, w1bd, w2k, w2vt):
    bsz, t, _ = kvc3.shape
    nc = t // CMP_STRIDE
    full = lambda a: pl.BlockSpec(a.shape, lambda b: (0,) * a.ndim)
    return pl.pallas_call(
        functools.partial(_compress_kernel, nc=nc),
        grid=(bsz,),
        in_specs=[pl.BlockSpec((1, t, LANES), lambda b: (b, 0, 0)),
                  pl.BlockSpec((1, t, LANES), lambda b: (b, 0, 1)),
                  full(pe2), full(w1bd), full(w2k), full(w2vt)],
        out_specs=[pl.BlockSpec((1, N_GROUPS, nc, LANES), lambda b: (b, 0, 0, 0)),
                   pl.BlockSpec((1, N_GROUPS, VCT_ROWS, nc), lambda b: (b, 0, 0, 0))],
        out_shape=[jax.ShapeDtypeStruct((bsz, N_GROUPS, nc, LANES), BF16),
                   jax.ShapeDtypeStruct((bsz, N_GROUPS, VCT_ROWS, nc), BF16)],
        compiler_params=pltpu.CompilerParams(dimension_semantics=("arbitrary",),
                                             vmem_limit_bytes=VMEM_LIMIT),
        name="compress",
    )(kvc3, kvc3, pe2, w1bd, w2k, w2vt)


def _cmp_topk_kernel(qt_ref, kc_ref, vct_ref, gatest_ref, ocmp_ref, selt_ref, flags_ref,
                     m_ref, acc_ref, *, ncp):
    step = pl.program_id(1)
    n_chunks = ncp // CMP_KEY_CHUNK
    chunk_tokens = CMP_KEY_CHUNK * CMP_STRIDE
    tile_heads = lambda a: jnp.concatenate([a] * HPG, axis=1)
    tail_chunks = min(2, n_chunks)
    head_chunks = n_chunks - tail_chunks
    tail_keys = tail_chunks * CMP_KEY_CHUNK
    blocks = range(CMP_BLOCKS)
    groups = range(N_GROUPS)
    items = [(bi, g) for bi in blocks for g in groups]
    start = [(step * CMP_BLOCKS + bi) * Q_BLOCK for bi in blocks]
    qlanes = [slice(Q_BLOCK * bi, Q_BLOCK * (bi + 1)) for bi in blocks]
    nck = [(start[bi] + Q_BLOCK - CMP_BLOCK) // CMP_STRIDE // CMP_KEY_CHUNK + 1 for bi in blocks]
    tail_c0 = [jnp.maximum(nck[bi] - tail_chunks, 0) for bi in blocks]
    tail_rows = [pl.ds(pl.multiple_of(tail_c0[bi] * CMP_KEY_CHUNK, CMP_KEY_CHUNK), tail_keys)
                 for bi in blocks]
    gt = gatest_ref[0]
    slope_rows = [_slope_row(g, Q_BLOCK) for g in groups]
    qas = {(bi, g): jnp.concatenate(
        [jnp.concatenate([qt_ref[0, HEAD_DIM * (HPG * g + j):HEAD_DIM * (HPG * g + j + 1), qlanes[bi]]
                          for j in range(HPG)], axis=1),
         _slope_feature_rows(slope_rows[g], LANES - HEAD_DIM)], axis=0) for bi, g in items}

    def delta(bi, g, c):
        return slope_rows[g] * (start[bi] - c * chunk_tokens).astype(F32)

    for bi in blocks:
        if head_chunks > 0:
            @pl.when(nck[bi] > tail_chunks)
            def _():
                sts = [_dot(kc_ref[0, g, 0:head_chunks * CMP_KEY_CHUNK, :], qas[bi, g]) for g in groups]
                dls = [[delta(bi, g, c) + jnp.where(c < nck[bi] - tail_chunks, 0.0, MASK_BIG)
                        for c in range(head_chunks)] for g in groups]
                ms = [_col_max(sts[g], dls[g], CMP_KEY_CHUNK) for g in groups]
                ps = [_probs(sts[g], dls[g], ms[g], CMP_KEY_CHUNK) for g in groups]
                for g in groups:
                    m_ref[bi, g] = ms[g]
                    acc_ref[bi, g] = _dot(vct_ref[0, g, :, 0:head_chunks * CMP_KEY_CHUNK], ps[g])

            @pl.when(nck[bi] <= tail_chunks)
            def _():
                m_ref[bi] = jnp.full(m_ref.shape[1:], NEG, F32)
                acc_ref[bi] = jnp.zeros(acc_ref.shape[1:], F32)

    def tail_and_select(n_rows):
        if head_chunks > 0:
            m_old = {it: m_ref[it[0], it[1]] for it in items}
            acc_old = {it: acc_ref[it[0], it[1]] for it in items}
        else:
            m_old = {it: jnp.full((1, HPG * Q_BLOCK), NEG, F32) for it in items}
            acc_old = {it: jnp.zeros((VCT_ROWS, HPG * Q_BLOCK), F32) for it in items}
        key_row = _iota((tail_keys, Q_BLOCK), 0)
        q_lane = _iota((tail_keys, Q_BLOCK), 1)
        tail_bias = []
        for bi in blocks:
            key_end = CMP_STRIDE * (tail_c0[bi] * CMP_KEY_CHUNK + key_row) + (CMP_BLOCK - 1) - start[bi]
            tail_bias.append(tile_heads(jnp.where(key_end <= q_lane, 0.0, NEG)))
        sts = {(bi, g): _dot(kc_ref[0, g, tail_rows[bi], :], qas[bi, g]) + tail_bias[bi] for bi, g in items}
        dls = {(bi, g): [delta(bi, g, tail_c0[bi] + u) for u in range(tail_chunks)] for bi, g in items}
        m_new = {it: jnp.maximum(m_old[it], _col_max(sts[it], dls[it], CMP_KEY_CHUNK)) for it in items}
        ps = {it: _probs(sts[it], dls[it], m_new[it], CMP_KEY_CHUNK) for it in items}
        accs = {(bi, g): jnp.exp2(m_old[bi, g] - m_new[bi, g]) * acc_old[bi, g]
                + _dot(vct_ref[0, g, :, tail_rows[bi]], ps[bi, g]) for bi, g in items}

        blk_n = _iota((n_rows, Q_BLOCK), 0)
        bf = blk_n.astype(F32)
        rk, cur = {}, []
        for bi in blocks:
            t_row = start[bi] + _iota((1, Q_BLOCK), 1)
            cur.append(lax.shift_right_logical(t_row, SEL_BLOCK.bit_length() - 1))
            forced = (blk_n == 0) | (blk_n == cur[bi]) | (blk_n == cur[bi] - 1)
            has_key = t_row >= CMP_BLOCK - 1
            per_head = []
            for g in groups:
                acc = accs[bi, g]
                inv_l = 1.0 / jnp.maximum(acc[HEAD_DIM:HEAD_DIM + 1], 1e-30)
                o = acc[0:HEAD_DIM] * inv_l
                imp_h = acc[VT_ROWS:VT_ROWS + n_rows] * inv_l
                imp = sum(imp_h[:, Q_BLOCK * j:Q_BLOCK * (j + 1)] for j in range(HPG))
                imp = jnp.where(has_key, imp, 0.0)
                for j in range(HPG):
                    h = HPG * g + j
                    per_head.append(jnp.where(
                        has_key, gt[3 * h:3 * h + 1, qlanes[bi]] * o[:, Q_BLOCK * j:Q_BLOCK * (j + 1)], 0.0))
                rk[bi, g] = jnp.where(blk_n <= cur[bi], jnp.where(forced, REMOVED, imp), NEG)
            ocmp_ref[0, :, qlanes[bi]] = jnp.concatenate(per_head, axis=0)

        for _ in range(N_SELECT - N_FORCED):
            for it in items:
                m = jnp.max(rk[it], axis=0, keepdims=True)
                idx = jnp.min(jnp.where(rk[it] == m, bf, float(N_BLK_PAD)), axis=0, keepdims=True)
                rk[it] = jnp.where(bf == idx, REMOVED, rk[it])
        ones8 = jnp.ones((SUBLANES, Q_BLOCK), F32)
        for bi, g in items:
            sel = jnp.where((blk_n <= cur[bi]) & (rk[bi, g] < 2.0 * NEG), 1.0, 0.0)
            selt_ref[0, g, 0:n_rows, qlanes[bi]] = sel.astype(BF16)
            flag = (_dot_nt(ones8, sel) > 0.5).astype(jnp.int32)
            if n_rows < N_BLK_PAD:
                selt_ref[0, g, n_rows:N_BLK_PAD, qlanes[bi]] = jnp.zeros((N_BLK_PAD - n_rows, Q_BLOCK), BF16)
                flag = jnp.concatenate([flag, jnp.zeros((SUBLANES, N_BLK_PAD - n_rows), jnp.int32)], axis=1)
            flags_ref[0, bi, g] = flag

    causal_rows = (start[-1] + Q_BLOCK) // SEL_BLOCK
    for n_rows in range(TOPK_ROW_STEP, N_BLK_PAD + 1, TOPK_ROW_STEP):
        @pl.when((causal_rows > n_rows - TOPK_ROW_STEP) & (causal_rows <= n_rows))
        def _():
            tail_and_select(n_rows)


def _cmp_topk(qt, kc, vct, gatest):
    bsz, _, t = qt.shape
    ncp = t // CMP_STRIDE
    nqb = t // Q_BLOCK
    step_q = CMP_BLOCKS * Q_BLOCK
    return pl.pallas_call(
        functools.partial(_cmp_topk_kernel, ncp=ncp),
        grid=(bsz, nqb // CMP_BLOCKS),
        in_specs=[pl.BlockSpec((1, NSA_W, step_q), lambda b, i: (b, 0, i)),
                  pl.BlockSpec((1, N_GROUPS, ncp, LANES), lambda b, i: (b, 0, 0, 0)),
                  pl.BlockSpec((1, N_GROUPS, VCT_ROWS, ncp), lambda b, i: (b, 0, 0, 0)),
                  pl.BlockSpec((1, GATET_ROWS, step_q), lambda b, i: (b, 0, i))],
        out_specs=[pl.BlockSpec((1, NSA_W, step_q), lambda b, i: (b, 0, i)),
                   pl.BlockSpec((1, N_GROUPS, N_BLK_PAD, step_q), lambda b, i: (b, 0, 0, i)),
                   pl.BlockSpec((1, CMP_BLOCKS, N_GROUPS, SUBLANES, N_BLK_PAD), lambda b, i: (b, i, 0, 0, 0))],
        out_shape=[jax.ShapeDtypeStruct((bsz, NSA_W, t), F32),
                   jax.ShapeDtypeStruct((bsz, N_GROUPS, N_BLK_PAD, t), BF16),
                   jax.ShapeDtypeStruct((bsz, nqb, N_GROUPS, SUBLANES, N_BLK_PAD), jnp.int32)],
        scratch_shapes=[pltpu.VMEM((CMP_BLOCKS, N_GROUPS, 1, HPG * Q_BLOCK), F32),
                        pltpu.VMEM((CMP_BLOCKS, N_GROUPS, VCT_ROWS, HPG * Q_BLOCK), F32)],
        compiler_params=pltpu.CompilerParams(dimension_semantics=("arbitrary", "arbitrary"),
                                             vmem_limit_bytes=VMEM_LIMIT),
        name="cmp_topk",
    )(qt, kc, vct, gatest)


def _slc_win_kernel(counts_ref, lists_ref, qt_ref, kaug_ref, vt_ref, oh_ref, selt_ref, gatest_ref,
                    ocmp_ref, out_ref, qaug_ref, m_ref, acc_ref, *, nqb):
    b = pl.program_id(0)
    qb = pl.program_id(1)
    step_id = b * nqb + qb
    start = qb * NSA_QB
    tile_heads = lambda a: jnp.concatenate([a] * HPG, axis=1)
    groups = range(N_GROUPS)
    gt = gatest_ref[0]
    slope_rows = [_slope_row(g, NSA_QB) for g in groups]

    def normalize(acc):
        return acc[0:HEAD_DIM] / jnp.maximum(acc[HEAD_DIM:HEAD_DIM + 1], 1e-30)

    def half_lanes(w):
        return [slice(NSA_QB * h + KEY_CHUNK * w, NSA_QB * h + KEY_CHUNK * (w + 1)) for h in range(HPG)]

    def half(a, w):
        return jnp.concatenate([a[:, s] for s in half_lanes(w)], axis=1)

    def unhalf(lo, hi):
        return jnp.concatenate([x[:, KEY_CHUNK * h:KEY_CHUNK * (h + 1)]
                                for h in range(HPG) for x in (lo, hi)], axis=1)

    def chunk_at(rel):
        pos = start + rel * KEY_CHUNK
        rows = pl.ds(pl.multiple_of(jnp.maximum(pos, 0), KEY_CHUNK), KEY_CHUNK)
        return rows, (jnp.where(pos >= 0, 0.0, MASK_BIG) if rel < 0 else 0.0)

    ki = _iota((KEY_CHUNK, KEY_CHUNK), 0)
    qi = _iota((KEY_CHUNK, KEY_CHUNK), 1)
    upper_bias = tile_heads(jnp.where(ki > qi, 0.0, NEG))
    lower_bias = tile_heads(jnp.where(ki <= qi, 0.0, NEG))
    n_mid = WINDOW // KEY_CHUNK
    mid_dist = (_iota((n_mid * KEY_CHUNK, NSA_QB), 1) + (n_mid - 1) * KEY_CHUNK
                - _iota((n_mid * KEY_CHUNK, NSA_QB), 0))
    mid_bias = tile_heads(jnp.where((mid_dist >= 0) & (mid_dist < WINDOW), 0.0, NEG))
    own_bias = tile_heads(jnp.where(_iota((KEY_CHUNK, NSA_QB), 0) <= _iota((KEY_CHUNK, NSA_QB), 1), 0.0, NEG))

    for g in range(N_GROUPS):
        qaug_ref[g, 0:HEAD_DIM, :] = jnp.concatenate(
            [qt_ref[0, HEAD_DIM * (HPG * g + j):HEAD_DIM * (HPG * g + j + 1), :] for j in range(HPG)],
            axis=1)
        qaug_ref[g, HEAD_DIM:LANES, :] = _slope_feature_rows(slope_rows[g], LANES - HEAD_DIM)
        sel_bias = ((selt_ref[0, g].astype(F32) - 1.0) * MASK_BIG).astype(BF16)
        qaug_ref[g, LANES:2 * LANES, :] = tile_heads(sel_bias)

    mid = [chunk_at(r) for r in range(1 - n_mid, 1)]
    lo_rows, lo_kill = chunk_at(-n_mid)
    hi_rows, _ = chunk_at(1)
    own_rows, _ = chunk_at(0)
    kw = lambda g: kaug_ref.at[0, N_GROUPS + g]
    vw = lambda g: vt_ref.at[0, N_GROUPS + g]
    slc_keys = lambda g, rows: jnp.concatenate([kaug_ref[0, g, rows, :], oh_ref[rows, :]], axis=1)
    q_half = lambda g, w, nrow: jnp.concatenate([qaug_ref[g, 0:nrow, s] for s in half_lanes(w)], axis=1)

    def slc_scores(i, n_words=1):
        sts, vss, dlss = [], [], []
        for g in range(N_GROUPS):
            words = [lists_ref[(step_id * N_GROUPS + g) * LIST_WORDS + i + w] for w in range(n_words)]
            ks, vs, dls = [], [], []
            for word, u in [(word, u) for word in words for u in range(SLC_BATCH[g])]:
                cid = lax.shift_right_logical(word, CHUNK_ID_BITS * u) & VOID_CHUNK
                valid = cid < VOID_CHUNK
                c = jnp.where(valid, cid, 0)
                rows = pl.ds(pl.multiple_of(c * KEY_CHUNK, KEY_CHUNK), KEY_CHUNK)
                ks.append(jnp.concatenate([kaug_ref[0, g, rows, :], oh_ref[rows, :]], axis=1))
                vs.append(vt_ref[0, g, :, rows])
                dls.append(slope_rows[g] * (start - c * KEY_CHUNK).astype(F32)
                           + jnp.where(valid, 0.0, MASK_BIG))
            sts.append(_dot(jnp.concatenate(ks, axis=0), qaug_ref[g]))
            vss.append(jnp.concatenate(vs, axis=1))
            dlss.append(dls)
        return sts, vss, dlss

    wmid_st = [_dot(jnp.concatenate([kw(g)[rows, :] for rows, _ in mid], axis=0), qaug_ref[g, 0:LANES, :])
               + mid_bias for g in groups]
    wlo_st = [_dot(kw(g)[lo_rows, :], q_half(g, 0, LANES)) + upper_bias for g in groups]
    whi_st = [_dot(kw(g)[hi_rows, :], q_half(g, 1, LANES)) + lower_bias for g in groups]
    own_st = [_dot(slc_keys(g, own_rows), qaug_ref[g]) + own_bias for g in groups]
    ohi_st = [_dot(slc_keys(g, hi_rows), q_half(g, 1, 2 * LANES)) + lower_bias for g in groups]
    st0, vs0, dls0 = slc_scores(0, PEELED_WORDS)

    wmid_dls = [[slope_rows[g] * float(-KEY_CHUNK * r) + kill
                 for r, (_, kill) in zip(range(1 - n_mid, 1), mid)] for g in groups]
    wlo_dl = [half(slope_rows[g], 0) * float(KEY_CHUNK * n_mid) + lo_kill for g in groups]
    hi_dl = [half(slope_rows[g], 1) * float(-KEY_CHUNK) for g in groups]
    neg_half = jnp.full((1, HPG * KEY_CHUNK), NEG, F32)
    win_m = [jnp.maximum(_col_max(wmid_st[g], wmid_dls[g], KEY_CHUNK),
                         unhalf(_col_max(wlo_st[g], [wlo_dl[g]], KEY_CHUNK),
                                _col_max(whi_st[g], [hi_dl[g]], KEY_CHUNK))) for g in groups]
    own_m = [jnp.maximum(jnp.maximum(_col_max(own_st[g], [0.0], KEY_CHUNK),
                                     _col_max(st0[g], dls0[g], KEY_CHUNK)),
                         unhalf(neg_half, _col_max(ohi_st[g], [hi_dl[g]], KEY_CHUNK))) for g in groups]
    probs = functools.partial(_probs, chunk=KEY_CHUNK, exp_dtype=BF16)
    wmid_p = [probs(wmid_st[g], wmid_dls[g], win_m[g]) for g in groups]
    wlo_p = [probs(wlo_st[g], [wlo_dl[g]], half(win_m[g], 0)) for g in groups]
    whi_p = [probs(whi_st[g], [hi_dl[g]], half(win_m[g], 1)) for g in groups]
    own_p = [probs(own_st[g], [0.0], own_m[g]) for g in groups]
    ohi_p = [probs(ohi_st[g], [hi_dl[g]], half(own_m[g], 1)) for g in groups]
    p0 = [probs(st0[g], dls0[g], own_m[g]) for g in groups]
    o_win = []
    for g in groups:
        acc = _dot(jnp.concatenate([vw(g)[:, rows] for rows, _ in mid], axis=1), wmid_p[g])
        acc = acc + unhalf(_dot(vw(g)[:, lo_rows], wlo_p[g]), _dot(vw(g)[:, hi_rows], whi_p[g]))
        o_win.append(normalize(acc))
    zero_half = jnp.zeros((VT_ROWS, HPG * KEY_CHUNK), F32)
    for g in groups:
        m_ref[g] = own_m[g]
        acc_ref[g] = (_dot(vt_ref[0, g, :, own_rows], own_p[g])
                      + unhalf(zero_half, _dot(vt_ref[0, g, :, hi_rows], ohi_p[g]))
                      + _dot(vs0[g], p0[g]))

    def slc_body(i, carry):
        sts, vss, dlss = slc_scores(i)
        m_old = [m_ref[g] for g in range(N_GROUPS)]
        m_new = [jnp.maximum(m_old[g], _col_max(sts[g], dlss[g], KEY_CHUNK)) for g in range(N_GROUPS)]
        ps = [probs(sts[g], dlss[g], m_new[g]) for g in range(N_GROUPS)]
        for g in range(N_GROUPS):
            acc_ref[g] = jnp.exp2(m_old[g] - m_new[g]) * acc_ref[g] + _dot(vss[g], ps[g])
            m_ref[g] = m_new[g]
        return carry

    lax.fori_loop(PEELED_WORDS, counts_ref[step_id], slc_body, 0)

    per_head = []
    for g in range(N_GROUPS):
        o_slc = normalize(acc_ref[g])
        for j in range(HPG):
            h = HPG * g + j
            lanes = slice(NSA_QB * j, NSA_QB * (j + 1))
            per_head.append(gt[3 * h + 1:3 * h + 2, :] * o_slc[:, lanes]
                            + gt[3 * h + 2:3 * h + 3, :] * o_win[g][:, lanes])
    o_t = jnp.concatenate(per_head, axis=0)
    out_ref[0] = (ocmp_ref[0] + o_t).T.astype(BF16)


def _slc_win(counts, lists, qt, kaug, vt, onehot, selt, gatest, ocmp):
    bsz, _, t = qt.shape
    nqb = t // NSA_QB
    once = lambda shape, imap: pl.BlockSpec(shape, imap, pipeline_mode=pl.Buffered(1))
    grid_spec = pltpu.PrefetchScalarGridSpec(
        num_scalar_prefetch=2,
        grid=(bsz, nqb),
        in_specs=[pl.BlockSpec((1, NSA_W, NSA_QB), lambda b, i, *_: (b, 0, i)),
                  pl.BlockSpec((1, 2 * N_GROUPS, t, LANES), lambda b, i, *_: (b, 0, 0, 0)),
                  pl.BlockSpec((1, 2 * N_GROUPS, VT_ROWS, t), lambda b, i, *_: (b, 0, 0, 0)),
                  once((t, N_BLK_PAD), lambda b, i, *_: (0, 0)),
                  pl.BlockSpec((1, N_GROUPS, N_BLK_PAD, NSA_QB), lambda b, i, *_: (b, 0, 0, i)),
                  pl.BlockSpec((1, GATET_ROWS, NSA_QB), lambda b, i, *_: (b, 0, i)),
                  pl.BlockSpec((1, NSA_W, NSA_QB), lambda b, i, *_: (b, 0, i))],
        out_specs=pl.BlockSpec((1, NSA_QB, NSA_W), lambda b, i, *_: (b, i, 0)),
        scratch_shapes=[pltpu.VMEM((N_GROUPS, 2 * LANES, HPG * NSA_QB), BF16),
                        pltpu.VMEM((N_GROUPS, 1, HPG * NSA_QB), F32),
                        pltpu.VMEM((N_GROUPS, VT_ROWS, HPG * NSA_QB), F32)],
    )
    return pl.pallas_call(
        functools.partial(_slc_win_kernel, nqb=nqb),
        grid_spec=grid_spec,
        out_shape=jax.ShapeDtypeStruct((bsz, t, NSA_W), BF16),
        compiler_params=pltpu.CompilerParams(dimension_semantics=("arbitrary", "arbitrary"),
                                             vmem_limit_bytes=VMEM_LIMIT),
        name="slc_win",
    )(counts, lists, qt, kaug, vt, onehot, selt, gatest, ocmp)


def _merge_kernel(x_ref, onsa_ref, u_ref, vn_ref, g_ref, wm_ref, bm_ref, ws_ref, bs_ref,
                  wpa_ref, wpb_ref, wo_ref, *refs, tm, n_cast):
    cast_in, h_ref, cast_out = refs[:n_cast], refs[n_cast], refs[n_cast + 1:]
    _cast_blocks(cast_in, cast_out)
    x = x_ref[...]
    xn = _rms(x, g_ref[...]).astype(BF16)
    tril = _iota((GMLP_CHUNK, GMLP_CHUNK), 0) >= _iota((GMLP_CHUNK, GMLP_CHUNK), 1)
    sgu_rows = []
    for c in range(tm // GMLP_CHUNK):
        rows = slice(GMLP_CHUNK * c, GMLP_CHUNK * (c + 1))
        cols = []
        for g in range(GMLP_GROUPS):
            lanes = slice(LANES * g, LANES * (g + 1))
            w = jnp.where(tril, ws_ref[g], 0.0).astype(BF16)
            cols.append(_dot(w, vn_ref[rows, lanes]) + bs_ref[:, g:g + 1])
        sgu_rows.append(u_ref[rows, :] * jnp.concatenate(cols, axis=1))
    o_sgu = jnp.concatenate(sgu_rows, axis=0).astype(BF16)
    mg = jax.nn.sigmoid(_dot(xn, wm_ref[...]) + bm_ref[...])
    mixed = (mg[:, :D_MODEL] * _dot(onsa_ref[...], wpa_ref[...])
             + mg[:, D_MODEL:] * _dot(o_sgu, wpb_ref[...]))
    h_ref[...] = x + _dot(mixed.astype(BF16), wo_ref[...])


def _merge(x2, onsa2, u2, vn2, norm_g, wm, bm, ws, bs_t, wpa, wpb, wo, to_cast, tm):
    n = x2.shape[0]
    row = lambda w: pl.BlockSpec((tm, w), lambda i: (i, 0))
    full = lambda a: pl.BlockSpec(a.shape, lambda i: (0,) * a.ndim)
    cast_specs, cast_shapes = _cast_plan(to_cast, n // tm)
    outs = pl.pallas_call(
        functools.partial(_merge_kernel, tm=tm, n_cast=len(to_cast)),
        grid=(n // tm,),
        in_specs=[row(D_MODEL), row(NSA_W), row(GMLP_WIDTH), row(GMLP_WIDTH), full(norm_g),
                  full(wm), full(bm), full(ws), full(bs_t), full(wpa), full(wpb), full(wo)] + cast_specs,
        out_specs=[row(D_MODEL)] + cast_specs,
        out_shape=[jax.ShapeDtypeStruct((n, D_MODEL), F32)] + cast_shapes,
        compiler_params=pltpu.CompilerParams(dimension_semantics=("arbitrary",),
                                             vmem_limit_bytes=VMEM_LIMIT),
        name="merge",
    )(x2, onsa2, u2, vn2, norm_g, wm, bm, ws, bs_t, wpa, wpb, wo, *to_cast)
    return outs[0], outs[1:]


def _memkv_kernel(mem_ref, g_ref, w_ref, out_ref):
    out_ref[0] = _dot(_rms(mem_ref[0], g_ref[...]).astype(BF16), w_ref[...]).astype(BF16)


def _memkv(mem, norm_g, w):
    bsz, nm, _ = mem.shape
    return pl.pallas_call(
        _memkv_kernel,
        grid=(bsz,),
        in_specs=[pl.BlockSpec((1, nm, D_MODEL), lambda b: (b, 0, 0)),
                  pl.BlockSpec(norm_g.shape, lambda b: (0, 0)),
                  pl.BlockSpec(w.shape, lambda b: (0, 0))],
        out_specs=pl.BlockSpec((1, nm, 2 * MEM_W), lambda b: (b, 0, 0)),
        out_shape=jax.ShapeDtypeStruct((bsz, nm, 2 * MEM_W), BF16),
        compiler_params=pltpu.CompilerParams(dimension_semantics=("arbitrary",),
                                             vmem_limit_bytes=VMEM_LIMIT),
        name="memkv",
    )(mem, norm_g, w)


def _xattn_kernel(h_ref, g_ref, wq_ref, mkv_ref, wo_ref, out_ref, *, tm):
    halves = [slice(0, tm // 2), slice(tm // 2, tm)]
    head_lanes = [slice(MEM_HEAD_DIM * a, MEM_HEAD_DIM * (a + 1)) for a in range(MEM_HEADS)]
    hs = [h_ref[rows, :] for rows in halves]
    hqs = [(_dot(_rms(h, g_ref[...]).astype(BF16), wq_ref[...]) * (MEM_HEAD_DIM ** -0.5 * LOG2E)).astype(BF16)
           for h in hs]
    ss = [[_dot_nt(hq[:, lanes], mkv_ref[0, :, lanes]) for lanes in head_lanes] for hq in hqs]
    es = [[jnp.exp2(s - jnp.max(s, axis=-1, keepdims=True)) for s in s_half] for s_half in ss]
    os = []
    for e_half in es:
        heads = []
        for a, e in enumerate(e_half):
            v = mkv_ref[0, :, MEM_W + MEM_HEAD_DIM * a:MEM_W + MEM_HEAD_DIM * (a + 1)]
            heads.append(_dot(e.astype(BF16), v) * (1.0 / jnp.sum(e, axis=-1, keepdims=True)))
        os.append(jnp.concatenate(heads, axis=1).astype(BF16))
    for rows, h, o in zip(halves, hs, os):
        out_ref[rows, :] = h + _dot(o, wo_ref[...])


def _xattn(h2d, norm_g, wq, mkv, wo, tm, rows_per_batch):
    n = h2d.shape[0]
    nm = mkv.shape[1]
    tiles_per_batch = rows_per_batch // tm
    full = lambda a: pl.BlockSpec(a.shape, lambda i: (0,) * a.ndim)
    return pl.pallas_call(
        functools.partial(_xattn_kernel, tm=tm),
        grid=(n // tm,),
        in_specs=[pl.BlockSpec((tm, D_MODEL), lambda i: (i, 0)), full(norm_g), full(wq),
                  pl.BlockSpec((1, nm, 2 * MEM_W), lambda i: (i // tiles_per_batch, 0, 0)),
                  full(wo)],
        out_specs=pl.BlockSpec((tm, D_MODEL), lambda i: (i, 0)),
        out_shape=jax.ShapeDtypeStruct((n, D_MODEL), F32),
        compiler_params=pltpu.CompilerParams(dimension_semantics=("arbitrary",),
                                             vmem_limit_bytes=VMEM_LIMIT),
        name="xattn",
    )(h2d, norm_g, wq, mkv, wo)


def _ffn_kernel(h_ref, g_ref, wgu_ref, wd_ref, gf_ref, out_ref, *, d_ff):
    tm = h_ref.shape[0]
    halves = [slice(0, tm // 2), slice(tm // 2, tm)]
    hs = [h_ref[rows, :] for rows in halves]
    hns = [_rms(h, g_ref[...]).astype(BF16) for h in hs]
    gates = [_dot(hn, wgu_ref[:, :d_ff]) for hn in hns]
    ups = [_dot(hn, wgu_ref[:, d_ff:]) for hn in hns]
    acts = [(jax.nn.silu(gate) * up).astype(BF16) for gate, up in zip(gates, ups)]
    for rows, h, act in zip(halves, hs, acts):
        out_ref[rows, :] = _rms(h + _dot(act, wd_ref[...]), gf_ref[...])


def _ffn(h2d, norm_g, wgu, wd, norm_f, tm):
    n = h2d.shape[0]
    d_ff = wd.shape[0]
    full = lambda a: pl.BlockSpec(a.shape, lambda i: (0,) * a.ndim)
    once = lambda a: pl.BlockSpec(a.shape, lambda i: (0,) * a.ndim, pipeline_mode=pl.Buffered(1))
    return pl.pallas_call(
        functools.partial(_ffn_kernel, d_ff=d_ff),
        grid=(n // tm,),
        in_specs=[pl.BlockSpec((tm, D_MODEL), lambda i: (i, 0)), full(norm_g), once(wgu), once(wd),
                  full(norm_f)],
        out_specs=pl.BlockSpec((tm, D_MODEL), lambda i: (i, 0)),
        out_shape=jax.ShapeDtypeStruct((n, D_MODEL), F32),
        compiler_params=pltpu.CompilerParams(dimension_semantics=("arbitrary",),
                                             vmem_limit_bytes=VMEM_LIMIT),
        name="ffn",
    )(h2d, norm_g, wgu, wd, norm_f)


def _block_diag2(w):
    z = jnp.zeros_like(w)
    return jnp.concatenate([jnp.concatenate([w, z], axis=-1), jnp.concatenate([z, w], axis=-1)], axis=-2)


def _chunk_lists(flags):
    bsz = flags.shape[0]
    n_chunks = N_KEY_CHUNKS
    per_step = NSA_QB // Q_BLOCK
    nqb = flags.shape[1] // per_step
    f = flags[:, :, :, 0, :].reshape(bsz, nqb, per_step, N_GROUPS, n_chunks, 2).max(axis=(2, 5))
    cid = jnp.arange(n_chunks, dtype=jnp.int32)
    own = (NSA_QB // KEY_CHUNK) * jnp.arange(nqb, dtype=jnp.int32)[None, :, None, None]
    active = (f > 0) & (cid < own)
    n_active = active.sum(axis=-1)
    slot = jnp.cumsum(active, axis=-1) - 1
    hit = active[..., :, None] & (slot[..., :, None] == cid)
    ids = jnp.sum(jnp.where(hit, cid[:, None], 0), axis=-2)
    ids = jnp.where(cid < n_active[..., None], ids, VOID_CHUNK)
    words, n_batches = [], []
    for g, batch in enumerate(SLC_BATCH):
        n_batches.append((n_active[:, :, g] + batch - 1) // batch)
        padded = jnp.pad(ids[:, :, g], ((0, 0), (0, 0), (0, LIST_WORDS * batch - n_chunks)),
                         constant_values=VOID_CHUNK).reshape(bsz, nqb, LIST_WORDS, batch)
        words.append(functools.reduce(jnp.bitwise_or,
                                      [padded[..., u] << (CHUNK_ID_BITS * u) for u in range(batch)]))
    n_batches = functools.reduce(jnp.maximum, n_batches)
    words = jnp.stack(words, axis=2)
    return n_batches.reshape(-1).astype(jnp.int32), words.reshape(-1).astype(jnp.int32)


def kernel(x, mem, norm_mix, w_in, w_cmp_k1, w_cmp_k2, w_cmp_v1, w_cmp_v2, pe_cmp_k, pe_cmp_v, ln_sgu, w_spatial, b_spatial, w_proj_a, w_proj_b, w_merge, b_merge, w_out, norm_mem_q, norm_mem_kv, w_mq, w_mkv, w_mo, norm_ffn, w_gate_up, w_down, norm_final):
    bsz, t, d = x.shape
    depth = norm_mix.shape[0]
    assert d == D_MODEL and t % Q_BLOCK == 0 and t // SEL_BLOCK <= N_BLK_PAD
    assert t // SEL_BLOCK >= N_SELECT and depth == 1 and t % 1024 == 0
    n = bsz * t
    tm = 512
    h = x.reshape(n, d)
    c0, c1, c2, c3 = NSA_W, NSA_W + KV_W, NSA_W + 3 * KV_W, NSA_W + 3 * KV_W + 2 * GMLP_WIDTH
    onehot = (jnp.arange(t)[:, None] // SEL_BLOCK == jnp.arange(N_BLK_PAD)[None, :]).astype(BF16)
    pad_cols = lambda w, width: jnp.pad(w, ((0, 0), (0, width - w.shape[1])))
    for l in range(depth):
        wi = w_in[l]
        wk, wv = [], []
        for branch in range(2):
            base = c1 + KV_W * branch
            for g in range(N_GROUPS):
                wk.append(pad_cols(wi[:, base + HEAD_DIM * g:base + HEAD_DIM * (g + 1)], LANES))
                v0 = base + N_GROUPS * HEAD_DIM + HEAD_DIM * g
                wv.append(pad_cols(wi[:, v0:v0 + HEAD_DIM], VT_ROWS))
        wrow = jnp.concatenate([wi[:, c0:c1]] + wk + [wi[:, c2:c3]], axis=1).astype(BF16)
        wnt = jnp.concatenate([wi[:, :c0]] + wv + [pad_cols(wi[:, c3:], GATET_ROWS)], axis=1).T.astype(BF16)
        (qt, kvc2, kaug, vt, u2, vn2, gatest), (wm_b, wpa_b, wpb_b, wo_b, wmq_b, wmkv_b, wmo_b) = _inproj(
            h, norm_mix[l][None], ln_sgu[l][None], wrow, wnt,
            [w_merge[l], w_proj_a[l], w_proj_b[l], w_out[l], w_mq[l], w_mkv[l], w_mo[l]], 2 * tm, bsz, t)

        pe2 = jnp.stack([pe_cmp_k[l], pe_cmp_v[l]])
        pe2 = jnp.concatenate([pe2, pe2], axis=-1)
        w1 = jnp.stack([w_cmp_k1[l], w_cmp_v1[l]]).reshape(2, CMP_BLOCK, HEAD_DIM, CMP_HIDDEN)
        w1bd = _block_diag2(w1).astype(BF16)
        w2k = _block_diag2(w_cmp_k2[l]).astype(BF16)
        w2vt = _block_diag2(w_cmp_v2[l]).T.astype(BF16)
        kc, vct = _compress(kvc2.reshape(bsz, t, KV_W), pe2, w1bd, w2k, w2vt)
        ocmp, selt, flags = _cmp_topk(qt, kc, vct, gatest)
        counts, lists = _chunk_lists(flags)
        onsa = _slc_win(counts, lists, qt, kaug, vt, onehot, selt, gatest, ocmp)

        h, (wgu_b, wd_b) = _merge(h, onsa.reshape(n, NSA_W), u2, vn2, norm_mix[l][None], wm_b,
                                  b_merge[l][None], w_spatial[l], b_spatial[l].T, wpa_b, wpb_b, wo_b,
                                  [w_gate_up[l], w_down[l]], 2 * tm)

        mkv = _memkv(mem, norm_mem_kv[l][None], wmkv_b)
        h = _xattn(h, norm_mem_q[l][None], wmq_b, mkv, wmo_b, 2 * tm, t)
        h = _ffn(h, norm_ffn[l][None], wgu_b, wd_b, norm_final[None], tm)
    return h.reshape(bsz, t, d)
```

```python
import functools

import jax
import jax.numpy as jnp
from jax import lax
from jax.experimental import pallas as pl
from jax.experimental.pallas import tpu as pltpu

F32 = jnp.float32
BF16 = jnp.bfloat16

LANES = 128
SUBLANES = 8
D_MODEL = 1024
N_HEADS = 8
HEAD_DIM = 64
N_GROUPS = 2
HPG = N_HEADS // N_GROUPS
CMP_BLOCK = 32
CMP_STRIDE = 16
CMP_HIDDEN = 128
SEL_BLOCK = 64
N_SELECT = 16
WINDOW = 512
Q_BLOCK = 256
KEY_CHUNK = 128
N_BLK_PAD = 128
NSA_W = N_HEADS * HEAD_DIM
KV_W = 2 * N_GROUPS * HEAD_DIM
GMLP_WIDTH = 512
GMLP_GROUPS = 4
GMLP_CHUNK = 128
MEM_HEADS = 4
MEM_HEAD_DIM = 128
MEM_W = MEM_HEADS * MEM_HEAD_DIM
NSA_QB = 256
SLC_BATCH = (4, 5)
PEELED_WORDS = 2
N_KEY_CHUNKS = N_BLK_PAD // 2
CHUNK_ID_BITS = 6
VOID_CHUNK = N_KEY_CHUNKS - 1
LIST_WORDS = -(-N_KEY_CHUNKS // min(SLC_BATCH))
VT_ROWS = 80
VCT_ROWS = VT_ROWS + N_BLK_PAD
CMP_KEY_CHUNK = 128
N_FORCED = 3
TOPK_ROW_STEP = 64
CMP_BLOCKS = 2
GATET_ROWS = 32
MASK_BIG = 1e30
EPS = 1e-6
NEG = -1e30
REMOVED = -3e38
SLOPES = tuple(2.0 ** (-8.0 * (h + 1) / N_HEADS) for h in range(N_HEADS))
LOG2E = 1.4426950408889634
Q_SCALE = HEAD_DIM ** -0.5 * LOG2E
VMEM_LIMIT = 56 * 1024 * 1024


def _dot(a, b):
    return jnp.dot(a, b, preferred_element_type=F32)


def _dot_nt(a, b):
    return lax.dot_general(a, b, (((1,), (1,)), ((), ())), preferred_element_type=F32)


def _rms(x, g):
    return x * lax.rsqrt(jnp.mean(x * x, axis=-1, keepdims=True) + EPS) * g


def _iota(shape, dim):
    return lax.broadcasted_iota(jnp.int32, shape, dim)


def _slope_row(g, nq):
    return jnp.concatenate(
        [jnp.full((1, nq), SLOPES[HPG * g + j] * LOG2E, F32) for j in range(HPG)], axis=1)


def _slope_feature_rows(slope_row, n_rows):
    hi = slope_row.astype(BF16).astype(F32)
    r = _iota((n_rows, slope_row.shape[1]), 0)
    return jnp.where(r == 0, hi, jnp.where(r == 1, slope_row - hi, 0.0)).astype(BF16)


def _chunk_slabs(st, chunk):
    if isinstance(st, list):
        return st
    return [st[chunk * u:chunk * (u + 1)] for u in range(st.shape[0] // chunk)]


def _col_max(st, dls, chunk):
    parts = [s.reshape(chunk // SUBLANES, SUBLANES, s.shape[1]).max(axis=0) - dl
             for s, dl in zip(_chunk_slabs(st, chunk), dls)]
    return functools.reduce(jnp.maximum, parts).max(axis=0, keepdims=True)


def _probs(st, dls, m, chunk, exp_dtype=F32):
    return jnp.concatenate([jnp.exp2((s - (m + dl)).astype(exp_dtype)).astype(BF16)
                            for s, dl in zip(_chunk_slabs(st, chunk), dls)], axis=0)


def _cast_plan(weights, n_steps):
    for w in weights:
        assert w.shape[0] % (16 * n_steps) == 0, (w.shape, n_steps)
    specs = [pl.BlockSpec((w.shape[0] // n_steps, w.shape[1]), lambda i: (i, 0)) for w in weights]
    shapes = [jax.ShapeDtypeStruct(w.shape, BF16) for w in weights]
    return specs, shapes


def _cast_blocks(src_refs, dst_refs):
    for src, dst in zip(src_refs, dst_refs):
        dst[...] = src[...].astype(BF16)


def _inproj_kernel(x_ref, g_ref, lng_ref, wrow_ref, wnt_ref, *refs, tm, n_cast):
    cast_in, cast_out = refs[:n_cast], refs[len(refs) - n_cast:]
    qt_ref, kvc_ref, kaug_ref, vt_ref, u_ref, vn_ref, gatest_ref = refs[n_cast:len(refs) - n_cast]
    _cast_blocks(cast_in, cast_out)
    xn = _rms(x_ref[...], g_ref[...]).astype(BF16)
    r = _dot(xn, wrow_ref[...])
    nt = _dot_nt(wnt_ref[...], xn)
    qt_ref[0] = (nt[0:NSA_W] * Q_SCALE).astype(BF16)
    kvc_ref[...] = r[:, 0:KV_W]
    lane = _iota((tm, LANES), 1)
    key_feat = jnp.where((lane == HEAD_DIM) | (lane == HEAD_DIM + 1),
                         _iota((tm, LANES), 0) & (KEY_CHUNK - 1), 0).astype(F32)
    ones_row = jnp.where(_iota((VT_ROWS, tm), 0) == HEAD_DIM, 1.0, 0.0)
    for a in range(2 * N_GROUPS):
        kaug_ref[0, a] = (r[:, KV_W + LANES * a:KV_W + LANES * (a + 1)] + key_feat).astype(BF16)
        vt_ref[0, a] = (nt[NSA_W + VT_ROWS * a:NSA_W + VT_ROWS * (a + 1)] + ones_row).astype(BF16)
    uv = jax.nn.gelu(r[:, KV_W + 2 * N_GROUPS * LANES:])
    u_ref[...] = uv[:, :GMLP_WIDTH]
    v = uv[:, GMLP_WIDTH:]
    vc = v - jnp.mean(v, axis=-1, keepdims=True)
    vn = vc * lax.rsqrt(jnp.mean(vc * vc, axis=-1, keepdims=True) + EPS) * lng_ref[...]
    vn_ref[...] = vn.astype(BF16)
    gatest_ref[0] = jax.nn.sigmoid(nt[NSA_W + 2 * N_GROUPS * VT_ROWS:])


def _inproj(x2, norm_g, ln_g, wrow, wnt, to_cast, tm, bsz, t):
    n = x2.shape[0]
    tpb = t // tm
    row = lambda w: pl.BlockSpec((tm, w), lambda i: (i, 0))
    full = lambda a: pl.BlockSpec(a.shape, lambda i: (0,) * a.ndim)
    cast_specs, cast_shapes = _cast_plan(to_cast, n // tm)
    outs = pl.pallas_call(
        functools.partial(_inproj_kernel, tm=tm, n_cast=len(to_cast)),
        grid=(n // tm,),
        in_specs=[row(D_MODEL), full(norm_g), full(ln_g), full(wrow), full(wnt)] + cast_specs,
        out_specs=[pl.BlockSpec((1, NSA_W, tm), lambda i: (i // tpb, 0, i % tpb)),
                   row(KV_W),
                   pl.BlockSpec((1, 2 * N_GROUPS, tm, LANES), lambda i: (i // tpb, 0, i % tpb, 0)),
                   pl.BlockSpec((1, 2 * N_GROUPS, VT_ROWS, tm), lambda i: (i // tpb, 0, 0, i % tpb)),
                   row(GMLP_WIDTH), row(GMLP_WIDTH),
                   pl.BlockSpec((1, GATET_ROWS, tm), lambda i: (i // tpb, 0, i % tpb))] + cast_specs,
        out_shape=[jax.ShapeDtypeStruct((bsz, NSA_W, t), BF16),
                   jax.ShapeDtypeStruct((n, KV_W), F32),
                   jax.ShapeDtypeStruct((bsz, 2 * N_GROUPS, t, LANES), BF16),
                   jax.ShapeDtypeStruct((bsz, 2 * N_GROUPS, VT_ROWS, t), BF16),
                   jax.ShapeDtypeStruct((n, GMLP_WIDTH), F32),
                   jax.ShapeDtypeStruct((n, GMLP_WIDTH), BF16),
                   jax.ShapeDtypeStruct((bsz, GATET_ROWS, t), F32)] + cast_shapes,
        compiler_params=pltpu.CompilerParams(dimension_semantics=("arbitrary",),
                                             vmem_limit_bytes=VMEM_LIMIT),
        name="inproj",
    )(x2, norm_g, ln_g, wrow, wnt, *to_cast)
    return outs[:7], outs[7:]


def _compress_kernel(xk_ref, xv_ref, pe_ref, w1_ref, w2k_ref, w2vt_ref, kc_ref, vct_ref, *, nc):
    outs = []
    for j, x_ref in enumerate((xk_ref, xv_ref)):
        a = jnp.zeros((nc, 2 * CMP_HIDDEN), F32)
        b = jnp.zeros((nc, 2 * CMP_HIDDEN), F32)
        for t in range(0, CMP_STRIDE, 2):
            xs = [x_ref[0, pl.ds(t + e, nc, stride=CMP_STRIDE), :] for e in range(2)]

            def half_block(first):
                lhs = jnp.concatenate([(xs[e] + pe_ref[j, first + t + e:first + t + e + 1, :]).astype(BF16)
                                       for e in range(2)], axis=1)
                rhs = jnp.concatenate([w1_ref[j, first + t], w1_ref[j, first + t + 1]], axis=0)
                return _dot(lhs, rhs)

            a = a + half_block(0)
            b = b + half_block(CMP_STRIDE)
        outs.append(jax.nn.gelu(a + pltpu.roll(b, nc - 1, 0)).astype(BF16))
    k2 = _dot(outs[0], w2k_ref[...])
    v_t = _dot_nt(w2vt_ref[...], outs[1])
    lane = _iota((nc, LANES), 1)
    key_feat = jnp.where((lane == HEAD_DIM) | (lane == HEAD_DIM + 1),
                         CMP_STRIDE * (_iota((nc, LANES), 0) & (CMP_KEY_CHUNK - 1)), 0).astype(F32)
    ci =_iota((N_BLK_PAD, nc), 1) * CMP_STRIDE
    sj = _iota((N_BLK_PAD, nc), 0) * SEL_BLOCK
    overlap_t = jnp.where((ci < sj + SEL_BLOCK) & (ci + (CMP_BLOCK - 1) >= sj), 1.0, 0.0).astype(BF16)
    ones_rows = jnp.where(_iota((VT_ROWS - HEAD_DIM, nc), 0) == 0, 1.0, 0.0).astype(BF16)
    for g in range(N_GROUPS):
        kg = k2 if g == 0 else pltpu.roll(k2, HEAD_DIM, 1)
        kc_ref[0, g] = jnp.where(lane < HEAD_DIM, kg, key_feat).astype(BF16)
        vct_ref[0, g, 0:HEAD_DIM, :] = v_t[HEAD_DIM * g:HEAD_DIM * (g + 1), :].astype(BF16)
        vct_ref[0, g, HEAD_DIM:VT_ROWS, :] = ones_rows
        vct_ref[0, g, VT_ROWS:VCT_ROWS, :] = overlap_t


def _compress(kvc3, pe2, w1bd, w2k, w2vt):
    bsz, t, _ = kvc3.shape
    nc = t // CMP_STRIDE
    full = lambda a: pl.BlockSpec(a.shape, lambda b: (0,) * a.ndim)
    return pl.pallas_call(
        functools.partial(_compress_kernel, nc=nc),
        grid=(bsz,),
        in_specs=[pl.BlockSpec((1, t, LANES), lambda b: (b, 0, 0)),
                  pl.BlockSpec((1, t, LANES), lambda b: (b, 0, 1)),
                  full(pe2), full(w1bd), full(w2k), full(w2vt)],
        out_specs=[pl.BlockSpec((1, N_GROUPS, nc, LANES), lambda b: (b, 0, 0, 0)),
                   pl.BlockSpec((1, N_GROUPS, VCT_ROWS, nc), lambda b: (b, 0, 0, 0))],
        out_shape=[jax.ShapeDtypeStruct((bsz, N_GROUPS, nc, LANES), BF16),
                   jax.ShapeDtypeStruct((bsz, N_GROUPS, VCT_ROWS, nc), BF16)],
        compiler_params=pltpu.CompilerParams(dimension_semantics=("arbitrary",),
                                             vmem_limit_bytes=VMEM_LIMIT),
        name="compress",
    )(kvc3, kvc3, pe2, w1bd, w2k, w2vt)


def _cmp_topk_kernel(qt_ref, kc_ref, vct_ref, gatest_ref, ocmp_ref, selt_ref, flags_ref,
                     m_ref, acc_ref, *, ncp):
    step = pl.program_id(1)
    n_chunks = ncp // CMP_KEY_CHUNK
    chunk_tokens = CMP_KEY_CHUNK * CMP_STRIDE
    tile_heads = lambda a: jnp.concatenate([a] * HPG, axis=1)
    tail_chunks = min(2, n_chunks)
    head_chunks = n_chunks - tail_chunks
    tail_keys = tail_chunks * CMP_KEY_CHUNK
    blocks = range(CMP_BLOCKS)
    groups = range(N_GROUPS)
    items = [(bi, g) for bi in blocks for g in groups]
    start = [(step * CMP_BLOCKS + bi) * Q_BLOCK for bi in blocks]
    qlanes = [slice(Q_BLOCK * bi, Q_BLOCK * (bi + 1)) for bi in blocks]
    nck = [(start[bi] + Q_BLOCK - CMP_BLOCK) // CMP_STRIDE // CMP_KEY_CHUNK + 1 for bi in blocks]
    tail_c0 = [jnp.maximum(nck[bi] - tail_chunks, 0) for bi in blocks]
    tail_rows = [pl.ds(pl.multiple_of(tail_c0[bi] * CMP_KEY_CHUNK, CMP_KEY_CHUNK), tail_keys)
                 for bi in blocks]
    gt = gatest_ref[0]
    slope_rows = [_slope_row(g, Q_BLOCK) for g in groups]
    qas = {(bi, g): jnp.concatenate(
        [jnp.concatenate([qt_ref[0, HEAD_DIM * (HPG * g + j):HEAD_DIM * (HPG * g + j + 1), qlanes[bi]]
                          for j in range(HPG)], axis=1),
         _slope_feature_rows(slope_rows[g], LANES - HEAD_DIM)], axis=0) for bi, g in items}

    def delta(bi, g, c):
        return slope_rows[g] * (start[bi] - c * chunk_tokens).astype(F32)

    for bi in blocks:
        if head_chunks > 0:
            @pl.when(nck[bi] > tail_chunks)
            def _():
                sts = [_dot(kc_ref[0, g, 0:head_chunks * CMP_KEY_CHUNK, :], qas[bi, g]) for g in groups]
                dls = [[delta(bi, g, c) + jnp.where(c < nck[bi] - tail_chunks, 0.0, MASK_BIG)
                        for c in range(head_chunks)] for g in groups]
                ms = [_col_max(sts[g], dls[g], CMP_KEY_CHUNK) for g in groups]
                ps = [_probs(sts[g], dls[g], ms[g], CMP_KEY_CHUNK) for g in groups]
                for g in groups:
                    m_ref[bi, g] = ms[g]
                    acc_ref[bi, g] = _dot(vct_ref[0, g, :, 0:head_chunks * CMP_KEY_CHUNK], ps[g])

            @pl.when(nck[bi] <= tail_chunks)
            def _():
                m_ref[bi] = jnp.full(m_ref.shape[1:], NEG, F32)
                acc_ref[bi] = jnp.zeros(acc_ref.shape[1:], F32)

    def tail_and_select(n_rows):
        if head_chunks > 0:
            m_old = {it: m_ref[it[0], it[1]] for it in items}
            acc_old = {it: acc_ref[it[0], it[1]] for it in items}
        else:
            m_old = {it: jnp.full((1, HPG * Q_BLOCK), NEG, F32) for it in items}
            acc_old = {it: jnp.zeros((VCT_ROWS, HPG * Q_BLOCK), F32) for it in items}
        key_row = _iota((tail_keys, Q_BLOCK), 0)
        q_lane = _iota((tail_keys, Q_BLOCK), 1)
        tail_bias = []
        for bi in blocks:
            key_end = CMP_STRIDE * (tail_c0[bi] * CMP_KEY_CHUNK + key_row) + (CMP_BLOCK - 1) - start[bi]
            tail_bias.append(tile_heads(jnp.where(key_end <= q_lane, 0.0, NEG)))
        sts = {(bi, g): _dot(kc_ref[0, g, tail_rows[bi], :], qas[bi, g]) + tail_bias[bi] for bi, g in items}
        dls = {(bi, g): [delta(bi, g, tail_c0[bi] + u) for u in range(tail_chunks)] for bi, g in items}
        m_new = {it: jnp.maximum(m_old[it], _col_max(sts[it], dls[it], CMP_KEY_CHUNK)) for it in items}
        ps = {it: _probs(sts[it], dls[it], m_new[it], CMP_KEY_CHUNK) for it in items}
        accs = {(bi, g): jnp.exp2(m_old[bi, g] - m_new[bi, g]) * acc_old[bi, g]
                + _dot(vct_ref[0, g, :, tail_rows[bi]], ps[bi, g]) for bi, g in items}

        blk_n = _iota((n_rows, Q_BLOCK), 0)
        bf = blk_n.astype(F32)
        rk, cur = {}, []
        for bi in blocks:
            t_row = start[bi] + _iota((1, Q_BLOCK), 1)
            cur.append(lax.shift_right_logical(t_row, SEL_BLOCK.bit_length() - 1))
            forced = (blk_n == 0) | (blk_n == cur[bi]) | (blk_n == cur[bi] - 1)
            has_key = t_row >= CMP_BLOCK - 1
            per_head = []
            for g in groups:
                acc = accs[bi, g]
                inv_l = 1.0 / jnp.maximum(acc[HEAD_DIM:HEAD_DIM + 1], 1e-30)
                o = acc[0:HEAD_DIM] * inv_l
                imp_h = acc[VT_ROWS:VT_ROWS + n_rows] * inv_l
                imp = sum(imp_h[:, Q_BLOCK * j:Q_BLOCK * (j + 1)] for j in range(HPG))
                imp = jnp.where(has_key, imp, 0.0)
                for j in range(HPG):
                    h = HPG * g + j
                    per_head.append(jnp.where(
                        has_key, gt[3 * h:3 * h + 1, qlanes[bi]] * o[:, Q_BLOCK * j:Q_BLOCK * (j + 1)], 0.0))
                rk[bi, g] = jnp.where(blk_n <= cur[bi], jnp.where(forced, REMOVED, imp), NEG)
            ocmp_ref[0, :, qlanes[bi]] = jnp.concatenate(per_head, axis=0)

        for _ in range(N_SELECT - N_FORCED):
            for it in items:
                m = jnp.max(rk[it], axis=0, keepdims=True)
                idx = jnp.min(jnp.where(rk[it] == m, bf, float(N_BLK_PAD)), axis=0, keepdims=True)
                rk[it] = jnp.where(bf == idx, REMOVED, rk[it])
        ones8 = jnp.ones((SUBLANES, Q_BLOCK), F32)
        for bi, g in items:
            sel = jnp.where((blk_n <= cur[bi]) & (rk[bi, g] < 2.0 * NEG), 1.0, 0.0)
            selt_ref[0, g, 0:n_rows, qlanes[bi]] = sel.astype(BF16)
            flag = (_dot_nt(ones8, sel) > 0.5).astype(jnp.int32)
            if n_rows < N_BLK_PAD:
                selt_ref[0, g, n_rows:N_BLK_PAD, qlanes[bi]] = jnp.zeros((N_BLK_PAD - n_rows, Q_BLOCK), BF16)
                flag = jnp.concatenate([flag, jnp.zeros((SUBLANES, N_BLK_PAD - n_rows), jnp.int32)], axis=1)
            flags_ref[0, bi, g] = flag

    causal_rows = (start[-1] + Q_BLOCK) // SEL_BLOCK
    for n_rows in range(TOPK_ROW_STEP, N_BLK_PAD + 1, TOPK_ROW_STEP):
        @pl.when((causal_rows > n_rows - TOPK_ROW_STEP) & (causal_rows <= n_rows))
        def _():
            tail_and_select(n_rows)


def _cmp_topk(qt, kc, vct, gatest):
    bsz, _, t = qt.shape
    ncp = t // CMP_STRIDE
    nqb = t // Q_BLOCK
    step_q = CMP_BLOCKS * Q_BLOCK
    return pl.pallas_call(
        functools.partial(_cmp_topk_kernel, ncp=ncp),
        grid=(bsz, nqb // CMP_BLOCKS),
        in_specs=[pl.BlockSpec((1, NSA_W, step_q), lambda b, i: (b, 0, i)),
                  pl.BlockSpec((1, N_GROUPS, ncp, LANES), lambda b, i: (b, 0, 0, 0)),
                  pl.BlockSpec((1, N_GROUPS, VCT_ROWS, ncp), lambda b, i: (b, 0, 0, 0)),
                  pl.BlockSpec((1, GATET_ROWS, step_q), lambda b, i: (b, 0, i))],
        out_specs=[pl.BlockSpec((1, NSA_W, step_q), lambda b, i: (b, 0, i)),
                   pl.BlockSpec((1, N_GROUPS, N_BLK_PAD, step_q), lambda b, i: (b, 0, 0, i)),
                   pl.BlockSpec((1, CMP_BLOCKS, N_GROUPS, SUBLANES, N_BLK_PAD), lambda b, i: (b, i, 0, 0, 0))],
        out_shape=[jax.ShapeDtypeStruct((bsz, NSA_W, t), F32),
                   jax.ShapeDtypeStruct((bsz, N_GROUPS, N_BLK_PAD, t), BF16),
                   jax.ShapeDtypeStruct((bsz, nqb, N_GROUPS, SUBLANES, N_BLK_PAD), jnp.int32)],
        scratch_shapes=[pltpu.VMEM((CMP_BLOCKS, N_GROUPS, 1, HPG * Q_BLOCK), F32),
                        pltpu.VMEM((CMP_BLOCKS, N_GROUPS, VCT_ROWS, HPG * Q_BLOCK), F32)],
        compiler_params=pltpu.CompilerParams(dimension_semantics=("arbitrary", "arbitrary"),
                                             vmem_limit_bytes=VMEM_LIMIT),
        name="cmp_topk",
    )(qt, kc, vct, gatest)


def _slc_win_kernel(counts_ref, lists_ref, qt_ref, kaug_ref, vt_ref, oh_ref, selt_ref, gatest_ref,
                    ocmp_ref, out_ref, qaug_ref, m_ref, acc_ref, *, nqb):
    b = pl.program_id(0)
    qb = pl.program_id(1)
    step_id = b * nqb + qb
    start = qb * NSA_QB
    tile_heads = lambda a: jnp.concatenate([a] * HPG, axis=1)
    groups = range(N_GROUPS)
    gt = gatest_ref[0]
    slope_rows = [_slope_row(g, NSA_QB) for g in groups]

    def normalize(acc):
        return acc[0:HEAD_DIM] / jnp.maximum(acc[HEAD_DIM:HEAD_DIM + 1], 1e-30)

    def half_lanes(w):
        return [slice(NSA_QB * h + KEY_CHUNK * w, NSA_QB * h + KEY_CHUNK * (w + 1)) for h in range(HPG)]

    def half(a, w):
        return jnp.concatenate([a[:, s] for s in half_lanes(w)], axis=1)

    def unhalf(lo, hi):
        return jnp.concatenate([x[:, KEY_CHUNK * h:KEY_CHUNK * (h + 1)]
                                for h in range(HPG) for x in (lo, hi)], axis=1)

    def chunk_at(rel):
        pos = start + rel * KEY_CHUNK
        rows = pl.ds(pl.multiple_of(jnp.maximum(pos, 0), KEY_CHUNK), KEY_CHUNK)
        return rows, (jnp.where(pos >= 0, 0.0, MASK_BIG) if rel < 0 else 0.0)

    ki = _iota((KEY_CHUNK, KEY_CHUNK), 0)
    qi = _iota((KEY_CHUNK, KEY_CHUNK), 1)
    upper_bias = tile_heads(jnp.where(ki > qi, 0.0, NEG))
    lower_bias = tile_heads(jnp.where(ki <= qi, 0.0, NEG))
    n_mid = WINDOW // KEY_CHUNK
    mid_dist = (_iota((n_mid * KEY_CHUNK, NSA_QB), 1) + (n_mid - 1) * KEY_CHUNK
                - _iota((n_mid * KEY_CHUNK, NSA_QB), 0))
    mid_bias = tile_heads(jnp.where((mid_dist >= 0) & (mid_dist < WINDOW), 0.0, NEG))
    own_bias = [tile_heads(jnp.where(_iota((KEY_CHUNK, NSA_QB), 0) + KEY_CHUNK * w
                                     <= _iota((KEY_CHUNK, NSA_QB), 1), 0.0, NEG)) for w in range(2)]

    for g in range(N_GROUPS):
        qaug_ref[g, 0:HEAD_DIM, :] = jnp.concatenate(
            [qt_ref[0, HEAD_DIM * (HPG * g + j):HEAD_DIM * (HPG * g + j + 1), :] for j in range(HPG)],
            axis=1)
        qaug_ref[g, HEAD_DIM:LANES, :] = _slope_feature_rows(slope_rows[g], LANES - HEAD_DIM)
        sel_bias = ((selt_ref[0, g].astype(F32) - 1.0) * MASK_BIG).astype(BF16)
        qaug_ref[g, LANES:2 * LANES, :] = tile_heads(sel_bias)

    mid = [chunk_at(r) for r in range(1 - n_mid, 1)]
    lo_rows, lo_kill = chunk_at(-n_mid)
    hi_rows, _ = chunk_at(1)
    own_rows = pl.ds(pl.multiple_of(start, NSA_QB), NSA_QB)
    kw = lambda g: kaug_ref.at[0, N_GROUPS + g]
    vw = lambda g: vt_ref.at[0, N_GROUPS + g]
    slc_keys = lambda g, rows: jnp.concatenate([kaug_ref[0, g, rows, :], oh_ref[rows, :]], axis=1)
    q_half = lambda g, w, nrow: jnp.concatenate([qaug_ref[g, 0:nrow, s] for s in half_lanes(w)], axis=1)

    def slc_scores_group(g, i, n_words=1, with_own=False):
        words = [lists_ref[(step_id * N_GROUPS + g) * LIST_WORDS + i + w] for w in range(n_words)]
        ks, vs, dls = [], [], []
        if with_own:
            ks.append(slc_keys(g, own_rows))
            vs.append(vt_ref[0, g, :, own_rows])
            dls.extend([0.0, slope_rows[g] * float(-KEY_CHUNK)])
        for word, u in [(word, u) for word in words for u in range(SLC_BATCH[g])]:
            cid = lax.shift_right_logical(word, CHUNK_ID_BITS * u) & VOID_CHUNK
            valid = cid < VOID_CHUNK
            c = jnp.where(valid, cid, 0)
            rows = pl.ds(pl.multiple_of(c * KEY_CHUNK, KEY_CHUNK), KEY_CHUNK)
            ks.append(jnp.concatenate([kaug_ref[0, g, rows, :], oh_ref[rows, :]], axis=1))
            vs.append(vt_ref[0, g, :, rows])
            dls.append(slope_rows[g] * (start - c * KEY_CHUNK).astype(F32)
                       + jnp.where(valid, 0.0, MASK_BIG))
        return _dot(jnp.concatenate(ks, axis=0), qaug_ref[g]), jnp.concatenate(vs, axis=1), dls

    def slc_scores(i):
        return zip(*[slc_scores_group(g, i) for g in range(N_GROUPS)])

    probs = functools.partial(_probs, chunk=KEY_CHUNK, exp_dtype=BF16)
    wmid_st = [_dot(jnp.concatenate([kw(g)[rows, :] for rows, _ in mid], axis=0), qaug_ref[g, 0:LANES, :])
               + mid_bias for g in groups]
    wlo_st = [_dot(kw(g)[lo_rows, :], q_half(g, 0, LANES)) + upper_bias for g in groups]
    whi_st = [_dot(kw(g)[hi_rows, :], q_half(g, 1, LANES)) + lower_bias for g in groups]

    def peeled_scores(g):
        st, vs, dls = slc_scores_group(g, 0, PEELED_WORDS, with_own=True)
        slabs = ([st[0:KEY_CHUNK] + own_bias[0], st[KEY_CHUNK:NSA_QB] + own_bias[1]]
                 + _chunk_slabs(st[NSA_QB:], KEY_CHUNK))
        return slabs, vs, dls

    def peeled_values(g, slabs, vs, dls):
        m = _col_max(slabs, dls, KEY_CHUNK)
        m_ref[g] = m
        acc_ref[g] = _dot(vs, probs(slabs, dls, m))

    peeled = [peeled_scores(0)]

    wmid_dls = [[slope_rows[g] * float(-KEY_CHUNK * r) + kill
                 for r, (_, kill) in zip(range(1 - n_mid, 1), mid)] for g in groups]
    wlo_dl = [half(slope_rows[g], 0) * float(KEY_CHUNK * n_mid) + lo_kill for g in groups]
    hi_dl = [half(slope_rows[g], 1) * float(-KEY_CHUNK) for g in groups]
    win_m = [jnp.maximum(_col_max(wmid_st[g], wmid_dls[g], KEY_CHUNK),
                         unhalf(_col_max(wlo_st[g], [wlo_dl[g]], KEY_CHUNK),
                                _col_max(whi_st[g], [hi_dl[g]], KEY_CHUNK))) for g in groups]
    wmid_p = [probs(wmid_st[g], wmid_dls[g], win_m[g]) for g in groups]
    wlo_p = [probs(wlo_st[g], [wlo_dl[g]], half(win_m[g], 0)) for g in groups]
    whi_p = [probs(whi_st[g], [hi_dl[g]], half(win_m[g], 1)) for g in groups]

    peeled.append(peeled_scores(1))

    o_win = []
    for g in groups:
        acc = _dot(jnp.concatenate([vw(g)[:, rows] for rows, _ in mid], axis=1), wmid_p[g])
        acc = acc + unhalf(_dot(vw(g)[:, lo_rows], wlo_p[g]), _dot(vw(g)[:, hi_rows], whi_p[g]))
        o_win.append(normalize(acc))
    for g in groups:
        peeled_values(g, *peeled[g])

    def slc_body(i, carry):
        sts, vss, dlss = slc_scores(i)
        m_old = [m_ref[g] for g in range(N_GROUPS)]
        m_new = [jnp.maximum(m_old[g], _col_max(sts[g], dlss[g], KEY_CHUNK)) for g in range(N_GROUPS)]
        ps = [probs(sts[g], dlss[g], m_new[g]) for g in range(N_GROUPS)]
        for g in range(N_GROUPS):
            acc_ref[g] = jnp.exp2(m_old[g] - m_new[g]) * acc_ref[g] + _dot(vss[g], ps[g])
            m_ref[g] = m_new[g]
        return carry

    lax.fori_loop(PEELED_WORDS, counts_ref[step_id], slc_body, 0)

    per_head = []
    for g in range(N_GROUPS):
        o_slc = normalize(acc_ref[g])
        for j in range(HPG):
            h = HPG * g + j
            lanes = slice(NSA_QB * j, NSA_QB * (j + 1))
            per_head.append(gt[3 * h + 1:3 * h + 2, :] * o_slc[:, lanes]
                            + gt[3 * h + 2:3 * h + 3, :] * o_win[g][:, lanes])
    o_t = jnp.concatenate(per_head, axis=0)
    out_ref[0] = (ocmp_ref[0] + o_t).T.astype(BF16)


def _slc_win(counts, lists, qt, kaug, vt, onehot, selt, gatest, ocmp):
    bsz, _, t = qt.shape
    nqb = t // NSA_QB
    once = lambda shape, imap: pl.BlockSpec(shape, imap, pipeline_mode=pl.Buffered(1))
    grid_spec = pltpu.PrefetchScalarGridSpec(
        num_scalar_prefetch=2,
        grid=(bsz, nqb),
        in_specs=[pl.BlockSpec((1, NSA_W, NSA_QB), lambda b, i, *_: (b, 0, i)),
                  pl.BlockSpec((1, 2 * N_GROUPS, t, LANES), lambda b, i, *_: (b, 0, 0, 0)),
                  pl.BlockSpec((1, 2 * N_GROUPS, VT_ROWS, t), lambda b, i, *_: (b, 0, 0, 0)),
                  once((t, N_BLK_PAD), lambda b, i, *_: (0, 0)),
                  pl.BlockSpec((1, N_GROUPS, N_BLK_PAD, NSA_QB), lambda b, i, *_: (b, 0, 0, i)),
                  pl.BlockSpec((1, GATET_ROWS, NSA_QB), lambda b, i, *_: (b, 0, i)),
                  pl.BlockSpec((1, NSA_W, NSA_QB), lambda b, i, *_: (b, 0, i))],
        out_specs=pl.BlockSpec((1, NSA_QB, NSA_W), lambda b, i, *_: (b, i, 0)),
        scratch_shapes=[pltpu.VMEM((N_GROUPS, 2 * LANES, HPG * NSA_QB), BF16),
                        pltpu.VMEM((N_GROUPS, 1, HPG * NSA_QB), F32),
                        pltpu.VMEM((N_GROUPS, VT_ROWS, HPG * NSA_QB), F32)],
    )
    return pl.pallas_call(
        functools.partial(_slc_win_kernel, nqb=nqb),
        grid_spec=grid_spec,
        out_shape=jax.ShapeDtypeStruct((bsz, t, NSA_W), BF16),
        compiler_params=pltpu.CompilerParams(dimension_semantics=("arbitrary", "arbitrary"),
                                             vmem_limit_bytes=VMEM_LIMIT),
        name="slc_win",
    )(counts, lists, qt, kaug, vt, onehot, selt, gatest, ocmp)


def _merge_kernel(x_ref, onsa_ref, u_ref, vn_ref, g_ref, wm_ref, bm_ref, ws_ref, bs_ref,
                  wpa_ref, wpb_ref, wo_ref, *refs, tm, n_cast):
    cast_in, h_ref, cast_out = refs[:n_cast], refs[n_cast], refs[n_cast + 1:]
    _cast_blocks(cast_in, cast_out)
    x = x_ref[...]
    xn = _rms(x, g_ref[...]).astype(BF16)
    tril = _iota((GMLP_CHUNK, GMLP_CHUNK), 0) >= _iota((GMLP_CHUNK, GMLP_CHUNK), 1)
    sgu_rows = []
    for c in range(tm // GMLP_CHUNK):
        rows = slice(GMLP_CHUNK * c, GMLP_CHUNK * (c + 1))
        cols = []
        for g in range(GMLP_GROUPS):
            lanes = slice(LANES * g, LANES * (g + 1))
            w = jnp.where(tril, ws_ref[g], 0.0).astype(BF16)
            cols.append(_dot(w, vn_ref[rows, lanes]) + bs_ref[:, g:g + 1])
        sgu_rows.append(u_ref[rows, :] * jnp.concatenate(cols, axis=1))
    o_sgu = jnp.concatenate(sgu_rows, axis=0).astype(BF16)
    mg = jax.nn.sigmoid(_dot(xn, wm_ref[...]) + bm_ref[...])
    mixed = (mg[:, :D_MODEL] * _dot(onsa_ref[...], wpa_ref[...])
             + mg[:, D_MODEL:] * _dot(o_sgu, wpb_ref[...]))
    h_ref[...] = x + _dot(mixed.astype(BF16), wo_ref[...])


def _merge(x2, onsa2, u2, vn2, norm_g, wm, bm, ws, bs_t, wpa, wpb, wo, to_cast, tm):
    n = x2.shape[0]
    row = lambda w: pl.BlockSpec((tm, w), lambda i: (i, 0))
    full = lambda a: pl.BlockSpec(a.shape, lambda i: (0,) * a.ndim)
    cast_specs, cast_shapes = _cast_plan(to_cast, n // tm)
    outs = pl.pallas_call(
        functools.partial(_merge_kernel, tm=tm, n_cast=len(to_cast)),
        grid=(n // tm,),
        in_specs=[row(D_MODEL), row(NSA_W), row(GMLP_WIDTH), row(GMLP_WIDTH), full(norm_g),
                  full(wm), full(bm), full(ws), full(bs_t), full(wpa), full(wpb), full(wo)] + cast_specs,
        out_specs=[row(D_MODEL)] + cast_specs,
        out_shape=[jax.ShapeDtypeStruct((n, D_MODEL), F32)] + cast_shapes,
        compiler_params=pltpu.CompilerParams(dimension_semantics=("arbitrary",),
                                             vmem_limit_bytes=VMEM_LIMIT),
        name="merge",
    )(x2, onsa2, u2, vn2, norm_g, wm, bm, ws, bs_t, wpa, wpb, wo, *to_cast)
    return outs[0], outs[1:]


def _memkv_kernel(mem_ref, g_ref, w_ref, out_ref):
    out_ref[0] = _dot(_rms(mem_ref[0], g_ref[...]).astype(BF16), w_ref[...]).astype(BF16)


def _memkv(mem, norm_g, w):
    bsz, nm, _ = mem.shape
    return pl.pallas_call(
        _memkv_kernel,
        grid=(bsz,),
        in_specs=[pl.BlockSpec((1, nm, D_MODEL), lambda b: (b, 0, 0)),
                  pl.BlockSpec(norm_g.shape, lambda b: (0, 0)),
                  pl.BlockSpec(w.shape, lambda b: (0, 0))],
        out_specs=pl.BlockSpec((1, nm, 2 * MEM_W), lambda b: (b, 0, 0)),
        out_shape=jax.ShapeDtypeStruct((bsz, nm, 2 * MEM_W), BF16),
        compiler_params=pltpu.CompilerParams(dimension_semantics=("arbitrary",),
                                             vmem_limit_bytes=VMEM_LIMIT),
        name="memkv",
    )(mem, norm_g, w)


def _xattn_kernel(h_ref, g_ref, wq_ref, mkv_ref, wo_ref, out_ref, *, tm):
    halves = [slice(0, tm // 2), slice(tm // 2, tm)]
    head_lanes = [slice(MEM_HEAD_DIM * a, MEM_HEAD_DIM * (a + 1)) for a in range(MEM_HEADS)]
    hs = [h_ref[rows, :] for rows in halves]
    hqs = [(_dot(_rms(h, g_ref[...]).astype(BF16), wq_ref[...]) * (MEM_HEAD_DIM ** -0.5 * LOG2E)).astype(BF16)
           for h in hs]
    ss = [[_dot_nt(hq[:, lanes], mkv_ref[0, :, lanes]) for lanes in head_lanes] for hq in hqs]
    es = [[jnp.exp2(s - jnp.max(s, axis=-1, keepdims=True)) for s in s_half] for s_half in ss]
    os = []
    for e_half in es:
        heads = []
        for a, e in enumerate(e_half):
            v = mkv_ref[0, :, MEM_W + MEM_HEAD_DIM * a:MEM_W + MEM_HEAD_DIM * (a + 1)]
            heads.append(_dot(e.astype(BF16), v) * (1.0 / jnp.sum(e, axis=-1, keepdims=True)))
        os.append(jnp.concatenate(heads, axis=1).astype(BF16))
    for rows, h, o in zip(halves, hs, os):
        out_ref[rows, :] = h + _dot(o, wo_ref[...])


def _xattn(h2d, norm_g, wq, mkv, wo, tm, rows_per_batch):
    n = h2d.shape[0]
    nm = mkv.shape[1]
    tiles_per_batch = rows_per_batch // tm
    full = lambda a: pl.BlockSpec(a.shape, lambda i: (0,) * a.ndim)
    return pl.pallas_call(
        functools.partial(_xattn_kernel, tm=tm),
        grid=(n // tm,),
        in_specs=[pl.BlockSpec((tm, D_MODEL), lambda i: (i, 0)), full(norm_g), full(wq),
                  pl.BlockSpec((1, nm, 2 * MEM_W), lambda i: (i // tiles_per_batch, 0, 0)),
                  full(wo)],
        out_specs=pl.BlockSpec((tm, D_MODEL), lambda i: (i, 0)),
        out_shape=jax.ShapeDtypeStruct((n, D_MODEL), F32),
        compiler_params=pltpu.CompilerParams(dimension_semantics=("arbitrary",),
                                             vmem_limit_bytes=VMEM_LIMIT),
        name="xattn",
    )(h2d, norm_g, wq, mkv, wo)


def _ffn_kernel(h_ref, g_ref, wgu_ref, wd_ref, gf_ref, out_ref, *, d_ff):
    tm = h_ref.shape[0]
    halves = [slice(0, tm // 2), slice(tm // 2, tm)]
    hs = [h_ref[rows, :] for rows in halves]
    hns = [_rms(h, g_ref[...]).astype(BF16) for h in hs]
    gates = [_dot(hn, wgu_ref[:, :d_ff]) for hn in hns]
    ups = [_dot(hn, wgu_ref[:, d_ff:]) for hn in hns]
    acts = [(jax.nn.silu(gate) * up).astype(BF16) for gate, up in zip(gates, ups)]
    for rows, h, act in zip(halves, hs, acts):
        out_ref[rows, :] = _rms(h + _dot(act, wd_ref[...]), gf_ref[...])


def _ffn(h2d, norm_g, wgu, wd, norm_f, tm):
    n = h2d.shape[0]
    d_ff = wd.shape[0]
    full = lambda a: pl.BlockSpec(a.shape, lambda i: (0,) * a.ndim)
    once = lambda a: pl.BlockSpec(a.shape, lambda i: (0,) * a.ndim, pipeline_mode=pl.Buffered(1))
    return pl.pallas_call(
        functools.partial(_ffn_kernel, d_ff=d_ff),
        grid=(n // tm,),
        in_specs=[pl.BlockSpec((tm, D_MODEL), lambda i: (i, 0)), full(norm_g), once(wgu), once(wd),
                  full(norm_f)],
        out_specs=pl.BlockSpec((tm, D_MODEL), lambda i: (i, 0)),
        out_shape=jax.ShapeDtypeStruct((n, D_MODEL), F32),
        compiler_params=pltpu.CompilerParams(dimension_semantics=("arbitrary",),
                                             vmem_limit_bytes=VMEM_LIMIT),
        name="ffn",
    )(h2d, norm_g, wgu, wd, norm_f)


def _block_diag2(w):
    z = jnp.zeros_like(w)
    return jnp.concatenate([jnp.concatenate([w, z], axis=-1), jnp.concatenate([z, w], axis=-1)], axis=-2)


def _chunk_lists(flags):
    bsz = flags.shape[0]
    n_chunks = N_KEY_CHUNKS
    per_step = NSA_QB // Q_BLOCK
    nqb = flags.shape[1] // per_step
    f = flags[:, :, :, 0, :].reshape(bsz, nqb, per_step, N_GROUPS, n_chunks, 2).max(axis=(2, 5))
    cid = jnp.arange(n_chunks, dtype=jnp.int32)
    own = (NSA_QB // KEY_CHUNK) * jnp.arange(nqb, dtype=jnp.int32)[None, :, None, None]
    active = (f > 0) & (cid < own)
    n_active = active.sum(axis=-1)
    slot = jnp.cumsum(active, axis=-1) - 1
    hit = active[..., :, None] & (slot[..., :, None] == cid)
    ids = jnp.sum(jnp.where(hit, cid[:, None], 0), axis=-2)
    ids = jnp.where(cid < n_active[..., None], ids, VOID_CHUNK)
    words, n_batches = [], []
    for g, batch in enumerate(SLC_BATCH):
        n_batches.append((n_active[:, :, g] + batch - 1) // batch)
        padded = jnp.pad(ids[:, :, g], ((0, 0), (0, 0), (0, LIST_WORDS * batch - n_chunks)),
                         constant_values=VOID_CHUNK).reshape(bsz, nqb, LIST_WORDS, batch)
        words.append(functools.reduce(jnp.bitwise_or,
                                      [padded[..., u] << (CHUNK_ID_BITS * u) for u in range(batch)]))
    n_batches = functools.reduce(jnp.maximum, n_batches)
    words = jnp.stack(words, axis=2)
    return n_batches.reshape(-1).astype(jnp.int32), words.reshape(-1).astype(jnp.int32)


def kernel(x, mem, norm_mix, w_in, w_cmp_k1, w_cmp_k2, w_cmp_v1, w_cmp_v2, pe_cmp_k, pe_cmp_v, ln_sgu, w_spatial, b_spatial, w_proj_a, w_proj_b, w_merge, b_merge, w_out, norm_mem_q, norm_mem_kv, w_mq, w_mkv, w_mo, norm_ffn, w_gate_up, w_down, norm_final):
    bsz, t, d = x.shape
    depth = norm_mix.shape[0]
    assert d == D_MODEL and t % Q_BLOCK == 0 and t // SEL_BLOCK <= N_BLK_PAD
    assert t // SEL_BLOCK >= N_SELECT and depth == 1 and t % 1024 == 0
    n = bsz * t
    tm = 512
    h = x.reshape(n, d)
    c0, c1, c2, c3 = NSA_W, NSA_W + KV_W, NSA_W + 3 * KV_W, NSA_W + 3 * KV_W + 2 * GMLP_WIDTH
    onehot = (jnp.arange(t)[:, None] // SEL_BLOCK == jnp.arange(N_BLK_PAD)[None, :]).astype(BF16)
    pad_cols = lambda w, width: jnp.pad(w, ((0, 0), (0, width - w.shape[1])))
    for l in range(depth):
        wi = w_in[l]
        wk, wv = [], []
        for branch in range(2):
            base = c1 + KV_W * branch
            for g in range(N_GROUPS):
                wk.append(pad_cols(wi[:, base + HEAD_DIM * g:base + HEAD_DIM * (g + 1)], LANES))
                v0 = base + N_GROUPS * HEAD_DIM + HEAD_DIM * g
                wv.append(pad_cols(wi[:, v0:v0 + HEAD_DIM], VT_ROWS))
        wrow = jnp.concatenate([wi[:, c0:c1]] + wk + [wi[:, c2:c3]], axis=1).astype(BF16)
        wnt = jnp.concatenate([wi[:, :c0]] + wv + [pad_cols(wi[:, c3:], GATET_ROWS)], axis=1).T.astype(BF16)
        (qt, kvc2, kaug, vt, u2, vn2, gatest), (wm_b, wpa_b, wpb_b, wo_b, wmq_b, wmkv_b, wmo_b) = _inproj(
            h, norm_mix[l][None], ln_sgu[l][None], wrow, wnt,
            [w_merge[l], w_proj_a[l], w_proj_b[l], w_out[l], w_mq[l], w_mkv[l], w_mo[l]], 2 * tm, bsz, t)

        pe2 = jnp.stack([pe_cmp_k[l], pe_cmp_v[l]])
        pe2 = jnp.concatenate([pe2, pe2], axis=-1)
        w1 = jnp.stack([w_cmp_k1[l], w_cmp_v1[l]]).reshape(2, CMP_BLOCK, HEAD_DIM, CMP_HIDDEN)
        w1bd = _block_diag2(w1).astype(BF16)
        w2k = _block_diag2(w_cmp_k2[l]).astype(BF16)
        w2vt = _block_diag2(w_cmp_v2[l]).T.astype(BF16)
        kc, vct = _compress(kvc2.reshape(bsz, t, KV_W), pe2, w1bd, w2k, w2vt)
        ocmp, selt, flags = _cmp_topk(qt, kc, vct, gatest)
        counts, lists = _chunk_lists(flags)
        onsa = _slc_win(counts, lists, qt, kaug, vt, onehot, selt, gatest, ocmp)

        h, (wgu_b, wd_b) = _merge(h, onsa.reshape(n, NSA_W), u2, vn2, norm_mix[l][None], wm_b,
                                  b_merge[l][None], w_spatial[l], b_spatial[l].T, wpa_b, wpb_b, wo_b,
                                  [w_gate_up[l], w_down[l]], 2 * tm)

        mkv = _memkv(mem, norm_mem_kv[l][None], wmkv_b)
        h = _xattn(h, norm_mem_q[l][None], wmq_b, mkv, wmo_b, 2 * tm, t)
        h = _ffn(h, norm_ffn[l][None], wgu_b, wd_b, norm_final[None], tm)
    return h.reshape(bsz, t, d)
```

```python
import functools

import jax
import jax.numpy as jnp
from jax import lax
from jax.experimental import pallas as pl
from jax.experimental.pallas import tpu as pltpu

F32 = jnp.float32
BF16 = jnp.bfloat16

LANES = 128
SUBLANES = 8
D_MODEL = 1024
N_HEADS = 8
HEAD_DIM = 64
N_GROUPS = 2
HPG = N_HEADS // N_GROUPS
CMP_BLOCK = 32
CMP_STRIDE = 16
CMP_HIDDEN = 128
SEL_BLOCK = 64
N_SELECT = 16
WINDOW = 512
Q_BLOCK = 256
KEY_CHUNK = 128
N_BLK_PAD = 128
NSA_W = N_HEADS * HEAD_DIM
KV_W = 2 * N_GROUPS * HEAD_DIM
GMLP_WIDTH = 512
GMLP_GROUPS = 4
GMLP_CHUNK = 128
MEM_HEADS = 4
MEM_HEAD_DIM = 128
MEM_W = MEM_HEADS * MEM_HEAD_DIM
NSA_QB = 256
SLC_BATCH = (4, 5)
PEELED_WORDS = 2
N_KEY_CHUNKS = N_BLK_PAD // 2
CHUNK_ID_BITS = 6
VOID_CHUNK = N_KEY_CHUNKS - 1
LIST_WORDS = -(-N_KEY_CHUNKS // min(SLC_BATCH))
VT_ROWS = 80
VCT_ROWS = VT_ROWS + N_BLK_PAD
CMP_KEY_CHUNK = 128
N_FORCED = 3
TOPK_ROW_STEP = 64
CMP_BLOCKS = 2
GATET_ROWS = 32
MASK_BIG = 1e30
EPS = 1e-6
NEG = -1e30
REMOVED = -3e38
SLOPES = tuple(2.0 ** (-8.0 * (h + 1) / N_HEADS) for h in range(N_HEADS))
LOG2E = 1.4426950408889634
Q_SCALE = HEAD_DIM ** -0.5 * LOG2E
VMEM_LIMIT = 56 * 1024 * 1024


def _dot(a, b):
    return jnp.dot(a, b, preferred_element_type=F32)


def _dot_nt(a, b):
    return lax.dot_general(a, b, (((1,), (1,)), ((), ())), preferred_element_type=F32)


def _rms(x, g):
    return x * lax.rsqrt(jnp.mean(x * x, axis=-1, keepdims=True) + EPS) * g


def _iota(shape, dim):
    return lax.broadcasted_iota(jnp.int32, shape, dim)


def _slope_row(g, nq):
    return jnp.concatenate(
        [jnp.full((1, nq), SLOPES[HPG * g + j] * LOG2E, F32) for j in range(HPG)], axis=1)


def _slope_feature_rows(slope_row, n_rows):
    hi = slope_row.astype(BF16).astype(F32)
    r = _iota((n_rows, slope_row.shape[1]), 0)
    return jnp.where(r == 0, hi, jnp.where(r == 1, slope_row - hi, 0.0)).astype(BF16)


def _chunk_slabs(st, chunk):
    if isinstance(st, list):
        return st
    return [st[chunk * u:chunk * (u + 1)] for u in range(st.shape[0] // chunk)]


def _col_max(st, dls, chunk):
    parts = [s.reshape(chunk // SUBLANES, SUBLANES, s.shape[1]).max(axis=0) - dl
             for s, dl in zip(_chunk_slabs(st, chunk), dls)]
    return functools.reduce(jnp.maximum, parts).max(axis=0, keepdims=True)


def _probs(st, dls, m, chunk, exp_dtype=F32):
    return jnp.concatenate([jnp.exp2((s - (m + dl)).astype(exp_dtype)).astype(BF16)
                            for s, dl in zip(_chunk_slabs(st, chunk), dls)], axis=0)


def _cast_plan(weights, n_steps):
    for w in weights:
        assert w.shape[0] % (16 * n_steps) == 0, (w.shape, n_steps)
    specs = [pl.BlockSpec((w.shape[0] // n_steps, w.shape[1]), lambda i: (i, 0)) for w in weights]
    shapes = [jax.ShapeDtypeStruct(w.shape, BF16) for w in weights]
    return specs, shapes


def _cast_blocks(src_refs, dst_refs):
    for src, dst in zip(src_refs, dst_refs):
        dst[...] = src[...].astype(BF16)


def _inproj_kernel(x_ref, g_ref, lng_ref, wrow_ref, wnt_ref, *refs, tm, n_cast):
    cast_in, cast_out = refs[:n_cast], refs[len(refs) - n_cast:]
    qt_ref, kvc_ref, kaug_ref, vt_ref, u_ref, vn_ref, gatest_ref = refs[n_cast:len(refs) - n_cast]
    _cast_blocks(cast_in, cast_out)
    xn = _rms(x_ref[...], g_ref[...]).astype(BF16)
    r = _dot(xn, wrow_ref[...])
    nt = _dot_nt(wnt_ref[...], xn)
    qt_ref[0] = (nt[0:NSA_W] * Q_SCALE).astype(BF16)
    kvc_ref[...] = r[:, 0:KV_W]
    lane = _iota((tm, LANES), 1)
    key_feat = jnp.where((lane == HEAD_DIM) | (lane == HEAD_DIM + 1),
                         _iota((tm, LANES), 0) & (KEY_CHUNK - 1), 0).astype(F32)
    ones_row = jnp.where(_iota((VT_ROWS, tm), 0) == HEAD_DIM, 1.0, 0.0)
    for a in range(2 * N_GROUPS):
        kaug_ref[0, a] = (r[:, KV_W + LANES * a:KV_W + LANES * (a + 1)] + key_feat).astype(BF16)
        vt_ref[0, a] = (nt[NSA_W + VT_ROWS * a:NSA_W + VT_ROWS * (a + 1)] + ones_row).astype(BF16)
    uv = jax.nn.gelu(r[:, KV_W + 2 * N_GROUPS * LANES:])
    u_ref[...] = uv[:, :GMLP_WIDTH]
    v = uv[:, GMLP_WIDTH:]
    vc = v - jnp.mean(v, axis=-1, keepdims=True)
    vn = vc * lax.rsqrt(jnp.mean(vc * vc, axis=-1, keepdims=True) + EPS) * lng_ref[...]
    vn_ref[...] = vn.astype(BF16)
    gatest_ref[0] = jax.nn.sigmoid(nt[NSA_W + 2 * N_GROUPS * VT_ROWS:])


def _inproj(x2, norm_g, ln_g, wrow, wnt, to_cast, tm, bsz, t):
    n = x2.shape[0]
    tpb = t // tm
    row = lambda w: pl.BlockSpec((tm, w), lambda i: (i, 0))
    full = lambda a: pl.BlockSpec(a.shape, lambda i: (0,) * a.ndim)
    cast_specs, cast_shapes = _cast_plan(to_cast, n // tm)
    outs = pl.pallas_call(
        functools.partial(_inproj_kernel, tm=tm, n_cast=len(to_cast)),
        grid=(n // tm,),
        in_specs=[row(D_MODEL), full(norm_g), full(ln_g), full(wrow), full(wnt)] + cast_specs,
        out_specs=[pl.BlockSpec((1, NSA_W, tm), lambda i: (i // tpb, 0, i % tpb)),
                   row(KV_W),
                   pl.BlockSpec((1, 2 * N_GROUPS, tm, LANES), lambda i: (i // tpb, 0, i % tpb, 0)),
                   pl.BlockSpec((1, 2 * N_GROUPS, VT_ROWS, tm), lambda i: (i // tpb, 0, 0, i % tpb)),
                   row(GMLP_WIDTH), row(GMLP_WIDTH),
                   pl.BlockSpec((1, GATET_ROWS, tm), lambda i: (i // tpb, 0, i % tpb))] + cast_specs,
        out_shape=[jax.ShapeDtypeStruct((bsz, NSA_W, t), BF16),
                   jax.ShapeDtypeStruct((n, KV_W), F32),
                   jax.ShapeDtypeStruct((bsz, 2 * N_GROUPS, t, LANES), BF16),
                   jax.ShapeDtypeStruct((bsz, 2 * N_GROUPS, VT_ROWS, t), BF16),
                   jax.ShapeDtypeStruct((n, GMLP_WIDTH), F32),
                   jax.ShapeDtypeStruct((n, GMLP_WIDTH), BF16),
                   jax.ShapeDtypeStruct((bsz, GATET_ROWS, t), F32)] + cast_shapes,
        compiler_params=pltpu.CompilerParams(dimension_semantics=("arbitrary",),
                                             vmem_limit_bytes=VMEM_LIMIT),
        name="inproj",
    )(x2, norm_g, ln_g, wrow, wnt, *to_cast)
    return outs[:7], outs[7:]


def _compress_kernel(xk_ref, xv_ref, pe_ref, w1_ref, w2k_ref, w2vt_ref, kc_ref, vct_ref, *, nc):
    outs = []
    for j, x_ref in enumerate((xk_ref, xv_ref)):
        a = jnp.zeros((nc, 2 * CMP_HIDDEN), F32)
        b = jnp.zeros((nc, 2 * CMP_HIDDEN), F32)
        for t in range(0, CMP_STRIDE, 2):
            xs = [x_ref[0, pl.ds(t + e, nc, stride=CMP_STRIDE), :] for e in range(2)]

            def half_block(first):
                lhs = jnp.concatenate([(xs[e] + pe_ref[j, first + t + e:first + t + e + 1, :]).astype(BF16)
                                       for e in range(2)], axis=1)
                rhs = jnp.concatenate([w1_ref[j, first + t], w1_ref[j, first + t + 1]], axis=0)
                return _dot(lhs, rhs)

            a = a + half_block(0)
            b = b + half_block(CMP_STRIDE)
        outs.append(jax.nn.gelu(a + pltpu.roll(b, nc - 1, 0)).astype(BF16))
    k2 = _dot(outs[0], w2k_ref[...])
    v_t = _dot_nt(w2vt_ref[...], outs[1])
    lane = _iota((nc, LANES), 1)
    key_feat = jnp.where((lane == HEAD_DIM) | (lane == HEAD_DIM + 1),
                         CMP_STRIDE * (_iota((nc, LANES), 0) & (CMP_KEY_CHUNK - 1)), 0).astype(F32)
    ci =_iota((N_BLK_PAD, nc), 1) * CMP_STRIDE
    sj = _iota((N_BLK_PAD, nc), 0) * SEL_BLOCK
    overlap_t = jnp.where((ci < sj + SEL_BLOCK) & (ci + (CMP_BLOCK - 1) >= sj), 1.0, 0.0).astype(BF16)
    ones_rows = jnp.where(_iota((VT_ROWS - HEAD_DIM, nc), 0) == 0, 1.0, 0.0).astype(BF16)
    for g in range(N_GROUPS):
        kg = k2 if g == 0 else pltpu.roll(k2, HEAD_DIM, 1)
        kc_ref[0, g] = jnp.where(lane < HEAD_DIM, kg, key_feat).astype(BF16)
        vct_ref[0, g, 0:HEAD_DIM, :] = v_t[HEAD_DIM * g:HEAD_DIM * (g + 1), :].astype(BF16)
        vct_ref[0, g, HEAD_DIM:VT_ROWS, :] = ones_rows
        vct_ref[0, g, VT_ROWS:VCT_ROWS, :] = overlap_t


def _compress(kvc3, pe2, w1bd, w2k, w2vt):
    bsz, t, _ = kvc3.shape
    nc = t // CMP_STRIDE
    full = lambda a: pl.BlockSpec(a.shape, lambda b: (0,) * a.ndim)
    return pl.pallas_call(
        functools.partial(_compress_kernel, nc=nc),
        grid=(bsz,),
        in_specs=[pl.BlockSpec((1, t, LANES), lambda b: (b, 0, 0)),
                  pl.BlockSpec((1, t, LANES), lambda b: (b, 0, 1)),
                  full(pe2), full(w1bd), full(w2k), full(w2vt)],
        out_specs=[pl.BlockSpec((1, N_GROUPS, nc, LANES), lambda b: (b, 0, 0, 0)),
                   pl.BlockSpec((1, N_GROUPS, VCT_ROWS, nc), lambda b: (b, 0, 0, 0))],
        out_shape=[jax.ShapeDtypeStruct((bsz, N_GROUPS, nc, LANES), BF16),
                   jax.ShapeDtypeStruct((bsz, N_GROUPS, VCT_ROWS, nc), BF16)],
        compiler_params=pltpu.CompilerParams(dimension_semantics=("arbitrary",),
                                             vmem_limit_bytes=VMEM_LIMIT),
        name="compress",
    )(kvc3, kvc3, pe2, w1bd, w2k, w2vt)


def _cmp_topk_kernel(qt_ref, kc_ref, vct_ref, gatest_ref, ocmp_ref, selt_ref, flags_ref,
                     m_ref, acc_ref, *, ncp):
    step = pl.program_id(1)
    n_chunks = ncp // CMP_KEY_CHUNK
    chunk_tokens = CMP_KEY_CHUNK * CMP_STRIDE
    tile_heads = lambda a: jnp.concatenate([a] * HPG, axis=1)
    tail_chunks = min(2, n_chunks)
    head_chunks = n_chunks - tail_chunks
    tail_keys = tail_chunks * CMP_KEY_CHUNK
    blocks = range(CMP_BLOCKS)
    groups = range(N_GROUPS)
    items = [(bi, g) for bi in blocks for g in groups]
    start = [(step * CMP_BLOCKS + bi) * Q_BLOCK for bi in blocks]
    qlanes = [slice(Q_BLOCK * bi, Q_BLOCK * (bi + 1)) for bi in blocks]
    nck = [(start[bi] + Q_BLOCK - CMP_BLOCK) // CMP_STRIDE // CMP_KEY_CHUNK + 1 for bi in blocks]
    tail_c0 = [jnp.maximum(nck[bi] - tail_chunks, 0) for bi in blocks]
    tail_rows = [pl.ds(pl.multiple_of(tail_c0[bi] * CMP_KEY_CHUNK, CMP_KEY_CHUNK), tail_keys)
                 for bi in blocks]
    gt = gatest_ref[0]
    slope_rows = [_slope_row(g, Q_BLOCK) for g in groups]
    qas = {(bi, g): jnp.concatenate(
        [jnp.concatenate([qt_ref[0, HEAD_DIM * (HPG * g + j):HEAD_DIM * (HPG * g + j + 1), qlanes[bi]]
                          for j in range(HPG)], axis=1),
         _slope_feature_rows(slope_rows[g], LANES - HEAD_DIM)], axis=0) for bi, g in items}

    def delta(bi, g, c):
        return slope_rows[g] * (start[bi] - c * chunk_tokens).astype(F32)

    for bi in blocks:
        if head_chunks > 0:
            @pl.when(nck[bi] > tail_chunks)
            def _():
                sts = [_dot(kc_ref[0, g, 0:head_chunks * CMP_KEY_CHUNK, :], qas[bi, g]) for g in groups]
                dls = [[delta(bi, g, c) + jnp.where(c < nck[bi] - tail_chunks, 0.0, MASK_BIG)
                        for c in range(head_chunks)] for g in groups]
                ms = [_col_max(sts[g], dls[g], CMP_KEY_CHUNK) for g in groups]
                ps = [_probs(sts[g], dls[g], ms[g], CMP_KEY_CHUNK) for g in groups]
                for g in groups:
                    m_ref[bi, g] = ms[g]
                    acc_ref[bi, g] = _dot(vct_ref[0, g, :, 0:head_chunks * CMP_KEY_CHUNK], ps[g])

            @pl.when(nck[bi] <= tail_chunks)
            def _():
                m_ref[bi] = jnp.full(m_ref.shape[1:], NEG, F32)
                acc_ref[bi] = jnp.zeros(acc_ref.shape[1:], F32)

    def tail_and_select(n_rows):
        if head_chunks > 0:
            m_old = {it: m_ref[it[0], it[1]] for it in items}
            acc_old = {it: acc_ref[it[0], it[1]] for it in items}
        else:
            m_old = {it: jnp.full((1, HPG * Q_BLOCK), NEG, F32) for it in items}
            acc_old = {it: jnp.zeros((VCT_ROWS, HPG * Q_BLOCK), F32) for it in items}
        key_row = _iota((tail_keys, Q_BLOCK), 0)
        q_lane = _iota((tail_keys, Q_BLOCK), 1)
        tail_bias = []
        for bi in blocks:
            key_end = CMP_STRIDE * (tail_c0[bi] * CMP_KEY_CHUNK + key_row) + (CMP_BLOCK - 1) - start[bi]
            tail_bias.append(tile_heads(jnp.where(key_end <= q_lane, 0.0, NEG)))
        sts = {(bi, g): _dot(kc_ref[0, g, tail_rows[bi], :], qas[bi, g]) + tail_bias[bi] for bi, g in items}
        dls = {(bi, g): [delta(bi, g, tail_c0[bi] + u) for u in range(tail_chunks)] for bi, g in items}
        m_new = {it: jnp.maximum(m_old[it], _col_max(sts[it], dls[it], CMP_KEY_CHUNK)) for it in items}
        ps = {it: _probs(sts[it], dls[it], m_new[it], CMP_KEY_CHUNK) for it in items}
        accs = {(bi, g): jnp.exp2(m_old[bi, g] - m_new[bi, g]) * acc_old[bi, g]
                + _dot(vct_ref[0, g, :, tail_rows[bi]], ps[bi, g]) for bi, g in items}

        blk_n = _iota((n_rows, Q_BLOCK), 0)
        bf = blk_n.astype(F32)
        rk, cur = {}, []
        for bi in blocks:
            t_row = start[bi] + _iota((1, Q_BLOCK), 1)
            cur.append(lax.shift_right_logical(t_row, SEL_BLOCK.bit_length() - 1))
            forced = (blk_n == 0) | (blk_n == cur[bi]) | (blk_n == cur[bi] - 1)
            has_key = t_row >= CMP_BLOCK - 1
            per_head = []
            for g in groups:
                acc = accs[bi, g]
                inv_l = 1.0 / jnp.maximum(acc[HEAD_DIM:HEAD_DIM + 1], 1e-30)
                o = acc[0:HEAD_DIM] * inv_l
                imp_h = acc[VT_ROWS:VT_ROWS + n_rows] * inv_l
                imp = sum(imp_h[:, Q_BLOCK * j:Q_BLOCK * (j + 1)] for j in range(HPG))
                imp = jnp.where(has_key, imp, 0.0)
                for j in range(HPG):
                    h = HPG * g + j
                    per_head.append(jnp.where(
                        has_key, gt[3 * h:3 * h + 1, qlanes[bi]] * o[:, Q_BLOCK * j:Q_BLOCK * (j + 1)], 0.0))
                rk[bi, g] = jnp.where(blk_n <= cur[bi], jnp.where(forced, REMOVED, imp), NEG)
            ocmp_ref[0, :, qlanes[bi]] = jnp.concatenate(per_head, axis=0)

        for _ in range(N_SELECT - N_FORCED):
            for it in items:
                m = jnp.max(rk[it], axis=0, keepdims=True)
                idx = jnp.min(jnp.where(rk[it] == m, bf, float(N_BLK_PAD)), axis=0, keepdims=True)
                rk[it] = jnp.where(bf == idx, REMOVED, rk[it])
        ones8 = jnp.ones((SUBLANES, Q_BLOCK), F32)
        for bi, g in items:
            sel = jnp.where((blk_n <= cur[bi]) & (rk[bi, g] < 2.0 * NEG), 1.0, 0.0)
            selt_ref[0, g, 0:n_rows, qlanes[bi]] = sel.astype(BF16)
            flag = (_dot_nt(ones8, sel) > 0.5).astype(jnp.int32)
            if n_rows < N_BLK_PAD:
                selt_ref[0, g, n_rows:N_BLK_PAD, qlanes[bi]] = jnp.zeros((N_BLK_PAD - n_rows, Q_BLOCK), BF16)
                flag = jnp.concatenate([flag, jnp.zeros((SUBLANES, N_BLK_PAD - n_rows), jnp.int32)], axis=1)
            flags_ref[0, bi, g] = flag

    causal_rows = (start[-1] + Q_BLOCK) // SEL_BLOCK
    for n_rows in range(TOPK_ROW_STEP, N_BLK_PAD + 1, TOPK_ROW_STEP):
        @pl.when((causal_rows > n_rows - TOPK_ROW_STEP) & (causal_rows <= n_rows))
        def _():
            tail_and_select(n_rows)


def _cmp_topk(qt, kc, vct, gatest):
    bsz, _, t = qt.shape
    ncp = t // CMP_STRIDE
    nqb = t // Q_BLOCK
    step_q = CMP_BLOCKS * Q_BLOCK
    return pl.pallas_call(
        functools.partial(_cmp_topk_kernel, ncp=ncp),
        grid=(bsz, nqb // CMP_BLOCKS),
        in_specs=[pl.BlockSpec((1, NSA_W, step_q), lambda b, i: (b, 0, i)),
                  pl.BlockSpec((1, N_GROUPS, ncp, LANES), lambda b, i: (b, 0, 0, 0)),
                  pl.BlockSpec((1, N_GROUPS, VCT_ROWS, ncp), lambda b, i: (b, 0, 0, 0)),
                  pl.BlockSpec((1, GATET_ROWS, step_q), lambda b, i: (b, 0, i))],
        out_specs=[pl.BlockSpec((1, NSA_W, step_q), lambda b, i: (b, 0, i)),
                   pl.BlockSpec((1, N_GROUPS, N_BLK_PAD, step_q), lambda b, i: (b, 0, 0, i)),
                   pl.BlockSpec((1, CMP_BLOCKS, N_GROUPS, SUBLANES, N_BLK_PAD), lambda b, i: (b, i, 0, 0, 0))],
        out_shape=[jax.ShapeDtypeStruct((bsz, NSA_W, t), F32),
                   jax.ShapeDtypeStruct((bsz, N_GROUPS, N_BLK_PAD, t), BF16),
                   jax.ShapeDtypeStruct((bsz, nqb, N_GROUPS, SUBLANES, N_BLK_PAD), jnp.int32)],
        scratch_shapes=[pltpu.VMEM((CMP_BLOCKS, N_GROUPS, 1, HPG * Q_BLOCK), F32),
                        pltpu.VMEM((CMP_BLOCKS, N_GROUPS, VCT_ROWS, HPG * Q_BLOCK), F32)],
        compiler_params=pltpu.CompilerParams(dimension_semantics=("arbitrary", "arbitrary"),
                                             vmem_limit_bytes=VMEM_LIMIT),
        name="cmp_topk",
    )(qt, kc, vct, gatest)


def _slc_win_kernel(counts_ref, lists_ref, qt_ref, kaug_ref, vt_ref, oh_ref, selt_ref, gatest_ref,
                    ocmp_ref, out_ref, qaug_ref, m_ref, acc_ref, *, nqb):
    b = pl.program_id(0)
    qb = pl.program_id(1)
    step_id = b * nqb + qb
    start = qb * NSA_QB
    tile_heads = lambda a: jnp.concatenate([a] * HPG, axis=1)
    groups = range(N_GROUPS)
    gt = gatest_ref[0]
    slope_rows = [_slope_row(g, NSA_QB) for g in groups]

    def normalize(acc):
        return acc[0:HEAD_DIM] / jnp.maximum(acc[HEAD_DIM:HEAD_DIM + 1], 1e-30)

    def half_lanes(w):
        return [slice(NSA_QB * h + KEY_CHUNK * w, NSA_QB * h + KEY_CHUNK * (w + 1)) for h in range(HPG)]

    def half(a, w):
        return jnp.concatenate([a[:, s] for s in half_lanes(w)], axis=1)

    def unhalf(lo, hi):
        return jnp.concatenate([x[:, KEY_CHUNK * h:KEY_CHUNK * (h + 1)]
                                for h in range(HPG) for x in (lo, hi)], axis=1)

    def chunk_at(rel):
        pos = start + rel * KEY_CHUNK
        rows = pl.ds(pl.multiple_of(jnp.maximum(pos, 0), KEY_CHUNK), KEY_CHUNK)
        return rows, (jnp.where(pos >= 0, 0.0, MASK_BIG) if rel < 0 else 0.0)

    ki = _iota((KEY_CHUNK, KEY_CHUNK), 0)
    qi = _iota((KEY_CHUNK, KEY_CHUNK), 1)
    upper_bias = tile_heads(jnp.where(ki > qi, 0.0, NEG))
    lower_bias = tile_heads(jnp.where(ki <= qi, 0.0, NEG))
    n_mid = WINDOW // KEY_CHUNK
    no_bias = jnp.zeros((KEY_CHUNK, KEY_CHUNK), F32)
    mid_first_bias = tile_heads(jnp.concatenate([no_bias, jnp.where(ki > qi, 0.0, NEG)], axis=1))
    mid_last_bias = tile_heads(jnp.concatenate([jnp.where(ki <= qi, 0.0, NEG), no_bias], axis=1))
    own_bias = tile_heads(jnp.where(_iota((KEY_CHUNK, NSA_QB), 0) <= _iota((KEY_CHUNK, NSA_QB), 1), 0.0, NEG))

    for g in range(N_GROUPS):
        qaug_ref[g, 0:HEAD_DIM, :] = jnp.concatenate(
            [qt_ref[0, HEAD_DIM * (HPG * g + j):HEAD_DIM * (HPG * g + j + 1), :] for j in range(HPG)],
            axis=1)
        qaug_ref[g, HEAD_DIM:LANES, :] = _slope_feature_rows(slope_rows[g], LANES - HEAD_DIM)
        sel_bias = ((selt_ref[0, g].astype(F32) - 1.0) * MASK_BIG).astype(BF16)
        qaug_ref[g, LANES:2 * LANES, :] = tile_heads(sel_bias)

    mid = [chunk_at(r) for r in range(1 - n_mid, 1)]
    lo_rows, lo_kill = chunk_at(-n_mid)
    hi_rows, _ = chunk_at(1)
    own_rows, _ = chunk_at(0)
    kw = lambda g: kaug_ref.at[0, N_GROUPS + g]
    vw = lambda g: vt_ref.at[0, N_GROUPS + g]
    slc_keys = lambda g, rows: jnp.concatenate([kaug_ref[0, g, rows, :], oh_ref[rows, :]], axis=1)
    q_half = lambda g, w, nrow: jnp.concatenate([qaug_ref[g, 0:nrow, s] for s in half_lanes(w)], axis=1)

    def slc_scores_group(g, i, n_words=1, with_own=False):
        words = [lists_ref[(step_id * N_GROUPS + g) * LIST_WORDS + i + w] for w in range(n_words)]
        ks, vs, dls = [], [], []
        if with_own:
            ks.append(slc_keys(g, own_rows))
            vs.append(vt_ref[0, g, :, own_rows])
            dls.append(0.0)
        for word, u in [(word, u) for word in words for u in range(SLC_BATCH[g])]:
            cid = lax.shift_right_logical(word, CHUNK_ID_BITS * u) & VOID_CHUNK
            valid = cid < VOID_CHUNK
            c = jnp.where(valid, cid, 0)
            rows = pl.ds(pl.multiple_of(c * KEY_CHUNK, KEY_CHUNK), KEY_CHUNK)
            ks.append(jnp.concatenate([kaug_ref[0, g, rows, :], oh_ref[rows, :]], axis=1))
            vs.append(vt_ref[0, g, :, rows])
            dls.append(slope_rows[g] * (start - c * KEY_CHUNK).astype(F32)
                       + jnp.where(valid, 0.0, MASK_BIG))
        return _dot(jnp.concatenate(ks, axis=0), qaug_ref[g]), jnp.concatenate(vs, axis=1), dls

    def slc_scores(i):
        return zip(*[slc_scores_group(g, i) for g in range(N_GROUPS)])

    probs = functools.partial(_probs, chunk=KEY_CHUNK, exp_dtype=BF16)
    wmid_st = [_dot(jnp.concatenate([kw(g)[rows, :] for rows, _ in mid], axis=0), qaug_ref[g, 0:LANES, :])
               for g in groups]
    wmid_st = [[st[0:KEY_CHUNK] + mid_first_bias] + _chunk_slabs(st[KEY_CHUNK:(n_mid - 1) * KEY_CHUNK], KEY_CHUNK)
               + [st[(n_mid - 1) * KEY_CHUNK:] + mid_last_bias] for st in wmid_st]
    wlo_st = [_dot(kw(g)[lo_rows, :], q_half(g, 0, LANES)) + upper_bias for g in groups]
    whi_st = [_dot(kw(g)[hi_rows, :], q_half(g, 1, LANES)) + lower_bias for g in groups]
    ohi_st = [_dot(slc_keys(g, hi_rows), q_half(g, 1, 2 * LANES)) + lower_bias for g in groups]
    st0, vs0, dls0 = zip(*[slc_scores_group(g, 0, PEELED_WORDS, with_own=True) for g in groups])
    st0 = [[st0[g][0:KEY_CHUNK] + own_bias] + _chunk_slabs(st0[g][KEY_CHUNK:], KEY_CHUNK) for g in groups]

    wmid_dls = [[slope_rows[g] * float(-KEY_CHUNK * r) + kill
                 for r, (_, kill) in zip(range(1 - n_mid, 1), mid)] for g in groups]
    wlo_dl = [half(slope_rows[g], 0) * float(KEY_CHUNK * n_mid) + lo_kill for g in groups]
    hi_dl = [half(slope_rows[g], 1) * float(-KEY_CHUNK) for g in groups]
    win_m = [jnp.maximum(_col_max(wmid_st[g], wmid_dls[g], KEY_CHUNK),
                         unhalf(_col_max(wlo_st[g], [wlo_dl[g]], KEY_CHUNK),
                                _col_max(whi_st[g], [hi_dl[g]], KEY_CHUNK))) for g in groups]
    wmid_p = [probs(wmid_st[g], wmid_dls[g], win_m[g]) for g in groups]
    wlo_p = [probs(wlo_st[g], [wlo_dl[g]], half(win_m[g], 0)) for g in groups]
    whi_p = [probs(whi_st[g], [hi_dl[g]], half(win_m[g], 1)) for g in groups]
    neg_half = jnp.full((1, HPG * KEY_CHUNK), NEG, F32)
    own_m = [jnp.maximum(_col_max(st0[g], dls0[g], KEY_CHUNK),
                         unhalf(neg_half, _col_max(ohi_st[g], [hi_dl[g]], KEY_CHUNK))) for g in groups]
    ohi_p = [probs(ohi_st[g], [hi_dl[g]], half(own_m[g], 1)) for g in groups]
    p0 = [probs(st0[g], dls0[g], own_m[g]) for g in groups]
    o_win = []
    for g in groups:
        acc = _dot(jnp.concatenate([vw(g)[:, rows] for rows, _ in mid], axis=1), wmid_p[g])
        acc = acc + unhalf(_dot(vw(g)[:, lo_rows], wlo_p[g]), _dot(vw(g)[:, hi_rows], whi_p[g]))
        o_win.append(normalize(acc))
    zero_half = jnp.zeros((VT_ROWS, HPG * KEY_CHUNK), F32)
    for g in groups:
        m_ref[g] = own_m[g]
        acc_ref[g] = (_dot(vs0[g], p0[g])
                      + unhalf(zero_half, _dot(vt_ref[0, g, :, hi_rows], ohi_p[g])))

    def slc_body(i, carry):
        sts, vss, dlss = slc_scores(i)
        m_old = [m_ref[g] for g in range(N_GROUPS)]
        m_new = [jnp.maximum(m_old[g], _col_max(sts[g], dlss[g], KEY_CHUNK)) for g in range(N_GROUPS)]
        ps = [probs(sts[g], dlss[g], m_new[g]) for g in range(N_GROUPS)]
        for g in range(N_GROUPS):
            acc_ref[g] = jnp.exp2(m_old[g] - m_new[g]) * acc_ref[g] + _dot(vss[g], ps[g])
            m_ref[g] = m_new[g]
        return carry

    lax.fori_loop(PEELED_WORDS, counts_ref[step_id], slc_body, 0)

    per_head = []
    for g in range(N_GROUPS):
        o_slc = normalize(acc_ref[g])
        for j in range(HPG):
            h = HPG * g + j
            lanes = slice(NSA_QB * j, NSA_QB * (j + 1))
            per_head.append(gt[3 * h + 1:3 * h + 2, :] * o_slc[:, lanes]
                            + gt[3 * h + 2:3 * h + 3, :] * o_win[g][:, lanes])
    o_t = jnp.concatenate(per_head, axis=0)
    out_ref[0] = (ocmp_ref[0] + o_t).T.astype(BF16)


def _slc_win(counts, lists, qt, kaug, vt, onehot, selt, gatest, ocmp):
    bsz, _, t = qt.shape
    nqb = t // NSA_QB
    once = lambda shape, imap: pl.BlockSpec(shape, imap, pipeline_mode=pl.Buffered(1))
    grid_spec = pltpu.PrefetchScalarGridSpec(
        num_scalar_prefetch=2,
        grid=(bsz, nqb),
        in_specs=[pl.BlockSpec((1, NSA_W, NSA_QB), lambda b, i, *_: (b, 0, i)),
                  pl.BlockSpec((1, 2 * N_GROUPS, t, LANES), lambda b, i, *_: (b, 0, 0, 0)),
                  pl.BlockSpec((1, 2 * N_GROUPS, VT_ROWS, t), lambda b, i, *_: (b, 0, 0, 0)),
                  once((t, N_BLK_PAD), lambda b, i, *_: (0, 0)),
                  pl.BlockSpec((1, N_GROUPS, N_BLK_PAD, NSA_QB), lambda b, i, *_: (b, 0, 0, i)),
                  pl.BlockSpec((1, GATET_ROWS, NSA_QB), lambda b, i, *_: (b, 0, i)),
                  pl.BlockSpec((1, NSA_W, NSA_QB), lambda b, i, *_: (b, 0, i))],
        out_specs=pl.BlockSpec((1, NSA_QB, NSA_W), lambda b, i, *_: (b, i, 0)),
        scratch_shapes=[pltpu.VMEM((N_GROUPS, 2 * LANES, HPG * NSA_QB), BF16),
                        pltpu.VMEM((N_GROUPS, 1, HPG * NSA_QB), F32),
                        pltpu.VMEM((N_GROUPS, VT_ROWS, HPG * NSA_QB), F32)],
    )
    return pl.pallas_call(
        functools.partial(_slc_win_kernel, nqb=nqb),
        grid_spec=grid_spec,
        out_shape=jax.ShapeDtypeStruct((bsz, t, NSA_W), BF16),
        compiler_params=pltpu.CompilerParams(dimension_semantics=("arbitrary", "arbitrary"),
                                             vmem_limit_bytes=VMEM_LIMIT),
        name="slc_win",
    )(counts, lists, qt, kaug, vt, onehot, selt, gatest, ocmp)


def _merge_kernel(x_ref, onsa_ref, u_ref, vn_ref, g_ref, wm_ref, bm_ref, ws_ref, bs_ref,
                  wpa_ref, wpb_ref, wo_ref, *refs, tm, n_cast):
    cast_in, h_ref, cast_out = refs[:n_cast], refs[n_cast], refs[n_cast + 1:]
    _cast_blocks(cast_in, cast_out)
    x = x_ref[...]
    xn = _rms(x, g_ref[...]).astype(BF16)
    tril = _iota((GMLP_CHUNK, GMLP_CHUNK), 0) >= _iota((GMLP_CHUNK, GMLP_CHUNK), 1)
    sgu_rows = []
    for c in range(tm // GMLP_CHUNK):
        rows = slice(GMLP_CHUNK * c, GMLP_CHUNK * (c + 1))
        cols = []
        for g in range(GMLP_GROUPS):
            lanes = slice(LANES * g, LANES * (g + 1))
            w = jnp.where(tril, ws_ref[g], 0.0).astype(BF16)
            cols.append(_dot(w, vn_ref[rows, lanes]) + bs_ref[:, g:g + 1])
        sgu_rows.append(u_ref[rows, :] * jnp.concatenate(cols, axis=1))
    o_sgu = jnp.concatenate(sgu_rows, axis=0).astype(BF16)
    mg = jax.nn.sigmoid(_dot(xn, wm_ref[...]) + bm_ref[...])
    mixed = (mg[:, :D_MODEL] * _dot(onsa_ref[...], wpa_ref[...])
             + mg[:, D_MODEL:] * _dot(o_sgu, wpb_ref[...]))
    h_ref[...] = x + _dot(mixed.astype(BF16), wo_ref[...])


def _merge(x2, onsa2, u2, vn2, norm_g, wm, bm, ws, bs_t, wpa, wpb, wo, to_cast, tm):
    n = x2.shape[0]
    row = lambda w: pl.BlockSpec((tm, w), lambda i: (i, 0))
    full = lambda a: pl.BlockSpec(a.shape, lambda i: (0,) * a.ndim)
    cast_specs, cast_shapes = _cast_plan(to_cast, n // tm)
    outs = pl.pallas_call(
        functools.partial(_merge_kernel, tm=tm, n_cast=len(to_cast)),
        grid=(n // tm,),
        in_specs=[row(D_MODEL), row(NSA_W), row(GMLP_WIDTH), row(GMLP_WIDTH), full(norm_g),
                  full(wm), full(bm), full(ws), full(bs_t), full(wpa), full(wpb), full(wo)] + cast_specs,
        out_specs=[row(D_MODEL)] + cast_specs,
        out_shape=[jax.ShapeDtypeStruct((n, D_MODEL), F32)] + cast_shapes,
        compiler_params=pltpu.CompilerParams(dimension_semantics=("arbitrary",),
                                             vmem_limit_bytes=VMEM_LIMIT),
        name="merge",
    )(x2, onsa2, u2, vn2, norm_g, wm, bm, ws, bs_t, wpa, wpb, wo, *to_cast)
    return outs[0], outs[1:]


def _memkv_kernel(mem_ref, g_ref, w_ref, out_ref):
    out_ref[0] = _dot(_rms(mem_ref[0], g_ref[...]).astype(BF16), w_ref[...]).astype(BF16)


def _memkv(mem, norm_g, w):
    bsz, nm, _ = mem.shape
    return pl.pallas_call(
        _memkv_kernel,
        grid=(bsz,),
        in_specs=[pl.BlockSpec((1, nm, D_MODEL), lambda b: (b, 0, 0)),
                  pl.BlockSpec(norm_g.shape, lambda b: (0, 0)),
                  pl.BlockSpec(w.shape, lambda b: (0, 0))],
        out_specs=pl.BlockSpec((1, nm, 2 * MEM_W), lambda b: (b, 0, 0)),
        out_shape=jax.ShapeDtypeStruct((bsz, nm, 2 * MEM_W), BF16),
        compiler_params=pltpu.CompilerParams(dimension_semantics=("arbitrary",),
                                             vmem_limit_bytes=VMEM_LIMIT),
        name="memkv",
    )(mem, norm_g, w)


def _xattn_kernel(h_ref, g_ref, wq_ref, mkv_ref, wo_ref, out_ref, *, tm):
    halves = [slice(0, tm // 2), slice(tm // 2, tm)]
    head_lanes = [slice(MEM_HEAD_DIM * a, MEM_HEAD_DIM * (a + 1)) for a in range(MEM_HEADS)]
    hs = [h_ref[rows, :] for rows in halves]
    hqs = [(_dot(_rms(h, g_ref[...]).astype(BF16), wq_ref[...]) * (MEM_HEAD_DIM ** -0.5 * LOG2E)).astype(BF16)
           for h in hs]
    ss = [[_dot_nt(hq[:, lanes], mkv_ref[0, :, lanes]) for lanes in head_lanes] for hq in hqs]
    es = [[jnp.exp2(s - jnp.max(s, axis=-1, keepdims=True)) for s in s_half] for s_half in ss]
    os = []
    for e_half in es:
        heads = []
        for a, e in enumerate(e_half):
            v = mkv_ref[0, :, MEM_W + MEM_HEAD_DIM * a:MEM_W + MEM_HEAD_DIM * (a + 1)]
            heads.append(_dot(e.astype(BF16), v) * (1.0 / jnp.sum(e, axis=-1, keepdims=True)))
        os.append(jnp.concatenate(heads, axis=1).astype(BF16))
    for rows, h, o in zip(halves, hs, os):
        out_ref[rows, :] = h + _dot(o, wo_ref[...])


def _xattn(h2d, norm_g, wq, mkv, wo, tm, rows_per_batch):
    n = h2d.shape[0]
    nm = mkv.shape[1]
    tiles_per_batch = rows_per_batch // tm
    full = lambda a: pl.BlockSpec(a.shape, lambda i: (0,) * a.ndim)
    return pl.pallas_call(
        functools.partial(_xattn_kernel, tm=tm),
        grid=(n // tm,),
        in_specs=[pl.BlockSpec((tm, D_MODEL), lambda i: (i, 0)), full(norm_g), full(wq),
                  pl.BlockSpec((1, nm, 2 * MEM_W), lambda i: (i // tiles_per_batch, 0, 0)),
                  full(wo)],
        out_specs=pl.BlockSpec((tm, D_MODEL), lambda i: (i, 0)),
        out_shape=jax.ShapeDtypeStruct((n, D_MODEL), F32),
        compiler_params=pltpu.CompilerParams(dimension_semantics=("arbitrary",),
                                             vmem_limit_bytes=VMEM_LIMIT),
        name="xattn",
    )(h2d, norm_g, wq, mkv, wo)


def _ffn_kernel(h_ref, g_ref, wgu_ref, wd_ref, gf_ref, out_ref, *, d_ff):
    tm = h_ref.shape[0]
    halves = [slice(0, tm // 2), slice(tm // 2, tm)]
    hs = [h_ref[rows, :] for rows in halves]
    hns = [_rms(h, g_ref[...]).astype(BF16) for h in hs]
    gates = [_dot(hn, wgu_ref[:, :d_ff]) for hn in hns]
    ups = [_dot(hn, wgu_ref[:, d_ff:]) for hn in hns]
    acts = [(jax.nn.silu(gate) * up).astype(BF16) for gate, up in zip(gates, ups)]
    for rows, h, act in zip(halves, hs, acts):
        out_ref[rows, :] = _rms(h + _dot(act, wd_ref[...]), gf_ref[...])


def _ffn(h2d, norm_g, wgu, wd, norm_f, tm):
    n = h2d.shape[0]
    d_ff = wd.shape[0]
    full = lambda a: pl.BlockSpec(a.shape, lambda i: (0,) * a.ndim)
    once = lambda a: pl.BlockSpec(a.shape, lambda i: (0,) * a.ndim, pipeline_mode=pl.Buffered(1))
    return pl.pallas_call(
        functools.partial(_ffn_kernel, d_ff=d_ff),
        grid=(n // tm,),
        in_specs=[pl.BlockSpec((tm, D_MODEL), lambda i: (i, 0)), full(norm_g), once(wgu), once(wd),
                  full(norm_f)],
        out_specs=pl.BlockSpec((tm, D_MODEL), lambda i: (i, 0)),
        out_shape=jax.ShapeDtypeStruct((n, D_MODEL), F32),
        compiler_params=pltpu.CompilerParams(dimension_semantics=("arbitrary",),
                                             vmem_limit_bytes=VMEM_LIMIT),
        name="ffn",
    )(h2d, norm_g, wgu, wd, norm_f)


def _block_diag2(w):
    z = jnp.zeros_like(w)
    return jnp.concatenate([jnp.concatenate([w, z], axis=-1), jnp.concatenate([z, w], axis=-1)], axis=-2)


def _chunk_lists(flags):
    bsz = flags.shape[0]
    n_chunks = N_KEY_CHUNKS
    per_step = NSA_QB // Q_BLOCK
    nqb = flags.shape[1] // per_step
    f = flags[:, :, :, 0, :].reshape(bsz, nqb, per_step, N_GROUPS, n_chunks, 2).max(axis=(2, 5))
    cid = jnp.arange(n_chunks, dtype=jnp.int32)
    own = (NSA_QB // KEY_CHUNK) * jnp.arange(nqb, dtype=jnp.int32)[None, :, None, None]
    active = (f > 0) & (cid < own)
    n_active = active.sum(axis=-1)
    slot = jnp.cumsum(active, axis=-1) - 1
    hit = active[..., :, None] & (slot[..., :, None] == cid)
    ids = jnp.sum(jnp.where(hit, cid[:, None], 0), axis=-2)
    ids = jnp.where(cid < n_active[..., None], ids, VOID_CHUNK)
    words, n_batches = [], []
    for g, batch in enumerate(SLC_BATCH):
        n_batches.append((n_active[:, :, g] + batch - 1) // batch)
        padded = jnp.pad(ids[:, :, g], ((0, 0), (0, 0), (0, LIST_WORDS * batch - n_chunks)),
                         constant_values=VOID_CHUNK).reshape(bsz, nqb, LIST_WORDS, batch)
        words.append(functools.reduce(jnp.bitwise_or,
                                      [padded[..., u] << (CHUNK_ID_BITS * u) for u in range(batch)]))
    n_batches = functools.reduce(jnp.maximum, n_batches)
    words = jnp.stack(words, axis=2)
    return n_batches.reshape(-1).astype(jnp.int32), words.reshape(-1).astype(jnp.int32)


def kernel(x, mem, norm_mix, w_in, w_cmp_k1, w_cmp_k2, w_cmp_v1, w_cmp_v2, pe_cmp_k, pe_cmp_v, ln_sgu, w_spatial, b_spatial, w_proj_a, w_proj_b, w_merge, b_merge, w_out, norm_mem_q, norm_mem_kv, w_mq, w_mkv, w_mo, norm_ffn, w_gate_up, w_down, norm_final):
    bsz, t, d = x.shape
    depth = norm_mix.shape[0]
    assert d == D_MODEL and t % Q_BLOCK == 0 and t // SEL_BLOCK <= N_BLK_PAD
    assert t // SEL_BLOCK >= N_SELECT and depth == 1 and t % 1024 == 0
    n = bsz * t
    tm = 512
    h = x.reshape(n, d)
    c0, c1, c2, c3 = NSA_W, NSA_W + KV_W, NSA_W + 3 * KV_W, NSA_W + 3 * KV_W + 2 * GMLP_WIDTH
    onehot = (jnp.arange(t)[:, None] // SEL_BLOCK == jnp.arange(N_BLK_PAD)[None, :]).astype(BF16)
    pad_cols = lambda w, width: jnp.pad(w, ((0, 0), (0, width - w.shape[1])))
    for l in range(depth):
        wi = w_in[l]
        wk, wv = [], []
        for branch in range(2):
            base = c1 + KV_W * branch
            for g in range(N_GROUPS):
                wk.append(pad_cols(wi[:, base + HEAD_DIM * g:base + HEAD_DIM * (g + 1)], LANES))
                v0 = base + N_GROUPS * HEAD_DIM + HEAD_DIM * g
                wv.append(pad_cols(wi[:, v0:v0 + HEAD_DIM], VT_ROWS))
        wrow = jnp.concatenate([wi[:, c0:c1]] + wk + [wi[:, c2:c3]], axis=1).astype(BF16)
        wnt = jnp.concatenate([wi[:, :c0]] + wv + [pad_cols(wi[:, c3:], GATET_ROWS)], axis=1).T.astype(BF16)
        (qt, kvc2, kaug, vt, u2, vn2, gatest), (wm_b, wpa_b, wpb_b, wo_b, wmq_b, wmkv_b, wmo_b) = _inproj(
            h, norm_mix[l][None], ln_sgu[l][None], wrow, wnt,
            [w_merge[l], w_proj_a[l], w_proj_b[l], w_out[l], w_mq[l], w_mkv[l], w_mo[l]], 2 * tm, bsz, t)

        pe2 = jnp.stack([pe_cmp_k[l], pe_cmp_v[l]])
        pe2 = jnp.concatenate([pe2, pe2], axis=-1)
        w1 = jnp.stack([w_cmp_k1[l], w_cmp_v1[l]]).reshape(2, CMP_BLOCK, HEAD_DIM, CMP_HIDDEN)
        w1bd = _block_diag2(w1).astype(BF16)
        w2k = _block_diag2(w_cmp_k2[l]).astype(BF16)
        w2vt = _block_diag2(w_cmp_v2[l]).T.astype(BF16)
        kc, vct = _compress(kvc2.reshape(bsz, t, KV_W), pe2, w1bd, w2k, w2vt)
        ocmp, selt, flags = _cmp_topk(qt, kc, vct, gatest)
        counts, lists = _chunk_lists(flags)
        onsa = _slc_win(counts, lists, qt, kaug, vt, onehot, selt, gatest, ocmp)

        h, (wgu_b, wd_b) = _merge(h, onsa.reshape(n, NSA_W), u2, vn2, norm_mix[l][None], wm_b,
                                  b_merge[l][None], w_spatial[l], b_spatial[l].T, wpa_b, wpb_b, wo_b,
                                  [w_gate_up[l], w_down[l]], 2 * tm)

        mkv = _memkv(mem, norm_mem_kv[l][None], wmkv_b)
        h = _xattn(h, norm_mem_q[l][None], wmq_b, mkv, wmo_b, 2 * tm, t)
        h = _ffn(h, norm_ffn[l][None], wgu_b, wd_b, norm_final[None], tm)
    return h.reshape(bsz, t, d)
```

```python
import functools

import jax
import jax.numpy as jnp
from jax import lax
from jax.experimental import pallas as pl
from jax.experimental.pallas import tpu as pltpu

F32 = jnp.float32
BF16 = jnp.bfloat16

LANES = 128
SUBLANES = 8
D_MODEL = 1024
N_HEADS = 8
HEAD_DIM = 64
N_GROUPS = 2
HPG = N_HEADS // N_GROUPS
CMP_BLOCK = 32
CMP_STRIDE = 16
CMP_HIDDEN = 128
SEL_BLOCK = 64
N_SELECT = 16
WINDOW = 512
Q_BLOCK = 256
KEY_CHUNK = 128
N_BLK_PAD = 128
NSA_W = N_HEADS * HEAD_DIM
KV_W = 2 * N_GROUPS * HEAD_DIM
GMLP_WIDTH = 512
GMLP_GROUPS = 4
GMLP_CHUNK = 128
MEM_HEADS = 4
MEM_HEAD_DIM = 128
MEM_W = MEM_HEADS * MEM_HEAD_DIM
NSA_QB = 256
SLC_BATCH = (4, 5)
PEELED_WORDS = 2
N_KEY_CHUNKS = N_BLK_PAD // 2
CHUNK_ID_BITS = 6
VOID_CHUNK = N_KEY_CHUNKS - 1
LIST_WORDS = -(-N_KEY_CHUNKS // min(SLC_BATCH))
VT_ROWS = 80
VCT_ROWS = VT_ROWS + N_BLK_PAD
CMP_KEY_CHUNK = 128
N_FORCED = 3
TOPK_ROW_STEP = 64
CMP_BLOCKS = 2
GATET_ROWS = 32
MASK_BIG = 1e30
EPS = 1e-6
NEG = -1e30
REMOVED = -3e38
SLOPES = tuple(2.0 ** (-8.0 * (h + 1) / N_HEADS) for h in range(N_HEADS))
LOG2E = 1.4426950408889634
Q_SCALE = HEAD_DIM ** -0.5 * LOG2E
VMEM_LIMIT = 56 * 1024 * 1024


def _dot(a, b):
    return jnp.dot(a, b, preferred_element_type=F32)


def _dot_nt(a, b):
    return lax.dot_general(a, b, (((1,), (1,)), ((), ())), preferred_element_type=F32)


def _rms(x, g):
    return x * lax.rsqrt(jnp.mean(x * x, axis=-1, keepdims=True) + EPS) * g


def _iota(shape, dim):
    return lax.broadcasted_iota(jnp.int32, shape, dim)


def _slope_row(g, nq):
    return jnp.concatenate(
        [jnp.full((1, nq), SLOPES[HPG * g + j] * LOG2E, F32) for j in range(HPG)], axis=1)


def _slope_feature_rows(slope_row, n_rows):
    hi = slope_row.astype(BF16).astype(F32)
    r = _iota((n_rows, slope_row.shape[1]), 0)
    return jnp.where(r == 0, hi, jnp.where(r == 1, slope_row - hi, 0.0)).astype(BF16)


def _chunk_slabs(st, chunk):
    if isinstance(st, list):
        return st
    return [st[chunk * u:chunk * (u + 1)] for u in range(st.shape[0] // chunk)]


def _col_max(st, dls, chunk):
    parts = [s.reshape(chunk // SUBLANES, SUBLANES, s.shape[1]).max(axis=0) - dl
             for s, dl in zip(_chunk_slabs(st, chunk), dls)]
    return functools.reduce(jnp.maximum, parts).max(axis=0, keepdims=True)


def _probs(st, dls, m, chunk, exp_dtype=F32):
    return jnp.concatenate([jnp.exp2((s - (m + dl)).astype(exp_dtype)).astype(BF16)
                            for s, dl in zip(_chunk_slabs(st, chunk), dls)], axis=0)


def _cast_plan(weights, n_steps):
    for w in weights:
        assert w.shape[0] % (16 * n_steps) == 0, (w.shape, n_steps)
    specs = [pl.BlockSpec((w.shape[0] // n_steps, w.shape[1]), lambda i: (i, 0)) for w in weights]
    shapes = [jax.ShapeDtypeStruct(w.shape, BF16) for w in weights]
    return specs, shapes


def _cast_blocks(src_refs, dst_refs):
    for src, dst in zip(src_refs, dst_refs):
        dst[...] = src[...].astype(BF16)


def _inproj_kernel(x_ref, g_ref, lng_ref, wrow_ref, wnt_ref, *refs, tm, n_cast):
    cast_in, cast_out = refs[:n_cast], refs[len(refs) - n_cast:]
    qt_ref, kvc_ref, kaug_ref, vt_ref, u_ref, vn_ref, gatest_ref = refs[n_cast:len(refs) - n_cast]
    _cast_blocks(cast_in, cast_out)
    xn = _rms(x_ref[...], g_ref[...]).astype(BF16)
    r = _dot(xn, wrow_ref[...])
    nt = _dot_nt(wnt_ref[...], xn)
    qt_ref[0] = (nt[0:NSA_W] * Q_SCALE).astype(BF16)
    kvc_ref[...] = r[:, 0:KV_W]
    lane = _iota((tm, LANES), 1)
    key_feat = jnp.where((lane == HEAD_DIM) | (lane == HEAD_DIM + 1),
                         _iota((tm, LANES), 0) & (KEY_CHUNK - 1), 0).astype(F32)
    ones_row = jnp.where(_iota((VT_ROWS, tm), 0) == HEAD_DIM, 1.0, 0.0)
    for a in range(2 * N_GROUPS):
        kaug_ref[0, a] = (r[:, KV_W + LANES * a:KV_W + LANES * (a + 1)] + key_feat).astype(BF16)
        vt_ref[0, a] = (nt[NSA_W + VT_ROWS * a:NSA_W + VT_ROWS * (a + 1)] + ones_row).astype(BF16)
    uv = jax.nn.gelu(r[:, KV_W + 2 * N_GROUPS * LANES:])
    u_ref[...] = uv[:, :GMLP_WIDTH]
    v = uv[:, GMLP_WIDTH:]
    vc = v - jnp.mean(v, axis=-1, keepdims=True)
    vn = vc * lax.rsqrt(jnp.mean(vc * vc, axis=-1, keepdims=True) + EPS) * lng_ref[...]
    vn_ref[...] = vn.astype(BF16)
    gatest_ref[0] = jax.nn.sigmoid(nt[NSA_W + 2 * N_GROUPS * VT_ROWS:])


def _inproj(x2, norm_g, ln_g, wrow, wnt, to_cast, tm, bsz, t):
    n = x2.shape[0]
    tpb = t // tm
    row = lambda w: pl.BlockSpec((tm, w), lambda i: (i, 0))
    full = lambda a: pl.BlockSpec(a.shape, lambda i: (0,) * a.ndim)
    cast_specs, cast_shapes = _cast_plan(to_cast, n // tm)
    outs = pl.pallas_call(
        functools.partial(_inproj_kernel, tm=tm, n_cast=len(to_cast)),
        grid=(n // tm,),
        in_specs=[row(D_MODEL), full(norm_g), full(ln_g), full(wrow), full(wnt)] + cast_specs,
        out_specs=[pl.BlockSpec((1, NSA_W, tm), lambda i: (i // tpb, 0, i % tpb)),
                   row(KV_W),
                   pl.BlockSpec((1, 2 * N_GROUPS, tm, LANES), lambda i: (i // tpb, 0, i % tpb, 0)),
                   pl.BlockSpec((1, 2 * N_GROUPS, VT_ROWS, tm), lambda i: (i // tpb, 0, 0, i % tpb)),
                   row(GMLP_WIDTH), row(GMLP_WIDTH),
                   pl.BlockSpec((1, GATET_ROWS, tm), lambda i: (i // tpb, 0, i % tpb))] + cast_specs,
        out_shape=[jax.ShapeDtypeStruct((bsz, NSA_W, t), BF16),
                   jax.ShapeDtypeStruct((n, KV_W), F32),
                   jax.ShapeDtypeStruct((bsz, 2 * N_GROUPS, t, LANES), BF16),
                   jax.ShapeDtypeStruct((bsz, 2 * N_GROUPS, VT_ROWS, t), BF16),
                   jax.ShapeDtypeStruct((n, GMLP_WIDTH), F32),
                   jax.ShapeDtypeStruct((n, GMLP_WIDTH), BF16),
                   jax.ShapeDtypeStruct((bsz, GATET_ROWS, t), F32)] + cast_shapes,
        compiler_params=pltpu.CompilerParams(dimension_semantics=("arbitrary",),
                                             vmem_limit_bytes=VMEM_LIMIT),
        name="inproj",
    )(x2, norm_g, ln_g, wrow, wnt, *to_cast)
    return outs[:7], outs[7:]


def _compress_kernel(xk_ref, xv_ref, pe_ref, w1_ref, w2k_ref, w2vt_ref, kc_ref, vct_ref, *, nc):
    outs = []
    for j, x_ref in enumerate((xk_ref, xv_ref)):
        a = jnp.zeros((nc, 2 * CMP_HIDDEN), F32)
        b = jnp.zeros((nc, 2 * CMP_HIDDEN), F32)
        for t in range(0, CMP_STRIDE, 2):
            xs = [x_ref[0, pl.ds(t + e, nc, stride=CMP_STRIDE), :] for e in range(2)]

            def half_block(first):
                lhs = jnp.concatenate([(xs[e] + pe_ref[j, first + t + e:first + t + e + 1, :]).astype(BF16)
                                       for e in range(2)], axis=1)
                rhs = jnp.concatenate([w1_ref[j, first + t], w1_ref[j, first + t + 1]], axis=0)
                return _dot(lhs, rhs)

            a = a + half_block(0)
            b = b + half_block(CMP_STRIDE)
        outs.append(jax.nn.gelu(a + pltpu.roll(b, nc - 1, 0)).astype(BF16))
    k2 = _dot(outs[0], w2k_ref[...])
    v_t = _dot_nt(w2vt_ref[...], outs[1])
    lane = _iota((nc, LANES), 1)
    key_feat = jnp.where((lane == HEAD_DIM) | (lane == HEAD_DIM + 1),
                         CMP_STRIDE * (_iota((nc, LANES), 0) & (CMP_KEY_CHUNK - 1)), 0).astype(F32)
    ci =_iota((N_BLK_PAD, nc), 1) * CMP_STRIDE
    sj = _iota((N_BLK_PAD, nc), 0) * SEL_BLOCK
    overlap_t = jnp.where((ci < sj + SEL_BLOCK) & (ci + (CMP_BLOCK - 1) >= sj), 1.0, 0.0).astype(BF16)
    ones_rows = jnp.where(_iota((VT_ROWS - HEAD_DIM, nc), 0) == 0, 1.0, 0.0).astype(BF16)
    for g in range(N_GROUPS):
        kg = k2 if g == 0 else pltpu.roll(k2, HEAD_DIM, 1)
        kc_ref[0, g] = jnp.where(lane < HEAD_DIM, kg, key_feat).astype(BF16)
        vct_ref[0, g, 0:HEAD_DIM, :] = v_t[HEAD_DIM * g:HEAD_DIM * (g + 1), :].astype(BF16)
        vct_ref[0, g, HEAD_DIM:VT_ROWS, :] = ones_rows
        vct_ref[0, g, VT_ROWS:VCT_ROWS, :] = overlap_t


def _compress(kvc3, pe2, w1bd, w2k, w2vt):
    bsz, t, _ = kvc3.shape
    nc = t // CMP_STRIDE
    full = lambda a: pl.BlockSpec(a.shape, lambda b: (0,) * a.ndim)
    return pl.pallas_call(
        functools.partial(_compress_kernel, nc=nc),
        grid=(bsz,),
        in_specs=[pl.BlockSpec((1, t, LANES), lambda b: (b, 0, 0)),
                  pl.BlockSpec((1, t, LANES), lambda b: (b, 0, 1)),
                  full(pe2), full(w1bd), full(w2k), full(w2vt)],
        out_specs=[pl.BlockSpec((1, N_GROUPS, nc, LANES), lambda b: (b, 0, 0, 0)),
                   pl.BlockSpec((1, N_GROUPS, VCT_ROWS, nc), lambda b: (b, 0, 0, 0))],
        out_shape=[jax.ShapeDtypeStruct((bsz, N_GROUPS, nc, LANES), BF16),
                   jax.ShapeDtypeStruct((bsz, N_GROUPS, VCT_ROWS, nc), BF16)],
        compiler_params=pltpu.CompilerParams(dimension_semantics=("arbitrary",),
                                             vmem_limit_bytes=VMEM_LIMIT),
        name="compress",
    )(kvc3, kvc3, pe2, w1bd, w2k, w2vt)


def _cmp_topk_kernel(qt_ref, kc_ref, vct_ref, gatest_ref, ocmp_ref, selt_ref, flags_ref,
                     m_ref, acc_ref, *, ncp):
    step = pl.program_id(1)
    n_chunks = ncp // CMP_KEY_CHUNK
    chunk_tokens = CMP_KEY_CHUNK * CMP_STRIDE
    tile_heads = lambda a: jnp.concatenate([a] * HPG, axis=1)
    tail_chunks = min(2, n_chunks)
    head_chunks = n_chunks - tail_chunks
    tail_keys = tail_chunks * CMP_KEY_CHUNK
    blocks = range(CMP_BLOCKS)
    groups = range(N_GROUPS)
    items = [(bi, g) for bi in blocks for g in groups]
    start = [(step * CMP_BLOCKS + bi) * Q_BLOCK for bi in blocks]
    qlanes = [slice(Q_BLOCK * bi, Q_BLOCK * (bi + 1)) for bi in blocks]
    nck = [(start[bi] + Q_BLOCK - CMP_BLOCK) // CMP_STRIDE // CMP_KEY_CHUNK + 1 for bi in blocks]
    tail_c0 = [jnp.maximum(nck[bi] - tail_chunks, 0) for bi in blocks]
    tail_rows = [pl.ds(pl.multiple_of(tail_c0[bi] * CMP_KEY_CHUNK, CMP_KEY_CHUNK), tail_keys)
                 for bi in blocks]
    gt = gatest_ref[0]
    slope_rows = [_slope_row(g, Q_BLOCK) for g in groups]
    qas = {(bi, g): jnp.concatenate(
        [jnp.concatenate([qt_ref[0, HEAD_DIM * (HPG * g + j):HEAD_DIM * (HPG * g + j + 1), qlanes[bi]]
                          for j in range(HPG)], axis=1),
         _slope_feature_rows(slope_rows[g], LANES - HEAD_DIM)], axis=0) for bi, g in items}

    def delta(bi, g, c):
        return slope_rows[g] * (start[bi] - c * chunk_tokens).astype(F32)

    for bi in blocks:
        if head_chunks > 0:
            @pl.when(nck[bi] > tail_chunks)
            def _():
                sts = [_dot(kc_ref[0, g, 0:head_chunks * CMP_KEY_CHUNK, :], qas[bi, g]) for g in groups]
                dls = [[delta(bi, g, c) + jnp.where(c < nck[bi] - tail_chunks, 0.0, MASK_BIG)
                        for c in range(head_chunks)] for g in groups]
                ms = [_col_max(sts[g], dls[g], CMP_KEY_CHUNK) for g in groups]
                ps = [_probs(sts[g], dls[g], ms[g], CMP_KEY_CHUNK) for g in groups]
                for g in groups:
                    m_ref[bi, g] = ms[g]
                    acc_ref[bi, g] = _dot(vct_ref[0, g, :, 0:head_chunks * CMP_KEY_CHUNK], ps[g])

            @pl.when(nck[bi] <= tail_chunks)
            def _():
                m_ref[bi] = jnp.full(m_ref.shape[1:], NEG, F32)
                acc_ref[bi] = jnp.zeros(acc_ref.shape[1:], F32)

    def tail_and_select(n_rows):
        if head_chunks > 0:
            m_old = {it: m_ref[it[0], it[1]] for it in items}
            acc_old = {it: acc_ref[it[0], it[1]] for it in items}
        else:
            m_old = {it: jnp.full((1, HPG * Q_BLOCK), NEG, F32) for it in items}
            acc_old = {it: jnp.zeros((VCT_ROWS, HPG * Q_BLOCK), F32) for it in items}
        key_row = _iota((tail_keys, Q_BLOCK), 0)
        q_lane = _iota((tail_keys, Q_BLOCK), 1)
        tail_bias = []
        for bi in blocks:
            key_end = CMP_STRIDE * (tail_c0[bi] * CMP_KEY_CHUNK + key_row) + (CMP_BLOCK - 1) - start[bi]
            tail_bias.append(tile_heads(jnp.where(key_end <= q_lane, 0.0, NEG)))
        sts = {(bi, g): _dot(kc_ref[0, g, tail_rows[bi], :], qas[bi, g]) + tail_bias[bi] for bi, g in items}
        dls = {(bi, g): [delta(bi, g, tail_c0[bi] + u) for u in range(tail_chunks)] for bi, g in items}
        m_new = {it: jnp.maximum(m_old[it], _col_max(sts[it], dls[it], CMP_KEY_CHUNK)) for it in items}
        ps = {it: _probs(sts[it], dls[it], m_new[it], CMP_KEY_CHUNK) for it in items}
        accs = {(bi, g): jnp.exp2(m_old[bi, g] - m_new[bi, g]) * acc_old[bi, g]
                + _dot(vct_ref[0, g, :, tail_rows[bi]], ps[bi, g]) for bi, g in items}

        blk_n = _iota((n_rows, Q_BLOCK), 0)
        bf = blk_n.astype(F32)
        rk, cur = {}, []
        for bi in blocks:
            t_row = start[bi] + _iota((1, Q_BLOCK), 1)
            cur.append(lax.shift_right_logical(t_row, SEL_BLOCK.bit_length() - 1))
            forced = (blk_n == 0) | (blk_n == cur[bi]) | (blk_n == cur[bi] - 1)
            has_key = t_row >= CMP_BLOCK - 1
            per_head = []
            for g in groups:
                acc = accs[bi, g]
                inv_l = 1.0 / jnp.maximum(acc[HEAD_DIM:HEAD_DIM + 1], 1e-30)
                o = acc[0:HEAD_DIM] * inv_l
                imp_h = acc[VT_ROWS:VT_ROWS + n_rows] * inv_l
                imp = sum(imp_h[:, Q_BLOCK * j:Q_BLOCK * (j + 1)] for j in range(HPG))
                imp = jnp.where(has_key, imp, 0.0)
                for j in range(HPG):
                    h = HPG * g + j
                    per_head.append(jnp.where(
                        has_key, gt[3 * h:3 * h + 1, qlanes[bi]] * o[:, Q_BLOCK * j:Q_BLOCK * (j + 1)], 0.0))
                rk[bi, g] = jnp.where(blk_n <= cur[bi], jnp.where(forced, REMOVED, imp), NEG)
            ocmp_ref[0, :, qlanes[bi]] = jnp.concatenate(per_head, axis=0)

        for _ in range(N_SELECT - N_FORCED):
            for it in items:
                m = jnp.max(rk[it], axis=0, keepdims=True)
                idx = jnp.min(jnp.where(rk[it] == m, bf, float(N_BLK_PAD)), axis=0, keepdims=True)
                rk[it] = jnp.where(bf == idx, REMOVED, rk[it])
        ones8 = jnp.ones((SUBLANES, Q_BLOCK), F32)
        for bi, g in items:
            sel = jnp.where((blk_n <= cur[bi]) & (rk[bi, g] < 2.0 * NEG), 1.0, 0.0)
            selt_ref[0, g, 0:n_rows, qlanes[bi]] = sel.astype(BF16)
            flag = (_dot_nt(ones8, sel) > 0.5).astype(jnp.int32)
            if n_rows < N_BLK_PAD:
                selt_ref[0, g, n_rows:N_BLK_PAD, qlanes[bi]] = jnp.zeros((N_BLK_PAD - n_rows, Q_BLOCK), BF16)
                flag = jnp.concatenate([flag, jnp.zeros((SUBLANES, N_BLK_PAD - n_rows), jnp.int32)], axis=1)
            flags_ref[0, bi, g] = flag

    causal_rows = (start[-1] + Q_BLOCK) // SEL_BLOCK
    for n_rows in range(TOPK_ROW_STEP, N_BLK_PAD + 1, TOPK_ROW_STEP):
        @pl.when((causal_rows > n_rows - TOPK_ROW_STEP) & (causal_rows <= n_rows))
        def _():
            tail_and_select(n_rows)


def _cmp_topk(qt, kc, vct, gatest):
    bsz, _, t = qt.shape
    ncp = t // CMP_STRIDE
    nqb = t // Q_BLOCK
    step_q = CMP_BLOCKS * Q_BLOCK
    return pl.pallas_call(
        functools.partial(_cmp_topk_kernel, ncp=ncp),
        grid=(bsz, nqb // CMP_BLOCKS),
        in_specs=[pl.BlockSpec((1, NSA_W, step_q), lambda b, i: (b, 0, i)),
                  pl.BlockSpec((1, N_GROUPS, ncp, LANES), lambda b, i: (b, 0, 0, 0)),
                  pl.BlockSpec((1, N_GROUPS, VCT_ROWS, ncp), lambda b, i: (b, 0, 0, 0)),
                  pl.BlockSpec((1, GATET_ROWS, step_q), lambda b, i: (b, 0, i))],
        out_specs=[pl.BlockSpec((1, NSA_W, step_q), lambda b, i: (b, 0, i)),
                   pl.BlockSpec((1, N_GROUPS, N_BLK_PAD, step_q), lambda b, i: (b, 0, 0, i)),
                   pl.BlockSpec((1, CMP_BLOCKS, N_GROUPS, SUBLANES, N_BLK_PAD), lambda b, i: (b, i, 0, 0, 0))],
        out_shape=[jax.ShapeDtypeStruct((bsz, NSA_W, t), F32),
                   jax.ShapeDtypeStruct((bsz, N_GROUPS, N_BLK_PAD, t), BF16),
                   jax.ShapeDtypeStruct((bsz, nqb, N_GROUPS, SUBLANES, N_BLK_PAD), jnp.int32)],
        scratch_shapes=[pltpu.VMEM((CMP_BLOCKS, N_GROUPS, 1, HPG * Q_BLOCK), F32),
                        pltpu.VMEM((CMP_BLOCKS, N_GROUPS, VCT_ROWS, HPG * Q_BLOCK), F32)],
        compiler_params=pltpu.CompilerParams(dimension_semantics=("arbitrary", "arbitrary"),
                                             vmem_limit_bytes=VMEM_LIMIT),
        name="cmp_topk",
    )(qt, kc, vct, gatest)


def _slc_win_kernel(counts_ref, lists_ref, qt_ref, kaug_ref, vt_ref, oh_ref, selt_ref, gatest_ref,
                    ocmp_ref, out_ref, qaug_ref, m_ref, acc_ref, *, nqb):
    b = pl.program_id(0)
    qb = pl.program_id(1)
    step_id = b * nqb + qb
    start = qb * NSA_QB
    tile_heads = lambda a: jnp.concatenate([a] * HPG, axis=1)
    groups = range(N_GROUPS)
    gt = gatest_ref[0]
    slope_rows = [_slope_row(g, NSA_QB) for g in groups]

    def normalize(acc):
        return acc[0:HEAD_DIM] / jnp.maximum(acc[HEAD_DIM:HEAD_DIM + 1], 1e-30)

    def half_lanes(w):
        return [slice(NSA_QB * h + KEY_CHUNK * w, NSA_QB * h + KEY_CHUNK * (w + 1)) for h in range(HPG)]

    def half(a, w):
        return jnp.concatenate([a[:, s] for s in half_lanes(w)], axis=1)

    def unhalf(lo, hi):
        return jnp.concatenate([x[:, KEY_CHUNK * h:KEY_CHUNK * (h + 1)]
                                for h in range(HPG) for x in (lo, hi)], axis=1)

    def chunk_at(rel):
        pos = start + rel * KEY_CHUNK
        rows = pl.ds(pl.multiple_of(jnp.maximum(pos, 0), KEY_CHUNK), KEY_CHUNK)
        return rows, (jnp.where(pos >= 0, 0.0, MASK_BIG) if rel < 0 else 0.0)

    ki = _iota((KEY_CHUNK, KEY_CHUNK), 0)
    qi = _iota((KEY_CHUNK, KEY_CHUNK), 1)
    upper_bias = tile_heads(jnp.where(ki > qi, 0.0, NEG))
    lower_bias = tile_heads(jnp.where(ki <= qi, 0.0, NEG))
    n_mid = WINDOW // KEY_CHUNK
    no_bias = jnp.zeros((KEY_CHUNK, KEY_CHUNK), F32)
    mid_first_bias = tile_heads(jnp.concatenate([no_bias, jnp.where(ki > qi, 0.0, NEG)], axis=1))
    mid_last_bias = tile_heads(jnp.concatenate([jnp.where(ki <= qi, 0.0, NEG), no_bias], axis=1))
    own_bias = tile_heads(jnp.where(_iota((KEY_CHUNK, NSA_QB), 0) <= _iota((KEY_CHUNK, NSA_QB), 1), 0.0, NEG))

    for g in range(N_GROUPS):
        qaug_ref[g, 0:HEAD_DIM, :] = jnp.concatenate(
            [qt_ref[0, HEAD_DIM * (HPG * g + j):HEAD_DIM * (HPG * g + j + 1), :] for j in range(HPG)],
            axis=1)
        qaug_ref[g, HEAD_DIM:LANES, :] = _slope_feature_rows(slope_rows[g], LANES - HEAD_DIM)
        sel_bias = ((selt_ref[0, g].astype(F32) - 1.0) * MASK_BIG).astype(BF16)
        qaug_ref[g, LANES:2 * LANES, :] = tile_heads(sel_bias)

    mid = [chunk_at(r) for r in range(1 - n_mid, 1)]
    lo_rows, lo_kill = chunk_at(-n_mid)
    hi_rows, _ = chunk_at(1)
    own_rows, _ = chunk_at(0)
    kw = lambda g: kaug_ref.at[0, N_GROUPS + g]
    vw = lambda g: vt_ref.at[0, N_GROUPS + g]
    slc_keys = lambda g, rows: jnp.concatenate([kaug_ref[0, g, rows, :], oh_ref[rows, :]], axis=1)
    q_half = lambda g, w, nrow: jnp.concatenate([qaug_ref[g, 0:nrow, s] for s in half_lanes(w)], axis=1)

    def slc_scores_group(g, i, n_words=1, with_own=False):
        words = [lists_ref[(step_id * N_GROUPS + g) * LIST_WORDS + i + w] for w in range(n_words)]
        ks, vs, dls = [], [], []
        if with_own:
            ks.append(slc_keys(g, own_rows))
            vs.append(vt_ref[0, g, :, own_rows])
            dls.append(0.0)
        for word, u in [(word, u) for word in words for u in range(SLC_BATCH[g])]:
            cid = lax.shift_right_logical(word, CHUNK_ID_BITS * u) & VOID_CHUNK
            valid = cid < VOID_CHUNK
            c = jnp.where(valid, cid, 0)
            rows = pl.ds(pl.multiple_of(c * KEY_CHUNK, KEY_CHUNK), KEY_CHUNK)
            ks.append(jnp.concatenate([kaug_ref[0, g, rows, :], oh_ref[rows, :]], axis=1))
            vs.append(vt_ref[0, g, :, rows])
            dls.append(slope_rows[g] * (start - c * KEY_CHUNK).astype(F32)
                       + jnp.where(valid, 0.0, MASK_BIG))
        return _dot(jnp.concatenate(ks, axis=0), qaug_ref[g]), jnp.concatenate(vs, axis=1), dls

    def slc_scores(i):
        return zip(*[slc_scores_group(g, i) for g in range(N_GROUPS)])

    probs = functools.partial(_probs, chunk=KEY_CHUNK, exp_dtype=BF16)
    wmid_st = [_dot(jnp.concatenate([kw(g)[rows, :] for rows, _ in mid], axis=0), qaug_ref[g, 0:LANES, :])
               for g in groups]
    wmid_st = [[st[0:KEY_CHUNK] + mid_first_bias] + _chunk_slabs(st[KEY_CHUNK:(n_mid - 1) * KEY_CHUNK], KEY_CHUNK)
               + [st[(n_mid - 1) * KEY_CHUNK:] + mid_last_bias] for st in wmid_st]
    wlo_st = [_dot(kw(g)[lo_rows, :], q_half(g, 0, LANES)) + upper_bias for g in groups]
    whi_st = [_dot(kw(g)[hi_rows, :], q_half(g, 1, LANES)) + lower_bias for g in groups]
    ohi_st = [_dot(slc_keys(g, hi_rows), q_half(g, 1, 2 * LANES)) + lower_bias for g in groups]
    st0, vs0, dls0 = zip(*[slc_scores_group(g, 0, PEELED_WORDS, with_own=True) for g in groups])
    st0 = [[st0[g][0:KEY_CHUNK] + own_bias] + _chunk_slabs(st0[g][KEY_CHUNK:], KEY_CHUNK) for g in groups]

    wmid_dls = [[slope_rows[g] * float(-KEY_CHUNK * r) + kill
                 for r, (_, kill) in zip(range(1 - n_mid, 1), mid)] for g in groups]
    wlo_dl = [half(slope_rows[g], 0) * float(KEY_CHUNK * n_mid) + lo_kill for g in groups]
    hi_dl = [half(slope_rows[g], 1) * float(-KEY_CHUNK) for g in groups]
    win_m = [jnp.maximum(_col_max(wmid_st[g], wmid_dls[g], KEY_CHUNK),
                         unhalf(_col_max(wlo_st[g], [wlo_dl[g]], KEY_CHUNK),
                                _col_max(whi_st[g], [hi_dl[g]], KEY_CHUNK))) for g in groups]
    wmid_p = [probs(wmid_st[g], wmid_dls[g], win_m[g]) for g in groups]
    wlo_p = [probs(wlo_st[g], [wlo_dl[g]], half(win_m[g], 0)) for g in groups]
    whi_p = [probs(whi_st[g], [hi_dl[g]], half(win_m[g], 1)) for g in groups]
    neg_half = jnp.full((1, HPG * KEY_CHUNK), NEG, F32)
    own_m = [jnp.maximum(_col_max(st0[g], dls0[g], KEY_CHUNK),
                         unhalf(neg_half, _col_max(ohi_st[g], [hi_dl[g]], KEY_CHUNK))) for g in groups]
    ohi_p = [probs(ohi_st[g], [hi_dl[g]], half(own_m[g], 1)) for g in groups]
    p0 = [probs(st0[g], dls0[g], own_m[g]) for g in groups]
    o_win = []
    for g in groups:
        acc = _dot(jnp.concatenate([vw(g)[:, rows] for rows, _ in mid], axis=1), wmid_p[g])
        acc = acc + unhalf(_dot(vw(g)[:, lo_rows], wlo_p[g]), _dot(vw(g)[:, hi_rows], whi_p[g]))
        o_win.append(normalize(acc))
    zero_half = jnp.zeros((VT_ROWS, HPG * KEY_CHUNK), F32)
    for g in groups:
        m_ref[g] = own_m[g]
        acc_ref[g] = (_dot(vs0[g], p0[g])
                      + unhalf(zero_half, _dot(vt_ref[0, g, :, hi_rows], ohi_p[g])))

    def slc_body(i, carry):
        sts, vss, dlss = slc_scores(i)
        m_old = [m_ref[g] for g in range(N_GROUPS)]
        m_new = [jnp.maximum(m_old[g], _col_max(sts[g], dlss[g], KEY_CHUNK)) for g in range(N_GROUPS)]
        ps = [probs(sts[g], dlss[g], m_new[g]) for g in range(N_GROUPS)]
        for g in range(N_GROUPS):
            acc_ref[g] = jnp.exp2(m_old[g] - m_new[g]) * acc_ref[g] + _dot(vss[g], ps[g])
            m_ref[g] = m_new[g]
        return carry

    lax.fori_loop(PEELED_WORDS, counts_ref[step_id], slc_body, 0)

    per_head = []
    for g in range(N_GROUPS):
        o_slc = normalize(acc_ref[g])
        for j in range(HPG):
            h = HPG * g + j
            lanes = slice(NSA_QB * j, NSA_QB * (j + 1))
            per_head.append(gt[3 * h + 1:3 * h + 2, :] * o_slc[:, lanes]
                            + gt[3 * h + 2:3 * h + 3, :] * o_win[g][:, lanes])
    o_t = jnp.concatenate(per_head, axis=0)
    out_ref[0] = (ocmp_ref[0] + o_t).T.astype(BF16)


def _slc_win(counts, lists, qt, kaug, vt, onehot, selt, gatest, ocmp):
    bsz, _, t = qt.shape
    nqb = t // NSA_QB
    once = lambda shape, imap: pl.BlockSpec(shape, imap, pipeline_mode=pl.Buffered(1))
    grid_spec = pltpu.PrefetchScalarGridSpec(
        num_scalar_prefetch=2,
        grid=(bsz, nqb),
        in_specs=[pl.BlockSpec((1, NSA_W, NSA_QB), lambda b, i, *_: (b, 0, i)),
                  pl.BlockSpec((1, 2 * N_GROUPS, t, LANES), lambda b, i, *_: (b, 0, 0, 0)),
                  pl.BlockSpec((1, 2 * N_GROUPS, VT_ROWS, t), lambda b, i, *_: (b, 0, 0, 0)),
                  once((t, N_BLK_PAD), lambda b, i, *_: (0, 0)),
                  pl.BlockSpec((1, N_GROUPS, N_BLK_PAD, NSA_QB), lambda b, i, *_: (b, 0, 0, i)),
                  pl.BlockSpec((1, GATET_ROWS, NSA_QB), lambda b, i, *_: (b, 0, i)),
                  pl.BlockSpec((1, NSA_W, NSA_QB), lambda b, i, *_: (b, 0, i))],
        out_specs=pl.BlockSpec((1, NSA_QB, NSA_W), lambda b, i, *_: (b, i, 0)),
        scratch_shapes=[pltpu.VMEM((N_GROUPS, 2 * LANES, HPG * NSA_QB), BF16),
                        pltpu.VMEM((N_GROUPS, 1, HPG * NSA_QB), F32),
                        pltpu.VMEM((N_GROUPS, VT_ROWS, HPG * NSA_QB), F32)],
    )
    return pl.pallas_call(
        functools.partial(_slc_win_kernel, nqb=nqb),
        grid_spec=grid_spec,
        out_shape=jax.ShapeDtypeStruct((bsz, t, NSA_W), BF16),
        compiler_params=pltpu.CompilerParams(dimension_semantics=("arbitrary", "arbitrary"),
                                             vmem_limit_bytes=VMEM_LIMIT),
        name="slc_win",
    )(counts, lists, qt, kaug, vt, onehot, selt, gatest, ocmp)


def _merge_kernel(x_ref, onsa_ref, u_ref, vn_ref, g_ref, wm_ref, bm_ref, ws_ref, bs_ref,
                  wpa_ref, wpb_ref, wo_ref, *refs, tm, n_cast):
    cast_in, h_ref, cast_out = refs[:n_cast], refs[n_cast], refs[n_cast + 1:]
    _cast_blocks(cast_in, cast_out)
    x = x_ref[...]
    xn = _rms(x, g_ref[...]).astype(BF16)
    tril = _iota((GMLP_CHUNK, GMLP_CHUNK), 0) >= _iota((GMLP_CHUNK, GMLP_CHUNK), 1)
    sgu_rows = []
    for c in range(tm // GMLP_CHUNK):
        rows = slice(GMLP_CHUNK * c, GMLP_CHUNK * (c + 1))
        cols = []
        for g in range(GMLP_GROUPS):
            lanes = slice(LANES * g, LANES * (g + 1))
            w = jnp.where(tril, ws_ref[g], 0.0).astype(BF16)
            cols.append(_dot(w, vn_ref[rows, lanes]) + bs_ref[:, g:g + 1])
        sgu_rows.append(u_ref[rows, :] * jnp.concatenate(cols, axis=1))
    o_sgu = jnp.concatenate(sgu_rows, axis=0).astype(BF16)
    mg = jax.nn.sigmoid(_dot(xn, wm_ref[...]) + bm_ref[...])
    mixed = (mg[:, :D_MODEL] * _dot(onsa_ref[...], wpa_ref[...])
             + mg[:, D_MODEL:] * _dot(o_sgu, wpb_ref[...]))
    h_ref[...] = x + _dot(mixed.astype(BF16), wo_ref[...])


def _merge(x2, onsa2, u2, vn2, norm_g, wm, bm, ws, bs_t, wpa, wpb, wo, to_cast, tm):
    n = x2.shape[0]
    row = lambda w: pl.BlockSpec((tm, w), lambda i: (i, 0))
    full = lambda a: pl.BlockSpec(a.shape, lambda i: (0,) * a.ndim)
    cast_specs, cast_shapes = _cast_plan(to_cast, n // tm)
    outs = pl.pallas_call(
        functools.partial(_merge_kernel, tm=tm, n_cast=len(to_cast)),
        grid=(n // tm,),
        in_specs=[row(D_MODEL), row(NSA_W), row(GMLP_WIDTH), row(GMLP_WIDTH), full(norm_g),
                  full(wm), full(bm), full(ws), full(bs_t), full(wpa), full(wpb), full(wo)] + cast_specs,
        out_specs=[row(D_MODEL)] + cast_specs,
        out_shape=[jax.ShapeDtypeStruct((n, D_MODEL), F32)] + cast_shapes,
        compiler_params=pltpu.CompilerParams(dimension_semantics=("arbitrary",),
                                             vmem_limit_bytes=VMEM_LIMIT),
        name="merge",
    )(x2, onsa2, u2, vn2, norm_g, wm, bm, ws, bs_t, wpa, wpb, wo, *to_cast)
    return outs[0], outs[1:]


def _memkv_kernel(mem_ref, g_ref, w_ref, out_ref):
    out_ref[0] = _dot(_rms(mem_ref[0], g_ref[...]).astype(BF16), w_ref[...]).astype(BF16)


def _memkv(mem, norm_g, w):
    bsz, nm, _ = mem.shape
    return pl.pallas_call(
        _memkv_kernel,
        grid=(bsz,),
        in_specs=[pl.BlockSpec((1, nm, D_MODEL), lambda b: (b, 0, 0)),
                  pl.BlockSpec(norm_g.shape, lambda b: (0, 0)),
                  pl.BlockSpec(w.shape, lambda b: (0, 0))],
        out_specs=pl.BlockSpec((1, nm, 2 * MEM_W), lambda b: (b, 0, 0)),
        out_shape=jax.ShapeDtypeStruct((bsz, nm, 2 * MEM_W), BF16),
        compiler_params=pltpu.CompilerParams(dimension_semantics=("arbitrary",),
                                             vmem_limit_bytes=VMEM_LIMIT),
        name="memkv",
    )(mem, norm_g, w)


def _xattn_kernel(h_ref, g_ref, wq_ref, mkv_ref, wo_ref, out_ref, *, tm):
    halves = [slice(0, tm // 2), slice(tm // 2, tm)]
    head_lanes = [slice(MEM_HEAD_DIM * a, MEM_HEAD_DIM * (a + 1)) for a in range(MEM_HEADS)]
    hs = [h_ref[rows, :] for rows in halves]
    hqs = [(_dot(_rms(h, g_ref[...]).astype(BF16), wq_ref[...]) * (MEM_HEAD_DIM ** -0.5 * LOG2E)).astype(BF16)
           for h in hs]
    ss = [[_dot_nt(hq[:, lanes], mkv_ref[0, :, lanes]) for lanes in head_lanes] for hq in hqs]
    es = [[jnp.exp2((s - jnp.max(s, axis=-1, keepdims=True)).astype(BF16)) for s in s_half] for s_half in ss]
    os = []
    for e_half in es:
        heads = []
        for a, e in enumerate(e_half):
            v = mkv_ref[0, :, MEM_W + MEM_HEAD_DIM * a:MEM_W + MEM_HEAD_DIM * (a + 1)]
            heads.append(_dot(e, v) * (1.0 / jnp.sum(e.astype(F32), axis=-1, keepdims=True)))
        os.append(jnp.concatenate(heads, axis=1).astype(BF16))
    for rows, h, o in zip(halves, hs, os):
        out_ref[rows, :] = h + _dot(o, wo_ref[...])


def _xattn(h2d, norm_g, wq, mkv, wo, tm, rows_per_batch):
    n = h2d.shape[0]
    nm = mkv.shape[1]
    tiles_per_batch = rows_per_batch // tm
    full = lambda a: pl.BlockSpec(a.shape, lambda i: (0,) * a.ndim)
    return pl.pallas_call(
        functools.partial(_xattn_kernel, tm=tm),
        grid=(n // tm,),
        in_specs=[pl.BlockSpec((tm, D_MODEL), lambda i: (i, 0)), full(norm_g), full(wq),
                  pl.BlockSpec((1, nm, 2 * MEM_W), lambda i: (i // tiles_per_batch, 0, 0)),
                  full(wo)],
        out_specs=pl.BlockSpec((tm, D_MODEL), lambda i: (i, 0)),
        out_shape=jax.ShapeDtypeStruct((n, D_MODEL), F32),
        compiler_params=pltpu.CompilerParams(dimension_semantics=("arbitrary",),
                                             vmem_limit_bytes=VMEM_LIMIT),
        name="xattn",
    )(h2d, norm_g, wq, mkv, wo)


def _ffn_kernel(h_ref, g_ref, wgu_ref, wd_ref, gf_ref, out_ref, *, d_ff):
    tm = h_ref.shape[0]
    halves = [slice(0, tm // 2), slice(tm // 2, tm)]
    hs = [h_ref[rows, :] for rows in halves]
    hns = [_rms(h, g_ref[...]).astype(BF16) for h in hs]
    gates = [_dot(hn, wgu_ref[:, :d_ff]) for hn in hns]
    ups = [_dot(hn, wgu_ref[:, d_ff:]) for hn in hns]
    acts = [(jax.nn.silu(gate) * up).astype(BF16) for gate, up in zip(gates, ups)]
    for rows, h, act in zip(halves, hs, acts):
        out_ref[rows, :] = _rms(h + _dot(act, wd_ref[...]), gf_ref[...])


def _ffn(h2d, norm_g, wgu, wd, norm_f, tm):
    n = h2d.shape[0]
    d_ff = wd.shape[0]
    full = lambda a: pl.BlockSpec(a.shape, lambda i: (0,) * a.ndim)
    once = lambda a: pl.BlockSpec(a.shape, lambda i: (0,) * a.ndim, pipeline_mode=pl.Buffered(1))
    return pl.pallas_call(
        functools.partial(_ffn_kernel, d_ff=d_ff),
        grid=(n // tm,),
        in_specs=[pl.BlockSpec((tm, D_MODEL), lambda i: (i, 0)), full(norm_g), once(wgu), once(wd),
                  full(norm_f)],
        out_specs=pl.BlockSpec((tm, D_MODEL), lambda i: (i, 0)),
        out_shape=jax.ShapeDtypeStruct((n, D_MODEL), F32),
        compiler_params=pltpu.CompilerParams(dimension_semantics=("arbitrary",),
                                             vmem_limit_bytes=VMEM_LIMIT),
        name="ffn",
    )(h2d, norm_g, wgu, wd, norm_f)


def _block_diag2(w):
    z = jnp.zeros_like(w)
    return jnp.concatenate([jnp.concatenate([w, z], axis=-1), jnp.concatenate([z, w], axis=-1)], axis=-2)


def _chunk_lists(flags):
    bsz = flags.shape[0]
    n_chunks = N_KEY_CHUNKS
    per_step = NSA_QB // Q_BLOCK
    nqb = flags.shape[1] // per_step
    f = flags[:, :, :, 0, :].reshape(bsz, nqb, per_step, N_GROUPS, n_chunks, 2).max(axis=(2, 5))
    cid = jnp.arange(n_chunks, dtype=jnp.int32)
    own = (NSA_QB // KEY_CHUNK) * jnp.arange(nqb, dtype=jnp.int32)[None, :, None, None]
    active = (f > 0) & (cid < own)
    n_active = active.sum(axis=-1)
    slot = jnp.cumsum(active, axis=-1) - 1
    hit = active[..., :, None] & (slot[..., :, None] == cid)
    ids = jnp.sum(jnp.where(hit, cid[:, None], 0), axis=-2)
    ids = jnp.where(cid < n_active[..., None], ids, VOID_CHUNK)
    words, n_batches = [], []
    for g, batch in enumerate(SLC_BATCH):
        n_batches.append((n_active[:, :, g] + batch - 1) // batch)
        padded = jnp.pad(ids[:, :, g], ((0, 0), (0, 0), (0, LIST_WORDS * batch - n_chunks)),
                         constant_values=VOID_CHUNK).reshape(bsz, nqb, LIST_WORDS, batch)
        words.append(functools.reduce(jnp.bitwise_or,
                                      [padded[..., u] << (CHUNK_ID_BITS * u) for u in range(batch)]))
    n_batches = functools.reduce(jnp.maximum, n_batches)
    words = jnp.stack(words, axis=2)
    return n_batches.reshape(-1).astype(jnp.int32), words.reshape(-1).astype(jnp.int32)


def kernel(x, mem, norm_mix, w_in, w_cmp_k1, w_cmp_k2, w_cmp_v1, w_cmp_v2, pe_cmp_k, pe_cmp_v, ln_sgu, w_spatial, b_spatial, w_proj_a, w_proj_b, w_merge, b_merge, w_out, norm_mem_q, norm_mem_kv, w_mq, w_mkv, w_mo, norm_ffn, w_gate_up, w_down, norm_final):
    bsz, t, d = x.shape
    depth = norm_mix.shape[0]
    assert d == D_MODEL and t % Q_BLOCK == 0 and t // SEL_BLOCK <= N_BLK_PAD
    assert t // SEL_BLOCK >= N_SELECT and depth == 1 and t % 1024 == 0
    n = bsz * t
    tm = 512
    h = x.reshape(n, d)
    c0, c1, c2, c3 = NSA_W, NSA_W + KV_W, NSA_W + 3 * KV_W, NSA_W + 3 * KV_W + 2 * GMLP_WIDTH
    onehot = (jnp.arange(t)[:, None] // SEL_BLOCK == jnp.arange(N_BLK_PAD)[None, :]).astype(BF16)
    pad_cols = lambda w, width: jnp.pad(w, ((0, 0), (0, width - w.shape[1])))
    for l in range(depth):
        wi = w_in[l]
        wk, wv = [], []
        for branch in range(2):
            base = c1 + KV_W * branch
            for g in range(N_GROUPS):
                wk.append(pad_cols(wi[:, base + HEAD_DIM * g:base + HEAD_DIM * (g + 1)], LANES))
                v0 = base + N_GROUPS * HEAD_DIM + HEAD_DIM * g
                wv.append(pad_cols(wi[:, v0:v0 + HEAD_DIM], VT_ROWS))
        wrow = jnp.concatenate([wi[:, c0:c1]] + wk + [wi[:, c2:c3]], axis=1).astype(BF16)
        wnt = jnp.concatenate([wi[:, :c0]] + wv + [pad_cols(wi[:, c3:], GATET_ROWS)], axis=1).T.astype(BF16)
        (qt, kvc2, kaug, vt, u2, vn2, gatest), (wm_b, wpa_b, wpb_b, wo_b, wmq_b, wmkv_b, wmo_b) = _inproj(
            h, norm_mix[l][None], ln_sgu[l][None], wrow, wnt,
            [w_merge[l], w_proj_a[l], w_proj_b[l], w_out[l], w_mq[l], w_mkv[l], w_mo[l]], 2 * tm, bsz, t)

        pe2 = jnp.stack([pe_cmp_k[l], pe_cmp_v[l]])
        pe2 = jnp.concatenate([pe2, pe2], axis=-1)
        w1 = jnp.stack([w_cmp_k1[l], w_cmp_v1[l]]).reshape(2, CMP_BLOCK, HEAD_DIM, CMP_HIDDEN)
        w1bd = _block_diag2(w1).astype(BF16)
        w2k = _block_diag2(w_cmp_k2[l]).astype(BF16)
        w2vt = _block_diag2(w_cmp_v2[l]).T.astype(BF16)
        kc, vct = _compress(kvc2.reshape(bsz, t, KV_W), pe2, w1bd, w2k, w2vt)
        ocmp, selt, flags = _cmp_topk(qt, kc, vct, gatest)
        counts, lists = _chunk_lists(flags)
        onsa = _slc_win(counts, lists, qt, kaug, vt, onehot, selt, gatest, ocmp)

        h, (wgu_b, wd_b) = _merge(h, onsa.reshape(n, NSA_W), u2, vn2, norm_mix[l][None], wm_b,
                                  b_merge[l][None], w_spatial[l], b_spatial[l].T, wpa_b, wpb_b, wo_b,
                                  [w_gate_up[l], w_down[l]], 2 * tm)

        mkv = _memkv(mem, norm_mem_kv[l][None], wmkv_b)
        h = _xattn(h, norm_mem_q[l][None], wmq_b, mkv, wmo_b, 2 * tm, t)
        h = _ffn(h, norm_ffn[l][None], wgu_b, wd_b, norm_final[None], tm)
    return h.reshape(bsz, t, d)
```
